```python
import jax, jax.numpy as jnp
from jax import lax
import numpy as np

D_MODEL = 1024
BATCH = 32
SEQ = 2048
DEPTH = 1

CHUNK = 64
LRU_WIDTH = 1280
LRU_HEADS = 10
LRU_HEAD_DIM = LRU_WIDTH // LRU_HEADS
CONV_WIDTH = 4
LRU_C = 8.0
SGU_WIDTH = 768
SGU_GROUPS = 6
SGU_GROUP_DIM = SGU_WIDTH // SGU_GROUPS
SGU_BLOCK = 128
D_FF = 4 * D_MODEL
N_BRANCH = 2
D_IN = 2 * LRU_WIDTH + 2 * SGU_WIDTH + N_BRANCH * D_MODEL
IN_SPLITS = (LRU_WIDTH, 2 * LRU_WIDTH, 2 * LRU_WIDTH + SGU_WIDTH,
             2 * LRU_WIDTH + 2 * SGU_WIDTH, 2 * LRU_WIDTH + 2 * SGU_WIDTH + D_MODEL)
ALPHA = (2.0 * DEPTH) ** 0.25
BETA = (8.0 * DEPTH) ** -0.25
LN_EPS = 1e-5

kernel_name = "hawk_gmlp_hybrid_deepnorm_adaln"


def _layer_norm(x, g, b):
    xf = x.astype(jnp.float32)
    mu = jnp.mean(xf, axis=-1, keepdims=True)
    var = jnp.mean(jnp.square(xf - mu), axis=-1, keepdims=True)
    y = (xf - mu) * lax.rsqrt(var + LN_EPS)
    return (y * g.astype(jnp.float32) + b.astype(jnp.float32)).astype(x.dtype)


def _causal_depthwise_conv(x, w, b):
    y = lax.conv_general_dilated(
        x, w[:, None, :].astype(x.dtype), window_strides=(1,),
        padding=[(CONV_WIDTH - 1, 0)], dimension_numbers=("NWC", "WIO", "NWC"),
        feature_group_count=x.shape[-1])
    return y + b


def _rg_lru(x, w_a, b_a, w_x, b_x, lam):
    B, S, _ = x.shape
    xh = x.reshape(B, S, LRU_HEADS, LRU_HEAD_DIM)
    r = jax.nn.sigmoid(jnp.einsum("bshi,hij->bshj", xh, w_a).reshape(B, S, LRU_WIDTH) + b_a)
    i = jax.nn.sigmoid(jnp.einsum("bshi,hij->bshj", xh, w_x).reshape(B, S, LRU_WIDTH) + b_x)
    log_a = (-LRU_C * jax.nn.softplus(-lam.astype(jnp.float32))) * r.astype(jnp.float32)
    a = jnp.exp(log_a)
    inp = jnp.sqrt(-jnp.expm1(2.0 * log_a)) * (i * x).astype(jnp.float32)

    def step(h, ab):
        a_t, b_t = ab
        h = a_t * h + b_t
        return h, h

    h0 = jnp.zeros((B, LRU_WIDTH), jnp.float32)
    _, hs = lax.scan(step, h0, (jnp.swapaxes(a, 0, 1), jnp.swapaxes(inp, 0, 1)))
    return jnp.swapaxes(hs, 0, 1).astype(x.dtype)


def _spatial_gating(u, v, w_sp, b_sp, ln_g, ln_b):
    B, S, _ = u.shape
    v = _layer_norm(v, ln_g, ln_b)
    nblk = S // SGU_BLOCK
    vb = v.reshape(B, nblk, SGU_BLOCK, SGU_GROUPS, SGU_GROUP_DIM)
    pos = jnp.arange(SGU_BLOCK)
    mask = (pos[None, :] // CHUNK) <= (pos[:, None] // CHUNK)
    w = jnp.where(mask[None], w_sp, 0.0).astype(v.dtype)
    mixed = jnp.einsum("gts,bnsgc->bntgc", w, vb) + jnp.transpose(b_sp)[None, None, :, :, None]
    return u * mixed.reshape(B, S, SGU_WIDTH)


def _fwd_setup_inputs(seed: int = 0) -> dict:
    key = jax.random.key(seed)
    ks = jax.random.split(key, 28)
    L = DEPTH
    nrm = lambda k, shape, s: jax.random.normal(k, shape, jnp.float32) * s
    u = jax.random.uniform(ks[12], (L, LRU_WIDTH), jnp.float32, 0.9, 0.999)
    a0 = u ** (1.0 / LRU_C)
    lru_lambda = jnp.log(a0) - jnp.log1p(-a0)
    return {
        "x": nrm(ks[0], (BATCH, SEQ, D_MODEL), 1.0),
        "c": nrm(ks[1], (BATCH, D_MODEL), 1.0),
        "w_ada": nrm(ks[2], (L, D_MODEL, 6 * D_MODEL), 0.1 * D_MODEL ** -0.5),
        "b_ada": nrm(ks[3], (L, 6 * D_MODEL), 0.01),
        "w_in": nrm(ks[4], (L, D_MODEL, D_IN), D_MODEL ** -0.5),
        "b_in": nrm(ks[5], (L, D_IN), 0.01),
        "w_conv": nrm(ks[6], (L, CONV_WIDTH, LRU_WIDTH), CONV_WIDTH ** -0.5),
        "b_conv": nrm(ks[7], (L, LRU_WIDTH), 0.01),
        "w_rg_a": nrm(ks[8], (L, LRU_HEADS, LRU_HEAD_DIM, LRU_HEAD_DIM), LRU_HEAD_DIM ** -0.5),
        "b_rg_a": nrm(ks[9], (L, LRU_WIDTH), 0.01),
        "w_rg_x": nrm(ks[10], (L, LRU_HEADS, LRU_HEAD_DIM, LRU_HEAD_DIM), LRU_HEAD_DIM ** -0.5),
        "b_rg_x": nrm(ks[11], (L, LRU_WIDTH), 0.01),
        "lru_lambda": lru_lambda,
        "w_sp": nrm(ks[13], (L, SGU_GROUPS, SGU_BLOCK, SGU_BLOCK), SGU_BLOCK ** -0.5),
        "b_sp": 1.0 + nrm(ks[14], (L, SGU_GROUPS, SGU_BLOCK), 0.01),
        "ln_v_g": 1.0 + nrm(ks[15], (L, SGU_WIDTH), 0.01),
        "ln_v_b": nrm(ks[16], (L, SGU_WIDTH), 0.01),
        "w_o_lru": nrm(ks[17], (L, LRU_WIDTH, D_MODEL), BETA * LRU_WIDTH ** -0.5),
        "w_o_sgu": nrm(ks[18], (L, SGU_WIDTH, D_MODEL), BETA * SGU_WIDTH ** -0.5),
        "w_out": nrm(ks[19], (L, D_MODEL, D_MODEL), BETA * D_MODEL ** -0.5),
        "ln1_g": 1.0 + nrm(ks[20], (L, D_MODEL), 0.01),
        "ln1_b": nrm(ks[21], (L, D_MODEL), 0.01),
        "w_up": nrm(ks[22], (L, D_MODEL, D_FF), BETA * D_MODEL ** -0.5),
        "w_down": nrm(ks[23], (L, D_FF, D_MODEL), BETA * D_FF ** -0.5),
        "ln2_g": 1.0 + nrm(ks[24], (L, D_MODEL), 0.01),
        "ln2_b": nrm(ks[25], (L, D_MODEL), 0.01),
    }


def _fwd_reference(x, c, w_ada, b_ada, w_in, b_in, w_conv, b_conv, w_rg_a, b_rg_a, w_rg_x, b_rg_x,
              lru_lambda, w_sp, b_sp, ln_v_g, ln_v_b, w_o_lru, w_o_sgu, w_out, ln1_g, ln1_b,
              w_up, w_down, ln2_g, ln2_b):
    c_act = jax.nn.silu(c)
    for l in range(DEPTH):
        mod = c_act @ w_ada[l] + b_ada[l]
        sh1, sc1, gt1, sh2, sc2, gt2 = jnp.split(mod, 6, axis=-1)

        h = x * (1.0 + sc1[:, None, :]) + sh1[:, None, :]
        proj = h @ w_in[l] + b_in[l]
        x_lru, g_lru, u, v, gate_a, gate_b = jnp.split(proj, IN_SPLITS, axis=-1)

        xc = _causal_depthwise_conv(x_lru, w_conv[l], b_conv[l])
        y_lru = _rg_lru(xc, w_rg_a[l], b_rg_a[l], w_rg_x[l], b_rg_x[l], lru_lambda[l])
        y_a = (y_lru * jax.nn.gelu(g_lru)) @ w_o_lru[l]

        y_sgu = _spatial_gating(jax.nn.gelu(u), jax.nn.gelu(v), w_sp[l], b_sp[l], ln_v_g[l], ln_v_b[l])
        y_b = y_sgu @ w_o_sgu[l]

        merged = jax.nn.sigmoid(gate_a) * y_a + jax.nn.sigmoid(gate_b) * y_b
        mix = merged @ w_out[l]
        x = _layer_norm(ALPHA * x + (1.0 + gt1[:, None, :]) * mix, ln1_g[l], ln1_b[l])

        h2 = x * (1.0 + sc2[:, None, :]) + sh2[:, None, :]
        f = jnp.square(jax.nn.relu(h2 @ w_up[l])) @ w_down[l]
        x = _layer_norm(ALPHA * x + (1.0 + gt2[:, None, :]) * f, ln2_g[l], ln2_b[l])
    return x


import jax as _jax
import jax.numpy as _jnp

TWIN_FORMAT = 'train_step'
FWD_PARAMS = ['x', 'c', 'w_ada', 'b_ada', 'w_in', 'b_in', 'w_conv', 'b_conv', 'w_rg_a', 'b_rg_a', 'w_rg_x', 'b_rg_x', 'lru_lambda', 'w_sp', 'b_sp', 'ln_v_g', 'ln_v_b', 'w_o_lru', 'w_o_sgu', 'w_out', 'ln1_g', 'ln1_b', 'w_up', 'w_down', 'ln2_g', 'ln2_b']
TWIN_WEIGHTS = ['w_ada', 'b_ada', 'w_in', 'b_in', 'w_conv', 'b_conv', 'w_rg_a', 'b_rg_a', 'w_rg_x', 'b_rg_x', 'lru_lambda', 'w_sp', 'b_sp', 'ln_v_g', 'ln_v_b', 'w_o_lru', 'w_o_sgu', 'w_out', 'ln1_g', 'ln1_b', 'w_up', 'w_down', 'ln2_g', 'ln2_b']
TWIN_DIFF_INPUT = 'x'
TWIN_INPUTS = ['x', 'c', 'w_ada', 'b_ada', 'w_in', 'b_in', 'w_conv', 'b_conv', 'w_rg_a', 'b_rg_a', 'w_rg_x', 'b_rg_x', 'lru_lambda', 'w_sp', 'b_sp', 'ln_v_g', 'ln_v_b', 'w_o_lru', 'w_o_sgu', 'w_out', 'ln1_g', 'ln1_b', 'w_up', 'w_down', 'ln2_g', 'ln2_b', 'loss_target', 'm_w_ada', 'm_b_ada', 'm_w_in', 'm_b_in', 'm_w_conv', 'm_b_conv', 'm_w_rg_a', 'm_b_rg_a', 'm_w_rg_x', 'm_b_rg_x', 'm_lru_lambda', 'm_w_sp', 'm_b_sp', 'm_ln_v_g', 'm_ln_v_b', 'm_w_o_lru', 'm_w_o_sgu', 'm_w_out', 'm_ln1_g', 'm_ln1_b', 'm_w_up', 'm_w_down', 'm_ln2_g', 'm_ln2_b', 'v_w_ada', 'v_b_ada', 'v_w_in', 'v_b_in', 'v_w_conv', 'v_b_conv', 'v_w_rg_a', 'v_b_rg_a', 'v_w_rg_x', 'v_b_rg_x', 'v_lru_lambda', 'v_w_sp', 'v_b_sp', 'v_ln_v_g', 'v_ln_v_b', 'v_w_o_lru', 'v_w_o_sgu', 'v_w_out', 'v_ln1_g', 'v_ln1_b', 'v_w_up', 'v_w_down', 'v_ln2_g', 'v_ln2_b']
TWIN_OUTPUTS = ['loss', 'grad_x', 'grad_w_ada', 'grad_b_ada', 'grad_w_in', 'grad_b_in', 'grad_w_conv', 'grad_b_conv', 'grad_w_rg_a', 'grad_b_rg_a', 'grad_w_rg_x', 'grad_b_rg_x', 'grad_lru_lambda', 'grad_w_sp', 'grad_b_sp', 'grad_ln_v_g', 'grad_ln_v_b', 'grad_w_o_lru', 'grad_w_o_sgu', 'grad_w_out', 'grad_ln1_g', 'grad_ln1_b', 'grad_w_up', 'grad_w_down', 'grad_ln2_g', 'grad_ln2_b', 'delta_w_ada', 'delta_b_ada', 'delta_w_in', 'delta_b_in', 'delta_w_conv', 'delta_b_conv', 'delta_w_rg_a', 'delta_b_rg_a', 'delta_w_rg_x', 'delta_b_rg_x', 'delta_lru_lambda', 'delta_w_sp', 'delta_b_sp', 'delta_ln_v_g', 'delta_ln_v_b', 'delta_w_o_lru', 'delta_w_o_sgu', 'delta_w_out', 'delta_ln1_g', 'delta_ln1_b', 'delta_w_up', 'delta_w_down', 'delta_ln2_g', 'delta_ln2_b', 'new_m_w_ada', 'new_m_b_ada', 'new_m_w_in', 'new_m_b_in', 'new_m_w_conv', 'new_m_b_conv', 'new_m_w_rg_a', 'new_m_b_rg_a', 'new_m_w_rg_x', 'new_m_b_rg_x', 'new_m_lru_lambda', 'new_m_w_sp', 'new_m_b_sp', 'new_m_ln_v_g', 'new_m_ln_v_b', 'new_m_w_o_lru', 'new_m_w_o_sgu', 'new_m_w_out', 'new_m_ln1_g', 'new_m_ln1_b', 'new_m_w_up', 'new_m_w_down', 'new_m_ln2_g', 'new_m_ln2_b', 'new_v_w_ada', 'new_v_b_ada', 'new_v_w_in', 'new_v_b_in', 'new_v_w_conv', 'new_v_b_conv', 'new_v_w_rg_a', 'new_v_b_rg_a', 'new_v_w_rg_x', 'new_v_b_rg_x', 'new_v_lru_lambda', 'new_v_w_sp', 'new_v_b_sp', 'new_v_ln_v_g', 'new_v_ln_v_b', 'new_v_w_o_lru', 'new_v_w_o_sgu', 'new_v_w_out', 'new_v_ln1_g', 'new_v_ln1_b', 'new_v_w_up', 'new_v_w_down', 'new_v_ln2_g', 'new_v_ln2_b']
TWIN_LEAF_KINDS = {'loss': 'loss', 'grad_x': 'grad_x', 'grad_w_ada': 'grad_w', 'grad_b_ada': 'grad_w', 'grad_w_in': 'grad_w', 'grad_b_in': 'grad_w', 'grad_w_conv': 'grad_w', 'grad_b_conv': 'grad_w', 'grad_w_rg_a': 'grad_w', 'grad_b_rg_a': 'grad_w', 'grad_w_rg_x': 'grad_w', 'grad_b_rg_x': 'grad_w', 'grad_lru_lambda': 'grad_w', 'grad_w_sp': 'grad_w', 'grad_b_sp': 'grad_w', 'grad_ln_v_g': 'grad_w', 'grad_ln_v_b': 'grad_w', 'grad_w_o_lru': 'grad_w', 'grad_w_o_sgu': 'grad_w', 'grad_w_out': 'grad_w', 'grad_ln1_g': 'grad_w', 'grad_ln1_b': 'grad_w', 'grad_w_up': 'grad_w', 'grad_w_down': 'grad_w', 'grad_ln2_g': 'grad_w', 'grad_ln2_b': 'grad_w', 'delta_w_ada': 'delta_w', 'delta_b_ada': 'delta_w', 'delta_w_in': 'delta_w', 'delta_b_in': 'delta_w', 'delta_w_conv': 'delta_w', 'delta_b_conv': 'delta_w', 'delta_w_rg_a': 'delta_w', 'delta_b_rg_a': 'delta_w', 'delta_w_rg_x': 'delta_w', 'delta_b_rg_x': 'delta_w', 'delta_lru_lambda': 'delta_w', 'delta_w_sp': 'delta_w', 'delta_b_sp': 'delta_w', 'delta_ln_v_g': 'delta_w', 'delta_ln_v_b': 'delta_w', 'delta_w_o_lru': 'delta_w', 'delta_w_o_sgu': 'delta_w', 'delta_w_out': 'delta_w', 'delta_ln1_g': 'delta_w', 'delta_ln1_b': 'delta_w', 'delta_w_up': 'delta_w', 'delta_w_down': 'delta_w', 'delta_ln2_g': 'delta_w', 'delta_ln2_b': 'delta_w', 'new_m_w_ada': 'new_m', 'new_m_b_ada': 'new_m', 'new_m_w_in': 'new_m', 'new_m_b_in': 'new_m', 'new_m_w_conv': 'new_m', 'new_m_b_conv': 'new_m', 'new_m_w_rg_a': 'new_m', 'new_m_b_rg_a': 'new_m', 'new_m_w_rg_x': 'new_m', 'new_m_b_rg_x': 'new_m', 'new_m_lru_lambda': 'new_m', 'new_m_w_sp': 'new_m', 'new_m_b_sp': 'new_m', 'new_m_ln_v_g': 'new_m', 'new_m_ln_v_b': 'new_m', 'new_m_w_o_lru': 'new_m', 'new_m_w_o_sgu': 'new_m', 'new_m_w_out': 'new_m', 'new_m_ln1_g': 'new_m', 'new_m_ln1_b': 'new_m', 'new_m_w_up': 'new_m', 'new_m_w_down': 'new_m', 'new_m_ln2_g': 'new_m', 'new_m_ln2_b': 'new_m', 'new_v_w_ada': 'new_v', 'new_v_b_ada': 'new_v', 'new_v_w_in': 'new_v', 'new_v_b_in': 'new_v', 'new_v_w_conv': 'new_v', 'new_v_b_conv': 'new_v', 'new_v_w_rg_a': 'new_v', 'new_v_b_rg_a': 'new_v', 'new_v_w_rg_x': 'new_v', 'new_v_b_rg_x': 'new_v', 'new_v_lru_lambda': 'new_v', 'new_v_w_sp': 'new_v', 'new_v_b_sp': 'new_v', 'new_v_ln_v_g': 'new_v', 'new_v_ln_v_b': 'new_v', 'new_v_w_o_lru': 'new_v', 'new_v_w_o_sgu': 'new_v', 'new_v_w_out': 'new_v', 'new_v_ln1_g': 'new_v', 'new_v_ln1_b': 'new_v', 'new_v_w_up': 'new_v', 'new_v_w_down': 'new_v', 'new_v_ln2_g': 'new_v', 'new_v_ln2_b': 'new_v'}


def _forward(args):
    return _fwd_reference(*[args[k] for k in FWD_PARAMS])


def _output_shape():
    out = _jax.eval_shape(lambda: _forward(_fwd_setup_inputs(0)))
    return out.shape, out.dtype

N_MICROBATCH = 1
ADAM_LR = 0.001
ADAM_B1 = 0.9
ADAM_B2 = 0.999
ADAM_EPS = 1e-08
ADAM_WD = 0.01
ADAM_STEP = 10
PER_EXAMPLE_BATCH_AXIS = {'x': 0, 'c': 0, 'loss_target': 0}
SHARED_INPUTS = []
_WEIGHT_DTYPES = {'w_ada': _jnp.float32, 'b_ada': _jnp.float32, 'w_in': _jnp.float32, 'b_in': _jnp.float32, 'w_conv': _jnp.float32, 'b_conv': _jnp.float32, 'w_rg_a': _jnp.float32, 'b_rg_a': _jnp.float32, 'w_rg_x': _jnp.float32, 'b_rg_x': _jnp.float32, 'lru_lambda': _jnp.float32, 'w_sp': _jnp.float32, 'b_sp': _jnp.float32, 'ln_v_g': _jnp.float32, 'ln_v_b': _jnp.float32, 'w_o_lru': _jnp.float32, 'w_o_sgu': _jnp.float32, 'w_out': _jnp.float32, 'ln1_g': _jnp.float32, 'ln1_b': _jnp.float32, 'w_up': _jnp.float32, 'w_down': _jnp.float32, 'ln2_g': _jnp.float32, 'ln2_b': _jnp.float32}
MOMENT_SCALE = {'w_ada': 6.768457e-02, 'b_ada': 1.090066e-01, 'w_in': 2.325321e-02, 'b_in': 1.043458e-01, 'w_conv': 2.195885e-02, 'b_conv': 2.589453e-01, 'w_rg_a': 6.858607e-03, 'b_rg_a': 6.373493e-03, 'w_rg_x': 1.247842e-02, 'b_rg_x': 7.575734e-03, 'lru_lambda': 1.279577e-02, 'w_sp': 2.645303e-02, 'b_sp': 2.993023e-02, 'ln_v_g': 2.660803e-02, 'ln_v_b': 2.646429e-02, 'w_o_lru': 4.137531e-02, 'w_o_sgu': 6.205596e-02, 'w_out': 7.255458e-02, 'ln1_g': 7.145836e-01, 'ln1_b': 3.753901e-01, 'w_up': 5.189274e-02, 'w_down': 9.763153e-02, 'ln2_g': 6.392498e+01, 'ln2_b': 5.833491e+00}


def _to_microbatches(a, axis):
    t = _jnp.moveaxis(a, axis, 0)
    t = t.reshape((N_MICROBATCH, t.shape[0] // N_MICROBATCH) + t.shape[1:])
    return _jnp.moveaxis(t, 1, axis + 1)


def setup_inputs(seed: int = 0) -> dict:
    inp = _fwd_setup_inputs(seed)
    key = _jax.random.fold_in(_jax.random.key(seed), 7919)
    shape, _ = _output_shape()
    out = dict(inp)
    out["loss_target"] = _jax.random.normal(_jax.random.fold_in(key, 0), shape, _jnp.float32)
    for i, name in enumerate(TWIN_WEIGHTS):
        w = inp[name].astype(_jnp.float32)
        if MOMENT_SCALE is None:
            s = _jnp.sqrt(_jnp.mean(_jnp.square(w)) + 1e-30)
        else:
            s = MOMENT_SCALE[name]
        km, kv = _jax.random.split(_jax.random.fold_in(key, i + 1))
        out[name] = w
        out["m_" + name] = s * _jax.random.normal(km, w.shape, _jnp.float32)
        out["v_" + name] = (s * s) * _jax.random.uniform(kv, w.shape, _jnp.float32, 0.5, 1.5)
    if N_MICROBATCH > 1:
        for name, axis in PER_EXAMPLE_BATCH_AXIS.items():
            out[name] = _to_microbatches(out[name], axis)
    return {'x': out['x'], 'c': out['c'], 'w_ada': out['w_ada'], 'b_ada': out['b_ada'], 'w_in': out['w_in'], 'b_in': out['b_in'], 'w_conv': out['w_conv'], 'b_conv': out['b_conv'], 'w_rg_a': out['w_rg_a'], 'b_rg_a': out['b_rg_a'], 'w_rg_x': out['w_rg_x'], 'b_rg_x': out['b_rg_x'], 'lru_lambda': out['lru_lambda'], 'w_sp': out['w_sp'], 'b_sp': out['b_sp'], 'ln_v_g': out['ln_v_g'], 'ln_v_b': out['ln_v_b'], 'w_o_lru': out['w_o_lru'], 'w_o_sgu': out['w_o_sgu'], 'w_out': out['w_out'], 'ln1_g': out['ln1_g'], 'ln1_b': out['ln1_b'], 'w_up': out['w_up'], 'w_down': out['w_down'], 'ln2_g': out['ln2_g'], 'ln2_b': out['ln2_b'], 'loss_target': out['loss_target'], 'm_w_ada': out['m_w_ada'], 'm_b_ada': out['m_b_ada'], 'm_w_in': out['m_w_in'], 'm_b_in': out['m_b_in'], 'm_w_conv': out['m_w_conv'], 'm_b_conv': out['m_b_conv'], 'm_w_rg_a': out['m_w_rg_a'], 'm_b_rg_a': out['m_b_rg_a'], 'm_w_rg_x': out['m_w_rg_x'], 'm_b_rg_x': out['m_b_rg_x'], 'm_lru_lambda': out['m_lru_lambda'], 'm_w_sp': out['m_w_sp'], 'm_b_sp': out['m_b_sp'], 'm_ln_v_g': out['m_ln_v_g'], 'm_ln_v_b': out['m_ln_v_b'], 'm_w_o_lru': out['m_w_o_lru'], 'm_w_o_sgu': out['m_w_o_sgu'], 'm_w_out': out['m_w_out'], 'm_ln1_g': out['m_ln1_g'], 'm_ln1_b': out['m_ln1_b'], 'm_w_up': out['m_w_up'], 'm_w_down': out['m_w_down'], 'm_ln2_g': out['m_ln2_g'], 'm_ln2_b': out['m_ln2_b'], 'v_w_ada': out['v_w_ada'], 'v_b_ada': out['v_b_ada'], 'v_w_in': out['v_w_in'], 'v_b_in': out['v_b_in'], 'v_w_conv': out['v_w_conv'], 'v_b_conv': out['v_b_conv'], 'v_w_rg_a': out['v_w_rg_a'], 'v_b_rg_a': out['v_b_rg_a'], 'v_w_rg_x': out['v_w_rg_x'], 'v_b_rg_x': out['v_b_rg_x'], 'v_lru_lambda': out['v_lru_lambda'], 'v_w_sp': out['v_w_sp'], 'v_b_sp': out['v_b_sp'], 'v_ln_v_g': out['v_ln_v_g'], 'v_ln_v_b': out['v_ln_v_b'], 'v_w_o_lru': out['v_w_o_lru'], 'v_w_o_sgu': out['v_w_o_sgu'], 'v_w_out': out['v_w_out'], 'v_ln1_g': out['v_ln1_g'], 'v_ln1_b': out['v_ln1_b'], 'v_w_up': out['v_w_up'], 'v_w_down': out['v_w_down'], 'v_ln2_g': out['v_ln2_g'], 'v_ln2_b': out['v_ln2_b']}


def _loss(weights, diff, rest, loss_target):
    with _jax.named_scope("forward"):
        args = {**rest, TWIN_DIFF_INPUT: diff, **{k: w.astype(_WEIGHT_DTYPES[k]) for k, w in weights.items()}}
        y = _forward(args)
    with _jax.named_scope("loss_head"):
        err = _jnp.square(y.astype(_jnp.float32) - loss_target)
        return 0.5 * _jnp.sum(_jnp.mean(err, axis=-1)) if err.ndim else 0.5 * err


def _adamw(w, g, m, v):
    m = ADAM_B1 * m + (1.0 - ADAM_B1) * g
    v = ADAM_B2 * v + (1.0 - ADAM_B2) * _jnp.square(g)
    m_hat = m / (1.0 - ADAM_B1 ** ADAM_STEP)
    v_hat = v / (1.0 - ADAM_B2 ** ADAM_STEP)
    delta = -ADAM_LR * (m_hat / (_jnp.sqrt(v_hat) + ADAM_EPS) + ADAM_WD * w)
    return delta, m, v


def reference(x, c, w_ada, b_ada, w_in, b_in, w_conv, b_conv, w_rg_a, b_rg_a, w_rg_x, b_rg_x, lru_lambda, w_sp, b_sp, ln_v_g, ln_v_b, w_o_lru, w_o_sgu, w_out, ln1_g, ln1_b, w_up, w_down, ln2_g, ln2_b, loss_target, m_w_ada, m_b_ada, m_w_in, m_b_in, m_w_conv, m_b_conv, m_w_rg_a, m_b_rg_a, m_w_rg_x, m_b_rg_x, m_lru_lambda, m_w_sp, m_b_sp, m_ln_v_g, m_ln_v_b, m_w_o_lru, m_w_o_sgu, m_w_out, m_ln1_g, m_ln1_b, m_w_up, m_w_down, m_ln2_g, m_ln2_b, v_w_ada, v_b_ada, v_w_in, v_b_in, v_w_conv, v_b_conv, v_w_rg_a, v_b_rg_a, v_w_rg_x, v_b_rg_x, v_lru_lambda, v_w_sp, v_b_sp, v_ln_v_g, v_ln_v_b, v_w_o_lru, v_w_o_sgu, v_w_out, v_ln1_g, v_ln1_b, v_w_up, v_w_down, v_ln2_g, v_ln2_b):
    given = dict(x=x, c=c, w_ada=w_ada, b_ada=b_ada, w_in=w_in, b_in=b_in, w_conv=w_conv, b_conv=b_conv, w_rg_a=w_rg_a, b_rg_a=b_rg_a, w_rg_x=w_rg_x, b_rg_x=b_rg_x, lru_lambda=lru_lambda, w_sp=w_sp, b_sp=b_sp, ln_v_g=ln_v_g, ln_v_b=ln_v_b, w_o_lru=w_o_lru, w_o_sgu=w_o_sgu, w_out=w_out, ln1_g=ln1_g, ln1_b=ln1_b, w_up=w_up, w_down=w_down, ln2_g=ln2_g, ln2_b=ln2_b, loss_target=loss_target, m_w_ada=m_w_ada, m_b_ada=m_b_ada, m_w_in=m_w_in, m_b_in=m_b_in, m_w_conv=m_w_conv, m_b_conv=m_b_conv, m_w_rg_a=m_w_rg_a, m_b_rg_a=m_b_rg_a, m_w_rg_x=m_w_rg_x, m_b_rg_x=m_b_rg_x, m_lru_lambda=m_lru_lambda, m_w_sp=m_w_sp, m_b_sp=m_b_sp, m_ln_v_g=m_ln_v_g, m_ln_v_b=m_ln_v_b, m_w_o_lru=m_w_o_lru, m_w_o_sgu=m_w_o_sgu, m_w_out=m_w_out, m_ln1_g=m_ln1_g, m_ln1_b=m_ln1_b, m_w_up=m_w_up, m_w_down=m_w_down, m_ln2_g=m_ln2_g, m_ln2_b=m_ln2_b, v_w_ada=v_w_ada, v_b_ada=v_b_ada, v_w_in=v_w_in, v_b_in=v_b_in, v_w_conv=v_w_conv, v_b_conv=v_b_conv, v_w_rg_a=v_w_rg_a, v_b_rg_a=v_b_rg_a, v_w_rg_x=v_w_rg_x, v_b_rg_x=v_b_rg_x, v_lru_lambda=v_lru_lambda, v_w_sp=v_w_sp, v_b_sp=v_b_sp, v_ln_v_g=v_ln_v_g, v_ln_v_b=v_ln_v_b, v_w_o_lru=v_w_o_lru, v_w_o_sgu=v_w_o_sgu, v_w_out=v_w_out, v_ln1_g=v_ln1_g, v_ln1_b=v_ln1_b, v_w_up=v_w_up, v_w_down=v_w_down, v_ln2_g=v_ln2_g, v_ln2_b=v_ln2_b)
    weights = {n: given[n] for n in TWIN_WEIGHTS}
    shared = {n: given[n] for n in SHARED_INPUTS}
    per_example = {n: given[n] for n in ['x', 'c']}
    grad_fn = _jax.value_and_grad(_loss, argnums=(0, 1))

    def one_microbatch(ex, loss_target):
        ex = dict(ex)
        diff = ex.pop(TWIN_DIFF_INPUT)
        return grad_fn(weights, diff, {**shared, **ex}, loss_target)

    if N_MICROBATCH == 1:
        loss, (grad_w, grad_x) = one_microbatch(per_example, given["loss_target"])
    else:
        def body(carry, xs):
            loss_sum, grad_sum = carry
            l_k, (gw_k, gx_k) = one_microbatch(xs[0], xs[1])
            with _jax.named_scope("update"):
                return (loss_sum + l_k, _jax.tree.map(_jnp.add, grad_sum, gw_k)), gx_k

        init = (_jnp.zeros((), _jnp.float32), _jax.tree.map(_jnp.zeros_like, weights))
        (loss, grad_w), grad_x = _jax.lax.scan(body, init, (per_example, given["loss_target"]))
    with _jax.named_scope("update"):
        delta_w, new_m, new_v = {}, {}, {}
        for n in TWIN_WEIGHTS:
            delta_w[n], new_m[n], new_v[n] = _adamw(weights[n], grad_w[n], given["m_" + n], given["v_" + n])
    return (loss, grad_x, *[grad_w[n] for n in TWIN_WEIGHTS], *[delta_w[n] for n in TWIN_WEIGHTS],
            *[new_m[n] for n in TWIN_WEIGHTS], *[new_v[n] for n in TWIN_WEIGHTS])
```

```python
import functools
import math

import jax
import jax.numpy as jnp
from jax import lax
from jax.experimental import pallas as pl
from jax.experimental.pallas import tpu as pltpu

N_DEV = 8
LN_EPS = 1e-5
LRU_C = 8.0
CHUNK = 64
SGU_BLOCK = 128
ALPHA = 2.0 ** 0.25
ADAM_LR = 0.001
ADAM_B1 = 0.9
ADAM_B2 = 0.999
ADAM_EPS = 1e-08
ADAM_WD = 0.01
ADAM_STEP = 10
GELU_K0 = math.sqrt(2.0 / math.pi)
GELU_K1 = 0.044715

SUBLANES = 8
LANES = 128
VMEM_LIMIT = 56 * 1024 * 1024

F32 = jnp.float32
BF16 = jnp.bfloat16
MESH = pl.DeviceIdType.MESH


def _cparams(n_axes, big=False):
    return pltpu.CompilerParams(dimension_semantics=("arbitrary",) * n_axes,
                                vmem_limit_bytes=VMEM_LIMIT if big else None)


def _sigmoid(x):
    return 1.0 / (1.0 + jnp.exp(-x))


def _gelu(x):
    t = jnp.tanh(GELU_K0 * (x + GELU_K1 * (x * x * x)))
    return 0.5 * x * (1.0 + t)


def _gelu_and_grad(x):
    x2 = x * x
    t = jnp.tanh(GELU_K0 * (x + GELU_K1 * (x2 * x)))
    g = 0.5 * x * (1.0 + t)
    dg = 0.5 * (1.0 + t) + 0.5 * x * (1.0 - t * t) * (GELU_K0 * (1.0 + 3.0 * GELU_K1 * x2))
    return g, dg


def _expm1(x):
    p = x * (1.0 + x * (1.0 / 2.0) * (1.0 + x * (1.0 / 3.0) * (1.0 + x * (1.0 / 4.0) * (
        1.0 + x * (1.0 / 5.0) * (1.0 + x * (1.0 / 6.0) * (1.0 + x * (1.0 / 7.0)))))))
    return jnp.where(jnp.abs(x) < 0.3, p, jnp.exp(x) - 1.0)


def _log1p_pos(e):
    p = e * (1.0 - e * (1.0 / 2.0) + e * e * (1.0 / 3.0) - e * e * e * (1.0 / 4.0))
    return jnp.where(e < 1e-2, p, jnp.log(1.0 + e))


def _ln_stats(z):
    mu = jnp.mean(z, axis=-1, keepdims=True)
    zc = z - mu
    var = jnp.mean(zc * zc, axis=-1, keepdims=True)
    rstd = lax.rsqrt(var + LN_EPS)
    return zc * rstd, rstd


def _ln_bwd(dy, xhat, rstd, g):
    dxh = dy * g
    m1 = jnp.mean(dxh, axis=-1, keepdims=True)
    m2 = jnp.mean(dxh * xhat, axis=-1, keepdims=True)
    return rstd * (dxh - m1 - xhat * m2)


def _colsum(v):
    return jnp.sum(v, axis=0, keepdims=True)


def _first_step():
    return jnp.logical_and(pl.program_id(0) == 0, pl.program_id(1) == 0)


def _exchange(arrs, gather, name):
    n = len(arrs)
    n_peer = N_DEV - 1

    def body(*refs):
        ins, outs = refs[:n], refs[n:2 * n]
        send_sems, recv_sems, loc_sems = refs[2 * n:]
        x, y, c = lax.axis_index("x"), lax.axis_index("y"), lax.axis_index("c")
        me = 4 * x + 2 * y + c
        started = []
        for a in range(n):
            src_me = ins[a] if gather else ins[a].at[me]
            lc = pltpu.make_async_copy(src_me, outs[a].at[me], loc_sems.at[a])
            lc.start()
            started.append((lc, None))
        for p in range(1, N_DEV):
            px, py, pc = x ^ ((p >> 2) & 1), y ^ ((p >> 1) & 1), c ^ (p & 1)
            peer = 4 * px + 2 * py + pc
            for a in range(n):
                k = a * n_peer + (p - 1)
                src = ins[a] if gather else ins[a].at[peer]
                cp = pltpu.make_async_remote_copy(src_ref=src, dst_ref=outs[a].at[me],
                                                  send_sem=send_sems.at[k], recv_sem=recv_sems.at[k],
                                                  device_id=(px, py, pc), device_id_type=MESH)
                cp.start()
                rc = pltpu.make_async_remote_copy(src_ref=src, dst_ref=outs[a].at[peer],
                                                  send_sem=send_sems.at[k], recv_sem=recv_sems.at[k],
                                                  device_id=(px, py, pc), device_id_type=MESH)
                started.append((cp, rc))
        for cp, rc in started:
            if rc is None:
                cp.wait()
            else:
                cp.wait_send()
                rc.wait_recv()

    hbm = pl.BlockSpec(memory_space=pltpu.HBM)
    out_shape = tuple(
        jax.ShapeDtypeStruct(((N_DEV,) + a.shape) if gather else a.shape, a.dtype) for a in arrs)
    return pl.pallas_call(
        body, name=name, out_shape=out_shape,
        in_specs=[hbm] * n, out_specs=tuple([hbm] * n),
        scratch_shapes=[pltpu.SemaphoreType.DMA((n * n_peer,)), pltpu.SemaphoreType.DMA((n * n_peer,)),
                        pltpu.SemaphoreType.DMA((n,))],
        compiler_params=pltpu.CompilerParams(has_side_effects=True),
    )(*arrs)


def _mm(a, b, *, mode, tm, tn, tk, outs, epilogue=None, extras=(), out_blocked=False, name):
    if mode == "nn":
        (M, K), (_, N) = a.shape, b.shape
    elif mode == "nt":
        (M, K), (N, _) = a.shape, b.shape
    else:
        (K, M), (_, N) = a.shape, b.shape
    tm, tn, tk = min(tm, M), min(tn, N), min(tk, K)
    assert M % tm == 0 and N % tn == 0 and K % tk == 0, (name, M, N, K, tm, tn, tk)
    if mode == "nn":
        a_spec = pl.BlockSpec((tm, tk), lambda i, j, k: (i, k))
        b_spec = pl.BlockSpec((tk, tn), lambda i, j, k: (k, j))
        dims = (((1,), (0,)), ((), ()))
    elif mode == "nt":
        a_spec = pl.BlockSpec((tm, tk), lambda i, j, k: (i, k))
        b_spec = pl.BlockSpec((tn, tk), lambda i, j, k: (j, k))
        dims = (((1,), (1,)), ((), ()))
    else:
        a_spec = pl.BlockSpec((tk, tm), lambda i, j, k: (k, i))
        b_spec = pl.BlockSpec((tk, tn), lambda i, j, k: (k, j))
        dims = (((0,), (0,)), ((), ()))
    nk = K // tk
    n_ex, n_out = len(extras), len(outs)
    if epilogue is None:
        epilogue = lambda acc, ex: tuple(acc.astype(d) for d in outs)

    def body(a_ref, b_ref, *refs):
        ex_refs, out_refs = refs[:n_ex], refs[n_ex:n_ex + n_out]

        def finish(acc):
            res = epilogue(acc, [r[...] for r in ex_refs])
            for o_ref, v in zip(out_refs, res):
                o_ref[...] = v.astype(o_ref.dtype)

        part = lax.dot_general(a_ref[...], b_ref[...], dims, preferred_element_type=F32)
        if nk == 1:
            finish(part)
        else:
            acc_ref = refs[n_ex + n_out]
            k = pl.program_id(2)

            @pl.when(k == 0)
            def _():
                acc_ref[...] = part

            @pl.when(k > 0)
            def _():
                acc_ref[...] += part

            @pl.when(k == nk - 1)
            def _():
                finish(acc_ref[...])

    ex_specs = [pl.BlockSpec((tm, tn), lambda i, j, k: (i, j)) if kind == "tile"
                else pl.BlockSpec((1, tn), lambda i, j, k: (0, j)) for _, kind in extras]
    if out_blocked:
        o_spec = pl.BlockSpec((None, tm, tn), lambda i, j, k: (j, i, 0))
        o_shape = (N // tn, M, tn)
    else:
        o_spec = pl.BlockSpec((tm, tn), lambda i, j, k: (i, j))
        o_shape = (M, N)
    res = pl.pallas_call(
        body, name=name, grid=(M // tm, N // tn, nk),
        in_specs=[a_spec, b_spec] + ex_specs,
        out_specs=tuple([o_spec] * n_out),
        out_shape=tuple(jax.ShapeDtypeStruct(o_shape, d) for d in outs),
        scratch_shapes=[pltpu.VMEM((tm, tn), F32)] if nk > 1 else [],
        compiler_params=_cparams(3, big=True),
    )(a, b, *[e for e, _ in extras])
    return res[0] if n_out == 1 else res


def _tok_spec(ts, width, col_block=0):
    return pl.BlockSpec((None, ts, width), lambda b, s: (b, s, col_block))


def _brow_spec(width):
    return pl.BlockSpec((None, 1, width), lambda b, s: (b, 0, 0))


def _vec_spec(width):
    return pl.BlockSpec((1, width), lambda b, s: (0, 0))


def _modulate(x, sc, sh, ts):
    Bl, S, D = x.shape

    def body(x_ref, sc_ref, sh_ref, o_ref):
        o_ref[...] = (x_ref[...] * (1.0 + sc_ref[...]) + sh_ref[...]).astype(BF16)

    return pl.pallas_call(
        body, name="modulate", grid=(Bl, S // ts),
        in_specs=[_tok_spec(ts, D), _brow_spec(D), _brow_spec(D)],
        out_specs=_tok_spec(ts, D), out_shape=jax.ShapeDtypeStruct((Bl, S, D), BF16),
        compiler_params=_cparams(2),
    )(x, sc, sh)


def _mix_fwd(proj, w_conv, b_conv, w_rg_a, b_rg_a, w_rg_x, b_rg_x, lam, w_sp, b_sp_t, ln_v_g, ln_v_b, *, tm, lw, sw):
    Bl, S, _ = proj.shape
    heads, hd = w_rg_a.shape[0], w_rg_a.shape[1]
    groups = w_sp.shape[0]
    cw = 2 * lw + 2 * sw
    nblk = tm // SGU_BLOCK

    def body(p_ref, wc_ref, bc_ref, wa_ref, ba_ref, wx_ref, bx_ref, lam_ref, wsp_ref, bsp_ref, lg_ref, lb_ref,
             hs_ref, ya_ref, ys_ref, xprev, hcar, a_scr, b_scr):
        s = pl.program_id(1)

        @pl.when(s == 0)
        def _():
            xprev[...] = jnp.zeros_like(xprev)
            hcar[...] = jnp.zeros_like(hcar)

        xl = p_ref[:, 0:lw]
        gl = p_ref[:, lw:2 * lw]
        row8 = lax.broadcasted_iota(jnp.int32, (SUBLANES, lw), 0)
        rowm = lax.broadcasted_iota(jnp.int32, (tm, lw), 0) & (SUBLANES - 1)

        prev = xprev[...]
        xc = xl * wc_ref[3:4, :] + bc_ref[...]
        for k in (1, 2, 3):
            xr = pltpu.roll(xl, k, 0)
            head = jnp.where(row8 < k, pltpu.roll(prev, k, 0), xr[0:SUBLANES])
            xs = jnp.concatenate([head, xr[SUBLANES:]], axis=0)
            xc = xc + xs * wc_ref[3 - k:4 - k, :]
        xprev[...] = xl[tm - SUBLANES:tm]

        xcb = xc.astype(BF16)
        pa = jnp.concatenate([jnp.dot(xcb[:, h * hd:(h + 1) * hd], wa_ref[h], preferred_element_type=F32)
                              for h in range(heads)], axis=1) + ba_ref[...]
        px = jnp.concatenate([jnp.dot(xcb[:, h * hd:(h + 1) * hd], wx_ref[h], preferred_element_type=F32)
                              for h in range(heads)], axis=1) + bx_ref[...]
        r = _sigmoid(pa)
        ig = _sigmoid(px)
        nl = -lam_ref[...]
        big_l = -LRU_C * (jnp.maximum(nl, 0.0) + _log1p_pos(jnp.exp(-jnp.abs(nl))))
        la = big_l * r
        a = jnp.exp(la)
        bin_ = jnp.sqrt(-_expm1(2.0 * la)) * (ig * xc)

        for d in (1, 2, 4):
            a_sh = pltpu.roll(a, d, 0)
            b_sh = pltpu.roll(bin_, d, 0)
            msk = rowm >= d
            bin_ = jnp.where(msk, a * b_sh + bin_, bin_)
            a = jnp.where(msk, a * a_sh, a)
        a_scr[...] = a
        b_scr[...] = bin_

        def grp(g, carry):
            off = pl.multiple_of(g * SUBLANES, SUBLANES)
            h = b_scr[pl.ds(off, SUBLANES), :] + a_scr[pl.ds(off, SUBLANES), :] * carry
            hs_ref[pl.ds(off, SUBLANES), :] = h
            return jnp.broadcast_to(h[SUBLANES - 1:SUBLANES, :], h.shape)

        hcar[...] = lax.fori_loop(0, tm // SUBLANES, grp, hcar[...])
        ya_ref[...] = (hs_ref[...] * _gelu(gl)).astype(BF16)

        gu = _gelu(p_ref[:, 2 * lw:2 * lw + sw])
        gv = _gelu(p_ref[:, 2 * lw + sw:cw])
        xhat, _ = _ln_stats(gv)
        vn = (xhat * lg_ref[...] + lb_ref[...]).astype(BF16)
        tpos = lax.broadcasted_iota(jnp.int32, (SGU_BLOCK, SGU_BLOCK), 0) // CHUNK
        spos = lax.broadcasted_iota(jnp.int32, (SGU_BLOCK, SGU_BLOCK), 1) // CHUNK
        gw = sw // groups
        rows_out = []
        for blk in range(nblk):
            r0 = blk * SGU_BLOCK
            cols = []
            for g in range(groups):
                wm = jnp.where(spos <= tpos, wsp_ref[g], 0.0).astype(BF16)
                mixed = jnp.dot(wm, vn[r0:r0 + SGU_BLOCK, g * gw:(g + 1) * gw], preferred_element_type=F32)
                cols.append(mixed + bsp_ref[:, g:g + 1])
            rows_out.append(jnp.concatenate(cols, axis=1))
        mixed_all = jnp.concatenate(rows_out, axis=0) if nblk > 1 else rows_out[0]
        ys_ref[...] = (gu * mixed_all).astype(BF16)

    full = lambda shp: pl.BlockSpec(shp, lambda b, s: (0,) * len(shp))
    return pl.pallas_call(
        body, name="mix_fwd", grid=(Bl, S // tm),
        in_specs=[_tok_spec(tm, cw), full(w_conv.shape), full(b_conv.shape), full(w_rg_a.shape), full(b_rg_a.shape),
                  full(w_rg_x.shape), full(b_rg_x.shape), full(lam.shape), full(w_sp.shape), full(b_sp_t.shape),
                  full(ln_v_g.shape), full(ln_v_b.shape)],
        out_specs=(_tok_spec(tm, lw), _tok_spec(tm, lw), _tok_spec(tm, sw)),
        out_shape=(jax.ShapeDtypeStruct((Bl, S, lw), F32), jax.ShapeDtypeStruct((Bl, S, lw), BF16),
                   jax.ShapeDtypeStruct((Bl, S, sw), BF16)),
        scratch_shapes=[pltpu.VMEM((SUBLANES, lw), F32), pltpu.VMEM((SUBLANES, lw), F32),
                        pltpu.VMEM((tm, lw), F32), pltpu.VMEM((tm, lw), F32)],
        compiler_params=_cparams(2, big=True),
    )(proj, w_conv, b_conv, w_rg_a, b_rg_a, w_rg_x, b_rg_x, lam, w_sp, b_sp_t, ln_v_g, ln_v_b)


def _merge_fwd(proj, y_a, y_b, *, ts, d):
    Bl, S, din = proj.shape
    gcol = (din - 2 * d) // (2 * d)
    assert gcol * 2 * d == din - 2 * d

    def body(g_ref, ya_ref, yb_ref, o_ref):
        o_ref[...] = (_sigmoid(g_ref[:, 0:d]) * ya_ref[...] + _sigmoid(g_ref[:, d:2 * d]) * yb_ref[...]).astype(BF16)

    return pl.pallas_call(
        body, name="merge_fwd", grid=(Bl, S // ts),
        in_specs=[_tok_spec(ts, 2 * d, gcol), _tok_spec(ts, d), _tok_spec(ts, d)],
        out_specs=_tok_spec(ts, d), out_shape=jax.ShapeDtypeStruct((Bl, S, d), BF16),
        compiler_params=_cparams(2),
    )(proj, y_a, y_b)


def _ln1_fwd(x, mix, gt1, g1, b1, sc2, sh2, *, ts):
    Bl, S, D = x.shape

    def body(x_ref, mix_ref, gt_ref, g_ref, b_ref, sc_ref, sh_ref, x1_ref, h2_ref):
        z = ALPHA * x_ref[...] + (1.0 + gt_ref[...]) * mix_ref[...]
        xhat, _ = _ln_stats(z)
        x1 = xhat * g_ref[...] + b_ref[...]
        x1_ref[...] = x1
        h2_ref[...] = (x1 * (1.0 + sc_ref[...]) + sh_ref[...]).astype(BF16)

    return pl.pallas_call(
        body, name="ln1_fwd", grid=(Bl, S // ts),
        in_specs=[_tok_spec(ts, D), _tok_spec(ts, D), _brow_spec(D), _vec_spec(D), _vec_spec(D), _brow_spec(D),
                  _brow_spec(D)],
        out_specs=(_tok_spec(ts, D), _tok_spec(ts, D)),
        out_shape=(jax.ShapeDtypeStruct((Bl, S, D), F32), jax.ShapeDtypeStruct((Bl, S, D), BF16)),
        compiler_params=_cparams(2),
    )(x, mix, gt1, g1, b1, sc2, sh2)


def _ln2_loss(x1, f, tgt, gt2, g2, b2, *, ts):
    Bl, S, D = x1.shape

    def body(x1_ref, f_ref, t_ref, gt_ref, g_ref, b_ref, df_ref, dx1_ref, dgt_ref, dg_ref, db_ref, loss_ref):
        s = pl.program_id(1)

        @pl.when(_first_step())
        def _():
            dg_ref[...] = jnp.zeros_like(dg_ref)
            db_ref[...] = jnp.zeros_like(db_ref)
            loss_ref[...] = jnp.zeros_like(loss_ref)

        @pl.when(s == 0)
        def _():
            dgt_ref[...] = jnp.zeros_like(dgt_ref)

        fv = f_ref[...]
        z = ALPHA * x1_ref[...] + (1.0 + gt_ref[...]) * fv
        xhat, rstd = _ln_stats(z)
        x2 = xhat * g_ref[...] + b_ref[...]
        err = x2 - t_ref[...]
        loss_ref[...] += 0.5 * jnp.sum(jnp.mean(err * err, axis=-1, keepdims=True))
        dy = err * (1.0 / D)
        dg_ref[...] += _colsum(dy * xhat)
        db_ref[...] += _colsum(dy)
        dz = _ln_bwd(dy, xhat, rstd, g_ref[...])
        dx1_ref[...] = ALPHA * dz
        dgt_ref[...] += _colsum(dz * fv)
        df_ref[...] = (dz * (1.0 + gt_ref[...])).astype(BF16)

    return pl.pallas_call(
        body, name="ln2_loss", grid=(Bl, S // ts),
        in_specs=[_tok_spec(ts, D), _tok_spec(ts, D), _tok_spec(ts, D), _brow_spec(D), _vec_spec(D), _vec_spec(D)],
        out_specs=(_tok_spec(ts, D), _tok_spec(ts, D), _brow_spec(D), _vec_spec(D), _vec_spec(D),
                   pl.BlockSpec((SUBLANES, LANES), lambda b, s: (0, 0))),
        out_shape=(jax.ShapeDtypeStruct((Bl, S, D), BF16), jax.ShapeDtypeStruct((Bl, S, D), F32),
                   jax.ShapeDtypeStruct((Bl, 1, D), F32), jax.ShapeDtypeStruct((1, D), F32),
                   jax.ShapeDtypeStruct((1, D), F32), jax.ShapeDtypeStruct((SUBLANES, LANES), F32)),
        compiler_params=_cparams(2),
    )(x1, f, tgt, gt2, g2, b2)


def _ln1_bwd(dx1p, dh2, x1, x, mix, sc2, gt1, g1, *, ts):
    Bl, S, D = x.shape

    def body(dx1p_ref, dh2_ref, x1_ref, x_ref, mix_ref, sc_ref, gt_ref, g_ref,
             dxp_ref, dmix_ref, dsc_ref, dsh_ref, dgt_ref, dg_ref, db_ref):
        s = pl.program_id(1)

        @pl.when(_first_step())
        def _():
            dg_ref[...] = jnp.zeros_like(dg_ref)
            db_ref[...] = jnp.zeros_like(db_ref)

        @pl.when(s == 0)
        def _():
            dsc_ref[...] = jnp.zeros_like(dsc_ref)
            dsh_ref[...] = jnp.zeros_like(dsh_ref)
            dgt_ref[...] = jnp.zeros_like(dgt_ref)

        dh2 = dh2_ref[...]
        mixv = mix_ref[...]
        dsc_ref[...] += _colsum(dh2 * x1_ref[...])
        dsh_ref[...] += _colsum(dh2)
        dx1 = dx1p_ref[...] + dh2 * (1.0 + sc_ref[...])
        z = ALPHA * x_ref[...] + (1.0 + gt_ref[...]) * mixv
        xhat, rstd = _ln_stats(z)
        dg_ref[...] += _colsum(dx1 * xhat)
        db_ref[...] += _colsum(dx1)
        dz = _ln_bwd(dx1, xhat, rstd, g_ref[...])
        dxp_ref[...] = ALPHA * dz
        dgt_ref[...] += _colsum(dz * mixv)
        dmix_ref[...] = (dz * (1.0 + gt_ref[...])).astype(BF16)

    return pl.pallas_call(
        body, name="ln1_bwd", grid=(Bl, S // ts),
        in_specs=[_tok_spec(ts, D)] * 5 + [_brow_spec(D), _brow_spec(D), _vec_spec(D)],
        out_specs=(_tok_spec(ts, D), _tok_spec(ts, D), _brow_spec(D), _brow_spec(D), _brow_spec(D), _vec_spec(D),
                   _vec_spec(D)),
        out_shape=(jax.ShapeDtypeStruct((Bl, S, D), F32), jax.ShapeDtypeStruct((Bl, S, D), BF16),
                   jax.ShapeDtypeStruct((Bl, 1, D), F32), jax.ShapeDtypeStruct((Bl, 1, D), F32),
                   jax.ShapeDtypeStruct((Bl, 1, D), F32), jax.ShapeDtypeStruct((1, D), F32),
                   jax.ShapeDtypeStruct((1, D), F32)),
        compiler_params=_cparams(2),
    )(dx1p, dh2, x1, x, mix, sc2, gt1, g1)


def _merge_bwd(dmerged, y_a, y_b, proj, *, ts, d):
    Bl, S, din = proj.shape
    gcol = (din - 2 * d) // (2 * d)

    def body(dm_ref, ya_ref, yb_ref, g_ref, dya_ref, dyb_ref, dp_ref, db_ref):
        @pl.when(_first_step())
        def _():
            db_ref[...] = jnp.zeros_like(db_ref)

        dm = dm_ref[...]
        sa = _sigmoid(g_ref[:, 0:d])
        sb = _sigmoid(g_ref[:, d:2 * d])
        dya_ref[...] = (dm * sa).astype(BF16)
        dyb_ref[...] = (dm * sb).astype(BF16)
        dga = dm * ya_ref[...] * sa * (1.0 - sa)
        dgb = dm * yb_ref[...] * sb * (1.0 - sb)
        dp_ref[:, 0:d] = dga.astype(BF16)
        dp_ref[:, d:2 * d] = dgb.astype(BF16)
        db_ref[:, 0:d] += _colsum(dga)
        db_ref[:, d:2 * d] += _colsum(dgb)

    return pl.pallas_call(
        body, name="merge_bwd", grid=(Bl, S // ts),
        in_specs=[_tok_spec(ts, d), _tok_spec(ts, d), _tok_spec(ts, d), _tok_spec(ts, 2 * d, gcol)],
        out_specs=(_tok_spec(ts, d), _tok_spec(ts, d), _tok_spec(ts, 2 * d, gcol), _vec_spec(2 * d)),
        out_shape=(jax.ShapeDtypeStruct((Bl, S, d), BF16), jax.ShapeDtypeStruct((Bl, S, d), BF16),
                   jax.ShapeDtypeStruct((Bl, S, din), BF16), jax.ShapeDtypeStruct((1, 2 * d), F32)),
        compiler_params=_cparams(2),
    )(dmerged, y_a, y_b, proj)


def _mix_bwd(proj, hs, dya, dys, dproj, w_conv, b_conv, w_rg_a, b_rg_a, w_rg_x, b_rg_x, lam, w_sp, b_sp_t,
             ln_v_g, ln_v_b, *, tm, lw, sw):
    Bl, S, din = proj.shape
    heads, hd = w_rg_a.shape[0], w_rg_a.shape[1]
    groups = w_sp.shape[0]
    gw = sw // groups
    cw = 2 * lw + 2 * sw
    nblk = tm // SGU_BLOCK
    n_s = S // tm
    per8 = tm // SUBLANES

    def body(p_ref, xh_ref, hs_ref, hh_ref, dya_ref, dys_ref, dpin_ref,
             wc_ref, bc_ref, wa_ref, ba_ref, wx_ref, bx_ref, lam_ref, wsp_ref, bsp_ref, lg_ref, lb_ref,
             dp_ref, dbin_ref, dwc_ref, dbc_ref, dwa_ref, dba_ref, dwx_ref, dbx_ref, dlam_ref, dwsp_ref, dbsp_ref,
             dlg_ref, dlb_ref,
             dhcar, acar, dxcn, a_scr, b_scr, g_scr):
        del dpin_ref
        sr = pl.program_id(1)
        first_tile = sr == n_s - 1

        @pl.when(_first_step())
        def _():
            for ref in (dbin_ref, dwc_ref, dbc_ref, dwa_ref, dba_ref, dwx_ref, dbx_ref, dlam_ref, dwsp_ref, dbsp_ref,
                        dlg_ref, dlb_ref):
                ref[...] = jnp.zeros_like(ref)

        @pl.when(sr == 0)
        def _():
            dhcar[...] = jnp.zeros_like(dhcar)
            acar[...] = jnp.zeros_like(acar)
            dxcn[...] = jnp.zeros_like(dxcn)

        keep = jnp.where(first_tile, 0.0, 1.0)
        xl = p_ref[:, 0:lw]
        gl = p_ref[:, lw:2 * lw]
        row8 = lax.broadcasted_iota(jnp.int32, (SUBLANES, lw), 0)
        rowm = lax.broadcasted_iota(jnp.int32, (tm, lw), 0) & (SUBLANES - 1)

        prev = xh_ref[...] * keep
        xsh = [xl]
        for k in (1, 2, 3):
            xr = pltpu.roll(xl, k, 0)
            head = jnp.where(row8 < k, pltpu.roll(prev, k, 0), xr[0:SUBLANES])
            xsh.append(jnp.concatenate([head, xr[SUBLANES:]], axis=0))
        xc = bc_ref[...] + xsh[0] * wc_ref[3:4, :]
        for k in (1, 2, 3):
            xc = xc + xsh[k] * wc_ref[3 - k:4 - k, :]
        xcb = xc.astype(BF16)
        pa = jnp.concatenate([jnp.dot(xcb[:, h * hd:(h + 1) * hd], wa_ref[h], preferred_element_type=F32)
                              for h in range(heads)], axis=1) + ba_ref[...]
        px = jnp.concatenate([jnp.dot(xcb[:, h * hd:(h + 1) * hd], wx_ref[h], preferred_element_type=F32)
                              for h in range(heads)], axis=1) + bx_ref[...]
        r = _sigmoid(pa)
        ig = _sigmoid(px)
        nl = -lam_ref[...]
        big_l = -LRU_C * (jnp.maximum(nl, 0.0) + _log1p_pos(jnp.exp(-jnp.abs(nl))))
        la = big_l * r
        a = jnp.exp(la)
        msq = -_expm1(2.0 * la)
        m = jnp.sqrt(msq)
        hsv = hs_ref[...]
        ggl, dggl = _gelu_and_grad(gl)

        dyav = dya_ref[...]
        dhs = dyav * ggl
        dp_ref[:, lw:2 * lw] = (dyav * hsv * dggl).astype(BF16)
        dbin_ref[:, lw:2 * lw] += _colsum(dyav * hsv * dggl)
        a_up = pltpu.roll(a, tm - 1, 0)
        tail = jnp.where(row8 == SUBLANES - 1, acar[...], a_up[tm - SUBLANES:tm])
        an = jnp.concatenate([a_up[:tm - SUBLANES], tail], axis=0)
        acar[...] = jnp.broadcast_to(a[0:1, :], (SUBLANES, lw))
        bb = dhs
        for d in (1, 2, 4):
            a_sh = pltpu.roll(an, tm - d, 0)
            b_sh = pltpu.roll(bb, tm - d, 0)
            msk = rowm < SUBLANES - d
            bb = jnp.where(msk, an * b_sh + bb, bb)
            an = jnp.where(msk, an * a_sh, an)
        a_scr[...] = an
        b_scr[...] = bb

        def grp(i, carry):
            off = pl.multiple_of((per8 - 1 - i) * SUBLANES, SUBLANES)
            g = b_scr[pl.ds(off, SUBLANES), :] + a_scr[pl.ds(off, SUBLANES), :] * carry
            g_scr[pl.ds(off, SUBLANES), :] = g
            return jnp.broadcast_to(g[0:1, :], g.shape)

        dhcar[...] = lax.fori_loop(0, per8, grp, dhcar[...])
        dh = g_scr[...]

        hr = pltpu.roll(hsv, 1, 0)
        hhead = jnp.where(row8 < 1, pltpu.roll(hh_ref[...] * keep, 1, 0), hr[0:SUBLANES])
        hprev = jnp.concatenate([hhead, hr[SUBLANES:]], axis=0)
        da = dh * hprev
        ixc = ig * xc
        dm = dh * ixc
        dixc = dh * m
        di = dixc * xc
        dxc = dixc * ig
        dla = da * a - dm * (a * a) / m
        dlam_ref[...] += _colsum(dla * r) * (LRU_C * _sigmoid(nl))
        dr = dla * big_l
        dpa = dr * r * (1.0 - r)
        dpx = di * ig * (1.0 - ig)
        dba_ref[...] += _colsum(dpa)
        dbx_ref[...] += _colsum(dpx)
        dpab = dpa.astype(BF16)
        dpxb = dpx.astype(BF16)
        nt = (((1,), (1,)), ((), ()))
        tn = (((0,), (0,)), ((), ()))
        dxc_g = []
        for h in range(heads):
            sl = slice(h * hd, (h + 1) * hd)
            dxc_g.append(lax.dot_general(dpab[:, sl], wa_ref[h], nt, preferred_element_type=F32)
                         + lax.dot_general(dpxb[:, sl], wx_ref[h], nt, preferred_element_type=F32))
            dwa_ref[h] += lax.dot_general(xcb[:, sl], dpab[:, sl], tn, preferred_element_type=F32)
            dwx_ref[h] += lax.dot_general(xcb[:, sl], dpxb[:, sl], tn, preferred_element_type=F32)
        dxc = dxc + jnp.concatenate(dxc_g, axis=1)

        dbc_ref[...] += _colsum(dxc)
        for k in range(4):
            dwc_ref[k:k + 1, :] += _colsum(dxc * xsh[3 - k])
        nxt = dxcn[...]
        dxl = dxc * wc_ref[3:4, :]
        for k in (1, 2, 3):
            ur = pltpu.roll(dxc, tm - k, 0)
            tl = jnp.where(row8 >= SUBLANES - k, pltpu.roll(nxt, SUBLANES - k, 0), ur[tm - SUBLANES:tm])
            dxl = dxl + jnp.concatenate([ur[:tm - SUBLANES], tl], axis=0) * wc_ref[3 - k:4 - k, :]
        dxcn[...] = dxc[0:SUBLANES]
        dp_ref[:, 0:lw] = dxl.astype(BF16)
        dbin_ref[:, 0:lw] += _colsum(dxl)

        gu, dgu_dx = _gelu_and_grad(p_ref[:, 2 * lw:2 * lw + sw])
        gv, dgv_dx = _gelu_and_grad(p_ref[:, 2 * lw + sw:cw])
        xhat, rstd = _ln_stats(gv)
        vn = (xhat * lg_ref[...] + lb_ref[...]).astype(BF16)
        dys = dys_ref[...]
        dmixed = dys * gu
        dmb = dmixed.astype(BF16)
        tpos = lax.broadcasted_iota(jnp.int32, (SGU_BLOCK, SGU_BLOCK), 0) // CHUNK
        spos = lax.broadcasted_iota(jnp.int32, (SGU_BLOCK, SGU_BLOCK), 1) // CHUNK
        causal = spos <= tpos
        mixed_rows, dvn_rows = [], []
        for blk in range(nblk):
            rs = slice(blk * SGU_BLOCK, (blk + 1) * SGU_BLOCK)
            mcols, dcols = [], []
            for g in range(groups):
                cs = slice(g * gw, (g + 1) * gw)
                wm = jnp.where(causal, wsp_ref[g], 0.0).astype(BF16)
                mcols.append(jnp.dot(wm, vn[rs, cs], preferred_element_type=F32) + bsp_ref[:, g:g + 1])
                dcols.append(lax.dot_general(wm, dmb[rs, cs], tn, preferred_element_type=F32))
                dw = lax.dot_general(dmb[rs, cs], vn[rs, cs], nt, preferred_element_type=F32)
                dwsp_ref[g] += jnp.where(causal, dw, 0.0)
                dbsp_ref[:, g:g + 1] += jnp.sum(dmixed[rs, cs], axis=1, keepdims=True)
            mixed_rows.append(jnp.concatenate(mcols, axis=1))
            dvn_rows.append(jnp.concatenate(dcols, axis=1))
        mixed_all = jnp.concatenate(mixed_rows, axis=0) if nblk > 1 else mixed_rows[0]
        dvn = jnp.concatenate(dvn_rows, axis=0) if nblk > 1 else dvn_rows[0]
        du = dys * mixed_all * dgu_dx
        dlg_ref[...] += _colsum(dvn * xhat)
        dlb_ref[...] += _colsum(dvn)
        dv = _ln_bwd(dvn, xhat, rstd, lg_ref[...]) * dgv_dx
        dp_ref[:, 2 * lw:2 * lw + sw] = du.astype(BF16)
        dp_ref[:, 2 * lw + sw:cw] = dv.astype(BF16)
        dbin_ref[:, 2 * lw:2 * lw + sw] += _colsum(du)
        dbin_ref[:, 2 * lw + sw:cw] += _colsum(dv)

    rev = lambda s: n_s - 1 - s
    tile = lambda w: pl.BlockSpec((None, tm, w), lambda b, s: (b, rev(s), 0))
    halo = lambda w: pl.BlockSpec((None, SUBLANES, w), lambda b, s: (b, jnp.maximum(rev(s) * per8 - 1, 0), 0))
    full = lambda shp: pl.BlockSpec(shp, lambda b, s: (0,) * len(shp))
    small = [w_conv, b_conv, w_rg_a, b_rg_a, w_rg_x, b_rg_x, lam, w_sp, b_sp_t, ln_v_g, ln_v_b]
    acc_shapes = [(1, cw), w_conv.shape, b_conv.shape, w_rg_a.shape, b_rg_a.shape, w_rg_x.shape, b_rg_x.shape,
                  lam.shape, w_sp.shape, b_sp_t.shape, ln_v_g.shape, ln_v_b.shape]
    res = pl.pallas_call(
        body, name="mix_bwd", grid=(Bl, n_s),
        in_specs=[tile(cw), halo(lw), tile(lw), halo(lw), tile(lw), tile(sw), pl.BlockSpec(memory_space=pl.ANY)]
                 + [full(w.shape) for w in small],
        out_specs=tuple([tile(cw)] + [full(shp) for shp in acc_shapes]),
        out_shape=tuple([jax.ShapeDtypeStruct((Bl, S, din), BF16)] + [jax.ShapeDtypeStruct(shp, F32) for shp in acc_shapes]),
        input_output_aliases={6: 0},
        scratch_shapes=[pltpu.VMEM((SUBLANES, lw), F32), pltpu.VMEM((SUBLANES, lw), F32), pltpu.VMEM((SUBLANES, lw), F32),
                        pltpu.VMEM((tm, lw), F32), pltpu.VMEM((tm, lw), F32), pltpu.VMEM((tm, lw), F32)],
        compiler_params=_cparams(2, big=True),
    )(proj, proj, hs, hs, dya, dys, dproj, *small)
    return res


def _final_dx(dxp, dh, x, sc1, *, ts):
    Bl, S, D = x.shape

    def body(dxp_ref, dh_ref, x_ref, sc_ref, dx_ref, dsc_ref, dsh_ref):
        @pl.when(pl.program_id(1) == 0)
        def _():
            dsc_ref[...] = jnp.zeros_like(dsc_ref)
            dsh_ref[...] = jnp.zeros_like(dsh_ref)

        dh = dh_ref[...]
        dx_ref[...] = dxp_ref[...] + dh * (1.0 + sc_ref[...])
        dsc_ref[...] += _colsum(dh * x_ref[...])
        dsh_ref[...] += _colsum(dh)

    return pl.pallas_call(
        body, name="final_dx", grid=(Bl, S // ts),
        in_specs=[_tok_spec(ts, D), _tok_spec(ts, D), _tok_spec(ts, D), _brow_spec(D)],
        out_specs=(_tok_spec(ts, D), _brow_spec(D), _brow_spec(D)),
        out_shape=(jax.ShapeDtypeStruct((Bl, S, D), F32), jax.ShapeDtypeStruct((Bl, 1, D), F32),
                   jax.ShapeDtypeStruct((Bl, 1, D), F32)),
        compiler_params=_cparams(2),
    )(dxp, dh, x, sc1)


def _ada_fwd(c_all, w_ada):
    R, D = c_all.shape
    nb = w_ada.shape[1]

    def body(c_ref, w_ref, act_ref, o_ref):
        cv = c_ref[...]
        act = (cv * _sigmoid(cv)).astype(BF16)
        act_ref[...] = act
        o_ref[...] = jnp.dot(act, w_ref[...].astype(BF16), preferred_element_type=F32)

    return pl.pallas_call(
        body, name="ada_fwd",
        out_shape=(jax.ShapeDtypeStruct((R, D), BF16), jax.ShapeDtypeStruct((R, nb), F32)),
        compiler_params=pltpu.CompilerParams(vmem_limit_bytes=VMEM_LIMIT),
    )(c_all, w_ada)


def _ada_bwd(c_act, dmod_cols):
    R, D = c_act.shape
    nb = dmod_cols.shape[1]

    def body(act_ref, d_ref, o_ref):
        o_ref[...] = lax.dot_general(act_ref[...], d_ref[...].astype(BF16), (((0,), (0,)), ((), ())),
                                     preferred_element_type=F32)

    return pl.pallas_call(
        body, name="ada_bwd", out_shape=jax.ShapeDtypeStruct((D, nb), F32),
        compiler_params=pltpu.CompilerParams(vmem_limit_bytes=VMEM_LIMIT),
    )(c_act, dmod_cols)


def _adamw(w, g_slots, m, v, *, tr, name):
    R, C = w.shape
    n_slot = g_slots.shape[0]
    tr = min(tr, R)
    assert R % tr == 0, (name, R, tr)
    c1 = 1.0 / (1.0 - ADAM_B1 ** ADAM_STEP)
    c2 = 1.0 / (1.0 - ADAM_B2 ** ADAM_STEP)

    def body(w_ref, g_ref, m_ref, v_ref, go_ref, d_ref, mo_ref, vo_ref):
        g = g_ref[0]
        for i in range(1, n_slot):
            g = g + g_ref[i]
        mn = ADAM_B1 * m_ref[...] + (1.0 - ADAM_B1) * g
        vn = ADAM_B2 * v_ref[...] + (1.0 - ADAM_B2) * (g * g)
        go_ref[...] = g
        mo_ref[...] = mn
        vo_ref[...] = vn
        d_ref[...] = -ADAM_LR * ((mn * c1) / (jnp.sqrt(vn * c2) + ADAM_EPS) + ADAM_WD * w_ref[...])

    blk = pl.BlockSpec((tr, C), lambda i: (i, 0))
    return pl.pallas_call(
        body, name=name, grid=(R // tr,),
        in_specs=[blk, pl.BlockSpec((n_slot, tr, C), lambda i: (0, i, 0)), blk, blk],
        out_specs=(blk, blk, blk, blk),
        out_shape=tuple(jax.ShapeDtypeStruct((R, C), F32) for _ in range(4)),
        compiler_params=_cparams(1, big=True),
    )(w, g_slots, m, v)


def _sum_slots(g_slots, *, name):
    n_slot, R, C = g_slots.shape

    def body(g_ref, o_ref):
        g = g_ref[0]
        for i in range(1, n_slot):
            g = g + g_ref[i]
        o_ref[...] = g

    return pl.pallas_call(body, name=name, out_shape=jax.ShapeDtypeStruct((R, C), F32),
                          compiler_params=pltpu.CompilerParams(vmem_limit_bytes=VMEM_LIMIT))(g_slots)


SMALL_NAMES = ("b_ada", "b_in", "b_conv", "w_rg_a", "b_rg_a", "w_rg_x", "b_rg_x", "lru_lambda", "w_sp", "b_sp",
               "ln_v_g", "ln_v_b", "ln1_g", "ln1_b", "ln2_g", "ln2_b")
BIG_NAMES = ("w_ada", "w_in", "w_conv", "w_o_lru", "w_o_sgu", "w_out", "w_up", "w_down")
WEIGHT_ORDER = ("w_ada", "b_ada", "w_in", "b_in", "w_conv", "b_conv", "w_rg_a", "b_rg_a", "w_rg_x", "b_rg_x",
                "lru_lambda", "w_sp", "b_sp", "ln_v_g", "ln_v_b", "w_o_lru", "w_o_sgu", "w_out", "ln1_g", "ln1_b",
                "w_up", "w_down", "ln2_g", "ln2_b")


def _pack_small(d):
    flat = jnp.concatenate([d[n].reshape(-1) for n in SMALL_NAMES])
    rows = -(-flat.shape[0] // LANES)
    rows = -(-rows // (N_DEV * SUBLANES)) * (N_DEV * SUBLANES)
    flat = jnp.pad(flat, (0, rows * LANES - flat.shape[0]))
    return flat.reshape(rows, LANES)


def _unpack_small(packed, like):
    flat = packed.reshape(-1)
    out, off = {}, 0
    for n in SMALL_NAMES:
        sz = like[n].size
        out[n] = flat[off:off + sz].reshape(like[n].shape)
        off += sz
    return out


def _blocked_cols(w2d):
    K, N = w2d.shape
    return jnp.transpose(w2d.reshape(K, N_DEV, N // N_DEV), (1, 0, 2))


def _unblock_cols(wb):
    n, K, nb = wb.shape
    return jnp.transpose(wb, (1, 0, 2)).reshape(K, n * nb)


def kernel(x, c, w_ada, b_ada, w_in, b_in, w_conv, b_conv, w_rg_a, b_rg_a, w_rg_x, b_rg_x, lru_lambda, w_sp, b_sp, ln_v_g, ln_v_b, w_o_lru, w_o_sgu, w_out, ln1_g, ln1_b, w_up, w_down, ln2_g, ln2_b, loss_target, m_w_ada, m_b_ada, m_w_in, m_b_in, m_w_conv, m_b_conv, m_w_rg_a, m_b_rg_a, m_w_rg_x, m_b_rg_x, m_lru_lambda, m_w_sp, m_b_sp, m_ln_v_g, m_ln_v_b, m_w_o_lru, m_w_o_sgu, m_w_out, m_ln1_g, m_ln1_b, m_w_up, m_w_down, m_ln2_g, m_ln2_b, v_w_ada, v_b_ada, v_w_in, v_b_in, v_w_conv, v_b_conv, v_w_rg_a, v_b_rg_a, v_w_rg_x, v_b_rg_x, v_lru_lambda, v_w_sp, v_b_sp, v_ln_v_g, v_ln_v_b, v_w_o_lru, v_w_o_sgu, v_w_out, v_ln1_g, v_ln1_b, v_w_up, v_w_down, v_ln2_g, v_ln2_b):
    W = dict(w_ada=w_ada, b_ada=b_ada, w_in=w_in, b_in=b_in, w_conv=w_conv, b_conv=b_conv, w_rg_a=w_rg_a,
             b_rg_a=b_rg_a, w_rg_x=w_rg_x, b_rg_x=b_rg_x, lru_lambda=lru_lambda, w_sp=w_sp, b_sp=b_sp,
             ln_v_g=ln_v_g, ln_v_b=ln_v_b, w_o_lru=w_o_lru, w_o_sgu=w_o_sgu, w_out=w_out, ln1_g=ln1_g, ln1_b=ln1_b,
             w_up=w_up, w_down=w_down, ln2_g=ln2_g, ln2_b=ln2_b)
    Mo = dict(w_ada=m_w_ada, b_ada=m_b_ada, w_in=m_w_in, b_in=m_b_in, w_conv=m_w_conv, b_conv=m_b_conv,
              w_rg_a=m_w_rg_a, b_rg_a=m_b_rg_a, w_rg_x=m_w_rg_x, b_rg_x=m_b_rg_x, lru_lambda=m_lru_lambda,
              w_sp=m_w_sp, b_sp=m_b_sp, ln_v_g=m_ln_v_g, ln_v_b=m_ln_v_b, w_o_lru=m_w_o_lru, w_o_sgu=m_w_o_sgu,
              w_out=m_w_out, ln1_g=m_ln1_g, ln1_b=m_ln1_b, w_up=m_w_up, w_down=m_w_down, ln2_g=m_ln2_g,
              ln2_b=m_ln2_b)
    Vo = dict(w_ada=v_w_ada, b_ada=v_b_ada, w_in=v_w_in, b_in=v_b_in, w_conv=v_w_conv, b_conv=v_b_conv,
              w_rg_a=v_w_rg_a, b_rg_a=v_b_rg_a, w_rg_x=v_w_rg_x, b_rg_x=v_b_rg_x, lru_lambda=v_lru_lambda,
              w_sp=v_w_sp, b_sp=v_b_sp, ln_v_g=v_ln_v_g, ln_v_b=v_ln_v_b, w_o_lru=v_w_o_lru, w_o_sgu=v_w_o_sgu,
              w_out=v_w_out, ln1_g=v_ln1_g, ln1_b=v_ln1_b, w_up=v_w_up, w_down=v_w_down, ln2_g=v_ln2_g,
              ln2_b=v_ln2_b)

    Bl, S, D = x.shape
    T = Bl * S
    lw = b_conv.shape[-1]
    sw = ln_v_g.shape[-1]
    din = b_in.shape[-1]
    dff = w_up.shape[-1] * N_DEV
    ts = min(512, S)
    tmix = min(256, S)
    tmm = min(1024, T)

    c_pad = jnp.pad(c, ((0, SUBLANES - Bl), (0, 0)))
    shards = [c_pad, w_in[0].astype(BF16), w_o_lru[0].astype(BF16), w_o_sgu[0].astype(BF16), w_out[0].astype(BF16),
              w_up[0].astype(BF16), w_down[0].astype(BF16), w_conv[0]]
    c_g, win_g, wol_g, wos_g, wout_g, wup_g, wdown_g, wconv_g = _exchange(shards, True, "xchg_gather")
    Win = _unblock_cols(win_g)
    Wol = wol_g.reshape(lw, D)
    Wos = _unblock_cols(wos_g)
    Wout = wout_g.reshape(D, D)
    Wup = _unblock_cols(wup_g)
    Wdown = wdown_g.reshape(dff, D)
    wconv_full = _unblock_cols(wconv_g)

    c_act, modcols = _ada_fwd(c_g.reshape(N_DEV * SUBLANES, D), w_ada[0])
    (mod_slots,) = _exchange([modcols.reshape(N_DEV, SUBLANES, -1)], False, "xchg_mod")
    mod = _unblock_cols(mod_slots)[:Bl] + b_ada
    sh1, sc1, gt1, sh2, sc2, gt2 = [mod[:, i * D:(i + 1) * D].reshape(Bl, 1, D) for i in range(6)]

    wa_b, wx_b = w_rg_a[0].astype(BF16), w_rg_x[0].astype(BF16)
    b_sp_t = jnp.transpose(b_sp[0])
    small_mix = (wconv_full, b_conv, wa_b, b_rg_a, wx_b, b_rg_x, lru_lambda, w_sp[0], b_sp_t, ln_v_g, ln_v_b)

    h = _modulate(x, sc1, sh1, ts)
    proj = _mm(h.reshape(T, D), Win, mode="nn", tm=tmm, tn=768, tk=D, outs=[F32],
               extras=[(b_in, "row")], epilogue=lambda acc, ex: (acc + ex[0],), name="mm_proj")
    proj3 = proj.reshape(Bl, S, din)
    hs, ya_pre, ysgu = _mix_fwd(proj3, *small_mix, tm=tmix, lw=lw, sw=sw)
    y_a = _mm(ya_pre.reshape(T, lw), Wol, mode="nn", tm=tmm, tn=512, tk=lw, outs=[F32], name="mm_ya")
    y_b = _mm(ysgu.reshape(T, sw), Wos, mode="nn", tm=tmm, tn=512, tk=sw, outs=[F32], name="mm_yb")
    merged = _merge_fwd(proj3, y_a.reshape(Bl, S, D), y_b.reshape(Bl, S, D), ts=ts, d=D)
    mix = _mm(merged.reshape(T, D), Wout, mode="nn", tm=tmm, tn=512, tk=D, outs=[F32], name="mm_mix")
    mix3 = mix.reshape(Bl, S, D)
    x1, h2 = _ln1_fwd(x, mix3, gt1, ln1_g, ln1_b, sc2, sh2, ts=ts)
    up, act = _mm(h2.reshape(T, D), Wup, mode="nn", tm=tmm, tn=512, tk=D, outs=[F32, BF16],
                  epilogue=lambda acc, ex: (acc, jnp.square(jnp.maximum(acc, 0.0))), name="mm_up")
    f = _mm(act, Wdown, mode="nn", tm=tmm, tn=512, tk=1024, outs=[F32], name="mm_down")
    df, dx1p, dgt2, dg2, db2, loss_part = _ln2_loss(x1, f.reshape(Bl, S, D), loss_target, gt2, ln2_g, ln2_b, ts=ts)
    loss = lax.psum(loss_part[0, 0], ("x", "y", "c"))

    df2 = df.reshape(T, D)
    dup = _mm(df2, Wdown, mode="nt", tm=tmm, tn=512, tk=D, outs=[BF16], extras=[(up, "tile")],
              epilogue=lambda acc, ex: (acc * (2.0 * jnp.maximum(ex[0], 0.0)),), name="mm_dup")
    g_wdown = _mm(act, df2, mode="tn", tm=1024, tn=512, tk=512, outs=[F32], name="mm_gwdown")
    dh2 = _mm(dup, Wup, mode="nt", tm=tmm, tn=512, tk=1024, outs=[F32], name="mm_dh2")
    g_wup = _mm(h2.reshape(T, D), dup, mode="tn", tm=1024, tn=dff // N_DEV, tk=512, outs=[F32], out_blocked=True,
                name="mm_gwup")
    dxp, dmix, dsc2, dsh2, dgt1, dg1, db1 = _ln1_bwd(dx1p, dh2.reshape(Bl, S, D), x1, x, mix3, sc2, gt1, ln1_g, ts=ts)

    dmix2 = dmix.reshape(T, D)
    dmerged = _mm(dmix2, Wout, mode="nt", tm=tmm, tn=512, tk=D, outs=[F32], name="mm_dmerged")
    g_wout = _mm(merged.reshape(T, D), dmix2, mode="tn", tm=1024, tn=512, tk=512, outs=[F32], name="mm_gwout")
    dy_a, dy_b, dproj, dbin_hi = _merge_bwd(dmerged.reshape(Bl, S, D), y_a.reshape(Bl, S, D), y_b.reshape(Bl, S, D),
                                            proj3, ts=ts, d=D)
    dya_pre = _mm(dy_a.reshape(T, D), Wol, mode="nt", tm=tmm, tn=lw // 2, tk=D, outs=[F32], name="mm_dya")
    dysgu = _mm(dy_b.reshape(T, D), Wos, mode="nt", tm=tmm, tn=sw, tk=D, outs=[F32], name="mm_dys")
    g_wol = _mm(ya_pre.reshape(T, lw), dy_a.reshape(T, D), mode="tn", tm=lw, tn=512, tk=512, outs=[F32],
                name="mm_gwol")
    g_wos = _mm(ysgu.reshape(T, sw), dy_b.reshape(T, D), mode="tn", tm=sw, tn=D // N_DEV, tk=512, outs=[F32],
                out_blocked=True, name="mm_gwos")
    (dproj, dbin_lo, g_wconv, g_bconv, g_wa, g_ba, g_wx, g_bx, g_lam, g_wsp, g_bsp_t, g_lvg, g_lvb) = _mix_bwd(
        proj3, hs, dya_pre.reshape(Bl, S, lw), dysgu.reshape(Bl, S, sw), dproj, *small_mix, tm=tmix, lw=lw, sw=sw)
    dproj2 = dproj.reshape(T, din)
    dh = _mm(dproj2, Win, mode="nt", tm=tmm, tn=512, tk=din // 4, outs=[F32], name="mm_dh")
    g_win = _mm(h.reshape(T, D), dproj2, mode="tn", tm=1024, tn=din // N_DEV, tk=512, outs=[F32], out_blocked=True,
                name="mm_gwin")
    grad_x, dsc1, dsh1 = _final_dx(dxp, dh.reshape(Bl, S, D), x, sc1, ts=ts)

    dmod = jnp.concatenate([dsh1, dsc1, dgt1, dsh2, dsc2, dgt2], axis=-1).reshape(Bl, 6 * D)
    dmod_b = _blocked_cols(jnp.pad(dmod, ((0, SUBLANES - Bl), (0, 0))))
    g_small_local = dict(
        b_ada=jnp.sum(dmod, axis=0, keepdims=True), b_in=jnp.concatenate([dbin_lo, dbin_hi], axis=-1),
        b_conv=g_bconv, w_rg_a=g_wa[None], b_rg_a=g_ba, w_rg_x=g_wx[None], b_rg_x=g_bx, lru_lambda=g_lam,
        w_sp=g_wsp[None], b_sp=jnp.transpose(g_bsp_t)[None], ln_v_g=g_lvg, ln_v_b=g_lvb, ln1_g=dg1, ln1_b=db1,
        ln2_g=dg2, ln2_b=db2)
    gs_packed = _pack_small(g_small_local)
    rows = gs_packed.shape[0]
    parts = [dmod_b, g_win, _blocked_cols(g_wconv), g_wol.reshape(N_DEV, lw // N_DEV, D), g_wos,
             g_wout.reshape(N_DEV, D // N_DEV, D), g_wup, g_wdown.reshape(N_DEV, dff // N_DEV, D),
             gs_packed.reshape(N_DEV, rows // N_DEV, LANES)]
    (dmod_s, gwin_s, gwconv_s, gwol_s, gwos_s, gwout_s, gwup_s, gwdown_s, gsmall_s) = _exchange(parts, False,
                                                                                             "xchg_grads")

    out_g, out_d, out_m, out_v = {}, {}, {}, {}

    def adam(name, g_slots, tr):
        shp = W[name].shape
        w2, m2, v2 = [t.reshape(g_slots.shape[1:]) for t in (W[name], Mo[name], Vo[name])]
        g, d, mn, vn = _adamw(w2, g_slots, m2, v2, tr=tr, name="adam_" + name)
        out_g[name], out_d[name], out_m[name], out_v[name] = [t.reshape(shp) for t in (g, d, mn, vn)]

    g_wada = _ada_bwd(c_act, dmod_s.reshape(N_DEV * SUBLANES, -1))
    adam("w_ada", g_wada[None], 256)
    adam("w_in", gwin_s, 256)
    adam("w_conv", gwconv_s, 8)
    adam("w_o_lru", gwol_s, 160)
    adam("w_o_sgu", gwos_s, 256)
    adam("w_out", gwout_s, 128)
    adam("w_up", gwup_s, 256)
    adam("w_down", gwdown_s, 256)

    g_chunk = _sum_slots(gsmall_s, name="sum_small")
    (gsmall_all,) = _exchange([g_chunk], True, "xchg_small")
    gs, ds, ms, vs = _adamw(_pack_small(W), gsmall_all.reshape(1, rows, LANES), _pack_small(Mo), _pack_small(Vo),
                            tr=rows // N_DEV, name="adam_small")
    for dst, packed in ((out_g, gs), (out_d, ds), (out_m, ms), (out_v, vs)):
        dst.update(_unpack_small(packed, W))

    return (loss, grad_x, *[out_g[n] for n in WEIGHT_ORDER], *[out_d[n] for n in WEIGHT_ORDER],
            *[out_m[n] for n in WEIGHT_ORDER], *[out_v[n] for n in WEIGHT_ORDER])
```

```python
import functools
import math

import jax
import jax.numpy as jnp
from jax import lax
from jax.experimental import pallas as pl
from jax.experimental.pallas import tpu as pltpu

N_DEV = 8
LN_EPS = 1e-5
LRU_C = 8.0
CHUNK = 64
SGU_BLOCK = 128
ALPHA = 2.0 ** 0.25
ADAM_LR = 0.001
ADAM_B1 = 0.9
ADAM_B2 = 0.999
ADAM_EPS = 1e-08
ADAM_WD = 0.01
ADAM_STEP = 10
GELU_K0 = math.sqrt(2.0 / math.pi)
GELU_K1 = 0.044715

SUBLANES = 8
LANES = 128
VMEM_LIMIT = 56 * 1024 * 1024

F32 = jnp.float32
BF16 = jnp.bfloat16
MESH = pl.DeviceIdType.MESH


def _cparams(n_axes, big=False):
    return pltpu.CompilerParams(dimension_semantics=("arbitrary",) * n_axes,
                                vmem_limit_bytes=VMEM_LIMIT if big else None)


def _sigmoid(x):
    return 1.0 / (1.0 + jnp.exp(-x))


def _gelu(x):
    t = jnp.tanh(GELU_K0 * (x + GELU_K1 * (x * x * x)))
    return 0.5 * x * (1.0 + t)


def _gelu_and_grad(x):
    x2 = x * x
    t = jnp.tanh(GELU_K0 * (x + GELU_K1 * (x2 * x)))
    g = 0.5 * x * (1.0 + t)
    dg = 0.5 * (1.0 + t) + 0.5 * x * (1.0 - t * t) * (GELU_K0 * (1.0 + 3.0 * GELU_K1 * x2))
    return g, dg


def _expm1(x):
    p = x * (1.0 + x * (1.0 / 2.0) * (1.0 + x * (1.0 / 3.0) * (1.0 + x * (1.0 / 4.0) * (
        1.0 + x * (1.0 / 5.0) * (1.0 + x * (1.0 / 6.0) * (1.0 + x * (1.0 / 7.0)))))))
    return jnp.where(jnp.abs(x) < 0.3, p, jnp.exp(x) - 1.0)


def _log1p_pos(e):
    p = e * (1.0 - e * (1.0 / 2.0) + e * e * (1.0 / 3.0) - e * e * e * (1.0 / 4.0))
    return jnp.where(e < 1e-2, p, jnp.log(1.0 + e))


def _ln_stats(z):
    mu = jnp.mean(z, axis=-1, keepdims=True)
    zc = z - mu
    var = jnp.mean(zc * zc, axis=-1, keepdims=True)
    rstd = lax.rsqrt(var + LN_EPS)
    return zc * rstd, rstd


def _ln_bwd(dy, xhat, rstd, g):
    dxh = dy * g
    m1 = jnp.mean(dxh, axis=-1, keepdims=True)
    m2 = jnp.mean(dxh * xhat, axis=-1, keepdims=True)
    return rstd * (dxh - m1 - xhat * m2)


def _colsum(v):
    return jnp.sum(v, axis=0, keepdims=True)


def _first_step():
    return jnp.logical_and(pl.program_id(0) == 0, pl.program_id(1) == 0)


def _exchange(arrs, gather, name):
    n = len(arrs)
    n_peer = N_DEV - 1

    def body(*refs):
        ins, outs = refs[:n], refs[n:2 * n]
        send_sems, recv_sems, loc_sems = refs[2 * n:]
        x, y, c = lax.axis_index("x"), lax.axis_index("y"), lax.axis_index("c")
        me = 4 * x + 2 * y + c
        started = []
        for a in range(n):
            src_me = ins[a] if gather else ins[a].at[me]
            lc = pltpu.make_async_copy(src_me, outs[a].at[me], loc_sems.at[a])
            lc.start()
            started.append((lc, None))
        for p in range(1, N_DEV):
            px, py, pc = x ^ ((p >> 2) & 1), y ^ ((p >> 1) & 1), c ^ (p & 1)
            peer = 4 * px + 2 * py + pc
            for a in range(n):
                k = a * n_peer + (p - 1)
                src = ins[a] if gather else ins[a].at[peer]
                cp = pltpu.make_async_remote_copy(src_ref=src, dst_ref=outs[a].at[me],
                                                  send_sem=send_sems.at[k], recv_sem=recv_sems.at[k],
                                                  device_id=(px, py, pc), device_id_type=MESH)
                cp.start()
                rc = pltpu.make_async_remote_copy(src_ref=src, dst_ref=outs[a].at[peer],
                                                  send_sem=send_sems.at[k], recv_sem=recv_sems.at[k],
                                                  device_id=(px, py, pc), device_id_type=MESH)
                started.append((cp, rc))
        for cp, rc in started:
            if rc is None:
                cp.wait()
            else:
                cp.wait_send()
                rc.wait_recv()

    hbm = pl.BlockSpec(memory_space=pltpu.HBM)
    out_shape = tuple(
        jax.ShapeDtypeStruct(((N_DEV,) + a.shape) if gather else a.shape, a.dtype) for a in arrs)
    return pl.pallas_call(
        body, name=name, out_shape=out_shape,
        in_specs=[hbm] * n, out_specs=tuple([hbm] * n),
        scratch_shapes=[pltpu.SemaphoreType.DMA((n * n_peer,)), pltpu.SemaphoreType.DMA((n * n_peer,)),
                        pltpu.SemaphoreType.DMA((n,))],
        compiler_params=pltpu.CompilerParams(has_side_effects=True),
    )(*arrs)


_HBM = pl.BlockSpec(memory_space=pltpu.HBM)
_SEM = pl.BlockSpec(memory_space=pltpu.SEMAPHORE)
_EFFECT = pltpu.SideEffectType.DATAFLOW_SIDE_EFFECTING


def _peer_of(p):
    x, y, c = lax.axis_index("x"), lax.axis_index("y"), lax.axis_index("c")
    px, py, pc = x ^ ((p >> 2) & 1), y ^ ((p >> 1) & 1), c ^ (p & 1)
    return (px, py, pc), 4 * px + 2 * py + pc


def _landing(srcs, gather, name):
    n = len(srcs)

    def body(*refs):
        me = 4 * lax.axis_index("x") + 2 * lax.axis_index("y") + lax.axis_index("c")
        cps = [pltpu.make_async_copy(refs[a] if gather else refs[a].at[me], refs[n + a].at[me], refs[2 * n].at[a])
               for a in range(n)]
        for cp in cps:
            cp.start()
        for cp in cps:
            cp.wait()

    shapes = tuple(jax.ShapeDtypeStruct(((N_DEV,) + t.shape) if gather else t.shape, t.dtype) for t in srcs)
    return pl.pallas_call(body, name=name, out_shape=shapes, in_specs=[_HBM] * n, out_specs=tuple([_HBM] * n),
                          scratch_shapes=[pltpu.SemaphoreType.DMA((n,))])(*srcs)


def _xstart(srcs, lands, gather, name):
    n = len(srcs)

    def body(*refs):
        src_refs, land_refs = refs[:n], refs[n:2 * n]
        send_sems, recv_sems = refs[2 * n:3 * n], refs[3 * n:4 * n]
        token = refs[6 * n]
        me = 4 * lax.axis_index("x") + 2 * lax.axis_index("y") + lax.axis_index("c")
        for a in range(n):
            for p in range(1, N_DEV):
                dev, peer = _peer_of(p)
                pltpu.make_async_remote_copy(
                    src_ref=src_refs[a] if gather else src_refs[a].at[peer], dst_ref=land_refs[a].at[me],
                    send_sem=send_sems[a].at[p - 1], recv_sem=recv_sems[a].at[p - 1],
                    device_id=dev, device_id_type=MESH).start()
        token[...] = jnp.zeros_like(token)

    sems = tuple(pltpu.SemaphoreType.DMA((N_DEV - 1,)) for _ in range(2 * n))
    thru = tuple(pltpu.HBM(t.shape, t.dtype) for t in list(srcs) + list(lands))
    res = pl.pallas_call(
        body, name=name,
        out_shape=sems + thru + (jax.ShapeDtypeStruct((SUBLANES, LANES), F32),),
        in_specs=[_HBM] * (2 * n),
        out_specs=tuple([_SEM] * (2 * n) + [_HBM] * (2 * n) + [pl.BlockSpec(memory_space=pltpu.VMEM)]),
        input_output_aliases={i: 2 * n + i for i in range(2 * n)},
        compiler_params=pltpu.CompilerParams(has_side_effects=_EFFECT),
    )(*[pltpu.with_memory_space_constraint(t, pltpu.HBM) for t in list(srcs) + list(lands)])
    return res[:n], res[n:2 * n], res[2 * n:3 * n], res[3 * n:4 * n], res[4 * n]


def _xwait(src, land, send_sem, recv_sem, after, gather, name):
    def body(src_ref, land_ref, send_ref, recv_ref, after_ref, src_dead, land_out):
        del after_ref, src_dead, land_out
        for p in range(1, N_DEV):
            dev, peer = _peer_of(p)
            cp = pltpu.make_async_remote_copy(
                src_ref=src_ref if gather else src_ref.at[peer], dst_ref=land_ref.at[peer],
                send_sem=send_ref.at[p - 1], recv_sem=recv_ref.at[p - 1], device_id=dev, device_id_type=MESH)
            cp.wait_send()
            cp.wait_recv()

    return pl.pallas_call(
        body, name=name, out_shape=(pltpu.HBM(src.shape, src.dtype), pltpu.HBM(land.shape, land.dtype)),
        in_specs=[_HBM, _HBM, _SEM, _SEM, pl.BlockSpec(memory_space=pl.ANY)], out_specs=(_HBM, _HBM),
        input_output_aliases={0: 0, 1: 1},
        compiler_params=pltpu.CompilerParams(has_side_effects=_EFFECT),
    )(src, land, send_sem, recv_sem, after)[1]


def _mm(a, b, *, mode, tm, tn, tk, outs, epilogue=None, extras=(), out_blocked=False, tok=None, name):
    if mode == "nn":
        (M, K), (_, N) = a.shape, b.shape
    elif mode == "nt":
        (M, K), (N, _) = a.shape, b.shape
    else:
        (K, M), (_, N) = a.shape, b.shape
    tm, tn, tk = min(tm, M), min(tn, N), min(tk, K)
    assert M % tm == 0 and N % tn == 0 and K % tk == 0, (name, M, N, K, tm, tn, tk)
    if mode == "nn":
        a_spec = pl.BlockSpec((tm, tk), lambda i, j, k: (i, k))
        b_spec = pl.BlockSpec((tk, tn), lambda i, j, k: (k, j))
        dims = (((1,), (0,)), ((), ()))
    elif mode == "nt":
        a_spec = pl.BlockSpec((tm, tk), lambda i, j, k: (i, k))
        b_spec = pl.BlockSpec((tn, tk), lambda i, j, k: (j, k))
        dims = (((1,), (1,)), ((), ()))
    else:
        a_spec = pl.BlockSpec((tk, tm), lambda i, j, k: (k, i))
        b_spec = pl.BlockSpec((tk, tn), lambda i, j, k: (k, j))
        dims = (((0,), (0,)), ((), ()))
    nk = K // tk
    n_ex, n_out = len(extras), len(outs)
    n_tok = 0 if tok is None else 1
    if epilogue is None:
        epilogue = lambda acc, ex: tuple(acc.astype(d) for d in outs)

    def body(a_ref, b_ref, *refs):
        refs = refs[n_tok:]
        ex_refs, out_refs = refs[:n_ex], refs[n_ex:n_ex + n_out]

        def finish(acc):
            res = epilogue(acc, [r[...] for r in ex_refs])
            for o_ref, v in zip(out_refs, res):
                o_ref[...] = v.astype(o_ref.dtype)

        part = lax.dot_general(a_ref[...], b_ref[...], dims, preferred_element_type=F32)
        if nk == 1:
            finish(part)
        else:
            acc_ref = refs[n_ex + n_out]
            k = pl.program_id(2)

            @pl.when(k == 0)
            def _():
                acc_ref[...] = part

            @pl.when(k > 0)
            def _():
                acc_ref[...] += part

            @pl.when(k == nk - 1)
            def _():
                finish(acc_ref[...])

    ex_specs = [pl.BlockSpec((tm, tn), lambda i, j, k: (i, j)) if kind == "tile"
                else pl.BlockSpec((1, tn), lambda i, j, k: (0, j)) for _, kind in extras]
    if out_blocked:
        o_spec = pl.BlockSpec((None, tm, tn), lambda i, j, k: (j, i, 0))
        o_shape = (N // tn, M, tn)
    else:
        o_spec = pl.BlockSpec((tm, tn), lambda i, j, k: (i, j))
        o_shape = (M, N)
    res = pl.pallas_call(
        body, name=name, grid=(M // tm, N // tn, nk),
        in_specs=[a_spec, b_spec] + [pl.BlockSpec((SUBLANES, LANES), lambda i, j, k: (0, 0))] * n_tok + ex_specs,
        out_specs=tuple([o_spec] * n_out),
        out_shape=tuple(jax.ShapeDtypeStruct(o_shape, d) for d in outs),
        scratch_shapes=[pltpu.VMEM((tm, tn), F32)] if nk > 1 else [],
        compiler_params=_cparams(3, big=True),
    )(a, b, *([tok] if n_tok else []), *[e for e, _ in extras])
    return res[0] if n_out == 1 else res


def _tok_spec(ts, width, col_block=0):
    return pl.BlockSpec((None, ts, width), lambda b, s: (b, s, col_block))


def _brow_spec(width):
    return pl.BlockSpec((None, 1, width), lambda b, s: (b, 0, 0))


def _vec_spec(width):
    return pl.BlockSpec((1, width), lambda b, s: (0, 0))


def _modulate(x, sc, sh, ts):
    Bl, S, D = x.shape

    def body(x_ref, sc_ref, sh_ref, o_ref):
        o_ref[...] = (x_ref[...] * (1.0 + sc_ref[...]) + sh_ref[...]).astype(BF16)

    return pl.pallas_call(
        body, name="modulate", grid=(Bl, S // ts),
        in_specs=[_tok_spec(ts, D), _brow_spec(D), _brow_spec(D)],
        out_specs=_tok_spec(ts, D), out_shape=jax.ShapeDtypeStruct((Bl, S, D), BF16),
        compiler_params=_cparams(2),
    )(x, sc, sh)


def _mix_fwd(proj, w_conv, b_conv, w_rg_a, b_rg_a, w_rg_x, b_rg_x, lam, w_sp, b_sp_t, ln_v_g, ln_v_b, *, tm, lw, sw):
    Bl, S, _ = proj.shape
    heads, hd = w_rg_a.shape[0], w_rg_a.shape[1]
    groups = w_sp.shape[0]
    cw = 2 * lw + 2 * sw
    nblk = tm // SGU_BLOCK

    def body(p_ref, wc_ref, bc_ref, wa_ref, ba_ref, wx_ref, bx_ref, lam_ref, wsp_ref, bsp_ref, lg_ref, lb_ref,
             hs_ref, ya_ref, ys_ref, xprev, hcar, a_scr, b_scr):
        s = pl.program_id(1)

        @pl.when(s == 0)
        def _():
            xprev[...] = jnp.zeros_like(xprev)
            hcar[...] = jnp.zeros_like(hcar)

        xl = p_ref[:, 0:lw]
        gl = p_ref[:, lw:2 * lw]
        row8 = lax.broadcasted_iota(jnp.int32, (SUBLANES, lw), 0)
        rowm = lax.broadcasted_iota(jnp.int32, (tm, lw), 0) & (SUBLANES - 1)

        prev = xprev[...]
        xc = xl * wc_ref[3:4, :] + bc_ref[...]
        for k in (1, 2, 3):
            xr = pltpu.roll(xl, k, 0)
            head = jnp.where(row8 < k, pltpu.roll(prev, k, 0), xr[0:SUBLANES])
            xs = jnp.concatenate([head, xr[SUBLANES:]], axis=0)
            xc = xc + xs * wc_ref[3 - k:4 - k, :]
        xprev[...] = xl[tm - SUBLANES:tm]

        xcb = xc.astype(BF16)
        pa = jnp.concatenate([jnp.dot(xcb[:, h * hd:(h + 1) * hd], wa_ref[h], preferred_element_type=F32)
                              for h in range(heads)], axis=1) + ba_ref[...]
        px = jnp.concatenate([jnp.dot(xcb[:, h * hd:(h + 1) * hd], wx_ref[h], preferred_element_type=F32)
                              for h in range(heads)], axis=1) + bx_ref[...]
        r = _sigmoid(pa)
        ig = _sigmoid(px)
        nl = -lam_ref[...]
        big_l = -LRU_C * (jnp.maximum(nl, 0.0) + _log1p_pos(jnp.exp(-jnp.abs(nl))))
        la = big_l * r
        a = jnp.exp(la)
        bin_ = jnp.sqrt(-_expm1(2.0 * la)) * (ig * xc)

        for d in (1, 2, 4):
            a_sh = pltpu.roll(a, d, 0)
            b_sh = pltpu.roll(bin_, d, 0)
            msk = rowm >= d
            bin_ = jnp.where(msk, a * b_sh + bin_, bin_)
            a = jnp.where(msk, a * a_sh, a)
        a_scr[...] = a
        b_scr[...] = bin_

        def grp(g, carry):
            off = pl.multiple_of(g * SUBLANES, SUBLANES)
            h = b_scr[pl.ds(off, SUBLANES), :] + a_scr[pl.ds(off, SUBLANES), :] * carry
            hs_ref[pl.ds(off, SUBLANES), :] = h
            return jnp.broadcast_to(h[SUBLANES - 1:SUBLANES, :], h.shape)

        hcar[...] = lax.fori_loop(0, tm // SUBLANES, grp, hcar[...])
        ya_ref[...] = (hs_ref[...] * _gelu(gl)).astype(BF16)

        gu = _gelu(p_ref[:, 2 * lw:2 * lw + sw])
        gv = _gelu(p_ref[:, 2 * lw + sw:cw])
        xhat, _ = _ln_stats(gv)
        vn = (xhat * lg_ref[...] + lb_ref[...]).astype(BF16)
        tpos = lax.broadcasted_iota(jnp.int32, (SGU_BLOCK, SGU_BLOCK), 0) // CHUNK
        spos = lax.broadcasted_iota(jnp.int32, (SGU_BLOCK, SGU_BLOCK), 1) // CHUNK
        gw = sw // groups
        rows_out = []
        for blk in range(nblk):
            r0 = blk * SGU_BLOCK
            cols = []
            for g in range(groups):
                wm = jnp.where(spos <= tpos, wsp_ref[g], 0.0).astype(BF16)
                mixed = jnp.dot(wm, vn[r0:r0 + SGU_BLOCK, g * gw:(g + 1) * gw], preferred_element_type=F32)
                cols.append(mixed + bsp_ref[:, g:g + 1])
            rows_out.append(jnp.concatenate(cols, axis=1))
        mixed_all = jnp.concatenate(rows_out, axis=0) if nblk > 1 else rows_out[0]
        ys_ref[...] = (gu * mixed_all).astype(BF16)

    full = lambda shp: pl.BlockSpec(shp, lambda b, s: (0,) * len(shp))
    return pl.pallas_call(
        body, name="mix_fwd", grid=(Bl, S // tm),
        in_specs=[_tok_spec(tm, cw), full(w_conv.shape), full(b_conv.shape), full(w_rg_a.shape), full(b_rg_a.shape),
                  full(w_rg_x.shape), full(b_rg_x.shape), full(lam.shape), full(w_sp.shape), full(b_sp_t.shape),
                  full(ln_v_g.shape), full(ln_v_b.shape)],
        out_specs=(_tok_spec(tm, lw), _tok_spec(tm, lw), _tok_spec(tm, sw)),
        out_shape=(jax.ShapeDtypeStruct((Bl, S, lw), F32), jax.ShapeDtypeStruct((Bl, S, lw), BF16),
                   jax.ShapeDtypeStruct((Bl, S, sw), BF16)),
        scratch_shapes=[pltpu.VMEM((SUBLANES, lw), F32), pltpu.VMEM((SUBLANES, lw), F32),
                        pltpu.VMEM((tm, lw), F32), pltpu.VMEM((tm, lw), F32)],
        compiler_params=_cparams(2, big=True),
    )(proj, w_conv, b_conv, w_rg_a, b_rg_a, w_rg_x, b_rg_x, lam, w_sp, b_sp_t, ln_v_g, ln_v_b)


def _merge_fwd(proj, y_a, y_b, *, ts, d):
    Bl, S, din = proj.shape
    gcol = (din - 2 * d) // (2 * d)
    assert gcol * 2 * d == din - 2 * d

    def body(g_ref, ya_ref, yb_ref, o_ref):
        o_ref[...] = (_sigmoid(g_ref[:, 0:d]) * ya_ref[...] + _sigmoid(g_ref[:, d:2 * d]) * yb_ref[...]).astype(BF16)

    return pl.pallas_call(
        body, name="merge_fwd", grid=(Bl, S // ts),
        in_specs=[_tok_spec(ts, 2 * d, gcol), _tok_spec(ts, d), _tok_spec(ts, d)],
        out_specs=_tok_spec(ts, d), out_shape=jax.ShapeDtypeStruct((Bl, S, d), BF16),
        compiler_params=_cparams(2),
    )(proj, y_a, y_b)


def _ln1_fwd(x, mix, gt1, g1, b1, sc2, sh2, *, ts):
    Bl, S, D = x.shape

    def body(x_ref, mix_ref, gt_ref, g_ref, b_ref, sc_ref, sh_ref, x1_ref, h2_ref):
        z = ALPHA * x_ref[...] + (1.0 + gt_ref[...]) * mix_ref[...]
        xhat, _ = _ln_stats(z)
        x1 = xhat * g_ref[...] + b_ref[...]
        x1_ref[...] = x1
        h2_ref[...] = (x1 * (1.0 + sc_ref[...]) + sh_ref[...]).astype(BF16)

    return pl.pallas_call(
        body, name="ln1_fwd", grid=(Bl, S // ts),
        in_specs=[_tok_spec(ts, D), _tok_spec(ts, D), _brow_spec(D), _vec_spec(D), _vec_spec(D), _brow_spec(D),
                  _brow_spec(D)],
        out_specs=(_tok_spec(ts, D), _tok_spec(ts, D)),
        out_shape=(jax.ShapeDtypeStruct((Bl, S, D), F32), jax.ShapeDtypeStruct((Bl, S, D), BF16)),
        compiler_params=_cparams(2),
    )(x, mix, gt1, g1, b1, sc2, sh2)


def _ln2_loss(x1, f, tgt, gt2, g2, b2, *, ts):
    Bl, S, D = x1.shape

    def body(x1_ref, f_ref, t_ref, gt_ref, g_ref, b_ref, df_ref, dx1_ref, dgt_ref, dg_ref, db_ref, loss_ref):
        s = pl.program_id(1)

        @pl.when(_first_step())
        def _():
            dg_ref[...] = jnp.zeros_like(dg_ref)
            db_ref[...] = jnp.zeros_like(db_ref)
            loss_ref[...] = jnp.zeros_like(loss_ref)

        @pl.when(s == 0)
        def _():
            dgt_ref[...] = jnp.zeros_like(dgt_ref)

        fv = f_ref[...]
        z = ALPHA * x1_ref[...] + (1.0 + gt_ref[...]) * fv
        xhat, rstd = _ln_stats(z)
        x2 = xhat * g_ref[...] + b_ref[...]
        err = x2 - t_ref[...]
        loss_ref[...] += 0.5 * jnp.sum(jnp.mean(err * err, axis=-1, keepdims=True))
        dy = err * (1.0 / D)
        dg_ref[...] += _colsum(dy * xhat)
        db_ref[...] += _colsum(dy)
        dz = _ln_bwd(dy, xhat, rstd, g_ref[...])
        dx1_ref[...] = ALPHA * dz
        dgt_ref[...] += _colsum(dz * fv)
        df_ref[...] = (dz * (1.0 + gt_ref[...])).astype(BF16)

    return pl.pallas_call(
        body, name="ln2_loss", grid=(Bl, S // ts),
        in_specs=[_tok_spec(ts, D), _tok_spec(ts, D), _tok_spec(ts, D), _brow_spec(D), _vec_spec(D), _vec_spec(D)],
        out_specs=(_tok_spec(ts, D), _tok_spec(ts, D), _brow_spec(D), _vec_spec(D), _vec_spec(D),
                   pl.BlockSpec((SUBLANES, LANES), lambda b, s: (0, 0))),
        out_shape=(jax.ShapeDtypeStruct((Bl, S, D), BF16), jax.ShapeDtypeStruct((Bl, S, D), F32),
                   jax.ShapeDtypeStruct((Bl, 1, D), F32), jax.ShapeDtypeStruct((1, D), F32),
                   jax.ShapeDtypeStruct((1, D), F32), jax.ShapeDtypeStruct((SUBLANES, LANES), F32)),
        compiler_params=_cparams(2),
    )(x1, f, tgt, gt2, g2, b2)


def _ln1_bwd(dx1p, dh2, x1, x, mix, sc2, gt1, g1, *, ts):
    Bl, S, D = x.shape

    def body(dx1p_ref, dh2_ref, x1_ref, x_ref, mix_ref, sc_ref, gt_ref, g_ref,
             dxp_ref, dmix_ref, dsc_ref, dsh_ref, dgt_ref, dg_ref, db_ref):
        s = pl.program_id(1)

        @pl.when(_first_step())
        def _():
            dg_ref[...] = jnp.zeros_like(dg_ref)
            db_ref[...] = jnp.zeros_like(db_ref)

        @pl.when(s == 0)
        def _():
            dsc_ref[...] = jnp.zeros_like(dsc_ref)
            dsh_ref[...] = jnp.zeros_like(dsh_ref)
            dgt_ref[...] = jnp.zeros_like(dgt_ref)

        dh2 = dh2_ref[...]
        mixv = mix_ref[...]
        dsc_ref[...] += _colsum(dh2 * x1_ref[...])
        dsh_ref[...] += _colsum(dh2)
        dx1 = dx1p_ref[...] + dh2 * (1.0 + sc_ref[...])
        z = ALPHA * x_ref[...] + (1.0 + gt_ref[...]) * mixv
        xhat, rstd = _ln_stats(z)
        dg_ref[...] += _colsum(dx1 * xhat)
        db_ref[...] += _colsum(dx1)
        dz = _ln_bwd(dx1, xhat, rstd, g_ref[...])
        dxp_ref[...] = ALPHA * dz
        dgt_ref[...] += _colsum(dz * mixv)
        dmix_ref[...] = (dz * (1.0 + gt_ref[...])).astype(BF16)

    return pl.pallas_call(
        body, name="ln1_bwd", grid=(Bl, S // ts),
        in_specs=[_tok_spec(ts, D)] * 5 + [_brow_spec(D), _brow_spec(D), _vec_spec(D)],
        out_specs=(_tok_spec(ts, D), _tok_spec(ts, D), _brow_spec(D), _brow_spec(D), _brow_spec(D), _vec_spec(D),
                   _vec_spec(D)),
        out_shape=(jax.ShapeDtypeStruct((Bl, S, D), F32), jax.ShapeDtypeStruct((Bl, S, D), BF16),
                   jax.ShapeDtypeStruct((Bl, 1, D), F32), jax.ShapeDtypeStruct((Bl, 1, D), F32),
                   jax.ShapeDtypeStruct((Bl, 1, D), F32), jax.ShapeDtypeStruct((1, D), F32),
                   jax.ShapeDtypeStruct((1, D), F32)),
        compiler_params=_cparams(2),
    )(dx1p, dh2, x1, x, mix, sc2, gt1, g1)


def _merge_bwd(dmerged, y_a, y_b, proj, *, ts, d):
    Bl, S, din = proj.shape
    gcol = (din - 2 * d) // (2 * d)

    def body(dm_ref, ya_ref, yb_ref, g_ref, dya_ref, dyb_ref, dp_ref, db_ref):
        @pl.when(_first_step())
        def _():
            db_ref[...] = jnp.zeros_like(db_ref)

        dm = dm_ref[...]
        sa = _sigmoid(g_ref[:, 0:d])
        sb = _sigmoid(g_ref[:, d:2 * d])
        dya_ref[...] = (dm * sa).astype(BF16)
        dyb_ref[...] = (dm * sb).astype(BF16)
        dga = dm * ya_ref[...] * sa * (1.0 - sa)
        dgb = dm * yb_ref[...] * sb * (1.0 - sb)
        dp_ref[:, 0:d] = dga.astype(BF16)
        dp_ref[:, d:2 * d] = dgb.astype(BF16)
        db_ref[:, 0:d] += _colsum(dga)
        db_ref[:, d:2 * d] += _colsum(dgb)

    return pl.pallas_call(
        body, name="merge_bwd", grid=(Bl, S // ts),
        in_specs=[_tok_spec(ts, d), _tok_spec(ts, d), _tok_spec(ts, d), _tok_spec(ts, 2 * d, gcol)],
        out_specs=(_tok_spec(ts, d), _tok_spec(ts, d), _tok_spec(ts, 2 * d, gcol), _vec_spec(2 * d)),
        out_shape=(jax.ShapeDtypeStruct((Bl, S, d), BF16), jax.ShapeDtypeStruct((Bl, S, d), BF16),
                   jax.ShapeDtypeStruct((Bl, S, din), BF16), jax.ShapeDtypeStruct((1, 2 * d), F32)),
        compiler_params=_cparams(2),
    )(dmerged, y_a, y_b, proj)


def _mix_bwd(proj, hs, dya, dys, dproj, w_conv, b_conv, w_rg_a, b_rg_a, w_rg_x, b_rg_x, lam, w_sp, b_sp_t,
             ln_v_g, ln_v_b, *, tm, lw, sw):
    Bl, S, din = proj.shape
    heads, hd = w_rg_a.shape[0], w_rg_a.shape[1]
    groups = w_sp.shape[0]
    gw = sw // groups
    cw = 2 * lw + 2 * sw
    nblk = tm // SGU_BLOCK
    n_s = S // tm
    per8 = tm // SUBLANES

    def body(p_ref, xh_ref, hs_ref, hh_ref, dya_ref, dys_ref, dpin_ref,
             wc_ref, bc_ref, wa_ref, ba_ref, wx_ref, bx_ref, lam_ref, wsp_ref, bsp_ref, lg_ref, lb_ref,
             dp_ref, dbin_ref, dwc_ref, dbc_ref, dwa_ref, dba_ref, dwx_ref, dbx_ref, dlam_ref, dwsp_ref, dbsp_ref,
             dlg_ref, dlb_ref,
             dhcar, acar, dxcn, a_scr, b_scr, g_scr):
        del dpin_ref
        sr = pl.program_id(1)
        first_tile = sr == n_s - 1

        @pl.when(_first_step())
        def _():
            for ref in (dbin_ref, dwc_ref, dbc_ref, dwa_ref, dba_ref, dwx_ref, dbx_ref, dlam_ref, dwsp_ref, dbsp_ref,
                        dlg_ref, dlb_ref):
                ref[...] = jnp.zeros_like(ref)

        @pl.when(sr == 0)
        def _():
            dhcar[...] = jnp.zeros_like(dhcar)
            acar[...] = jnp.zeros_like(acar)
            dxcn[...] = jnp.zeros_like(dxcn)

        keep = jnp.where(first_tile, 0.0, 1.0)
        xl = p_ref[:, 0:lw]
        gl = p_ref[:, lw:2 * lw]
        row8 = lax.broadcasted_iota(jnp.int32, (SUBLANES, lw), 0)
        rowm = lax.broadcasted_iota(jnp.int32, (tm, lw), 0) & (SUBLANES - 1)

        prev = xh_ref[...] * keep
        xsh = [xl]
        for k in (1, 2, 3):
            xr = pltpu.roll(xl, k, 0)
            head = jnp.where(row8 < k, pltpu.roll(prev, k, 0), xr[0:SUBLANES])
            xsh.append(jnp.concatenate([head, xr[SUBLANES:]], axis=0))
        xc = bc_ref[...] + xsh[0] * wc_ref[3:4, :]
        for k in (1, 2, 3):
            xc = xc + xsh[k] * wc_ref[3 - k:4 - k, :]
        xcb = xc.astype(BF16)
        pa = jnp.concatenate([jnp.dot(xcb[:, h * hd:(h + 1) * hd], wa_ref[h], preferred_element_type=F32)
                              for h in range(heads)], axis=1) + ba_ref[...]
        px = jnp.concatenate([jnp.dot(xcb[:, h * hd:(h + 1) * hd], wx_ref[h], preferred_element_type=F32)
                              for h in range(heads)], axis=1) + bx_ref[...]
        r = _sigmoid(pa)
        ig = _sigmoid(px)
        nl = -lam_ref[...]
        big_l = -LRU_C * (jnp.maximum(nl, 0.0) + _log1p_pos(jnp.exp(-jnp.abs(nl))))
        la = big_l * r
        a = jnp.exp(la)
        msq = -_expm1(2.0 * la)
        m = jnp.sqrt(msq)
        hsv = hs_ref[...]
        ggl, dggl = _gelu_and_grad(gl)

        dyav = dya_ref[...]
        dhs = dyav * ggl
        dp_ref[:, lw:2 * lw] = (dyav * hsv * dggl).astype(BF16)
        dbin_ref[:, lw:2 * lw] += _colsum(dyav * hsv * dggl)
        a_up = pltpu.roll(a, tm - 1, 0)
        tail = jnp.where(row8 == SUBLANES - 1, acar[...], a_up[tm - SUBLANES:tm])
        an = jnp.concatenate([a_up[:tm - SUBLANES], tail], axis=0)
        acar[...] = jnp.broadcast_to(a[0:1, :], (SUBLANES, lw))
        bb = dhs
        for d in (1, 2, 4):
            a_sh = pltpu.roll(an, tm - d, 0)
            b_sh = pltpu.roll(bb, tm - d, 0)
            msk = rowm < SUBLANES - d
            bb = jnp.where(msk, an * b_sh + bb, bb)
            an = jnp.where(msk, an * a_sh, an)
        a_scr[...] = an
        b_scr[...] = bb

        def grp(i, carry):
            off = pl.multiple_of((per8 - 1 - i) * SUBLANES, SUBLANES)
            g = b_scr[pl.ds(off, SUBLANES), :] + a_scr[pl.ds(off, SUBLANES), :] * carry
            g_scr[pl.ds(off, SUBLANES), :] = g
            return jnp.broadcast_to(g[0:1, :], g.shape)

        dhcar[...] = lax.fori_loop(0, per8, grp, dhcar[...])
        dh = g_scr[...]

        hr = pltpu.roll(hsv, 1, 0)
        hhead = jnp.where(row8 < 1, pltpu.roll(hh_ref[...] * keep, 1, 0), hr[0:SUBLANES])
        hprev = jnp.concatenate([hhead, hr[SUBLANES:]], axis=0)
        da = dh * hprev
        ixc = ig * xc
        dm = dh * ixc
        dixc = dh * m
        di = dixc * xc
        dxc = dixc * ig
        dla = da * a - dm * (a * a) / m
        dlam_ref[...] += _colsum(dla * r) * (LRU_C * _sigmoid(nl))
        dr = dla * big_l
        dpa = dr * r * (1.0 - r)
        dpx = di * ig * (1.0 - ig)
        dba_ref[...] += _colsum(dpa)
        dbx_ref[...] += _colsum(dpx)
        dpab = dpa.astype(BF16)
        dpxb = dpx.astype(BF16)
        nt = (((1,), (1,)), ((), ()))
        tn = (((0,), (0,)), ((), ()))
        dxc_g = []
        for h in range(heads):
            sl = slice(h * hd, (h + 1) * hd)
            dxc_g.append(lax.dot_general(dpab[:, sl], wa_ref[h], nt, preferred_element_type=F32)
                         + lax.dot_general(dpxb[:, sl], wx_ref[h], nt, preferred_element_type=F32))
            dwa_ref[h] += lax.dot_general(xcb[:, sl], dpab[:, sl], tn, preferred_element_type=F32)
            dwx_ref[h] += lax.dot_general(xcb[:, sl], dpxb[:, sl], tn, preferred_element_type=F32)
        dxc = dxc + jnp.concatenate(dxc_g, axis=1)

        dbc_ref[...] += _colsum(dxc)
        for k in range(4):
            dwc_ref[k:k + 1, :] += _colsum(dxc * xsh[3 - k])
        nxt = dxcn[...]
        dxl = dxc * wc_ref[3:4, :]
        for k in (1, 2, 3):
            ur = pltpu.roll(dxc, tm - k, 0)
            tl = jnp.where(row8 >= SUBLANES - k, pltpu.roll(nxt, SUBLANES - k, 0), ur[tm - SUBLANES:tm])
            dxl = dxl + jnp.concatenate([ur[:tm - SUBLANES], tl], axis=0) * wc_ref[3 - k:4 - k, :]
        dxcn[...] = dxc[0:SUBLANES]
        dp_ref[:, 0:lw] = dxl.astype(BF16)
        dbin_ref[:, 0:lw] += _colsum(dxl)

        gu, dgu_dx = _gelu_and_grad(p_ref[:, 2 * lw:2 * lw + sw])
        gv, dgv_dx = _gelu_and_grad(p_ref[:, 2 * lw + sw:cw])
        xhat, rstd = _ln_stats(gv)
        vn = (xhat * lg_ref[...] + lb_ref[...]).astype(BF16)
        dys = dys_ref[...]
        dmixed = dys * gu
        dmb = dmixed.astype(BF16)
        tpos = lax.broadcasted_iota(jnp.int32, (SGU_BLOCK, SGU_BLOCK), 0) // CHUNK
        spos = lax.broadcasted_iota(jnp.int32, (SGU_BLOCK, SGU_BLOCK), 1) // CHUNK
        causal = spos <= tpos
        mixed_rows, dvn_rows = [], []
        for blk in range(nblk):
            rs = slice(blk * SGU_BLOCK, (blk + 1) * SGU_BLOCK)
            mcols, dcols = [], []
            for g in range(groups):
                cs = slice(g * gw, (g + 1) * gw)
                wm = jnp.where(causal, wsp_ref[g], 0.0).astype(BF16)
                mcols.append(jnp.dot(wm, vn[rs, cs], preferred_element_type=F32) + bsp_ref[:, g:g + 1])
                dcols.append(lax.dot_general(wm, dmb[rs, cs], tn, preferred_element_type=F32))
                dw = lax.dot_general(dmb[rs, cs], vn[rs, cs], nt, preferred_element_type=F32)
                dwsp_ref[g] += jnp.where(causal, dw, 0.0)
                dbsp_ref[:, g:g + 1] += jnp.sum(dmixed[rs, cs], axis=1, keepdims=True)
            mixed_rows.append(jnp.concatenate(mcols, axis=1))
            dvn_rows.append(jnp.concatenate(dcols, axis=1))
        mixed_all = jnp.concatenate(mixed_rows, axis=0) if nblk > 1 else mixed_rows[0]
        dvn = jnp.concatenate(dvn_rows, axis=0) if nblk > 1 else dvn_rows[0]
        du = dys * mixed_all * dgu_dx
        dlg_ref[...] += _colsum(dvn * xhat)
        dlb_ref[...] += _colsum(dvn)
        dv = _ln_bwd(dvn, xhat, rstd, lg_ref[...]) * dgv_dx
        dp_ref[:, 2 * lw:2 * lw + sw] = du.astype(BF16)
        dp_ref[:, 2 * lw + sw:cw] = dv.astype(BF16)
        dbin_ref[:, 2 * lw:2 * lw + sw] += _colsum(du)
        dbin_ref[:, 2 * lw + sw:cw] += _colsum(dv)

    rev = lambda s: n_s - 1 - s
    tile = lambda w: pl.BlockSpec((None, tm, w), lambda b, s: (b, rev(s), 0))
    halo = lambda w: pl.BlockSpec((None, SUBLANES, w), lambda b, s: (b, jnp.maximum(rev(s) * per8 - 1, 0), 0))
    full = lambda shp: pl.BlockSpec(shp, lambda b, s: (0,) * len(shp))
    small = [w_conv, b_conv, w_rg_a, b_rg_a, w_rg_x, b_rg_x, lam, w_sp, b_sp_t, ln_v_g, ln_v_b]
    acc_shapes = [(1, cw), w_conv.shape, b_conv.shape, w_rg_a.shape, b_rg_a.shape, w_rg_x.shape, b_rg_x.shape,
                  lam.shape, w_sp.shape, b_sp_t.shape, ln_v_g.shape, ln_v_b.shape]
    res = pl.pallas_call(
        body, name="mix_bwd", grid=(Bl, n_s),
        in_specs=[tile(cw), halo(lw), tile(lw), halo(lw), tile(lw), tile(sw), pl.BlockSpec(memory_space=pl.ANY)]
                 + [full(w.shape) for w in small],
        out_specs=tuple([tile(cw)] + [full(shp) for shp in acc_shapes]),
        out_shape=tuple([jax.ShapeDtypeStruct((Bl, S, din), BF16)] + [jax.ShapeDtypeStruct(shp, F32) for shp in acc_shapes]),
        input_output_aliases={6: 0},
        scratch_shapes=[pltpu.VMEM((SUBLANES, lw), F32), pltpu.VMEM((SUBLANES, lw), F32), pltpu.VMEM((SUBLANES, lw), F32),
                        pltpu.VMEM((tm, lw), F32), pltpu.VMEM((tm, lw), F32), pltpu.VMEM((tm, lw), F32)],
        compiler_params=_cparams(2, big=True),
    )(proj, proj, hs, hs, dya, dys, dproj, *small)
    return res


def _final_dx(dxp, dh, x, sc1, *, ts):
    Bl, S, D = x.shape

    def body(dxp_ref, dh_ref, x_ref, sc_ref, dx_ref, dsc_ref, dsh_ref):
        @pl.when(pl.program_id(1) == 0)
        def _():
            dsc_ref[...] = jnp.zeros_like(dsc_ref)
            dsh_ref[...] = jnp.zeros_like(dsh_ref)

        dh = dh_ref[...]
        dx_ref[...] = dxp_ref[...] + dh * (1.0 + sc_ref[...])
        dsc_ref[...] += _colsum(dh * x_ref[...])
        dsh_ref[...] += _colsum(dh)

    return pl.pallas_call(
        body, name="final_dx", grid=(Bl, S // ts),
        in_specs=[_tok_spec(ts, D), _tok_spec(ts, D), _tok_spec(ts, D), _brow_spec(D)],
        out_specs=(_tok_spec(ts, D), _brow_spec(D), _brow_spec(D)),
        out_shape=(jax.ShapeDtypeStruct((Bl, S, D), F32), jax.ShapeDtypeStruct((Bl, 1, D), F32),
                   jax.ShapeDtypeStruct((Bl, 1, D), F32)),
        compiler_params=_cparams(2),
    )(dxp, dh, x, sc1)


def _ada_fwd(c_all, w_ada):
    R, D = c_all.shape
    nb = w_ada.shape[1]

    def body(c_ref, w_ref, act_ref, o_ref):
        cv = c_ref[...]
        act = (cv * _sigmoid(cv)).astype(BF16)
        act_ref[...] = act
        o_ref[...] = jnp.dot(act, w_ref[...].astype(BF16), preferred_element_type=F32)

    return pl.pallas_call(
        body, name="ada_fwd",
        out_shape=(jax.ShapeDtypeStruct((R, D), BF16), jax.ShapeDtypeStruct((R, nb), F32)),
        compiler_params=pltpu.CompilerParams(vmem_limit_bytes=VMEM_LIMIT),
    )(c_all, w_ada)


def _ada_bwd(c_act, dmod_cols):
    R, D = c_act.shape
    nb = dmod_cols.shape[1]

    def body(act_ref, d_ref, o_ref):
        o_ref[...] = lax.dot_general(act_ref[...], d_ref[...].astype(BF16), (((0,), (0,)), ((), ())),
                                     preferred_element_type=F32)

    return pl.pallas_call(
        body, name="ada_bwd", out_shape=jax.ShapeDtypeStruct((D, nb), F32),
        compiler_params=pltpu.CompilerParams(vmem_limit_bytes=VMEM_LIMIT),
    )(c_act, dmod_cols)


def _adamw(w, g_slots, m, v, *, tr, name):
    R, C = w.shape
    n_slot = g_slots.shape[0]
    tr = min(tr, R)
    assert R % tr == 0, (name, R, tr)
    c1 = 1.0 / (1.0 - ADAM_B1 ** ADAM_STEP)
    c2 = 1.0 / (1.0 - ADAM_B2 ** ADAM_STEP)

    def body(w_ref, g_ref, m_ref, v_ref, go_ref, d_ref, mo_ref, vo_ref):
        g = g_ref[0].astype(F32)
        for i in range(1, n_slot):
            g = g + g_ref[i].astype(F32)
        mn = ADAM_B1 * m_ref[...] + (1.0 - ADAM_B1) * g
        vn = ADAM_B2 * v_ref[...] + (1.0 - ADAM_B2) * (g * g)
        go_ref[...] = g
        mo_ref[...] = mn
        vo_ref[...] = vn
        d_ref[...] = -ADAM_LR * ((mn * c1) / (jnp.sqrt(vn * c2) + ADAM_EPS) + ADAM_WD * w_ref[...])

    blk = pl.BlockSpec((tr, C), lambda i: (i, 0))
    return pl.pallas_call(
        body, name=name, grid=(R // tr,),
        in_specs=[blk, pl.BlockSpec((n_slot, tr, C), lambda i: (0, i, 0)), blk, blk],
        out_specs=(blk, blk, blk, blk),
        out_shape=tuple(jax.ShapeDtypeStruct((R, C), F32) for _ in range(4)),
        compiler_params=_cparams(1, big=True),
    )(w, g_slots, m, v)


def _sum_slots(g_slots, *, name):
    n_slot, R, C = g_slots.shape

    def body(g_ref, o_ref):
        g = g_ref[0]
        for i in range(1, n_slot):
            g = g + g_ref[i]
        o_ref[...] = g

    return pl.pallas_call(body, name=name, out_shape=jax.ShapeDtypeStruct((R, C), F32),
                          compiler_params=pltpu.CompilerParams(vmem_limit_bytes=VMEM_LIMIT))(g_slots)


SMALL_NAMES = ("b_ada", "b_in", "b_conv", "w_rg_a", "b_rg_a", "w_rg_x", "b_rg_x", "lru_lambda", "w_sp", "b_sp",
               "ln_v_g", "ln_v_b", "ln1_g", "ln1_b", "ln2_g", "ln2_b")
BIG_NAMES = ("w_ada", "w_in", "w_conv", "w_o_lru", "w_o_sgu", "w_out", "w_up", "w_down")
WEIGHT_ORDER = ("w_ada", "b_ada", "w_in", "b_in", "w_conv", "b_conv", "w_rg_a", "b_rg_a", "w_rg_x", "b_rg_x",
                "lru_lambda", "w_sp", "b_sp", "ln_v_g", "ln_v_b", "w_o_lru", "w_o_sgu", "w_out", "ln1_g", "ln1_b",
                "w_up", "w_down", "ln2_g", "ln2_b")


def _pack_small(d):
    flat = jnp.concatenate([d[n].reshape(-1) for n in SMALL_NAMES])
    rows = -(-flat.shape[0] // LANES)
    rows = -(-rows // (N_DEV * SUBLANES)) * (N_DEV * SUBLANES)
    flat = jnp.pad(flat, (0, rows * LANES - flat.shape[0]))
    return flat.reshape(rows, LANES)


def _unpack_small(packed, like):
    flat = packed.reshape(-1)
    out, off = {}, 0
    for n in SMALL_NAMES:
        sz = like[n].size
        out[n] = flat[off:off + sz].reshape(like[n].shape)
        off += sz
    return out


def _blocked_cols(w2d):
    K, N = w2d.shape
    return jnp.transpose(w2d.reshape(K, N_DEV, N // N_DEV), (1, 0, 2))


def _unblock_cols(wb):
    n, K, nb = wb.shape
    return jnp.transpose(wb, (1, 0, 2)).reshape(K, n * nb)


def kernel(x, c, w_ada, b_ada, w_in, b_in, w_conv, b_conv, w_rg_a, b_rg_a, w_rg_x, b_rg_x, lru_lambda, w_sp, b_sp, ln_v_g, ln_v_b, w_o_lru, w_o_sgu, w_out, ln1_g, ln1_b, w_up, w_down, ln2_g, ln2_b, loss_target, m_w_ada, m_b_ada, m_w_in, m_b_in, m_w_conv, m_b_conv, m_w_rg_a, m_b_rg_a, m_w_rg_x, m_b_rg_x, m_lru_lambda, m_w_sp, m_b_sp, m_ln_v_g, m_ln_v_b, m_w_o_lru, m_w_o_sgu, m_w_out, m_ln1_g, m_ln1_b, m_w_up, m_w_down, m_ln2_g, m_ln2_b, v_w_ada, v_b_ada, v_w_in, v_b_in, v_w_conv, v_b_conv, v_w_rg_a, v_b_rg_a, v_w_rg_x, v_b_rg_x, v_lru_lambda, v_w_sp, v_b_sp, v_ln_v_g, v_ln_v_b, v_w_o_lru, v_w_o_sgu, v_w_out, v_ln1_g, v_ln1_b, v_w_up, v_w_down, v_ln2_g, v_ln2_b):
    W = dict(w_ada=w_ada, b_ada=b_ada, w_in=w_in, b_in=b_in, w_conv=w_conv, b_conv=b_conv, w_rg_a=w_rg_a,
             b_rg_a=b_rg_a, w_rg_x=w_rg_x, b_rg_x=b_rg_x, lru_lambda=lru_lambda, w_sp=w_sp, b_sp=b_sp,
             ln_v_g=ln_v_g, ln_v_b=ln_v_b, w_o_lru=w_o_lru, w_o_sgu=w_o_sgu, w_out=w_out, ln1_g=ln1_g, ln1_b=ln1_b,
             w_up=w_up, w_down=w_down, ln2_g=ln2_g, ln2_b=ln2_b)
    Mo = dict(w_ada=m_w_ada, b_ada=m_b_ada, w_in=m_w_in, b_in=m_b_in, w_conv=m_w_conv, b_conv=m_b_conv,
              w_rg_a=m_w_rg_a, b_rg_a=m_b_rg_a, w_rg_x=m_w_rg_x, b_rg_x=m_b_rg_x, lru_lambda=m_lru_lambda,
              w_sp=m_w_sp, b_sp=m_b_sp, ln_v_g=m_ln_v_g, ln_v_b=m_ln_v_b, w_o_lru=m_w_o_lru, w_o_sgu=m_w_o_sgu,
              w_out=m_w_out, ln1_g=m_ln1_g, ln1_b=m_ln1_b, w_up=m_w_up, w_down=m_w_down, ln2_g=m_ln2_g,
              ln2_b=m_ln2_b)
    Vo = dict(w_ada=v_w_ada, b_ada=v_b_ada, w_in=v_w_in, b_in=v_b_in, w_conv=v_w_conv, b_conv=v_b_conv,
              w_rg_a=v_w_rg_a, b_rg_a=v_b_rg_a, w_rg_x=v_w_rg_x, b_rg_x=v_b_rg_x, lru_lambda=v_lru_lambda,
              w_sp=v_w_sp, b_sp=v_b_sp, ln_v_g=v_ln_v_g, ln_v_b=v_ln_v_b, w_o_lru=v_w_o_lru, w_o_sgu=v_w_o_sgu,
              w_out=v_w_out, ln1_g=v_ln1_g, ln1_b=v_ln1_b, w_up=v_w_up, w_down=v_w_down, ln2_g=v_ln2_g,
              ln2_b=v_ln2_b)

    Bl, S, D = x.shape
    T = Bl * S
    lw = b_conv.shape[-1]
    sw = ln_v_g.shape[-1]
    din = b_in.shape[-1]
    dff = w_up.shape[-1] * N_DEV
    ts = min(512, S)
    tmix = min(256, S)
    tmm = min(1024, T)

    wnames = ("win", "wol", "wos", "wout", "wup", "wdown")
    shards = [w_in[0].astype(BF16), w_o_lru[0].astype(BF16), w_o_sgu[0].astype(BF16), w_out[0].astype(BF16),
              w_up[0].astype(BF16), w_down[0].astype(BF16)]
    lands = _landing(shards, True, "gather_own")
    g_send, g_recv, g_src, g_land, g_tok = _xstart(shards, lands, True, "gather_start")
    gidx = {n: i for i, n in enumerate(wnames)}

    def gathered(n, after):
        i = gidx[n]
        return _xwait(g_src[i], g_land[i], g_send[i], g_recv[i], after, True, "gather_wait_" + n)

    c_pad = jnp.pad(c, ((0, SUBLANES - Bl), (0, 0))) + g_tok[0, 0]
    c_g, wconv_g = _exchange([c_pad, w_conv[0]], True, "xchg_c")
    wconv_full = _unblock_cols(wconv_g)

    c_act, modcols = _ada_fwd(c_g.reshape(N_DEV * SUBLANES, D), w_ada[0])
    (mod_slots,) = _exchange([modcols.reshape(N_DEV, SUBLANES, -1)], False, "xchg_mod")
    mod = _unblock_cols(mod_slots)[:Bl] + b_ada
    sh1, sc1, gt1, sh2, sc2, gt2 = [mod[:, i * D:(i + 1) * D].reshape(Bl, 1, D) for i in range(6)]

    wa_b, wx_b = w_rg_a[0].astype(BF16), w_rg_x[0].astype(BF16)
    b_sp_t = jnp.transpose(b_sp[0])
    small_mix = (wconv_full, b_conv, wa_b, b_rg_a, wx_b, b_rg_x, lru_lambda, w_sp[0], b_sp_t, ln_v_g, ln_v_b)

    h = _modulate(x, sc1, sh1, ts)
    Win = _unblock_cols(gathered("win", h))
    proj = _mm(h.reshape(T, D), Win, mode="nn", tm=tmm, tn=768, tk=D, outs=[F32],
               extras=[(b_in, "row")], epilogue=lambda acc, ex: (acc + ex[0],), name="mm_proj")
    proj3 = proj.reshape(Bl, S, din)
    hs, ya_pre, ysgu = _mix_fwd(proj3, *small_mix, tm=tmix, lw=lw, sw=sw)
    Wol = gathered("wol", ya_pre).reshape(lw, D)
    Wos = _unblock_cols(gathered("wos", ysgu))
    y_a = _mm(ya_pre.reshape(T, lw), Wol, mode="nn", tm=tmm, tn=512, tk=lw, outs=[F32], name="mm_ya")
    y_b = _mm(ysgu.reshape(T, sw), Wos, mode="nn", tm=tmm, tn=512, tk=sw, outs=[F32], name="mm_yb")
    merged = _merge_fwd(proj3, y_a.reshape(Bl, S, D), y_b.reshape(Bl, S, D), ts=ts, d=D)
    Wout = gathered("wout", merged).reshape(D, D)
    mix = _mm(merged.reshape(T, D), Wout, mode="nn", tm=tmm, tn=512, tk=D, outs=[F32], name="mm_mix")
    mix3 = mix.reshape(Bl, S, D)
    x1, h2 = _ln1_fwd(x, mix3, gt1, ln1_g, ln1_b, sc2, sh2, ts=ts)
    Wup = _unblock_cols(gathered("wup", h2))
    up, act = _mm(h2.reshape(T, D), Wup, mode="nn", tm=tmm, tn=512, tk=D, outs=[F32, BF16],
                  epilogue=lambda acc, ex: (acc, jnp.square(jnp.maximum(acc, 0.0))), name="mm_up")
    Wdown = gathered("wdown", act).reshape(dff, D)
    f = _mm(act, Wdown, mode="nn", tm=tmm, tn=512, tk=1024, outs=[F32], name="mm_down")
    df, dx1p, dgt2, dg2, db2, loss_part = _ln2_loss(x1, f.reshape(Bl, S, D), loss_target, gt2, ln2_g, ln2_b, ts=ts)
    loss = lax.psum(loss_part[0, 0], ("x", "y", "c"))

    def send_grads(parts, name):
        own = _landing(parts, False, name + "_own")
        snd, rcv, src, land, tok = _xstart(parts, own, False, name + "_start")
        return [(src[i], land[i], snd[i], rcv[i]) for i in range(len(parts))], tok

    df2 = df.reshape(T, D)
    dup = _mm(df2, Wdown, mode="nt", tm=tmm, tn=512, tk=D, outs=[BF16], extras=[(up, "tile")],
              epilogue=lambda acc, ex: (acc * (2.0 * jnp.maximum(ex[0], 0.0)),), name="mm_dup")
    g_wdown = _mm(act, df2, mode="tn", tm=1024, tn=512, tk=512, outs=[BF16], name="mm_gwdown")
    (x_wdown,), tok = send_grads([g_wdown.reshape(N_DEV, dff // N_DEV, D)], "gx_wdown")
    dh2 = _mm(dup, Wup, mode="nt", tm=tmm, tn=512, tk=1024, outs=[F32], tok=tok, name="mm_dh2")
    g_wup = _mm(h2.reshape(T, D), dup, mode="tn", tm=1024, tn=dff // N_DEV, tk=512, outs=[BF16], out_blocked=True,
                name="mm_gwup")
    (x_wup,), tok = send_grads([g_wup], "gx_wup")
    dxp, dmix, dsc2, dsh2, dgt1, dg1, db1 = _ln1_bwd(dx1p, dh2.reshape(Bl, S, D), x1, x, mix3, sc2 + tok[0, 0], gt1,
                                                     ln1_g, ts=ts)

    dmix2 = dmix.reshape(T, D)
    dmerged = _mm(dmix2, Wout, mode="nt", tm=tmm, tn=512, tk=D, outs=[F32], name="mm_dmerged")
    g_wout = _mm(merged.reshape(T, D), dmix2, mode="tn", tm=1024, tn=512, tk=512, outs=[BF16], name="mm_gwout")
    (x_wout,), tok = send_grads([g_wout.reshape(N_DEV, D // N_DEV, D)], "gx_wout")
    dy_a, dy_b, dproj, dbin_hi = _merge_bwd(dmerged.reshape(Bl, S, D), y_a.reshape(Bl, S, D), y_b.reshape(Bl, S, D),
                                            proj3, ts=ts, d=D)
    dya_pre = _mm(dy_a.reshape(T, D), Wol, mode="nt", tm=tmm, tn=lw // 2, tk=D, outs=[F32], tok=tok, name="mm_dya")
    dysgu = _mm(dy_b.reshape(T, D), Wos, mode="nt", tm=tmm, tn=sw, tk=D, outs=[F32], name="mm_dys")
    g_wol = _mm(ya_pre.reshape(T, lw), dy_a.reshape(T, D), mode="tn", tm=lw, tn=512, tk=512, outs=[BF16],
                name="mm_gwol")
    g_wos = _mm(ysgu.reshape(T, sw), dy_b.reshape(T, D), mode="tn", tm=sw, tn=512, tk=512, outs=[BF16],
                name="mm_gwos")
    (x_wol, x_wos), tok = send_grads([g_wol.reshape(N_DEV, lw // N_DEV, D), _blocked_cols(g_wos)], "gx_wo")
    small_mix_b = (wconv_full, b_conv + tok[0, 0]) + small_mix[2:]
    (dproj, dbin_lo, g_wconv, g_bconv, g_wa, g_ba, g_wx, g_bx, g_lam, g_wsp, g_bsp_t, g_lvg, g_lvb) = _mix_bwd(
        proj3, hs, dya_pre.reshape(Bl, S, lw), dysgu.reshape(Bl, S, sw), dproj, *small_mix_b, tm=tmix, lw=lw, sw=sw)
    dproj2 = dproj.reshape(T, din)
    g_win = _mm(h.reshape(T, D), dproj2, mode="tn", tm=1024, tn=din // N_DEV, tk=512, outs=[BF16], out_blocked=True,
                name="mm_gwin")
    (x_win,), tok = send_grads([g_win], "gx_win")
    dh = _mm(dproj2, Win, mode="nt", tm=tmm, tn=512, tk=din // 4, outs=[F32], tok=tok, name="mm_dh")
    grad_x, dsc1, dsh1 = _final_dx(dxp, dh.reshape(Bl, S, D), x, sc1, ts=ts)

    dmod = jnp.concatenate([dsh1, dsc1, dgt1, dsh2, dsc2, dgt2], axis=-1).reshape(Bl, 6 * D)
    dmod_b = _blocked_cols(jnp.pad(dmod, ((0, SUBLANES - Bl), (0, 0))))
    g_small_local = dict(
        b_ada=jnp.sum(dmod, axis=0, keepdims=True), b_in=jnp.concatenate([dbin_lo, dbin_hi], axis=-1),
        b_conv=g_bconv, w_rg_a=g_wa[None], b_rg_a=g_ba, w_rg_x=g_wx[None], b_rg_x=g_bx, lru_lambda=g_lam,
        w_sp=g_wsp[None], b_sp=jnp.transpose(g_bsp_t)[None], ln_v_g=g_lvg, ln_v_b=g_lvb, ln1_g=dg1, ln1_b=db1,
        ln2_g=dg2, ln2_b=db2)
    gs_packed = _pack_small(g_small_local)
    rows = gs_packed.shape[0]
    parts = [dmod_b, _blocked_cols(g_wconv), gs_packed.reshape(N_DEV, rows // N_DEV, LANES)]
    dmod_s, gwconv_s, gsmall_s = _exchange(parts, False, "xchg_grads")
    gwdown_s = _xwait(*x_wdown, dmod_s, False, "gx_wdown_wait")
    gwup_s = _xwait(*x_wup, dmod_s, False, "gx_wup_wait")
    gwout_s = _xwait(*x_wout, dmod_s, False, "gx_wout_wait")
    gwol_s = _xwait(*x_wol, dmod_s, False, "gx_wol_wait")
    gwos_s = _xwait(*x_wos, dmod_s, False, "gx_wos_wait")
    gwin_s = _xwait(*x_win, dmod_s, False, "gx_win_wait")

    out_g, out_d, out_m, out_v = {}, {}, {}, {}

    def adam(name, g_slots, tr):
        shp = W[name].shape
        w2, m2, v2 = [t.reshape(g_slots.shape[1:]) for t in (W[name], Mo[name], Vo[name])]
        g, d, mn, vn = _adamw(w2, g_slots, m2, v2, tr=tr, name="adam_" + name)
        out_g[name], out_d[name], out_m[name], out_v[name] = [t.reshape(shp) for t in (g, d, mn, vn)]

    g_wada = _ada_bwd(c_act, dmod_s.reshape(N_DEV * SUBLANES, -1))
    adam("w_ada", g_wada[None], 256)
    adam("w_in", gwin_s, 256)
    adam("w_conv", gwconv_s, 8)
    adam("w_o_lru", gwol_s, 160)
    adam("w_o_sgu", gwos_s, 256)
    adam("w_out", gwout_s, 128)
    adam("w_up", gwup_s, 256)
    adam("w_down", gwdown_s, 256)

    g_chunk = _sum_slots(gsmall_s, name="sum_small")
    (gsmall_all,) = _exchange([g_chunk], True, "xchg_small")
    gs, ds, ms, vs = _adamw(_pack_small(W), gsmall_all.reshape(1, rows, LANES), _pack_small(Mo), _pack_small(Vo),
                            tr=rows // N_DEV, name="adam_small")
    for dst, packed in ((out_g, gs), (out_d, ds), (out_m, ms), (out_v, vs)):
        dst.update(_unpack_small(packed, W))

    return (loss, grad_x, *[out_g[n] for n in WEIGHT_ORDER], *[out_d[n] for n in WEIGHT_ORDER],
            *[out_m[n] for n in WEIGHT_ORDER], *[out_v[n] for n in WEIGHT_ORDER])
```

```python
import functools
import math

import jax
import jax.numpy as jnp
from jax import lax
from jax.experimental import pallas as pl
from jax.experimental.pallas import tpu as pltpu

N_DEV = 8
LN_EPS = 1e-5
LRU_C = 8.0
CHUNK = 64
SGU_BLOCK = 128
ALPHA = 2.0 ** 0.25
ADAM_LR = 0.001
ADAM_B1 = 0.9
ADAM_B2 = 0.999
ADAM_EPS = 1e-08
ADAM_WD = 0.01
ADAM_STEP = 10
GELU_K0 = math.sqrt(2.0 / math.pi)
GELU_K1 = 0.044715

SUBLANES = 8
LANES = 128
VMEM_LIMIT = 56 * 1024 * 1024

F32 = jnp.float32
BF16 = jnp.bfloat16
MESH = pl.DeviceIdType.MESH


def _cparams(n_axes, big=False):
    return pltpu.CompilerParams(dimension_semantics=("arbitrary",) * n_axes,
                                vmem_limit_bytes=VMEM_LIMIT if big else None)


def _sigmoid(x):
    return 1.0 / (1.0 + jnp.exp(-x))


def _gelu(x):
    t = jnp.tanh(GELU_K0 * (x + GELU_K1 * (x * x * x)))
    return 0.5 * x * (1.0 + t)


def _gelu_and_grad(x):
    x2 = x * x
    t = jnp.tanh(GELU_K0 * (x + GELU_K1 * (x2 * x)))
    g = 0.5 * x * (1.0 + t)
    dg = 0.5 * (1.0 + t) + 0.5 * x * (1.0 - t * t) * (GELU_K0 * (1.0 + 3.0 * GELU_K1 * x2))
    return g, dg


def _expm1(x):
    p = x * (1.0 + x * (1.0 / 2.0) * (1.0 + x * (1.0 / 3.0) * (1.0 + x * (1.0 / 4.0) * (
        1.0 + x * (1.0 / 5.0) * (1.0 + x * (1.0 / 6.0) * (1.0 + x * (1.0 / 7.0)))))))
    return jnp.where(jnp.abs(x) < 0.3, p, jnp.exp(x) - 1.0)


def _log1p_pos(e):
    p = e * (1.0 - e * (1.0 / 2.0) + e * e * (1.0 / 3.0) - e * e * e * (1.0 / 4.0))
    return jnp.where(e < 1e-2, p, jnp.log(1.0 + e))


def _ln_stats(z):
    mu = jnp.mean(z, axis=-1, keepdims=True)
    zc = z - mu
    var = jnp.mean(zc * zc, axis=-1, keepdims=True)
    rstd = lax.rsqrt(var + LN_EPS)
    return zc * rstd, rstd


def _ln_bwd(dy, xhat, rstd, g):
    dxh = dy * g
    m1 = jnp.mean(dxh, axis=-1, keepdims=True)
    m2 = jnp.mean(dxh * xhat, axis=-1, keepdims=True)
    return rstd * (dxh - m1 - xhat * m2)


def _colsum(v):
    return jnp.sum(v, axis=0, keepdims=True)


def _first_step():
    return jnp.logical_and(pl.program_id(0) == 0, pl.program_id(1) == 0)


def _exchange(arrs, gather, name):
    n = len(arrs)
    n_peer = N_DEV - 1

    def body(*refs):
        ins, outs = refs[:n], refs[n:2 * n]
        send_sems, recv_sems, loc_sems = refs[2 * n:]
        x, y, c = lax.axis_index("x"), lax.axis_index("y"), lax.axis_index("c")
        me = 4 * x + 2 * y + c
        started = []
        for a in range(n):
            src_me = ins[a] if gather else ins[a].at[me]
            lc = pltpu.make_async_copy(src_me, outs[a].at[me], loc_sems.at[a])
            lc.start()
            started.append((lc, None))
        for p in range(1, N_DEV):
            px, py, pc = x ^ ((p >> 2) & 1), y ^ ((p >> 1) & 1), c ^ (p & 1)
            peer = 4 * px + 2 * py + pc
            for a in range(n):
                k = a * n_peer + (p - 1)
                src = ins[a] if gather else ins[a].at[peer]
                cp = pltpu.make_async_remote_copy(src_ref=src, dst_ref=outs[a].at[me],
                                                  send_sem=send_sems.at[k], recv_sem=recv_sems.at[k],
                                                  device_id=(px, py, pc), device_id_type=MESH)
                cp.start()
                rc = pltpu.make_async_remote_copy(src_ref=src, dst_ref=outs[a].at[peer],
                                                  send_sem=send_sems.at[k], recv_sem=recv_sems.at[k],
                                                  device_id=(px, py, pc), device_id_type=MESH)
                started.append((cp, rc))
        for cp, rc in started:
            if rc is None:
                cp.wait()
            else:
                cp.wait_send()
                rc.wait_recv()

    hbm = pl.BlockSpec(memory_space=pltpu.HBM)
    out_shape = tuple(
        jax.ShapeDtypeStruct(((N_DEV,) + a.shape) if gather else a.shape, a.dtype) for a in arrs)
    return pl.pallas_call(
        body, name=name, out_shape=out_shape,
        in_specs=[hbm] * n, out_specs=tuple([hbm] * n),
        scratch_shapes=[pltpu.SemaphoreType.DMA((n * n_peer,)), pltpu.SemaphoreType.DMA((n * n_peer,)),
                        pltpu.SemaphoreType.DMA((n,))],
        compiler_params=pltpu.CompilerParams(has_side_effects=True),
    )(*arrs)


_HBM = pl.BlockSpec(memory_space=pltpu.HBM)
_SEM = pl.BlockSpec(memory_space=pltpu.SEMAPHORE)
_EFFECT = pltpu.SideEffectType.DATAFLOW_SIDE_EFFECTING


def _peer_of(p):
    x, y, c = lax.axis_index("x"), lax.axis_index("y"), lax.axis_index("c")
    px, py, pc = x ^ ((p >> 2) & 1), y ^ ((p >> 1) & 1), c ^ (p & 1)
    return (px, py, pc), 4 * px + 2 * py + pc


def _xstart(srcs, gather, after, name):
    n = len(srcs)
    lands = [lax.empty(((N_DEV,) + t.shape) if gather else t.shape, t.dtype) for t in srcs]
    n_after = 0 if after is None else 1

    def body(*refs):
        src_refs, land_refs = refs[:n], refs[n:2 * n]
        refs = refs[n_after:]
        send_sems, recv_sems = refs[2 * n:3 * n], refs[3 * n:4 * n]
        token = refs[6 * n]
        me = 4 * lax.axis_index("x") + 2 * lax.axis_index("y") + lax.axis_index("c")
        for a in range(n):
            for p in range(1, N_DEV):
                dev, peer = _peer_of(p)
                pltpu.make_async_remote_copy(
                    src_ref=src_refs[a] if gather else src_refs[a].at[peer], dst_ref=land_refs[a].at[me],
                    send_sem=send_sems[a].at[p - 1], recv_sem=recv_sems[a].at[p - 1],
                    device_id=dev, device_id_type=MESH).start()
        token[...] = jnp.zeros_like(token)

    sems = tuple(pltpu.SemaphoreType.DMA((N_DEV - 1,)) for _ in range(2 * n))
    thru = tuple(pltpu.HBM(t.shape, t.dtype) for t in list(srcs) + list(lands))
    res = pl.pallas_call(
        body, name=name,
        out_shape=sems + thru + (jax.ShapeDtypeStruct((SUBLANES, LANES), F32),),
        in_specs=[_HBM] * (2 * n) + [pl.BlockSpec(memory_space=pl.ANY)] * n_after,
        out_specs=tuple([_SEM] * (2 * n) + [_HBM] * (2 * n) + [pl.BlockSpec(memory_space=pltpu.VMEM)]),
        input_output_aliases={i: 2 * n + i for i in range(2 * n)},
        compiler_params=pltpu.CompilerParams(has_side_effects=_EFFECT),
    )(*[pltpu.with_memory_space_constraint(t, pltpu.HBM) for t in list(srcs) + list(lands)],
      *([after] if n_after else []))
    return res[:n], res[n:2 * n], res[2 * n:3 * n], res[3 * n:4 * n], res[4 * n]


def _xwait(src, land, send_sem, recv_sem, after, gather, name):
    def body(src_ref, land_ref, send_ref, recv_ref, after_ref, src_dead, land_out):
        del after_ref, src_dead, land_out
        for p in range(1, N_DEV):
            dev, peer = _peer_of(p)
            cp = pltpu.make_async_remote_copy(
                src_ref=src_ref if gather else src_ref.at[peer], dst_ref=land_ref.at[peer],
                send_sem=send_ref.at[p - 1], recv_sem=recv_ref.at[p - 1], device_id=dev, device_id_type=MESH)
            cp.wait_send()
            cp.wait_recv()

    src_done, landed = pl.pallas_call(
        body, name=name, out_shape=(pltpu.HBM(src.shape, src.dtype), pltpu.HBM(land.shape, land.dtype)),
        in_specs=[_HBM, _HBM, _SEM, _SEM, pl.BlockSpec(memory_space=pl.ANY)], out_specs=(_HBM, _HBM),
        input_output_aliases={0: 0, 1: 1},
        compiler_params=pltpu.CompilerParams(has_side_effects=_EFFECT),
    )(src, land, send_sem, recv_sem, after)
    me = 4 * lax.axis_index("x") + 2 * lax.axis_index("y") + lax.axis_index("c")
    own = src_done if gather else lax.dynamic_index_in_dim(src_done, me, 0, keepdims=False)
    return lax.dynamic_update_slice(landed, own[None], (me,) + (0,) * own.ndim)


def _mm(a, b, *, mode, tm, tn, tk, outs, epilogue=None, extras=(), out_blocked=False, tok=None, name):
    if mode == "nn":
        (M, K), (_, N) = a.shape, b.shape
    elif mode == "nt":
        (M, K), (N, _) = a.shape, b.shape
    else:
        (K, M), (_, N) = a.shape, b.shape
    tm, tn, tk = min(tm, M), min(tn, N), min(tk, K)
    assert M % tm == 0 and N % tn == 0 and K % tk == 0, (name, M, N, K, tm, tn, tk)
    if mode == "nn":
        a_spec = pl.BlockSpec((tm, tk), lambda i, j, k: (i, k))
        b_spec = pl.BlockSpec((tk, tn), lambda i, j, k: (k, j))
        dims = (((1,), (0,)), ((), ()))
    elif mode == "nt":
        a_spec = pl.BlockSpec((tm, tk), lambda i, j, k: (i, k))
        b_spec = pl.BlockSpec((tn, tk), lambda i, j, k: (j, k))
        dims = (((1,), (1,)), ((), ()))
    else:
        a_spec = pl.BlockSpec((tk, tm), lambda i, j, k: (k, i))
        b_spec = pl.BlockSpec((tk, tn), lambda i, j, k: (k, j))
        dims = (((0,), (0,)), ((), ()))
    nk = K // tk
    n_ex, n_out = len(extras), len(outs)
    n_tok = 0 if tok is None else 1
    if epilogue is None:
        epilogue = lambda acc, ex: tuple(acc.astype(d) for d in outs)

    def body(a_ref, b_ref, *refs):
        refs = refs[n_tok:]
        ex_refs, out_refs = refs[:n_ex], refs[n_ex:n_ex + n_out]

        def finish(acc):
            res = epilogue(acc, [r[...] for r in ex_refs])
            for o_ref, v in zip(out_refs, res):
                o_ref[...] = v.astype(o_ref.dtype)

        part = lax.dot_general(a_ref[...], b_ref[...], dims, preferred_element_type=F32)
        if nk == 1:
            finish(part)
        else:
            acc_ref = refs[n_ex + n_out]
            k = pl.program_id(2)

            @pl.when(k == 0)
            def _():
                acc_ref[...] = part

            @pl.when(k > 0)
            def _():
                acc_ref[...] += part

            @pl.when(k == nk - 1)
            def _():
                finish(acc_ref[...])

    ex_specs = [pl.BlockSpec((tm, tn), lambda i, j, k: (i, j)) if kind == "tile"
                else pl.BlockSpec((1, tn), lambda i, j, k: (0, j)) for _, kind in extras]
    if out_blocked:
        o_spec = pl.BlockSpec((None, tm, tn), lambda i, j, k: (j, i, 0))
        o_shape = (N // tn, M, tn)
    else:
        o_spec = pl.BlockSpec((tm, tn), lambda i, j, k: (i, j))
        o_shape = (M, N)
    res = pl.pallas_call(
        body, name=name, grid=(M // tm, N // tn, nk),
        in_specs=[a_spec, b_spec] + [pl.BlockSpec((SUBLANES, LANES), lambda i, j, k: (0, 0))] * n_tok + ex_specs,
        out_specs=tuple([o_spec] * n_out),
        out_shape=tuple(jax.ShapeDtypeStruct(o_shape, d) for d in outs),
        scratch_shapes=[pltpu.VMEM((tm, tn), F32)] if nk > 1 else [],
        compiler_params=_cparams(3, big=True),
    )(a, b, *([tok] if n_tok else []), *[e for e, _ in extras])
    return res[0] if n_out == 1 else res


def _tok_spec(ts, width, col_block=0):
    return pl.BlockSpec((None, ts, width), lambda b, s: (b, s, col_block))


def _brow_spec(width):
    return pl.BlockSpec((None, 1, width), lambda b, s: (b, 0, 0))


def _vec_spec(width):
    return pl.BlockSpec((1, width), lambda b, s: (0, 0))


def _modulate(x, sc, sh, ts):
    Bl, S, D = x.shape

    def body(x_ref, sc_ref, sh_ref, o_ref):
        o_ref[...] = (x_ref[...] * (1.0 + sc_ref[...]) + sh_ref[...]).astype(BF16)

    return pl.pallas_call(
        body, name="modulate", grid=(Bl, S // ts),
        in_specs=[_tok_spec(ts, D), _brow_spec(D), _brow_spec(D)],
        out_specs=_tok_spec(ts, D), out_shape=jax.ShapeDtypeStruct((Bl, S, D), BF16),
        compiler_params=_cparams(2),
    )(x, sc, sh)


def _mix_fwd(proj, w_conv, b_conv, w_rg_a, b_rg_a, w_rg_x, b_rg_x, lam, w_sp, b_sp_t, ln_v_g, ln_v_b, *, tm, lw, sw):
    Bl, S, _ = proj.shape
    heads, hd = w_rg_a.shape[0], w_rg_a.shape[1]
    groups = w_sp.shape[0]
    cw = 2 * lw + 2 * sw
    nblk = tm // SGU_BLOCK

    def body(p_ref, wc_ref, bc_ref, wa_ref, ba_ref, wx_ref, bx_ref, lam_ref, wsp_ref, bsp_ref, lg_ref, lb_ref,
             hs_ref, ya_ref, ys_ref, xprev, hcar, a_scr, b_scr):
        s = pl.program_id(1)

        @pl.when(s == 0)
        def _():
            xprev[...] = jnp.zeros_like(xprev)
            hcar[...] = jnp.zeros_like(hcar)

        xl = p_ref[:, 0:lw]
        gl = p_ref[:, lw:2 * lw]
        row8 = lax.broadcasted_iota(jnp.int32, (SUBLANES, lw), 0)
        rowm = lax.broadcasted_iota(jnp.int32, (tm, lw), 0) & (SUBLANES - 1)

        prev = xprev[...]
        xc = xl * wc_ref[3:4, :] + bc_ref[...]
        for k in (1, 2, 3):
            xr = pltpu.roll(xl, k, 0)
            head = jnp.where(row8 < k, pltpu.roll(prev, k, 0), xr[0:SUBLANES])
            xs = jnp.concatenate([head, xr[SUBLANES:]], axis=0)
            xc = xc + xs * wc_ref[3 - k:4 - k, :]
        xprev[...] = xl[tm - SUBLANES:tm]

        xcb = xc.astype(BF16)
        pa = jnp.concatenate([jnp.dot(xcb[:, h * hd:(h + 1) * hd], wa_ref[h], preferred_element_type=F32)
                              for h in range(heads)], axis=1) + ba_ref[...]
        px = jnp.concatenate([jnp.dot(xcb[:, h * hd:(h + 1) * hd], wx_ref[h], preferred_element_type=F32)
                              for h in range(heads)], axis=1) + bx_ref[...]
        r = _sigmoid(pa)
        ig = _sigmoid(px)
        nl = -lam_ref[...]
        big_l = -LRU_C * (jnp.maximum(nl, 0.0) + _log1p_pos(jnp.exp(-jnp.abs(nl))))
        la = big_l * r
        a = jnp.exp(la)
        bin_ = jnp.sqrt(-_expm1(2.0 * la)) * (ig * xc)

        for d in (1, 2, 4):
            a_sh = pltpu.roll(a, d, 0)
            b_sh = pltpu.roll(bin_, d, 0)
            msk = rowm >= d
            bin_ = jnp.where(msk, a * b_sh + bin_, bin_)
            a = jnp.where(msk, a * a_sh, a)
        a_scr[...] = a
        b_scr[...] = bin_

        def grp(g, carry):
            off = pl.multiple_of(g * SUBLANES, SUBLANES)
            h = b_scr[pl.ds(off, SUBLANES), :] + a_scr[pl.ds(off, SUBLANES), :] * carry
            hs_ref[pl.ds(off, SUBLANES), :] = h
            return jnp.broadcast_to(h[SUBLANES - 1:SUBLANES, :], h.shape)

        hcar[...] = lax.fori_loop(0, tm // SUBLANES, grp, hcar[...])
        ya_ref[...] = (hs_ref[...] * _gelu(gl)).astype(BF16)

        gu = _gelu(p_ref[:, 2 * lw:2 * lw + sw])
        gv = _gelu(p_ref[:, 2 * lw + sw:cw])
        xhat, _ = _ln_stats(gv)
        vn = (xhat * lg_ref[...] + lb_ref[...]).astype(BF16)
        tpos = lax.broadcasted_iota(jnp.int32, (SGU_BLOCK, SGU_BLOCK), 0) // CHUNK
        spos = lax.broadcasted_iota(jnp.int32, (SGU_BLOCK, SGU_BLOCK), 1) // CHUNK
        gw = sw // groups
        rows_out = []
        for blk in range(nblk):
            r0 = blk * SGU_BLOCK
            cols = []
            for g in range(groups):
                wm = jnp.where(spos <= tpos, wsp_ref[g], 0.0).astype(BF16)
                mixed = jnp.dot(wm, vn[r0:r0 + SGU_BLOCK, g * gw:(g + 1) * gw], preferred_element_type=F32)
                cols.append(mixed + bsp_ref[:, g:g + 1])
            rows_out.append(jnp.concatenate(cols, axis=1))
        mixed_all = jnp.concatenate(rows_out, axis=0) if nblk > 1 else rows_out[0]
        ys_ref[...] = (gu * mixed_all).astype(BF16)

    full = lambda shp: pl.BlockSpec(shp, lambda b, s: (0,) * len(shp))
    return pl.pallas_call(
        body, name="mix_fwd", grid=(Bl, S // tm),
        in_specs=[_tok_spec(tm, cw), full(w_conv.shape), full(b_conv.shape), full(w_rg_a.shape), full(b_rg_a.shape),
                  full(w_rg_x.shape), full(b_rg_x.shape), full(lam.shape), full(w_sp.shape), full(b_sp_t.shape),
                  full(ln_v_g.shape), full(ln_v_b.shape)],
        out_specs=(_tok_spec(tm, lw), _tok_spec(tm, lw), _tok_spec(tm, sw)),
        out_shape=(jax.ShapeDtypeStruct((Bl, S, lw), F32), jax.ShapeDtypeStruct((Bl, S, lw), BF16),
                   jax.ShapeDtypeStruct((Bl, S, sw), BF16)),
        scratch_shapes=[pltpu.VMEM((SUBLANES, lw), F32), pltpu.VMEM((SUBLANES, lw), F32),
                        pltpu.VMEM((tm, lw), F32), pltpu.VMEM((tm, lw), F32)],
        compiler_params=_cparams(2, big=True),
    )(proj, w_conv, b_conv, w_rg_a, b_rg_a, w_rg_x, b_rg_x, lam, w_sp, b_sp_t, ln_v_g, ln_v_b)


def _merge_fwd(proj, y_a, y_b, *, ts, d):
    Bl, S, din = proj.shape
    gcol = (din - 2 * d) // (2 * d)
    assert gcol * 2 * d == din - 2 * d

    def body(g_ref, ya_ref, yb_ref, o_ref):
        o_ref[...] = (_sigmoid(g_ref[:, 0:d]) * ya_ref[...] + _sigmoid(g_ref[:, d:2 * d]) * yb_ref[...]).astype(BF16)

    return pl.pallas_call(
        body, name="merge_fwd", grid=(Bl, S // ts),
        in_specs=[_tok_spec(ts, 2 * d, gcol), _tok_spec(ts, d), _tok_spec(ts, d)],
        out_specs=_tok_spec(ts, d), out_shape=jax.ShapeDtypeStruct((Bl, S, d), BF16),
        compiler_params=_cparams(2),
    )(proj, y_a, y_b)


def _ln1_fwd(x, mix, gt1, g1, b1, sc2, sh2, *, ts):
    Bl, S, D = x.shape

    def body(x_ref, mix_ref, gt_ref, g_ref, b_ref, sc_ref, sh_ref, x1_ref, h2_ref):
        z = ALPHA * x_ref[...] + (1.0 + gt_ref[...]) * mix_ref[...]
        xhat, _ = _ln_stats(z)
        x1 = xhat * g_ref[...] + b_ref[...]
        x1_ref[...] = x1
        h2_ref[...] = (x1 * (1.0 + sc_ref[...]) + sh_ref[...]).astype(BF16)

    return pl.pallas_call(
        body, name="ln1_fwd", grid=(Bl, S // ts),
        in_specs=[_tok_spec(ts, D), _tok_spec(ts, D), _brow_spec(D), _vec_spec(D), _vec_spec(D), _brow_spec(D),
                  _brow_spec(D)],
        out_specs=(_tok_spec(ts, D), _tok_spec(ts, D)),
        out_shape=(jax.ShapeDtypeStruct((Bl, S, D), F32), jax.ShapeDtypeStruct((Bl, S, D), BF16)),
        compiler_params=_cparams(2),
    )(x, mix, gt1, g1, b1, sc2, sh2)


def _ln2_loss(x1, f, tgt, gt2, g2, b2, *, ts):
    Bl, S, D = x1.shape

    def body(x1_ref, f_ref, t_ref, gt_ref, g_ref, b_ref, df_ref, dx1_ref, dgt_ref, dg_ref, db_ref, loss_ref):
        s = pl.program_id(1)

        @pl.when(_first_step())
        def _():
            dg_ref[...] = jnp.zeros_like(dg_ref)
            db_ref[...] = jnp.zeros_like(db_ref)
            loss_ref[...] = jnp.zeros_like(loss_ref)

        @pl.when(s == 0)
        def _():
            dgt_ref[...] = jnp.zeros_like(dgt_ref)

        fv = f_ref[...]
        z = ALPHA * x1_ref[...] + (1.0 + gt_ref[...]) * fv
        xhat, rstd = _ln_stats(z)
        x2 = xhat * g_ref[...] + b_ref[...]
        err = x2 - t_ref[...]
        loss_ref[...] += 0.5 * jnp.sum(jnp.mean(err * err, axis=-1, keepdims=True))
        dy = err * (1.0 / D)
        dg_ref[...] += _colsum(dy * xhat)
        db_ref[...] += _colsum(dy)
        dz = _ln_bwd(dy, xhat, rstd, g_ref[...])
        dx1_ref[...] = ALPHA * dz
        dgt_ref[...] += _colsum(dz * fv)
        df_ref[...] = (dz * (1.0 + gt_ref[...])).astype(BF16)

    return pl.pallas_call(
        body, name="ln2_loss", grid=(Bl, S // ts),
        in_specs=[_tok_spec(ts, D), _tok_spec(ts, D), _tok_spec(ts, D), _brow_spec(D), _vec_spec(D), _vec_spec(D)],
        out_specs=(_tok_spec(ts, D), _tok_spec(ts, D), _brow_spec(D), _vec_spec(D), _vec_spec(D),
                   pl.BlockSpec((SUBLANES, LANES), lambda b, s: (0, 0))),
        out_shape=(jax.ShapeDtypeStruct((Bl, S, D), BF16), jax.ShapeDtypeStruct((Bl, S, D), F32),
                   jax.ShapeDtypeStruct((Bl, 1, D), F32), jax.ShapeDtypeStruct((1, D), F32),
                   jax.ShapeDtypeStruct((1, D), F32), jax.ShapeDtypeStruct((SUBLANES, LANES), F32)),
        compiler_params=_cparams(2),
    )(x1, f, tgt, gt2, g2, b2)


def _ln1_bwd(dx1p, dh2, x1, x, mix, sc2, gt1, g1, *, ts):
    Bl, S, D = x.shape

    def body(dx1p_ref, dh2_ref, x1_ref, x_ref, mix_ref, sc_ref, gt_ref, g_ref,
             dxp_ref, dmix_ref, dsc_ref, dsh_ref, dgt_ref, dg_ref, db_ref):
        s = pl.program_id(1)

        @pl.when(_first_step())
        def _():
            dg_ref[...] = jnp.zeros_like(dg_ref)
            db_ref[...] = jnp.zeros_like(db_ref)

        @pl.when(s == 0)
        def _():
            dsc_ref[...] = jnp.zeros_like(dsc_ref)
            dsh_ref[...] = jnp.zeros_like(dsh_ref)
            dgt_ref[...] = jnp.zeros_like(dgt_ref)

        dh2 = dh2_ref[...]
        mixv = mix_ref[...]
        dsc_ref[...] += _colsum(dh2 * x1_ref[...])
        dsh_ref[...] += _colsum(dh2)
        dx1 = dx1p_ref[...] + dh2 * (1.0 + sc_ref[...])
        z = ALPHA * x_ref[...] + (1.0 + gt_ref[...]) * mixv
        xhat, rstd = _ln_stats(z)
        dg_ref[...] += _colsum(dx1 * xhat)
        db_ref[...] += _colsum(dx1)
        dz = _ln_bwd(dx1, xhat, rstd, g_ref[...])
        dxp_ref[...] = ALPHA * dz
        dgt_ref[...] += _colsum(dz * mixv)
        dmix_ref[...] = (dz * (1.0 + gt_ref[...])).astype(BF16)

    return pl.pallas_call(
        body, name="ln1_bwd", grid=(Bl, S // ts),
        in_specs=[_tok_spec(ts, D)] * 5 + [_brow_spec(D), _brow_spec(D), _vec_spec(D)],
        out_specs=(_tok_spec(ts, D), _tok_spec(ts, D), _brow_spec(D), _brow_spec(D), _brow_spec(D), _vec_spec(D),
                   _vec_spec(D)),
        out_shape=(jax.ShapeDtypeStruct((Bl, S, D), F32), jax.ShapeDtypeStruct((Bl, S, D), BF16),
                   jax.ShapeDtypeStruct((Bl, 1, D), F32), jax.ShapeDtypeStruct((Bl, 1, D), F32),
                   jax.ShapeDtypeStruct((Bl, 1, D), F32), jax.ShapeDtypeStruct((1, D), F32),
                   jax.ShapeDtypeStruct((1, D), F32)),
        compiler_params=_cparams(2),
    )(dx1p, dh2, x1, x, mix, sc2, gt1, g1)


def _merge_bwd(dmerged, y_a, y_b, proj, *, ts, d):
    Bl, S, din = proj.shape
    gcol = (din - 2 * d) // (2 * d)

    def body(dm_ref, ya_ref, yb_ref, g_ref, dya_ref, dyb_ref, dp_ref, db_ref):
        @pl.when(_first_step())
        def _():
            db_ref[...] = jnp.zeros_like(db_ref)

        dm = dm_ref[...]
        sa = _sigmoid(g_ref[:, 0:d])
        sb = _sigmoid(g_ref[:, d:2 * d])
        dya_ref[...] = (dm * sa).astype(BF16)
        dyb_ref[...] = (dm * sb).astype(BF16)
        dga = dm * ya_ref[...] * sa * (1.0 - sa)
        dgb = dm * yb_ref[...] * sb * (1.0 - sb)
        dp_ref[:, 0:d] = dga.astype(BF16)
        dp_ref[:, d:2 * d] = dgb.astype(BF16)
        db_ref[:, 0:d] += _colsum(dga)
        db_ref[:, d:2 * d] += _colsum(dgb)

    return pl.pallas_call(
        body, name="merge_bwd", grid=(Bl, S // ts),
        in_specs=[_tok_spec(ts, d), _tok_spec(ts, d), _tok_spec(ts, d), _tok_spec(ts, 2 * d, gcol)],
        out_specs=(_tok_spec(ts, d), _tok_spec(ts, d), _tok_spec(ts, 2 * d, gcol), _vec_spec(2 * d)),
        out_shape=(jax.ShapeDtypeStruct((Bl, S, d), BF16), jax.ShapeDtypeStruct((Bl, S, d), BF16),
                   jax.ShapeDtypeStruct((Bl, S, din), BF16), jax.ShapeDtypeStruct((1, 2 * d), F32)),
        compiler_params=_cparams(2),
    )(dmerged, y_a, y_b, proj)


def _mix_bwd(proj, hs, dya, dys, dproj, w_conv, b_conv, w_rg_a, b_rg_a, w_rg_x, b_rg_x, lam, w_sp, b_sp_t,
             ln_v_g, ln_v_b, *, tm, lw, sw):
    Bl, S, din = proj.shape
    heads, hd = w_rg_a.shape[0], w_rg_a.shape[1]
    groups = w_sp.shape[0]
    gw = sw // groups
    cw = 2 * lw + 2 * sw
    nblk = tm // SGU_BLOCK
    n_s = S // tm
    per8 = tm // SUBLANES

    def body(p_ref, xh_ref, hs_ref, hh_ref, dya_ref, dys_ref, dpin_ref,
             wc_ref, bc_ref, wa_ref, ba_ref, wx_ref, bx_ref, lam_ref, wsp_ref, bsp_ref, lg_ref, lb_ref,
             dp_ref, dbin_ref, dwc_ref, dbc_ref, dwa_ref, dba_ref, dwx_ref, dbx_ref, dlam_ref, dwsp_ref, dbsp_ref,
             dlg_ref, dlb_ref,
             dhcar, acar, dxcn, a_scr, b_scr, g_scr):
        del dpin_ref
        sr = pl.program_id(1)
        first_tile = sr == n_s - 1

        @pl.when(_first_step())
        def _():
            for ref in (dbin_ref, dwc_ref, dbc_ref, dwa_ref, dba_ref, dwx_ref, dbx_ref, dlam_ref, dwsp_ref, dbsp_ref,
                        dlg_ref, dlb_ref):
                ref[...] = jnp.zeros_like(ref)

        @pl.when(sr == 0)
        def _():
            dhcar[...] = jnp.zeros_like(dhcar)
            acar[...] = jnp.zeros_like(acar)
            dxcn[...] = jnp.zeros_like(dxcn)

        keep = jnp.where(first_tile, 0.0, 1.0)
        xl = p_ref[:, 0:lw]
        gl = p_ref[:, lw:2 * lw]
        row8 = lax.broadcasted_iota(jnp.int32, (SUBLANES, lw), 0)
        rowm = lax.broadcasted_iota(jnp.int32, (tm, lw), 0) & (SUBLANES - 1)

        prev = xh_ref[...] * keep
        xsh = [xl]
        for k in (1, 2, 3):
            xr = pltpu.roll(xl, k, 0)
            head = jnp.where(row8 < k, pltpu.roll(prev, k, 0), xr[0:SUBLANES])
            xsh.append(jnp.concatenate([head, xr[SUBLANES:]], axis=0))
        xc = bc_ref[...] + xsh[0] * wc_ref[3:4, :]
        for k in (1, 2, 3):
            xc = xc + xsh[k] * wc_ref[3 - k:4 - k, :]
        xcb = xc.astype(BF16)
        pa = jnp.concatenate([jnp.dot(xcb[:, h * hd:(h + 1) * hd], wa_ref[h], preferred_element_type=F32)
                              for h in range(heads)], axis=1) + ba_ref[...]
        px = jnp.concatenate([jnp.dot(xcb[:, h * hd:(h + 1) * hd], wx_ref[h], preferred_element_type=F32)
                              for h in range(heads)], axis=1) + bx_ref[...]
        r = _sigmoid(pa)
        ig = _sigmoid(px)
        nl = -lam_ref[...]
        big_l = -LRU_C * (jnp.maximum(nl, 0.0) + _log1p_pos(jnp.exp(-jnp.abs(nl))))
        la = big_l * r
        a = jnp.exp(la)
        msq = -_expm1(2.0 * la)
        m = jnp.sqrt(msq)
        hsv = hs_ref[...]
        ggl, dggl = _gelu_and_grad(gl)

        dyav = dya_ref[...]
        dhs = dyav * ggl
        dp_ref[:, lw:2 * lw] = (dyav * hsv * dggl).astype(BF16)
        dbin_ref[:, lw:2 * lw] += _colsum(dyav * hsv * dggl)
        a_up = pltpu.roll(a, tm - 1, 0)
        tail = jnp.where(row8 == SUBLANES - 1, acar[...], a_up[tm - SUBLANES:tm])
        an = jnp.concatenate([a_up[:tm - SUBLANES], tail], axis=0)
        acar[...] = jnp.broadcast_to(a[0:1, :], (SUBLANES, lw))
        bb = dhs
        for d in (1, 2, 4):
            a_sh = pltpu.roll(an, tm - d, 0)
            b_sh = pltpu.roll(bb, tm - d, 0)
            msk = rowm < SUBLANES - d
            bb = jnp.where(msk, an * b_sh + bb, bb)
            an = jnp.where(msk, an * a_sh, an)
        a_scr[...] = an
        b_scr[...] = bb

        def grp(i, carry):
            off = pl.multiple_of((per8 - 1 - i) * SUBLANES, SUBLANES)
            g = b_scr[pl.ds(off, SUBLANES), :] + a_scr[pl.ds(off, SUBLANES), :] * carry
            g_scr[pl.ds(off, SUBLANES), :] = g
            return jnp.broadcast_to(g[0:1, :], g.shape)

        dhcar[...] = lax.fori_loop(0, per8, grp, dhcar[...])
        dh = g_scr[...]

        hr = pltpu.roll(hsv, 1, 0)
        hhead = jnp.where(row8 < 1, pltpu.roll(hh_ref[...] * keep, 1, 0), hr[0:SUBLANES])
        hprev = jnp.concatenate([hhead, hr[SUBLANES:]], axis=0)
        da = dh * hprev
        ixc = ig * xc
        dm = dh * ixc
        dixc = dh * m
        di = dixc * xc
        dxc = dixc * ig
        dla = da * a - dm * (a * a) / m
        dlam_ref[...] += _colsum(dla * r) * (LRU_C * _sigmoid(nl))
        dr = dla * big_l
        dpa = dr * r * (1.0 - r)
        dpx = di * ig * (1.0 - ig)
        dba_ref[...] += _colsum(dpa)
        dbx_ref[...] += _colsum(dpx)
        dpab = dpa.astype(BF16)
        dpxb = dpx.astype(BF16)
        nt = (((1,), (1,)), ((), ()))
        tn = (((0,), (0,)), ((), ()))
        dxc_g = []
        for h in range(heads):
            sl = slice(h * hd, (h + 1) * hd)
            dxc_g.append(lax.dot_general(dpab[:, sl], wa_ref[h], nt, preferred_element_type=F32)
                         + lax.dot_general(dpxb[:, sl], wx_ref[h], nt, preferred_element_type=F32))
            dwa_ref[h] += lax.dot_general(xcb[:, sl], dpab[:, sl], tn, preferred_element_type=F32)
            dwx_ref[h] += lax.dot_general(xcb[:, sl], dpxb[:, sl], tn, preferred_element_type=F32)
        dxc = dxc + jnp.concatenate(dxc_g, axis=1)

        dbc_ref[...] += _colsum(dxc)
        for k in range(4):
            dwc_ref[k:k + 1, :] += _colsum(dxc * xsh[3 - k])
        nxt = dxcn[...]
        dxl = dxc * wc_ref[3:4, :]
        for k in (1, 2, 3):
            ur = pltpu.roll(dxc, tm - k, 0)
            tl = jnp.where(row8 >= SUBLANES - k, pltpu.roll(nxt, SUBLANES - k, 0), ur[tm - SUBLANES:tm])
            dxl = dxl + jnp.concatenate([ur[:tm - SUBLANES], tl], axis=0) * wc_ref[3 - k:4 - k, :]
        dxcn[...] = dxc[0:SUBLANES]
        dp_ref[:, 0:lw] = dxl.astype(BF16)
        dbin_ref[:, 0:lw] += _colsum(dxl)

        gu, dgu_dx = _gelu_and_grad(p_ref[:, 2 * lw:2 * lw + sw])
        gv, dgv_dx = _gelu_and_grad(p_ref[:, 2 * lw + sw:cw])
        xhat, rstd = _ln_stats(gv)
        vn = (xhat * lg_ref[...] + lb_ref[...]).astype(BF16)
        dys = dys_ref[...]
        dmixed = dys * gu
        dmb = dmixed.astype(BF16)
        tpos = lax.broadcasted_iota(jnp.int32, (SGU_BLOCK, SGU_BLOCK), 0) // CHUNK
        spos = lax.broadcasted_iota(jnp.int32, (SGU_BLOCK, SGU_BLOCK), 1) // CHUNK
        causal = spos <= tpos
        mixed_rows, dvn_rows = [], []
        for blk in range(nblk):
            rs = slice(blk * SGU_BLOCK, (blk + 1) * SGU_BLOCK)
            mcols, dcols = [], []
            for g in range(groups):
                cs = slice(g * gw, (g + 1) * gw)
                wm = jnp.where(causal, wsp_ref[g], 0.0).astype(BF16)
                mcols.append(jnp.dot(wm, vn[rs, cs], preferred_element_type=F32) + bsp_ref[:, g:g + 1])
                dcols.append(lax.dot_general(wm, dmb[rs, cs], tn, preferred_element_type=F32))
                dw = lax.dot_general(dmb[rs, cs], vn[rs, cs], nt, preferred_element_type=F32)
                dwsp_ref[g] += jnp.where(causal, dw, 0.0)
                dbsp_ref[:, g:g + 1] += jnp.sum(dmixed[rs, cs], axis=1, keepdims=True)
            mixed_rows.append(jnp.concatenate(mcols, axis=1))
            dvn_rows.append(jnp.concatenate(dcols, axis=1))
        mixed_all = jnp.concatenate(mixed_rows, axis=0) if nblk > 1 else mixed_rows[0]
        dvn = jnp.concatenate(dvn_rows, axis=0) if nblk > 1 else dvn_rows[0]
        du = dys * mixed_all * dgu_dx
        dlg_ref[...] += _colsum(dvn * xhat)
        dlb_ref[...] += _colsum(dvn)
        dv = _ln_bwd(dvn, xhat, rstd, lg_ref[...]) * dgv_dx
        dp_ref[:, 2 * lw:2 * lw + sw] = du.astype(BF16)
        dp_ref[:, 2 * lw + sw:cw] = dv.astype(BF16)
        dbin_ref[:, 2 * lw:2 * lw + sw] += _colsum(du)
        dbin_ref[:, 2 * lw + sw:cw] += _colsum(dv)

    rev = lambda s: n_s - 1 - s
    tile = lambda w: pl.BlockSpec((None, tm, w), lambda b, s: (b, rev(s), 0))
    halo = lambda w: pl.BlockSpec((None, SUBLANES, w), lambda b, s: (b, jnp.maximum(rev(s) * per8 - 1, 0), 0))
    full = lambda shp: pl.BlockSpec(shp, lambda b, s: (0,) * len(shp))
    small = [w_conv, b_conv, w_rg_a, b_rg_a, w_rg_x, b_rg_x, lam, w_sp, b_sp_t, ln_v_g, ln_v_b]
    acc_shapes = [(1, cw), w_conv.shape, b_conv.shape, w_rg_a.shape, b_rg_a.shape, w_rg_x.shape, b_rg_x.shape,
                  lam.shape, w_sp.shape, b_sp_t.shape, ln_v_g.shape, ln_v_b.shape]
    res = pl.pallas_call(
        body, name="mix_bwd", grid=(Bl, n_s),
        in_specs=[tile(cw), halo(lw), tile(lw), halo(lw), tile(lw), tile(sw), pl.BlockSpec(memory_space=pl.ANY)]
                 + [full(w.shape) for w in small],
        out_specs=tuple([tile(cw)] + [full(shp) for shp in acc_shapes]),
        out_shape=tuple([jax.ShapeDtypeStruct((Bl, S, din), BF16)] + [jax.ShapeDtypeStruct(shp, F32) for shp in acc_shapes]),
        input_output_aliases={6: 0},
        scratch_shapes=[pltpu.VMEM((SUBLANES, lw), F32), pltpu.VMEM((SUBLANES, lw), F32), pltpu.VMEM((SUBLANES, lw), F32),
                        pltpu.VMEM((tm, lw), F32), pltpu.VMEM((tm, lw), F32), pltpu.VMEM((tm, lw), F32)],
        compiler_params=_cparams(2, big=True),
    )(proj, proj, hs, hs, dya, dys, dproj, *small)
    return res


def _final_dx(dxp, dh, x, sc1, *, ts):
    Bl, S, D = x.shape

    def body(dxp_ref, dh_ref, x_ref, sc_ref, dx_ref, dsc_ref, dsh_ref):
        @pl.when(pl.program_id(1) == 0)
        def _():
            dsc_ref[...] = jnp.zeros_like(dsc_ref)
            dsh_ref[...] = jnp.zeros_like(dsh_ref)

        dh = dh_ref[...]
        dx_ref[...] = dxp_ref[...] + dh * (1.0 + sc_ref[...])
        dsc_ref[...] += _colsum(dh * x_ref[...])
        dsh_ref[...] += _colsum(dh)

    return pl.pallas_call(
        body, name="final_dx", grid=(Bl, S // ts),
        in_specs=[_tok_spec(ts, D), _tok_spec(ts, D), _tok_spec(ts, D), _brow_spec(D)],
        out_specs=(_tok_spec(ts, D), _brow_spec(D), _brow_spec(D)),
        out_shape=(jax.ShapeDtypeStruct((Bl, S, D), F32), jax.ShapeDtypeStruct((Bl, 1, D), F32),
                   jax.ShapeDtypeStruct((Bl, 1, D), F32)),
        compiler_params=_cparams(2),
    )(dxp, dh, x, sc1)


def _ada_fwd(c_all, w_ada):
    R, D = c_all.shape
    nb = w_ada.shape[1]

    def body(c_ref, w_ref, act_ref, o_ref):
        cv = c_ref[...]
        act = (cv * _sigmoid(cv)).astype(BF16)
        act_ref[...] = act
        o_ref[...] = jnp.dot(act, w_ref[...].astype(BF16), preferred_element_type=F32)

    return pl.pallas_call(
        body, name="ada_fwd",
        out_shape=(jax.ShapeDtypeStruct((R, D), BF16), jax.ShapeDtypeStruct((R, nb), F32)),
        compiler_params=pltpu.CompilerParams(vmem_limit_bytes=VMEM_LIMIT),
    )(c_all, w_ada)


def _ada_bwd(c_act, dmod_cols):
    R, D = c_act.shape
    nb = dmod_cols.shape[1]

    def body(act_ref, d_ref, o_ref):
        o_ref[...] = lax.dot_general(act_ref[...], d_ref[...].astype(BF16), (((0,), (0,)), ((), ())),
                                     preferred_element_type=F32)

    return pl.pallas_call(
        body, name="ada_bwd", out_shape=jax.ShapeDtypeStruct((D, nb), F32),
        compiler_params=pltpu.CompilerParams(vmem_limit_bytes=VMEM_LIMIT),
    )(c_act, dmod_cols)


def _adamw(w, g_slots, m, v, *, tr, name):
    R, C = w.shape
    n_slot = g_slots.shape[0]
    tr = min(tr, R)
    assert R % tr == 0, (name, R, tr)
    c1 = 1.0 / (1.0 - ADAM_B1 ** ADAM_STEP)
    c2 = 1.0 / (1.0 - ADAM_B2 ** ADAM_STEP)

    def body(w_ref, g_ref, m_ref, v_ref, go_ref, d_ref, mo_ref, vo_ref):
        g = g_ref[0].astype(F32)
        for i in range(1, n_slot):
            g = g + g_ref[i].astype(F32)
        mn = ADAM_B1 * m_ref[...] + (1.0 - ADAM_B1) * g
        vn = ADAM_B2 * v_ref[...] + (1.0 - ADAM_B2) * (g * g)
        go_ref[...] = g
        mo_ref[...] = mn
        vo_ref[...] = vn
        d_ref[...] = -ADAM_LR * ((mn * c1) / (jnp.sqrt(vn * c2) + ADAM_EPS) + ADAM_WD * w_ref[...])

    blk = pl.BlockSpec((tr, C), lambda i: (i, 0))
    return pl.pallas_call(
        body, name=name, grid=(R // tr,),
        in_specs=[blk, pl.BlockSpec((n_slot, tr, C), lambda i: (0, i, 0)), blk, blk],
        out_specs=(blk, blk, blk, blk),
        out_shape=tuple(jax.ShapeDtypeStruct((R, C), F32) for _ in range(4)),
        compiler_params=_cparams(1, big=True),
    )(w, g_slots, m, v)


def _sum_slots(g_slots, *, name):
    n_slot, R, C = g_slots.shape

    def body(g_ref, o_ref):
        g = g_ref[0]
        for i in range(1, n_slot):
            g = g + g_ref[i]
        o_ref[...] = g

    return pl.pallas_call(body, name=name, out_shape=jax.ShapeDtypeStruct((R, C), F32),
                          compiler_params=pltpu.CompilerParams(vmem_limit_bytes=VMEM_LIMIT))(g_slots)


SMALL_NAMES = ("b_ada", "b_in", "b_conv", "w_rg_a", "b_rg_a", "w_rg_x", "b_rg_x", "lru_lambda", "w_sp", "b_sp",
               "ln_v_g", "ln_v_b", "ln1_g", "ln1_b", "ln2_g", "ln2_b")
BIG_NAMES = ("w_ada", "w_in", "w_conv", "w_o_lru", "w_o_sgu", "w_out", "w_up", "w_down")
WEIGHT_ORDER = ("w_ada", "b_ada", "w_in", "b_in", "w_conv", "b_conv", "w_rg_a", "b_rg_a", "w_rg_x", "b_rg_x",
                "lru_lambda", "w_sp", "b_sp", "ln_v_g", "ln_v_b", "w_o_lru", "w_o_sgu", "w_out", "ln1_g", "ln1_b",
                "w_up", "w_down", "ln2_g", "ln2_b")


def _pack_small(d):
    flat = jnp.concatenate([d[n].reshape(-1) for n in SMALL_NAMES])
    rows = -(-flat.shape[0] // LANES)
    rows = -(-rows // (N_DEV * SUBLANES)) * (N_DEV * SUBLANES)
    flat = jnp.pad(flat, (0, rows * LANES - flat.shape[0]))
    return flat.reshape(rows, LANES)


def _unpack_small(packed, like):
    flat = packed.reshape(-1)
    out, off = {}, 0
    for n in SMALL_NAMES:
        sz = like[n].size
        out[n] = flat[off:off + sz].reshape(like[n].shape)
        off += sz
    return out


def _blocked_cols(w2d):
    K, N = w2d.shape
    return jnp.transpose(w2d.reshape(K, N_DEV, N // N_DEV), (1, 0, 2))


def _unblock_cols(wb):
    n, K, nb = wb.shape
    return jnp.transpose(wb, (1, 0, 2)).reshape(K, n * nb)


def kernel(x, c, w_ada, b_ada, w_in, b_in, w_conv, b_conv, w_rg_a, b_rg_a, w_rg_x, b_rg_x, lru_lambda, w_sp, b_sp, ln_v_g, ln_v_b, w_o_lru, w_o_sgu, w_out, ln1_g, ln1_b, w_up, w_down, ln2_g, ln2_b, loss_target, m_w_ada, m_b_ada, m_w_in, m_b_in, m_w_conv, m_b_conv, m_w_rg_a, m_b_rg_a, m_w_rg_x, m_b_rg_x, m_lru_lambda, m_w_sp, m_b_sp, m_ln_v_g, m_ln_v_b, m_w_o_lru, m_w_o_sgu, m_w_out, m_ln1_g, m_ln1_b, m_w_up, m_w_down, m_ln2_g, m_ln2_b, v_w_ada, v_b_ada, v_w_in, v_b_in, v_w_conv, v_b_conv, v_w_rg_a, v_b_rg_a, v_w_rg_x, v_b_rg_x, v_lru_lambda, v_w_sp, v_b_sp, v_ln_v_g, v_ln_v_b, v_w_o_lru, v_w_o_sgu, v_w_out, v_ln1_g, v_ln1_b, v_w_up, v_w_down, v_ln2_g, v_ln2_b):
    W = dict(w_ada=w_ada, b_ada=b_ada, w_in=w_in, b_in=b_in, w_conv=w_conv, b_conv=b_conv, w_rg_a=w_rg_a,
             b_rg_a=b_rg_a, w_rg_x=w_rg_x, b_rg_x=b_rg_x, lru_lambda=lru_lambda, w_sp=w_sp, b_sp=b_sp,
             ln_v_g=ln_v_g, ln_v_b=ln_v_b, w_o_lru=w_o_lru, w_o_sgu=w_o_sgu, w_out=w_out, ln1_g=ln1_g, ln1_b=ln1_b,
             w_up=w_up, w_down=w_down, ln2_g=ln2_g, ln2_b=ln2_b)
    Mo = dict(w_ada=m_w_ada, b_ada=m_b_ada, w_in=m_w_in, b_in=m_b_in, w_conv=m_w_conv, b_conv=m_b_conv,
              w_rg_a=m_w_rg_a, b_rg_a=m_b_rg_a, w_rg_x=m_w_rg_x, b_rg_x=m_b_rg_x, lru_lambda=m_lru_lambda,
              w_sp=m_w_sp, b_sp=m_b_sp, ln_v_g=m_ln_v_g, ln_v_b=m_ln_v_b, w_o_lru=m_w_o_lru, w_o_sgu=m_w_o_sgu,
              w_out=m_w_out, ln1_g=m_ln1_g, ln1_b=m_ln1_b, w_up=m_w_up, w_down=m_w_down, ln2_g=m_ln2_g,
              ln2_b=m_ln2_b)
    Vo = dict(w_ada=v_w_ada, b_ada=v_b_ada, w_in=v_w_in, b_in=v_b_in, w_conv=v_w_conv, b_conv=v_b_conv,
              w_rg_a=v_w_rg_a, b_rg_a=v_b_rg_a, w_rg_x=v_w_rg_x, b_rg_x=v_b_rg_x, lru_lambda=v_lru_lambda,
              w_sp=v_w_sp, b_sp=v_b_sp, ln_v_g=v_ln_v_g, ln_v_b=v_ln_v_b, w_o_lru=v_w_o_lru, w_o_sgu=v_w_o_sgu,
              w_out=v_w_out, ln1_g=v_ln1_g, ln1_b=v_ln1_b, w_up=v_w_up, w_down=v_w_down, ln2_g=v_ln2_g,
              ln2_b=v_ln2_b)

    Bl, S, D = x.shape
    T = Bl * S
    lw = b_conv.shape[-1]
    sw = ln_v_g.shape[-1]
    din = b_in.shape[-1]
    dff = w_up.shape[-1] * N_DEV
    ts = min(512, S)
    tmix = min(256, S)
    tmm = min(1024, T)

    c_pad = jnp.pad(c, ((0, SUBLANES - Bl), (0, 0)))
    c_g, wconv_g = _exchange([c_pad, w_conv[0]], True, "xchg_c")
    wconv_full = _unblock_cols(wconv_g)
    c_act, modcols = _ada_fwd(c_g.reshape(N_DEV * SUBLANES, D), w_ada[0])
    (mod_slots,) = _exchange([modcols.reshape(N_DEV, SUBLANES, -1)], False, "xchg_mod")

    wnames = ("win", "wol", "wos", "wout", "wup", "wdown")
    shards = [w_in[0].astype(BF16), w_o_lru[0].astype(BF16), w_o_sgu[0].astype(BF16), w_out[0].astype(BF16),
              w_up[0].astype(BF16), w_down[0].astype(BF16)]
    g_send, g_recv, g_src, g_land, g_tok = _xstart(shards, True, mod_slots, "gather_start")
    gidx = {n: i for i, n in enumerate(wnames)}

    def gathered(n, after):
        i = gidx[n]
        return _xwait(g_src[i], g_land[i], g_send[i], g_recv[i], after, True, "gather_wait_" + n)

    mod = _unblock_cols(mod_slots)[:Bl] + (b_ada + g_tok[0, 0])
    sh1, sc1, gt1, sh2, sc2, gt2 = [mod[:, i * D:(i + 1) * D].reshape(Bl, 1, D) for i in range(6)]

    wa_b, wx_b = w_rg_a[0].astype(BF16), w_rg_x[0].astype(BF16)
    b_sp_t = jnp.transpose(b_sp[0])
    small_mix = (wconv_full, b_conv, wa_b, b_rg_a, wx_b, b_rg_x, lru_lambda, w_sp[0], b_sp_t, ln_v_g, ln_v_b)

    h = _modulate(x, sc1, sh1, ts)
    Win = _unblock_cols(gathered("win", h))
    proj = _mm(h.reshape(T, D), Win, mode="nn", tm=tmm, tn=768, tk=D, outs=[F32],
               extras=[(b_in, "row")], epilogue=lambda acc, ex: (acc + ex[0],), name="mm_proj")
    proj3 = proj.reshape(Bl, S, din)
    hs, ya_pre, ysgu = _mix_fwd(proj3, *small_mix, tm=tmix, lw=lw, sw=sw)
    Wol = gathered("wol", ya_pre).reshape(lw, D)
    Wos = _unblock_cols(gathered("wos", ysgu))
    y_a = _mm(ya_pre.reshape(T, lw), Wol, mode="nn", tm=tmm, tn=512, tk=lw, outs=[F32], name="mm_ya")
    y_b = _mm(ysgu.reshape(T, sw), Wos, mode="nn", tm=tmm, tn=512, tk=sw, outs=[F32], name="mm_yb")
    merged = _merge_fwd(proj3, y_a.reshape(Bl, S, D), y_b.reshape(Bl, S, D), ts=ts, d=D)
    Wout = gathered("wout", merged).reshape(D, D)
    mix = _mm(merged.reshape(T, D), Wout, mode="nn", tm=tmm, tn=512, tk=D, outs=[F32], name="mm_mix")
    mix3 = mix.reshape(Bl, S, D)
    x1, h2 = _ln1_fwd(x, mix3, gt1, ln1_g, ln1_b, sc2, sh2, ts=ts)
    Wup = _unblock_cols(gathered("wup", h2))
    up, act = _mm(h2.reshape(T, D), Wup, mode="nn", tm=tmm, tn=512, tk=D, outs=[F32, BF16],
                  epilogue=lambda acc, ex: (acc, jnp.square(jnp.maximum(acc, 0.0))), name="mm_up")
    Wdown = gathered("wdown", act).reshape(dff, D)
    f = _mm(act, Wdown, mode="nn", tm=tmm, tn=512, tk=1024, outs=[F32], name="mm_down")
    df, dx1p, dgt2, dg2, db2, loss_part = _ln2_loss(x1, f.reshape(Bl, S, D), loss_target, gt2, ln2_g, ln2_b, ts=ts)
    loss = lax.psum(loss_part[0, 0], ("x", "y", "c"))

    def send_grads(parts, name):
        snd, rcv, src, land, tok = _xstart(parts, False, None, name + "_start")
        return [(src[i], land[i], snd[i], rcv[i]) for i in range(len(parts))], tok

    df2 = df.reshape(T, D)
    dup = _mm(df2, Wdown, mode="nt", tm=tmm, tn=512, tk=D, outs=[BF16], extras=[(up, "tile")],
              epilogue=lambda acc, ex: (acc * (2.0 * jnp.maximum(ex[0], 0.0)),), name="mm_dup")
    g_wdown = _mm(act, df2, mode="tn", tm=1024, tn=512, tk=512, outs=[BF16], name="mm_gwdown")
    (x_wdown,), tok = send_grads([g_wdown.reshape(N_DEV, dff // N_DEV, D)], "gx_wdown")
    dh2 = _mm(dup, Wup, mode="nt", tm=tmm, tn=512, tk=1024, outs=[F32], tok=tok, name="mm_dh2")
    g_wup = _mm(h2.reshape(T, D), dup, mode="tn", tm=1024, tn=dff // N_DEV, tk=512, outs=[BF16], out_blocked=True,
                name="mm_gwup")
    (x_wup,), tok = send_grads([g_wup], "gx_wup")
    dxp, dmix, dsc2, dsh2, dgt1, dg1, db1 = _ln1_bwd(dx1p, dh2.reshape(Bl, S, D), x1, x, mix3, sc2 + tok[0, 0], gt1,
                                                     ln1_g, ts=ts)

    dmix2 = dmix.reshape(T, D)
    dmerged = _mm(dmix2, Wout, mode="nt", tm=tmm, tn=512, tk=D, outs=[F32], name="mm_dmerged")
    g_wout = _mm(merged.reshape(T, D), dmix2, mode="tn", tm=1024, tn=512, tk=512, outs=[BF16], name="mm_gwout")
    (x_wout,), tok = send_grads([g_wout.reshape(N_DEV, D // N_DEV, D)], "gx_wout")
    dy_a, dy_b, dproj, dbin_hi = _merge_bwd(dmerged.reshape(Bl, S, D), y_a.reshape(Bl, S, D), y_b.reshape(Bl, S, D),
                                            proj3, ts=ts, d=D)
    dya_pre = _mm(dy_a.reshape(T, D), Wol, mode="nt", tm=tmm, tn=lw // 2, tk=D, outs=[F32], tok=tok, name="mm_dya")
    dysgu = _mm(dy_b.reshape(T, D), Wos, mode="nt", tm=tmm, tn=sw, tk=D, outs=[F32], name="mm_dys")
    g_wol = _mm(ya_pre.reshape(T, lw), dy_a.reshape(T, D), mode="tn", tm=lw, tn=512, tk=512, outs=[BF16],
                name="mm_gwol")
    g_wos = _mm(ysgu.reshape(T, sw), dy_b.reshape(T, D), mode="tn", tm=sw, tn=512, tk=512, outs=[BF16],
                name="mm_gwos")
    (x_wol, x_wos), tok = send_grads([g_wol.reshape(N_DEV, lw // N_DEV, D), _blocked_cols(g_wos)], "gx_wo")
    small_mix_b = (wconv_full, b_conv + tok[0, 0]) + small_mix[2:]
    (dproj, dbin_lo, g_wconv, g_bconv, g_wa, g_ba, g_wx, g_bx, g_lam, g_wsp, g_bsp_t, g_lvg, g_lvb) = _mix_bwd(
        proj3, hs, dya_pre.reshape(Bl, S, lw), dysgu.reshape(Bl, S, sw), dproj, *small_mix_b, tm=tmix, lw=lw, sw=sw)
    dproj2 = dproj.reshape(T, din)
    g_win = _mm(h.reshape(T, D), dproj2, mode="tn", tm=1024, tn=din // N_DEV, tk=512, outs=[BF16], out_blocked=True,
                name="mm_gwin")
    (x_win,), tok = send_grads([g_win], "gx_win")
    dh = _mm(dproj2, Win, mode="nt", tm=tmm, tn=512, tk=din // 4, outs=[F32], tok=tok, name="mm_dh")
    grad_x, dsc1, dsh1 = _final_dx(dxp, dh.reshape(Bl, S, D), x, sc1, ts=ts)

    dmod = jnp.concatenate([dsh1, dsc1, dgt1, dsh2, dsc2, dgt2], axis=-1).reshape(Bl, 6 * D)
    dmod_b = _blocked_cols(jnp.pad(dmod, ((0, SUBLANES - Bl), (0, 0))))
    g_small_local = dict(
        b_ada=jnp.sum(dmod, axis=0, keepdims=True), b_in=jnp.concatenate([dbin_lo, dbin_hi], axis=-1),
        b_conv=g_bconv, w_rg_a=g_wa[None], b_rg_a=g_ba, w_rg_x=g_wx[None], b_rg_x=g_bx, lru_lambda=g_lam,
        w_sp=g_wsp[None], b_sp=jnp.transpose(g_bsp_t)[None], ln_v_g=g_lvg, ln_v_b=g_lvb, ln1_g=dg1, ln1_b=db1,
        ln2_g=dg2, ln2_b=db2)
    gs_packed = _pack_small(g_small_local)
    rows = gs_packed.shape[0]
    parts = [dmod_b, _blocked_cols(g_wconv), gs_packed.reshape(N_DEV, rows // N_DEV, LANES)]
    dmod_s, gwconv_s, gsmall_s = _exchange(parts, False, "xchg_grads")
    gwdown_s = _xwait(*x_wdown, dmod_s, False, "gx_wdown_wait")
    gwup_s = _xwait(*x_wup, dmod_s, False, "gx_wup_wait")
    gwout_s = _xwait(*x_wout, dmod_s, False, "gx_wout_wait")
    gwol_s = _xwait(*x_wol, dmod_s, False, "gx_wol_wait")
    gwos_s = _xwait(*x_wos, dmod_s, False, "gx_wos_wait")
    gwin_s = _xwait(*x_win, dmod_s, False, "gx_win_wait")

    out_g, out_d, out_m, out_v = {}, {}, {}, {}

    def adam(name, g_slots, tr):
        shp = W[name].shape
        w2, m2, v2 = [t.reshape(g_slots.shape[1:]) for t in (W[name], Mo[name], Vo[name])]
        g, d, mn, vn = _adamw(w2, g_slots, m2, v2, tr=tr, name="adam_" + name)
        out_g[name], out_d[name], out_m[name], out_v[name] = [t.reshape(shp) for t in (g, d, mn, vn)]

    g_wada = _ada_bwd(c_act, dmod_s.reshape(N_DEV * SUBLANES, -1))
    adam("w_ada", g_wada[None], 256)
    adam("w_in", gwin_s, 256)
    adam("w_conv", gwconv_s, 8)
    adam("w_o_lru", gwol_s, 160)
    adam("w_o_sgu", gwos_s, 256)
    adam("w_out", gwout_s, 128)
    adam("w_up", gwup_s, 256)
    adam("w_down", gwdown_s, 256)

    g_chunk = _sum_slots(gsmall_s, name="sum_small")
    (gsmall_all,) = _exchange([g_chunk], True, "xchg_small")
    gs, ds, ms, vs = _adamw(_pack_small(W), gsmall_all.reshape(1, rows, LANES), _pack_small(Mo), _pack_small(Vo),
                            tr=rows // N_DEV, name="adam_small")
    for dst, packed in ((out_g, gs), (out_d, ds), (out_m, ms), (out_v, vs)):
        dst.update(_unpack_small(packed, W))

    return (loss, grad_x, *[out_g[n] for n in WEIGHT_ORDER], *[out_d[n] for n in WEIGHT_ORDER],
            *[out_m[n] for n in WEIGHT_ORDER], *[out_v[n] for n in WEIGHT_ORDER])
```

```python
import functools
import math

import jax
import jax.numpy as jnp
from jax import lax
from jax.experimental import pallas as pl
from jax.experimental.pallas import tpu as pltpu

N_DEV = 8
LN_EPS = 1e-5
LRU_C = 8.0
CHUNK = 64
SGU_BLOCK = 128
ALPHA = 2.0 ** 0.25
ADAM_LR = 0.001
ADAM_B1 = 0.9
ADAM_B2 = 0.999
ADAM_EPS = 1e-08
ADAM_WD = 0.01
ADAM_STEP = 10
GELU_K0 = math.sqrt(2.0 / math.pi)
GELU_K1 = 0.044715

SUBLANES = 8
LANES = 128
VMEM_LIMIT = 56 * 1024 * 1024

F32 = jnp.float32
BF16 = jnp.bfloat16
MESH = pl.DeviceIdType.MESH


def _cparams(n_axes, big=False):
    return pltpu.CompilerParams(dimension_semantics=("arbitrary",) * n_axes,
                                vmem_limit_bytes=VMEM_LIMIT if big else None)


def _sigmoid(x):
    return 1.0 / (1.0 + jnp.exp(-x))


def _gelu(x):
    t = jnp.tanh(GELU_K0 * (x + GELU_K1 * (x * x * x)))
    return 0.5 * x * (1.0 + t)


def _gelu_and_grad(x):
    x2 = x * x
    t = jnp.tanh(GELU_K0 * (x + GELU_K1 * (x2 * x)))
    g = 0.5 * x * (1.0 + t)
    dg = 0.5 * (1.0 + t) + 0.5 * x * (1.0 - t * t) * (GELU_K0 * (1.0 + 3.0 * GELU_K1 * x2))
    return g, dg


def _expm1(x):
    p = x * (1.0 + x * (1.0 / 2.0) * (1.0 + x * (1.0 / 3.0) * (1.0 + x * (1.0 / 4.0) * (
        1.0 + x * (1.0 / 5.0) * (1.0 + x * (1.0 / 6.0) * (1.0 + x * (1.0 / 7.0)))))))
    return jnp.where(jnp.abs(x) < 0.3, p, jnp.exp(x) - 1.0)


def _log1p_pos(e):
    p = e * (1.0 - e * (1.0 / 2.0) + e * e * (1.0 / 3.0) - e * e * e * (1.0 / 4.0))
    return jnp.where(e < 1e-2, p, jnp.log(1.0 + e))


def _ln_stats(z):
    mu = jnp.mean(z, axis=-1, keepdims=True)
    zc = z - mu
    var = jnp.mean(zc * zc, axis=-1, keepdims=True)
    rstd = lax.rsqrt(var + LN_EPS)
    return zc * rstd, rstd


def _ln_bwd(dy, xhat, rstd, g):
    dxh = dy * g
    m1 = jnp.mean(dxh, axis=-1, keepdims=True)
    m2 = jnp.mean(dxh * xhat, axis=-1, keepdims=True)
    return rstd * (dxh - m1 - xhat * m2)


def _colsum(v):
    return jnp.sum(v, axis=0, keepdims=True)


def _first_step():
    return jnp.logical_and(pl.program_id(0) == 0, pl.program_id(1) == 0)


def _exchange(arrs, gather, name):
    n = len(arrs)
    n_peer = N_DEV - 1

    def body(*refs):
        ins, outs = refs[:n], refs[n:2 * n]
        send_sems, recv_sems, loc_sems = refs[2 * n:]
        x, y, c = lax.axis_index("x"), lax.axis_index("y"), lax.axis_index("c")
        me = 4 * x + 2 * y + c
        started = []
        for a in range(n):
            src_me = ins[a] if gather else ins[a].at[me]
            lc = pltpu.make_async_copy(src_me, outs[a].at[me], loc_sems.at[a])
            lc.start()
            started.append((lc, None))
        for p in range(1, N_DEV):
            px, py, pc = x ^ ((p >> 2) & 1), y ^ ((p >> 1) & 1), c ^ (p & 1)
            peer = 4 * px + 2 * py + pc
            for a in range(n):
                k = a * n_peer + (p - 1)
                src = ins[a] if gather else ins[a].at[peer]
                cp = pltpu.make_async_remote_copy(src_ref=src, dst_ref=outs[a].at[me],
                                                  send_sem=send_sems.at[k], recv_sem=recv_sems.at[k],
                                                  device_id=(px, py, pc), device_id_type=MESH)
                cp.start()
                rc = pltpu.make_async_remote_copy(src_ref=src, dst_ref=outs[a].at[peer],
                                                  send_sem=send_sems.at[k], recv_sem=recv_sems.at[k],
                                                  device_id=(px, py, pc), device_id_type=MESH)
                started.append((cp, rc))
        for cp, rc in started:
            if rc is None:
                cp.wait()
            else:
                cp.wait_send()
                rc.wait_recv()

    hbm = pl.BlockSpec(memory_space=pltpu.HBM)
    out_shape = tuple(
        jax.ShapeDtypeStruct(((N_DEV,) + a.shape) if gather else a.shape, a.dtype) for a in arrs)
    return pl.pallas_call(
        body, name=name, out_shape=out_shape,
        in_specs=[hbm] * n, out_specs=tuple([hbm] * n),
        scratch_shapes=[pltpu.SemaphoreType.DMA((n * n_peer,)), pltpu.SemaphoreType.DMA((n * n_peer,)),
                        pltpu.SemaphoreType.DMA((n,))],
        compiler_params=pltpu.CompilerParams(has_side_effects=True),
    )(*arrs)


_HBM = pl.BlockSpec(memory_space=pltpu.HBM)
_SEM = pl.BlockSpec(memory_space=pltpu.SEMAPHORE)
_EFFECT = pltpu.SideEffectType.DATAFLOW_SIDE_EFFECTING


def _peer_of(p):
    x, y, c = lax.axis_index("x"), lax.axis_index("y"), lax.axis_index("c")
    px, py, pc = x ^ ((p >> 2) & 1), y ^ ((p >> 1) & 1), c ^ (p & 1)
    return (px, py, pc), 4 * px + 2 * py + pc


def _xstart(srcs, gather, after, name):
    n = len(srcs)
    lands = [lax.empty(((N_DEV,) + t.shape) if gather else t.shape, t.dtype) for t in srcs]
    n_after = 0 if after is None else 1

    def body(*refs):
        src_refs, land_refs = refs[:n], refs[n:2 * n]
        refs = refs[n_after:]
        send_sems, recv_sems = refs[2 * n:3 * n], refs[3 * n:4 * n]
        token = refs[6 * n]
        me = 4 * lax.axis_index("x") + 2 * lax.axis_index("y") + lax.axis_index("c")
        for a in range(n):
            for p in range(1, N_DEV):
                dev, peer = _peer_of(p)
                pltpu.make_async_remote_copy(
                    src_ref=src_refs[a] if gather else src_refs[a].at[peer], dst_ref=land_refs[a].at[me],
                    send_sem=send_sems[a].at[p - 1], recv_sem=recv_sems[a].at[p - 1],
                    device_id=dev, device_id_type=MESH).start()
        token[...] = jnp.zeros_like(token)

    sems = tuple(pltpu.SemaphoreType.DMA((N_DEV - 1,)) for _ in range(2 * n))
    thru = tuple(pltpu.HBM(t.shape, t.dtype) for t in list(srcs) + list(lands))
    res = pl.pallas_call(
        body, name=name,
        out_shape=sems + thru + (jax.ShapeDtypeStruct((SUBLANES, LANES), F32),),
        in_specs=[_HBM] * (2 * n) + [pl.BlockSpec(memory_space=pl.ANY)] * n_after,
        out_specs=tuple([_SEM] * (2 * n) + [_HBM] * (2 * n) + [pl.BlockSpec(memory_space=pltpu.VMEM)]),
        input_output_aliases={i: 2 * n + i for i in range(2 * n)},
        compiler_params=pltpu.CompilerParams(has_side_effects=_EFFECT),
    )(*[pltpu.with_memory_space_constraint(t, pltpu.HBM) for t in list(srcs) + list(lands)],
      *([after] if n_after else []))
    return res[:n], res[n:2 * n], res[2 * n:3 * n], res[3 * n:4 * n], res[4 * n]


def _xwait(src, land, send_sem, recv_sem, after, gather, name):
    def body(src_ref, land_ref, send_ref, recv_ref, after_ref, src_dead, land_out):
        del after_ref, src_dead, land_out
        for p in range(1, N_DEV):
            dev, peer = _peer_of(p)
            cp = pltpu.make_async_remote_copy(
                src_ref=src_ref if gather else src_ref.at[peer], dst_ref=land_ref.at[peer],
                send_sem=send_ref.at[p - 1], recv_sem=recv_ref.at[p - 1], device_id=dev, device_id_type=MESH)
            cp.wait_send()
            cp.wait_recv()

    src_done, landed = pl.pallas_call(
        body, name=name, out_shape=(pltpu.HBM(src.shape, src.dtype), pltpu.HBM(land.shape, land.dtype)),
        in_specs=[_HBM, _HBM, _SEM, _SEM, pl.BlockSpec(memory_space=pl.ANY)], out_specs=(_HBM, _HBM),
        input_output_aliases={0: 0, 1: 1},
        compiler_params=pltpu.CompilerParams(has_side_effects=_EFFECT),
    )(src, land, send_sem, recv_sem, after)
    me = 4 * lax.axis_index("x") + 2 * lax.axis_index("y") + lax.axis_index("c")
    own = src_done if gather else lax.dynamic_index_in_dim(src_done, me, 0, keepdims=False)
    return lax.dynamic_update_slice(landed, own[None], (me,) + (0,) * own.ndim)


def _mm(a, b, *, mode, tm, tn, tk, outs, epilogue=None, extras=(), nb=None, tok=None, name):
    if mode == "nn":
        (M, K), (_, N) = a.shape, b.shape
    elif mode == "nt":
        (M, K), (N, _) = a.shape, b.shape
    else:
        (K, M), (_, N) = a.shape, b.shape
    tm, tn, tk = min(tm, M), min(tn, N), min(tk, K)
    assert M % tm == 0 and N % tn == 0 and K % tk == 0, (name, M, N, K, tm, tn, tk)
    if mode == "nn":
        a_spec = pl.BlockSpec((tm, tk), lambda i, j, k: (i, k))
        b_spec = pl.BlockSpec((tk, tn), lambda i, j, k: (k, j))
        dims = (((1,), (0,)), ((), ()))
    elif mode == "nt":
        a_spec = pl.BlockSpec((tm, tk), lambda i, j, k: (i, k))
        b_spec = pl.BlockSpec((tn, tk), lambda i, j, k: (j, k))
        dims = (((1,), (1,)), ((), ()))
    else:
        a_spec = pl.BlockSpec((tk, tm), lambda i, j, k: (k, i))
        b_spec = pl.BlockSpec((tk, tn), lambda i, j, k: (k, j))
        dims = (((0,), (0,)), ((), ()))
    nk = K // tk
    n_ex, n_out = len(extras), len(outs)
    n_tok = 0 if tok is None else 1
    nbytes = lambda d: jnp.dtype(d).itemsize
    vmem_est = (2 * (tm * tk * nbytes(a.dtype) + tk * tn * nbytes(b.dtype)
                     + sum(tm * tn * nbytes(e.dtype) for e, kind in extras if kind == "tile")
                     + sum(tm * tn * nbytes(d) for d in outs)) + tm * tn * 4)
    assert vmem_est <= VMEM_LIMIT, (name, vmem_est)
    if epilogue is None:
        epilogue = lambda acc, ex: tuple(acc.astype(d) for d in outs)

    def body(a_ref, b_ref, *refs):
        refs = refs[n_tok:]
        ex_refs, out_refs = refs[:n_ex], refs[n_ex:n_ex + n_out]

        def finish(acc):
            res = epilogue(acc, [r[...] for r in ex_refs])
            for o_ref, v in zip(out_refs, res):
                if nb is None:
                    o_ref[...] = v.astype(o_ref.dtype)
                else:
                    for q in range(tn // nb):
                        o_ref[q] = v[:, q * nb:(q + 1) * nb].astype(o_ref.dtype)

        part = lax.dot_general(a_ref[...], b_ref[...], dims, preferred_element_type=F32)
        if nk == 1:
            finish(part)
        else:
            acc_ref = refs[n_ex + n_out]
            k = pl.program_id(2)

            @pl.when(k == 0)
            def _():
                acc_ref[...] = part

            @pl.when(k > 0)
            def _():
                acc_ref[...] += part

            @pl.when(k == nk - 1)
            def _():
                finish(acc_ref[...])

    ex_specs = [pl.BlockSpec((tm, tn), lambda i, j, k: (i, j)) if kind == "tile"
                else pl.BlockSpec((1, tn), lambda i, j, k: (0, j)) for _, kind in extras]
    if nb is not None:
        assert tn % nb == 0, (name, tn, nb)
        o_spec = pl.BlockSpec((tn // nb, tm, nb), lambda i, j, k: (j, i, 0))
        o_shape = (N // nb, M, nb)
    else:
        o_spec = pl.BlockSpec((tm, tn), lambda i, j, k: (i, j))
        o_shape = (M, N)
    res = pl.pallas_call(
        body, name=name, grid=(M // tm, N // tn, nk),
        in_specs=[a_spec, b_spec] + [pl.BlockSpec((SUBLANES, LANES), lambda i, j, k: (0, 0))] * n_tok + ex_specs,
        out_specs=tuple([o_spec] * n_out),
        out_shape=tuple(jax.ShapeDtypeStruct(o_shape, d) for d in outs),
        scratch_shapes=[pltpu.VMEM((tm, tn), F32)] if nk > 1 else [],
        compiler_params=_cparams(3, big=True),
    )(a, b, *([tok] if n_tok else []), *[e for e, _ in extras])
    return res[0] if n_out == 1 else res


def _tok_spec(ts, width, col_block=0):
    return pl.BlockSpec((None, ts, width), lambda b, s: (b, s, col_block))


def _brow_spec(width):
    return pl.BlockSpec((None, 1, width), lambda b, s: (b, 0, 0))


def _vec_spec(width):
    return pl.BlockSpec((1, width), lambda b, s: (0, 0))


def _modulate(x, sc, sh, ts):
    Bl, S, D = x.shape

    def body(x_ref, sc_ref, sh_ref, o_ref):
        o_ref[...] = (x_ref[...] * (1.0 + sc_ref[...]) + sh_ref[...]).astype(BF16)

    return pl.pallas_call(
        body, name="modulate", grid=(Bl, S // ts),
        in_specs=[_tok_spec(ts, D), _brow_spec(D), _brow_spec(D)],
        out_specs=_tok_spec(ts, D), out_shape=jax.ShapeDtypeStruct((Bl, S, D), BF16),
        compiler_params=_cparams(2),
    )(x, sc, sh)


def _mix_fwd(proj, w_conv, b_conv, w_rg_a, b_rg_a, w_rg_x, b_rg_x, lam, w_sp, b_sp_t, ln_v_g, ln_v_b, *, tm, lw, sw):
    Bl, S, _ = proj.shape
    heads, hd = w_rg_a.shape[0], w_rg_a.shape[1]
    groups = w_sp.shape[0]
    cw = 2 * lw + 2 * sw
    nblk = tm // SGU_BLOCK

    def body(p_ref, wc_ref, bc_ref, wa_ref, ba_ref, wx_ref, bx_ref, lam_ref, wsp_ref, bsp_ref, lg_ref, lb_ref,
             hs_ref, ya_ref, ys_ref, xprev, hcar, a_scr, b_scr):
        s = pl.program_id(1)

        @pl.when(s == 0)
        def _():
            xprev[...] = jnp.zeros_like(xprev)
            hcar[...] = jnp.zeros_like(hcar)

        xl = p_ref[:, 0:lw].astype(F32)
        gl = p_ref[:, lw:2 * lw].astype(F32)
        row8 = lax.broadcasted_iota(jnp.int32, (SUBLANES, lw), 0)
        rowm = lax.broadcasted_iota(jnp.int32, (tm, lw), 0) & (SUBLANES - 1)

        prev = xprev[...]
        xc = xl * wc_ref[3:4, :] + bc_ref[...]
        for k in (1, 2, 3):
            xr = pltpu.roll(xl, k, 0)
            head = jnp.where(row8 < k, pltpu.roll(prev, k, 0), xr[0:SUBLANES])
            xs = jnp.concatenate([head, xr[SUBLANES:]], axis=0)
            xc = xc + xs * wc_ref[3 - k:4 - k, :]
        xprev[...] = xl[tm - SUBLANES:tm]

        xcb = xc.astype(BF16)
        pa = jnp.concatenate([jnp.dot(xcb[:, h * hd:(h + 1) * hd], wa_ref[h], preferred_element_type=F32)
                              for h in range(heads)], axis=1) + ba_ref[...]
        px = jnp.concatenate([jnp.dot(xcb[:, h * hd:(h + 1) * hd], wx_ref[h], preferred_element_type=F32)
                              for h in range(heads)], axis=1) + bx_ref[...]
        r = _sigmoid(pa)
        ig = _sigmoid(px)
        nl = -lam_ref[...]
        big_l = -LRU_C * (jnp.maximum(nl, 0.0) + _log1p_pos(jnp.exp(-jnp.abs(nl))))
        la = big_l * r
        a = jnp.exp(la)
        bin_ = jnp.sqrt(-_expm1(2.0 * la)) * (ig * xc)

        for d in (1, 2, 4):
            a_sh = pltpu.roll(a, d, 0)
            b_sh = pltpu.roll(bin_, d, 0)
            msk = rowm >= d
            bin_ = jnp.where(msk, a * b_sh + bin_, bin_)
            a = jnp.where(msk, a * a_sh, a)
        a_scr[...] = a
        b_scr[...] = bin_

        def grp(g, carry):
            off = pl.multiple_of(g * SUBLANES, SUBLANES)
            h = b_scr[pl.ds(off, SUBLANES), :] + a_scr[pl.ds(off, SUBLANES), :] * carry
            hs_ref[pl.ds(off, SUBLANES), :] = h
            return jnp.broadcast_to(h[SUBLANES - 1:SUBLANES, :], h.shape)

        hcar[...] = lax.fori_loop(0, tm // SUBLANES, grp, hcar[...])
        ya_ref[...] = (hs_ref[...] * _gelu(gl)).astype(BF16)

        gu = _gelu(p_ref[:, 2 * lw:2 * lw + sw].astype(F32))
        gv = _gelu(p_ref[:, 2 * lw + sw:cw].astype(F32))
        xhat, _ = _ln_stats(gv)
        vn = (xhat * lg_ref[...] + lb_ref[...]).astype(BF16)
        tpos = lax.broadcasted_iota(jnp.int32, (SGU_BLOCK, SGU_BLOCK), 0) // CHUNK
        spos = lax.broadcasted_iota(jnp.int32, (SGU_BLOCK, SGU_BLOCK), 1) // CHUNK
        gw = sw // groups
        rows_out = []
        for blk in range(nblk):
            r0 = blk * SGU_BLOCK
            cols = []
            for g in range(groups):
                wm = jnp.where(spos <= tpos, wsp_ref[g], 0.0).astype(BF16)
                mixed = jnp.dot(wm, vn[r0:r0 + SGU_BLOCK, g * gw:(g + 1) * gw], preferred_element_type=F32)
                cols.append(mixed + bsp_ref[:, g:g + 1])
            rows_out.append(jnp.concatenate(cols, axis=1))
        mixed_all = jnp.concatenate(rows_out, axis=0) if nblk > 1 else rows_out[0]
        ys_ref[...] = (gu * mixed_all).astype(BF16)

    full = lambda shp: pl.BlockSpec(shp, lambda b, s: (0,) * len(shp))
    return pl.pallas_call(
        body, name="mix_fwd", grid=(Bl, S // tm),
        in_specs=[_tok_spec(tm, cw), full(w_conv.shape), full(b_conv.shape), full(w_rg_a.shape), full(b_rg_a.shape),
                  full(w_rg_x.shape), full(b_rg_x.shape), full(lam.shape), full(w_sp.shape), full(b_sp_t.shape),
                  full(ln_v_g.shape), full(ln_v_b.shape)],
        out_specs=(_tok_spec(tm, lw), _tok_spec(tm, lw), _tok_spec(tm, sw)),
        out_shape=(jax.ShapeDtypeStruct((Bl, S, lw), F32), jax.ShapeDtypeStruct((Bl, S, lw), BF16),
                   jax.ShapeDtypeStruct((Bl, S, sw), BF16)),
        scratch_shapes=[pltpu.VMEM((SUBLANES, lw), F32), pltpu.VMEM((SUBLANES, lw), F32),
                        pltpu.VMEM((tm, lw), F32), pltpu.VMEM((tm, lw), F32)],
        compiler_params=_cparams(2, big=True),
    )(proj, w_conv, b_conv, w_rg_a, b_rg_a, w_rg_x, b_rg_x, lam, w_sp, b_sp_t, ln_v_g, ln_v_b)


def _merge_fwd(proj, y_a, y_b, *, ts, d):
    Bl, S, din = proj.shape
    gcol = (din - 2 * d) // (2 * d)
    assert gcol * 2 * d == din - 2 * d

    def body(g_ref, ya_ref, yb_ref, o_ref):
        sa = _sigmoid(g_ref[:, 0:d].astype(F32))
        sb = _sigmoid(g_ref[:, d:2 * d].astype(F32))
        o_ref[...] = (sa * ya_ref[...].astype(F32) + sb * yb_ref[...].astype(F32)).astype(BF16)

    return pl.pallas_call(
        body, name="merge_fwd", grid=(Bl, S // ts),
        in_specs=[_tok_spec(ts, 2 * d, gcol), _tok_spec(ts, d), _tok_spec(ts, d)],
        out_specs=_tok_spec(ts, d), out_shape=jax.ShapeDtypeStruct((Bl, S, d), BF16),
        compiler_params=_cparams(2),
    )(proj, y_a, y_b)


def _ln1_fwd(x, mix, gt1, g1, b1, sc2, sh2, *, ts):
    Bl, S, D = x.shape

    def body(x_ref, mix_ref, gt_ref, g_ref, b_ref, sc_ref, sh_ref, x1_ref, h2_ref):
        z = ALPHA * x_ref[...] + (1.0 + gt_ref[...]) * mix_ref[...].astype(F32)
        xhat, _ = _ln_stats(z)
        x1 = xhat * g_ref[...] + b_ref[...]
        x1_ref[...] = x1
        h2_ref[...] = (x1 * (1.0 + sc_ref[...]) + sh_ref[...]).astype(BF16)

    return pl.pallas_call(
        body, name="ln1_fwd", grid=(Bl, S // ts),
        in_specs=[_tok_spec(ts, D), _tok_spec(ts, D), _brow_spec(D), _vec_spec(D), _vec_spec(D), _brow_spec(D),
                  _brow_spec(D)],
        out_specs=(_tok_spec(ts, D), _tok_spec(ts, D)),
        out_shape=(jax.ShapeDtypeStruct((Bl, S, D), F32), jax.ShapeDtypeStruct((Bl, S, D), BF16)),
        compiler_params=_cparams(2),
    )(x, mix, gt1, g1, b1, sc2, sh2)


def _ln2_loss(x1, f, tgt, gt2, g2, b2, *, ts):
    Bl, S, D = x1.shape

    def body(x1_ref, f_ref, t_ref, gt_ref, g_ref, b_ref, df_ref, dx1_ref, dgt_ref, dg_ref, db_ref, loss_ref):
        s = pl.program_id(1)

        @pl.when(_first_step())
        def _():
            dg_ref[...] = jnp.zeros_like(dg_ref)
            db_ref[...] = jnp.zeros_like(db_ref)
            loss_ref[...] = jnp.zeros_like(loss_ref)

        @pl.when(s == 0)
        def _():
            dgt_ref[...] = jnp.zeros_like(dgt_ref)

        fv = f_ref[...]
        z = ALPHA * x1_ref[...] + (1.0 + gt_ref[...]) * fv
        xhat, rstd = _ln_stats(z)
        x2 = xhat * g_ref[...] + b_ref[...]
        err = x2 - t_ref[...]
        loss_ref[...] += 0.5 * jnp.sum(jnp.mean(err * err, axis=-1, keepdims=True))
        dy = err * (1.0 / D)
        dg_ref[...] += _colsum(dy * xhat)
        db_ref[...] += _colsum(dy)
        dz = _ln_bwd(dy, xhat, rstd, g_ref[...])
        dx1_ref[...] = ALPHA * dz
        dgt_ref[...] += _colsum(dz * fv)
        df_ref[...] = (dz * (1.0 + gt_ref[...])).astype(BF16)

    return pl.pallas_call(
        body, name="ln2_loss", grid=(Bl, S // ts),
        in_specs=[_tok_spec(ts, D), _tok_spec(ts, D), _tok_spec(ts, D), _brow_spec(D), _vec_spec(D), _vec_spec(D)],
        out_specs=(_tok_spec(ts, D), _tok_spec(ts, D), _brow_spec(D), _vec_spec(D), _vec_spec(D),
                   pl.BlockSpec((SUBLANES, LANES), lambda b, s: (0, 0))),
        out_shape=(jax.ShapeDtypeStruct((Bl, S, D), BF16), jax.ShapeDtypeStruct((Bl, S, D), F32),
                   jax.ShapeDtypeStruct((Bl, 1, D), F32), jax.ShapeDtypeStruct((1, D), F32),
                   jax.ShapeDtypeStruct((1, D), F32), jax.ShapeDtypeStruct((SUBLANES, LANES), F32)),
        compiler_params=_cparams(2),
    )(x1, f, tgt, gt2, g2, b2)


def _ln1_bwd(dx1p, dh2, x1, x, mix, sc2, gt1, g1, *, ts):
    Bl, S, D = x.shape

    def body(dx1p_ref, dh2_ref, x1_ref, x_ref, mix_ref, sc_ref, gt_ref, g_ref,
             dxp_ref, dmix_ref, dsc_ref, dsh_ref, dgt_ref, dg_ref, db_ref):
        s = pl.program_id(1)

        @pl.when(_first_step())
        def _():
            dg_ref[...] = jnp.zeros_like(dg_ref)
            db_ref[...] = jnp.zeros_like(db_ref)

        @pl.when(s == 0)
        def _():
            dsc_ref[...] = jnp.zeros_like(dsc_ref)
            dsh_ref[...] = jnp.zeros_like(dsh_ref)
            dgt_ref[...] = jnp.zeros_like(dgt_ref)

        dh2 = dh2_ref[...].astype(F32)
        mixv = mix_ref[...].astype(F32)
        dsc_ref[...] += _colsum(dh2 * x1_ref[...])
        dsh_ref[...] += _colsum(dh2)
        dx1 = dx1p_ref[...] + dh2 * (1.0 + sc_ref[...])
        z = ALPHA * x_ref[...] + (1.0 + gt_ref[...]) * mixv
        xhat, rstd = _ln_stats(z)
        dg_ref[...] += _colsum(dx1 * xhat)
        db_ref[...] += _colsum(dx1)
        dz = _ln_bwd(dx1, xhat, rstd, g_ref[...])
        dxp_ref[...] = ALPHA * dz
        dgt_ref[...] += _colsum(dz * mixv)
        dmix_ref[...] = (dz * (1.0 + gt_ref[...])).astype(BF16)

    return pl.pallas_call(
        body, name="ln1_bwd", grid=(Bl, S // ts),
        in_specs=[_tok_spec(ts, D)] * 5 + [_brow_spec(D), _brow_spec(D), _vec_spec(D)],
        out_specs=(_tok_spec(ts, D), _tok_spec(ts, D), _brow_spec(D), _brow_spec(D), _brow_spec(D), _vec_spec(D),
                   _vec_spec(D)),
        out_shape=(jax.ShapeDtypeStruct((Bl, S, D), F32), jax.ShapeDtypeStruct((Bl, S, D), BF16),
                   jax.ShapeDtypeStruct((Bl, 1, D), F32), jax.ShapeDtypeStruct((Bl, 1, D), F32),
                   jax.ShapeDtypeStruct((Bl, 1, D), F32), jax.ShapeDtypeStruct((1, D), F32),
                   jax.ShapeDtypeStruct((1, D), F32)),
        compiler_params=_cparams(2),
    )(dx1p, dh2, x1, x, mix, sc2, gt1, g1)


def _merge_bwd(dmerged, y_a, y_b, proj, *, ts, d):
    Bl, S, din = proj.shape
    gcol = (din - 2 * d) // (2 * d)

    def body(dm_ref, ya_ref, yb_ref, g_ref, dya_ref, dyb_ref, dp_ref, db_ref):
        @pl.when(_first_step())
        def _():
            db_ref[...] = jnp.zeros_like(db_ref)

        dm = dm_ref[...].astype(F32)
        sa = _sigmoid(g_ref[:, 0:d].astype(F32))
        sb = _sigmoid(g_ref[:, d:2 * d].astype(F32))
        dya_ref[...] = (dm * sa).astype(BF16)
        dyb_ref[...] = (dm * sb).astype(BF16)
        dga = dm * ya_ref[...].astype(F32) * sa * (1.0 - sa)
        dgb = dm * yb_ref[...].astype(F32) * sb * (1.0 - sb)
        dp_ref[:, 0:d] = dga.astype(BF16)
        dp_ref[:, d:2 * d] = dgb.astype(BF16)
        db_ref[:, 0:d] += _colsum(dga)
        db_ref[:, d:2 * d] += _colsum(dgb)

    return pl.pallas_call(
        body, name="merge_bwd", grid=(Bl, S // ts),
        in_specs=[_tok_spec(ts, d), _tok_spec(ts, d), _tok_spec(ts, d), _tok_spec(ts, 2 * d, gcol)],
        out_specs=(_tok_spec(ts, d), _tok_spec(ts, d), _tok_spec(ts, 2 * d, gcol), _vec_spec(2 * d)),
        out_shape=(jax.ShapeDtypeStruct((Bl, S, d), BF16), jax.ShapeDtypeStruct((Bl, S, d), BF16),
                   jax.ShapeDtypeStruct((Bl, S, din), BF16), jax.ShapeDtypeStruct((1, 2 * d), F32)),
        compiler_params=_cparams(2),
    )(dmerged, y_a, y_b, proj)


def _mix_bwd(proj, hs, dya, dys, dproj, w_conv, b_conv, w_rg_a, b_rg_a, w_rg_x, b_rg_x, lam, w_sp, b_sp_t,
             ln_v_g, ln_v_b, *, tm, lw, sw):
    Bl, S, din = proj.shape
    heads, hd = w_rg_a.shape[0], w_rg_a.shape[1]
    groups = w_sp.shape[0]
    gw = sw // groups
    cw = 2 * lw + 2 * sw
    nblk = tm // SGU_BLOCK
    n_s = S // tm
    per8 = tm // SUBLANES
    halo_rows = 2 * SUBLANES

    def body(p_ref, xh_ref, hs_ref, hh_ref, dya_ref, dys_ref, dpin_ref,
             wc_ref, bc_ref, wa_ref, ba_ref, wx_ref, bx_ref, lam_ref, wsp_ref, bsp_ref, lg_ref, lb_ref,
             dp_ref, dbin_ref, dwc_ref, dbc_ref, dwa_ref, dba_ref, dwx_ref, dbx_ref, dlam_ref, dwsp_ref, dbsp_ref,
             dlg_ref, dlb_ref,
             dhcar, acar, dxcn, a_scr, b_scr, g_scr):
        del dpin_ref
        sr = pl.program_id(1)
        first_tile = sr == n_s - 1

        @pl.when(_first_step())
        def _():
            for ref in (dbin_ref, dwc_ref, dbc_ref, dwa_ref, dba_ref, dwx_ref, dbx_ref, dlam_ref, dwsp_ref, dbsp_ref,
                        dlg_ref, dlb_ref):
                ref[...] = jnp.zeros_like(ref)

        @pl.when(sr == 0)
        def _():
            dhcar[...] = jnp.zeros_like(dhcar)
            acar[...] = jnp.zeros_like(acar)
            dxcn[...] = jnp.zeros_like(dxcn)

        keep = jnp.where(first_tile, 0.0, 1.0)
        xl = p_ref[:, 0:lw].astype(F32)
        gl = p_ref[:, lw:2 * lw].astype(F32)
        row8 = lax.broadcasted_iota(jnp.int32, (SUBLANES, lw), 0)
        rowm = lax.broadcasted_iota(jnp.int32, (tm, lw), 0) & (SUBLANES - 1)

        prev = xh_ref[...].astype(F32)[halo_rows - SUBLANES:halo_rows] * keep
        xsh = [xl]
        for k in (1, 2, 3):
            xr = pltpu.roll(xl, k, 0)
            head = jnp.where(row8 < k, pltpu.roll(prev, k, 0), xr[0:SUBLANES])
            xsh.append(jnp.concatenate([head, xr[SUBLANES:]], axis=0))
        xc = bc_ref[...] + xsh[0] * wc_ref[3:4, :]
        for k in (1, 2, 3):
            xc = xc + xsh[k] * wc_ref[3 - k:4 - k, :]
        xcb = xc.astype(BF16)
        pa = jnp.concatenate([jnp.dot(xcb[:, h * hd:(h + 1) * hd], wa_ref[h], preferred_element_type=F32)
                              for h in range(heads)], axis=1) + ba_ref[...]
        px = jnp.concatenate([jnp.dot(xcb[:, h * hd:(h + 1) * hd], wx_ref[h], preferred_element_type=F32)
                              for h in range(heads)], axis=1) + bx_ref[...]
        r = _sigmoid(pa)
        ig = _sigmoid(px)
        nl = -lam_ref[...]
        big_l = -LRU_C * (jnp.maximum(nl, 0.0) + _log1p_pos(jnp.exp(-jnp.abs(nl))))
        la = big_l * r
        a = jnp.exp(la)
        msq = -_expm1(2.0 * la)
        m = jnp.sqrt(msq)
        hsv = hs_ref[...]
        ggl, dggl = _gelu_and_grad(gl)

        dyav = dya_ref[...]
        dhs = dyav * ggl
        dp_ref[:, lw:2 * lw] = (dyav * hsv * dggl).astype(BF16)
        dbin_ref[:, lw:2 * lw] += _colsum(dyav * hsv * dggl)
        a_up = pltpu.roll(a, tm - 1, 0)
        tail = jnp.where(row8 == SUBLANES - 1, acar[...], a_up[tm - SUBLANES:tm])
        an = jnp.concatenate([a_up[:tm - SUBLANES], tail], axis=0)
        acar[...] = jnp.broadcast_to(a[0:1, :], (SUBLANES, lw))
        bb = dhs
        for d in (1, 2, 4):
            a_sh = pltpu.roll(an, tm - d, 0)
            b_sh = pltpu.roll(bb, tm - d, 0)
            msk = rowm < SUBLANES - d
            bb = jnp.where(msk, an * b_sh + bb, bb)
            an = jnp.where(msk, an * a_sh, an)
        a_scr[...] = an
        b_scr[...] = bb

        def grp(i, carry):
            off = pl.multiple_of((per8 - 1 - i) * SUBLANES, SUBLANES)
            g = b_scr[pl.ds(off, SUBLANES), :] + a_scr[pl.ds(off, SUBLANES), :] * carry
            g_scr[pl.ds(off, SUBLANES), :] = g
            return jnp.broadcast_to(g[0:1, :], g.shape)

        dhcar[...] = lax.fori_loop(0, per8, grp, dhcar[...])
        dh = g_scr[...]

        hr = pltpu.roll(hsv, 1, 0)
        hhead = jnp.where(row8 < 1, pltpu.roll(hh_ref[...] * keep, 1, 0), hr[0:SUBLANES])
        hprev = jnp.concatenate([hhead, hr[SUBLANES:]], axis=0)
        da = dh * hprev
        ixc = ig * xc
        dm = dh * ixc
        dixc = dh * m
        di = dixc * xc
        dxc = dixc * ig
        dla = da * a - dm * (a * a) / m
        dlam_ref[...] += _colsum(dla * r) * (LRU_C * _sigmoid(nl))
        dr = dla * big_l
        dpa = dr * r * (1.0 - r)
        dpx = di * ig * (1.0 - ig)
        dba_ref[...] += _colsum(dpa)
        dbx_ref[...] += _colsum(dpx)
        dpab = dpa.astype(BF16)
        dpxb = dpx.astype(BF16)
        nt = (((1,), (1,)), ((), ()))
        tn = (((0,), (0,)), ((), ()))
        dxc_g = []
        for h in range(heads):
            sl = slice(h * hd, (h + 1) * hd)
            dxc_g.append(lax.dot_general(dpab[:, sl], wa_ref[h], nt, preferred_element_type=F32)
                         + lax.dot_general(dpxb[:, sl], wx_ref[h], nt, preferred_element_type=F32))
            dwa_ref[h] += lax.dot_general(xcb[:, sl], dpab[:, sl], tn, preferred_element_type=F32)
            dwx_ref[h] += lax.dot_general(xcb[:, sl], dpxb[:, sl], tn, preferred_element_type=F32)
        dxc = dxc + jnp.concatenate(dxc_g, axis=1)

        dbc_ref[...] += _colsum(dxc)
        for k in range(4):
            dwc_ref[k:k + 1, :] += _colsum(dxc * xsh[3 - k])
        nxt = dxcn[...]
        dxl = dxc * wc_ref[3:4, :]
        for k in (1, 2, 3):
            ur = pltpu.roll(dxc, tm - k, 0)
            tl = jnp.where(row8 >= SUBLANES - k, pltpu.roll(nxt, SUBLANES - k, 0), ur[tm - SUBLANES:tm])
            dxl = dxl + jnp.concatenate([ur[:tm - SUBLANES], tl], axis=0) * wc_ref[3 - k:4 - k, :]
        dxcn[...] = dxc[0:SUBLANES]
        dp_ref[:, 0:lw] = dxl.astype(BF16)
        dbin_ref[:, 0:lw] += _colsum(dxl)

        gu, dgu_dx = _gelu_and_grad(p_ref[:, 2 * lw:2 * lw + sw].astype(F32))
        gv, dgv_dx = _gelu_and_grad(p_ref[:, 2 * lw + sw:cw].astype(F32))
        xhat, rstd = _ln_stats(gv)
        vn = (xhat * lg_ref[...] + lb_ref[...]).astype(BF16)
        dys = dys_ref[...]
        dmixed = dys * gu
        dmb = dmixed.astype(BF16)
        tpos = lax.broadcasted_iota(jnp.int32, (SGU_BLOCK, SGU_BLOCK), 0) // CHUNK
        spos = lax.broadcasted_iota(jnp.int32, (SGU_BLOCK, SGU_BLOCK), 1) // CHUNK
        causal = spos <= tpos
        mixed_rows, dvn_rows = [], []
        for blk in range(nblk):
            rs = slice(blk * SGU_BLOCK, (blk + 1) * SGU_BLOCK)
            mcols, dcols = [], []
            for g in range(groups):
                cs = slice(g * gw, (g + 1) * gw)
                wm = jnp.where(causal, wsp_ref[g], 0.0).astype(BF16)
                mcols.append(jnp.dot(wm, vn[rs, cs], preferred_element_type=F32) + bsp_ref[:, g:g + 1])
                dcols.append(lax.dot_general(wm, dmb[rs, cs], tn, preferred_element_type=F32))
                dw = lax.dot_general(dmb[rs, cs], vn[rs, cs], nt, preferred_element_type=F32)
                dwsp_ref[g] += jnp.where(causal, dw, 0.0)
                dbsp_ref[:, g:g + 1] += jnp.sum(dmixed[rs, cs], axis=1, keepdims=True)
            mixed_rows.append(jnp.concatenate(mcols, axis=1))
            dvn_rows.append(jnp.concatenate(dcols, axis=1))
        mixed_all = jnp.concatenate(mixed_rows, axis=0) if nblk > 1 else mixed_rows[0]
        dvn = jnp.concatenate(dvn_rows, axis=0) if nblk > 1 else dvn_rows[0]
        du = dys * mixed_all * dgu_dx
        dlg_ref[...] += _colsum(dvn * xhat)
        dlb_ref[...] += _colsum(dvn)
        dv = _ln_bwd(dvn, xhat, rstd, lg_ref[...]) * dgv_dx
        dp_ref[:, 2 * lw:2 * lw + sw] = du.astype(BF16)
        dp_ref[:, 2 * lw + sw:cw] = dv.astype(BF16)
        dbin_ref[:, 2 * lw:2 * lw + sw] += _colsum(du)
        dbin_ref[:, 2 * lw + sw:cw] += _colsum(dv)

    rev = lambda s: n_s - 1 - s
    tile = lambda w: pl.BlockSpec((None, tm, w), lambda b, s: (b, rev(s), 0))
    halo = lambda w: pl.BlockSpec((None, SUBLANES, w), lambda b, s: (b, jnp.maximum(rev(s) * per8 - 1, 0), 0))
    xhalo = pl.BlockSpec((None, halo_rows, lw), lambda b, s: (b, jnp.maximum(rev(s) * (tm // halo_rows) - 1, 0), 0))
    full = lambda shp: pl.BlockSpec(shp, lambda b, s: (0,) * len(shp))
    small = [w_conv, b_conv, w_rg_a, b_rg_a, w_rg_x, b_rg_x, lam, w_sp, b_sp_t, ln_v_g, ln_v_b]
    acc_shapes = [(1, cw), w_conv.shape, b_conv.shape, w_rg_a.shape, b_rg_a.shape, w_rg_x.shape, b_rg_x.shape,
                  lam.shape, w_sp.shape, b_sp_t.shape, ln_v_g.shape, ln_v_b.shape]
    res = pl.pallas_call(
        body, name="mix_bwd", grid=(Bl, n_s),
        in_specs=[tile(cw), xhalo, tile(lw), halo(lw), tile(lw), tile(sw), pl.BlockSpec(memory_space=pl.ANY)]
                 + [full(w.shape) for w in small],
        out_specs=tuple([tile(cw)] + [full(shp) for shp in acc_shapes]),
        out_shape=tuple([jax.ShapeDtypeStruct((Bl, S, din), BF16)] + [jax.ShapeDtypeStruct(shp, F32) for shp in acc_shapes]),
        input_output_aliases={6: 0},
        scratch_shapes=[pltpu.VMEM((SUBLANES, lw), F32), pltpu.VMEM((SUBLANES, lw), F32), pltpu.VMEM((SUBLANES, lw), F32),
                        pltpu.VMEM((tm, lw), F32), pltpu.VMEM((tm, lw), F32), pltpu.VMEM((tm, lw), F32)],
        compiler_params=_cparams(2, big=True),
    )(proj, proj, hs, hs, dya, dys, dproj, *small)
    return res


def _final_dx(dxp, dh, x, sc1, *, ts):
    Bl, S, D = x.shape

    def body(dxp_ref, dh_ref, x_ref, sc_ref, dx_ref, dsc_ref, dsh_ref):
        @pl.when(pl.program_id(1) == 0)
        def _():
            dsc_ref[...] = jnp.zeros_like(dsc_ref)
            dsh_ref[...] = jnp.zeros_like(dsh_ref)

        dh = dh_ref[...]
        dx_ref[...] = dxp_ref[...] + dh * (1.0 + sc_ref[...])
        dsc_ref[...] += _colsum(dh * x_ref[...])
        dsh_ref[...] += _colsum(dh)

    return pl.pallas_call(
        body, name="final_dx", grid=(Bl, S // ts),
        in_specs=[_tok_spec(ts, D), _tok_spec(ts, D), _tok_spec(ts, D), _brow_spec(D)],
        out_specs=(_tok_spec(ts, D), _brow_spec(D), _brow_spec(D)),
        out_shape=(jax.ShapeDtypeStruct((Bl, S, D), F32), jax.ShapeDtypeStruct((Bl, 1, D), F32),
                   jax.ShapeDtypeStruct((Bl, 1, D), F32)),
        compiler_params=_cparams(2),
    )(dxp, dh, x, sc1)


def _ada_fwd(c_all, w_ada):
    R, D = c_all.shape
    nb = w_ada.shape[1]

    def body(c_ref, w_ref, act_ref, o_ref):
        cv = c_ref[...]
        act = (cv * _sigmoid(cv)).astype(BF16)
        act_ref[...] = act
        o_ref[...] = jnp.dot(act, w_ref[...].astype(BF16), preferred_element_type=F32)

    return pl.pallas_call(
        body, name="ada_fwd",
        out_shape=(jax.ShapeDtypeStruct((R, D), BF16), jax.ShapeDtypeStruct((R, nb), F32)),
        compiler_params=pltpu.CompilerParams(vmem_limit_bytes=VMEM_LIMIT),
    )(c_all, w_ada)


def _ada_bwd(c_act, dmod_cols):
    R, D = c_act.shape
    nb = dmod_cols.shape[1]

    def body(act_ref, d_ref, o_ref):
        o_ref[...] = lax.dot_general(act_ref[...], d_ref[...].astype(BF16), (((0,), (0,)), ((), ())),
                                     preferred_element_type=F32)

    return pl.pallas_call(
        body, name="ada_bwd", out_shape=jax.ShapeDtypeStruct((D, nb), F32),
        compiler_params=pltpu.CompilerParams(vmem_limit_bytes=VMEM_LIMIT),
    )(c_act, dmod_cols)


def _adamw(w, g_slots, m, v, *, tr, name):
    R, C = w.shape
    n_slot = g_slots.shape[0]
    tr = min(tr, R)
    assert R % tr == 0, (name, R, tr)
    c1 = 1.0 / (1.0 - ADAM_B1 ** ADAM_STEP)
    c2 = 1.0 / (1.0 - ADAM_B2 ** ADAM_STEP)

    def body(w_ref, g_ref, m_ref, v_ref, go_ref, d_ref, mo_ref, vo_ref):
        g = g_ref[0].astype(F32)
        for i in range(1, n_slot):
            g = g + g_ref[i].astype(F32)
        mn = ADAM_B1 * m_ref[...] + (1.0 - ADAM_B1) * g
        vn = ADAM_B2 * v_ref[...] + (1.0 - ADAM_B2) * (g * g)
        go_ref[...] = g
        mo_ref[...] = mn
        vo_ref[...] = vn
        d_ref[...] = -ADAM_LR * ((mn * c1) / (jnp.sqrt(vn * c2) + ADAM_EPS) + ADAM_WD * w_ref[...])

    blk = pl.BlockSpec((tr, C), lambda i: (i, 0))
    return pl.pallas_call(
        body, name=name, grid=(R // tr,),
        in_specs=[blk, pl.BlockSpec((n_slot, tr, C), lambda i: (0, i, 0)), blk, blk],
        out_specs=(blk, blk, blk, blk),
        out_shape=tuple(jax.ShapeDtypeStruct((R, C), F32) for _ in range(4)),
        compiler_params=_cparams(1, big=True),
    )(w, g_slots, m, v)


def _sum_slots(g_slots, *, name):
    n_slot, R, C = g_slots.shape

    def body(g_ref, o_ref):
        g = g_ref[0]
        for i in range(1, n_slot):
            g = g + g_ref[i]
        o_ref[...] = g

    return pl.pallas_call(body, name=name, out_shape=jax.ShapeDtypeStruct((R, C), F32),
                          compiler_params=pltpu.CompilerParams(vmem_limit_bytes=VMEM_LIMIT))(g_slots)


SMALL_NAMES = ("b_ada", "b_in", "b_conv", "w_rg_a", "b_rg_a", "w_rg_x", "b_rg_x", "lru_lambda", "w_sp", "b_sp",
               "ln_v_g", "ln_v_b", "ln1_g", "ln1_b", "ln2_g", "ln2_b")
BIG_NAMES = ("w_ada", "w_in", "w_conv", "w_o_lru", "w_o_sgu", "w_out", "w_up", "w_down")
WEIGHT_ORDER = ("w_ada", "b_ada", "w_in", "b_in", "w_conv", "b_conv", "w_rg_a", "b_rg_a", "w_rg_x", "b_rg_x",
                "lru_lambda", "w_sp", "b_sp", "ln_v_g", "ln_v_b", "w_o_lru", "w_o_sgu", "w_out", "ln1_g", "ln1_b",
                "w_up", "w_down", "ln2_g", "ln2_b")


def _pack_small(d):
    flat = jnp.concatenate([d[n].reshape(-1) for n in SMALL_NAMES])
    rows = -(-flat.shape[0] // LANES)
    rows = -(-rows // (N_DEV * SUBLANES)) * (N_DEV * SUBLANES)
    flat = jnp.pad(flat, (0, rows * LANES - flat.shape[0]))
    return flat.reshape(rows, LANES)


def _unpack_small(packed, like):
    flat = packed.reshape(-1)
    out, off = {}, 0
    for n in SMALL_NAMES:
        sz = like[n].size
        out[n] = flat[off:off + sz].reshape(like[n].shape)
        off += sz
    return out


def _blocked_cols(w2d):
    K, N = w2d.shape
    return jnp.transpose(w2d.reshape(K, N_DEV, N // N_DEV), (1, 0, 2))


def _unblock_cols(wb):
    n, K, nb = wb.shape
    return jnp.transpose(wb, (1, 0, 2)).reshape(K, n * nb)


def kernel(x, c, w_ada, b_ada, w_in, b_in, w_conv, b_conv, w_rg_a, b_rg_a, w_rg_x, b_rg_x, lru_lambda, w_sp, b_sp, ln_v_g, ln_v_b, w_o_lru, w_o_sgu, w_out, ln1_g, ln1_b, w_up, w_down, ln2_g, ln2_b, loss_target, m_w_ada, m_b_ada, m_w_in, m_b_in, m_w_conv, m_b_conv, m_w_rg_a, m_b_rg_a, m_w_rg_x, m_b_rg_x, m_lru_lambda, m_w_sp, m_b_sp, m_ln_v_g, m_ln_v_b, m_w_o_lru, m_w_o_sgu, m_w_out, m_ln1_g, m_ln1_b, m_w_up, m_w_down, m_ln2_g, m_ln2_b, v_w_ada, v_b_ada, v_w_in, v_b_in, v_w_conv, v_b_conv, v_w_rg_a, v_b_rg_a, v_w_rg_x, v_b_rg_x, v_lru_lambda, v_w_sp, v_b_sp, v_ln_v_g, v_ln_v_b, v_w_o_lru, v_w_o_sgu, v_w_out, v_ln1_g, v_ln1_b, v_w_up, v_w_down, v_ln2_g, v_ln2_b):
    W = dict(w_ada=w_ada, b_ada=b_ada, w_in=w_in, b_in=b_in, w_conv=w_conv, b_conv=b_conv, w_rg_a=w_rg_a,
             b_rg_a=b_rg_a, w_rg_x=w_rg_x, b_rg_x=b_rg_x, lru_lambda=lru_lambda, w_sp=w_sp, b_sp=b_sp,
             ln_v_g=ln_v_g, ln_v_b=ln_v_b, w_o_lru=w_o_lru, w_o_sgu=w_o_sgu, w_out=w_out, ln1_g=ln1_g, ln1_b=ln1_b,
             w_up=w_up, w_down=w_down, ln2_g=ln2_g, ln2_b=ln2_b)
    Mo = dict(w_ada=m_w_ada, b_ada=m_b_ada, w_in=m_w_in, b_in=m_b_in, w_conv=m_w_conv, b_conv=m_b_conv,
              w_rg_a=m_w_rg_a, b_rg_a=m_b_rg_a, w_rg_x=m_w_rg_x, b_rg_x=m_b_rg_x, lru_lambda=m_lru_lambda,
              w_sp=m_w_sp, b_sp=m_b_sp, ln_v_g=m_ln_v_g, ln_v_b=m_ln_v_b, w_o_lru=m_w_o_lru, w_o_sgu=m_w_o_sgu,
              w_out=m_w_out, ln1_g=m_ln1_g, ln1_b=m_ln1_b, w_up=m_w_up, w_down=m_w_down, ln2_g=m_ln2_g,
              ln2_b=m_ln2_b)
    Vo = dict(w_ada=v_w_ada, b_ada=v_b_ada, w_in=v_w_in, b_in=v_b_in, w_conv=v_w_conv, b_conv=v_b_conv,
              w_rg_a=v_w_rg_a, b_rg_a=v_b_rg_a, w_rg_x=v_w_rg_x, b_rg_x=v_b_rg_x, lru_lambda=v_lru_lambda,
              w_sp=v_w_sp, b_sp=v_b_sp, ln_v_g=v_ln_v_g, ln_v_b=v_ln_v_b, w_o_lru=v_w_o_lru, w_o_sgu=v_w_o_sgu,
              w_out=v_w_out, ln1_g=v_ln1_g, ln1_b=v_ln1_b, w_up=v_w_up, w_down=v_w_down, ln2_g=v_ln2_g,
              ln2_b=v_ln2_b)

    Bl, S, D = x.shape
    T = Bl * S
    lw = b_conv.shape[-1]
    sw = ln_v_g.shape[-1]
    din = b_in.shape[-1]
    dff = w_up.shape[-1] * N_DEV
    ts = min(512, S)
    tmix = min(256, S)
    tmm = min(1024, T)

    c_pad = jnp.pad(c, ((0, SUBLANES - Bl), (0, 0)))
    c_g, wconv_g = _exchange([c_pad, w_conv[0]], True, "xchg_c")
    wconv_full = _unblock_cols(wconv_g)
    c_act, modcols = _ada_fwd(c_g.reshape(N_DEV * SUBLANES, D), w_ada[0])
    (mod_slots,) = _exchange([modcols.reshape(N_DEV, SUBLANES, -1)], False, "xchg_mod")

    wnames = ("win", "wol", "wos", "wout", "wup", "wdown")
    shards = [w_in[0].astype(BF16), w_o_lru[0].astype(BF16), w_o_sgu[0].astype(BF16), w_out[0].astype(BF16),
              w_up[0].astype(BF16), w_down[0].astype(BF16)]
    g_send, g_recv, g_src, g_land, g_tok = _xstart(shards, True, mod_slots, "gather_start")
    gidx = {n: i for i, n in enumerate(wnames)}

    def gathered(n, after):
        i = gidx[n]
        return _xwait(g_src[i], g_land[i], g_send[i], g_recv[i], after, True, "gather_wait_" + n)

    mod = _unblock_cols(mod_slots)[:Bl] + (b_ada + g_tok[0, 0])
    sh1, sc1, gt1, sh2, sc2, gt2 = [mod[:, i * D:(i + 1) * D].reshape(Bl, 1, D) for i in range(6)]

    wa_b, wx_b = w_rg_a[0].astype(BF16), w_rg_x[0].astype(BF16)
    b_sp_t = jnp.transpose(b_sp[0])
    small_mix = (wconv_full, b_conv, wa_b, b_rg_a, wx_b, b_rg_x, lru_lambda, w_sp[0], b_sp_t, ln_v_g, ln_v_b)

    h = _modulate(x, sc1, sh1, ts)
    Win = _unblock_cols(gathered("win", h))
    proj = _mm(h.reshape(T, D), Win, mode="nn", tm=2048, tn=din // 4, tk=D, outs=[BF16],
               extras=[(b_in, "row")], epilogue=lambda acc, ex: (acc + ex[0],), name="mm_proj")
    proj3 = proj.reshape(Bl, S, din)
    hs, ya_pre, ysgu = _mix_fwd(proj3, *small_mix, tm=tmix, lw=lw, sw=sw)
    Wol = gathered("wol", ya_pre).reshape(lw, D)
    Wos = _unblock_cols(gathered("wos", ysgu))
    y_a = _mm(ya_pre.reshape(T, lw), Wol, mode="nn", tm=2048, tn=D, tk=lw, outs=[BF16], name="mm_ya")
    y_b = _mm(ysgu.reshape(T, sw), Wos, mode="nn", tm=2048, tn=D, tk=sw, outs=[BF16], name="mm_yb")
    merged = _merge_fwd(proj3, y_a.reshape(Bl, S, D), y_b.reshape(Bl, S, D), ts=ts, d=D)
    Wout = gathered("wout", merged).reshape(D, D)
    mix = _mm(merged.reshape(T, D), Wout, mode="nn", tm=2048, tn=D, tk=D, outs=[BF16], name="mm_mix")
    mix3 = mix.reshape(Bl, S, D)
    x1, h2 = _ln1_fwd(x, mix3, gt1, ln1_g, ln1_b, sc2, sh2, ts=ts)
    Wup = _unblock_cols(gathered("wup", h2))
    act = _mm(h2.reshape(T, D), Wup, mode="nn", tm=2048, tn=1024, tk=D, outs=[BF16],
              epilogue=lambda acc, ex: (jnp.square(jnp.maximum(acc, 0.0)),), name="mm_up")
    Wdown = gathered("wdown", act).reshape(dff, D)
    f = _mm(act, Wdown, mode="nn", tm=1024, tn=D, tk=2048, outs=[F32], name="mm_down")
    df, dx1p, dgt2, dg2, db2, loss_part = _ln2_loss(x1, f.reshape(Bl, S, D), loss_target, gt2, ln2_g, ln2_b, ts=ts)
    loss = lax.psum(loss_part[0, 0], ("x", "y", "c"))

    def send_grads(parts, name):
        snd, rcv, src, land, tok = _xstart(parts, False, None, name + "_start")
        return [(src[i], land[i], snd[i], rcv[i]) for i in range(len(parts))], tok

    df2 = df.reshape(T, D)
    dup = _mm(df2, Wdown, mode="nt", tm=2048, tn=1024, tk=D, outs=[BF16], extras=[(act, "tile")],
              epilogue=lambda acc, ex: (acc * (2.0 * jnp.sqrt(ex[0].astype(F32))),), name="mm_dup")
    g_wdown = _mm(act, df2, mode="tn", tm=1024, tn=D, tk=2048, outs=[BF16], name="mm_gwdown")
    (x_wdown,), tok = send_grads([g_wdown.reshape(N_DEV, dff // N_DEV, D)], "gx_wdown")
    dh2 = _mm(dup, Wup, mode="nt", tm=1024, tn=D, tk=2048, outs=[F32], tok=tok, name="mm_dh2")
    g_wup = _mm(h2.reshape(T, D), dup, mode="tn", tm=D, tn=1024, tk=2048, outs=[BF16], nb=dff // N_DEV,
                name="mm_gwup")
    (x_wup,), tok = send_grads([g_wup], "gx_wup")
    dxp, dmix, dsc2, dsh2, dgt1, dg1, db1 = _ln1_bwd(dx1p, dh2.reshape(Bl, S, D), x1, x, mix3, sc2 + tok[0, 0], gt1,
                                                     ln1_g, ts=ts)

    dmix2 = dmix.reshape(T, D)
    dmerged = _mm(dmix2, Wout, mode="nt", tm=2048, tn=D, tk=D, outs=[F32], name="mm_dmerged")
    g_wout = _mm(merged.reshape(T, D), dmix2, mode="tn", tm=D, tn=D, tk=2048, outs=[BF16], name="mm_gwout")
    (x_wout,), tok = send_grads([g_wout.reshape(N_DEV, D // N_DEV, D)], "gx_wout")
    dy_a, dy_b, dproj, dbin_hi = _merge_bwd(dmerged.reshape(Bl, S, D), y_a.reshape(Bl, S, D), y_b.reshape(Bl, S, D),
                                            proj3, ts=ts, d=D)
    dya_pre = _mm(dy_a.reshape(T, D), Wol, mode="nt", tm=2048, tn=lw, tk=D, outs=[F32], tok=tok, name="mm_dya")
    dysgu = _mm(dy_b.reshape(T, D), Wos, mode="nt", tm=2048, tn=sw, tk=D, outs=[F32], name="mm_dys")
    g_wol = _mm(ya_pre.reshape(T, lw), dy_a.reshape(T, D), mode="tn", tm=lw, tn=D, tk=2048, outs=[BF16],
                name="mm_gwol")
    g_wos = _mm(ysgu.reshape(T, sw), dy_b.reshape(T, D), mode="tn", tm=sw, tn=D, tk=2048, outs=[BF16],
                nb=D // N_DEV, name="mm_gwos")
    (x_wol, x_wos), tok = send_grads([g_wol.reshape(N_DEV, lw // N_DEV, D), g_wos], "gx_wo")
    small_mix_b = (wconv_full, b_conv + tok[0, 0]) + small_mix[2:]
    (dproj, dbin_lo, g_wconv, g_bconv, g_wa, g_ba, g_wx, g_bx, g_lam, g_wsp, g_bsp_t, g_lvg, g_lvb) = _mix_bwd(
        proj3, hs, dya_pre.reshape(Bl, S, lw), dysgu.reshape(Bl, S, sw), dproj, *small_mix_b, tm=tmix, lw=lw, sw=sw)
    dproj2 = dproj.reshape(T, din)
    g_win = _mm(h.reshape(T, D), dproj2, mode="tn", tm=D, tn=din // 4, tk=2048, outs=[BF16], nb=din // N_DEV,
                name="mm_gwin")
    (x_win,), tok = send_grads([g_win], "gx_win")
    dh = _mm(dproj2, Win, mode="nt", tm=1024, tn=D, tk=din // 2, outs=[F32], tok=tok, name="mm_dh")
    grad_x, dsc1, dsh1 = _final_dx(dxp, dh.reshape(Bl, S, D), x, sc1, ts=ts)

    dmod = jnp.concatenate([dsh1, dsc1, dgt1, dsh2, dsc2, dgt2], axis=-1).reshape(Bl, 6 * D)
    dmod_b = _blocked_cols(jnp.pad(dmod, ((0, SUBLANES - Bl), (0, 0))))
    g_small_local = dict(
        b_ada=jnp.sum(dmod, axis=0, keepdims=True), b_in=jnp.concatenate([dbin_lo, dbin_hi], axis=-1),
        b_conv=g_bconv, w_rg_a=g_wa[None], b_rg_a=g_ba, w_rg_x=g_wx[None], b_rg_x=g_bx, lru_lambda=g_lam,
        w_sp=g_wsp[None], b_sp=jnp.transpose(g_bsp_t)[None], ln_v_g=g_lvg, ln_v_b=g_lvb, ln1_g=dg1, ln1_b=db1,
        ln2_g=dg2, ln2_b=db2)
    gs_packed = _pack_small(g_small_local)
    rows = gs_packed.shape[0]
    parts = [dmod_b, _blocked_cols(g_wconv), gs_packed.reshape(N_DEV, rows // N_DEV, LANES)]
    dmod_s, gwconv_s, gsmall_s = _exchange(parts, False, "xchg_grads")
    gwdown_s = _xwait(*x_wdown, dmod_s, False, "gx_wdown_wait")
    gwup_s = _xwait(*x_wup, dmod_s, False, "gx_wup_wait")
    gwout_s = _xwait(*x_wout, dmod_s, False, "gx_wout_wait")
    gwol_s = _xwait(*x_wol, dmod_s, False, "gx_wol_wait")
    gwos_s = _xwait(*x_wos, dmod_s, False, "gx_wos_wait")
    gwin_s = _xwait(*x_win, dmod_s, False, "gx_win_wait")

    out_g, out_d, out_m, out_v = {}, {}, {}, {}

    def adam(name, g_slots, tr):
        shp = W[name].shape
        w2, m2, v2 = [t.reshape(g_slots.shape[1:]) for t in (W[name], Mo[name], Vo[name])]
        g, d, mn, vn = _adamw(w2, g_slots, m2, v2, tr=tr, name="adam_" + name)
        out_g[name], out_d[name], out_m[name], out_v[name] = [t.reshape(shp) for t in (g, d, mn, vn)]

    g_wada = _ada_bwd(c_act, dmod_s.reshape(N_DEV * SUBLANES, -1))
    adam("w_ada", g_wada[None], 256)
    adam("w_in", gwin_s, 256)
    adam("w_conv", gwconv_s, 8)
    adam("w_o_lru", gwol_s, 160)
    adam("w_o_sgu", gwos_s, 256)
    adam("w_out", gwout_s, 128)
    adam("w_up", gwup_s, 256)
    adam("w_down", gwdown_s, 256)

    g_chunk = _sum_slots(gsmall_s, name="sum_small")
    (gsmall_all,) = _exchange([g_chunk], True, "xchg_small")
    gs, ds, ms, vs = _adamw(_pack_small(W), gsmall_all.reshape(1, rows, LANES), _pack_small(Mo), _pack_small(Vo),
                            tr=rows // N_DEV, name="adam_small")
    for dst, packed in ((out_g, gs), (out_d, ds), (out_m, ms), (out_v, vs)):
        dst.update(_unpack_small(packed, W))

    return (loss, grad_x, *[out_g[n] for n in WEIGHT_ORDER], *[out_d[n] for n in WEIGHT_ORDER],
            *[out_m[n] for n in WEIGHT_ORDER], *[out_v[n] for n in WEIGHT_ORDER])
```

```python
import functools
import math

import jax
import jax.numpy as jnp
from jax import lax
from jax.experimental import pallas as pl
from jax.experimental.pallas import tpu as pltpu

N_DEV = 8
LN_EPS = 1e-5
LRU_C = 8.0
CHUNK = 64
SGU_BLOCK = 128
ALPHA = 2.0 ** 0.25
ADAM_LR = 0.001
ADAM_B1 = 0.9
ADAM_B2 = 0.999
ADAM_EPS = 1e-08
ADAM_WD = 0.01
ADAM_STEP = 10
GELU_K0 = math.sqrt(2.0 / math.pi)
GELU_K1 = 0.044715

SUBLANES = 8
LANES = 128
VMEM_LIMIT = 56 * 1024 * 1024

F32 = jnp.float32
BF16 = jnp.bfloat16
MESH = pl.DeviceIdType.MESH


def _cparams(n_axes, big=False):
    return pltpu.CompilerParams(dimension_semantics=("arbitrary",) * n_axes,
                                vmem_limit_bytes=VMEM_LIMIT if big else None)


def _sigmoid(x):
    return 1.0 / (1.0 + jnp.exp(-x))


def _gelu(x):
    t = jnp.tanh(GELU_K0 * (x + GELU_K1 * (x * x * x)))
    return 0.5 * x * (1.0 + t)


def _gelu_and_grad(x):
    x2 = x * x
    t = jnp.tanh(GELU_K0 * (x + GELU_K1 * (x2 * x)))
    g = 0.5 * x * (1.0 + t)
    dg = 0.5 * (1.0 + t) + 0.5 * x * (1.0 - t * t) * (GELU_K0 * (1.0 + 3.0 * GELU_K1 * x2))
    return g, dg


def _expm1(x):
    p = x * (1.0 + x * (1.0 / 2.0 + x * (1.0 / 6.0 + x * (1.0 / 24.0 + x * (1.0 / 120.0)))))
    return jnp.where(jnp.abs(x) < 0.0625, p, jnp.exp(x) - 1.0)


def _log1p_pos(e):
    p = e * (1.0 - e * (1.0 / 2.0) + e * e * (1.0 / 3.0) - e * e * e * (1.0 / 4.0))
    return jnp.where(e < 1e-2, p, jnp.log(1.0 + e))


def _ln_stats(z):
    mu = jnp.mean(z, axis=-1, keepdims=True)
    zc = z - mu
    var = jnp.mean(zc * zc, axis=-1, keepdims=True)
    rstd = lax.rsqrt(var + LN_EPS)
    return zc * rstd, rstd


def _ln_bwd(dy, xhat, rstd, g):
    dxh = dy * g
    m1 = jnp.mean(dxh, axis=-1, keepdims=True)
    m2 = jnp.mean(dxh * xhat, axis=-1, keepdims=True)
    return rstd * (dxh - m1 - xhat * m2)


def _colsum(v):
    return jnp.sum(v, axis=0, keepdims=True)


def _first_step():
    return jnp.logical_and(pl.program_id(0) == 0, pl.program_id(1) == 0)


def _exchange(arrs, gather, name):
    n = len(arrs)
    n_peer = N_DEV - 1

    def body(*refs):
        ins, outs = refs[:n], refs[n:2 * n]
        send_sems, recv_sems, loc_sems = refs[2 * n:]
        x, y, c = lax.axis_index("x"), lax.axis_index("y"), lax.axis_index("c")
        me = 4 * x + 2 * y + c
        started = []
        for a in range(n):
            src_me = ins[a] if gather else ins[a].at[me]
            lc = pltpu.make_async_copy(src_me, outs[a].at[me], loc_sems.at[a])
            lc.start()
            started.append((lc, None))
        for p in range(1, N_DEV):
            px, py, pc = x ^ ((p >> 2) & 1), y ^ ((p >> 1) & 1), c ^ (p & 1)
            peer = 4 * px + 2 * py + pc
            for a in range(n):
                k = a * n_peer + (p - 1)
                src = ins[a] if gather else ins[a].at[peer]
                cp = pltpu.make_async_remote_copy(src_ref=src, dst_ref=outs[a].at[me],
                                                  send_sem=send_sems.at[k], recv_sem=recv_sems.at[k],
                                                  device_id=(px, py, pc), device_id_type=MESH)
                cp.start()
                rc = pltpu.make_async_remote_copy(src_ref=src, dst_ref=outs[a].at[peer],
                                                  send_sem=send_sems.at[k], recv_sem=recv_sems.at[k],
                                                  device_id=(px, py, pc), device_id_type=MESH)
                started.append((cp, rc))
        for cp, rc in started:
            if rc is None:
                cp.wait()
            else:
                cp.wait_send()
                rc.wait_recv()

    hbm = pl.BlockSpec(memory_space=pltpu.HBM)
    out_shape = tuple(
        jax.ShapeDtypeStruct(((N_DEV,) + a.shape) if gather else a.shape, a.dtype) for a in arrs)
    return pl.pallas_call(
        body, name=name, out_shape=out_shape,
        in_specs=[hbm] * n, out_specs=tuple([hbm] * n),
        scratch_shapes=[pltpu.SemaphoreType.DMA((n * n_peer,)), pltpu.SemaphoreType.DMA((n * n_peer,)),
                        pltpu.SemaphoreType.DMA((n,))],
        compiler_params=pltpu.CompilerParams(has_side_effects=True),
    )(*arrs)


_HBM = pl.BlockSpec(memory_space=pltpu.HBM)
_SEM = pl.BlockSpec(memory_space=pltpu.SEMAPHORE)
_EFFECT = pltpu.SideEffectType.DATAFLOW_SIDE_EFFECTING


def _peer_of(p):
    x, y, c = lax.axis_index("x"), lax.axis_index("y"), lax.axis_index("c")
    px, py, pc = x ^ ((p >> 2) & 1), y ^ ((p >> 1) & 1), c ^ (p & 1)
    return (px, py, pc), 4 * px + 2 * py + pc


def _xstart(srcs, gather, after, name):
    n = len(srcs)
    lands = [lax.empty(((N_DEV,) + t.shape) if gather else t.shape, t.dtype) for t in srcs]
    n_after = 0 if after is None else 1

    def body(*refs):
        src_refs, land_refs = refs[:n], refs[n:2 * n]
        refs = refs[n_after:]
        send_sems, recv_sems = refs[2 * n:3 * n], refs[3 * n:4 * n]
        token = refs[6 * n]
        me = 4 * lax.axis_index("x") + 2 * lax.axis_index("y") + lax.axis_index("c")
        for a in range(n):
            for p in range(1, N_DEV):
                dev, peer = _peer_of(p)
                pltpu.make_async_remote_copy(
                    src_ref=src_refs[a] if gather else src_refs[a].at[peer], dst_ref=land_refs[a].at[me],
                    send_sem=send_sems[a].at[p - 1], recv_sem=recv_sems[a].at[p - 1],
                    device_id=dev, device_id_type=MESH).start()
        token[...] = jnp.zeros_like(token)

    sems = tuple(pltpu.SemaphoreType.DMA((N_DEV - 1,)) for _ in range(2 * n))
    thru = tuple(pltpu.HBM(t.shape, t.dtype) for t in list(srcs) + list(lands))
    res = pl.pallas_call(
        body, name=name,
        out_shape=sems + thru + (jax.ShapeDtypeStruct((SUBLANES, LANES), F32),),
        in_specs=[_HBM] * (2 * n) + [pl.BlockSpec(memory_space=pl.ANY)] * n_after,
        out_specs=tuple([_SEM] * (2 * n) + [_HBM] * (2 * n) + [pl.BlockSpec(memory_space=pltpu.VMEM)]),
        input_output_aliases={i: 2 * n + i for i in range(2 * n)},
        compiler_params=pltpu.CompilerParams(has_side_effects=_EFFECT),
    )(*[pltpu.with_memory_space_constraint(t, pltpu.HBM) for t in list(srcs) + list(lands)],
      *([after] if n_after else []))
    return res[:n], res[n:2 * n], res[2 * n:3 * n], res[3 * n:4 * n], res[4 * n]


def _xwait(src, land, send_sem, recv_sem, after, gather, name):
    def body(src_ref, land_ref, send_ref, recv_ref, after_ref, src_dead, land_out):
        del after_ref, src_dead, land_out
        for p in range(1, N_DEV):
            dev, peer = _peer_of(p)
            cp = pltpu.make_async_remote_copy(
                src_ref=src_ref if gather else src_ref.at[peer], dst_ref=land_ref.at[peer],
                send_sem=send_ref.at[p - 1], recv_sem=recv_ref.at[p - 1], device_id=dev, device_id_type=MESH)
            cp.wait_send()
            cp.wait_recv()

    src_done, landed = pl.pallas_call(
        body, name=name, out_shape=(pltpu.HBM(src.shape, src.dtype), pltpu.HBM(land.shape, land.dtype)),
        in_specs=[_HBM, _HBM, _SEM, _SEM, pl.BlockSpec(memory_space=pl.ANY)], out_specs=(_HBM, _HBM),
        input_output_aliases={0: 0, 1: 1},
        compiler_params=pltpu.CompilerParams(has_side_effects=_EFFECT),
    )(src, land, send_sem, recv_sem, after)
    me = 4 * lax.axis_index("x") + 2 * lax.axis_index("y") + lax.axis_index("c")
    own = src_done if gather else lax.dynamic_index_in_dim(src_done, me, 0, keepdims=False)
    return lax.dynamic_update_slice(landed, own[None], (me,) + (0,) * own.ndim)


def _mm(a, b, *, mode, tm, tn, tk, outs, epilogue=None, extras=(), nb=None, tok=None, name):
    if mode == "nn":
        (M, K), (_, N) = a.shape, b.shape
    elif mode == "nt":
        (M, K), (N, _) = a.shape, b.shape
    else:
        (K, M), (_, N) = a.shape, b.shape
    tm, tn, tk = min(tm, M), min(tn, N), min(tk, K)
    assert M % tm == 0 and N % tn == 0 and K % tk == 0, (name, M, N, K, tm, tn, tk)
    if mode == "nn":
        a_spec = pl.BlockSpec((tm, tk), lambda i, j, k: (i, k))
        b_spec = pl.BlockSpec((tk, tn), lambda i, j, k: (k, j))
        dims = (((1,), (0,)), ((), ()))
    elif mode == "nt":
        a_spec = pl.BlockSpec((tm, tk), lambda i, j, k: (i, k))
        b_spec = pl.BlockSpec((tn, tk), lambda i, j, k: (j, k))
        dims = (((1,), (1,)), ((), ()))
    else:
        a_spec = pl.BlockSpec((tk, tm), lambda i, j, k: (k, i))
        b_spec = pl.BlockSpec((tk, tn), lambda i, j, k: (k, j))
        dims = (((0,), (0,)), ((), ()))
    nk = K // tk
    n_ex, n_out = len(extras), len(outs)
    n_tok = 0 if tok is None else 1
    nbytes = lambda d: jnp.dtype(d).itemsize
    vmem_est = (2 * (tm * tk * nbytes(a.dtype) + tk * tn * nbytes(b.dtype)
                     + sum(tm * tn * nbytes(e.dtype) for e, kind in extras if kind == "tile")
                     + sum(tm * tn * nbytes(d) for d in outs)) + tm * tn * 4)
    assert vmem_est <= VMEM_LIMIT, (name, vmem_est)
    if epilogue is None:
        epilogue = lambda acc, ex: tuple(acc.astype(d) for d in outs)

    def body(a_ref, b_ref, *refs):
        refs = refs[n_tok:]
        ex_refs, out_refs = refs[:n_ex], refs[n_ex:n_ex + n_out]

        def finish(acc):
            res = epilogue(acc, [r[...] for r in ex_refs])
            for o_ref, v in zip(out_refs, res):
                if nb is None:
                    o_ref[...] = v.astype(o_ref.dtype)
                else:
                    for q in range(tn // nb):
                        o_ref[q] = v[:, q * nb:(q + 1) * nb].astype(o_ref.dtype)

        part = lax.dot_general(a_ref[...], b_ref[...], dims, preferred_element_type=F32)
        if nk == 1:
            finish(part)
        else:
            acc_ref = refs[n_ex + n_out]
            k = pl.program_id(2)

            @pl.when(k == 0)
            def _():
                acc_ref[...] = part

            @pl.when(k > 0)
            def _():
                acc_ref[...] += part

            @pl.when(k == nk - 1)
            def _():
                finish(acc_ref[...])

    ex_specs = [pl.BlockSpec((tm, tn), lambda i, j, k: (i, j)) if kind == "tile"
                else pl.BlockSpec((1, tn), lambda i, j, k: (0, j)) for _, kind in extras]
    if nb is not None:
        assert tn % nb == 0, (name, tn, nb)
        o_spec = pl.BlockSpec((tn // nb, tm, nb), lambda i, j, k: (j, i, 0))
        o_shape = (N // nb, M, nb)
    else:
        o_spec = pl.BlockSpec((tm, tn), lambda i, j, k: (i, j))
        o_shape = (M, N)
    res = pl.pallas_call(
        body, name=name, grid=(M // tm, N // tn, nk),
        in_specs=[a_spec, b_spec] + [pl.BlockSpec((SUBLANES, LANES), lambda i, j, k: (0, 0))] * n_tok + ex_specs,
        out_specs=tuple([o_spec] * n_out),
        out_shape=tuple(jax.ShapeDtypeStruct(o_shape, d) for d in outs),
        scratch_shapes=[pltpu.VMEM((tm, tn), F32)] if nk > 1 else [],
        compiler_params=_cparams(3, big=True),
    )(a, b, *([tok] if n_tok else []), *[e for e, _ in extras])
    return res[0] if n_out == 1 else res


def _mm_rows(a, b, *, mode, tm, seq, ins, outs, epilogue, tok=None, name):
    M, K = a.shape
    N = b.shape[1] if mode == "nn" else b.shape[0]
    tm = min(tm, M)
    assert M % tm == 0 and seq % tm == 0, (name, M, seq, tm)
    tpb = seq // tm
    n_b = M // seq
    dims = (((1,), (0,)), ((), ())) if mode == "nn" else (((1,), (1,)), ((), ()))
    n_tok = 0 if tok is None else 1
    n_in, n_out = len(ins), len(outs)

    in_specs, in_arrs = [], []
    for spec in ins:
        kind, arr = spec[0], spec[1]
        in_arrs.append(arr)
        if kind == "tile":
            in_specs.append(pl.BlockSpec((tm, arr.shape[1]), lambda i: (i, 0)))
        elif kind == "tilecol":
            in_specs.append(pl.BlockSpec((tm, spec[2]), lambda i, cb=spec[3]: (i, cb)))
        elif kind == "row":
            in_specs.append(pl.BlockSpec(arr.shape, lambda i: (0, 0)))
        else:
            in_specs.append(pl.BlockSpec((None, 1, arr.shape[2]), lambda i: (i // tpb, 0, 0)))
    out_specs, out_shapes = [], []
    for spec in outs:
        kind = spec[0]
        if kind == "tile":
            out_specs.append(pl.BlockSpec((tm, spec[2]), lambda i: (i, 0)))
            out_shapes.append(jax.ShapeDtypeStruct((M, spec[2]), spec[1]))
        elif kind == "tilecol":
            out_specs.append(pl.BlockSpec((tm, spec[2]), lambda i, cb=spec[3]: (i, cb)))
            out_shapes.append(jax.ShapeDtypeStruct((M, spec[4]), spec[1]))
        elif kind == "acc_row":
            out_specs.append(pl.BlockSpec((1, spec[1]), lambda i: (0, 0)))
            out_shapes.append(jax.ShapeDtypeStruct((1, spec[1]), F32))
        elif kind == "acc_brow":
            out_specs.append(pl.BlockSpec((None, 1, spec[1]), lambda i: (i // tpb, 0, 0)))
            out_shapes.append(jax.ShapeDtypeStruct((n_b, 1, spec[1]), F32))
        else:
            out_specs.append(pl.BlockSpec((SUBLANES, LANES), lambda i: (0, 0)))
            out_shapes.append(jax.ShapeDtypeStruct((SUBLANES, LANES), F32))

    def body(a_ref, b_ref, *refs):
        refs = refs[n_tok:]
        in_refs, out_refs = refs[:n_in], refs[n_in:n_in + n_out]
        i = pl.program_id(0)
        prod = lax.dot_general(a_ref[...], b_ref[...], dims, preferred_element_type=F32)
        vals = epilogue(prod, [r[...] for r in in_refs])
        for spec, o_ref, v in zip(outs, out_refs, vals):
            kind = spec[0]
            if kind in ("tile", "tilecol"):
                o_ref[...] = v.astype(o_ref.dtype)
            else:
                first = (i % tpb == 0) if kind == "acc_brow" else (i == 0)

                @pl.when(first)
                def _(o_ref=o_ref, v=v):
                    o_ref[...] = jnp.broadcast_to(v, o_ref.shape)

                @pl.when(jnp.logical_not(first))
                def _(o_ref=o_ref, v=v):
                    o_ref[...] += v

    res = pl.pallas_call(
        body, name=name, grid=(M // tm,),
        in_specs=[pl.BlockSpec((tm, K), lambda i: (i, 0)),
                  pl.BlockSpec(b.shape, lambda i: (0, 0), pipeline_mode=pl.Buffered(1))]
                 + [pl.BlockSpec((SUBLANES, LANES), lambda i: (0, 0))] * n_tok + in_specs,
        out_specs=tuple(out_specs), out_shape=tuple(out_shapes),
        compiler_params=_cparams(1, big=True),
    )(a, b, *([tok] if n_tok else []), *in_arrs)
    return res


def _tok_spec(ts, width, col_block=0):
    return pl.BlockSpec((None, ts, width), lambda b, s: (b, s, col_block))


def _brow_spec(width):
    return pl.BlockSpec((None, 1, width), lambda b, s: (b, 0, 0))


def _vec_spec(width):
    return pl.BlockSpec((1, width), lambda b, s: (0, 0))


def _modulate(x, sc, sh, ts):
    Bl, S, D = x.shape

    def body(x_ref, sc_ref, sh_ref, o_ref):
        o_ref[...] = (x_ref[...] * (1.0 + sc_ref[...]) + sh_ref[...]).astype(BF16)

    return pl.pallas_call(
        body, name="modulate", grid=(Bl, S // ts),
        in_specs=[_tok_spec(ts, D), _brow_spec(D), _brow_spec(D)],
        out_specs=_tok_spec(ts, D), out_shape=jax.ShapeDtypeStruct((Bl, S, D), BF16),
        compiler_params=_cparams(2),
    )(x, sc, sh)


def _mix_fwd(proj, w_conv, b_conv, w_rg_a, b_rg_a, w_rg_x, b_rg_x, lam, w_sp, b_sp_t, ln_v_g, ln_v_b, *, tm, lw, sw):
    Bl, S, _ = proj.shape
    heads, hd = w_rg_a.shape[0], w_rg_a.shape[1]
    groups = w_sp.shape[0]
    cw = 2 * lw + 2 * sw
    nblk = tm // SGU_BLOCK

    def body(p_ref, wc_ref, bc_ref, wa_ref, ba_ref, wx_ref, bx_ref, lam_ref, wsp_ref, bsp_ref, lg_ref, lb_ref,
             hs_ref, ya_ref, ys_ref, xprev, hcar, a_scr, b_scr):
        s = pl.program_id(1)

        @pl.when(s == 0)
        def _():
            xprev[...] = jnp.zeros_like(xprev)
            hcar[...] = jnp.zeros_like(hcar)

        xl = p_ref[:, 0:lw].astype(F32)
        gl = p_ref[:, lw:2 * lw].astype(F32)
        row8 = lax.broadcasted_iota(jnp.int32, (SUBLANES, lw), 0)
        rowm = lax.broadcasted_iota(jnp.int32, (tm, lw), 0) & (SUBLANES - 1)

        prev = xprev[...]
        xc = xl * wc_ref[3:4, :] + bc_ref[...]
        for k in (1, 2, 3):
            xr = pltpu.roll(xl, k, 0)
            head = jnp.where(row8 < k, pltpu.roll(prev, k, 0), xr[0:SUBLANES])
            xs = jnp.concatenate([head, xr[SUBLANES:]], axis=0)
            xc = xc + xs * wc_ref[3 - k:4 - k, :]
        xprev[...] = xl[tm - SUBLANES:tm]

        xcb = xc.astype(BF16)
        pa = jnp.concatenate([jnp.dot(xcb[:, h * hd:(h + 1) * hd], wa_ref[h], preferred_element_type=F32)
                              for h in range(heads)], axis=1) + ba_ref[...]
        px = jnp.concatenate([jnp.dot(xcb[:, h * hd:(h + 1) * hd], wx_ref[h], preferred_element_type=F32)
                              for h in range(heads)], axis=1) + bx_ref[...]
        r = _sigmoid(pa)
        ig = _sigmoid(px)
        nl = -lam_ref[...]
        big_l = -LRU_C * (jnp.maximum(nl, 0.0) + _log1p_pos(jnp.exp(-jnp.abs(nl))))
        la = big_l * r
        a = jnp.exp(la)
        bin_ = jnp.sqrt(-_expm1(2.0 * la)) * (ig * xc)

        for d in (1, 2, 4):
            a_sh = pltpu.roll(a, d, 0)
            b_sh = pltpu.roll(bin_, d, 0)
            msk = rowm >= d
            bin_ = jnp.where(msk, a * b_sh + bin_, bin_)
            a = jnp.where(msk, a * a_sh, a)
        a_scr[...] = a
        b_scr[...] = bin_

        def grp(g, carry):
            off = pl.multiple_of(g * SUBLANES, SUBLANES)
            h = b_scr[pl.ds(off, SUBLANES), :] + a_scr[pl.ds(off, SUBLANES), :] * carry
            hs_ref[pl.ds(off, SUBLANES), :] = h
            return jnp.broadcast_to(h[SUBLANES - 1:SUBLANES, :], h.shape)

        hcar[...] = lax.fori_loop(0, tm // SUBLANES, grp, hcar[...])
        ya_ref[...] = (hs_ref[...] * _gelu(gl)).astype(BF16)

        gu = _gelu(p_ref[:, 2 * lw:2 * lw + sw].astype(F32))
        gv = _gelu(p_ref[:, 2 * lw + sw:cw].astype(F32))
        xhat, _ = _ln_stats(gv)
        vn = (xhat * lg_ref[...] + lb_ref[...]).astype(BF16)
        tpos = lax.broadcasted_iota(jnp.int32, (SGU_BLOCK, SGU_BLOCK), 0) // CHUNK
        spos = lax.broadcasted_iota(jnp.int32, (SGU_BLOCK, SGU_BLOCK), 1) // CHUNK
        gw = sw // groups
        rows_out = []
        for blk in range(nblk):
            r0 = blk * SGU_BLOCK
            cols = []
            for g in range(groups):
                wm = jnp.where(spos <= tpos, wsp_ref[g], 0.0).astype(BF16)
                mixed = jnp.dot(wm, vn[r0:r0 + SGU_BLOCK, g * gw:(g + 1) * gw], preferred_element_type=F32)
                cols.append(mixed + bsp_ref[:, g:g + 1])
            rows_out.append(jnp.concatenate(cols, axis=1))
        mixed_all = jnp.concatenate(rows_out, axis=0) if nblk > 1 else rows_out[0]
        ys_ref[...] = (gu * mixed_all).astype(BF16)

    full = lambda shp: pl.BlockSpec(shp, lambda b, s: (0,) * len(shp))
    return pl.pallas_call(
        body, name="mix_fwd", grid=(Bl, S // tm),
        in_specs=[_tok_spec(tm, cw), full(w_conv.shape), full(b_conv.shape), full(w_rg_a.shape), full(b_rg_a.shape),
                  full(w_rg_x.shape), full(b_rg_x.shape), full(lam.shape), full(w_sp.shape), full(b_sp_t.shape),
                  full(ln_v_g.shape), full(ln_v_b.shape)],
        out_specs=(_tok_spec(tm, lw), _tok_spec(tm, lw), _tok_spec(tm, sw)),
        out_shape=(jax.ShapeDtypeStruct((Bl, S, lw), F32), jax.ShapeDtypeStruct((Bl, S, lw), BF16),
                   jax.ShapeDtypeStruct((Bl, S, sw), BF16)),
        scratch_shapes=[pltpu.VMEM((SUBLANES, lw), F32), pltpu.VMEM((SUBLANES, lw), F32),
                        pltpu.VMEM((tm, lw), F32), pltpu.VMEM((tm, lw), F32)],
        compiler_params=_cparams(2, big=True),
    )(proj, w_conv, b_conv, w_rg_a, b_rg_a, w_rg_x, b_rg_x, lam, w_sp, b_sp_t, ln_v_g, ln_v_b)


def _merge_fwd(proj, y_a, y_b, *, ts, d):
    Bl, S, din = proj.shape
    gcol = (din - 2 * d) // (2 * d)
    assert gcol * 2 * d == din - 2 * d

    def body(g_ref, ya_ref, yb_ref, o_ref):
        sa = _sigmoid(g_ref[:, 0:d].astype(F32))
        sb = _sigmoid(g_ref[:, d:2 * d].astype(F32))
        o_ref[...] = (sa * ya_ref[...].astype(F32) + sb * yb_ref[...].astype(F32)).astype(BF16)

    return pl.pallas_call(
        body, name="merge_fwd", grid=(Bl, S // ts),
        in_specs=[_tok_spec(ts, 2 * d, gcol), _tok_spec(ts, d), _tok_spec(ts, d)],
        out_specs=_tok_spec(ts, d), out_shape=jax.ShapeDtypeStruct((Bl, S, d), BF16),
        compiler_params=_cparams(2),
    )(proj, y_a, y_b)


def _ln1_fwd(x, mix, gt1, g1, b1, sc2, sh2, *, ts):
    Bl, S, D = x.shape

    def body(x_ref, mix_ref, gt_ref, g_ref, b_ref, sc_ref, sh_ref, x1_ref, h2_ref):
        z = ALPHA * x_ref[...] + (1.0 + gt_ref[...]) * mix_ref[...].astype(F32)
        xhat, _ = _ln_stats(z)
        x1 = xhat * g_ref[...] + b_ref[...]
        x1_ref[...] = x1
        h2_ref[...] = (x1 * (1.0 + sc_ref[...]) + sh_ref[...]).astype(BF16)

    return pl.pallas_call(
        body, name="ln1_fwd", grid=(Bl, S // ts),
        in_specs=[_tok_spec(ts, D), _tok_spec(ts, D), _brow_spec(D), _vec_spec(D), _vec_spec(D), _brow_spec(D),
                  _brow_spec(D)],
        out_specs=(_tok_spec(ts, D), _tok_spec(ts, D)),
        out_shape=(jax.ShapeDtypeStruct((Bl, S, D), F32), jax.ShapeDtypeStruct((Bl, S, D), BF16)),
        compiler_params=_cparams(2),
    )(x, mix, gt1, g1, b1, sc2, sh2)


def _ln2_loss(x1, f, tgt, gt2, g2, b2, *, ts):
    Bl, S, D = x1.shape

    def body(x1_ref, f_ref, t_ref, gt_ref, g_ref, b_ref, df_ref, dx1_ref, dgt_ref, dg_ref, db_ref, loss_ref):
        s = pl.program_id(1)

        @pl.when(_first_step())
        def _():
            dg_ref[...] = jnp.zeros_like(dg_ref)
            db_ref[...] = jnp.zeros_like(db_ref)
            loss_ref[...] = jnp.zeros_like(loss_ref)

        @pl.when(s == 0)
        def _():
            dgt_ref[...] = jnp.zeros_like(dgt_ref)

        fv = f_ref[...]
        z = ALPHA * x1_ref[...] + (1.0 + gt_ref[...]) * fv
        xhat, rstd = _ln_stats(z)
        x2 = xhat * g_ref[...] + b_ref[...]
        err = x2 - t_ref[...]
        loss_ref[...] += 0.5 * jnp.sum(jnp.mean(err * err, axis=-1, keepdims=True))
        dy = err * (1.0 / D)
        dg_ref[...] += _colsum(dy * xhat)
        db_ref[...] += _colsum(dy)
        dz = _ln_bwd(dy, xhat, rstd, g_ref[...])
        dx1_ref[...] = ALPHA * dz
        dgt_ref[...] += _colsum(dz * fv)
        df_ref[...] = (dz * (1.0 + gt_ref[...])).astype(BF16)

    return pl.pallas_call(
        body, name="ln2_loss", grid=(Bl, S // ts),
        in_specs=[_tok_spec(ts, D), _tok_spec(ts, D), _tok_spec(ts, D), _brow_spec(D), _vec_spec(D), _vec_spec(D)],
        out_specs=(_tok_spec(ts, D), _tok_spec(ts, D), _brow_spec(D), _vec_spec(D), _vec_spec(D),
                   pl.BlockSpec((SUBLANES, LANES), lambda b, s: (0, 0))),
        out_shape=(jax.ShapeDtypeStruct((Bl, S, D), BF16), jax.ShapeDtypeStruct((Bl, S, D), F32),
                   jax.ShapeDtypeStruct((Bl, 1, D), F32), jax.ShapeDtypeStruct((1, D), F32),
                   jax.ShapeDtypeStruct((1, D), F32), jax.ShapeDtypeStruct((SUBLANES, LANES), F32)),
        compiler_params=_cparams(2),
    )(x1, f, tgt, gt2, g2, b2)


def _ln1_bwd(dx1p, dh2, x1, x, mix, sc2, gt1, g1, *, ts):
    Bl, S, D = x.shape

    def body(dx1p_ref, dh2_ref, x1_ref, x_ref, mix_ref, sc_ref, gt_ref, g_ref,
             dxp_ref, dmix_ref, dsc_ref, dsh_ref, dgt_ref, dg_ref, db_ref):
        s = pl.program_id(1)

        @pl.when(_first_step())
        def _():
            dg_ref[...] = jnp.zeros_like(dg_ref)
            db_ref[...] = jnp.zeros_like(db_ref)

        @pl.when(s == 0)
        def _():
            dsc_ref[...] = jnp.zeros_like(dsc_ref)
            dsh_ref[...] = jnp.zeros_like(dsh_ref)
            dgt_ref[...] = jnp.zeros_like(dgt_ref)

        dh2 = dh2_ref[...].astype(F32)
        mixv = mix_ref[...].astype(F32)
        dsc_ref[...] += _colsum(dh2 * x1_ref[...])
        dsh_ref[...] += _colsum(dh2)
        dx1 = dx1p_ref[...] + dh2 * (1.0 + sc_ref[...])
        z = ALPHA * x_ref[...] + (1.0 + gt_ref[...]) * mixv
        xhat, rstd = _ln_stats(z)
        dg_ref[...] += _colsum(dx1 * xhat)
        db_ref[...] += _colsum(dx1)
        dz = _ln_bwd(dx1, xhat, rstd, g_ref[...])
        dxp_ref[...] = ALPHA * dz
        dgt_ref[...] += _colsum(dz * mixv)
        dmix_ref[...] = (dz * (1.0 + gt_ref[...])).astype(BF16)

    return pl.pallas_call(
        body, name="ln1_bwd", grid=(Bl, S // ts),
        in_specs=[_tok_spec(ts, D)] * 5 + [_brow_spec(D), _brow_spec(D), _vec_spec(D)],
        out_specs=(_tok_spec(ts, D), _tok_spec(ts, D), _brow_spec(D), _brow_spec(D), _brow_spec(D), _vec_spec(D),
                   _vec_spec(D)),
        out_shape=(jax.ShapeDtypeStruct((Bl, S, D), F32), jax.ShapeDtypeStruct((Bl, S, D), BF16),
                   jax.ShapeDtypeStruct((Bl, 1, D), F32), jax.ShapeDtypeStruct((Bl, 1, D), F32),
                   jax.ShapeDtypeStruct((Bl, 1, D), F32), jax.ShapeDtypeStruct((1, D), F32),
                   jax.ShapeDtypeStruct((1, D), F32)),
        compiler_params=_cparams(2),
    )(dx1p, dh2, x1, x, mix, sc2, gt1, g1)


def _merge_bwd(dmerged, y_a, y_b, proj, *, ts, d):
    Bl, S, din = proj.shape
    gcol = (din - 2 * d) // (2 * d)

    def body(dm_ref, ya_ref, yb_ref, g_ref, dya_ref, dyb_ref, dp_ref, db_ref):
        @pl.when(_first_step())
        def _():
            db_ref[...] = jnp.zeros_like(db_ref)

        dm = dm_ref[...].astype(F32)
        sa = _sigmoid(g_ref[:, 0:d].astype(F32))
        sb = _sigmoid(g_ref[:, d:2 * d].astype(F32))
        dya_ref[...] = (dm * sa).astype(BF16)
        dyb_ref[...] = (dm * sb).astype(BF16)
        dga = dm * ya_ref[...].astype(F32) * sa * (1.0 - sa)
        dgb = dm * yb_ref[...].astype(F32) * sb * (1.0 - sb)
        dp_ref[:, 0:d] = dga.astype(BF16)
        dp_ref[:, d:2 * d] = dgb.astype(BF16)
        db_ref[:, 0:d] += _colsum(dga)
        db_ref[:, d:2 * d] += _colsum(dgb)

    return pl.pallas_call(
        body, name="merge_bwd", grid=(Bl, S // ts),
        in_specs=[_tok_spec(ts, d), _tok_spec(ts, d), _tok_spec(ts, d), _tok_spec(ts, 2 * d, gcol)],
        out_specs=(_tok_spec(ts, d), _tok_spec(ts, d), _tok_spec(ts, 2 * d, gcol), _vec_spec(2 * d)),
        out_shape=(jax.ShapeDtypeStruct((Bl, S, d), BF16), jax.ShapeDtypeStruct((Bl, S, d), BF16),
                   jax.ShapeDtypeStruct((Bl, S, din), BF16), jax.ShapeDtypeStruct((1, 2 * d), F32)),
        compiler_params=_cparams(2),
    )(dmerged, y_a, y_b, proj)


def _mix_bwd(proj, hs, dya, dys, dproj, w_conv, b_conv, w_rg_a, b_rg_a, w_rg_x, b_rg_x, lam, w_sp, b_sp_t,
             ln_v_g, ln_v_b, *, tm, lw, sw):
    Bl, S, din = proj.shape
    heads, hd = w_rg_a.shape[0], w_rg_a.shape[1]
    groups = w_sp.shape[0]
    gw = sw // groups
    cw = 2 * lw + 2 * sw
    nblk = tm // SGU_BLOCK
    n_s = S // tm
    per8 = tm // SUBLANES
    halo_rows = 2 * SUBLANES

    def body(p_ref, xh_ref, hs_ref, hh_ref, dya_ref, dys_ref, dpin_ref,
             wc_ref, bc_ref, wa_ref, ba_ref, wx_ref, bx_ref, lam_ref, wsp_ref, bsp_ref, lg_ref, lb_ref,
             dp_ref, dbin_ref, dwc_ref, dbc_ref, dwa_ref, dba_ref, dwx_ref, dbx_ref, dlam_ref, dwsp_ref, dbsp_ref,
             dlg_ref, dlb_ref,
             dhcar, acar, dxcn, a_scr, b_scr, g_scr):
        del dpin_ref
        sr = pl.program_id(1)
        first_tile = sr == n_s - 1

        @pl.when(_first_step())
        def _():
            for ref in (dbin_ref, dwc_ref, dbc_ref, dwa_ref, dba_ref, dwx_ref, dbx_ref, dlam_ref, dwsp_ref, dbsp_ref,
                        dlg_ref, dlb_ref):
                ref[...] = jnp.zeros_like(ref)

        @pl.when(sr == 0)
        def _():
            dhcar[...] = jnp.zeros_like(dhcar)
            acar[...] = jnp.zeros_like(acar)
            dxcn[...] = jnp.zeros_like(dxcn)

        keep = jnp.where(first_tile, 0.0, 1.0)
        xl = p_ref[:, 0:lw].astype(F32)
        gl = p_ref[:, lw:2 * lw].astype(F32)
        row8 = lax.broadcasted_iota(jnp.int32, (SUBLANES, lw), 0)
        rowm = lax.broadcasted_iota(jnp.int32, (tm, lw), 0) & (SUBLANES - 1)

        prev = xh_ref[...].astype(F32)[halo_rows - SUBLANES:halo_rows] * keep
        xsh = [xl]
        for k in (1, 2, 3):
            xr = pltpu.roll(xl, k, 0)
            head = jnp.where(row8 < k, pltpu.roll(prev, k, 0), xr[0:SUBLANES])
            xsh.append(jnp.concatenate([head, xr[SUBLANES:]], axis=0))
        xc = bc_ref[...] + xsh[0] * wc_ref[3:4, :]
        for k in (1, 2, 3):
            xc = xc + xsh[k] * wc_ref[3 - k:4 - k, :]
        xcb = xc.astype(BF16)
        pa = jnp.concatenate([jnp.dot(xcb[:, h * hd:(h + 1) * hd], wa_ref[h], preferred_element_type=F32)
                              for h in range(heads)], axis=1) + ba_ref[...]
        px = jnp.concatenate([jnp.dot(xcb[:, h * hd:(h + 1) * hd], wx_ref[h], preferred_element_type=F32)
                              for h in range(heads)], axis=1) + bx_ref[...]
        r = _sigmoid(pa)
        ig = _sigmoid(px)
        nl = -lam_ref[...]
        big_l = -LRU_C * (jnp.maximum(nl, 0.0) + _log1p_pos(jnp.exp(-jnp.abs(nl))))
        la = big_l * r
        a = jnp.exp(la)
        msq = -_expm1(2.0 * la)
        m = jnp.sqrt(msq)
        hsv = hs_ref[...]
        ggl, dggl = _gelu_and_grad(gl)

        dyav = dya_ref[...]
        dhs = dyav * ggl
        dp_ref[:, lw:2 * lw] = (dyav * hsv * dggl).astype(BF16)
        dbin_ref[:, lw:2 * lw] += _colsum(dyav * hsv * dggl)
        a_up = pltpu.roll(a, tm - 1, 0)
        tail = jnp.where(row8 == SUBLANES - 1, acar[...], a_up[tm - SUBLANES:tm])
        an = jnp.concatenate([a_up[:tm - SUBLANES], tail], axis=0)
        acar[...] = jnp.broadcast_to(a[0:1, :], (SUBLANES, lw))
        bb = dhs
        for d in (1, 2, 4):
            a_sh = pltpu.roll(an, tm - d, 0)
            b_sh = pltpu.roll(bb, tm - d, 0)
            msk = rowm < SUBLANES - d
            bb = jnp.where(msk, an * b_sh + bb, bb)
            an = jnp.where(msk, an * a_sh, an)
        a_scr[...] = an
        b_scr[...] = bb

        def grp(i, carry):
            off = pl.multiple_of((per8 - 1 - i) * SUBLANES, SUBLANES)
            g = b_scr[pl.ds(off, SUBLANES), :] + a_scr[pl.ds(off, SUBLANES), :] * carry
            g_scr[pl.ds(off, SUBLANES), :] = g
            return jnp.broadcast_to(g[0:1, :], g.shape)

        dhcar[...] = lax.fori_loop(0, per8, grp, dhcar[...])
        dh = g_scr[...]

        hr = pltpu.roll(hsv, 1, 0)
        hhead = jnp.where(row8 < 1, pltpu.roll(hh_ref[...] * keep, 1, 0), hr[0:SUBLANES])
        hprev = jnp.concatenate([hhead, hr[SUBLANES:]], axis=0)
        da = dh * hprev
        ixc = ig * xc
        dm = dh * ixc
        dixc = dh * m
        di = dixc * xc
        dxc = dixc * ig
        dla = da * a - dm * (a * a) / m
        dlam_ref[...] += _colsum(dla * r) * (LRU_C * _sigmoid(nl))
        dr = dla * big_l
        dpa = dr * r * (1.0 - r)
        dpx = di * ig * (1.0 - ig)
        dba_ref[...] += _colsum(dpa)
        dbx_ref[...] += _colsum(dpx)
        dpab = dpa.astype(BF16)
        dpxb = dpx.astype(BF16)
        nt = (((1,), (1,)), ((), ()))
        tn = (((0,), (0,)), ((), ()))
        dxc_g = []
        for h in range(heads):
            sl = slice(h * hd, (h + 1) * hd)
            dxc_g.append(lax.dot_general(dpab[:, sl], wa_ref[h], nt, preferred_element_type=F32)
                         + lax.dot_general(dpxb[:, sl], wx_ref[h], nt, preferred_element_type=F32))
            dwa_ref[h] += lax.dot_general(xcb[:, sl], dpab[:, sl], tn, preferred_element_type=F32)
            dwx_ref[h] += lax.dot_general(xcb[:, sl], dpxb[:, sl], tn, preferred_element_type=F32)
        dxc = dxc + jnp.concatenate(dxc_g, axis=1)

        dbc_ref[...] += _colsum(dxc)
        for k in range(4):
            dwc_ref[k:k + 1, :] += _colsum(dxc * xsh[3 - k])
        nxt = dxcn[...]
        dxl = dxc * wc_ref[3:4, :]
        for k in (1, 2, 3):
            ur = pltpu.roll(dxc, tm - k, 0)
            tl = jnp.where(row8 >= SUBLANES - k, pltpu.roll(nxt, SUBLANES - k, 0), ur[tm - SUBLANES:tm])
            dxl = dxl + jnp.concatenate([ur[:tm - SUBLANES], tl], axis=0) * wc_ref[3 - k:4 - k, :]
        dxcn[...] = dxc[0:SUBLANES]
        dp_ref[:, 0:lw] = dxl.astype(BF16)
        dbin_ref[:, 0:lw] += _colsum(dxl)

        gu, dgu_dx = _gelu_and_grad(p_ref[:, 2 * lw:2 * lw + sw].astype(F32))
        gv, dgv_dx = _gelu_and_grad(p_ref[:, 2 * lw + sw:cw].astype(F32))
        xhat, rstd = _ln_stats(gv)
        vn = (xhat * lg_ref[...] + lb_ref[...]).astype(BF16)
        dys = dys_ref[...]
        dmixed = dys * gu
        dmb = dmixed.astype(BF16)
        tpos = lax.broadcasted_iota(jnp.int32, (SGU_BLOCK, SGU_BLOCK), 0) // CHUNK
        spos = lax.broadcasted_iota(jnp.int32, (SGU_BLOCK, SGU_BLOCK), 1) // CHUNK
        causal = spos <= tpos
        mixed_rows, dvn_rows = [], []
        for blk in range(nblk):
            rs = slice(blk * SGU_BLOCK, (blk + 1) * SGU_BLOCK)
            mcols, dcols = [], []
            for g in range(groups):
                cs = slice(g * gw, (g + 1) * gw)
                wm = jnp.where(causal, wsp_ref[g], 0.0).astype(BF16)
                mcols.append(jnp.dot(wm, vn[rs, cs], preferred_element_type=F32) + bsp_ref[:, g:g + 1])
                dcols.append(lax.dot_general(wm, dmb[rs, cs], tn, preferred_element_type=F32))
                dw = lax.dot_general(dmb[rs, cs], vn[rs, cs], nt, preferred_element_type=F32)
                dwsp_ref[g] += jnp.where(causal, dw, 0.0)
                dbsp_ref[:, g:g + 1] += jnp.sum(dmixed[rs, cs], axis=1, keepdims=True)
            mixed_rows.append(jnp.concatenate(mcols, axis=1))
            dvn_rows.append(jnp.concatenate(dcols, axis=1))
        mixed_all = jnp.concatenate(mixed_rows, axis=0) if nblk > 1 else mixed_rows[0]
        dvn = jnp.concatenate(dvn_rows, axis=0) if nblk > 1 else dvn_rows[0]
        du = dys * mixed_all * dgu_dx
        dlg_ref[...] += _colsum(dvn * xhat)
        dlb_ref[...] += _colsum(dvn)
        dv = _ln_bwd(dvn, xhat, rstd, lg_ref[...]) * dgv_dx
        dp_ref[:, 2 * lw:2 * lw + sw] = du.astype(BF16)
        dp_ref[:, 2 * lw + sw:cw] = dv.astype(BF16)
        dbin_ref[:, 2 * lw:2 * lw + sw] += _colsum(du)
        dbin_ref[:, 2 * lw + sw:cw] += _colsum(dv)

    rev = lambda s: n_s - 1 - s
    tile = lambda w: pl.BlockSpec((None, tm, w), lambda b, s: (b, rev(s), 0))
    halo = lambda w: pl.BlockSpec((None, SUBLANES, w), lambda b, s: (b, jnp.maximum(rev(s) * per8 - 1, 0), 0))
    xhalo = pl.BlockSpec((None, halo_rows, lw), lambda b, s: (b, jnp.maximum(rev(s) * (tm // halo_rows) - 1, 0), 0))
    full = lambda shp: pl.BlockSpec(shp, lambda b, s: (0,) * len(shp))
    small = [w_conv, b_conv, w_rg_a, b_rg_a, w_rg_x, b_rg_x, lam, w_sp, b_sp_t, ln_v_g, ln_v_b]
    acc_shapes = [(1, cw), w_conv.shape, b_conv.shape, w_rg_a.shape, b_rg_a.shape, w_rg_x.shape, b_rg_x.shape,
                  lam.shape, w_sp.shape, b_sp_t.shape, ln_v_g.shape, ln_v_b.shape]
    res = pl.pallas_call(
        body, name="mix_bwd", grid=(Bl, n_s),
        in_specs=[tile(cw), xhalo, tile(lw), halo(lw), tile(lw), tile(sw), pl.BlockSpec(memory_space=pl.ANY)]
                 + [full(w.shape) for w in small],
        out_specs=tuple([tile(cw)] + [full(shp) for shp in acc_shapes]),
        out_shape=tuple([jax.ShapeDtypeStruct((Bl, S, din), BF16)] + [jax.ShapeDtypeStruct(shp, F32) for shp in acc_shapes]),
        input_output_aliases={6: 0},
        scratch_shapes=[pltpu.VMEM((SUBLANES, lw), F32), pltpu.VMEM((SUBLANES, lw), F32), pltpu.VMEM((SUBLANES, lw), F32),
                        pltpu.VMEM((tm, lw), F32), pltpu.VMEM((tm, lw), F32), pltpu.VMEM((tm, lw), F32)],
        compiler_params=_cparams(2, big=True),
    )(proj, proj, hs, hs, dya, dys, dproj, *small)
    return res


def _final_dx(dxp, dh, x, sc1, *, ts):
    Bl, S, D = x.shape

    def body(dxp_ref, dh_ref, x_ref, sc_ref, dx_ref, dsc_ref, dsh_ref):
        @pl.when(pl.program_id(1) == 0)
        def _():
            dsc_ref[...] = jnp.zeros_like(dsc_ref)
            dsh_ref[...] = jnp.zeros_like(dsh_ref)

        dh = dh_ref[...]
        dx_ref[...] = dxp_ref[...] + dh * (1.0 + sc_ref[...])
        dsc_ref[...] += _colsum(dh * x_ref[...])
        dsh_ref[...] += _colsum(dh)

    return pl.pallas_call(
        body, name="final_dx", grid=(Bl, S // ts),
        in_specs=[_tok_spec(ts, D), _tok_spec(ts, D), _tok_spec(ts, D), _brow_spec(D)],
        out_specs=(_tok_spec(ts, D), _brow_spec(D), _brow_spec(D)),
        out_shape=(jax.ShapeDtypeStruct((Bl, S, D), F32), jax.ShapeDtypeStruct((Bl, 1, D), F32),
                   jax.ShapeDtypeStruct((Bl, 1, D), F32)),
        compiler_params=_cparams(2),
    )(dxp, dh, x, sc1)


def _ada_fwd(c_all, w_ada):
    R, D = c_all.shape
    nb = w_ada.shape[1]

    def body(c_ref, w_ref, act_ref, o_ref):
        cv = c_ref[...]
        act = (cv * _sigmoid(cv)).astype(BF16)
        act_ref[...] = act
        o_ref[...] = jnp.dot(act, w_ref[...].astype(BF16), preferred_element_type=F32)

    return pl.pallas_call(
        body, name="ada_fwd",
        out_shape=(jax.ShapeDtypeStruct((R, D), BF16), jax.ShapeDtypeStruct((R, nb), F32)),
        compiler_params=pltpu.CompilerParams(vmem_limit_bytes=VMEM_LIMIT),
    )(c_all, w_ada)


def _ada_bwd(c_act, dmod_cols):
    R, D = c_act.shape
    nb = dmod_cols.shape[1]

    def body(act_ref, d_ref, o_ref):
        o_ref[...] = lax.dot_general(act_ref[...], d_ref[...].astype(BF16), (((0,), (0,)), ((), ())),
                                     preferred_element_type=F32)

    return pl.pallas_call(
        body, name="ada_bwd", out_shape=jax.ShapeDtypeStruct((D, nb), F32),
        compiler_params=pltpu.CompilerParams(vmem_limit_bytes=VMEM_LIMIT),
    )(c_act, dmod_cols)


def _adamw(w, g_slots, m, v, *, tr, name):
    R, C = w.shape
    n_slot = g_slots.shape[0]
    tr = min(tr, R)
    assert R % tr == 0, (name, R, tr)
    c1 = 1.0 / (1.0 - ADAM_B1 ** ADAM_STEP)
    c2 = 1.0 / (1.0 - ADAM_B2 ** ADAM_STEP)

    def body(w_ref, g_ref, m_ref, v_ref, go_ref, d_ref, mo_ref, vo_ref):
        g = g_ref[0].astype(F32)
        for i in range(1, n_slot):
            g = g + g_ref[i].astype(F32)
        mn = ADAM_B1 * m_ref[...] + (1.0 - ADAM_B1) * g
        vn = ADAM_B2 * v_ref[...] + (1.0 - ADAM_B2) * (g * g)
        go_ref[...] = g
        mo_ref[...] = mn
        vo_ref[...] = vn
        d_ref[...] = -ADAM_LR * ((mn * c1) / (jnp.sqrt(vn * c2) + ADAM_EPS) + ADAM_WD * w_ref[...])

    blk = pl.BlockSpec((tr, C), lambda i: (i, 0))
    return pl.pallas_call(
        body, name=name, grid=(R // tr,),
        in_specs=[blk, pl.BlockSpec((n_slot, tr, C), lambda i: (0, i, 0)), blk, blk],
        out_specs=(blk, blk, blk, blk),
        out_shape=tuple(jax.ShapeDtypeStruct((R, C), F32) for _ in range(4)),
        compiler_params=_cparams(1, big=True),
    )(w, g_slots, m, v)


def _sum_slots(g_slots, *, name):
    n_slot, R, C = g_slots.shape

    def body(g_ref, o_ref):
        g = g_ref[0]
        for i in range(1, n_slot):
            g = g + g_ref[i]
        o_ref[...] = g

    return pl.pallas_call(body, name=name, out_shape=jax.ShapeDtypeStruct((R, C), F32),
                          compiler_params=pltpu.CompilerParams(vmem_limit_bytes=VMEM_LIMIT))(g_slots)


SMALL_NAMES = ("b_ada", "b_in", "b_conv", "w_rg_a", "b_rg_a", "w_rg_x", "b_rg_x", "lru_lambda", "w_sp", "b_sp",
               "ln_v_g", "ln_v_b", "ln1_g", "ln1_b", "ln2_g", "ln2_b")
BIG_NAMES = ("w_ada", "w_in", "w_conv", "w_o_lru", "w_o_sgu", "w_out", "w_up", "w_down")
WEIGHT_ORDER = ("w_ada", "b_ada", "w_in", "b_in", "w_conv", "b_conv", "w_rg_a", "b_rg_a", "w_rg_x", "b_rg_x",
                "lru_lambda", "w_sp", "b_sp", "ln_v_g", "ln_v_b", "w_o_lru", "w_o_sgu", "w_out", "ln1_g", "ln1_b",
                "w_up", "w_down", "ln2_g", "ln2_b")


def _pack_small(d):
    flat = jnp.concatenate([d[n].reshape(-1) for n in SMALL_NAMES])
    rows = -(-flat.shape[0] // LANES)
    rows = -(-rows // (N_DEV * SUBLANES)) * (N_DEV * SUBLANES)
    flat = jnp.pad(flat, (0, rows * LANES - flat.shape[0]))
    return flat.reshape(rows, LANES)


def _unpack_small(packed, like):
    flat = packed.reshape(-1)
    out, off = {}, 0
    for n in SMALL_NAMES:
        sz = like[n].size
        out[n] = flat[off:off + sz].reshape(like[n].shape)
        off += sz
    return out


def _blocked_cols(w2d):
    K, N = w2d.shape
    return jnp.transpose(w2d.reshape(K, N_DEV, N // N_DEV), (1, 0, 2))


def _unblock_cols(wb):
    n, K, nb = wb.shape
    return jnp.transpose(wb, (1, 0, 2)).reshape(K, n * nb)


def kernel(x, c, w_ada, b_ada, w_in, b_in, w_conv, b_conv, w_rg_a, b_rg_a, w_rg_x, b_rg_x, lru_lambda, w_sp, b_sp, ln_v_g, ln_v_b, w_o_lru, w_o_sgu, w_out, ln1_g, ln1_b, w_up, w_down, ln2_g, ln2_b, loss_target, m_w_ada, m_b_ada, m_w_in, m_b_in, m_w_conv, m_b_conv, m_w_rg_a, m_b_rg_a, m_w_rg_x, m_b_rg_x, m_lru_lambda, m_w_sp, m_b_sp, m_ln_v_g, m_ln_v_b, m_w_o_lru, m_w_o_sgu, m_w_out, m_ln1_g, m_ln1_b, m_w_up, m_w_down, m_ln2_g, m_ln2_b, v_w_ada, v_b_ada, v_w_in, v_b_in, v_w_conv, v_b_conv, v_w_rg_a, v_b_rg_a, v_w_rg_x, v_b_rg_x, v_lru_lambda, v_w_sp, v_b_sp, v_ln_v_g, v_ln_v_b, v_w_o_lru, v_w_o_sgu, v_w_out, v_ln1_g, v_ln1_b, v_w_up, v_w_down, v_ln2_g, v_ln2_b):
    W = dict(w_ada=w_ada, b_ada=b_ada, w_in=w_in, b_in=b_in, w_conv=w_conv, b_conv=b_conv, w_rg_a=w_rg_a,
             b_rg_a=b_rg_a, w_rg_x=w_rg_x, b_rg_x=b_rg_x, lru_lambda=lru_lambda, w_sp=w_sp, b_sp=b_sp,
             ln_v_g=ln_v_g, ln_v_b=ln_v_b, w_o_lru=w_o_lru, w_o_sgu=w_o_sgu, w_out=w_out, ln1_g=ln1_g, ln1_b=ln1_b,
             w_up=w_up, w_down=w_down, ln2_g=ln2_g, ln2_b=ln2_b)
    Mo = dict(w_ada=m_w_ada, b_ada=m_b_ada, w_in=m_w_in, b_in=m_b_in, w_conv=m_w_conv, b_conv=m_b_conv,
              w_rg_a=m_w_rg_a, b_rg_a=m_b_rg_a, w_rg_x=m_w_rg_x, b_rg_x=m_b_rg_x, lru_lambda=m_lru_lambda,
              w_sp=m_w_sp, b_sp=m_b_sp, ln_v_g=m_ln_v_g, ln_v_b=m_ln_v_b, w_o_lru=m_w_o_lru, w_o_sgu=m_w_o_sgu,
              w_out=m_w_out, ln1_g=m_ln1_g, ln1_b=m_ln1_b, w_up=m_w_up, w_down=m_w_down, ln2_g=m_ln2_g,
              ln2_b=m_ln2_b)
    Vo = dict(w_ada=v_w_ada, b_ada=v_b_ada, w_in=v_w_in, b_in=v_b_in, w_conv=v_w_conv, b_conv=v_b_conv,
              w_rg_a=v_w_rg_a, b_rg_a=v_b_rg_a, w_rg_x=v_w_rg_x, b_rg_x=v_b_rg_x, lru_lambda=v_lru_lambda,
              w_sp=v_w_sp, b_sp=v_b_sp, ln_v_g=v_ln_v_g, ln_v_b=v_ln_v_b, w_o_lru=v_w_o_lru, w_o_sgu=v_w_o_sgu,
              w_out=v_w_out, ln1_g=v_ln1_g, ln1_b=v_ln1_b, w_up=v_w_up, w_down=v_w_down, ln2_g=v_ln2_g,
              ln2_b=v_ln2_b)

    Bl, S, D = x.shape
    T = Bl * S
    lw = b_conv.shape[-1]
    sw = ln_v_g.shape[-1]
    din = b_in.shape[-1]
    dff = w_up.shape[-1] * N_DEV
    ts = min(512, S)
    tmix = min(256, S)
    trow = min(512, S)

    c_pad = jnp.pad(c, ((0, SUBLANES - Bl), (0, 0)))
    c_g, wconv_g = _exchange([c_pad, w_conv[0]], True, "xchg_c")
    wconv_full = _unblock_cols(wconv_g)
    c_act, modcols = _ada_fwd(c_g.reshape(N_DEV * SUBLANES, D), w_ada[0])
    (mod_slots,) = _exchange([modcols.reshape(N_DEV, SUBLANES, -1)], False, "xchg_mod")

    wnames = ("win", "wol", "wos", "wout", "wup", "wdown")
    shards = [w_in[0].astype(BF16), w_o_lru[0].astype(BF16), w_o_sgu[0].astype(BF16), w_out[0].astype(BF16),
              w_up[0].astype(BF16), w_down[0].astype(BF16)]
    g_send, g_recv, g_src, g_land, g_tok = _xstart(shards, True, mod_slots, "gather_start")
    gidx = {n: i for i, n in enumerate(wnames)}

    def gathered(n, after):
        i = gidx[n]
        return _xwait(g_src[i], g_land[i], g_send[i], g_recv[i], after, True, "gather_wait_" + n)

    mod = _unblock_cols(mod_slots)[:Bl] + (b_ada + g_tok[0, 0])
    sh1, sc1, gt1, sh2, sc2, gt2 = [mod[:, i * D:(i + 1) * D].reshape(Bl, 1, D) for i in range(6)]

    wa_b, wx_b = w_rg_a[0].astype(BF16), w_rg_x[0].astype(BF16)
    b_sp_t = jnp.transpose(b_sp[0])
    small_mix = (wconv_full, b_conv, wa_b, b_rg_a, wx_b, b_rg_x, lru_lambda, w_sp[0], b_sp_t, ln_v_g, ln_v_b)

    h = _modulate(x, sc1, sh1, ts)
    Win = _unblock_cols(gathered("win", h))
    proj = _mm(h.reshape(T, D), Win, mode="nn", tm=2048, tn=din // 4, tk=D, outs=[BF16],
               extras=[(b_in, "row")], epilogue=lambda acc, ex: (acc + ex[0],), name="mm_proj")
    proj3 = proj.reshape(Bl, S, din)
    hs, ya_pre, ysgu = _mix_fwd(proj3, *small_mix, tm=tmix, lw=lw, sw=sw)
    Wol = gathered("wol", ya_pre).reshape(lw, D)
    Wos = _unblock_cols(gathered("wos", ysgu))
    y_a = _mm(ya_pre.reshape(T, lw), Wol, mode="nn", tm=2048, tn=D, tk=lw, outs=[BF16], name="mm_ya")
    x2d, tgt2d = x.reshape(T, D), loss_target.reshape(T, D)
    gate_cb = (din - 2 * D) // D

    def ep_merge(y_b, v):
        ya, ga, gb = [t.astype(F32) for t in v]
        yb = y_b.astype(BF16).astype(F32)
        return [yb, _sigmoid(ga) * ya + _sigmoid(gb) * yb]

    y_b, merged = _mm_rows(ysgu.reshape(T, sw), Wos, mode="nn", tm=trow, seq=S,
                           ins=[("tile", y_a), ("tilecol", proj, D, gate_cb), ("tilecol", proj, D, gate_cb + 1)],
                           outs=[("tile", BF16, D), ("tile", BF16, D)], epilogue=ep_merge, name="mm_yb_merge")
    Wout = gathered("wout", merged).reshape(D, D)

    def ep_ln1(mix_acc, v):
        x_, gt, g, b, sc, sh = v
        mixr = mix_acc.astype(BF16).astype(F32)
        xhat, _ = _ln_stats(ALPHA * x_ + (1.0 + gt) * mixr)
        x1_ = xhat * g + b
        return [mixr, x1_, x1_ * (1.0 + sc) + sh]

    mix, x1, h2 = _mm_rows(merged, Wout, mode="nn", tm=trow, seq=S,
                           ins=[("tile", x2d), ("brow", gt1), ("row", ln1_g), ("row", ln1_b), ("brow", sc2),
                                ("brow", sh2)],
                           outs=[("tile", BF16, D), ("tile", F32, D), ("tile", BF16, D)], epilogue=ep_ln1,
                           name="mm_mix_ln1")
    Wup = _unblock_cols(gathered("wup", h2))
    act = _mm(h2, Wup, mode="nn", tm=2048, tn=1024, tk=D, outs=[BF16],
              epilogue=lambda acc, ex: (jnp.square(jnp.maximum(acc, 0.0)),), name="mm_up")
    Wdown = gathered("wdown", act).reshape(dff, D)

    def ep_ln2(f_acc, v):
        x1_, t_, gt, g, b = v
        xhat, rstd = _ln_stats(ALPHA * x1_ + (1.0 + gt) * f_acc)
        err = xhat * g + b - t_
        loss_t = 0.5 * jnp.sum(jnp.mean(err * err, axis=-1, keepdims=True))
        dy = err * (1.0 / D)
        dz = _ln_bwd(dy, xhat, rstd, g)
        return [dz * (1.0 + gt), ALPHA * dz, _colsum(dz * f_acc), _colsum(dy * xhat), _colsum(dy), loss_t]

    df2, dx1p, dgt2, dg2, db2, loss_part = _mm_rows(
        act, Wdown, mode="nn", tm=trow, seq=S,
        ins=[("tile", x1), ("tile", tgt2d), ("brow", gt2), ("row", ln2_g), ("row", ln2_b)],
        outs=[("tile", BF16, D), ("tile", F32, D), ("acc_brow", D), ("acc_row", D), ("acc_row", D), ("acc_scalar",)],
        epilogue=ep_ln2, name="mm_down_ln2")
    loss = lax.psum(loss_part[0, 0], ("x", "y", "c"))

    def send_grads(parts, name):
        snd, rcv, src, land, tok = _xstart(parts, False, None, name + "_start")
        return [(src[i], land[i], snd[i], rcv[i]) for i in range(len(parts))], tok

    dup = _mm(df2, Wdown, mode="nt", tm=2048, tn=1024, tk=D, outs=[BF16], extras=[(act, "tile")],
              epilogue=lambda acc, ex: (acc * (2.0 * jnp.sqrt(ex[0].astype(F32))),), name="mm_dup")
    g_wdown = _mm(act, df2, mode="tn", tm=1024, tn=D, tk=2048, outs=[BF16], name="mm_gwdown")
    (x_wdown,), tok = send_grads([g_wdown.reshape(N_DEV, dff // N_DEV, D)], "gx_wdown")
    def ep_ln1_bwd(dh2, v):
        dx1p_, x1_, x_, mix_, sc, gt, g = v
        mixv = mix_.astype(F32)
        dx1 = dx1p_ + dh2 * (1.0 + sc)
        xhat, rstd = _ln_stats(ALPHA * x_ + (1.0 + gt) * mixv)
        dz = _ln_bwd(dx1, xhat, rstd, g)
        return [ALPHA * dz, dz * (1.0 + gt), _colsum(dh2 * x1_), _colsum(dh2), _colsum(dz * mixv),
                _colsum(dx1 * xhat), _colsum(dx1)]

    dxp, dmix, dsc2, dsh2, dgt1, dg1, db1 = _mm_rows(
        dup, Wup, mode="nt", tm=trow, seq=S, tok=tok,
        ins=[("tile", dx1p), ("tile", x1), ("tile", x2d), ("tile", mix), ("brow", sc2), ("brow", gt1), ("row", ln1_g)],
        outs=[("tile", F32, D), ("tile", BF16, D), ("acc_brow", D), ("acc_brow", D), ("acc_brow", D), ("acc_row", D),
              ("acc_row", D)],
        epilogue=ep_ln1_bwd, name="mm_dh2_ln1b")
    g_wup = _mm(h2, dup, mode="tn", tm=D, tn=1024, tk=2048, outs=[BF16], nb=dff // N_DEV, name="mm_gwup")
    (x_wup,), tok = send_grads([g_wup], "gx_wup")

    def ep_merge_bwd(dm, v):
        ya, yb, ga, gb = [t.astype(F32) for t in v]
        sa, sb = _sigmoid(ga), _sigmoid(gb)
        dg = jnp.concatenate([dm * ya * sa * (1.0 - sa), dm * yb * sb * (1.0 - sb)], axis=1)
        return [dm * sa, dm * sb, dg, _colsum(dg)]

    dy_a, dy_b, dproj, dbin_hi = _mm_rows(
        dmix, Wout, mode="nt", tm=trow, seq=S, tok=tok,
        ins=[("tile", y_a), ("tile", y_b), ("tilecol", proj, D, gate_cb), ("tilecol", proj, D, gate_cb + 1)],
        outs=[("tile", BF16, D), ("tile", BF16, D), ("tilecol", BF16, 2 * D, gate_cb // 2, din), ("acc_row", 2 * D)],
        epilogue=ep_merge_bwd, name="mm_dmerged_mb")
    g_wout = _mm(merged, dmix, mode="tn", tm=D, tn=D, tk=2048, outs=[BF16], name="mm_gwout")
    (x_wout,), tok = send_grads([g_wout.reshape(N_DEV, D // N_DEV, D)], "gx_wout")
    dya_pre = _mm(dy_a, Wol, mode="nt", tm=2048, tn=lw, tk=D, outs=[F32], tok=tok, name="mm_dya")
    dysgu = _mm(dy_b, Wos, mode="nt", tm=2048, tn=sw, tk=D, outs=[F32], name="mm_dys")
    g_wol = _mm(ya_pre.reshape(T, lw), dy_a, mode="tn", tm=lw, tn=D, tk=2048, outs=[BF16], name="mm_gwol")
    g_wos = _mm(ysgu.reshape(T, sw), dy_b, mode="tn", tm=sw, tn=D, tk=2048, outs=[BF16], nb=D // N_DEV,
                name="mm_gwos")
    (x_wol, x_wos), tok = send_grads([g_wol.reshape(N_DEV, lw // N_DEV, D), g_wos], "gx_wo")
    small_mix_b = (wconv_full, b_conv + tok[0, 0]) + small_mix[2:]
    (dproj, dbin_lo, g_wconv, g_bconv, g_wa, g_ba, g_wx, g_bx, g_lam, g_wsp, g_bsp_t, g_lvg, g_lvb) = _mix_bwd(
        proj3, hs, dya_pre.reshape(Bl, S, lw), dysgu.reshape(Bl, S, sw), dproj.reshape(Bl, S, din), *small_mix_b,
        tm=tmix, lw=lw, sw=sw)
    dproj2 = dproj.reshape(T, din)
    g_win = _mm(h.reshape(T, D), dproj2, mode="tn", tm=D, tn=din // 4, tk=2048, outs=[BF16], nb=din // N_DEV,
                name="mm_gwin")
    (x_win,), tok = send_grads([g_win], "gx_win")

    def ep_final(dh, v):
        dxp_, x_, sc = v
        return [dxp_ + dh * (1.0 + sc), _colsum(dh * x_), _colsum(dh)]

    grad_x, dsc1, dsh1 = _mm_rows(dproj2, Win, mode="nt", tm=trow, seq=S, tok=tok,
                                  ins=[("tile", dxp), ("tile", x2d), ("brow", sc1)],
                                  outs=[("tile", F32, D), ("acc_brow", D), ("acc_brow", D)], epilogue=ep_final,
                                  name="mm_dh_final")
    grad_x = grad_x.reshape(Bl, S, D)

    dmod = jnp.concatenate([dsh1, dsc1, dgt1, dsh2, dsc2, dgt2], axis=-1).reshape(Bl, 6 * D)
    dmod_b = _blocked_cols(jnp.pad(dmod, ((0, SUBLANES - Bl), (0, 0))))
    g_small_local = dict(
        b_ada=jnp.sum(dmod, axis=0, keepdims=True), b_in=jnp.concatenate([dbin_lo, dbin_hi], axis=-1),
        b_conv=g_bconv, w_rg_a=g_wa[None], b_rg_a=g_ba, w_rg_x=g_wx[None], b_rg_x=g_bx, lru_lambda=g_lam,
        w_sp=g_wsp[None], b_sp=jnp.transpose(g_bsp_t)[None], ln_v_g=g_lvg, ln_v_b=g_lvb, ln1_g=dg1, ln1_b=db1,
        ln2_g=dg2, ln2_b=db2)
    gs_packed = _pack_small(g_small_local)
    rows = gs_packed.shape[0]
    parts = [dmod_b, _blocked_cols(g_wconv), gs_packed.reshape(N_DEV, rows // N_DEV, LANES)]
    dmod_s, gwconv_s, gsmall_s = _exchange(parts, False, "xchg_grads")
    gwdown_s = _xwait(*x_wdown, dmod_s, False, "gx_wdown_wait")
    gwup_s = _xwait(*x_wup, dmod_s, False, "gx_wup_wait")
    gwout_s = _xwait(*x_wout, dmod_s, False, "gx_wout_wait")
    gwol_s = _xwait(*x_wol, dmod_s, False, "gx_wol_wait")
    gwos_s = _xwait(*x_wos, dmod_s, False, "gx_wos_wait")
    gwin_s = _xwait(*x_win, dmod_s, False, "gx_win_wait")

    out_g, out_d, out_m, out_v = {}, {}, {}, {}

    def adam(name, g_slots, tr):
        shp = W[name].shape
        w2, m2, v2 = [t.reshape(g_slots.shape[1:]) for t in (W[name], Mo[name], Vo[name])]
        g, d, mn, vn = _adamw(w2, g_slots, m2, v2, tr=tr, name="adam_" + name)
        out_g[name], out_d[name], out_m[name], out_v[name] = [t.reshape(shp) for t in (g, d, mn, vn)]

    g_wada = _ada_bwd(c_act, dmod_s.reshape(N_DEV * SUBLANES, -1))
    adam("w_ada", g_wada[None], 256)
    adam("w_in", gwin_s, 256)
    adam("w_conv", gwconv_s, 8)
    adam("w_o_lru", gwol_s, 160)
    adam("w_o_sgu", gwos_s, 256)
    adam("w_out", gwout_s, 128)
    adam("w_up", gwup_s, 256)
    adam("w_down", gwdown_s, 256)

    g_chunk = _sum_slots(gsmall_s, name="sum_small")
    (gsmall_all,) = _exchange([g_chunk], True, "xchg_small")
    gs, ds, ms, vs = _adamw(_pack_small(W), gsmall_all.reshape(1, rows, LANES), _pack_small(Mo), _pack_small(Vo),
                            tr=rows // N_DEV, name="adam_small")
    for dst, packed in ((out_g, gs), (out_d, ds), (out_m, ms), (out_v, vs)):
        dst.update(_unpack_small(packed, W))

    return (loss, grad_x, *[out_g[n] for n in WEIGHT_ORDER], *[out_d[n] for n in WEIGHT_ORDER],
            *[out_m[n] for n in WEIGHT_ORDER], *[out_v[n] for n in WEIGHT_ORDER])
```

```python
import functools
import math

import jax
import jax.numpy as jnp
from jax import lax
from jax.experimental import pallas as pl
from jax.experimental.pallas import tpu as pltpu

N_DEV = 8
LN_EPS = 1e-5
LRU_C = 8.0
CHUNK = 64
SGU_BLOCK = 128
ALPHA = 2.0 ** 0.25
ADAM_LR = 0.001
ADAM_B1 = 0.9
ADAM_B2 = 0.999
ADAM_EPS = 1e-08
ADAM_WD = 0.01
ADAM_STEP = 10
GELU_K0 = math.sqrt(2.0 / math.pi)
GELU_K1 = 0.044715

SUBLANES = 8
LANES = 128
VMEM_LIMIT = 56 * 1024 * 1024

F32 = jnp.float32
BF16 = jnp.bfloat16
MESH = pl.DeviceIdType.MESH


def _cparams(n_axes, big=False):
    return pltpu.CompilerParams(dimension_semantics=("arbitrary",) * n_axes,
                                vmem_limit_bytes=VMEM_LIMIT if big else None)


def _sigmoid(x):
    return 0.5 * jnp.tanh(0.5 * x) + 0.5


def _gelu(x):
    t = jnp.tanh(GELU_K0 * (x + GELU_K1 * (x * x * x)))
    return 0.5 * x * (1.0 + t)


def _gelu_and_grad(x):
    x2 = x * x
    t = jnp.tanh(GELU_K0 * (x + GELU_K1 * (x2 * x)))
    g = 0.5 * x * (1.0 + t)
    dg = 0.5 * (1.0 + t) + 0.5 * x * (1.0 - t * t) * (GELU_K0 * (1.0 + 3.0 * GELU_K1 * x2))
    return g, dg


def _expm1(x):
    p = x * (1.0 + x * (1.0 / 2.0 + x * (1.0 / 6.0 + x * (1.0 / 24.0 + x * (1.0 / 120.0)))))
    return jnp.where(jnp.abs(x) < 0.0625, p, jnp.exp(x) - 1.0)


def _log1p_pos(e):
    p = e * (1.0 - e * (1.0 / 2.0) + e * e * (1.0 / 3.0) - e * e * e * (1.0 / 4.0))
    return jnp.where(e < 1e-2, p, jnp.log(1.0 + e))


def _ln_stats(z):
    mu = jnp.mean(z, axis=-1, keepdims=True)
    zc = z - mu
    var = jnp.mean(zc * zc, axis=-1, keepdims=True)
    rstd = lax.rsqrt(var + LN_EPS)
    return zc * rstd, rstd


def _ln_bwd(dy, xhat, rstd, g):
    dxh = dy * g
    m1 = jnp.mean(dxh, axis=-1, keepdims=True)
    m2 = jnp.mean(dxh * xhat, axis=-1, keepdims=True)
    return rstd * (dxh - m1 - xhat * m2)


def _colsum(v):
    return jnp.sum(v, axis=0, keepdims=True)


def _first_step():
    return jnp.logical_and(pl.program_id(0) == 0, pl.program_id(1) == 0)


def _exchange(arrs, gather, name):
    n = len(arrs)
    n_peer = N_DEV - 1

    def body(*refs):
        ins, outs = refs[:n], refs[n:2 * n]
        send_sems, recv_sems, loc_sems = refs[2 * n:]
        x, y, c = lax.axis_index("x"), lax.axis_index("y"), lax.axis_index("c")
        me = 4 * x + 2 * y + c
        started = []
        for a in range(n):
            src_me = ins[a] if gather else ins[a].at[me]
            lc = pltpu.make_async_copy(src_me, outs[a].at[me], loc_sems.at[a])
            lc.start()
            started.append((lc, None))
        for p in range(1, N_DEV):
            px, py, pc = x ^ ((p >> 2) & 1), y ^ ((p >> 1) & 1), c ^ (p & 1)
            peer = 4 * px + 2 * py + pc
            for a in range(n):
                k = a * n_peer + (p - 1)
                src = ins[a] if gather else ins[a].at[peer]
                cp = pltpu.make_async_remote_copy(src_ref=src, dst_ref=outs[a].at[me],
                                                  send_sem=send_sems.at[k], recv_sem=recv_sems.at[k],
                                                  device_id=(px, py, pc), device_id_type=MESH)
                cp.start()
                rc = pltpu.make_async_remote_copy(src_ref=src, dst_ref=outs[a].at[peer],
                                                  send_sem=send_sems.at[k], recv_sem=recv_sems.at[k],
                                                  device_id=(px, py, pc), device_id_type=MESH)
                started.append((cp, rc))
        for cp, rc in started:
            if rc is None:
                cp.wait()
            else:
                cp.wait_send()
                rc.wait_recv()

    hbm = pl.BlockSpec(memory_space=pltpu.HBM)
    out_shape = tuple(
        jax.ShapeDtypeStruct(((N_DEV,) + a.shape) if gather else a.shape, a.dtype) for a in arrs)
    return pl.pallas_call(
        body, name=name, out_shape=out_shape,
        in_specs=[hbm] * n, out_specs=tuple([hbm] * n),
        scratch_shapes=[pltpu.SemaphoreType.DMA((n * n_peer,)), pltpu.SemaphoreType.DMA((n * n_peer,)),
                        pltpu.SemaphoreType.DMA((n,))],
        compiler_params=pltpu.CompilerParams(has_side_effects=True),
    )(*arrs)


_HBM = pl.BlockSpec(memory_space=pltpu.HBM)
_SEM = pl.BlockSpec(memory_space=pltpu.SEMAPHORE)
_EFFECT = pltpu.SideEffectType.DATAFLOW_SIDE_EFFECTING


def _peer_of(p):
    x, y, c = lax.axis_index("x"), lax.axis_index("y"), lax.axis_index("c")
    px, py, pc = x ^ ((p >> 2) & 1), y ^ ((p >> 1) & 1), c ^ (p & 1)
    return (px, py, pc), 4 * px + 2 * py + pc


def _xstart(srcs, gather, after, name):
    n = len(srcs)
    lands = [lax.empty(((N_DEV,) + t.shape) if gather else t.shape, t.dtype) for t in srcs]
    n_after = 0 if after is None else 1

    def body(*refs):
        src_refs, land_refs = refs[:n], refs[n:2 * n]
        refs = refs[n_after:]
        send_sems, recv_sems = refs[2 * n:3 * n], refs[3 * n:4 * n]
        token = refs[6 * n]
        me = 4 * lax.axis_index("x") + 2 * lax.axis_index("y") + lax.axis_index("c")
        for a in range(n):
            for p in range(1, N_DEV):
                dev, peer = _peer_of(p)
                pltpu.make_async_remote_copy(
                    src_ref=src_refs[a] if gather else src_refs[a].at[peer], dst_ref=land_refs[a].at[me],
                    send_sem=send_sems[a].at[p - 1], recv_sem=recv_sems[a].at[p - 1],
                    device_id=dev, device_id_type=MESH).start()
        token[...] = jnp.zeros_like(token)

    sems = tuple(pltpu.SemaphoreType.DMA((N_DEV - 1,)) for _ in range(2 * n))
    thru = tuple(pltpu.HBM(t.shape, t.dtype) for t in list(srcs) + list(lands))
    res = pl.pallas_call(
        body, name=name,
        out_shape=sems + thru + (jax.ShapeDtypeStruct((SUBLANES, LANES), F32),),
        in_specs=[_HBM] * (2 * n) + [pl.BlockSpec(memory_space=pl.ANY)] * n_after,
        out_specs=tuple([_SEM] * (2 * n) + [_HBM] * (2 * n) + [pl.BlockSpec(memory_space=pltpu.VMEM)]),
        input_output_aliases={i: 2 * n + i for i in range(2 * n)},
        compiler_params=pltpu.CompilerParams(has_side_effects=_EFFECT),
    )(*[pltpu.with_memory_space_constraint(t, pltpu.HBM) for t in list(srcs) + list(lands)],
      *([after] if n_after else []))
    return res[:n], res[n:2 * n], res[2 * n:3 * n], res[3 * n:4 * n], res[4 * n]


def _xwait(src, land, send_sem, recv_sem, after, gather, name):
    def body(src_ref, land_ref, send_ref, recv_ref, after_ref, src_dead, land_out):
        del after_ref, src_dead, land_out
        for p in range(1, N_DEV):
            dev, peer = _peer_of(p)
            cp = pltpu.make_async_remote_copy(
                src_ref=src_ref if gather else src_ref.at[peer], dst_ref=land_ref.at[peer],
                send_sem=send_ref.at[p - 1], recv_sem=recv_ref.at[p - 1], device_id=dev, device_id_type=MESH)
            cp.wait_send()
            cp.wait_recv()

    src_done, landed = pl.pallas_call(
        body, name=name, out_shape=(pltpu.HBM(src.shape, src.dtype), pltpu.HBM(land.shape, land.dtype)),
        in_specs=[_HBM, _HBM, _SEM, _SEM, pl.BlockSpec(memory_space=pl.ANY)], out_specs=(_HBM, _HBM),
        input_output_aliases={0: 0, 1: 1},
        compiler_params=pltpu.CompilerParams(has_side_effects=_EFFECT),
    )(src, land, send_sem, recv_sem, after)
    me = 4 * lax.axis_index("x") + 2 * lax.axis_index("y") + lax.axis_index("c")
    own = src_done if gather else lax.dynamic_index_in_dim(src_done, me, 0, keepdims=False)
    return lax.dynamic_update_slice(landed, own[None], (me,) + (0,) * own.ndim)


def _mm(a, b, *, mode, tm, tn, tk, outs, epilogue=None, extras=(), nb=None, tok=None, name):
    if mode == "nn":
        (M, K), (_, N) = a.shape, b.shape
    elif mode == "nt":
        (M, K), (N, _) = a.shape, b.shape
    else:
        (K, M), (_, N) = a.shape, b.shape
    tm, tn, tk = min(tm, M), min(tn, N), min(tk, K)
    assert M % tm == 0 and N % tn == 0 and K % tk == 0, (name, M, N, K, tm, tn, tk)
    if mode == "nn":
        a_spec = pl.BlockSpec((tm, tk), lambda i, j, k: (i, k))
        b_spec = pl.BlockSpec((tk, tn), lambda i, j, k: (k, j))
        dims = (((1,), (0,)), ((), ()))
    elif mode == "nt":
        a_spec = pl.BlockSpec((tm, tk), lambda i, j, k: (i, k))
        b_spec = pl.BlockSpec((tn, tk), lambda i, j, k: (j, k))
        dims = (((1,), (1,)), ((), ()))
    else:
        a_spec = pl.BlockSpec((tk, tm), lambda i, j, k: (k, i))
        b_spec = pl.BlockSpec((tk, tn), lambda i, j, k: (k, j))
        dims = (((0,), (0,)), ((), ()))
    nk = K // tk
    n_ex, n_out = len(extras), len(outs)
    n_tok = 0 if tok is None else 1
    nbytes = lambda d: jnp.dtype(d).itemsize
    vmem_est = (2 * (tm * tk * nbytes(a.dtype) + tk * tn * nbytes(b.dtype)
                     + sum(tm * tn * nbytes(e.dtype) for e, kind in extras if kind == "tile")
                     + sum(tm * tn * nbytes(d) for d in outs)) + tm * tn * 4)
    assert vmem_est <= VMEM_LIMIT, (name, vmem_est)
    if epilogue is None:
        epilogue = lambda acc, ex: tuple(acc.astype(d) for d in outs)

    def body(a_ref, b_ref, *refs):
        refs = refs[n_tok:]
        ex_refs, out_refs = refs[:n_ex], refs[n_ex:n_ex + n_out]

        def finish(acc):
            res = epilogue(acc, [r[...] for r in ex_refs])
            for o_ref, v in zip(out_refs, res):
                if nb is None:
                    o_ref[...] = v.astype(o_ref.dtype)
                else:
                    for q in range(tn // nb):
                        o_ref[q] = v[:, q * nb:(q + 1) * nb].astype(o_ref.dtype)

        part = lax.dot_general(a_ref[...], b_ref[...], dims, preferred_element_type=F32)
        if nk == 1:
            finish(part)
        else:
            acc_ref = refs[n_ex + n_out]
            k = pl.program_id(2)

            @pl.when(k == 0)
            def _():
                acc_ref[...] = part

            @pl.when(k > 0)
            def _():
                acc_ref[...] += part

            @pl.when(k == nk - 1)
            def _():
                finish(acc_ref[...])

    ex_specs = [pl.BlockSpec((tm, tn), lambda i, j, k: (i, j)) if kind == "tile"
                else pl.BlockSpec((1, tn), lambda i, j, k: (0, j)) for _, kind in extras]
    if nb is not None:
        assert tn % nb == 0, (name, tn, nb)
        o_spec = pl.BlockSpec((tn // nb, tm, nb), lambda i, j, k: (j, i, 0))
        o_shape = (N // nb, M, nb)
    else:
        o_spec = pl.BlockSpec((tm, tn), lambda i, j, k: (i, j))
        o_shape = (M, N)
    res = pl.pallas_call(
        body, name=name, grid=(M // tm, N // tn, nk),
        in_specs=[a_spec, b_spec] + [pl.BlockSpec((SUBLANES, LANES), lambda i, j, k: (0, 0))] * n_tok + ex_specs,
        out_specs=tuple([o_spec] * n_out),
        out_shape=tuple(jax.ShapeDtypeStruct(o_shape, d) for d in outs),
        scratch_shapes=[pltpu.VMEM((tm, tn), F32)] if nk > 1 else [],
        compiler_params=_cparams(3, big=True),
    )(a, b, *([tok] if n_tok else []), *[e for e, _ in extras])
    return res[0] if n_out == 1 else res


def _mm_rows(a, b, *, mode, tm, seq, ins, outs, epilogue, tok=None, name):
    M, K = a.shape
    N = b.shape[1] if mode == "nn" else b.shape[0]
    tm = min(tm, M)
    assert M % tm == 0 and seq % tm == 0, (name, M, seq, tm)
    tpb = seq // tm
    n_b = M // seq
    dims = (((1,), (0,)), ((), ())) if mode == "nn" else (((1,), (1,)), ((), ()))
    n_tok = 0 if tok is None else 1
    n_in, n_out = len(ins), len(outs)

    in_specs, in_arrs = [], []
    for spec in ins:
        kind, arr = spec[0], spec[1]
        in_arrs.append(arr)
        if kind == "tile":
            in_specs.append(pl.BlockSpec((tm, arr.shape[1]), lambda i: (i, 0)))
        elif kind == "tilecol":
            in_specs.append(pl.BlockSpec((tm, spec[2]), lambda i, cb=spec[3]: (i, cb)))
        elif kind == "row":
            in_specs.append(pl.BlockSpec(arr.shape, lambda i: (0, 0)))
        else:
            in_specs.append(pl.BlockSpec((None, 1, arr.shape[2]), lambda i: (i // tpb, 0, 0)))
    out_specs, out_shapes = [], []
    for spec in outs:
        kind = spec[0]
        if kind == "tile":
            out_specs.append(pl.BlockSpec((tm, spec[2]), lambda i: (i, 0)))
            out_shapes.append(jax.ShapeDtypeStruct((M, spec[2]), spec[1]))
        elif kind == "tilecol":
            out_specs.append(pl.BlockSpec((tm, spec[2]), lambda i, cb=spec[3]: (i, cb)))
            out_shapes.append(jax.ShapeDtypeStruct((M, spec[4]), spec[1]))
        elif kind == "acc_row":
            out_specs.append(pl.BlockSpec((1, spec[1]), lambda i: (0, 0)))
            out_shapes.append(jax.ShapeDtypeStruct((1, spec[1]), F32))
        elif kind == "acc_brow":
            out_specs.append(pl.BlockSpec((None, 1, spec[1]), lambda i: (i // tpb, 0, 0)))
            out_shapes.append(jax.ShapeDtypeStruct((n_b, 1, spec[1]), F32))
        else:
            out_specs.append(pl.BlockSpec((SUBLANES, LANES), lambda i: (0, 0)))
            out_shapes.append(jax.ShapeDtypeStruct((SUBLANES, LANES), F32))

    def body(a_ref, b_ref, *refs):
        refs = refs[n_tok:]
        in_refs, out_refs = refs[:n_in], refs[n_in:n_in + n_out]
        i = pl.program_id(0)
        prod = lax.dot_general(a_ref[...], b_ref[...], dims, preferred_element_type=F32)
        vals = epilogue(prod, [r[...] for r in in_refs])
        for spec, o_ref, v in zip(outs, out_refs, vals):
            kind = spec[0]
            if kind in ("tile", "tilecol"):
                o_ref[...] = v.astype(o_ref.dtype)
            else:
                first = (i % tpb == 0) if kind == "acc_brow" else (i == 0)

                @pl.when(first)
                def _(o_ref=o_ref, v=v):
                    o_ref[...] = jnp.broadcast_to(v, o_ref.shape)

                @pl.when(jnp.logical_not(first))
                def _(o_ref=o_ref, v=v):
                    o_ref[...] += v

    res = pl.pallas_call(
        body, name=name, grid=(M // tm,),
        in_specs=[pl.BlockSpec((tm, K), lambda i: (i, 0)),
                  pl.BlockSpec(b.shape, lambda i: (0, 0), pipeline_mode=pl.Buffered(1))]
                 + [pl.BlockSpec((SUBLANES, LANES), lambda i: (0, 0))] * n_tok + in_specs,
        out_specs=tuple(out_specs), out_shape=tuple(out_shapes),
        compiler_params=_cparams(1, big=True),
    )(a, b, *([tok] if n_tok else []), *in_arrs)
    return res


def _tok_spec(ts, width, col_block=0):
    return pl.BlockSpec((None, ts, width), lambda b, s: (b, s, col_block))


def _brow_spec(width):
    return pl.BlockSpec((None, 1, width), lambda b, s: (b, 0, 0))


def _vec_spec(width):
    return pl.BlockSpec((1, width), lambda b, s: (0, 0))


def _modulate(x, sc, sh, ts):
    Bl, S, D = x.shape

    def body(x_ref, sc_ref, sh_ref, o_ref):
        o_ref[...] = (x_ref[...] * (1.0 + sc_ref[...]) + sh_ref[...]).astype(BF16)

    return pl.pallas_call(
        body, name="modulate", grid=(Bl, S // ts),
        in_specs=[_tok_spec(ts, D), _brow_spec(D), _brow_spec(D)],
        out_specs=_tok_spec(ts, D), out_shape=jax.ShapeDtypeStruct((Bl, S, D), BF16),
        compiler_params=_cparams(2),
    )(x, sc, sh)


def _mix_fwd(proj, w_conv, b_conv, w_rg_a, b_rg_a, w_rg_x, b_rg_x, lam, w_sp, b_sp_t, ln_v_g, ln_v_b, *, tm, lw, sw):
    Bl, S, _ = proj.shape
    heads, hd = w_rg_a.shape[0], w_rg_a.shape[1]
    groups = w_sp.shape[0]
    cw = 2 * lw + 2 * sw
    nblk = tm // SGU_BLOCK

    G = tm // SUBLANES
    nc = lw // LANES

    def body(p_ref, wc_ref, bc_ref, wa_ref, ba_ref, wx_ref, bx_ref, lam_ref, wsp_ref, bsp_ref, lg_ref, lb_ref,
             hs_ref, ya_ref, ys_ref, xc_ref, r_ref, ig_ref, a_ref, m_ref,
             xext, hnat, hcar, h7_scr, a7_scr, hp_scr):
        s = pl.program_id(1)

        @pl.when(s == 0)
        def _():
            xext[:, 0:SUBLANES, :] = jnp.zeros((nc, SUBLANES, LANES), F32)
            hcar[...] = jnp.zeros_like(hcar)

        @pl.when(s > 0)
        def _():
            xext[:, 0:SUBLANES, :] = xext[:, tm:tm + SUBLANES, :]

        for c in range(nc):
            xext[c, SUBLANES:SUBLANES + tm, :] = p_ref[:, c * LANES:(c + 1) * LANES].astype(F32)
        gl = p_ref[:, lw:2 * lw].astype(F32)

        def slab(ref3, start):
            return jnp.concatenate([ref3[c, pl.ds(start, G, stride=SUBLANES), :] for c in range(nc)], axis=1)

        xs = {st: slab(xext, st) for st in range(SUBLANES - 3, 2 * SUBLANES)}
        xc_slabs = []
        for j in range(SUBLANES):
            acc = bc_ref[...] + xs[SUBLANES + j] * wc_ref[3:4, :]
            for k in (1, 2, 3):
                acc = acc + xs[SUBLANES + j - k] * wc_ref[3 - k:4 - k, :]
            xc_slabs.append(acc)
        xc = jnp.concatenate(xc_slabs, axis=0)

        xcb = xc.astype(BF16)
        pa = jnp.concatenate([jnp.dot(xcb[:, h * hd:(h + 1) * hd], wa_ref[h], preferred_element_type=F32)
                              for h in range(heads)], axis=1) + ba_ref[...]
        px = jnp.concatenate([jnp.dot(xcb[:, h * hd:(h + 1) * hd], wx_ref[h], preferred_element_type=F32)
                              for h in range(heads)], axis=1) + bx_ref[...]
        r = _sigmoid(pa)
        ig = _sigmoid(px)
        nl = -lam_ref[...]
        big_l = -LRU_C * (jnp.maximum(nl, 0.0) + _log1p_pos(jnp.exp(-jnp.abs(nl))))
        la = big_l * r
        a = jnp.exp(la)
        m = jnp.sqrt(-_expm1(2.0 * la))
        bin_ = m * (ig * xc)
        xc_ref[...] = xc
        r_ref[...] = r
        ig_ref[...] = ig
        a_ref[...] = a
        m_ref[...] = m

        h0 = [bin_[0:G]]
        cp = [a[0:G]]
        for j in range(1, SUBLANES):
            aj = a[j * G:(j + 1) * G]
            h0.append(aj * h0[j - 1] + bin_[j * G:(j + 1) * G])
            cp.append(aj * cp[j - 1])
        h7_scr[...] = h0[SUBLANES - 1]
        a7_scr[...] = cp[SUBLANES - 1]
        carry = hcar[0:1, :]
        for g in range(G):
            hp_scr[g:g + 1, :] = carry
            carry = h7_scr[g:g + 1, :] + a7_scr[g:g + 1, :] * carry
        hcar[0:1, :] = carry
        hprev = hp_scr[...]
        for j in range(SUBLANES):
            hj = h0[j] + cp[j] * hprev
            for c in range(nc):
                hnat[c, pl.ds(j, G, stride=SUBLANES), :] = hj[:, c * LANES:(c + 1) * LANES]
        hs = jnp.concatenate([hnat[c] for c in range(nc)], axis=1)
        hs_ref[...] = hs
        ya_ref[...] = (hs * _gelu(gl)).astype(BF16)

        gu = _gelu(p_ref[:, 2 * lw:2 * lw + sw].astype(F32))
        gv = _gelu(p_ref[:, 2 * lw + sw:cw].astype(F32))
        xhat, _ = _ln_stats(gv)
        vn = (xhat * lg_ref[...] + lb_ref[...]).astype(BF16)
        tpos = lax.broadcasted_iota(jnp.int32, (SGU_BLOCK, SGU_BLOCK), 0) // CHUNK
        spos = lax.broadcasted_iota(jnp.int32, (SGU_BLOCK, SGU_BLOCK), 1) // CHUNK
        gw = sw // groups
        rows_out = []
        for blk in range(nblk):
            r0 = blk * SGU_BLOCK
            cols = []
            for g in range(groups):
                wm = jnp.where(spos <= tpos, wsp_ref[g], 0.0).astype(BF16)
                mixed = jnp.dot(wm, vn[r0:r0 + SGU_BLOCK, g * gw:(g + 1) * gw], preferred_element_type=F32)
                cols.append(mixed + bsp_ref[:, g:g + 1])
            rows_out.append(jnp.concatenate(cols, axis=1))
        mixed_all = jnp.concatenate(rows_out, axis=0) if nblk > 1 else rows_out[0]
        ys_ref[...] = (gu * mixed_all).astype(BF16)

    full = lambda shp: pl.BlockSpec(shp, lambda b, s: (0,) * len(shp))
    return pl.pallas_call(
        body, name="mix_fwd", grid=(Bl, S // tm),
        in_specs=[_tok_spec(tm, cw), full(w_conv.shape), full(b_conv.shape), full(w_rg_a.shape), full(b_rg_a.shape),
                  full(w_rg_x.shape), full(b_rg_x.shape), full(lam.shape), full(w_sp.shape), full(b_sp_t.shape),
                  full(ln_v_g.shape), full(ln_v_b.shape)],
        out_specs=(_tok_spec(tm, lw), _tok_spec(tm, lw), _tok_spec(tm, sw)) + (_tok_spec(tm, lw),) * 5,
        out_shape=(jax.ShapeDtypeStruct((Bl, S, lw), F32), jax.ShapeDtypeStruct((Bl, S, lw), BF16),
                   jax.ShapeDtypeStruct((Bl, S, sw), BF16)) + (jax.ShapeDtypeStruct((Bl, S, lw), F32),) * 5,
        scratch_shapes=[pltpu.VMEM((nc, tm + SUBLANES, LANES), F32), pltpu.VMEM((nc, tm, LANES), F32),
                        pltpu.VMEM((SUBLANES, lw), F32), pltpu.VMEM((G, lw), F32), pltpu.VMEM((G, lw), F32),
                        pltpu.VMEM((G, lw), F32)],
        compiler_params=_cparams(2, big=True),
    )(proj, w_conv, b_conv, w_rg_a, b_rg_a, w_rg_x, b_rg_x, lam, w_sp, b_sp_t, ln_v_g, ln_v_b)


def _merge_fwd(proj, y_a, y_b, *, ts, d):
    Bl, S, din = proj.shape
    gcol = (din - 2 * d) // (2 * d)
    assert gcol * 2 * d == din - 2 * d

    def body(g_ref, ya_ref, yb_ref, o_ref):
        sa = _sigmoid(g_ref[:, 0:d].astype(F32))
        sb = _sigmoid(g_ref[:, d:2 * d].astype(F32))
        o_ref[...] = (sa * ya_ref[...].astype(F32) + sb * yb_ref[...].astype(F32)).astype(BF16)

    return pl.pallas_call(
        body, name="merge_fwd", grid=(Bl, S // ts),
        in_specs=[_tok_spec(ts, 2 * d, gcol), _tok_spec(ts, d), _tok_spec(ts, d)],
        out_specs=_tok_spec(ts, d), out_shape=jax.ShapeDtypeStruct((Bl, S, d), BF16),
        compiler_params=_cparams(2),
    )(proj, y_a, y_b)


def _ln1_fwd(x, mix, gt1, g1, b1, sc2, sh2, *, ts):
    Bl, S, D = x.shape

    def body(x_ref, mix_ref, gt_ref, g_ref, b_ref, sc_ref, sh_ref, x1_ref, h2_ref):
        z = ALPHA * x_ref[...] + (1.0 + gt_ref[...]) * mix_ref[...].astype(F32)
        xhat, _ = _ln_stats(z)
        x1 = xhat * g_ref[...] + b_ref[...]
        x1_ref[...] = x1
        h2_ref[...] = (x1 * (1.0 + sc_ref[...]) + sh_ref[...]).astype(BF16)

    return pl.pallas_call(
        body, name="ln1_fwd", grid=(Bl, S // ts),
        in_specs=[_tok_spec(ts, D), _tok_spec(ts, D), _brow_spec(D), _vec_spec(D), _vec_spec(D), _brow_spec(D),
                  _brow_spec(D)],
        out_specs=(_tok_spec(ts, D), _tok_spec(ts, D)),
        out_shape=(jax.ShapeDtypeStruct((Bl, S, D), F32), jax.ShapeDtypeStruct((Bl, S, D), BF16)),
        compiler_params=_cparams(2),
    )(x, mix, gt1, g1, b1, sc2, sh2)


def _ln2_loss(x1, f, tgt, gt2, g2, b2, *, ts):
    Bl, S, D = x1.shape

    def body(x1_ref, f_ref, t_ref, gt_ref, g_ref, b_ref, df_ref, dx1_ref, dgt_ref, dg_ref, db_ref, loss_ref):
        s = pl.program_id(1)

        @pl.when(_first_step())
        def _():
            dg_ref[...] = jnp.zeros_like(dg_ref)
            db_ref[...] = jnp.zeros_like(db_ref)
            loss_ref[...] = jnp.zeros_like(loss_ref)

        @pl.when(s == 0)
        def _():
            dgt_ref[...] = jnp.zeros_like(dgt_ref)

        fv = f_ref[...]
        z = ALPHA * x1_ref[...] + (1.0 + gt_ref[...]) * fv
        xhat, rstd = _ln_stats(z)
        x2 = xhat * g_ref[...] + b_ref[...]
        err = x2 - t_ref[...]
        loss_ref[...] += 0.5 * jnp.sum(jnp.mean(err * err, axis=-1, keepdims=True))
        dy = err * (1.0 / D)
        dg_ref[...] += _colsum(dy * xhat)
        db_ref[...] += _colsum(dy)
        dz = _ln_bwd(dy, xhat, rstd, g_ref[...])
        dx1_ref[...] = ALPHA * dz
        dgt_ref[...] += _colsum(dz * fv)
        df_ref[...] = (dz * (1.0 + gt_ref[...])).astype(BF16)

    return pl.pallas_call(
        body, name="ln2_loss", grid=(Bl, S // ts),
        in_specs=[_tok_spec(ts, D), _tok_spec(ts, D), _tok_spec(ts, D), _brow_spec(D), _vec_spec(D), _vec_spec(D)],
        out_specs=(_tok_spec(ts, D), _tok_spec(ts, D), _brow_spec(D), _vec_spec(D), _vec_spec(D),
                   pl.BlockSpec((SUBLANES, LANES), lambda b, s: (0, 0))),
        out_shape=(jax.ShapeDtypeStruct((Bl, S, D), BF16), jax.ShapeDtypeStruct((Bl, S, D), F32),
                   jax.ShapeDtypeStruct((Bl, 1, D), F32), jax.ShapeDtypeStruct((1, D), F32),
                   jax.ShapeDtypeStruct((1, D), F32), jax.ShapeDtypeStruct((SUBLANES, LANES), F32)),
        compiler_params=_cparams(2),
    )(x1, f, tgt, gt2, g2, b2)


def _ln1_bwd(dx1p, dh2, x1, x, mix, sc2, gt1, g1, *, ts):
    Bl, S, D = x.shape

    def body(dx1p_ref, dh2_ref, x1_ref, x_ref, mix_ref, sc_ref, gt_ref, g_ref,
             dxp_ref, dmix_ref, dsc_ref, dsh_ref, dgt_ref, dg_ref, db_ref):
        s = pl.program_id(1)

        @pl.when(_first_step())
        def _():
            dg_ref[...] = jnp.zeros_like(dg_ref)
            db_ref[...] = jnp.zeros_like(db_ref)

        @pl.when(s == 0)
        def _():
            dsc_ref[...] = jnp.zeros_like(dsc_ref)
            dsh_ref[...] = jnp.zeros_like(dsh_ref)
            dgt_ref[...] = jnp.zeros_like(dgt_ref)

        dh2 = dh2_ref[...].astype(F32)
        mixv = mix_ref[...].astype(F32)
        dsc_ref[...] += _colsum(dh2 * x1_ref[...])
        dsh_ref[...] += _colsum(dh2)
        dx1 = dx1p_ref[...] + dh2 * (1.0 + sc_ref[...])
        z = ALPHA * x_ref[...] + (1.0 + gt_ref[...]) * mixv
        xhat, rstd = _ln_stats(z)
        dg_ref[...] += _colsum(dx1 * xhat)
        db_ref[...] += _colsum(dx1)
        dz = _ln_bwd(dx1, xhat, rstd, g_ref[...])
        dxp_ref[...] = ALPHA * dz
        dgt_ref[...] += _colsum(dz * mixv)
        dmix_ref[...] = (dz * (1.0 + gt_ref[...])).astype(BF16)

    return pl.pallas_call(
        body, name="ln1_bwd", grid=(Bl, S // ts),
        in_specs=[_tok_spec(ts, D)] * 5 + [_brow_spec(D), _brow_spec(D), _vec_spec(D)],
        out_specs=(_tok_spec(ts, D), _tok_spec(ts, D), _brow_spec(D), _brow_spec(D), _brow_spec(D), _vec_spec(D),
                   _vec_spec(D)),
        out_shape=(jax.ShapeDtypeStruct((Bl, S, D), F32), jax.ShapeDtypeStruct((Bl, S, D), BF16),
                   jax.ShapeDtypeStruct((Bl, 1, D), F32), jax.ShapeDtypeStruct((Bl, 1, D), F32),
                   jax.ShapeDtypeStruct((Bl, 1, D), F32), jax.ShapeDtypeStruct((1, D), F32),
                   jax.ShapeDtypeStruct((1, D), F32)),
        compiler_params=_cparams(2),
    )(dx1p, dh2, x1, x, mix, sc2, gt1, g1)


def _merge_bwd(dmerged, y_a, y_b, proj, *, ts, d):
    Bl, S, din = proj.shape
    gcol = (din - 2 * d) // (2 * d)

    def body(dm_ref, ya_ref, yb_ref, g_ref, dya_ref, dyb_ref, dp_ref, db_ref):
        @pl.when(_first_step())
        def _():
            db_ref[...] = jnp.zeros_like(db_ref)

        dm = dm_ref[...].astype(F32)
        sa = _sigmoid(g_ref[:, 0:d].astype(F32))
        sb = _sigmoid(g_ref[:, d:2 * d].astype(F32))
        dya_ref[...] = (dm * sa).astype(BF16)
        dyb_ref[...] = (dm * sb).astype(BF16)
        dga = dm * ya_ref[...].astype(F32) * sa * (1.0 - sa)
        dgb = dm * yb_ref[...].astype(F32) * sb * (1.0 - sb)
        dp_ref[:, 0:d] = dga.astype(BF16)
        dp_ref[:, d:2 * d] = dgb.astype(BF16)
        db_ref[:, 0:d] += _colsum(dga)
        db_ref[:, d:2 * d] += _colsum(dgb)

    return pl.pallas_call(
        body, name="merge_bwd", grid=(Bl, S // ts),
        in_specs=[_tok_spec(ts, d), _tok_spec(ts, d), _tok_spec(ts, d), _tok_spec(ts, 2 * d, gcol)],
        out_specs=(_tok_spec(ts, d), _tok_spec(ts, d), _tok_spec(ts, 2 * d, gcol), _vec_spec(2 * d)),
        out_shape=(jax.ShapeDtypeStruct((Bl, S, d), BF16), jax.ShapeDtypeStruct((Bl, S, d), BF16),
                   jax.ShapeDtypeStruct((Bl, S, din), BF16), jax.ShapeDtypeStruct((1, 2 * d), F32)),
        compiler_params=_cparams(2),
    )(dmerged, y_a, y_b, proj)


def _mix_bwd(proj, hs, dya, dys, dproj, saved, w_conv, b_conv, w_rg_a, b_rg_a, w_rg_x, b_rg_x, lam, w_sp, b_sp_t,
             ln_v_g, ln_v_b, *, tm, lw, sw):
    Bl, S, din = proj.shape
    heads, hd = w_rg_a.shape[0], w_rg_a.shape[1]
    groups = w_sp.shape[0]
    gw = sw // groups
    cw = 2 * lw + 2 * sw
    nblk = tm // SGU_BLOCK
    n_s = S // tm
    per8 = tm // SUBLANES
    halo_rows = 2 * SUBLANES

    G = tm // SUBLANES
    nc = lw // LANES

    def body(p_ref, xh_ref, hs_ref, hh_ref, dya_ref, dys_ref, dpin_ref, xc_ref, r_ref, ig_ref, a_ref, m_ref,
             wc_ref, bc_ref, wa_ref, ba_ref, wx_ref, bx_ref, lam_ref, wsp_ref, bsp_ref, lg_ref, lb_ref,
             dp_ref, dbin_ref, dwc_ref, dbc_ref, dwa_ref, dba_ref, dwx_ref, dbx_ref, dlam_ref, dwsp_ref, dbsp_ref,
             dlg_ref, dlb_ref,
             xext, hext, dnat, dxext, dhcar, g00_scr, p0_scr, a0_scr, cin_scr):
        del dpin_ref
        sr = pl.program_id(1)
        first_tile = sr == n_s - 1

        @pl.when(_first_step())
        def _():
            for ref in (dbin_ref, dwc_ref, dbc_ref, dwa_ref, dba_ref, dwx_ref, dbx_ref, dlam_ref, dwsp_ref, dbsp_ref,
                        dlg_ref, dlb_ref):
                ref[...] = jnp.zeros_like(ref)

        @pl.when(sr == 0)
        def _():
            dhcar[...] = jnp.zeros_like(dhcar)
            dxext[:, tm:tm + SUBLANES, :] = jnp.zeros((nc, SUBLANES, LANES), F32)

        @pl.when(sr > 0)
        def _():
            dxext[:, tm:tm + SUBLANES, :] = dxext[:, 0:SUBLANES, :]

        def slab(ref3, start):
            return jnp.concatenate([ref3[c, pl.ds(start, G, stride=SUBLANES), :] for c in range(nc)], axis=1)

        def put_slab(ref3, j, val):
            for c in range(nc):
                ref3[c, pl.ds(j, G, stride=SUBLANES), :] = val[:, c * LANES:(c + 1) * LANES]

        keep = jnp.where(first_tile, 0.0, 1.0)
        xprev = xh_ref[...].astype(F32)[halo_rows - SUBLANES:halo_rows] * keep
        hsv = hs_ref[...]
        hprev8 = hh_ref[...] * keep
        for c in range(nc):
            cs = slice(c * LANES, (c + 1) * LANES)
            xext[c, 0:SUBLANES, :] = xprev[:, cs]
            xext[c, SUBLANES:SUBLANES + tm, :] = p_ref[:, cs].astype(F32)
            hext[c, 0:SUBLANES, :] = hprev8[:, cs]
            hext[c, SUBLANES:SUBLANES + tm, :] = hsv[:, cs]
        gl = p_ref[:, lw:2 * lw].astype(F32)
        ggl, dggl = _gelu_and_grad(gl)
        dyav = dya_ref[...]
        dhs = dyav * ggl
        dgl = dyav * hsv * dggl
        dp_ref[:, lw:2 * lw] = dgl.astype(BF16)
        dbin_ref[:, lw:2 * lw] += _colsum(dgl)
        for c in range(nc):
            dnat[c] = dhs[:, c * LANES:(c + 1) * LANES]

        xc, r, ig, a, m = xc_ref[...], r_ref[...], ig_ref[...], a_ref[...], m_ref[...]
        xcb = xc.astype(BF16)
        nl = -lam_ref[...]
        big_l = -LRU_C * (jnp.maximum(nl, 0.0) + _log1p_pos(jnp.exp(-jnp.abs(nl))))

        g0 = [None] * SUBLANES
        pp = [None] * SUBLANES
        g0[SUBLANES - 1] = slab(dnat, SUBLANES - 1)
        for j in range(SUBLANES - 2, -1, -1):
            an = a[(j + 1) * G:(j + 2) * G]
            g0[j] = slab(dnat, j) + an * g0[j + 1]
            pp[j] = an if j == SUBLANES - 2 else an * pp[j + 1]
        g00_scr[...] = g0[0]
        p0_scr[...] = pp[0]
        a0_scr[...] = a[0:G]
        cin = dhcar[0:1, :]
        for g in range(G - 1, -1, -1):
            cin_scr[g:g + 1, :] = cin
            cin = a0_scr[g:g + 1, :] * (g00_scr[g:g + 1, :] + p0_scr[g:g + 1, :] * cin)
        dhcar[0:1, :] = cin
        cinv = cin_scr[...]
        dh = jnp.concatenate([g0[j] + pp[j] * cinv for j in range(SUBLANES - 1)] + [g0[SUBLANES - 1] + cinv], axis=0)

        hprev = jnp.concatenate([slab(hext, SUBLANES - 1 + j) for j in range(SUBLANES)], axis=0)
        da = dh * hprev
        ixc = ig * xc
        dm = dh * ixc
        dixc = dh * m
        di = dixc * xc
        dxc = dixc * ig
        dla = da * a - dm * (a * a) / m
        dlam_ref[...] += _colsum(dla * r) * (LRU_C * _sigmoid(nl))
        dr = dla * big_l
        dpa = dr * r * (1.0 - r)
        dpx = di * ig * (1.0 - ig)
        dba_ref[...] += _colsum(dpa)
        dbx_ref[...] += _colsum(dpx)
        dpab = dpa.astype(BF16)
        dpxb = dpx.astype(BF16)
        nt = (((1,), (1,)), ((), ()))
        tn = (((0,), (0,)), ((), ()))
        dxc_g = []
        for h in range(heads):
            sl = slice(h * hd, (h + 1) * hd)
            dxc_g.append(lax.dot_general(dpab[:, sl], wa_ref[h], nt, preferred_element_type=F32)
                         + lax.dot_general(dpxb[:, sl], wx_ref[h], nt, preferred_element_type=F32))
            dwa_ref[h] += lax.dot_general(xcb[:, sl], dpab[:, sl], tn, preferred_element_type=F32)
            dwx_ref[h] += lax.dot_general(xcb[:, sl], dpxb[:, sl], tn, preferred_element_type=F32)
        dxc = dxc + jnp.concatenate(dxc_g, axis=1)

        dbc_ref[...] += _colsum(dxc)
        xs = {st: slab(xext, st) for st in range(SUBLANES - 3, 2 * SUBLANES)}
        for k in range(4):
            xsh = jnp.concatenate([xs[SUBLANES + j - (3 - k)] for j in range(SUBLANES)], axis=0)
            dwc_ref[k:k + 1, :] += _colsum(dxc * xsh)
        for j in range(SUBLANES):
            put_slab(dxext, j, dxc[j * G:(j + 1) * G])
        us = {st: slab(dxext, st) for st in range(SUBLANES + 3)}
        for j in range(SUBLANES):
            acc = us[j] * wc_ref[3:4, :]
            for k in (1, 2, 3):
                acc = acc + us[j + k] * wc_ref[3 - k:4 - k, :]
            put_slab(dnat, j, acc)
        dxl = jnp.concatenate([dnat[c] for c in range(nc)], axis=1)
        dp_ref[:, 0:lw] = dxl.astype(BF16)
        dbin_ref[:, 0:lw] += _colsum(dxl)

        gu, dgu_dx = _gelu_and_grad(p_ref[:, 2 * lw:2 * lw + sw].astype(F32))
        gv, dgv_dx = _gelu_and_grad(p_ref[:, 2 * lw + sw:cw].astype(F32))
        xhat, rstd = _ln_stats(gv)
        vn = (xhat * lg_ref[...] + lb_ref[...]).astype(BF16)
        dys = dys_ref[...]
        dmixed = dys * gu
        dmb = dmixed.astype(BF16)
        tpos = lax.broadcasted_iota(jnp.int32, (SGU_BLOCK, SGU_BLOCK), 0) // CHUNK
        spos = lax.broadcasted_iota(jnp.int32, (SGU_BLOCK, SGU_BLOCK), 1) // CHUNK
        causal = spos <= tpos
        mixed_rows, dvn_rows = [], []
        for blk in range(nblk):
            rs = slice(blk * SGU_BLOCK, (blk + 1) * SGU_BLOCK)
            mcols, dcols = [], []
            for g in range(groups):
                cs = slice(g * gw, (g + 1) * gw)
                wm = jnp.where(causal, wsp_ref[g], 0.0).astype(BF16)
                mcols.append(jnp.dot(wm, vn[rs, cs], preferred_element_type=F32) + bsp_ref[:, g:g + 1])
                dcols.append(lax.dot_general(wm, dmb[rs, cs], tn, preferred_element_type=F32))
                dw = lax.dot_general(dmb[rs, cs], vn[rs, cs], nt, preferred_element_type=F32)
                dwsp_ref[g] += jnp.where(causal, dw, 0.0)
                dbsp_ref[:, g:g + 1] += jnp.sum(dmixed[rs, cs], axis=1, keepdims=True)
            mixed_rows.append(jnp.concatenate(mcols, axis=1))
            dvn_rows.append(jnp.concatenate(dcols, axis=1))
        mixed_all = jnp.concatenate(mixed_rows, axis=0) if nblk > 1 else mixed_rows[0]
        dvn = jnp.concatenate(dvn_rows, axis=0) if nblk > 1 else dvn_rows[0]
        du = dys * mixed_all * dgu_dx
        dlg_ref[...] += _colsum(dvn * xhat)
        dlb_ref[...] += _colsum(dvn)
        dv = _ln_bwd(dvn, xhat, rstd, lg_ref[...]) * dgv_dx
        dp_ref[:, 2 * lw:2 * lw + sw] = du.astype(BF16)
        dp_ref[:, 2 * lw + sw:cw] = dv.astype(BF16)
        dbin_ref[:, 2 * lw:2 * lw + sw] += _colsum(du)
        dbin_ref[:, 2 * lw + sw:cw] += _colsum(dv)

    rev = lambda s: n_s - 1 - s
    tile = lambda w: pl.BlockSpec((None, tm, w), lambda b, s: (b, rev(s), 0))
    halo = lambda w: pl.BlockSpec((None, SUBLANES, w), lambda b, s: (b, jnp.maximum(rev(s) * per8 - 1, 0), 0))
    xhalo = pl.BlockSpec((None, halo_rows, lw), lambda b, s: (b, jnp.maximum(rev(s) * (tm // halo_rows) - 1, 0), 0))
    full = lambda shp: pl.BlockSpec(shp, lambda b, s: (0,) * len(shp))
    small = [w_conv, b_conv, w_rg_a, b_rg_a, w_rg_x, b_rg_x, lam, w_sp, b_sp_t, ln_v_g, ln_v_b]
    acc_shapes = [(1, cw), w_conv.shape, b_conv.shape, w_rg_a.shape, b_rg_a.shape, w_rg_x.shape, b_rg_x.shape,
                  lam.shape, w_sp.shape, b_sp_t.shape, ln_v_g.shape, ln_v_b.shape]
    res = pl.pallas_call(
        body, name="mix_bwd", grid=(Bl, n_s),
        in_specs=[tile(cw), xhalo, tile(lw), halo(lw), tile(lw), tile(sw), pl.BlockSpec(memory_space=pl.ANY)]
                 + [tile(lw)] * 5 + [full(w.shape) for w in small],
        out_specs=tuple([tile(cw)] + [full(shp) for shp in acc_shapes]),
        out_shape=tuple([jax.ShapeDtypeStruct((Bl, S, din), BF16)] + [jax.ShapeDtypeStruct(shp, F32) for shp in acc_shapes]),
        input_output_aliases={6: 0},
        scratch_shapes=[pltpu.VMEM((nc, tm + SUBLANES, LANES), F32), pltpu.VMEM((nc, tm + SUBLANES, LANES), F32),
                        pltpu.VMEM((nc, tm, LANES), F32), pltpu.VMEM((nc, tm + SUBLANES, LANES), F32),
                        pltpu.VMEM((SUBLANES, lw), F32), pltpu.VMEM((G, lw), F32), pltpu.VMEM((G, lw), F32),
                        pltpu.VMEM((G, lw), F32), pltpu.VMEM((G, lw), F32)],
        compiler_params=_cparams(2, big=True),
    )(proj, proj, hs, hs, dya, dys, dproj, *saved, *small)
    return res


def _final_dx(dxp, dh, x, sc1, *, ts):
    Bl, S, D = x.shape

    def body(dxp_ref, dh_ref, x_ref, sc_ref, dx_ref, dsc_ref, dsh_ref):
        @pl.when(pl.program_id(1) == 0)
        def _():
            dsc_ref[...] = jnp.zeros_like(dsc_ref)
            dsh_ref[...] = jnp.zeros_like(dsh_ref)

        dh = dh_ref[...]
        dx_ref[...] = dxp_ref[...] + dh * (1.0 + sc_ref[...])
        dsc_ref[...] += _colsum(dh * x_ref[...])
        dsh_ref[...] += _colsum(dh)

    return pl.pallas_call(
        body, name="final_dx", grid=(Bl, S // ts),
        in_specs=[_tok_spec(ts, D), _tok_spec(ts, D), _tok_spec(ts, D), _brow_spec(D)],
        out_specs=(_tok_spec(ts, D), _brow_spec(D), _brow_spec(D)),
        out_shape=(jax.ShapeDtypeStruct((Bl, S, D), F32), jax.ShapeDtypeStruct((Bl, 1, D), F32),
                   jax.ShapeDtypeStruct((Bl, 1, D), F32)),
        compiler_params=_cparams(2),
    )(dxp, dh, x, sc1)


def _ada_fwd(c_all, w_ada):
    R, D = c_all.shape
    nb = w_ada.shape[1]

    def body(c_ref, w_ref, act_ref, o_ref):
        cv = c_ref[...]
        act = (cv * _sigmoid(cv)).astype(BF16)
        act_ref[...] = act
        o_ref[...] = jnp.dot(act, w_ref[...].astype(BF16), preferred_element_type=F32)

    return pl.pallas_call(
        body, name="ada_fwd",
        out_shape=(jax.ShapeDtypeStruct((R, D), BF16), jax.ShapeDtypeStruct((R, nb), F32)),
        compiler_params=pltpu.CompilerParams(vmem_limit_bytes=VMEM_LIMIT),
    )(c_all, w_ada)


def _ada_bwd(c_act, dmod_cols):
    R, D = c_act.shape
    nb = dmod_cols.shape[1]

    def body(act_ref, d_ref, o_ref):
        o_ref[...] = lax.dot_general(act_ref[...], d_ref[...].astype(BF16), (((0,), (0,)), ((), ())),
                                     preferred_element_type=F32)

    return pl.pallas_call(
        body, name="ada_bwd", out_shape=jax.ShapeDtypeStruct((D, nb), F32),
        compiler_params=pltpu.CompilerParams(vmem_limit_bytes=VMEM_LIMIT),
    )(c_act, dmod_cols)


def _adamw(w, g_slots, m, v, *, tr, name):
    R, C = w.shape
    n_slot = g_slots.shape[0]
    tr = min(tr, R)
    assert R % tr == 0, (name, R, tr)
    c1 = 1.0 / (1.0 - ADAM_B1 ** ADAM_STEP)
    c2 = 1.0 / (1.0 - ADAM_B2 ** ADAM_STEP)

    def body(w_ref, g_ref, m_ref, v_ref, go_ref, d_ref, mo_ref, vo_ref):
        g = g_ref[0].astype(F32)
        for i in range(1, n_slot):
            g = g + g_ref[i].astype(F32)
        mn = ADAM_B1 * m_ref[...] + (1.0 - ADAM_B1) * g
        vn = ADAM_B2 * v_ref[...] + (1.0 - ADAM_B2) * (g * g)
        go_ref[...] = g
        mo_ref[...] = mn
        vo_ref[...] = vn
        d_ref[...] = -ADAM_LR * ((mn * c1) / (jnp.sqrt(vn * c2) + ADAM_EPS) + ADAM_WD * w_ref[...])

    blk = pl.BlockSpec((tr, C), lambda i: (i, 0))
    return pl.pallas_call(
        body, name=name, grid=(R // tr,),
        in_specs=[blk, pl.BlockSpec((n_slot, tr, C), lambda i: (0, i, 0)), blk, blk],
        out_specs=(blk, blk, blk, blk),
        out_shape=tuple(jax.ShapeDtypeStruct((R, C), F32) for _ in range(4)),
        compiler_params=_cparams(1, big=True),
    )(w, g_slots, m, v)


def _sum_slots(g_slots, *, name):
    n_slot, R, C = g_slots.shape

    def body(g_ref, o_ref):
        g = g_ref[0]
        for i in range(1, n_slot):
            g = g + g_ref[i]
        o_ref[...] = g

    return pl.pallas_call(body, name=name, out_shape=jax.ShapeDtypeStruct((R, C), F32),
                          compiler_params=pltpu.CompilerParams(vmem_limit_bytes=VMEM_LIMIT))(g_slots)


SMALL_NAMES = ("b_ada", "b_in", "b_conv", "w_rg_a", "b_rg_a", "w_rg_x", "b_rg_x", "lru_lambda", "w_sp", "b_sp",
               "ln_v_g", "ln_v_b", "ln1_g", "ln1_b", "ln2_g", "ln2_b")
BIG_NAMES = ("w_ada", "w_in", "w_conv", "w_o_lru", "w_o_sgu", "w_out", "w_up", "w_down")
WEIGHT_ORDER = ("w_ada", "b_ada", "w_in", "b_in", "w_conv", "b_conv", "w_rg_a", "b_rg_a", "w_rg_x", "b_rg_x",
                "lru_lambda", "w_sp", "b_sp", "ln_v_g", "ln_v_b", "w_o_lru", "w_o_sgu", "w_out", "ln1_g", "ln1_b",
                "w_up", "w_down", "ln2_g", "ln2_b")


def _pack_small(d):
    flat = jnp.concatenate([d[n].reshape(-1) for n in SMALL_NAMES])
    rows = -(-flat.shape[0] // LANES)
    rows = -(-rows // (N_DEV * SUBLANES)) * (N_DEV * SUBLANES)
    flat = jnp.pad(flat, (0, rows * LANES - flat.shape[0]))
    return flat.reshape(rows, LANES)


def _unpack_small(packed, like):
    flat = packed.reshape(-1)
    out, off = {}, 0
    for n in SMALL_NAMES:
        sz = like[n].size
        out[n] = flat[off:off + sz].reshape(like[n].shape)
        off += sz
    return out


def _blocked_cols(w2d):
    K, N = w2d.shape
    return jnp.transpose(w2d.reshape(K, N_DEV, N // N_DEV), (1, 0, 2))


def _unblock_cols(wb):
    n, K, nb = wb.shape
    return jnp.transpose(wb, (1, 0, 2)).reshape(K, n * nb)


def kernel(x, c, w_ada, b_ada, w_in, b_in, w_conv, b_conv, w_rg_a, b_rg_a, w_rg_x, b_rg_x, lru_lambda, w_sp, b_sp, ln_v_g, ln_v_b, w_o_lru, w_o_sgu, w_out, ln1_g, ln1_b, w_up, w_down, ln2_g, ln2_b, loss_target, m_w_ada, m_b_ada, m_w_in, m_b_in, m_w_conv, m_b_conv, m_w_rg_a, m_b_rg_a, m_w_rg_x, m_b_rg_x, m_lru_lambda, m_w_sp, m_b_sp, m_ln_v_g, m_ln_v_b, m_w_o_lru, m_w_o_sgu, m_w_out, m_ln1_g, m_ln1_b, m_w_up, m_w_down, m_ln2_g, m_ln2_b, v_w_ada, v_b_ada, v_w_in, v_b_in, v_w_conv, v_b_conv, v_w_rg_a, v_b_rg_a, v_w_rg_x, v_b_rg_x, v_lru_lambda, v_w_sp, v_b_sp, v_ln_v_g, v_ln_v_b, v_w_o_lru, v_w_o_sgu, v_w_out, v_ln1_g, v_ln1_b, v_w_up, v_w_down, v_ln2_g, v_ln2_b):
    W = dict(w_ada=w_ada, b_ada=b_ada, w_in=w_in, b_in=b_in, w_conv=w_conv, b_conv=b_conv, w_rg_a=w_rg_a,
             b_rg_a=b_rg_a, w_rg_x=w_rg_x, b_rg_x=b_rg_x, lru_lambda=lru_lambda, w_sp=w_sp, b_sp=b_sp,
             ln_v_g=ln_v_g, ln_v_b=ln_v_b, w_o_lru=w_o_lru, w_o_sgu=w_o_sgu, w_out=w_out, ln1_g=ln1_g, ln1_b=ln1_b,
             w_up=w_up, w_down=w_down, ln2_g=ln2_g, ln2_b=ln2_b)
    Mo = dict(w_ada=m_w_ada, b_ada=m_b_ada, w_in=m_w_in, b_in=m_b_in, w_conv=m_w_conv, b_conv=m_b_conv,
              w_rg_a=m_w_rg_a, b_rg_a=m_b_rg_a, w_rg_x=m_w_rg_x, b_rg_x=m_b_rg_x, lru_lambda=m_lru_lambda,
              w_sp=m_w_sp, b_sp=m_b_sp, ln_v_g=m_ln_v_g, ln_v_b=m_ln_v_b, w_o_lru=m_w_o_lru, w_o_sgu=m_w_o_sgu,
              w_out=m_w_out, ln1_g=m_ln1_g, ln1_b=m_ln1_b, w_up=m_w_up, w_down=m_w_down, ln2_g=m_ln2_g,
              ln2_b=m_ln2_b)
    Vo = dict(w_ada=v_w_ada, b_ada=v_b_ada, w_in=v_w_in, b_in=v_b_in, w_conv=v_w_conv, b_conv=v_b_conv,
              w_rg_a=v_w_rg_a, b_rg_a=v_b_rg_a, w_rg_x=v_w_rg_x, b_rg_x=v_b_rg_x, lru_lambda=v_lru_lambda,
              w_sp=v_w_sp, b_sp=v_b_sp, ln_v_g=v_ln_v_g, ln_v_b=v_ln_v_b, w_o_lru=v_w_o_lru, w_o_sgu=v_w_o_sgu,
              w_out=v_w_out, ln1_g=v_ln1_g, ln1_b=v_ln1_b, w_up=v_w_up, w_down=v_w_down, ln2_g=v_ln2_g,
              ln2_b=v_ln2_b)

    Bl, S, D = x.shape
    T = Bl * S
    lw = b_conv.shape[-1]
    sw = ln_v_g.shape[-1]
    din = b_in.shape[-1]
    dff = w_up.shape[-1] * N_DEV
    ts = min(512, S)
    tmix = min(256, S)
    trow = min(512, S)

    c_pad = jnp.pad(c, ((0, SUBLANES - Bl), (0, 0)))
    c_g, wconv_g = _exchange([c_pad, w_conv[0]], True, "xchg_c")
    wconv_full = _unblock_cols(wconv_g)
    c_act, modcols = _ada_fwd(c_g.reshape(N_DEV * SUBLANES, D), w_ada[0])
    (mod_slots,) = _exchange([modcols.reshape(N_DEV, SUBLANES, -1)], False, "xchg_mod")

    wnames = ("win", "wol", "wos", "wout", "wup", "wdown")
    shards = [w_in[0].astype(BF16), w_o_lru[0].astype(BF16), w_o_sgu[0].astype(BF16), w_out[0].astype(BF16),
              w_up[0].astype(BF16), w_down[0].astype(BF16)]
    g_send, g_recv, g_src, g_land, g_tok = _xstart(shards, True, mod_slots, "gather_start")
    gidx = {n: i for i, n in enumerate(wnames)}

    def gathered(n, after):
        i = gidx[n]
        return _xwait(g_src[i], g_land[i], g_send[i], g_recv[i], after, True, "gather_wait_" + n)

    mod = _unblock_cols(mod_slots)[:Bl] + (b_ada + g_tok[0, 0])
    sh1, sc1, gt1, sh2, sc2, gt2 = [mod[:, i * D:(i + 1) * D].reshape(Bl, 1, D) for i in range(6)]

    wa_b, wx_b = w_rg_a[0].astype(BF16), w_rg_x[0].astype(BF16)
    b_sp_t = jnp.transpose(b_sp[0])
    small_mix = (wconv_full, b_conv, wa_b, b_rg_a, wx_b, b_rg_x, lru_lambda, w_sp[0], b_sp_t, ln_v_g, ln_v_b)

    h = _modulate(x, sc1, sh1, ts)
    Win = _unblock_cols(gathered("win", h))
    proj = _mm(h.reshape(T, D), Win, mode="nn", tm=2048, tn=din // 4, tk=D, outs=[BF16],
               extras=[(b_in, "row")], epilogue=lambda acc, ex: (acc + ex[0],), name="mm_proj")
    proj3 = proj.reshape(Bl, S, din)
    hs, ya_pre, ysgu, *lru_saved = _mix_fwd(proj3, *small_mix, tm=tmix, lw=lw, sw=sw)
    Wol = gathered("wol", ya_pre).reshape(lw, D)
    Wos = _unblock_cols(gathered("wos", ysgu))
    y_a = _mm(ya_pre.reshape(T, lw), Wol, mode="nn", tm=2048, tn=D, tk=lw, outs=[BF16], name="mm_ya")
    x2d, tgt2d = x.reshape(T, D), loss_target.reshape(T, D)
    gate_cb = (din - 2 * D) // D

    def ep_merge(y_b, v):
        ya, ga, gb = [t.astype(F32) for t in v]
        yb = y_b.astype(BF16).astype(F32)
        return [yb, _sigmoid(ga) * ya + _sigmoid(gb) * yb]

    y_b, merged = _mm_rows(ysgu.reshape(T, sw), Wos, mode="nn", tm=trow, seq=S,
                           ins=[("tile", y_a), ("tilecol", proj, D, gate_cb), ("tilecol", proj, D, gate_cb + 1)],
                           outs=[("tile", BF16, D), ("tile", BF16, D)], epilogue=ep_merge, name="mm_yb_merge")
    Wout = gathered("wout", merged).reshape(D, D)

    def ep_ln1(mix_acc, v):
        x_, gt, g, b, sc, sh = v
        mixr = mix_acc.astype(BF16).astype(F32)
        xhat, _ = _ln_stats(ALPHA * x_ + (1.0 + gt) * mixr)
        x1_ = xhat * g + b
        return [mixr, x1_, x1_ * (1.0 + sc) + sh]

    mix, x1, h2 = _mm_rows(merged, Wout, mode="nn", tm=trow, seq=S,
                           ins=[("tile", x2d), ("brow", gt1), ("row", ln1_g), ("row", ln1_b), ("brow", sc2),
                                ("brow", sh2)],
                           outs=[("tile", BF16, D), ("tile", F32, D), ("tile", BF16, D)], epilogue=ep_ln1,
                           name="mm_mix_ln1")
    Wup = _unblock_cols(gathered("wup", h2))
    act = _mm(h2, Wup, mode="nn", tm=2048, tn=1024, tk=D, outs=[BF16],
              epilogue=lambda acc, ex: (jnp.square(jnp.maximum(acc, 0.0)),), name="mm_up")
    Wdown = gathered("wdown", act).reshape(dff, D)

    def ep_ln2(f_acc, v):
        x1_, t_, gt, g, b = v
        xhat, rstd = _ln_stats(ALPHA * x1_ + (1.0 + gt) * f_acc)
        err = xhat * g + b - t_
        loss_t = 0.5 * jnp.sum(jnp.mean(err * err, axis=-1, keepdims=True))
        dy = err * (1.0 / D)
        dz = _ln_bwd(dy, xhat, rstd, g)
        return [dz * (1.0 + gt), ALPHA * dz, _colsum(dz * f_acc), _colsum(dy * xhat), _colsum(dy), loss_t]

    df2, dx1p, dgt2, dg2, db2, loss_part = _mm_rows(
        act, Wdown, mode="nn", tm=trow, seq=S,
        ins=[("tile", x1), ("tile", tgt2d), ("brow", gt2), ("row", ln2_g), ("row", ln2_b)],
        outs=[("tile", BF16, D), ("tile", F32, D), ("acc_brow", D), ("acc_row", D), ("acc_row", D), ("acc_scalar",)],
        epilogue=ep_ln2, name="mm_down_ln2")
    loss = lax.psum(loss_part[0, 0], ("x", "y", "c"))

    def send_grads(parts, name):
        snd, rcv, src, land, tok = _xstart(parts, False, None, name + "_start")
        return [(src[i], land[i], snd[i], rcv[i]) for i in range(len(parts))], tok

    dup = _mm(df2, Wdown, mode="nt", tm=2048, tn=1024, tk=D, outs=[BF16], extras=[(act, "tile")],
              epilogue=lambda acc, ex: (acc * (2.0 * jnp.sqrt(ex[0].astype(F32))),), name="mm_dup")
    g_wdown = _mm(act, df2, mode="tn", tm=1024, tn=D, tk=2048, outs=[BF16], name="mm_gwdown")
    (x_wdown,), tok = send_grads([g_wdown.reshape(N_DEV, dff // N_DEV, D)], "gx_wdown")
    def ep_ln1_bwd(dh2, v):
        dx1p_, x1_, x_, mix_, sc, gt, g = v
        mixv = mix_.astype(F32)
        dx1 = dx1p_ + dh2 * (1.0 + sc)
        xhat, rstd = _ln_stats(ALPHA * x_ + (1.0 + gt) * mixv)
        dz = _ln_bwd(dx1, xhat, rstd, g)
        return [ALPHA * dz, dz * (1.0 + gt), _colsum(dh2 * x1_), _colsum(dh2), _colsum(dz * mixv),
                _colsum(dx1 * xhat), _colsum(dx1)]

    dxp, dmix, dsc2, dsh2, dgt1, dg1, db1 = _mm_rows(
        dup, Wup, mode="nt", tm=trow, seq=S, tok=tok,
        ins=[("tile", dx1p), ("tile", x1), ("tile", x2d), ("tile", mix), ("brow", sc2), ("brow", gt1), ("row", ln1_g)],
        outs=[("tile", F32, D), ("tile", BF16, D), ("acc_brow", D), ("acc_brow", D), ("acc_brow", D), ("acc_row", D),
              ("acc_row", D)],
        epilogue=ep_ln1_bwd, name="mm_dh2_ln1b")
    g_wup = _mm(h2, dup, mode="tn", tm=D, tn=1024, tk=2048, outs=[BF16], nb=dff // N_DEV, name="mm_gwup")
    (x_wup,), tok = send_grads([g_wup], "gx_wup")

    def ep_merge_bwd(dm, v):
        ya, yb, ga, gb = [t.astype(F32) for t in v]
        sa, sb = _sigmoid(ga), _sigmoid(gb)
        dg = jnp.concatenate([dm * ya * sa * (1.0 - sa), dm * yb * sb * (1.0 - sb)], axis=1)
        return [dm * sa, dm * sb, dg, _colsum(dg)]

    dy_a, dy_b, dproj, dbin_hi = _mm_rows(
        dmix, Wout, mode="nt", tm=trow, seq=S, tok=tok,
        ins=[("tile", y_a), ("tile", y_b), ("tilecol", proj, D, gate_cb), ("tilecol", proj, D, gate_cb + 1)],
        outs=[("tile", BF16, D), ("tile", BF16, D), ("tilecol", BF16, 2 * D, gate_cb // 2, din), ("acc_row", 2 * D)],
        epilogue=ep_merge_bwd, name="mm_dmerged_mb")
    g_wout = _mm(merged, dmix, mode="tn", tm=D, tn=D, tk=2048, outs=[BF16], name="mm_gwout")
    (x_wout,), tok = send_grads([g_wout.reshape(N_DEV, D // N_DEV, D)], "gx_wout")
    dya_pre = _mm(dy_a, Wol, mode="nt", tm=2048, tn=lw, tk=D, outs=[F32], tok=tok, name="mm_dya")
    dysgu = _mm(dy_b, Wos, mode="nt", tm=2048, tn=sw, tk=D, outs=[F32], name="mm_dys")
    g_wol = _mm(ya_pre.reshape(T, lw), dy_a, mode="tn", tm=lw, tn=D, tk=2048, outs=[BF16], name="mm_gwol")
    g_wos = _mm(ysgu.reshape(T, sw), dy_b, mode="tn", tm=sw, tn=D, tk=2048, outs=[BF16], nb=D // N_DEV,
                name="mm_gwos")
    (x_wol, x_wos), tok = send_grads([g_wol.reshape(N_DEV, lw // N_DEV, D), g_wos], "gx_wo")
    small_mix_b = (wconv_full, b_conv + tok[0, 0]) + small_mix[2:]
    (dproj, dbin_lo, g_wconv, g_bconv, g_wa, g_ba, g_wx, g_bx, g_lam, g_wsp, g_bsp_t, g_lvg, g_lvb) = _mix_bwd(
        proj3, hs, dya_pre.reshape(Bl, S, lw), dysgu.reshape(Bl, S, sw), dproj.reshape(Bl, S, din), lru_saved,
        *small_mix_b, tm=tmix, lw=lw, sw=sw)
    dproj2 = dproj.reshape(T, din)
    g_win = _mm(h.reshape(T, D), dproj2, mode="tn", tm=D, tn=din // 4, tk=2048, outs=[BF16], nb=din // N_DEV,
                name="mm_gwin")
    (x_win,), tok = send_grads([g_win], "gx_win")

    def ep_final(dh, v):
        dxp_, x_, sc = v
        return [dxp_ + dh * (1.0 + sc), _colsum(dh * x_), _colsum(dh)]

    grad_x, dsc1, dsh1 = _mm_rows(dproj2, Win, mode="nt", tm=trow, seq=S, tok=tok,
                                  ins=[("tile", dxp), ("tile", x2d), ("brow", sc1)],
                                  outs=[("tile", F32, D), ("acc_brow", D), ("acc_brow", D)], epilogue=ep_final,
                                  name="mm_dh_final")
    grad_x = grad_x.reshape(Bl, S, D)

    dmod = jnp.concatenate([dsh1, dsc1, dgt1, dsh2, dsc2, dgt2], axis=-1).reshape(Bl, 6 * D)
    dmod_b = _blocked_cols(jnp.pad(dmod, ((0, SUBLANES - Bl), (0, 0))))
    g_small_local = dict(
        b_ada=jnp.sum(dmod, axis=0, keepdims=True), b_in=jnp.concatenate([dbin_lo, dbin_hi], axis=-1),
        b_conv=g_bconv, w_rg_a=g_wa[None], b_rg_a=g_ba, w_rg_x=g_wx[None], b_rg_x=g_bx, lru_lambda=g_lam,
        w_sp=g_wsp[None], b_sp=jnp.transpose(g_bsp_t)[None], ln_v_g=g_lvg, ln_v_b=g_lvb, ln1_g=dg1, ln1_b=db1,
        ln2_g=dg2, ln2_b=db2)
    gs_packed = _pack_small(g_small_local)
    rows = gs_packed.shape[0]
    parts = [dmod_b, _blocked_cols(g_wconv), gs_packed.reshape(N_DEV, rows // N_DEV, LANES)]
    dmod_s, gwconv_s, gsmall_s = _exchange(parts, False, "xchg_grads")
    gwdown_s = _xwait(*x_wdown, dmod_s, False, "gx_wdown_wait")
    gwup_s = _xwait(*x_wup, dmod_s, False, "gx_wup_wait")
    gwout_s = _xwait(*x_wout, dmod_s, False, "gx_wout_wait")
    gwol_s = _xwait(*x_wol, dmod_s, False, "gx_wol_wait")
    gwos_s = _xwait(*x_wos, dmod_s, False, "gx_wos_wait")
    gwin_s = _xwait(*x_win, dmod_s, False, "gx_win_wait")

    out_g, out_d, out_m, out_v = {}, {}, {}, {}

    def adam(name, g_slots, tr):
        shp = W[name].shape
        w2, m2, v2 = [t.reshape(g_slots.shape[1:]) for t in (W[name], Mo[name], Vo[name])]
        g, d, mn, vn = _adamw(w2, g_slots, m2, v2, tr=tr, name="adam_" + name)
        out_g[name], out_d[name], out_m[name], out_v[name] = [t.reshape(shp) for t in (g, d, mn, vn)]

    g_wada = _ada_bwd(c_act, dmod_s.reshape(N_DEV * SUBLANES, -1))
    adam("w_ada", g_wada[None], 256)
    adam("w_in", gwin_s, 256)
    adam("w_conv", gwconv_s, 8)
    adam("w_o_lru", gwol_s, 160)
    adam("w_o_sgu", gwos_s, 256)
    adam("w_out", gwout_s, 128)
    adam("w_up", gwup_s, 256)
    adam("w_down", gwdown_s, 256)

    g_chunk = _sum_slots(gsmall_s, name="sum_small")
    (gsmall_all,) = _exchange([g_chunk], True, "xchg_small")
    gs, ds, ms, vs = _adamw(_pack_small(W), gsmall_all.reshape(1, rows, LANES), _pack_small(Mo), _pack_small(Vo),
                            tr=rows // N_DEV, name="adam_small")
    for dst, packed in ((out_g, gs), (out_d, ds), (out_m, ms), (out_v, vs)):
        dst.update(_unpack_small(packed, W))

    return (loss, grad_x, *[out_g[n] for n in WEIGHT_ORDER], *[out_d[n] for n in WEIGHT_ORDER],
            *[out_m[n] for n in WEIGHT_ORDER], *[out_v[n] for n in WEIGHT_ORDER])
```

```python
import functools
import math

import jax
import jax.numpy as jnp
from jax import lax
from jax.experimental import pallas as pl
from jax.experimental.pallas import tpu as pltpu

N_DEV = 8
LN_EPS = 1e-5
LRU_C = 8.0
CHUNK = 64
SGU_BLOCK = 128
ALPHA = 2.0 ** 0.25
ADAM_LR = 0.001
ADAM_B1 = 0.9
ADAM_B2 = 0.999
ADAM_EPS = 1e-08
ADAM_WD = 0.01
ADAM_STEP = 10
GELU_K0 = math.sqrt(2.0 / math.pi)
GELU_K1 = 0.044715

SUBLANES = 8
LANES = 128
VMEM_LIMIT = 56 * 1024 * 1024

F32 = jnp.float32
BF16 = jnp.bfloat16
MESH = pl.DeviceIdType.MESH


def _cparams(n_axes, big=False):
    return pltpu.CompilerParams(dimension_semantics=("arbitrary",) * n_axes,
                                vmem_limit_bytes=VMEM_LIMIT if big else None)


def _sigmoid(x):
    return 0.5 * jnp.tanh(0.5 * x) + 0.5


def _gelu(x):
    t = jnp.tanh(GELU_K0 * (x + GELU_K1 * (x * x * x)))
    return 0.5 * x * (1.0 + t)


def _gelu_and_grad(x):
    x2 = x * x
    t = jnp.tanh(GELU_K0 * (x + GELU_K1 * (x2 * x)))
    g = 0.5 * x * (1.0 + t)
    dg = 0.5 * (1.0 + t) + 0.5 * x * (1.0 - t * t) * (GELU_K0 * (1.0 + 3.0 * GELU_K1 * x2))
    return g, dg


def _expm1(x):
    p = x * (1.0 + x * (1.0 / 2.0 + x * (1.0 / 6.0 + x * (1.0 / 24.0 + x * (1.0 / 120.0)))))
    return jnp.where(jnp.abs(x) < 0.0625, p, jnp.exp(x) - 1.0)


def _log1p_pos(e):
    p = e * (1.0 - e * (1.0 / 2.0) + e * e * (1.0 / 3.0) - e * e * e * (1.0 / 4.0))
    return jnp.where(e < 1e-2, p, jnp.log(1.0 + e))


def _ln_stats(z):
    mu = jnp.mean(z, axis=-1, keepdims=True)
    zc = z - mu
    var = jnp.mean(zc * zc, axis=-1, keepdims=True)
    rstd = lax.rsqrt(var + LN_EPS)
    return zc * rstd, rstd


def _ln_bwd(dy, xhat, rstd, g):
    dxh = dy * g
    m1 = jnp.mean(dxh, axis=-1, keepdims=True)
    m2 = jnp.mean(dxh * xhat, axis=-1, keepdims=True)
    return rstd * (dxh - m1 - xhat * m2)


def _colsum(v):
    return jnp.sum(v, axis=0, keepdims=True)


def _first_step():
    return jnp.logical_and(pl.program_id(0) == 0, pl.program_id(1) == 0)


def _exchange(arrs, gather, name):
    n = len(arrs)
    n_peer = N_DEV - 1

    def body(*refs):
        ins, outs = refs[:n], refs[n:2 * n]
        send_sems, recv_sems, loc_sems = refs[2 * n:]
        x, y, c = lax.axis_index("x"), lax.axis_index("y"), lax.axis_index("c")
        me = 4 * x + 2 * y + c
        started = []
        for a in range(n):
            src_me = ins[a] if gather else ins[a].at[me]
            lc = pltpu.make_async_copy(src_me, outs[a].at[me], loc_sems.at[a])
            lc.start()
            started.append((lc, None))
        for p in range(1, N_DEV):
            px, py, pc = x ^ ((p >> 2) & 1), y ^ ((p >> 1) & 1), c ^ (p & 1)
            peer = 4 * px + 2 * py + pc
            for a in range(n):
                k = a * n_peer + (p - 1)
                src = ins[a] if gather else ins[a].at[peer]
                cp = pltpu.make_async_remote_copy(src_ref=src, dst_ref=outs[a].at[me],
                                                  send_sem=send_sems.at[k], recv_sem=recv_sems.at[k],
                                                  device_id=(px, py, pc), device_id_type=MESH)
                cp.start()
                rc = pltpu.make_async_remote_copy(src_ref=src, dst_ref=outs[a].at[peer],
                                                  send_sem=send_sems.at[k], recv_sem=recv_sems.at[k],
                                                  device_id=(px, py, pc), device_id_type=MESH)
                started.append((cp, rc))
        for cp, rc in started:
            if rc is None:
                cp.wait()
            else:
                cp.wait_send()
                rc.wait_recv()

    hbm = pl.BlockSpec(memory_space=pltpu.HBM)
    out_shape = tuple(
        jax.ShapeDtypeStruct(((N_DEV,) + a.shape) if gather else a.shape, a.dtype) for a in arrs)
    return pl.pallas_call(
        body, name=name, out_shape=out_shape,
        in_specs=[hbm] * n, out_specs=tuple([hbm] * n),
        scratch_shapes=[pltpu.SemaphoreType.DMA((n * n_peer,)), pltpu.SemaphoreType.DMA((n * n_peer,)),
                        pltpu.SemaphoreType.DMA((n,))],
        compiler_params=pltpu.CompilerParams(has_side_effects=True),
    )(*arrs)


_HBM = pl.BlockSpec(memory_space=pltpu.HBM)
_SEM = pl.BlockSpec(memory_space=pltpu.SEMAPHORE)
_EFFECT = pltpu.SideEffectType.DATAFLOW_SIDE_EFFECTING


def _peer_of(p):
    x, y, c = lax.axis_index("x"), lax.axis_index("y"), lax.axis_index("c")
    px, py, pc = x ^ ((p >> 2) & 1), y ^ ((p >> 1) & 1), c ^ (p & 1)
    return (px, py, pc), 4 * px + 2 * py + pc


def _slot(land_ref, idx, width):
    if width is None:
        return land_ref.at[idx]
    return land_ref.at[:, pl.ds(pl.multiple_of(idx * width, LANES), width)]


def _xstart(srcs, gather, after, name, cols=None):
    n = len(srcs)
    cols = cols or [False] * n
    widths = [t.shape[1] if cols[a] else None for a, t in enumerate(srcs)]
    lands = [lax.empty((t.shape[0], N_DEV * t.shape[1]) if cols[a] else (((N_DEV,) + t.shape) if gather else t.shape),
                       t.dtype) for a, t in enumerate(srcs)]
    n_after = 0 if after is None else 1

    def body(*refs):
        src_refs, land_refs = refs[:n], refs[n:2 * n]
        refs = refs[n_after:]
        send_sems, recv_sems = refs[2 * n:3 * n], refs[3 * n:4 * n]
        token = refs[6 * n]
        me = 4 * lax.axis_index("x") + 2 * lax.axis_index("y") + lax.axis_index("c")
        for a in range(n):
            for p in range(1, N_DEV):
                dev, peer = _peer_of(p)
                pltpu.make_async_remote_copy(
                    src_ref=src_refs[a] if gather else src_refs[a].at[peer], dst_ref=_slot(land_refs[a], me, widths[a]),
                    send_sem=send_sems[a].at[p - 1], recv_sem=recv_sems[a].at[p - 1],
                    device_id=dev, device_id_type=MESH).start()
        token[...] = jnp.zeros_like(token)

    sems = tuple(pltpu.SemaphoreType.DMA((N_DEV - 1,)) for _ in range(2 * n))
    thru = tuple(pltpu.HBM(t.shape, t.dtype) for t in list(srcs) + list(lands))
    res = pl.pallas_call(
        body, name=name,
        out_shape=sems + thru + (jax.ShapeDtypeStruct((SUBLANES, LANES), F32),),
        in_specs=[_HBM] * (2 * n) + [pl.BlockSpec(memory_space=pl.ANY)] * n_after,
        out_specs=tuple([_SEM] * (2 * n) + [_HBM] * (2 * n) + [pl.BlockSpec(memory_space=pltpu.VMEM)]),
        input_output_aliases={i: 2 * n + i for i in range(2 * n)},
        compiler_params=pltpu.CompilerParams(has_side_effects=_EFFECT),
    )(*[pltpu.with_memory_space_constraint(t, pltpu.HBM) for t in list(srcs) + list(lands)],
      *([after] if n_after else []))
    return res[:n], res[n:2 * n], res[2 * n:3 * n], res[3 * n:4 * n], res[4 * n]


def _xwait(src, land, send_sem, recv_sem, after, gather, name, col=False):
    width = src.shape[1] if col else None

    def body(src_ref, land_ref, send_ref, recv_ref, after_ref, src_dead, land_out):
        del after_ref, src_dead, land_out
        for p in range(1, N_DEV):
            dev, peer = _peer_of(p)
            cp = pltpu.make_async_remote_copy(
                src_ref=src_ref if gather else src_ref.at[peer], dst_ref=_slot(land_ref, peer, width),
                send_sem=send_ref.at[p - 1], recv_sem=recv_ref.at[p - 1], device_id=dev, device_id_type=MESH)
            cp.wait_send()
            cp.wait_recv()

    src_done, landed = pl.pallas_call(
        body, name=name, out_shape=(pltpu.HBM(src.shape, src.dtype), pltpu.HBM(land.shape, land.dtype)),
        in_specs=[_HBM, _HBM, _SEM, _SEM, pl.BlockSpec(memory_space=pl.ANY)], out_specs=(_HBM, _HBM),
        input_output_aliases={0: 0, 1: 1},
        compiler_params=pltpu.CompilerParams(has_side_effects=_EFFECT),
    )(src, land, send_sem, recv_sem, after)
    me = 4 * lax.axis_index("x") + 2 * lax.axis_index("y") + lax.axis_index("c")
    if col:
        return lax.dynamic_update_slice(landed, src_done, (0, me * width))
    own = src_done if gather else lax.dynamic_index_in_dim(src_done, me, 0, keepdims=False)
    return lax.dynamic_update_slice(landed, own[None], (me,) + (0,) * own.ndim)


def _xwait_many(srcs, lands, send_sems, recv_sems, after, name):
    n = len(srcs)

    def body(*refs):
        src_refs, land_refs = refs[:n], refs[n:2 * n]
        snd, rcv = refs[2 * n:3 * n], refs[3 * n:4 * n]
        for a in range(n):
            for p in range(1, N_DEV):
                dev, peer = _peer_of(p)
                cp = pltpu.make_async_remote_copy(
                    src_ref=src_refs[a], dst_ref=land_refs[a].at[peer], send_sem=snd[a].at[p - 1],
                    recv_sem=rcv[a].at[p - 1], device_id=dev, device_id_type=MESH)
                cp.wait_send()
                cp.wait_recv()

    res = pl.pallas_call(
        body, name=name, out_shape=tuple(pltpu.HBM(t.shape, t.dtype) for t in list(srcs) + list(lands)),
        in_specs=[_HBM] * (2 * n) + [_SEM] * (2 * n) + [pl.BlockSpec(memory_space=pl.ANY)],
        out_specs=tuple([_HBM] * (2 * n)), input_output_aliases={i: i for i in range(2 * n)},
        compiler_params=pltpu.CompilerParams(has_side_effects=_EFFECT),
    )(*srcs, *lands, *send_sems, *recv_sems, after)
    return res[:n], res[n:]


def _mm(a, b, *, mode, tm, tn, tk, outs, epilogue=None, extras=(), nb=None, tok=None, name):
    if mode == "nn":
        (M, K), (_, N) = a.shape, b.shape
    elif mode == "nt":
        (M, K), (N, _) = a.shape, b.shape
    else:
        (K, M), (_, N) = a.shape, b.shape
    tm, tn, tk = min(tm, M), min(tn, N), min(tk, K)
    assert M % tm == 0 and N % tn == 0 and K % tk == 0, (name, M, N, K, tm, tn, tk)
    if mode == "nn":
        a_spec = pl.BlockSpec((tm, tk), lambda i, j, k: (i, k))
        b_spec = pl.BlockSpec((tk, tn), lambda i, j, k: (k, j))
        dims = (((1,), (0,)), ((), ()))
    elif mode == "nt":
        a_spec = pl.BlockSpec((tm, tk), lambda i, j, k: (i, k))
        b_spec = pl.BlockSpec((tn, tk), lambda i, j, k: (j, k))
        dims = (((1,), (1,)), ((), ()))
    else:
        a_spec = pl.BlockSpec((tk, tm), lambda i, j, k: (k, i))
        b_spec = pl.BlockSpec((tk, tn), lambda i, j, k: (k, j))
        dims = (((0,), (0,)), ((), ()))
    nk = K // tk
    n_ex, n_out = len(extras), len(outs)
    n_tok = 0 if tok is None else 1
    nbytes = lambda d: jnp.dtype(d).itemsize
    vmem_est = (2 * (tm * tk * nbytes(a.dtype) + tk * tn * nbytes(b.dtype)
                     + sum(tm * tn * nbytes(e.dtype) for e, kind in extras if kind == "tile")
                     + sum(tm * tn * nbytes(d) for d in outs)) + tm * tn * 4)
    assert vmem_est <= VMEM_LIMIT, (name, vmem_est)
    if epilogue is None:
        epilogue = lambda acc, ex: tuple(acc.astype(d) for d in outs)

    def body(a_ref, b_ref, *refs):
        refs = refs[n_tok:]
        ex_refs, out_refs = refs[:n_ex], refs[n_ex:n_ex + n_out]

        def finish(acc):
            res = epilogue(acc, [r[...] for r in ex_refs])
            for o_ref, v in zip(out_refs, res):
                if nb is None:
                    o_ref[...] = v.astype(o_ref.dtype)
                else:
                    for q in range(tn // nb):
                        o_ref[q] = v[:, q * nb:(q + 1) * nb].astype(o_ref.dtype)

        part = lax.dot_general(a_ref[...], b_ref[...], dims, preferred_element_type=F32)
        if nk == 1:
            finish(part)
        else:
            acc_ref = refs[n_ex + n_out]
            k = pl.program_id(2)

            @pl.when(k == 0)
            def _():
                acc_ref[...] = part

            @pl.when(k > 0)
            def _():
                acc_ref[...] += part

            @pl.when(k == nk - 1)
            def _():
                finish(acc_ref[...])

    ex_specs = [pl.BlockSpec((tm, tn), lambda i, j, k: (i, j)) if kind == "tile"
                else pl.BlockSpec((1, tn), lambda i, j, k: (0, j)) for _, kind in extras]
    if nb is not None:
        assert tn % nb == 0, (name, tn, nb)
        o_spec = pl.BlockSpec((tn // nb, tm, nb), lambda i, j, k: (j, i, 0))
        o_shape = (N // nb, M, nb)
    else:
        o_spec = pl.BlockSpec((tm, tn), lambda i, j, k: (i, j))
        o_shape = (M, N)
    res = pl.pallas_call(
        body, name=name, grid=(M // tm, N // tn, nk),
        in_specs=[a_spec, b_spec] + [pl.BlockSpec((SUBLANES, LANES), lambda i, j, k: (0, 0))] * n_tok + ex_specs,
        out_specs=tuple([o_spec] * n_out),
        out_shape=tuple(jax.ShapeDtypeStruct(o_shape, d) for d in outs),
        scratch_shapes=[pltpu.VMEM((tm, tn), F32)] if nk > 1 else [],
        compiler_params=_cparams(3, big=True),
    )(a, b, *([tok] if n_tok else []), *[e for e, _ in extras])
    return res[0] if n_out == 1 else res


def _mm_rows(a, b, *, mode, tm, seq, ins, outs, epilogue, tok=None, name):
    M, K = a.shape
    N = b.shape[1] if mode == "nn" else b.shape[0]
    tm = min(tm, M)
    assert M % tm == 0 and seq % tm == 0, (name, M, seq, tm)
    tpb = seq // tm
    n_b = M // seq
    dims = (((1,), (0,)), ((), ())) if mode == "nn" else (((1,), (1,)), ((), ()))
    n_tok = 0 if tok is None else 1
    n_in, n_out = len(ins), len(outs)

    in_specs, in_arrs = [], []
    for spec in ins:
        kind, arr = spec[0], spec[1]
        in_arrs.append(arr)
        if kind == "tile":
            in_specs.append(pl.BlockSpec((tm, arr.shape[1]), lambda i: (i, 0)))
        elif kind == "tilecol":
            in_specs.append(pl.BlockSpec((tm, spec[2]), lambda i, cb=spec[3]: (i, cb)))
        elif kind == "row":
            in_specs.append(pl.BlockSpec(arr.shape, lambda i: (0, 0)))
        else:
            in_specs.append(pl.BlockSpec((None, 1, arr.shape[2]), lambda i: (i // tpb, 0, 0)))
    out_specs, out_shapes = [], []
    for spec in outs:
        kind = spec[0]
        if kind == "tile":
            out_specs.append(pl.BlockSpec((tm, spec[2]), lambda i: (i, 0)))
            out_shapes.append(jax.ShapeDtypeStruct((M, spec[2]), spec[1]))
        elif kind == "tilecol":
            out_specs.append(pl.BlockSpec((tm, spec[2]), lambda i, cb=spec[3]: (i, cb)))
            out_shapes.append(jax.ShapeDtypeStruct((M, spec[4]), spec[1]))
        elif kind == "acc_row":
            out_specs.append(pl.BlockSpec((1, spec[1]), lambda i: (0, 0)))
            out_shapes.append(jax.ShapeDtypeStruct((1, spec[1]), F32))
        elif kind == "acc_brow":
            out_specs.append(pl.BlockSpec((None, 1, spec[1]), lambda i: (i // tpb, 0, 0)))
            out_shapes.append(jax.ShapeDtypeStruct((n_b, 1, spec[1]), F32))
        else:
            out_specs.append(pl.BlockSpec((SUBLANES, LANES), lambda i: (0, 0)))
            out_shapes.append(jax.ShapeDtypeStruct((SUBLANES, LANES), F32))

    def body(a_ref, b_ref, *refs):
        refs = refs[n_tok:]
        in_refs, out_refs = refs[:n_in], refs[n_in:n_in + n_out]
        i = pl.program_id(0)
        prod = lax.dot_general(a_ref[...], b_ref[...], dims, preferred_element_type=F32)
        vals = epilogue(prod, [r[...] for r in in_refs])
        for spec, o_ref, v in zip(outs, out_refs, vals):
            kind = spec[0]
            if kind in ("tile", "tilecol"):
                o_ref[...] = v.astype(o_ref.dtype)
            else:
                first = (i % tpb == 0) if kind == "acc_brow" else (i == 0)

                @pl.when(first)
                def _(o_ref=o_ref, v=v):
                    o_ref[...] = jnp.broadcast_to(v, o_ref.shape)

                @pl.when(jnp.logical_not(first))
                def _(o_ref=o_ref, v=v):
                    o_ref[...] += v

    res = pl.pallas_call(
        body, name=name, grid=(M // tm,),
        in_specs=[pl.BlockSpec((tm, K), lambda i: (i, 0)),
                  pl.BlockSpec(b.shape, lambda i: (0, 0), pipeline_mode=pl.Buffered(1))]
                 + [pl.BlockSpec((SUBLANES, LANES), lambda i: (0, 0))] * n_tok + in_specs,
        out_specs=tuple(out_specs), out_shape=tuple(out_shapes),
        compiler_params=_cparams(1, big=True),
    )(a, b, *([tok] if n_tok else []), *in_arrs)
    return res


def _tok_spec(ts, width, col_block=0):
    return pl.BlockSpec((None, ts, width), lambda b, s: (b, s, col_block))


def _brow_spec(width):
    return pl.BlockSpec((None, 1, width), lambda b, s: (b, 0, 0))


def _vec_spec(width):
    return pl.BlockSpec((1, width), lambda b, s: (0, 0))


def _modulate(x, sc, sh, ts):
    Bl, S, D = x.shape

    def body(x_ref, sc_ref, sh_ref, o_ref):
        o_ref[...] = (x_ref[...] * (1.0 + sc_ref[...]) + sh_ref[...]).astype(BF16)

    return pl.pallas_call(
        body, name="modulate", grid=(Bl, S // ts),
        in_specs=[_tok_spec(ts, D), _brow_spec(D), _brow_spec(D)],
        out_specs=_tok_spec(ts, D), out_shape=jax.ShapeDtypeStruct((Bl, S, D), BF16),
        compiler_params=_cparams(2),
    )(x, sc, sh)


def _mix_fwd(proj, w_conv, b_conv, w_rg_a, b_rg_a, w_rg_x, b_rg_x, lam, w_sp, b_sp_t, ln_v_g, ln_v_b, *, tm, lw, sw):
    Bl, S, _ = proj.shape
    heads, hd = w_rg_a.shape[0], w_rg_a.shape[1]
    groups = w_sp.shape[0]
    cw = 2 * lw + 2 * sw
    nblk = tm // SGU_BLOCK

    G = tm // SUBLANES
    nc = lw // LANES

    def body(p_ref, wc_ref, bc_ref, wa_ref, ba_ref, wx_ref, bx_ref, lam_ref, wsp_ref, bsp_ref, lg_ref, lb_ref,
             hs_ref, ya_ref, ys_ref, xc_ref, r_ref, ig_ref, a_ref, m_ref,
             xext, hnat, hcar, h7_scr, a7_scr, hp_scr):
        s = pl.program_id(1)

        @pl.when(s == 0)
        def _():
            xext[:, 0:SUBLANES, :] = jnp.zeros((nc, SUBLANES, LANES), F32)
            hcar[...] = jnp.zeros_like(hcar)

        @pl.when(s > 0)
        def _():
            xext[:, 0:SUBLANES, :] = xext[:, tm:tm + SUBLANES, :]

        for c in range(nc):
            xext[c, SUBLANES:SUBLANES + tm, :] = p_ref[:, c * LANES:(c + 1) * LANES].astype(F32)
        gl = p_ref[:, lw:2 * lw].astype(F32)

        def slab(ref3, start):
            return jnp.concatenate([ref3[c, pl.ds(start, G, stride=SUBLANES), :] for c in range(nc)], axis=1)

        xs = {st: slab(xext, st) for st in range(SUBLANES - 3, 2 * SUBLANES)}
        xc_slabs = []
        for j in range(SUBLANES):
            acc = bc_ref[...] + xs[SUBLANES + j] * wc_ref[3:4, :]
            for k in (1, 2, 3):
                acc = acc + xs[SUBLANES + j - k] * wc_ref[3 - k:4 - k, :]
            xc_slabs.append(acc)
        xc = jnp.concatenate(xc_slabs, axis=0)

        xcb = xc.astype(BF16)
        pa = jnp.concatenate([jnp.dot(xcb[:, h * hd:(h + 1) * hd], wa_ref[h], preferred_element_type=F32)
                              for h in range(heads)], axis=1) + ba_ref[...]
        px = jnp.concatenate([jnp.dot(xcb[:, h * hd:(h + 1) * hd], wx_ref[h], preferred_element_type=F32)
                              for h in range(heads)], axis=1) + bx_ref[...]
        r = _sigmoid(pa)
        ig = _sigmoid(px)
        nl = -lam_ref[...]
        big_l = -LRU_C * (jnp.maximum(nl, 0.0) + _log1p_pos(jnp.exp(-jnp.abs(nl))))
        la = big_l * r
        a = jnp.exp(la)
        m = jnp.sqrt(-_expm1(2.0 * la))
        bin_ = m * (ig * xc)
        xc_ref[...] = xc
        r_ref[...] = r
        ig_ref[...] = ig
        a_ref[...] = a
        m_ref[...] = m

        h0 = [bin_[0:G]]
        cp = [a[0:G]]
        for j in range(1, SUBLANES):
            aj = a[j * G:(j + 1) * G]
            h0.append(aj * h0[j - 1] + bin_[j * G:(j + 1) * G])
            cp.append(aj * cp[j - 1])
        h7_scr[...] = h0[SUBLANES - 1]
        a7_scr[...] = cp[SUBLANES - 1]
        carry = hcar[0:1, :]
        for g in range(G):
            hp_scr[g:g + 1, :] = carry
            carry = h7_scr[g:g + 1, :] + a7_scr[g:g + 1, :] * carry
        hcar[0:1, :] = carry
        hprev = hp_scr[...]
        for j in range(SUBLANES):
            hj = h0[j] + cp[j] * hprev
            for c in range(nc):
                hnat[c, pl.ds(j, G, stride=SUBLANES), :] = hj[:, c * LANES:(c + 1) * LANES]
        hs = jnp.concatenate([hnat[c] for c in range(nc)], axis=1)
        hs_ref[...] = hs
        ya_ref[...] = (hs * _gelu(gl)).astype(BF16)

        gu = _gelu(p_ref[:, 2 * lw:2 * lw + sw].astype(F32))
        gv = _gelu(p_ref[:, 2 * lw + sw:cw].astype(F32))
        xhat, _ = _ln_stats(gv)
        vn = (xhat * lg_ref[...] + lb_ref[...]).astype(BF16)
        tpos = lax.broadcasted_iota(jnp.int32, (SGU_BLOCK, SGU_BLOCK), 0) // CHUNK
        spos = lax.broadcasted_iota(jnp.int32, (SGU_BLOCK, SGU_BLOCK), 1) // CHUNK
        gw = sw // groups
        rows_out = []
        for blk in range(nblk):
            r0 = blk * SGU_BLOCK
            cols = []
            for g in range(groups):
                wm = jnp.where(spos <= tpos, wsp_ref[g], 0.0).astype(BF16)
                mixed = jnp.dot(wm, vn[r0:r0 + SGU_BLOCK, g * gw:(g + 1) * gw], preferred_element_type=F32)
                cols.append(mixed + bsp_ref[:, g:g + 1])
            rows_out.append(jnp.concatenate(cols, axis=1))
        mixed_all = jnp.concatenate(rows_out, axis=0) if nblk > 1 else rows_out[0]
        ys_ref[...] = (gu * mixed_all).astype(BF16)

    full = lambda shp: pl.BlockSpec(shp, lambda b, s: (0,) * len(shp))
    return pl.pallas_call(
        body, name="mix_fwd", grid=(Bl, S // tm),
        in_specs=[_tok_spec(tm, cw), full(w_conv.shape), full(b_conv.shape), full(w_rg_a.shape), full(b_rg_a.shape),
                  full(w_rg_x.shape), full(b_rg_x.shape), full(lam.shape), full(w_sp.shape), full(b_sp_t.shape),
                  full(ln_v_g.shape), full(ln_v_b.shape)],
        out_specs=(_tok_spec(tm, lw), _tok_spec(tm, lw), _tok_spec(tm, sw)) + (_tok_spec(tm, lw),) * 5,
        out_shape=(jax.ShapeDtypeStruct((Bl, S, lw), F32), jax.ShapeDtypeStruct((Bl, S, lw), BF16),
                   jax.ShapeDtypeStruct((Bl, S, sw), BF16)) + (jax.ShapeDtypeStruct((Bl, S, lw), F32),) * 5,
        scratch_shapes=[pltpu.VMEM((nc, tm + SUBLANES, LANES), F32), pltpu.VMEM((nc, tm, LANES), F32),
                        pltpu.VMEM((SUBLANES, lw), F32), pltpu.VMEM((G, lw), F32), pltpu.VMEM((G, lw), F32),
                        pltpu.VMEM((G, lw), F32)],
        compiler_params=_cparams(2, big=True),
    )(proj, w_conv, b_conv, w_rg_a, b_rg_a, w_rg_x, b_rg_x, lam, w_sp, b_sp_t, ln_v_g, ln_v_b)


def _merge_fwd(proj, y_a, y_b, *, ts, d):
    Bl, S, din = proj.shape
    gcol = (din - 2 * d) // (2 * d)
    assert gcol * 2 * d == din - 2 * d

    def body(g_ref, ya_ref, yb_ref, o_ref):
        sa = _sigmoid(g_ref[:, 0:d].astype(F32))
        sb = _sigmoid(g_ref[:, d:2 * d].astype(F32))
        o_ref[...] = (sa * ya_ref[...].astype(F32) + sb * yb_ref[...].astype(F32)).astype(BF16)

    return pl.pallas_call(
        body, name="merge_fwd", grid=(Bl, S // ts),
        in_specs=[_tok_spec(ts, 2 * d, gcol), _tok_spec(ts, d), _tok_spec(ts, d)],
        out_specs=_tok_spec(ts, d), out_shape=jax.ShapeDtypeStruct((Bl, S, d), BF16),
        compiler_params=_cparams(2),
    )(proj, y_a, y_b)


def _ln1_fwd(x, mix, gt1, g1, b1, sc2, sh2, *, ts):
    Bl, S, D = x.shape

    def body(x_ref, mix_ref, gt_ref, g_ref, b_ref, sc_ref, sh_ref, x1_ref, h2_ref):
        z = ALPHA * x_ref[...] + (1.0 + gt_ref[...]) * mix_ref[...].astype(F32)
        xhat, _ = _ln_stats(z)
        x1 = xhat * g_ref[...] + b_ref[...]
        x1_ref[...] = x1
        h2_ref[...] = (x1 * (1.0 + sc_ref[...]) + sh_ref[...]).astype(BF16)

    return pl.pallas_call(
        body, name="ln1_fwd", grid=(Bl, S // ts),
        in_specs=[_tok_spec(ts, D), _tok_spec(ts, D), _brow_spec(D), _vec_spec(D), _vec_spec(D), _brow_spec(D),
                  _brow_spec(D)],
        out_specs=(_tok_spec(ts, D), _tok_spec(ts, D)),
        out_shape=(jax.ShapeDtypeStruct((Bl, S, D), F32), jax.ShapeDtypeStruct((Bl, S, D), BF16)),
        compiler_params=_cparams(2),
    )(x, mix, gt1, g1, b1, sc2, sh2)


def _ln2_loss(x1, f, tgt, gt2, g2, b2, *, ts):
    Bl, S, D = x1.shape

    def body(x1_ref, f_ref, t_ref, gt_ref, g_ref, b_ref, df_ref, dx1_ref, dgt_ref, dg_ref, db_ref, loss_ref):
        s = pl.program_id(1)

        @pl.when(_first_step())
        def _():
            dg_ref[...] = jnp.zeros_like(dg_ref)
            db_ref[...] = jnp.zeros_like(db_ref)
            loss_ref[...] = jnp.zeros_like(loss_ref)

        @pl.when(s == 0)
        def _():
            dgt_ref[...] = jnp.zeros_like(dgt_ref)

        fv = f_ref[...]
        z = ALPHA * x1_ref[...] + (1.0 + gt_ref[...]) * fv
        xhat, rstd = _ln_stats(z)
        x2 = xhat * g_ref[...] + b_ref[...]
        err = x2 - t_ref[...]
        loss_ref[...] += 0.5 * jnp.sum(jnp.mean(err * err, axis=-1, keepdims=True))
        dy = err * (1.0 / D)
        dg_ref[...] += _colsum(dy * xhat)
        db_ref[...] += _colsum(dy)
        dz = _ln_bwd(dy, xhat, rstd, g_ref[...])
        dx1_ref[...] = ALPHA * dz
        dgt_ref[...] += _colsum(dz * fv)
        df_ref[...] = (dz * (1.0 + gt_ref[...])).astype(BF16)

    return pl.pallas_call(
        body, name="ln2_loss", grid=(Bl, S // ts),
        in_specs=[_tok_spec(ts, D), _tok_spec(ts, D), _tok_spec(ts, D), _brow_spec(D), _vec_spec(D), _vec_spec(D)],
        out_specs=(_tok_spec(ts, D), _tok_spec(ts, D), _brow_spec(D), _vec_spec(D), _vec_spec(D),
                   pl.BlockSpec((SUBLANES, LANES), lambda b, s: (0, 0))),
        out_shape=(jax.ShapeDtypeStruct((Bl, S, D), BF16), jax.ShapeDtypeStruct((Bl, S, D), F32),
                   jax.ShapeDtypeStruct((Bl, 1, D), F32), jax.ShapeDtypeStruct((1, D), F32),
                   jax.ShapeDtypeStruct((1, D), F32), jax.ShapeDtypeStruct((SUBLANES, LANES), F32)),
        compiler_params=_cparams(2),
    )(x1, f, tgt, gt2, g2, b2)


def _ln1_bwd(dx1p, dh2, x1, x, mix, sc2, gt1, g1, *, ts):
    Bl, S, D = x.shape

    def body(dx1p_ref, dh2_ref, x1_ref, x_ref, mix_ref, sc_ref, gt_ref, g_ref,
             dxp_ref, dmix_ref, dsc_ref, dsh_ref, dgt_ref, dg_ref, db_ref):
        s = pl.program_id(1)

        @pl.when(_first_step())
        def _():
            dg_ref[...] = jnp.zeros_like(dg_ref)
            db_ref[...] = jnp.zeros_like(db_ref)

        @pl.when(s == 0)
        def _():
            dsc_ref[...] = jnp.zeros_like(dsc_ref)
            dsh_ref[...] = jnp.zeros_like(dsh_ref)
            dgt_ref[...] = jnp.zeros_like(dgt_ref)

        dh2 = dh2_ref[...].astype(F32)
        mixv = mix_ref[...].astype(F32)
        dsc_ref[...] += _colsum(dh2 * x1_ref[...])
        dsh_ref[...] += _colsum(dh2)
        dx1 = dx1p_ref[...] + dh2 * (1.0 + sc_ref[...])
        z = ALPHA * x_ref[...] + (1.0 + gt_ref[...]) * mixv
        xhat, rstd = _ln_stats(z)
        dg_ref[...] += _colsum(dx1 * xhat)
        db_ref[...] += _colsum(dx1)
        dz = _ln_bwd(dx1, xhat, rstd, g_ref[...])
        dxp_ref[...] = ALPHA * dz
        dgt_ref[...] += _colsum(dz * mixv)
        dmix_ref[...] = (dz * (1.0 + gt_ref[...])).astype(BF16)

    return pl.pallas_call(
        body, name="ln1_bwd", grid=(Bl, S // ts),
        in_specs=[_tok_spec(ts, D)] * 5 + [_brow_spec(D), _brow_spec(D), _vec_spec(D)],
        out_specs=(_tok_spec(ts, D), _tok_spec(ts, D), _brow_spec(D), _brow_spec(D), _brow_spec(D), _vec_spec(D),
                   _vec_spec(D)),
        out_shape=(jax.ShapeDtypeStruct((Bl, S, D), F32), jax.ShapeDtypeStruct((Bl, S, D), BF16),
                   jax.ShapeDtypeStruct((Bl, 1, D), F32), jax.ShapeDtypeStruct((Bl, 1, D), F32),
                   jax.ShapeDtypeStruct((Bl, 1, D), F32), jax.ShapeDtypeStruct((1, D), F32),
                   jax.ShapeDtypeStruct((1, D), F32)),
        compiler_params=_cparams(2),
    )(dx1p, dh2, x1, x, mix, sc2, gt1, g1)


def _merge_bwd(dmerged, y_a, y_b, proj, *, ts, d):
    Bl, S, din = proj.shape
    gcol = (din - 2 * d) // (2 * d)

    def body(dm_ref, ya_ref, yb_ref, g_ref, dya_ref, dyb_ref, dp_ref, db_ref):
        @pl.when(_first_step())
        def _():
            db_ref[...] = jnp.zeros_like(db_ref)

        dm = dm_ref[...].astype(F32)
        sa = _sigmoid(g_ref[:, 0:d].astype(F32))
        sb = _sigmoid(g_ref[:, d:2 * d].astype(F32))
        dya_ref[...] = (dm * sa).astype(BF16)
        dyb_ref[...] = (dm * sb).astype(BF16)
        dga = dm * ya_ref[...].astype(F32) * sa * (1.0 - sa)
        dgb = dm * yb_ref[...].astype(F32) * sb * (1.0 - sb)
        dp_ref[:, 0:d] = dga.astype(BF16)
        dp_ref[:, d:2 * d] = dgb.astype(BF16)
        db_ref[:, 0:d] += _colsum(dga)
        db_ref[:, d:2 * d] += _colsum(dgb)

    return pl.pallas_call(
        body, name="merge_bwd", grid=(Bl, S // ts),
        in_specs=[_tok_spec(ts, d), _tok_spec(ts, d), _tok_spec(ts, d), _tok_spec(ts, 2 * d, gcol)],
        out_specs=(_tok_spec(ts, d), _tok_spec(ts, d), _tok_spec(ts, 2 * d, gcol), _vec_spec(2 * d)),
        out_shape=(jax.ShapeDtypeStruct((Bl, S, d), BF16), jax.ShapeDtypeStruct((Bl, S, d), BF16),
                   jax.ShapeDtypeStruct((Bl, S, din), BF16), jax.ShapeDtypeStruct((1, 2 * d), F32)),
        compiler_params=_cparams(2),
    )(dmerged, y_a, y_b, proj)


def _mix_bwd(proj, hs, dya, dys, dproj, saved, w_conv, b_conv, w_rg_a, b_rg_a, w_rg_x, b_rg_x, lam, w_sp, b_sp_t,
             ln_v_g, ln_v_b, *, tm, lw, sw):
    Bl, S, din = proj.shape
    heads, hd = w_rg_a.shape[0], w_rg_a.shape[1]
    groups = w_sp.shape[0]
    gw = sw // groups
    cw = 2 * lw + 2 * sw
    nblk = tm // SGU_BLOCK
    n_s = S // tm
    per8 = tm // SUBLANES
    halo_rows = 2 * SUBLANES

    G = tm // SUBLANES
    nc = lw // LANES

    def body(p_ref, xh_ref, hs_ref, hh_ref, dya_ref, dys_ref, dpin_ref, xc_ref, r_ref, ig_ref, a_ref, m_ref,
             wc_ref, bc_ref, wa_ref, ba_ref, wx_ref, bx_ref, lam_ref, wsp_ref, bsp_ref, lg_ref, lb_ref,
             dp_ref, dbin_ref, dwc_ref, dbc_ref, dwa_ref, dba_ref, dwx_ref, dbx_ref, dlam_ref, dwsp_ref, dbsp_ref,
             dlg_ref, dlb_ref,
             xext, hext, dnat, dxext, dhcar, g00_scr, p0_scr, a0_scr, cin_scr):
        del dpin_ref
        sr = pl.program_id(1)
        first_tile = sr == n_s - 1

        @pl.when(_first_step())
        def _():
            for ref in (dbin_ref, dwc_ref, dbc_ref, dwa_ref, dba_ref, dwx_ref, dbx_ref, dlam_ref, dwsp_ref, dbsp_ref,
                        dlg_ref, dlb_ref):
                ref[...] = jnp.zeros_like(ref)

        @pl.when(sr == 0)
        def _():
            dhcar[...] = jnp.zeros_like(dhcar)
            dxext[:, tm:tm + SUBLANES, :] = jnp.zeros((nc, SUBLANES, LANES), F32)

        @pl.when(sr > 0)
        def _():
            dxext[:, tm:tm + SUBLANES, :] = dxext[:, 0:SUBLANES, :]

        def slab(ref3, start):
            return jnp.concatenate([ref3[c, pl.ds(start, G, stride=SUBLANES), :] for c in range(nc)], axis=1)

        def put_slab(ref3, j, val):
            for c in range(nc):
                ref3[c, pl.ds(j, G, stride=SUBLANES), :] = val[:, c * LANES:(c + 1) * LANES]

        keep = jnp.where(first_tile, 0.0, 1.0)
        xprev = xh_ref[...].astype(F32)[halo_rows - SUBLANES:halo_rows] * keep
        hsv = hs_ref[...]
        hprev8 = hh_ref[...] * keep
        for c in range(nc):
            cs = slice(c * LANES, (c + 1) * LANES)
            xext[c, 0:SUBLANES, :] = xprev[:, cs]
            xext[c, SUBLANES:SUBLANES + tm, :] = p_ref[:, cs].astype(F32)
            hext[c, 0:SUBLANES, :] = hprev8[:, cs]
            hext[c, SUBLANES:SUBLANES + tm, :] = hsv[:, cs]
        gl = p_ref[:, lw:2 * lw].astype(F32)
        ggl, dggl = _gelu_and_grad(gl)
        dyav = dya_ref[...]
        dhs = dyav * ggl
        dgl = dyav * hsv * dggl
        dp_ref[:, lw:2 * lw] = dgl.astype(BF16)
        dbin_ref[:, lw:2 * lw] += _colsum(dgl)
        for c in range(nc):
            dnat[c] = dhs[:, c * LANES:(c + 1) * LANES]

        xc, r, ig, a, m = xc_ref[...], r_ref[...], ig_ref[...], a_ref[...], m_ref[...]
        xcb = xc.astype(BF16)
        nl = -lam_ref[...]
        big_l = -LRU_C * (jnp.maximum(nl, 0.0) + _log1p_pos(jnp.exp(-jnp.abs(nl))))

        g0 = [None] * SUBLANES
        pp = [None] * SUBLANES
        g0[SUBLANES - 1] = slab(dnat, SUBLANES - 1)
        for j in range(SUBLANES - 2, -1, -1):
            an = a[(j + 1) * G:(j + 2) * G]
            g0[j] = slab(dnat, j) + an * g0[j + 1]
            pp[j] = an if j == SUBLANES - 2 else an * pp[j + 1]
        g00_scr[...] = g0[0]
        p0_scr[...] = pp[0]
        a0_scr[...] = a[0:G]
        cin = dhcar[0:1, :]
        for g in range(G - 1, -1, -1):
            cin_scr[g:g + 1, :] = cin
            cin = a0_scr[g:g + 1, :] * (g00_scr[g:g + 1, :] + p0_scr[g:g + 1, :] * cin)
        dhcar[0:1, :] = cin
        cinv = cin_scr[...]
        dh = jnp.concatenate([g0[j] + pp[j] * cinv for j in range(SUBLANES - 1)] + [g0[SUBLANES - 1] + cinv], axis=0)

        hprev = jnp.concatenate([slab(hext, SUBLANES - 1 + j) for j in range(SUBLANES)], axis=0)
        da = dh * hprev
        ixc = ig * xc
        dm = dh * ixc
        dixc = dh * m
        di = dixc * xc
        dxc = dixc * ig
        dla = da * a - dm * (a * a) / m
        dlam_ref[...] += _colsum(dla * r) * (LRU_C * _sigmoid(nl))
        dr = dla * big_l
        dpa = dr * r * (1.0 - r)
        dpx = di * ig * (1.0 - ig)
        dba_ref[...] += _colsum(dpa)
        dbx_ref[...] += _colsum(dpx)
        dpab = dpa.astype(BF16)
        dpxb = dpx.astype(BF16)
        nt = (((1,), (1,)), ((), ()))
        tn = (((0,), (0,)), ((), ()))
        dxc_g = []
        for h in range(heads):
            sl = slice(h * hd, (h + 1) * hd)
            dxc_g.append(lax.dot_general(dpab[:, sl], wa_ref[h], nt, preferred_element_type=F32)
                         + lax.dot_general(dpxb[:, sl], wx_ref[h], nt, preferred_element_type=F32))
            dwa_ref[h] += lax.dot_general(xcb[:, sl], dpab[:, sl], tn, preferred_element_type=F32)
            dwx_ref[h] += lax.dot_general(xcb[:, sl], dpxb[:, sl], tn, preferred_element_type=F32)
        dxc = dxc + jnp.concatenate(dxc_g, axis=1)

        dbc_ref[...] += _colsum(dxc)
        xs = {st: slab(xext, st) for st in range(SUBLANES - 3, 2 * SUBLANES)}
        for k in range(4):
            xsh = jnp.concatenate([xs[SUBLANES + j - (3 - k)] for j in range(SUBLANES)], axis=0)
            dwc_ref[k:k + 1, :] += _colsum(dxc * xsh)
        for j in range(SUBLANES):
            put_slab(dxext, j, dxc[j * G:(j + 1) * G])
        us = {st: slab(dxext, st) for st in range(SUBLANES + 3)}
        for j in range(SUBLANES):
            acc = us[j] * wc_ref[3:4, :]
            for k in (1, 2, 3):
                acc = acc + us[j + k] * wc_ref[3 - k:4 - k, :]
            put_slab(dnat, j, acc)
        dxl = jnp.concatenate([dnat[c] for c in range(nc)], axis=1)
        dp_ref[:, 0:lw] = dxl.astype(BF16)
        dbin_ref[:, 0:lw] += _colsum(dxl)

        gu, dgu_dx = _gelu_and_grad(p_ref[:, 2 * lw:2 * lw + sw].astype(F32))
        gv, dgv_dx = _gelu_and_grad(p_ref[:, 2 * lw + sw:cw].astype(F32))
        xhat, rstd = _ln_stats(gv)
        vn = (xhat * lg_ref[...] + lb_ref[...]).astype(BF16)
        dys = dys_ref[...]
        dmixed = dys * gu
        dmb = dmixed.astype(BF16)
        tpos = lax.broadcasted_iota(jnp.int32, (SGU_BLOCK, SGU_BLOCK), 0) // CHUNK
        spos = lax.broadcasted_iota(jnp.int32, (SGU_BLOCK, SGU_BLOCK), 1) // CHUNK
        causal = spos <= tpos
        mixed_rows, dvn_rows = [], []
        for blk in range(nblk):
            rs = slice(blk * SGU_BLOCK, (blk + 1) * SGU_BLOCK)
            mcols, dcols = [], []
            for g in range(groups):
                cs = slice(g * gw, (g + 1) * gw)
                wm = jnp.where(causal, wsp_ref[g], 0.0).astype(BF16)
                mcols.append(jnp.dot(wm, vn[rs, cs], preferred_element_type=F32) + bsp_ref[:, g:g + 1])
                dcols.append(lax.dot_general(wm, dmb[rs, cs], tn, preferred_element_type=F32))
                dw = lax.dot_general(dmb[rs, cs], vn[rs, cs], nt, preferred_element_type=F32)
                dwsp_ref[g] += jnp.where(causal, dw, 0.0)
                dbsp_ref[:, g:g + 1] += jnp.sum(dmixed[rs, cs], axis=1, keepdims=True)
            mixed_rows.append(jnp.concatenate(mcols, axis=1))
            dvn_rows.append(jnp.concatenate(dcols, axis=1))
        mixed_all = jnp.concatenate(mixed_rows, axis=0) if nblk > 1 else mixed_rows[0]
        dvn = jnp.concatenate(dvn_rows, axis=0) if nblk > 1 else dvn_rows[0]
        du = dys * mixed_all * dgu_dx
        dlg_ref[...] += _colsum(dvn * xhat)
        dlb_ref[...] += _colsum(dvn)
        dv = _ln_bwd(dvn, xhat, rstd, lg_ref[...]) * dgv_dx
        dp_ref[:, 2 * lw:2 * lw + sw] = du.astype(BF16)
        dp_ref[:, 2 * lw + sw:cw] = dv.astype(BF16)
        dbin_ref[:, 2 * lw:2 * lw + sw] += _colsum(du)
        dbin_ref[:, 2 * lw + sw:cw] += _colsum(dv)

    rev = lambda s: n_s - 1 - s
    tile = lambda w: pl.BlockSpec((None, tm, w), lambda b, s: (b, rev(s), 0))
    halo = lambda w: pl.BlockSpec((None, SUBLANES, w), lambda b, s: (b, jnp.maximum(rev(s) * per8 - 1, 0), 0))
    xhalo = pl.BlockSpec((None, halo_rows, lw), lambda b, s: (b, jnp.maximum(rev(s) * (tm // halo_rows) - 1, 0), 0))
    full = lambda shp: pl.BlockSpec(shp, lambda b, s: (0,) * len(shp))
    small = [w_conv, b_conv, w_rg_a, b_rg_a, w_rg_x, b_rg_x, lam, w_sp, b_sp_t, ln_v_g, ln_v_b]
    acc_shapes = [(1, cw), w_conv.shape, b_conv.shape, w_rg_a.shape, b_rg_a.shape, w_rg_x.shape, b_rg_x.shape,
                  lam.shape, w_sp.shape, b_sp_t.shape, ln_v_g.shape, ln_v_b.shape]
    res = pl.pallas_call(
        body, name="mix_bwd", grid=(Bl, n_s),
        in_specs=[tile(cw), xhalo, tile(lw), halo(lw), tile(lw), tile(sw), pl.BlockSpec(memory_space=pl.ANY)]
                 + [tile(lw)] * 5 + [full(w.shape) for w in small],
        out_specs=tuple([tile(cw)] + [full(shp) for shp in acc_shapes]),
        out_shape=tuple([jax.ShapeDtypeStruct((Bl, S, din), BF16)] + [jax.ShapeDtypeStruct(shp, F32) for shp in acc_shapes]),
        input_output_aliases={6: 0},
        scratch_shapes=[pltpu.VMEM((nc, tm + SUBLANES, LANES), F32), pltpu.VMEM((nc, tm + SUBLANES, LANES), F32),
                        pltpu.VMEM((nc, tm, LANES), F32), pltpu.VMEM((nc, tm + SUBLANES, LANES), F32),
                        pltpu.VMEM((SUBLANES, lw), F32), pltpu.VMEM((G, lw), F32), pltpu.VMEM((G, lw), F32),
                        pltpu.VMEM((G, lw), F32), pltpu.VMEM((G, lw), F32)],
        compiler_params=_cparams(2, big=True),
    )(proj, proj, hs, hs, dya, dys, dproj, *saved, *small)
    return res


def _final_dx(dxp, dh, x, sc1, *, ts):
    Bl, S, D = x.shape

    def body(dxp_ref, dh_ref, x_ref, sc_ref, dx_ref, dsc_ref, dsh_ref):
        @pl.when(pl.program_id(1) == 0)
        def _():
            dsc_ref[...] = jnp.zeros_like(dsc_ref)
            dsh_ref[...] = jnp.zeros_like(dsh_ref)

        dh = dh_ref[...]
        dx_ref[...] = dxp_ref[...] + dh * (1.0 + sc_ref[...])
        dsc_ref[...] += _colsum(dh * x_ref[...])
        dsh_ref[...] += _colsum(dh)

    return pl.pallas_call(
        body, name="final_dx", grid=(Bl, S // ts),
        in_specs=[_tok_spec(ts, D), _tok_spec(ts, D), _tok_spec(ts, D), _brow_spec(D)],
        out_specs=(_tok_spec(ts, D), _brow_spec(D), _brow_spec(D)),
        out_shape=(jax.ShapeDtypeStruct((Bl, S, D), F32), jax.ShapeDtypeStruct((Bl, 1, D), F32),
                   jax.ShapeDtypeStruct((Bl, 1, D), F32)),
        compiler_params=_cparams(2),
    )(dxp, dh, x, sc1)


def _ada_fwd(c_all, w_ada):
    R, D = c_all.shape
    nb = w_ada.shape[1]

    def body(c_ref, w_ref, act_ref, o_ref):
        cv = c_ref[...]
        act = (cv * _sigmoid(cv)).astype(BF16)
        act_ref[...] = act
        o_ref[...] = jnp.dot(act, w_ref[...].astype(BF16), preferred_element_type=F32)

    return pl.pallas_call(
        body, name="ada_fwd",
        out_shape=(jax.ShapeDtypeStruct((R, D), BF16), jax.ShapeDtypeStruct((R, nb), F32)),
        compiler_params=pltpu.CompilerParams(vmem_limit_bytes=VMEM_LIMIT),
    )(c_all, w_ada)


def _ada_bwd(c_act, dmod_cols):
    R, D = c_act.shape
    nb = dmod_cols.shape[1]

    def body(act_ref, d_ref, o_ref, b_ref):
        o_ref[...] = lax.dot_general(act_ref[...], d_ref[...].astype(BF16), (((0,), (0,)), ((), ())),
                                     preferred_element_type=F32)
        b_ref[...] = _colsum(d_ref[...])

    return pl.pallas_call(
        body, name="ada_bwd", out_shape=(jax.ShapeDtypeStruct((D, nb), F32), jax.ShapeDtypeStruct((1, nb), F32)),
        compiler_params=pltpu.CompilerParams(vmem_limit_bytes=VMEM_LIMIT),
    )(c_act, dmod_cols)


def _adamw(w, g_slots, m, v, *, tr, name):
    R, C = w.shape
    n_slot = g_slots.shape[0]
    tr = min(tr, R)
    assert R % tr == 0, (name, R, tr)
    c1 = 1.0 / (1.0 - ADAM_B1 ** ADAM_STEP)
    c2 = 1.0 / (1.0 - ADAM_B2 ** ADAM_STEP)

    def body(w_ref, g_ref, m_ref, v_ref, go_ref, d_ref, mo_ref, vo_ref):
        g = g_ref[0].astype(F32)
        for i in range(1, n_slot):
            g = g + g_ref[i].astype(F32)
        mn = ADAM_B1 * m_ref[...] + (1.0 - ADAM_B1) * g
        vn = ADAM_B2 * v_ref[...] + (1.0 - ADAM_B2) * (g * g)
        go_ref[...] = g
        mo_ref[...] = mn
        vo_ref[...] = vn
        d_ref[...] = -ADAM_LR * ((mn * c1) / (jnp.sqrt(vn * c2) + ADAM_EPS) + ADAM_WD * w_ref[...])

    blk = pl.BlockSpec((tr, C), lambda i: (i, 0))
    return pl.pallas_call(
        body, name=name, grid=(R // tr,),
        in_specs=[blk, pl.BlockSpec((n_slot, tr, C), lambda i: (0, i, 0)), blk, blk],
        out_specs=(blk, blk, blk, blk),
        out_shape=tuple(jax.ShapeDtypeStruct((R, C), F32) for _ in range(4)),
        compiler_params=_cparams(1, big=True),
    )(w, g_slots, m, v)


def _adamw_many(ws, g_slots, g_owns, ms, vs, *, name):
    n = len(ws)
    c1 = 1.0 / (1.0 - ADAM_B1 ** ADAM_STEP)
    c2 = 1.0 / (1.0 - ADAM_B2 ** ADAM_STEP)

    def body(*refs):
        w_refs, g_refs, o_refs = refs[:n], refs[n:2 * n], refs[2 * n:3 * n]
        m_refs, v_refs = refs[3 * n:4 * n], refs[4 * n:5 * n]
        outs = refs[5 * n:]
        me = 4 * lax.axis_index("x") + 2 * lax.axis_index("y") + lax.axis_index("c")
        for i in range(n):
            own = o_refs[i][...]
            g = jnp.where(me == 0, own, g_refs[i][0])
            for d in range(1, N_DEV):
                g = g + jnp.where(me == d, own, g_refs[i][d])
            mn = ADAM_B1 * m_refs[i][...] + (1.0 - ADAM_B1) * g
            vn = ADAM_B2 * v_refs[i][...] + (1.0 - ADAM_B2) * (g * g)
            outs[i][...] = g
            outs[n + i][...] = -ADAM_LR * ((mn * c1) / (jnp.sqrt(vn * c2) + ADAM_EPS) + ADAM_WD * w_refs[i][...])
            outs[2 * n + i][...] = mn
            outs[3 * n + i][...] = vn

    res = pl.pallas_call(
        body, name=name, out_shape=tuple(jax.ShapeDtypeStruct(w.shape, F32) for _ in range(4) for w in ws),
        compiler_params=pltpu.CompilerParams(vmem_limit_bytes=VMEM_LIMIT),
    )(*ws, *g_slots, *g_owns, *ms, *vs)
    return res[:n], res[n:2 * n], res[2 * n:3 * n], res[3 * n:]


SMALL_NAMES = ("b_ada", "b_in", "b_conv", "w_rg_a", "b_rg_a", "w_rg_x", "b_rg_x", "lru_lambda", "w_sp", "b_sp",
               "ln_v_g", "ln_v_b", "ln1_g", "ln1_b", "ln2_g", "ln2_b")
BIG_NAMES = ("w_ada", "w_in", "w_conv", "w_o_lru", "w_o_sgu", "w_out", "w_up", "w_down")
WEIGHT_ORDER = ("w_ada", "b_ada", "w_in", "b_in", "w_conv", "b_conv", "w_rg_a", "b_rg_a", "w_rg_x", "b_rg_x",
                "lru_lambda", "w_sp", "b_sp", "ln_v_g", "ln_v_b", "w_o_lru", "w_o_sgu", "w_out", "ln1_g", "ln1_b",
                "w_up", "w_down", "ln2_g", "ln2_b")


def _pack_small(d):
    flat = jnp.concatenate([d[n].reshape(-1) for n in SMALL_NAMES])
    rows = -(-flat.shape[0] // LANES)
    rows = -(-rows // (N_DEV * SUBLANES)) * (N_DEV * SUBLANES)
    flat = jnp.pad(flat, (0, rows * LANES - flat.shape[0]))
    return flat.reshape(rows, LANES)


def _unpack_small(packed, like):
    flat = packed.reshape(-1)
    out, off = {}, 0
    for n in SMALL_NAMES:
        sz = like[n].size
        out[n] = flat[off:off + sz].reshape(like[n].shape)
        off += sz
    return out


def _blocked_cols(w2d):
    K, N = w2d.shape
    return jnp.transpose(w2d.reshape(K, N_DEV, N // N_DEV), (1, 0, 2))


def _unblock_cols(wb):
    n, K, nb = wb.shape
    return jnp.transpose(wb, (1, 0, 2)).reshape(K, n * nb)


def kernel(x, c, w_ada, b_ada, w_in, b_in, w_conv, b_conv, w_rg_a, b_rg_a, w_rg_x, b_rg_x, lru_lambda, w_sp, b_sp, ln_v_g, ln_v_b, w_o_lru, w_o_sgu, w_out, ln1_g, ln1_b, w_up, w_down, ln2_g, ln2_b, loss_target, m_w_ada, m_b_ada, m_w_in, m_b_in, m_w_conv, m_b_conv, m_w_rg_a, m_b_rg_a, m_w_rg_x, m_b_rg_x, m_lru_lambda, m_w_sp, m_b_sp, m_ln_v_g, m_ln_v_b, m_w_o_lru, m_w_o_sgu, m_w_out, m_ln1_g, m_ln1_b, m_w_up, m_w_down, m_ln2_g, m_ln2_b, v_w_ada, v_b_ada, v_w_in, v_b_in, v_w_conv, v_b_conv, v_w_rg_a, v_b_rg_a, v_w_rg_x, v_b_rg_x, v_lru_lambda, v_w_sp, v_b_sp, v_ln_v_g, v_ln_v_b, v_w_o_lru, v_w_o_sgu, v_w_out, v_ln1_g, v_ln1_b, v_w_up, v_w_down, v_ln2_g, v_ln2_b):
    W = dict(w_ada=w_ada, b_ada=b_ada, w_in=w_in, b_in=b_in, w_conv=w_conv, b_conv=b_conv, w_rg_a=w_rg_a,
             b_rg_a=b_rg_a, w_rg_x=w_rg_x, b_rg_x=b_rg_x, lru_lambda=lru_lambda, w_sp=w_sp, b_sp=b_sp,
             ln_v_g=ln_v_g, ln_v_b=ln_v_b, w_o_lru=w_o_lru, w_o_sgu=w_o_sgu, w_out=w_out, ln1_g=ln1_g, ln1_b=ln1_b,
             w_up=w_up, w_down=w_down, ln2_g=ln2_g, ln2_b=ln2_b)
    Mo = dict(w_ada=m_w_ada, b_ada=m_b_ada, w_in=m_w_in, b_in=m_b_in, w_conv=m_w_conv, b_conv=m_b_conv,
              w_rg_a=m_w_rg_a, b_rg_a=m_b_rg_a, w_rg_x=m_w_rg_x, b_rg_x=m_b_rg_x, lru_lambda=m_lru_lambda,
              w_sp=m_w_sp, b_sp=m_b_sp, ln_v_g=m_ln_v_g, ln_v_b=m_ln_v_b, w_o_lru=m_w_o_lru, w_o_sgu=m_w_o_sgu,
              w_out=m_w_out, ln1_g=m_ln1_g, ln1_b=m_ln1_b, w_up=m_w_up, w_down=m_w_down, ln2_g=m_ln2_g,
              ln2_b=m_ln2_b)
    Vo = dict(w_ada=v_w_ada, b_ada=v_b_ada, w_in=v_w_in, b_in=v_b_in, w_conv=v_w_conv, b_conv=v_b_conv,
              w_rg_a=v_w_rg_a, b_rg_a=v_b_rg_a, w_rg_x=v_w_rg_x, b_rg_x=v_b_rg_x, lru_lambda=v_lru_lambda,
              w_sp=v_w_sp, b_sp=v_b_sp, ln_v_g=v_ln_v_g, ln_v_b=v_ln_v_b, w_o_lru=v_w_o_lru, w_o_sgu=v_w_o_sgu,
              w_out=v_w_out, ln1_g=v_ln1_g, ln1_b=v_ln1_b, w_up=v_w_up, w_down=v_w_down, ln2_g=v_ln2_g,
              ln2_b=v_ln2_b)

    Bl, S, D = x.shape
    T = Bl * S
    lw = b_conv.shape[-1]
    sw = ln_v_g.shape[-1]
    din = b_in.shape[-1]
    dff = w_up.shape[-1] * N_DEV
    ts = min(512, S)
    tmix = min(256, S)
    trow = min(512, S)

    c_pad = jnp.pad(c, ((0, SUBLANES - Bl), (0, 0)))
    c_g, wconv_g = _exchange([c_pad, w_conv[0]], True, "xchg_c")
    wconv_full = _unblock_cols(wconv_g)
    c_act, modcols = _ada_fwd(c_g.reshape(N_DEV * SUBLANES, D), w_ada[0])
    (mod_slots,) = _exchange([modcols.reshape(N_DEV, SUBLANES, -1)], False, "xchg_mod")

    wnames = ("win", "wol", "wos", "wout", "wup", "wdown")
    shards = [w_in[0].astype(BF16), w_o_lru[0].astype(BF16), w_o_sgu[0].astype(BF16), w_out[0].astype(BF16),
              w_up[0].astype(BF16), w_down[0].astype(BF16)]
    col_sharded = [True, False, True, False, True, False]
    g_send, g_recv, g_src, g_land, g_tok = _xstart(shards, True, mod_slots, "gather_start", cols=col_sharded)
    gidx = {n: i for i, n in enumerate(wnames)}

    def gathered(n, after):
        i = gidx[n]
        return _xwait(g_src[i], g_land[i], g_send[i], g_recv[i], after, True, "gather_wait_" + n, col=col_sharded[i])

    mod = _unblock_cols(mod_slots)[:Bl] + (b_ada + g_tok[0, 0])
    sh1, sc1, gt1, sh2, sc2, gt2 = [mod[:, i * D:(i + 1) * D].reshape(Bl, 1, D) for i in range(6)]

    wa_b, wx_b = w_rg_a[0].astype(BF16), w_rg_x[0].astype(BF16)
    b_sp_t = jnp.transpose(b_sp[0])
    small_mix = (wconv_full, b_conv, wa_b, b_rg_a, wx_b, b_rg_x, lru_lambda, w_sp[0], b_sp_t, ln_v_g, ln_v_b)

    h = _modulate(x, sc1, sh1, ts)
    Win = gathered("win", h)
    proj = _mm(h.reshape(T, D), Win, mode="nn", tm=2048, tn=din // 4, tk=D, outs=[BF16],
               extras=[(b_in, "row")], epilogue=lambda acc, ex: (acc + ex[0],), name="mm_proj")
    proj3 = proj.reshape(Bl, S, din)
    hs, ya_pre, ysgu, *lru_saved = _mix_fwd(proj3, *small_mix, tm=tmix, lw=lw, sw=sw)
    Wol = gathered("wol", ya_pre).reshape(lw, D)
    Wos = gathered("wos", ysgu)
    y_a = _mm(ya_pre.reshape(T, lw), Wol, mode="nn", tm=2048, tn=D, tk=lw, outs=[BF16], name="mm_ya")
    x2d, tgt2d = x.reshape(T, D), loss_target.reshape(T, D)
    gate_cb = (din - 2 * D) // D

    def ep_merge(y_b, v):
        ya, ga, gb = [t.astype(F32) for t in v]
        yb = y_b.astype(BF16).astype(F32)
        return [yb, _sigmoid(ga) * ya + _sigmoid(gb) * yb]

    y_b, merged = _mm_rows(ysgu.reshape(T, sw), Wos, mode="nn", tm=trow, seq=S,
                           ins=[("tile", y_a), ("tilecol", proj, D, gate_cb), ("tilecol", proj, D, gate_cb + 1)],
                           outs=[("tile", BF16, D), ("tile", BF16, D)], epilogue=ep_merge, name="mm_yb_merge")
    Wout = gathered("wout", merged).reshape(D, D)

    def ep_ln1(mix_acc, v):
        x_, gt, g, b, sc, sh = v
        mixr = mix_acc.astype(BF16).astype(F32)
        xhat, _ = _ln_stats(ALPHA * x_ + (1.0 + gt) * mixr)
        x1_ = xhat * g + b
        return [mixr, x1_, x1_ * (1.0 + sc) + sh]

    mix, x1, h2 = _mm_rows(merged, Wout, mode="nn", tm=trow, seq=S,
                           ins=[("tile", x2d), ("brow", gt1), ("row", ln1_g), ("row", ln1_b), ("brow", sc2),
                                ("brow", sh2)],
                           outs=[("tile", BF16, D), ("tile", F32, D), ("tile", BF16, D)], epilogue=ep_ln1,
                           name="mm_mix_ln1")
    Wup = gathered("wup", h2)
    act = _mm(h2, Wup, mode="nn", tm=2048, tn=1024, tk=D, outs=[BF16],
              epilogue=lambda acc, ex: (jnp.square(jnp.maximum(acc, 0.0)),), name="mm_up")
    Wdown = gathered("wdown", act).reshape(dff, D)

    def ep_ln2(f_acc, v):
        x1_, t_, gt, g, b = v
        xhat, rstd = _ln_stats(ALPHA * x1_ + (1.0 + gt) * f_acc)
        err = xhat * g + b - t_
        loss_t = 0.5 * jnp.sum(jnp.mean(err * err, axis=-1, keepdims=True))
        dy = err * (1.0 / D)
        dz = _ln_bwd(dy, xhat, rstd, g)
        return [dz * (1.0 + gt), ALPHA * dz, _colsum(dz * f_acc), _colsum(dy * xhat), _colsum(dy), loss_t]

    df2, dx1p, dgt2, dg2, db2, loss_part = _mm_rows(
        act, Wdown, mode="nn", tm=trow, seq=S,
        ins=[("tile", x1), ("tile", tgt2d), ("brow", gt2), ("row", ln2_g), ("row", ln2_b)],
        outs=[("tile", BF16, D), ("tile", F32, D), ("acc_brow", D), ("acc_row", D), ("acc_row", D), ("acc_scalar",)],
        epilogue=ep_ln2, name="mm_down_ln2")
    loss = lax.psum(loss_part[0, 0], ("x", "y", "c"))

    def send_grads(parts, name):
        snd, rcv, src, land, tok = _xstart(parts, False, None, name + "_start")
        return [(src[i], land[i], snd[i], rcv[i]) for i in range(len(parts))], tok

    dup = _mm(df2, Wdown, mode="nt", tm=2048, tn=1024, tk=D, outs=[BF16], extras=[(act, "tile")],
              epilogue=lambda acc, ex: (acc * (2.0 * jnp.sqrt(ex[0].astype(F32))),), name="mm_dup")
    g_wdown = _mm(act, df2, mode="tn", tm=1024, tn=D, tk=2048, outs=[BF16], name="mm_gwdown")
    (x_wdown,), tok = send_grads([g_wdown.reshape(N_DEV, dff // N_DEV, D)], "gx_wdown")
    def ep_ln1_bwd(dh2, v):
        dx1p_, x1_, x_, mix_, sc, gt, g = v
        mixv = mix_.astype(F32)
        dx1 = dx1p_ + dh2 * (1.0 + sc)
        xhat, rstd = _ln_stats(ALPHA * x_ + (1.0 + gt) * mixv)
        dz = _ln_bwd(dx1, xhat, rstd, g)
        return [ALPHA * dz, dz * (1.0 + gt), _colsum(dh2 * x1_), _colsum(dh2), _colsum(dz * mixv),
                _colsum(dx1 * xhat), _colsum(dx1)]

    dxp, dmix, dsc2, dsh2, dgt1, dg1, db1 = _mm_rows(
        dup, Wup, mode="nt", tm=trow, seq=S, tok=tok,
        ins=[("tile", dx1p), ("tile", x1), ("tile", x2d), ("tile", mix), ("brow", sc2), ("brow", gt1), ("row", ln1_g)],
        outs=[("tile", F32, D), ("tile", BF16, D), ("acc_brow", D), ("acc_brow", D), ("acc_brow", D), ("acc_row", D),
              ("acc_row", D)],
        epilogue=ep_ln1_bwd, name="mm_dh2_ln1b")
    g_wup = _mm(h2, dup, mode="tn", tm=D, tn=1024, tk=2048, outs=[BF16], nb=dff // N_DEV, name="mm_gwup")
    (x_wup,), tok = send_grads([g_wup], "gx_wup")

    def ep_merge_bwd(dm, v):
        ya, yb, ga, gb = [t.astype(F32) for t in v]
        sa, sb = _sigmoid(ga), _sigmoid(gb)
        dg = jnp.concatenate([dm * ya * sa * (1.0 - sa), dm * yb * sb * (1.0 - sb)], axis=1)
        return [dm * sa, dm * sb, dg, _colsum(dg)]

    dy_a, dy_b, dproj, dbin_hi = _mm_rows(
        dmix, Wout, mode="nt", tm=trow, seq=S, tok=tok,
        ins=[("tile", y_a), ("tile", y_b), ("tilecol", proj, D, gate_cb), ("tilecol", proj, D, gate_cb + 1)],
        outs=[("tile", BF16, D), ("tile", BF16, D), ("tilecol", BF16, 2 * D, gate_cb // 2, din), ("acc_row", 2 * D)],
        epilogue=ep_merge_bwd, name="mm_dmerged_mb")
    g_wout = _mm(merged, dmix, mode="tn", tm=D, tn=D, tk=2048, outs=[BF16], name="mm_gwout")
    (x_wout,), tok = send_grads([g_wout.reshape(N_DEV, D // N_DEV, D)], "gx_wout")
    dya_pre = _mm(dy_a, Wol, mode="nt", tm=2048, tn=lw, tk=D, outs=[F32], tok=tok, name="mm_dya")
    dysgu = _mm(dy_b, Wos, mode="nt", tm=2048, tn=sw, tk=D, outs=[F32], name="mm_dys")
    g_wol = _mm(ya_pre.reshape(T, lw), dy_a, mode="tn", tm=lw, tn=D, tk=2048, outs=[BF16], name="mm_gwol")
    g_wos = _mm(ysgu.reshape(T, sw), dy_b, mode="tn", tm=sw, tn=D, tk=2048, outs=[BF16], nb=D // N_DEV,
                name="mm_gwos")
    (x_wol, x_wos), tok = send_grads([g_wol.reshape(N_DEV, lw // N_DEV, D), g_wos], "gx_wo")
    small_mix_b = (wconv_full, b_conv + tok[0, 0]) + small_mix[2:]
    (dproj, dbin_lo, g_wconv, g_bconv, g_wa, g_ba, g_wx, g_bx, g_lam, g_wsp, g_bsp_t, g_lvg, g_lvb) = _mix_bwd(
        proj3, hs, dya_pre.reshape(Bl, S, lw), dysgu.reshape(Bl, S, sw), dproj.reshape(Bl, S, din), lru_saved,
        *small_mix_b, tm=tmix, lw=lw, sw=sw)
    dproj2 = dproj.reshape(T, din)
    small_names = [n for n in SMALL_NAMES if n != "b_ada"]
    small_g = dict(b_in=jnp.concatenate([dbin_lo, dbin_hi], axis=-1), b_conv=g_bconv, w_rg_a=g_wa[None], b_rg_a=g_ba,
                   w_rg_x=g_wx[None], b_rg_x=g_bx, lru_lambda=g_lam, w_sp=g_wsp[None],
                   b_sp=jnp.transpose(g_bsp_t)[None], ln_v_g=g_lvg, ln_v_b=g_lvb, ln1_g=dg1, ln1_b=db1, ln2_g=dg2,
                   ln2_b=db2)
    gs_snd, gs_rcv, gs_src, gs_land, tok_s = _xstart([small_g[n] for n in small_names], True, None, "gsmall_start")
    g_win = _mm(h.reshape(T, D), dproj2, mode="tn", tm=D, tn=din // 4, tk=2048, outs=[BF16], nb=din // N_DEV,
                tok=tok_s, name="mm_gwin")
    (x_win,), tok = send_grads([g_win], "gx_win")

    def ep_final(dh, v):
        dxp_, x_, sc = v
        return [dxp_ + dh * (1.0 + sc), _colsum(dh * x_), _colsum(dh)]

    grad_x, dsc1, dsh1 = _mm_rows(dproj2, Win, mode="nt", tm=trow, seq=S, tok=tok,
                                  ins=[("tile", dxp), ("tile", x2d), ("brow", sc1)],
                                  outs=[("tile", F32, D), ("acc_brow", D), ("acc_brow", D)], epilogue=ep_final,
                                  name="mm_dh_final")
    grad_x = grad_x.reshape(Bl, S, D)

    dmod = jnp.concatenate([dsh1, dsc1, dgt1, dsh2, dsc2, dgt2], axis=-1).reshape(Bl, 6 * D)
    dmod_b = _blocked_cols(jnp.pad(dmod, ((0, SUBLANES - Bl), (0, 0))))
    dmod_s, gwconv_s = _exchange([dmod_b, _blocked_cols(g_wconv)], False, "xchg_dmod")
    g_wada, g_bada_mine = _ada_bwd(c_act, dmod_s.reshape(N_DEV * SUBLANES, -1))
    (g_bada_all,) = _exchange([g_bada_mine], True, "xchg_bada")

    gwdown_s = _xwait(*x_wdown, g_bada_all, False, "gx_wdown_wait")
    gwup_s = _xwait(*x_wup, g_bada_all, False, "gx_wup_wait")
    gwout_s = _xwait(*x_wout, g_bada_all, False, "gx_wout_wait")
    gwol_s = _xwait(*x_wol, g_bada_all, False, "gx_wol_wait")
    gwos_s = _xwait(*x_wos, g_bada_all, False, "gx_wos_wait")
    gwin_s = _xwait(*x_win, g_bada_all, False, "gx_win_wait")
    gs_own, gs_slots = _xwait_many(gs_src, gs_land, gs_snd, gs_rcv, g_bada_all, "gsmall_wait")
    out_g, out_d, out_m, out_v = {}, {}, {}, {}

    def adam(name, g_slots, tr):
        shp = W[name].shape
        w2, m2, v2 = [t.reshape(g_slots.shape[1:]) for t in (W[name], Mo[name], Vo[name])]
        g, d, mn, vn = _adamw(w2, g_slots, m2, v2, tr=tr, name="adam_" + name)
        out_g[name], out_d[name], out_m[name], out_v[name] = [t.reshape(shp) for t in (g, d, mn, vn)]

    adam("w_ada", g_wada[None], 256)
    adam("b_ada", g_bada_all.reshape(1, 1, 6 * D), 1)
    adam("w_in", gwin_s, 256)
    adam("w_conv", gwconv_s, 8)
    adam("w_o_lru", gwol_s, 160)
    adam("w_o_sgu", gwos_s, 256)
    adam("w_out", gwout_s, 128)
    adam("w_up", gwup_s, 256)
    adam("w_down", gwdown_s, 256)
    res_small = _adamw_many([W[n] for n in small_names], gs_slots, gs_own, [Mo[n] for n in small_names],
                            [Vo[n] for n in small_names], name="adam_small")
    for dst, vals in zip((out_g, out_d, out_m, out_v), res_small):
        dst.update(dict(zip(small_names, vals)))

    return (loss, grad_x, *[out_g[n] for n in WEIGHT_ORDER], *[out_d[n] for n in WEIGHT_ORDER],
            *[out_m[n] for n in WEIGHT_ORDER], *[out_v[n] for n in WEIGHT_ORDER])
```

```python
import functools
import math

import jax
import jax.numpy as jnp
from jax import lax
from jax.experimental import pallas as pl
from jax.experimental.pallas import tpu as pltpu

N_DEV = 8
LN_EPS = 1e-5
LRU_C = 8.0
CHUNK = 64
SGU_BLOCK = 128
ALPHA = 2.0 ** 0.25
ADAM_LR = 0.001
ADAM_B1 = 0.9
ADAM_B2 = 0.999
ADAM_EPS = 1e-08
ADAM_WD = 0.01
ADAM_STEP = 10
GELU_K0 = math.sqrt(2.0 / math.pi)
GELU_K1 = 0.044715

SUBLANES = 8
LANES = 128
VMEM_LIMIT = 56 * 1024 * 1024

F32 = jnp.float32
BF16 = jnp.bfloat16
MESH = pl.DeviceIdType.MESH


def _cparams(n_axes, big=False):
    return pltpu.CompilerParams(dimension_semantics=("arbitrary",) * n_axes,
                                vmem_limit_bytes=VMEM_LIMIT if big else None)


def _sigmoid(x):
    return 0.5 * jnp.tanh(0.5 * x) + 0.5


def _gelu(x):
    t = jnp.tanh(GELU_K0 * (x + GELU_K1 * (x * x * x)))
    return 0.5 * x * (1.0 + t)


def _gelu_and_grad(x):
    x2 = x * x
    t = jnp.tanh(GELU_K0 * (x + GELU_K1 * (x2 * x)))
    g = 0.5 * x * (1.0 + t)
    dg = 0.5 * (1.0 + t) + 0.5 * x * (1.0 - t * t) * (GELU_K0 * (1.0 + 3.0 * GELU_K1 * x2))
    return g, dg


def _expm1(x):
    p = x * (1.0 + x * (1.0 / 2.0 + x * (1.0 / 6.0 + x * (1.0 / 24.0 + x * (1.0 / 120.0)))))
    return jnp.where(jnp.abs(x) < 0.0625, p, jnp.exp(x) - 1.0)


def _log1p_pos(e):
    p = e * (1.0 - e * (1.0 / 2.0) + e * e * (1.0 / 3.0) - e * e * e * (1.0 / 4.0))
    return jnp.where(e < 1e-2, p, jnp.log(1.0 + e))


def _ln_stats(z):
    mu = jnp.mean(z, axis=-1, keepdims=True)
    zc = z - mu
    var = jnp.mean(zc * zc, axis=-1, keepdims=True)
    rstd = lax.rsqrt(var + LN_EPS)
    return zc * rstd, rstd


def _ln_bwd(dy, xhat, rstd, g):
    dxh = dy * g
    m1 = jnp.mean(dxh, axis=-1, keepdims=True)
    m2 = jnp.mean(dxh * xhat, axis=-1, keepdims=True)
    return rstd * (dxh - m1 - xhat * m2)


def _colsum(v):
    return jnp.sum(v, axis=0, keepdims=True)


def _first_step():
    return jnp.logical_and(pl.program_id(0) == 0, pl.program_id(1) == 0)


def _exchange(arrs, gather, name):
    n = len(arrs)
    n_peer = N_DEV - 1

    def body(*refs):
        ins, outs = refs[:n], refs[n:2 * n]
        send_sems, recv_sems, loc_sems = refs[2 * n:]
        x, y, c = lax.axis_index("x"), lax.axis_index("y"), lax.axis_index("c")
        me = 4 * x + 2 * y + c
        started = []
        for a in range(n):
            src_me = ins[a] if gather else ins[a].at[me]
            lc = pltpu.make_async_copy(src_me, outs[a].at[me], loc_sems.at[a])
            lc.start()
            started.append((lc, None))
        for p in range(1, N_DEV):
            px, py, pc = x ^ ((p >> 2) & 1), y ^ ((p >> 1) & 1), c ^ (p & 1)
            peer = 4 * px + 2 * py + pc
            for a in range(n):
                k = a * n_peer + (p - 1)
                src = ins[a] if gather else ins[a].at[peer]
                cp = pltpu.make_async_remote_copy(src_ref=src, dst_ref=outs[a].at[me],
                                                  send_sem=send_sems.at[k], recv_sem=recv_sems.at[k],
                                                  device_id=(px, py, pc), device_id_type=MESH)
                cp.start()
                rc = pltpu.make_async_remote_copy(src_ref=src, dst_ref=outs[a].at[peer],
                                                  send_sem=send_sems.at[k], recv_sem=recv_sems.at[k],
                                                  device_id=(px, py, pc), device_id_type=MESH)
                started.append((cp, rc))
        for cp, rc in started:
            if rc is None:
                cp.wait()
            else:
                cp.wait_send()
                rc.wait_recv()

    hbm = pl.BlockSpec(memory_space=pltpu.HBM)
    out_shape = tuple(
        jax.ShapeDtypeStruct(((N_DEV,) + a.shape) if gather else a.shape, a.dtype) for a in arrs)
    return pl.pallas_call(
        body, name=name, out_shape=out_shape,
        in_specs=[hbm] * n, out_specs=tuple([hbm] * n),
        scratch_shapes=[pltpu.SemaphoreType.DMA((n * n_peer,)), pltpu.SemaphoreType.DMA((n * n_peer,)),
                        pltpu.SemaphoreType.DMA((n,))],
        compiler_params=pltpu.CompilerParams(has_side_effects=True),
    )(*arrs)


_HBM = pl.BlockSpec(memory_space=pltpu.HBM)
_SEM = pl.BlockSpec(memory_space=pltpu.SEMAPHORE)
_EFFECT = pltpu.SideEffectType.DATAFLOW_SIDE_EFFECTING


def _peer_of(p):
    x, y, c = lax.axis_index("x"), lax.axis_index("y"), lax.axis_index("c")
    px, py, pc = x ^ ((p >> 2) & 1), y ^ ((p >> 1) & 1), c ^ (p & 1)
    return (px, py, pc), 4 * px + 2 * py + pc


def _slot(land_ref, idx, width):
    if width is None:
        return land_ref.at[idx]
    return land_ref.at[:, pl.ds(pl.multiple_of(idx * width, LANES), width)]


def _xstart(srcs, gather, after, name, cols=None, fill_own=True):
    n = len(srcs)
    cols = cols or [False] * n
    widths = [t.shape[1] if cols[a] else None for a, t in enumerate(srcs)]
    me_out = 4 * lax.axis_index("x") + 2 * lax.axis_index("y") + lax.axis_index("c")
    lands = []
    for a, t in enumerate(srcs):
        if cols[a]:
            zone, own, at = lax.empty((t.shape[0], N_DEV * t.shape[1]), t.dtype), t, (0, me_out * t.shape[1])
        elif gather:
            zone, own, at = lax.empty((N_DEV,) + t.shape, t.dtype), t[None], (me_out,) + (0,) * t.ndim
        else:
            zone, own = lax.empty(t.shape, t.dtype), lax.dynamic_index_in_dim(t, me_out, 0, keepdims=True)
            at = (me_out,) + (0,) * (t.ndim - 1)
        lands.append(lax.dynamic_update_slice(zone, own, at) if fill_own else zone)
    n_after = 0 if after is None else 1

    def body(*refs):
        src_refs, land_refs = refs[:n], refs[n:2 * n]
        refs = refs[n_after:]
        send_sems, recv_sems = refs[2 * n:3 * n], refs[3 * n:4 * n]
        token = refs[6 * n]
        me = 4 * lax.axis_index("x") + 2 * lax.axis_index("y") + lax.axis_index("c")
        for a in range(n):
            for p in range(1, N_DEV):
                dev, peer = _peer_of(p)
                pltpu.make_async_remote_copy(
                    src_ref=src_refs[a] if gather else src_refs[a].at[peer], dst_ref=_slot(land_refs[a], me, widths[a]),
                    send_sem=send_sems[a].at[p - 1], recv_sem=recv_sems[a].at[p - 1],
                    device_id=dev, device_id_type=MESH).start()
        token[...] = jnp.zeros_like(token)

    sems = tuple(pltpu.SemaphoreType.DMA((N_DEV - 1,)) for _ in range(2 * n))
    thru = tuple(pltpu.HBM(t.shape, t.dtype) for t in list(srcs) + list(lands))
    res = pl.pallas_call(
        body, name=name,
        out_shape=sems + thru + (jax.ShapeDtypeStruct((SUBLANES, LANES), F32),),
        in_specs=[_HBM] * (2 * n) + [pl.BlockSpec(memory_space=pl.ANY)] * n_after,
        out_specs=tuple([_SEM] * (2 * n) + [_HBM] * (2 * n) + [pl.BlockSpec(memory_space=pltpu.VMEM)]),
        input_output_aliases={i: 2 * n + i for i in range(2 * n)},
        compiler_params=pltpu.CompilerParams(has_side_effects=_EFFECT),
    )(*[pltpu.with_memory_space_constraint(t, pltpu.HBM) for t in list(srcs) + list(lands)],
      *([after] if n_after else []))
    return res[:n], res[n:2 * n], res[2 * n:3 * n], res[3 * n:4 * n], res[4 * n]


def _xwait(src, land, send_sem, recv_sem, after, gather, name, col=False):
    width = src.shape[1] if col else None

    def body(src_ref, land_ref, send_ref, recv_ref, after_ref, src_dead, land_out):
        del after_ref, src_dead, land_out
        for p in range(1, N_DEV):
            dev, peer = _peer_of(p)
            cp = pltpu.make_async_remote_copy(
                src_ref=src_ref if gather else src_ref.at[peer], dst_ref=_slot(land_ref, peer, width),
                send_sem=send_ref.at[p - 1], recv_sem=recv_ref.at[p - 1], device_id=dev, device_id_type=MESH)
            cp.wait_send()
            cp.wait_recv()

    src_done, landed = pl.pallas_call(
        body, name=name, out_shape=(pltpu.HBM(src.shape, src.dtype), pltpu.HBM(land.shape, land.dtype)),
        in_specs=[_HBM, _HBM, _SEM, _SEM, pl.BlockSpec(memory_space=pl.ANY)], out_specs=(_HBM, _HBM),
        input_output_aliases={0: 0, 1: 1},
        compiler_params=pltpu.CompilerParams(has_side_effects=_EFFECT),
    )(src, land, send_sem, recv_sem, after)
    del src_done
    return landed


def _xwait_many(srcs, lands, send_sems, recv_sems, after, name):
    n = len(srcs)

    def body(*refs):
        src_refs, land_refs = refs[:n], refs[n:2 * n]
        snd, rcv = refs[2 * n:3 * n], refs[3 * n:4 * n]
        for a in range(n):
            for p in range(1, N_DEV):
                dev, peer = _peer_of(p)
                cp = pltpu.make_async_remote_copy(
                    src_ref=src_refs[a], dst_ref=land_refs[a].at[peer], send_sem=snd[a].at[p - 1],
                    recv_sem=rcv[a].at[p - 1], device_id=dev, device_id_type=MESH)
                cp.wait_send()
                cp.wait_recv()

    res = pl.pallas_call(
        body, name=name, out_shape=tuple(pltpu.HBM(t.shape, t.dtype) for t in list(srcs) + list(lands)),
        in_specs=[_HBM] * (2 * n) + [_SEM] * (2 * n) + [pl.BlockSpec(memory_space=pl.ANY)],
        out_specs=tuple([_HBM] * (2 * n)), input_output_aliases={i: i for i in range(2 * n)},
        compiler_params=pltpu.CompilerParams(has_side_effects=_EFFECT),
    )(*srcs, *lands, *send_sems, *recv_sems, after)
    return res[:n], res[n:]


def _mm(a, b, *, mode, tm, tn, tk, outs, epilogue=None, extras=(), nb=None, tok=None, name):
    if mode == "nn":
        (M, K), (_, N) = a.shape, b.shape
    elif mode == "nt":
        (M, K), (N, _) = a.shape, b.shape
    else:
        (K, M), (_, N) = a.shape, b.shape
    tm, tn, tk = min(tm, M), min(tn, N), min(tk, K)
    assert M % tm == 0 and N % tn == 0 and K % tk == 0, (name, M, N, K, tm, tn, tk)
    if mode == "nn":
        a_spec = pl.BlockSpec((tm, tk), lambda i, j, k: (i, k))
        b_spec = pl.BlockSpec((tk, tn), lambda i, j, k: (k, j))
        dims = (((1,), (0,)), ((), ()))
    elif mode == "nt":
        a_spec = pl.BlockSpec((tm, tk), lambda i, j, k: (i, k))
        b_spec = pl.BlockSpec((tn, tk), lambda i, j, k: (j, k))
        dims = (((1,), (1,)), ((), ()))
    else:
        a_spec = pl.BlockSpec((tk, tm), lambda i, j, k: (k, i))
        b_spec = pl.BlockSpec((tk, tn), lambda i, j, k: (k, j))
        dims = (((0,), (0,)), ((), ()))
    nk = K // tk
    n_ex, n_out = len(extras), len(outs)
    n_tok = 0 if tok is None else 1
    nbytes = lambda d: jnp.dtype(d).itemsize
    vmem_est = (2 * (tm * tk * nbytes(a.dtype) + tk * tn * nbytes(b.dtype)
                     + sum(tm * tn * nbytes(e.dtype) for e, kind in extras if kind == "tile")
                     + sum(tm * tn * nbytes(d) for d in outs)) + tm * tn * 4)
    assert vmem_est <= VMEM_LIMIT, (name, vmem_est)
    if epilogue is None:
        epilogue = lambda acc, ex: tuple(acc.astype(d) for d in outs)

    def body(a_ref, b_ref, *refs):
        refs = refs[n_tok:]
        ex_refs, out_refs = refs[:n_ex], refs[n_ex:n_ex + n_out]

        def finish(acc):
            res = epilogue(acc, [r[...] for r in ex_refs])
            for o_ref, v in zip(out_refs, res):
                if nb is None:
                    o_ref[...] = v.astype(o_ref.dtype)
                else:
                    for q in range(tn // nb):
                        o_ref[q] = v[:, q * nb:(q + 1) * nb].astype(o_ref.dtype)

        part = lax.dot_general(a_ref[...], b_ref[...], dims, preferred_element_type=F32)
        if nk == 1:
            finish(part)
        else:
            acc_ref = refs[n_ex + n_out]
            k = pl.program_id(2)

            @pl.when(k == 0)
            def _():
                acc_ref[...] = part

            @pl.when(k > 0)
            def _():
                acc_ref[...] += part

            @pl.when(k == nk - 1)
            def _():
                finish(acc_ref[...])

    ex_specs = [pl.BlockSpec((tm, tn), lambda i, j, k: (i, j)) if kind == "tile"
                else pl.BlockSpec((1, tn), lambda i, j, k: (0, j)) for _, kind in extras]
    if nb is not None:
        assert tn % nb == 0, (name, tn, nb)
        o_spec = pl.BlockSpec((tn // nb, tm, nb), lambda i, j, k: (j, i, 0))
        o_shape = (N // nb, M, nb)
    else:
        o_spec = pl.BlockSpec((tm, tn), lambda i, j, k: (i, j))
        o_shape = (M, N)
    res = pl.pallas_call(
        body, name=name, grid=(M // tm, N // tn, nk),
        in_specs=[a_spec, b_spec] + [pl.BlockSpec((SUBLANES, LANES), lambda i, j, k: (0, 0))] * n_tok + ex_specs,
        out_specs=tuple([o_spec] * n_out),
        out_shape=tuple(jax.ShapeDtypeStruct(o_shape, d) for d in outs),
        scratch_shapes=[pltpu.VMEM((tm, tn), F32)] if nk > 1 else [],
        compiler_params=_cparams(3, big=True),
    )(a, b, *([tok] if n_tok else []), *[e for e, _ in extras])
    return res[0] if n_out == 1 else res


def _mm_rows(a, b, *, mode, tm, seq, ins, outs, epilogue, tok=None, name):
    M, K = a.shape
    N = b.shape[1] if mode == "nn" else b.shape[0]
    tm = min(tm, M)
    assert M % tm == 0 and seq % tm == 0, (name, M, seq, tm)
    tpb = seq // tm
    n_b = M // seq
    dims = (((1,), (0,)), ((), ())) if mode == "nn" else (((1,), (1,)), ((), ()))
    n_tok = 0 if tok is None else 1
    n_in, n_out = len(ins), len(outs)

    nt = M // tm
    ep = lambda i: jnp.maximum(i - 1, 0)
    in_specs, in_arrs = [], []
    for spec in ins:
        kind, arr = spec[0], spec[1]
        in_arrs.append(arr)
        if kind == "tile":
            in_specs.append(pl.BlockSpec((tm, arr.shape[1]), lambda i: (ep(i), 0)))
        elif kind == "tilecol":
            in_specs.append(pl.BlockSpec((tm, spec[2]), lambda i, cb=spec[3]: (ep(i), cb)))
        elif kind == "row":
            in_specs.append(pl.BlockSpec(arr.shape, lambda i: (0, 0)))
        else:
            in_specs.append(pl.BlockSpec((None, 1, arr.shape[2]), lambda i: (ep(i) // tpb, 0, 0)))
    out_specs, out_shapes = [], []
    for spec in outs:
        kind = spec[0]
        if kind == "tile":
            out_specs.append(pl.BlockSpec((tm, spec[2]), lambda i: (ep(i), 0)))
            out_shapes.append(jax.ShapeDtypeStruct((M, spec[2]), spec[1]))
        elif kind == "tilecol":
            out_specs.append(pl.BlockSpec((tm, spec[2]), lambda i, cb=spec[3]: (ep(i), cb)))
            out_shapes.append(jax.ShapeDtypeStruct((M, spec[4]), spec[1]))
        elif kind == "acc_row":
            out_specs.append(pl.BlockSpec((1, spec[1]), lambda i: (0, 0)))
            out_shapes.append(jax.ShapeDtypeStruct((1, spec[1]), F32))
        elif kind == "acc_brow":
            out_specs.append(pl.BlockSpec((None, 1, spec[1]), lambda i: (ep(i) // tpb, 0, 0)))
            out_shapes.append(jax.ShapeDtypeStruct((n_b, 1, spec[1]), F32))
        else:
            out_specs.append(pl.BlockSpec((SUBLANES, LANES), lambda i: (0, 0)))
            out_shapes.append(jax.ShapeDtypeStruct((SUBLANES, LANES), F32))

    def body(a_ref, b_ref, *refs):
        refs = refs[n_tok:]
        in_refs, out_refs = refs[:n_in], refs[n_in:n_in + n_out]
        prod_even, prod_odd = refs[n_in + n_out:]
        i = pl.program_id(0)
        e = i - 1

        @pl.when(i == 0)
        def _():
            prod_odd[...] = jnp.zeros_like(prod_odd)

        def run(dst, src):
            dst[...] = lax.dot_general(a_ref[...], b_ref[...], dims, preferred_element_type=F32)
            vals = epilogue(src[...], [r[...] for r in in_refs])
            for spec, o_ref, v in zip(outs, out_refs, vals):
                kind = spec[0]
                if kind in ("tile", "tilecol"):
                    o_ref[...] = v.astype(o_ref.dtype)
                else:
                    first = (e % tpb == 0) if kind == "acc_brow" else (e == 0)

                    @pl.when(jnp.logical_and(e >= 0, first))
                    def _(o_ref=o_ref, v=v):
                        o_ref[...] = jnp.broadcast_to(v, o_ref.shape)

                    @pl.when(jnp.logical_and(e >= 0, jnp.logical_not(first)))
                    def _(o_ref=o_ref, v=v):
                        o_ref[...] += v

        @pl.when(i % 2 == 0)
        def _():
            run(prod_even, prod_odd)

        @pl.when(i % 2 == 1)
        def _():
            run(prod_odd, prod_even)

    res = pl.pallas_call(
        body, name=name, grid=(nt + 1,),
        in_specs=[pl.BlockSpec((tm, K), lambda i: (jnp.minimum(i, nt - 1), 0)),
                  pl.BlockSpec(b.shape, lambda i: (0, 0), pipeline_mode=pl.Buffered(1))]
                 + [pl.BlockSpec((SUBLANES, LANES), lambda i: (0, 0))] * n_tok + in_specs,
        out_specs=tuple(out_specs), out_shape=tuple(out_shapes),
        scratch_shapes=[pltpu.VMEM((tm, N), F32), pltpu.VMEM((tm, N), F32)],
        compiler_params=_cparams(1, big=True),
    )(a, b, *([tok] if n_tok else []), *in_arrs)
    return res


def _tok_spec(ts, width, col_block=0):
    return pl.BlockSpec((None, ts, width), lambda b, s: (b, s, col_block))


def _brow_spec(width):
    return pl.BlockSpec((None, 1, width), lambda b, s: (b, 0, 0))


def _vec_spec(width):
    return pl.BlockSpec((1, width), lambda b, s: (0, 0))


def _modulate(x, sc, sh, ts):
    Bl, S, D = x.shape

    def body(x_ref, sc_ref, sh_ref, o_ref):
        o_ref[...] = (x_ref[...] * (1.0 + sc_ref[...]) + sh_ref[...]).astype(BF16)

    return pl.pallas_call(
        body, name="modulate", grid=(Bl, S // ts),
        in_specs=[_tok_spec(ts, D), _brow_spec(D), _brow_spec(D)],
        out_specs=_tok_spec(ts, D), out_shape=jax.ShapeDtypeStruct((Bl, S, D), BF16),
        compiler_params=_cparams(2),
    )(x, sc, sh)


def _mix_fwd(proj, w_conv, b_conv, w_rg_a, b_rg_a, w_rg_x, b_rg_x, lam, w_sp, b_sp_t, ln_v_g, ln_v_b, *, tm, lw, sw):
    Bl, S, _ = proj.shape
    heads, hd = w_rg_a.shape[0], w_rg_a.shape[1]
    groups = w_sp.shape[0]
    cw = 2 * lw + 2 * sw
    nblk = tm // SGU_BLOCK

    G = tm // SUBLANES
    nc = lw // LANES

    def body(p_ref, wc_ref, bc_ref, wa_ref, ba_ref, wx_ref, bx_ref, lam_ref, wsp_ref, bsp_ref, lg_ref, lb_ref,
             hs_ref, ya_ref, ys_ref, xc_ref, r_ref, ig_ref, a_ref, m_ref,
             xext, hnat, hcar, h7_scr, a7_scr, hp_scr):
        s = pl.program_id(1)

        @pl.when(s == 0)
        def _():
            xext[:, 0:SUBLANES, :] = jnp.zeros((nc, SUBLANES, LANES), F32)
            hcar[...] = jnp.zeros_like(hcar)

        @pl.when(s > 0)
        def _():
            xext[:, 0:SUBLANES, :] = xext[:, tm:tm + SUBLANES, :]

        for c in range(nc):
            xext[c, SUBLANES:SUBLANES + tm, :] = p_ref[:, c * LANES:(c + 1) * LANES].astype(F32)
        gl = p_ref[:, lw:2 * lw].astype(F32)

        def slab(ref3, start):
            return jnp.concatenate([ref3[c, pl.ds(start, G, stride=SUBLANES), :] for c in range(nc)], axis=1)

        xs = {st: slab(xext, st) for st in range(SUBLANES - 3, 2 * SUBLANES)}
        xc_slabs = []
        for j in range(SUBLANES):
            acc = bc_ref[...] + xs[SUBLANES + j] * wc_ref[3:4, :]
            for k in (1, 2, 3):
                acc = acc + xs[SUBLANES + j - k] * wc_ref[3 - k:4 - k, :]
            xc_slabs.append(acc)
        xc = jnp.concatenate(xc_slabs, axis=0)

        xcb = xc.astype(BF16)
        pa = jnp.concatenate([jnp.dot(xcb[:, h * hd:(h + 1) * hd], wa_ref[h], preferred_element_type=F32)
                              for h in range(heads)], axis=1) + ba_ref[...]
        px = jnp.concatenate([jnp.dot(xcb[:, h * hd:(h + 1) * hd], wx_ref[h], preferred_element_type=F32)
                              for h in range(heads)], axis=1) + bx_ref[...]
        r = _sigmoid(pa)
        ig = _sigmoid(px)
        nl = -lam_ref[...]
        big_l = -LRU_C * (jnp.maximum(nl, 0.0) + _log1p_pos(jnp.exp(-jnp.abs(nl))))
        la = big_l * r
        a = jnp.exp(la)
        m = jnp.sqrt(-_expm1(2.0 * la))
        bin_ = m * (ig * xc)
        xc_ref[...] = xc
        r_ref[...] = r
        ig_ref[...] = ig
        a_ref[...] = a
        m_ref[...] = m

        h0 = [bin_[0:G]]
        cp = [a[0:G]]
        for j in range(1, SUBLANES):
            aj = a[j * G:(j + 1) * G]
            h0.append(aj * h0[j - 1] + bin_[j * G:(j + 1) * G])
            cp.append(aj * cp[j - 1])
        h7_scr[...] = h0[SUBLANES - 1]
        a7_scr[...] = cp[SUBLANES - 1]
        carry = hcar[0:1, :]
        for g in range(G):
            hp_scr[g:g + 1, :] = carry
            carry = h7_scr[g:g + 1, :] + a7_scr[g:g + 1, :] * carry
        hcar[0:1, :] = carry
        hprev = hp_scr[...]
        for j in range(SUBLANES):
            hj = h0[j] + cp[j] * hprev
            for c in range(nc):
                hnat[c, pl.ds(j, G, stride=SUBLANES), :] = hj[:, c * LANES:(c + 1) * LANES]
        hs = jnp.concatenate([hnat[c] for c in range(nc)], axis=1)
        hs_ref[...] = hs
        ya_ref[...] = (hs * _gelu(gl)).astype(BF16)

        gu = _gelu(p_ref[:, 2 * lw:2 * lw + sw].astype(F32))
        gv = _gelu(p_ref[:, 2 * lw + sw:cw].astype(F32))
        xhat, _ = _ln_stats(gv)
        vn = (xhat * lg_ref[...] + lb_ref[...]).astype(BF16)
        tpos = lax.broadcasted_iota(jnp.int32, (SGU_BLOCK, SGU_BLOCK), 0) // CHUNK
        spos = lax.broadcasted_iota(jnp.int32, (SGU_BLOCK, SGU_BLOCK), 1) // CHUNK
        gw = sw // groups
        rows_out = []
        for blk in range(nblk):
            r0 = blk * SGU_BLOCK
            cols = []
            for g in range(groups):
                wm = jnp.where(spos <= tpos, wsp_ref[g], 0.0).astype(BF16)
                mixed = jnp.dot(wm, vn[r0:r0 + SGU_BLOCK, g * gw:(g + 1) * gw], preferred_element_type=F32)
                cols.append(mixed + bsp_ref[:, g:g + 1])
            rows_out.append(jnp.concatenate(cols, axis=1))
        mixed_all = jnp.concatenate(rows_out, axis=0) if nblk > 1 else rows_out[0]
        ys_ref[...] = (gu * mixed_all).astype(BF16)

    full = lambda shp: pl.BlockSpec(shp, lambda b, s: (0,) * len(shp))
    return pl.pallas_call(
        body, name="mix_fwd", grid=(Bl, S // tm),
        in_specs=[_tok_spec(tm, cw), full(w_conv.shape), full(b_conv.shape), full(w_rg_a.shape), full(b_rg_a.shape),
                  full(w_rg_x.shape), full(b_rg_x.shape), full(lam.shape), full(w_sp.shape), full(b_sp_t.shape),
                  full(ln_v_g.shape), full(ln_v_b.shape)],
        out_specs=(_tok_spec(tm, lw), _tok_spec(tm, lw), _tok_spec(tm, sw)) + (_tok_spec(tm, lw),) * 5,
        out_shape=(jax.ShapeDtypeStruct((Bl, S, lw), F32), jax.ShapeDtypeStruct((Bl, S, lw), BF16),
                   jax.ShapeDtypeStruct((Bl, S, sw), BF16)) + (jax.ShapeDtypeStruct((Bl, S, lw), F32),) * 5,
        scratch_shapes=[pltpu.VMEM((nc, tm + SUBLANES, LANES), F32), pltpu.VMEM((nc, tm, LANES), F32),
                        pltpu.VMEM((SUBLANES, lw), F32), pltpu.VMEM((G, lw), F32), pltpu.VMEM((G, lw), F32),
                        pltpu.VMEM((G, lw), F32)],
        compiler_params=_cparams(2, big=True),
    )(proj, w_conv, b_conv, w_rg_a, b_rg_a, w_rg_x, b_rg_x, lam, w_sp, b_sp_t, ln_v_g, ln_v_b)


def _merge_fwd(proj, y_a, y_b, *, ts, d):
    Bl, S, din = proj.shape
    gcol = (din - 2 * d) // (2 * d)
    assert gcol * 2 * d == din - 2 * d

    def body(g_ref, ya_ref, yb_ref, o_ref):
        sa = _sigmoid(g_ref[:, 0:d].astype(F32))
        sb = _sigmoid(g_ref[:, d:2 * d].astype(F32))
        o_ref[...] = (sa * ya_ref[...].astype(F32) + sb * yb_ref[...].astype(F32)).astype(BF16)

    return pl.pallas_call(
        body, name="merge_fwd", grid=(Bl, S // ts),
        in_specs=[_tok_spec(ts, 2 * d, gcol), _tok_spec(ts, d), _tok_spec(ts, d)],
        out_specs=_tok_spec(ts, d), out_shape=jax.ShapeDtypeStruct((Bl, S, d), BF16),
        compiler_params=_cparams(2),
    )(proj, y_a, y_b)


def _ln1_fwd(x, mix, gt1, g1, b1, sc2, sh2, *, ts):
    Bl, S, D = x.shape

    def body(x_ref, mix_ref, gt_ref, g_ref, b_ref, sc_ref, sh_ref, x1_ref, h2_ref):
        z = ALPHA * x_ref[...] + (1.0 + gt_ref[...]) * mix_ref[...].astype(F32)
        xhat, _ = _ln_stats(z)
        x1 = xhat * g_ref[...] + b_ref[...]
        x1_ref[...] = x1
        h2_ref[...] = (x1 * (1.0 + sc_ref[...]) + sh_ref[...]).astype(BF16)

    return pl.pallas_call(
        body, name="ln1_fwd", grid=(Bl, S // ts),
        in_specs=[_tok_spec(ts, D), _tok_spec(ts, D), _brow_spec(D), _vec_spec(D), _vec_spec(D), _brow_spec(D),
                  _brow_spec(D)],
        out_specs=(_tok_spec(ts, D), _tok_spec(ts, D)),
        out_shape=(jax.ShapeDtypeStruct((Bl, S, D), F32), jax.ShapeDtypeStruct((Bl, S, D), BF16)),
        compiler_params=_cparams(2),
    )(x, mix, gt1, g1, b1, sc2, sh2)


def _ln2_loss(x1, f, tgt, gt2, g2, b2, *, ts):
    Bl, S, D = x1.shape

    def body(x1_ref, f_ref, t_ref, gt_ref, g_ref, b_ref, df_ref, dx1_ref, dgt_ref, dg_ref, db_ref, loss_ref):
        s = pl.program_id(1)

        @pl.when(_first_step())
        def _():
            dg_ref[...] = jnp.zeros_like(dg_ref)
            db_ref[...] = jnp.zeros_like(db_ref)
            loss_ref[...] = jnp.zeros_like(loss_ref)

        @pl.when(s == 0)
        def _():
            dgt_ref[...] = jnp.zeros_like(dgt_ref)

        fv = f_ref[...]
        z = ALPHA * x1_ref[...] + (1.0 + gt_ref[...]) * fv
        xhat, rstd = _ln_stats(z)
        x2 = xhat * g_ref[...] + b_ref[...]
        err = x2 - t_ref[...]
        loss_ref[...] += 0.5 * jnp.sum(jnp.mean(err * err, axis=-1, keepdims=True))
        dy = err * (1.0 / D)
        dg_ref[...] += _colsum(dy * xhat)
        db_ref[...] += _colsum(dy)
        dz = _ln_bwd(dy, xhat, rstd, g_ref[...])
        dx1_ref[...] = ALPHA * dz
        dgt_ref[...] += _colsum(dz * fv)
        df_ref[...] = (dz * (1.0 + gt_ref[...])).astype(BF16)

    return pl.pallas_call(
        body, name="ln2_loss", grid=(Bl, S // ts),
        in_specs=[_tok_spec(ts, D), _tok_spec(ts, D), _tok_spec(ts, D), _brow_spec(D), _vec_spec(D), _vec_spec(D)],
        out_specs=(_tok_spec(ts, D), _tok_spec(ts, D), _brow_spec(D), _vec_spec(D), _vec_spec(D),
                   pl.BlockSpec((SUBLANES, LANES), lambda b, s: (0, 0))),
        out_shape=(jax.ShapeDtypeStruct((Bl, S, D), BF16), jax.ShapeDtypeStruct((Bl, S, D), F32),
                   jax.ShapeDtypeStruct((Bl, 1, D), F32), jax.ShapeDtypeStruct((1, D), F32),
                   jax.ShapeDtypeStruct((1, D), F32), jax.ShapeDtypeStruct((SUBLANES, LANES), F32)),
        compiler_params=_cparams(2),
    )(x1, f, tgt, gt2, g2, b2)


def _ln1_bwd(dx1p, dh2, x1, x, mix, sc2, gt1, g1, *, ts):
    Bl, S, D = x.shape

    def body(dx1p_ref, dh2_ref, x1_ref, x_ref, mix_ref, sc_ref, gt_ref, g_ref,
             dxp_ref, dmix_ref, dsc_ref, dsh_ref, dgt_ref, dg_ref, db_ref):
        s = pl.program_id(1)

        @pl.when(_first_step())
        def _():
            dg_ref[...] = jnp.zeros_like(dg_ref)
            db_ref[...] = jnp.zeros_like(db_ref)

        @pl.when(s == 0)
        def _():
            dsc_ref[...] = jnp.zeros_like(dsc_ref)
            dsh_ref[...] = jnp.zeros_like(dsh_ref)
            dgt_ref[...] = jnp.zeros_like(dgt_ref)

        dh2 = dh2_ref[...].astype(F32)
        mixv = mix_ref[...].astype(F32)
        dsc_ref[...] += _colsum(dh2 * x1_ref[...])
        dsh_ref[...] += _colsum(dh2)
        dx1 = dx1p_ref[...] + dh2 * (1.0 + sc_ref[...])
        z = ALPHA * x_ref[...] + (1.0 + gt_ref[...]) * mixv
        xhat, rstd = _ln_stats(z)
        dg_ref[...] += _colsum(dx1 * xhat)
        db_ref[...] += _colsum(dx1)
        dz = _ln_bwd(dx1, xhat, rstd, g_ref[...])
        dxp_ref[...] = ALPHA * dz
        dgt_ref[...] += _colsum(dz * mixv)
        dmix_ref[...] = (dz * (1.0 + gt_ref[...])).astype(BF16)

    return pl.pallas_call(
        body, name="ln1_bwd", grid=(Bl, S // ts),
        in_specs=[_tok_spec(ts, D)] * 5 + [_brow_spec(D), _brow_spec(D), _vec_spec(D)],
        out_specs=(_tok_spec(ts, D), _tok_spec(ts, D), _brow_spec(D), _brow_spec(D), _brow_spec(D), _vec_spec(D),
                   _vec_spec(D)),
        out_shape=(jax.ShapeDtypeStruct((Bl, S, D), F32), jax.ShapeDtypeStruct((Bl, S, D), BF16),
                   jax.ShapeDtypeStruct((Bl, 1, D), F32), jax.ShapeDtypeStruct((Bl, 1, D), F32),
                   jax.ShapeDtypeStruct((Bl, 1, D), F32), jax.ShapeDtypeStruct((1, D), F32),
                   jax.ShapeDtypeStruct((1, D), F32)),
        compiler_params=_cparams(2),
    )(dx1p, dh2, x1, x, mix, sc2, gt1, g1)


def _merge_bwd(dmerged, y_a, y_b, proj, *, ts, d):
    Bl, S, din = proj.shape
    gcol = (din - 2 * d) // (2 * d)

    def body(dm_ref, ya_ref, yb_ref, g_ref, dya_ref, dyb_ref, dp_ref, db_ref):
        @pl.when(_first_step())
        def _():
            db_ref[...] = jnp.zeros_like(db_ref)

        dm = dm_ref[...].astype(F32)
        sa = _sigmoid(g_ref[:, 0:d].astype(F32))
        sb = _sigmoid(g_ref[:, d:2 * d].astype(F32))
        dya_ref[...] = (dm * sa).astype(BF16)
        dyb_ref[...] = (dm * sb).astype(BF16)
        dga = dm * ya_ref[...].astype(F32) * sa * (1.0 - sa)
        dgb = dm * yb_ref[...].astype(F32) * sb * (1.0 - sb)
        dp_ref[:, 0:d] = dga.astype(BF16)
        dp_ref[:, d:2 * d] = dgb.astype(BF16)
        db_ref[:, 0:d] += _colsum(dga)
        db_ref[:, d:2 * d] += _colsum(dgb)

    return pl.pallas_call(
        body, name="merge_bwd", grid=(Bl, S // ts),
        in_specs=[_tok_spec(ts, d), _tok_spec(ts, d), _tok_spec(ts, d), _tok_spec(ts, 2 * d, gcol)],
        out_specs=(_tok_spec(ts, d), _tok_spec(ts, d), _tok_spec(ts, 2 * d, gcol), _vec_spec(2 * d)),
        out_shape=(jax.ShapeDtypeStruct((Bl, S, d), BF16), jax.ShapeDtypeStruct((Bl, S, d), BF16),
                   jax.ShapeDtypeStruct((Bl, S, din), BF16), jax.ShapeDtypeStruct((1, 2 * d), F32)),
        compiler_params=_cparams(2),
    )(dmerged, y_a, y_b, proj)


def _mix_bwd(proj, hs, dya, dys, dproj, saved, w_conv, b_conv, w_rg_a, b_rg_a, w_rg_x, b_rg_x, lam, w_sp, b_sp_t,
             ln_v_g, ln_v_b, *, tm, lw, sw):
    Bl, S, din = proj.shape
    heads, hd = w_rg_a.shape[0], w_rg_a.shape[1]
    groups = w_sp.shape[0]
    gw = sw // groups
    cw = 2 * lw + 2 * sw
    nblk = tm // SGU_BLOCK
    n_s = S // tm
    per8 = tm // SUBLANES
    halo_rows = 2 * SUBLANES

    G = tm // SUBLANES
    nc = lw // LANES

    def body(p_ref, xh_ref, hs_ref, hh_ref, dya_ref, dys_ref, dpin_ref, xc_ref, r_ref, ig_ref, a_ref, m_ref,
             wc_ref, bc_ref, wa_ref, ba_ref, wx_ref, bx_ref, lam_ref, wsp_ref, bsp_ref, lg_ref, lb_ref,
             dp_ref, dbin_ref, dwc_ref, dbc_ref, dwa_ref, dba_ref, dwx_ref, dbx_ref, dlam_ref, dwsp_ref, dbsp_ref,
             dlg_ref, dlb_ref,
             xext, hext, dnat, dxext, dhcar, g00_scr, p0_scr, a0_scr, cin_scr):
        del dpin_ref
        sr = pl.program_id(1)
        first_tile = sr == n_s - 1

        @pl.when(_first_step())
        def _():
            for ref in (dbin_ref, dwc_ref, dbc_ref, dwa_ref, dba_ref, dwx_ref, dbx_ref, dlam_ref, dwsp_ref, dbsp_ref,
                        dlg_ref, dlb_ref):
                ref[...] = jnp.zeros_like(ref)

        @pl.when(sr == 0)
        def _():
            dhcar[...] = jnp.zeros_like(dhcar)
            dxext[:, tm:tm + SUBLANES, :] = jnp.zeros((nc, SUBLANES, LANES), F32)

        @pl.when(sr > 0)
        def _():
            dxext[:, tm:tm + SUBLANES, :] = dxext[:, 0:SUBLANES, :]

        def slab(ref3, start):
            return jnp.concatenate([ref3[c, pl.ds(start, G, stride=SUBLANES), :] for c in range(nc)], axis=1)

        def put_slab(ref3, j, val):
            for c in range(nc):
                ref3[c, pl.ds(j, G, stride=SUBLANES), :] = val[:, c * LANES:(c + 1) * LANES]

        keep = jnp.where(first_tile, 0.0, 1.0)
        xprev = xh_ref[...].astype(F32)[halo_rows - SUBLANES:halo_rows] * keep
        hsv = hs_ref[...]
        hprev8 = hh_ref[...] * keep
        for c in range(nc):
            cs = slice(c * LANES, (c + 1) * LANES)
            xext[c, 0:SUBLANES, :] = xprev[:, cs]
            xext[c, SUBLANES:SUBLANES + tm, :] = p_ref[:, cs].astype(F32)
            hext[c, 0:SUBLANES, :] = hprev8[:, cs]
            hext[c, SUBLANES:SUBLANES + tm, :] = hsv[:, cs]
        gl = p_ref[:, lw:2 * lw].astype(F32)
        ggl, dggl = _gelu_and_grad(gl)
        dyav = dya_ref[...]
        dhs = dyav * ggl
        dgl = dyav * hsv * dggl
        dp_ref[:, lw:2 * lw] = dgl.astype(BF16)
        dbin_ref[:, lw:2 * lw] += _colsum(dgl)
        for c in range(nc):
            dnat[c] = dhs[:, c * LANES:(c + 1) * LANES]

        xc, r, ig, a, m = xc_ref[...], r_ref[...], ig_ref[...], a_ref[...], m_ref[...]
        xcb = xc.astype(BF16)
        nl = -lam_ref[...]
        big_l = -LRU_C * (jnp.maximum(nl, 0.0) + _log1p_pos(jnp.exp(-jnp.abs(nl))))

        g0 = [None] * SUBLANES
        pp = [None] * SUBLANES
        g0[SUBLANES - 1] = slab(dnat, SUBLANES - 1)
        for j in range(SUBLANES - 2, -1, -1):
            an = a[(j + 1) * G:(j + 2) * G]
            g0[j] = slab(dnat, j) + an * g0[j + 1]
            pp[j] = an if j == SUBLANES - 2 else an * pp[j + 1]
        g00_scr[...] = g0[0]
        p0_scr[...] = pp[0]
        a0_scr[...] = a[0:G]
        cin = dhcar[0:1, :]
        for g in range(G - 1, -1, -1):
            cin_scr[g:g + 1, :] = cin
            cin = a0_scr[g:g + 1, :] * (g00_scr[g:g + 1, :] + p0_scr[g:g + 1, :] * cin)
        dhcar[0:1, :] = cin
        cinv = cin_scr[...]
        dh = jnp.concatenate([g0[j] + pp[j] * cinv for j in range(SUBLANES - 1)] + [g0[SUBLANES - 1] + cinv], axis=0)

        hprev = jnp.concatenate([slab(hext, SUBLANES - 1 + j) for j in range(SUBLANES)], axis=0)
        da = dh * hprev
        ixc = ig * xc
        dm = dh * ixc
        dixc = dh * m
        di = dixc * xc
        dxc = dixc * ig
        dla = da * a - dm * (a * a) / m
        dlam_ref[...] += _colsum(dla * r) * (LRU_C * _sigmoid(nl))
        dr = dla * big_l
        dpa = dr * r * (1.0 - r)
        dpx = di * ig * (1.0 - ig)
        dba_ref[...] += _colsum(dpa)
        dbx_ref[...] += _colsum(dpx)
        dpab = dpa.astype(BF16)
        dpxb = dpx.astype(BF16)
        nt = (((1,), (1,)), ((), ()))
        tn = (((0,), (0,)), ((), ()))
        dxc_g = []
        for h in range(heads):
            sl = slice(h * hd, (h + 1) * hd)
            dxc_g.append(lax.dot_general(dpab[:, sl], wa_ref[h], nt, preferred_element_type=F32)
                         + lax.dot_general(dpxb[:, sl], wx_ref[h], nt, preferred_element_type=F32))
            dwa_ref[h] += lax.dot_general(xcb[:, sl], dpab[:, sl], tn, preferred_element_type=F32)
            dwx_ref[h] += lax.dot_general(xcb[:, sl], dpxb[:, sl], tn, preferred_element_type=F32)
        dxc = dxc + jnp.concatenate(dxc_g, axis=1)

        dbc_ref[...] += _colsum(dxc)
        xs = {st: slab(xext, st) for st in range(SUBLANES - 3, 2 * SUBLANES)}
        for k in range(4):
            xsh = jnp.concatenate([xs[SUBLANES + j - (3 - k)] for j in range(SUBLANES)], axis=0)
            dwc_ref[k:k + 1, :] += _colsum(dxc * xsh)
        for j in range(SUBLANES):
            put_slab(dxext, j, dxc[j * G:(j + 1) * G])
        us = {st: slab(dxext, st) for st in range(SUBLANES + 3)}
        for j in range(SUBLANES):
            acc = us[j] * wc_ref[3:4, :]
            for k in (1, 2, 3):
                acc = acc + us[j + k] * wc_ref[3 - k:4 - k, :]
            put_slab(dnat, j, acc)
        dxl = jnp.concatenate([dnat[c] for c in range(nc)], axis=1)
        dp_ref[:, 0:lw] = dxl.astype(BF16)
        dbin_ref[:, 0:lw] += _colsum(dxl)

        gu, dgu_dx = _gelu_and_grad(p_ref[:, 2 * lw:2 * lw + sw].astype(F32))
        gv, dgv_dx = _gelu_and_grad(p_ref[:, 2 * lw + sw:cw].astype(F32))
        xhat, rstd = _ln_stats(gv)
        vn = (xhat * lg_ref[...] + lb_ref[...]).astype(BF16)
        dys = dys_ref[...]
        dmixed = dys * gu
        dmb = dmixed.astype(BF16)
        tpos = lax.broadcasted_iota(jnp.int32, (SGU_BLOCK, SGU_BLOCK), 0) // CHUNK
        spos = lax.broadcasted_iota(jnp.int32, (SGU_BLOCK, SGU_BLOCK), 1) // CHUNK
        causal = spos <= tpos
        mixed_rows, dvn_rows = [], []
        for blk in range(nblk):
            rs = slice(blk * SGU_BLOCK, (blk + 1) * SGU_BLOCK)
            mcols, dcols = [], []
            for g in range(groups):
                cs = slice(g * gw, (g + 1) * gw)
                wm = jnp.where(causal, wsp_ref[g], 0.0).astype(BF16)
                mcols.append(jnp.dot(wm, vn[rs, cs], preferred_element_type=F32) + bsp_ref[:, g:g + 1])
                dcols.append(lax.dot_general(wm, dmb[rs, cs], tn, preferred_element_type=F32))
                dw = lax.dot_general(dmb[rs, cs], vn[rs, cs], nt, preferred_element_type=F32)
                dwsp_ref[g] += jnp.where(causal, dw, 0.0)
                dbsp_ref[:, g:g + 1] += jnp.sum(dmixed[rs, cs], axis=1, keepdims=True)
            mixed_rows.append(jnp.concatenate(mcols, axis=1))
            dvn_rows.append(jnp.concatenate(dcols, axis=1))
        mixed_all = jnp.concatenate(mixed_rows, axis=0) if nblk > 1 else mixed_rows[0]
        dvn = jnp.concatenate(dvn_rows, axis=0) if nblk > 1 else dvn_rows[0]
        du = dys * mixed_all * dgu_dx
        dlg_ref[...] += _colsum(dvn * xhat)
        dlb_ref[...] += _colsum(dvn)
        dv = _ln_bwd(dvn, xhat, rstd, lg_ref[...]) * dgv_dx
        dp_ref[:, 2 * lw:2 * lw + sw] = du.astype(BF16)
        dp_ref[:, 2 * lw + sw:cw] = dv.astype(BF16)
        dbin_ref[:, 2 * lw:2 * lw + sw] += _colsum(du)
        dbin_ref[:, 2 * lw + sw:cw] += _colsum(dv)

    rev = lambda s: n_s - 1 - s
    tile = lambda w: pl.BlockSpec((None, tm, w), lambda b, s: (b, rev(s), 0))
    halo = lambda w: pl.BlockSpec((None, SUBLANES, w), lambda b, s: (b, jnp.maximum(rev(s) * per8 - 1, 0), 0))
    xhalo = pl.BlockSpec((None, halo_rows, lw), lambda b, s: (b, jnp.maximum(rev(s) * (tm // halo_rows) - 1, 0), 0))
    full = lambda shp: pl.BlockSpec(shp, lambda b, s: (0,) * len(shp))
    small = [w_conv, b_conv, w_rg_a, b_rg_a, w_rg_x, b_rg_x, lam, w_sp, b_sp_t, ln_v_g, ln_v_b]
    acc_shapes = [(1, cw), w_conv.shape, b_conv.shape, w_rg_a.shape, b_rg_a.shape, w_rg_x.shape, b_rg_x.shape,
                  lam.shape, w_sp.shape, b_sp_t.shape, ln_v_g.shape, ln_v_b.shape]
    res = pl.pallas_call(
        body, name="mix_bwd", grid=(Bl, n_s),
        in_specs=[tile(cw), xhalo, tile(lw), halo(lw), tile(lw), tile(sw), pl.BlockSpec(memory_space=pl.ANY)]
                 + [tile(lw)] * 5 + [full(w.shape) for w in small],
        out_specs=tuple([tile(cw)] + [full(shp) for shp in acc_shapes]),
        out_shape=tuple([jax.ShapeDtypeStruct((Bl, S, din), BF16)] + [jax.ShapeDtypeStruct(shp, F32) for shp in acc_shapes]),
        input_output_aliases={6: 0},
        scratch_shapes=[pltpu.VMEM((nc, tm + SUBLANES, LANES), F32), pltpu.VMEM((nc, tm + SUBLANES, LANES), F32),
                        pltpu.VMEM((nc, tm, LANES), F32), pltpu.VMEM((nc, tm + SUBLANES, LANES), F32),
                        pltpu.VMEM((SUBLANES, lw), F32), pltpu.VMEM((G, lw), F32), pltpu.VMEM((G, lw), F32),
                        pltpu.VMEM((G, lw), F32), pltpu.VMEM((G, lw), F32)],
        compiler_params=_cparams(2, big=True),
    )(proj, proj, hs, hs, dya, dys, dproj, *saved, *small)
    return res


def _final_dx(dxp, dh, x, sc1, *, ts):
    Bl, S, D = x.shape

    def body(dxp_ref, dh_ref, x_ref, sc_ref, dx_ref, dsc_ref, dsh_ref):
        @pl.when(pl.program_id(1) == 0)
        def _():
            dsc_ref[...] = jnp.zeros_like(dsc_ref)
            dsh_ref[...] = jnp.zeros_like(dsh_ref)

        dh = dh_ref[...]
        dx_ref[...] = dxp_ref[...] + dh * (1.0 + sc_ref[...])
        dsc_ref[...] += _colsum(dh * x_ref[...])
        dsh_ref[...] += _colsum(dh)

    return pl.pallas_call(
        body, name="final_dx", grid=(Bl, S // ts),
        in_specs=[_tok_spec(ts, D), _tok_spec(ts, D), _tok_spec(ts, D), _brow_spec(D)],
        out_specs=(_tok_spec(ts, D), _brow_spec(D), _brow_spec(D)),
        out_shape=(jax.ShapeDtypeStruct((Bl, S, D), F32), jax.ShapeDtypeStruct((Bl, 1, D), F32),
                   jax.ShapeDtypeStruct((Bl, 1, D), F32)),
        compiler_params=_cparams(2),
    )(dxp, dh, x, sc1)


def _ada_fwd(c_all, w_ada):
    R, D = c_all.shape
    nb = w_ada.shape[1]

    def body(c_ref, w_ref, act_ref, o_ref):
        cv = c_ref[...]
        act = (cv * _sigmoid(cv)).astype(BF16)
        act_ref[...] = act
        o_ref[...] = jnp.dot(act, w_ref[...].astype(BF16), preferred_element_type=F32)

    return pl.pallas_call(
        body, name="ada_fwd",
        out_shape=(jax.ShapeDtypeStruct((R, D), BF16), jax.ShapeDtypeStruct((R, nb), F32)),
        compiler_params=pltpu.CompilerParams(vmem_limit_bytes=VMEM_LIMIT),
    )(c_all, w_ada)


def _ada_bwd(c_act, dmod_cols):
    R, D = c_act.shape
    nb = dmod_cols.shape[1]

    def body(act_ref, d_ref, o_ref, b_ref):
        o_ref[...] = lax.dot_general(act_ref[...], d_ref[...].astype(BF16), (((0,), (0,)), ((), ())),
                                     preferred_element_type=F32)
        b_ref[...] = _colsum(d_ref[...])

    return pl.pallas_call(
        body, name="ada_bwd", out_shape=(jax.ShapeDtypeStruct((D, nb), F32), jax.ShapeDtypeStruct((1, nb), F32)),
        compiler_params=pltpu.CompilerParams(vmem_limit_bytes=VMEM_LIMIT),
    )(c_act, dmod_cols)


def _adamw(w, g_slots, m, v, *, tr, name):
    R, C = w.shape
    n_slot = g_slots.shape[0]
    tr = min(tr, R)
    assert R % tr == 0, (name, R, tr)
    c1 = 1.0 / (1.0 - ADAM_B1 ** ADAM_STEP)
    c2 = 1.0 / (1.0 - ADAM_B2 ** ADAM_STEP)

    def body(w_ref, g_ref, m_ref, v_ref, go_ref, d_ref, mo_ref, vo_ref):
        g = g_ref[0].astype(F32)
        for i in range(1, n_slot):
            g = g + g_ref[i].astype(F32)
        mn = ADAM_B1 * m_ref[...] + (1.0 - ADAM_B1) * g
        vn = ADAM_B2 * v_ref[...] + (1.0 - ADAM_B2) * (g * g)
        go_ref[...] = g
        mo_ref[...] = mn
        vo_ref[...] = vn
        d_ref[...] = -ADAM_LR * ((mn * c1) / (jnp.sqrt(vn * c2) + ADAM_EPS) + ADAM_WD * w_ref[...])

    blk = pl.BlockSpec((tr, C), lambda i: (i, 0))
    return pl.pallas_call(
        body, name=name, grid=(R // tr,),
        in_specs=[blk, pl.BlockSpec((n_slot, tr, C), lambda i: (0, i, 0)), blk, blk],
        out_specs=(blk, blk, blk, blk),
        out_shape=tuple(jax.ShapeDtypeStruct((R, C), F32) for _ in range(4)),
        compiler_params=_cparams(1, big=True),
    )(w, g_slots, m, v)


def _adamw_many(ws, g_slots, g_owns, ms, vs, *, name):
    n = len(ws)
    c1 = 1.0 / (1.0 - ADAM_B1 ** ADAM_STEP)
    c2 = 1.0 / (1.0 - ADAM_B2 ** ADAM_STEP)

    def body(*refs):
        w_refs, g_refs, o_refs = refs[:n], refs[n:2 * n], refs[2 * n:3 * n]
        m_refs, v_refs = refs[3 * n:4 * n], refs[4 * n:5 * n]
        outs = refs[5 * n:]
        me = 4 * lax.axis_index("x") + 2 * lax.axis_index("y") + lax.axis_index("c")
        for i in range(n):
            own = o_refs[i][...]
            g = jnp.where(me == 0, own, g_refs[i][0])
            for d in range(1, N_DEV):
                g = g + jnp.where(me == d, own, g_refs[i][d])
            mn = ADAM_B1 * m_refs[i][...] + (1.0 - ADAM_B1) * g
            vn = ADAM_B2 * v_refs[i][...] + (1.0 - ADAM_B2) * (g * g)
            outs[i][...] = g
            outs[n + i][...] = -ADAM_LR * ((mn * c1) / (jnp.sqrt(vn * c2) + ADAM_EPS) + ADAM_WD * w_refs[i][...])
            outs[2 * n + i][...] = mn
            outs[3 * n + i][...] = vn

    res = pl.pallas_call(
        body, name=name, out_shape=tuple(jax.ShapeDtypeStruct(w.shape, F32) for _ in range(4) for w in ws),
        compiler_params=pltpu.CompilerParams(vmem_limit_bytes=VMEM_LIMIT),
    )(*ws, *g_slots, *g_owns, *ms, *vs)
    return res[:n], res[n:2 * n], res[2 * n:3 * n], res[3 * n:]


SMALL_NAMES = ("b_ada", "b_in", "b_conv", "w_rg_a", "b_rg_a", "w_rg_x", "b_rg_x", "lru_lambda", "w_sp", "b_sp",
               "ln_v_g", "ln_v_b", "ln1_g", "ln1_b", "ln2_g", "ln2_b")
BIG_NAMES = ("w_ada", "w_in", "w_conv", "w_o_lru", "w_o_sgu", "w_out", "w_up", "w_down")
WEIGHT_ORDER = ("w_ada", "b_ada", "w_in", "b_in", "w_conv", "b_conv", "w_rg_a", "b_rg_a", "w_rg_x", "b_rg_x",
                "lru_lambda", "w_sp", "b_sp", "ln_v_g", "ln_v_b", "w_o_lru", "w_o_sgu", "w_out", "ln1_g", "ln1_b",
                "w_up", "w_down", "ln2_g", "ln2_b")


def _pack_small(d):
    flat = jnp.concatenate([d[n].reshape(-1) for n in SMALL_NAMES])
    rows = -(-flat.shape[0] // LANES)
    rows = -(-rows // (N_DEV * SUBLANES)) * (N_DEV * SUBLANES)
    flat = jnp.pad(flat, (0, rows * LANES - flat.shape[0]))
    return flat.reshape(rows, LANES)


def _unpack_small(packed, like):
    flat = packed.reshape(-1)
    out, off = {}, 0
    for n in SMALL_NAMES:
        sz = like[n].size
        out[n] = flat[off:off + sz].reshape(like[n].shape)
        off += sz
    return out


def _blocked_cols(w2d):
    K, N = w2d.shape
    return jnp.transpose(w2d.reshape(K, N_DEV, N // N_DEV), (1, 0, 2))


def _unblock_cols(wb):
    n, K, nb = wb.shape
    return jnp.transpose(wb, (1, 0, 2)).reshape(K, n * nb)


def kernel(x, c, w_ada, b_ada, w_in, b_in, w_conv, b_conv, w_rg_a, b_rg_a, w_rg_x, b_rg_x, lru_lambda, w_sp, b_sp, ln_v_g, ln_v_b, w_o_lru, w_o_sgu, w_out, ln1_g, ln1_b, w_up, w_down, ln2_g, ln2_b, loss_target, m_w_ada, m_b_ada, m_w_in, m_b_in, m_w_conv, m_b_conv, m_w_rg_a, m_b_rg_a, m_w_rg_x, m_b_rg_x, m_lru_lambda, m_w_sp, m_b_sp, m_ln_v_g, m_ln_v_b, m_w_o_lru, m_w_o_sgu, m_w_out, m_ln1_g, m_ln1_b, m_w_up, m_w_down, m_ln2_g, m_ln2_b, v_w_ada, v_b_ada, v_w_in, v_b_in, v_w_conv, v_b_conv, v_w_rg_a, v_b_rg_a, v_w_rg_x, v_b_rg_x, v_lru_lambda, v_w_sp, v_b_sp, v_ln_v_g, v_ln_v_b, v_w_o_lru, v_w_o_sgu, v_w_out, v_ln1_g, v_ln1_b, v_w_up, v_w_down, v_ln2_g, v_ln2_b):
    W = dict(w_ada=w_ada, b_ada=b_ada, w_in=w_in, b_in=b_in, w_conv=w_conv, b_conv=b_conv, w_rg_a=w_rg_a,
             b_rg_a=b_rg_a, w_rg_x=w_rg_x, b_rg_x=b_rg_x, lru_lambda=lru_lambda, w_sp=w_sp, b_sp=b_sp,
             ln_v_g=ln_v_g, ln_v_b=ln_v_b, w_o_lru=w_o_lru, w_o_sgu=w_o_sgu, w_out=w_out, ln1_g=ln1_g, ln1_b=ln1_b,
             w_up=w_up, w_down=w_down, ln2_g=ln2_g, ln2_b=ln2_b)
    Mo = dict(w_ada=m_w_ada, b_ada=m_b_ada, w_in=m_w_in, b_in=m_b_in, w_conv=m_w_conv, b_conv=m_b_conv,
              w_rg_a=m_w_rg_a, b_rg_a=m_b_rg_a, w_rg_x=m_w_rg_x, b_rg_x=m_b_rg_x, lru_lambda=m_lru_lambda,
              w_sp=m_w_sp, b_sp=m_b_sp, ln_v_g=m_ln_v_g, ln_v_b=m_ln_v_b, w_o_lru=m_w_o_lru, w_o_sgu=m_w_o_sgu,
              w_out=m_w_out, ln1_g=m_ln1_g, ln1_b=m_ln1_b, w_up=m_w_up, w_down=m_w_down, ln2_g=m_ln2_g,
              ln2_b=m_ln2_b)
    Vo = dict(w_ada=v_w_ada, b_ada=v_b_ada, w_in=v_w_in, b_in=v_b_in, w_conv=v_w_conv, b_conv=v_b_conv,
              w_rg_a=v_w_rg_a, b_rg_a=v_b_rg_a, w_rg_x=v_w_rg_x, b_rg_x=v_b_rg_x, lru_lambda=v_lru_lambda,
              w_sp=v_w_sp, b_sp=v_b_sp, ln_v_g=v_ln_v_g, ln_v_b=v_ln_v_b, w_o_lru=v_w_o_lru, w_o_sgu=v_w_o_sgu,
              w_out=v_w_out, ln1_g=v_ln1_g, ln1_b=v_ln1_b, w_up=v_w_up, w_down=v_w_down, ln2_g=v_ln2_g,
              ln2_b=v_ln2_b)

    Bl, S, D = x.shape
    T = Bl * S
    lw = b_conv.shape[-1]
    sw = ln_v_g.shape[-1]
    din = b_in.shape[-1]
    dff = w_up.shape[-1] * N_DEV
    ts = min(512, S)
    tmix = min(256, S)
    trow = min(512, S)

    c_pad = jnp.pad(c, ((0, SUBLANES - Bl), (0, 0)))
    c_g, wconv_g = _exchange([c_pad, w_conv[0]], True, "xchg_c")
    wconv_full = _unblock_cols(wconv_g)
    c_act, modcols = _ada_fwd(c_g.reshape(N_DEV * SUBLANES, D), w_ada[0])
    (mod_slots,) = _exchange([modcols.reshape(N_DEV, SUBLANES, -1)], False, "xchg_mod")

    wnames = ("win", "wol", "wos", "wout", "wup", "wdown")
    shards = [w_in[0].astype(BF16), w_o_lru[0].astype(BF16), w_o_sgu[0].astype(BF16), w_out[0].astype(BF16),
              w_up[0].astype(BF16), w_down[0].astype(BF16)]
    col_sharded = [True, False, True, False, True, False]
    g_send, g_recv, g_src, g_land, g_tok = _xstart(shards, True, mod_slots, "gather_start", cols=col_sharded)
    gidx = {n: i for i, n in enumerate(wnames)}

    def gathered(n, after):
        i = gidx[n]
        return _xwait(g_src[i], g_land[i], g_send[i], g_recv[i], after, True, "gather_wait_" + n, col=col_sharded[i])

    mod = _unblock_cols(mod_slots)[:Bl] + (b_ada + g_tok[0, 0])
    sh1, sc1, gt1, sh2, sc2, gt2 = [mod[:, i * D:(i + 1) * D].reshape(Bl, 1, D) for i in range(6)]

    wa_b, wx_b = w_rg_a[0].astype(BF16), w_rg_x[0].astype(BF16)
    b_sp_t = jnp.transpose(b_sp[0])
    small_mix = (wconv_full, b_conv, wa_b, b_rg_a, wx_b, b_rg_x, lru_lambda, w_sp[0], b_sp_t, ln_v_g, ln_v_b)

    h = _modulate(x, sc1, sh1, ts)
    Win = gathered("win", h)
    proj = _mm(h.reshape(T, D), Win, mode="nn", tm=2048, tn=din // 4, tk=D, outs=[BF16],
               extras=[(b_in, "row")], epilogue=lambda acc, ex: (acc + ex[0],), name="mm_proj")
    proj3 = proj.reshape(Bl, S, din)
    hs, ya_pre, ysgu, *lru_saved = _mix_fwd(proj3, *small_mix, tm=tmix, lw=lw, sw=sw)
    Wol = gathered("wol", ya_pre).reshape(lw, D)
    Wos = gathered("wos", ysgu)
    y_a = _mm(ya_pre.reshape(T, lw), Wol, mode="nn", tm=2048, tn=D, tk=lw, outs=[BF16], name="mm_ya")
    x2d, tgt2d = x.reshape(T, D), loss_target.reshape(T, D)
    gate_cb = (din - 2 * D) // D

    def ep_merge(y_b, v):
        ya, ga, gb = [t.astype(F32) for t in v]
        yb = y_b.astype(BF16).astype(F32)
        return [yb, _sigmoid(ga) * ya + _sigmoid(gb) * yb]

    y_b, merged = _mm_rows(ysgu.reshape(T, sw), Wos, mode="nn", tm=trow, seq=S,
                           ins=[("tile", y_a), ("tilecol", proj, D, gate_cb), ("tilecol", proj, D, gate_cb + 1)],
                           outs=[("tile", BF16, D), ("tile", BF16, D)], epilogue=ep_merge, name="mm_yb_merge")
    Wout = gathered("wout", merged).reshape(D, D)

    def ep_ln1(mix_acc, v):
        x_, gt, g, b, sc, sh = v
        mixr = mix_acc.astype(BF16).astype(F32)
        xhat, _ = _ln_stats(ALPHA * x_ + (1.0 + gt) * mixr)
        x1_ = xhat * g + b
        return [mixr, x1_, x1_ * (1.0 + sc) + sh]

    mix, x1, h2 = _mm_rows(merged, Wout, mode="nn", tm=trow, seq=S,
                           ins=[("tile", x2d), ("brow", gt1), ("row", ln1_g), ("row", ln1_b), ("brow", sc2),
                                ("brow", sh2)],
                           outs=[("tile", BF16, D), ("tile", F32, D), ("tile", BF16, D)], epilogue=ep_ln1,
                           name="mm_mix_ln1")
    Wup = gathered("wup", h2)
    act = _mm(h2, Wup, mode="nn", tm=2048, tn=1024, tk=D, outs=[BF16],
              epilogue=lambda acc, ex: (jnp.square(jnp.maximum(acc, 0.0)),), name="mm_up")
    Wdown = gathered("wdown", act).reshape(dff, D)

    def ep_ln2(f_acc, v):
        x1_, t_, gt, g, b = v
        xhat, rstd = _ln_stats(ALPHA * x1_ + (1.0 + gt) * f_acc)
        err = xhat * g + b - t_
        loss_t = 0.5 * jnp.sum(jnp.mean(err * err, axis=-1, keepdims=True))
        dy = err * (1.0 / D)
        dz = _ln_bwd(dy, xhat, rstd, g)
        return [dz * (1.0 + gt), ALPHA * dz, _colsum(dz * f_acc), _colsum(dy * xhat), _colsum(dy), loss_t]

    df2, dx1p, dgt2, dg2, db2, loss_part = _mm_rows(
        act, Wdown, mode="nn", tm=trow, seq=S,
        ins=[("tile", x1), ("tile", tgt2d), ("brow", gt2), ("row", ln2_g), ("row", ln2_b)],
        outs=[("tile", BF16, D), ("tile", F32, D), ("acc_brow", D), ("acc_row", D), ("acc_row", D), ("acc_scalar",)],
        epilogue=ep_ln2, name="mm_down_ln2")
    loss = lax.psum(loss_part[0, 0], ("x", "y", "c"))

    def send_grads(parts, name):
        snd, rcv, src, land, tok = _xstart(parts, False, None, name + "_start")
        return [(src[i], land[i], snd[i], rcv[i]) for i in range(len(parts))], tok

    dup = _mm(df2, Wdown, mode="nt", tm=2048, tn=1024, tk=D, outs=[BF16], extras=[(act, "tile")],
              epilogue=lambda acc, ex: (acc * (2.0 * jnp.sqrt(ex[0].astype(F32))),), name="mm_dup")
    g_wdown = _mm(act, df2, mode="tn", tm=1024, tn=D, tk=2048, outs=[BF16], name="mm_gwdown")
    (x_wdown,), tok = send_grads([g_wdown.reshape(N_DEV, dff // N_DEV, D)], "gx_wdown")
    def ep_ln1_bwd(dh2, v):
        dx1p_, x1_, x_, mix_, sc, gt, g = v
        mixv = mix_.astype(F32)
        dx1 = dx1p_ + dh2 * (1.0 + sc)
        xhat, rstd = _ln_stats(ALPHA * x_ + (1.0 + gt) * mixv)
        dz = _ln_bwd(dx1, xhat, rstd, g)
        return [ALPHA * dz, dz * (1.0 + gt), _colsum(dh2 * x1_), _colsum(dh2), _colsum(dz * mixv),
                _colsum(dx1 * xhat), _colsum(dx1)]

    dxp, dmix, dsc2, dsh2, dgt1, dg1, db1 = _mm_rows(
        dup, Wup, mode="nt", tm=trow, seq=S, tok=tok,
        ins=[("tile", dx1p), ("tile", x1), ("tile", x2d), ("tile", mix), ("brow", sc2), ("brow", gt1), ("row", ln1_g)],
        outs=[("tile", F32, D), ("tile", BF16, D), ("acc_brow", D), ("acc_brow", D), ("acc_brow", D), ("acc_row", D),
              ("acc_row", D)],
        epilogue=ep_ln1_bwd, name="mm_dh2_ln1b")
    g_wup = _mm(h2, dup, mode="tn", tm=D, tn=1024, tk=2048, outs=[BF16], nb=dff // N_DEV, name="mm_gwup")
    (x_wup,), tok = send_grads([g_wup], "gx_wup")

    def ep_merge_bwd(dm, v):
        ya, yb, ga, gb = [t.astype(F32) for t in v]
        sa, sb = _sigmoid(ga), _sigmoid(gb)
        dg = jnp.concatenate([dm * ya * sa * (1.0 - sa), dm * yb * sb * (1.0 - sb)], axis=1)
        return [dm * sa, dm * sb, dg, _colsum(dg)]

    dy_a, dy_b, dproj, dbin_hi = _mm_rows(
        dmix, Wout, mode="nt", tm=trow, seq=S, tok=tok,
        ins=[("tile", y_a), ("tile", y_b), ("tilecol", proj, D, gate_cb), ("tilecol", proj, D, gate_cb + 1)],
        outs=[("tile", BF16, D), ("tile", BF16, D), ("tilecol", BF16, 2 * D, gate_cb // 2, din), ("acc_row", 2 * D)],
        epilogue=ep_merge_bwd, name="mm_dmerged_mb")
    g_wout = _mm(merged, dmix, mode="tn", tm=D, tn=D, tk=2048, outs=[BF16], name="mm_gwout")
    (x_wout,), tok = send_grads([g_wout.reshape(N_DEV, D // N_DEV, D)], "gx_wout")
    dya_pre = _mm(dy_a, Wol, mode="nt", tm=2048, tn=lw, tk=D, outs=[F32], tok=tok, name="mm_dya")
    dysgu = _mm(dy_b, Wos, mode="nt", tm=2048, tn=sw, tk=D, outs=[F32], name="mm_dys")
    g_wol = _mm(ya_pre.reshape(T, lw), dy_a, mode="tn", tm=lw, tn=D, tk=2048, outs=[BF16], name="mm_gwol")
    g_wos = _mm(ysgu.reshape(T, sw), dy_b, mode="tn", tm=sw, tn=D, tk=2048, outs=[BF16], nb=D // N_DEV,
                name="mm_gwos")
    (x_wol, x_wos), tok = send_grads([g_wol.reshape(N_DEV, lw // N_DEV, D), g_wos], "gx_wo")
    small_mix_b = (wconv_full, b_conv + tok[0, 0]) + small_mix[2:]
    (dproj, dbin_lo, g_wconv, g_bconv, g_wa, g_ba, g_wx, g_bx, g_lam, g_wsp, g_bsp_t, g_lvg, g_lvb) = _mix_bwd(
        proj3, hs, dya_pre.reshape(Bl, S, lw), dysgu.reshape(Bl, S, sw), dproj.reshape(Bl, S, din), lru_saved,
        *small_mix_b, tm=tmix, lw=lw, sw=sw)
    dproj2 = dproj.reshape(T, din)
    small_names = [n for n in SMALL_NAMES if n != "b_ada"]
    small_g = dict(b_in=jnp.concatenate([dbin_lo, dbin_hi], axis=-1), b_conv=g_bconv, w_rg_a=g_wa[None], b_rg_a=g_ba,
                   w_rg_x=g_wx[None], b_rg_x=g_bx, lru_lambda=g_lam, w_sp=g_wsp[None],
                   b_sp=jnp.transpose(g_bsp_t)[None], ln_v_g=g_lvg, ln_v_b=g_lvb, ln1_g=dg1, ln1_b=db1, ln2_g=dg2,
                   ln2_b=db2)
    gs_snd, gs_rcv, gs_src, gs_land, tok_s = _xstart([small_g[n] for n in small_names], True, None, "gsmall_start",
                                                      fill_own=False)
    g_win = _mm(h.reshape(T, D), dproj2, mode="tn", tm=D, tn=din // 4, tk=2048, outs=[BF16], nb=din // N_DEV,
                tok=tok_s, name="mm_gwin")
    (x_win,), tok = send_grads([g_win], "gx_win")

    def ep_final(dh, v):
        dxp_, x_, sc = v
        return [dxp_ + dh * (1.0 + sc), _colsum(dh * x_), _colsum(dh)]

    grad_x, dsc1, dsh1 = _mm_rows(dproj2, Win, mode="nt", tm=trow, seq=S, tok=tok,
                                  ins=[("tile", dxp), ("tile", x2d), ("brow", sc1)],
                                  outs=[("tile", F32, D), ("acc_brow", D), ("acc_brow", D)], epilogue=ep_final,
                                  name="mm_dh_final")
    grad_x = grad_x.reshape(Bl, S, D)

    dmod = jnp.concatenate([dsh1, dsc1, dgt1, dsh2, dsc2, dgt2], axis=-1).reshape(Bl, 6 * D)
    dmod_b = _blocked_cols(jnp.pad(dmod, ((0, SUBLANES - Bl), (0, 0))))
    dmod_s, gwconv_s = _exchange([dmod_b, _blocked_cols(g_wconv)], False, "xchg_dmod")
    g_wada, g_bada_mine = _ada_bwd(c_act, dmod_s.reshape(N_DEV * SUBLANES, -1))
    (g_bada_all,) = _exchange([g_bada_mine], True, "xchg_bada")

    gwdown_s = _xwait(*x_wdown, g_bada_all, False, "gx_wdown_wait")
    gwup_s = _xwait(*x_wup, g_bada_all, False, "gx_wup_wait")
    gwout_s = _xwait(*x_wout, g_bada_all, False, "gx_wout_wait")
    gwol_s = _xwait(*x_wol, g_bada_all, False, "gx_wol_wait")
    gwos_s = _xwait(*x_wos, g_bada_all, False, "gx_wos_wait")
    gwin_s = _xwait(*x_win, g_bada_all, False, "gx_win_wait")
    gs_own, gs_slots = _xwait_many(gs_src, gs_land, gs_snd, gs_rcv, g_bada_all, "gsmall_wait")
    out_g, out_d, out_m, out_v = {}, {}, {}, {}

    def adam(name, g_slots, tr):
        shp = W[name].shape
        w2, m2, v2 = [t.reshape(g_slots.shape[1:]) for t in (W[name], Mo[name], Vo[name])]
        g, d, mn, vn = _adamw(w2, g_slots, m2, v2, tr=tr, name="adam_" + name)
        out_g[name], out_d[name], out_m[name], out_v[name] = [t.reshape(shp) for t in (g, d, mn, vn)]

    adam("w_ada", g_wada[None], 256)
    adam("b_ada", g_bada_all.reshape(1, 1, 6 * D), 1)
    adam("w_in", gwin_s, 256)
    adam("w_conv", gwconv_s, 8)
    adam("w_o_lru", gwol_s, 160)
    adam("w_o_sgu", gwos_s, 256)
    adam("w_out", gwout_s, 128)
    adam("w_up", gwup_s, 256)
    adam("w_down", gwdown_s, 256)
    res_small = _adamw_many([W[n] for n in small_names], gs_slots, gs_own, [Mo[n] for n in small_names],
                            [Vo[n] for n in small_names], name="adam_small")
    for dst, vals in zip((out_g, out_d, out_m, out_v), res_small):
        dst.update(dict(zip(small_names, vals)))

    return (loss, grad_x, *[out_g[n] for n in WEIGHT_ORDER], *[out_d[n] for n in WEIGHT_ORDER],
            *[out_m[n] for n in WEIGHT_ORDER], *[out_v[n] for n in WEIGHT_ORDER])
```

```python
import functools
import math

import jax
import jax.numpy as jnp
from jax import lax
from jax.experimental import pallas as pl
from jax.experimental.pallas import tpu as pltpu

N_DEV = 8
LN_EPS = 1e-5
LRU_C = 8.0
CHUNK = 64
SGU_BLOCK = 128
ALPHA = 2.0 ** 0.25
ADAM_LR = 0.001
ADAM_B1 = 0.9
ADAM_B2 = 0.999
ADAM_EPS = 1e-08
ADAM_WD = 0.01
ADAM_STEP = 10
GELU_K0 = math.sqrt(2.0 / math.pi)
GELU_K1 = 0.044715

SUBLANES = 8
LANES = 128
VMEM_LIMIT = 56 * 1024 * 1024
WIN_PARTS = 3

F32 = jnp.float32
BF16 = jnp.bfloat16
MESH = pl.DeviceIdType.MESH


def _cparams(n_axes, big=False):
    return pltpu.CompilerParams(dimension_semantics=("arbitrary",) * n_axes,
                                vmem_limit_bytes=VMEM_LIMIT if big else None)


def _sigmoid(x):
    return 0.5 * jnp.tanh(0.5 * x) + 0.5


def _gelu(x):
    t = jnp.tanh(GELU_K0 * (x + GELU_K1 * (x * x * x)))
    return 0.5 * x * (1.0 + t)


def _gelu_and_grad(x):
    x2 = x * x
    t = jnp.tanh(GELU_K0 * (x + GELU_K1 * (x2 * x)))
    g = 0.5 * x * (1.0 + t)
    dg = 0.5 * (1.0 + t) + 0.5 * x * (1.0 - t * t) * (GELU_K0 * (1.0 + 3.0 * GELU_K1 * x2))
    return g, dg


def _expm1(x):
    p = x * (1.0 + x * (1.0 / 2.0 + x * (1.0 / 6.0 + x * (1.0 / 24.0 + x * (1.0 / 120.0)))))
    return jnp.where(jnp.abs(x) < 0.0625, p, jnp.exp(x) - 1.0)


def _log1p_pos(e):
    p = e * (1.0 - e * (1.0 / 2.0) + e * e * (1.0 / 3.0) - e * e * e * (1.0 / 4.0))
    return jnp.where(e < 1e-2, p, jnp.log(1.0 + e))


def _ln_stats(z):
    mu = jnp.mean(z, axis=-1, keepdims=True)
    zc = z - mu
    var = jnp.mean(zc * zc, axis=-1, keepdims=True)
    rstd = lax.rsqrt(var + LN_EPS)
    return zc * rstd, rstd


def _ln_bwd(dy, xhat, rstd, g):
    dxh = dy * g
    m1 = jnp.mean(dxh, axis=-1, keepdims=True)
    m2 = jnp.mean(dxh * xhat, axis=-1, keepdims=True)
    return rstd * (dxh - m1 - xhat * m2)


def _colsum(v):
    return jnp.sum(v, axis=0, keepdims=True)


def _first_step():
    return jnp.logical_and(pl.program_id(0) == 0, pl.program_id(1) == 0)


def _exchange(arrs, gather, name):
    n = len(arrs)
    n_peer = N_DEV - 1

    def body(*refs):
        ins, outs = refs[:n], refs[n:2 * n]
        send_sems, recv_sems, loc_sems = refs[2 * n:]
        x, y, c = lax.axis_index("x"), lax.axis_index("y"), lax.axis_index("c")
        me = 4 * x + 2 * y + c
        started = []
        for a in range(n):
            src_me = ins[a] if gather else ins[a].at[me]
            lc = pltpu.make_async_copy(src_me, outs[a].at[me], loc_sems.at[a])
            lc.start()
            started.append((lc, None))
        for p in range(1, N_DEV):
            px, py, pc = x ^ ((p >> 2) & 1), y ^ ((p >> 1) & 1), c ^ (p & 1)
            peer = 4 * px + 2 * py + pc
            for a in range(n):
                k = a * n_peer + (p - 1)
                src = ins[a] if gather else ins[a].at[peer]
                cp = pltpu.make_async_remote_copy(src_ref=src, dst_ref=outs[a].at[me],
                                                  send_sem=send_sems.at[k], recv_sem=recv_sems.at[k],
                                                  device_id=(px, py, pc), device_id_type=MESH)
                cp.start()
                rc = pltpu.make_async_remote_copy(src_ref=src, dst_ref=outs[a].at[peer],
                                                  send_sem=send_sems.at[k], recv_sem=recv_sems.at[k],
                                                  device_id=(px, py, pc), device_id_type=MESH)
                started.append((cp, rc))
        for cp, rc in started:
            if rc is None:
                cp.wait()
            else:
                cp.wait_send()
                rc.wait_recv()

    hbm = pl.BlockSpec(memory_space=pltpu.HBM)
    out_shape = tuple(
        jax.ShapeDtypeStruct(((N_DEV,) + a.shape) if gather else a.shape, a.dtype) for a in arrs)
    return pl.pallas_call(
        body, name=name, out_shape=out_shape,
        in_specs=[hbm] * n, out_specs=tuple([hbm] * n),
        scratch_shapes=[pltpu.SemaphoreType.DMA((n * n_peer,)), pltpu.SemaphoreType.DMA((n * n_peer,)),
                        pltpu.SemaphoreType.DMA((n,))],
        compiler_params=pltpu.CompilerParams(has_side_effects=True),
    )(*arrs)


_HBM = pl.BlockSpec(memory_space=pltpu.HBM)
_SEM = pl.BlockSpec(memory_space=pltpu.SEMAPHORE)
_EFFECT = pltpu.SideEffectType.DATAFLOW_SIDE_EFFECTING


def _peer_of(p):
    x, y, c = lax.axis_index("x"), lax.axis_index("y"), lax.axis_index("c")
    px, py, pc = x ^ ((p >> 2) & 1), y ^ ((p >> 1) & 1), c ^ (p & 1)
    return (px, py, pc), 4 * px + 2 * py + pc


def _slot(land_ref, idx, width):
    if width is None:
        return land_ref.at[idx]
    return land_ref.at[:, pl.ds(pl.multiple_of(idx * width, LANES), width)]


def _xstart(srcs, gather, after, name, cols=None, fill_own=False):
    n = len(srcs)
    cols = cols or [False] * n
    widths = [t.shape[1] if cols[a] else None for a, t in enumerate(srcs)]
    me_out = 4 * lax.axis_index("x") + 2 * lax.axis_index("y") + lax.axis_index("c")
    lands = []
    for a, t in enumerate(srcs):
        if cols[a]:
            zone, own, at = lax.empty((t.shape[0], N_DEV * t.shape[1]), t.dtype), t, (0, me_out * t.shape[1])
        elif gather:
            zone, own, at = lax.empty((N_DEV,) + t.shape, t.dtype), t[None], (me_out,) + (0,) * t.ndim
        else:
            zone, own = lax.empty(t.shape, t.dtype), lax.dynamic_index_in_dim(t, me_out, 0, keepdims=True)
            at = (me_out,) + (0,) * (t.ndim - 1)
        lands.append(lax.dynamic_update_slice(zone, own, at) if fill_own else zone)
    n_after = 0 if after is None else 1

    def body(*refs):
        src_refs, land_refs = refs[:n], refs[n:2 * n]
        refs = refs[n_after:]
        send_sems, recv_sems = refs[2 * n:3 * n], refs[3 * n:4 * n]
        token = refs[6 * n]
        me = 4 * lax.axis_index("x") + 2 * lax.axis_index("y") + lax.axis_index("c")
        for a in range(n):
            for p in range(1, N_DEV):
                dev, peer = _peer_of(p)
                pltpu.make_async_remote_copy(
                    src_ref=src_refs[a] if gather else src_refs[a].at[peer], dst_ref=_slot(land_refs[a], me, widths[a]),
                    send_sem=send_sems[a].at[p - 1], recv_sem=recv_sems[a].at[p - 1],
                    device_id=dev, device_id_type=MESH).start()
        token[...] = jnp.zeros_like(token)

    sems = tuple(pltpu.SemaphoreType.DMA((N_DEV - 1,)) for _ in range(2 * n))
    thru = tuple(pltpu.HBM(t.shape, t.dtype) for t in list(srcs) + list(lands))
    res = pl.pallas_call(
        body, name=name,
        out_shape=sems + thru + (jax.ShapeDtypeStruct((SUBLANES, LANES), F32),),
        in_specs=[_HBM] * (2 * n) + [pl.BlockSpec(memory_space=pl.ANY)] * n_after,
        out_specs=tuple([_SEM] * (2 * n) + [_HBM] * (2 * n) + [pl.BlockSpec(memory_space=pltpu.VMEM)]),
        input_output_aliases={i: 2 * n + i for i in range(2 * n)},
        compiler_params=pltpu.CompilerParams(has_side_effects=_EFFECT),
    )(*[pltpu.with_memory_space_constraint(t, pltpu.HBM) for t in list(srcs) + list(lands)],
      *([after] if n_after else []))
    return res[:n], res[n:2 * n], res[2 * n:3 * n], res[3 * n:4 * n], res[4 * n]


def _xwait(src, land, send_sem, recv_sem, after, gather, name, col=False):
    width = src.shape[1] if col else None

    def body(src_ref, land_ref, send_ref, recv_ref, after_ref, src_dead, land_out):
        del after_ref, src_dead, land_out
        for p in range(1, N_DEV):
            dev, peer = _peer_of(p)
            cp = pltpu.make_async_remote_copy(
                src_ref=src_ref if gather else src_ref.at[peer], dst_ref=_slot(land_ref, peer, width),
                send_sem=send_ref.at[p - 1], recv_sem=recv_ref.at[p - 1], device_id=dev, device_id_type=MESH)
            cp.wait_send()
            cp.wait_recv()

    src_done, landed = pl.pallas_call(
        body, name=name, out_shape=(pltpu.HBM(src.shape, src.dtype), pltpu.HBM(land.shape, land.dtype)),
        in_specs=[_HBM, _HBM, _SEM, _SEM, pl.BlockSpec(memory_space=pl.ANY)], out_specs=(_HBM, _HBM),
        input_output_aliases={0: 0, 1: 1},
        compiler_params=pltpu.CompilerParams(has_side_effects=_EFFECT),
    )(src, land, send_sem, recv_sem, after)
    me = 4 * lax.axis_index("x") + 2 * lax.axis_index("y") + lax.axis_index("c")
    if col:
        return _place_cols(landed, src_done, me, name + "_own")
    own = src_done if gather else lax.dynamic_index_in_dim(src_done, me, 0, keepdims=False)
    return lax.dynamic_update_slice(landed, own[None], (me,) + (0,) * own.ndim)


def _place_cols(zone, own, me, name):
    K, nb = own.shape
    tr = min(256, K)

    def body(me_ref, own_ref, zone_ref, out_ref):
        del me_ref, zone_ref
        out_ref[...] = own_ref[...]

    return pl.pallas_call(
        body, name=name, out_shape=jax.ShapeDtypeStruct(zone.shape, zone.dtype),
        grid_spec=pltpu.PrefetchScalarGridSpec(
            num_scalar_prefetch=1, grid=(K // tr,),
            in_specs=[pl.BlockSpec((tr, nb), lambda i, me_ref: (i, 0)), pl.BlockSpec(memory_space=pl.ANY)],
            out_specs=pl.BlockSpec((tr, nb), lambda i, me_ref: (i, me_ref[0]))),
        input_output_aliases={2: 0},
    )(jnp.reshape(me, (1,)).astype(jnp.int32), own, zone)


def _xwait_many(srcs, lands, send_sems, recv_sems, after, name):
    n = len(srcs)

    def body(*refs):
        src_refs, land_refs = refs[:n], refs[n:2 * n]
        snd, rcv = refs[2 * n:3 * n], refs[3 * n:4 * n]
        for a in range(n):
            for p in range(1, N_DEV):
                dev, peer = _peer_of(p)
                cp = pltpu.make_async_remote_copy(
                    src_ref=src_refs[a], dst_ref=land_refs[a].at[peer], send_sem=snd[a].at[p - 1],
                    recv_sem=rcv[a].at[p - 1], device_id=dev, device_id_type=MESH)
                cp.wait_send()
                cp.wait_recv()

    res = pl.pallas_call(
        body, name=name, out_shape=tuple(pltpu.HBM(t.shape, t.dtype) for t in list(srcs) + list(lands)),
        in_specs=[_HBM] * (2 * n) + [_SEM] * (2 * n) + [pl.BlockSpec(memory_space=pl.ANY)],
        out_specs=tuple([_HBM] * (2 * n)), input_output_aliases={i: i for i in range(2 * n)},
        compiler_params=pltpu.CompilerParams(has_side_effects=_EFFECT),
    )(*srcs, *lands, *send_sems, *recv_sems, after)
    return res[:n], res[n:]


def _mm(a, b, *, mode, tm, tn, tk, outs, epilogue=None, extras=(), nb=None, tok=None, scatter=None, into=None, name):
    if mode == "nn":
        (M, K), (_, N) = a.shape, b.shape
    elif mode == "nt":
        (M, K), (N, _) = a.shape, b.shape
    else:
        (K, M), (_, N) = a.shape, b.shape
    tm, tn, tk = min(tm, M), min(tn, N), min(tk, K)
    assert M % tm == 0 and N % tn == 0 and K % tk == 0, (name, M, N, K, tm, tn, tk)
    if mode == "nn":
        a_spec = pl.BlockSpec((tm, tk), lambda i, j, k: (i, k))
        b_spec = pl.BlockSpec((tk, tn), lambda i, j, k: (k, j))
        dims = (((1,), (0,)), ((), ()))
    elif mode == "nt":
        a_spec = pl.BlockSpec((tm, tk), lambda i, j, k: (i, k))
        b_spec = pl.BlockSpec((tn, tk), lambda i, j, k: (j, k))
        dims = (((1,), (1,)), ((), ()))
    else:
        a_spec = pl.BlockSpec((tk, tm), lambda i, j, k: (k, i))
        b_spec = pl.BlockSpec((tk, tn), lambda i, j, k: (k, j))
        dims = (((0,), (0,)), ((), ()))
    nk = K // tk
    n_ex, n_out = len(extras), len(outs)
    n_tok = 0 if tok is None else 1
    nbytes = lambda d: jnp.dtype(d).itemsize
    vmem_est = (2 * (tm * tk * nbytes(a.dtype) + tk * tn * nbytes(b.dtype)
                     + sum(tm * tn * nbytes(e.dtype) for e, kind in extras if kind == "tile")
                     + sum(tm * tn * nbytes(d) for d in outs)) + tm * tn * 4)
    assert vmem_est <= VMEM_LIMIT, (name, vmem_est)
    if epilogue is None:
        epilogue = lambda acc, ex: tuple(acc.astype(d) for d in outs)

    n_into = 0 if into is None else 1

    def body(a_ref, b_ref, *refs):
        refs = refs[n_tok:]
        ex_refs, out_refs = refs[:n_ex], refs[n_ex + n_into:n_ex + n_into + n_out]

        def finish(acc):
            res = epilogue(acc, [r[...] for r in ex_refs])
            for o_ref, v in zip(out_refs, res):
                if nb is None:
                    o_ref[...] = v.astype(o_ref.dtype)
                else:
                    for q in range(tn // nb):
                        o_ref[q] = v[:, q * nb:(q + 1) * nb].astype(o_ref.dtype)

        part = lax.dot_general(a_ref[...], b_ref[...], dims, preferred_element_type=F32)
        if nk == 1:
            finish(part)
        else:
            acc_ref = refs[n_ex + n_into + n_out]
            k = pl.program_id(2)

            @pl.when(k == 0)
            def _():
                acc_ref[...] = part

            @pl.when(k > 0)
            def _():
                acc_ref[...] += part

            @pl.when(k == nk - 1)
            def _():
                finish(acc_ref[...])

    col = (lambda j: j) if scatter is None else (lambda j: scatter[0] * j + scatter[1])
    ex_specs = [pl.BlockSpec((tm, tn), lambda i, j, k: (i, j)) if kind == "tile"
                else pl.BlockSpec((1, tn), lambda i, j, k: (0, col(j))) for _, kind in extras]
    if nb is not None:
        assert tn % nb == 0, (name, tn, nb)
        o_spec = pl.BlockSpec((tn // nb, tm, nb), lambda i, j, k: (j, i, 0))
        o_shape = (N // nb, M, nb)
    else:
        o_spec = pl.BlockSpec((tm, tn), lambda i, j, k: (i, col(j)))
        o_shape = (M, N if scatter is None else scatter[2])
    assert n_into == 0 or n_out == 1
    res = pl.pallas_call(
        body, name=name, grid=(M // tm, N // tn, nk),
        in_specs=[a_spec, b_spec] + [pl.BlockSpec((SUBLANES, LANES), lambda i, j, k: (0, 0))] * n_tok + ex_specs
                 + [pl.BlockSpec(memory_space=pl.ANY)] * n_into,
        out_specs=tuple([o_spec] * n_out),
        out_shape=tuple(jax.ShapeDtypeStruct(o_shape, d) for d in outs),
        input_output_aliases={2 + n_tok + n_ex: 0} if n_into else {},
        scratch_shapes=[pltpu.VMEM((tm, tn), F32)] if nk > 1 else [],
        compiler_params=_cparams(3, big=True),
    )(a, b, *([tok] if n_tok else []), *[e for e, _ in extras], *([into] if n_into else []))
    return res[0] if n_out == 1 else res


def _mm_rows(a, b, *, mode, tm, seq, ins, outs, epilogue, tok=None, name):
    M, K = a.shape
    N = b.shape[1] if mode == "nn" else b.shape[0]
    tm = min(tm, M)
    assert M % tm == 0 and seq % tm == 0, (name, M, seq, tm)
    tpb = seq // tm
    n_b = M // seq
    dims = (((1,), (0,)), ((), ())) if mode == "nn" else (((1,), (1,)), ((), ()))
    n_tok = 0 if tok is None else 1
    n_in, n_out = len(ins), len(outs)

    in_specs, in_arrs = [], []
    for spec in ins:
        kind, arr = spec[0], spec[1]
        in_arrs.append(arr)
        if kind == "tile":
            in_specs.append(pl.BlockSpec((tm, arr.shape[1]), lambda i: (i, 0)))
        elif kind == "tilecol":
            in_specs.append(pl.BlockSpec((tm, spec[2]), lambda i, cb=spec[3]: (i, cb)))
        elif kind == "row":
            in_specs.append(pl.BlockSpec(arr.shape, lambda i: (0, 0)))
        else:
            in_specs.append(pl.BlockSpec((None, 1, arr.shape[2]), lambda i: (i // tpb, 0, 0)))
    out_specs, out_shapes = [], []
    for spec in outs:
        kind = spec[0]
        if kind == "tile":
            out_specs.append(pl.BlockSpec((tm, spec[2]), lambda i: (i, 0)))
            out_shapes.append(jax.ShapeDtypeStruct((M, spec[2]), spec[1]))
        elif kind == "tilecol":
            out_specs.append(pl.BlockSpec((tm, spec[2]), lambda i, cb=spec[3]: (i, cb)))
            out_shapes.append(jax.ShapeDtypeStruct((M, spec[4]), spec[1]))
        elif kind == "acc_row":
            out_specs.append(pl.BlockSpec((1, spec[1]), lambda i: (0, 0)))
            out_shapes.append(jax.ShapeDtypeStruct((1, spec[1]), F32))
        elif kind == "acc_brow":
            out_specs.append(pl.BlockSpec((None, 1, spec[1]), lambda i: (i // tpb, 0, 0)))
            out_shapes.append(jax.ShapeDtypeStruct((n_b, 1, spec[1]), F32))
        else:
            out_specs.append(pl.BlockSpec((SUBLANES, LANES), lambda i: (0, 0)))
            out_shapes.append(jax.ShapeDtypeStruct((SUBLANES, LANES), F32))

    def body(a_ref, b_ref, *refs):
        refs = refs[n_tok:]
        in_refs, out_refs = refs[:n_in], refs[n_in:n_in + n_out]
        i = pl.program_id(0)
        prod = lax.dot_general(a_ref[...], b_ref[...], dims, preferred_element_type=F32)
        vals = epilogue(prod, [r[...] for r in in_refs])
        for spec, o_ref, v in zip(outs, out_refs, vals):
            kind = spec[0]
            if kind in ("tile", "tilecol"):
                o_ref[...] = v.astype(o_ref.dtype)
            else:
                first = (i % tpb == 0) if kind == "acc_brow" else (i == 0)

                @pl.when(first)
                def _(o_ref=o_ref, v=v):
                    o_ref[...] = jnp.broadcast_to(v, o_ref.shape)

                @pl.when(jnp.logical_not(first))
                def _(o_ref=o_ref, v=v):
                    o_ref[...] += v

    res = pl.pallas_call(
        body, name=name, grid=(M // tm,),
        in_specs=[pl.BlockSpec((tm, K), lambda i: (i, 0)),
                  pl.BlockSpec(b.shape, lambda i: (0, 0), pipeline_mode=pl.Buffered(1))]
                 + [pl.BlockSpec((SUBLANES, LANES), lambda i: (0, 0))] * n_tok + in_specs,
        out_specs=tuple(out_specs), out_shape=tuple(out_shapes),
        compiler_params=_cparams(1, big=True),
    )(a, b, *([tok] if n_tok else []), *in_arrs)
    return res


def _tok_spec(ts, width, col_block=0):
    return pl.BlockSpec((None, ts, width), lambda b, s: (b, s, col_block))


def _brow_spec(width):
    return pl.BlockSpec((None, 1, width), lambda b, s: (b, 0, 0))


def _vec_spec(width):
    return pl.BlockSpec((1, width), lambda b, s: (0, 0))


def _modulate(x, sc, sh, ts):
    Bl, S, D = x.shape

    def body(x_ref, sc_ref, sh_ref, o_ref):
        o_ref[...] = (x_ref[...] * (1.0 + sc_ref[...]) + sh_ref[...]).astype(BF16)

    return pl.pallas_call(
        body, name="modulate", grid=(Bl, S // ts),
        in_specs=[_tok_spec(ts, D), _brow_spec(D), _brow_spec(D)],
        out_specs=_tok_spec(ts, D), out_shape=jax.ShapeDtypeStruct((Bl, S, D), BF16),
        compiler_params=_cparams(2),
    )(x, sc, sh)


def _mix_fwd(proj, w_conv, b_conv, w_rg_a, b_rg_a, w_rg_x, b_rg_x, lam, w_sp, b_sp_t, ln_v_g, ln_v_b, *, tm, lw, sw):
    Bl, S, _ = proj.shape
    heads, hd = w_rg_a.shape[0], w_rg_a.shape[1]
    groups = w_sp.shape[0]
    cw = 2 * lw + 2 * sw
    nblk = tm // SGU_BLOCK

    G = tm // SUBLANES
    nc = lw // LANES

    def body(p_ref, wc_ref, bc_ref, wa_ref, ba_ref, wx_ref, bx_ref, lam_ref, wsp_ref, bsp_ref, lg_ref, lb_ref,
             hs_ref, ya_ref, ys_ref, xc_ref, r_ref, ig_ref, a_ref, m_ref,
             xext, hnat, hcar, h7_scr, a7_scr, hp_scr):
        s = pl.program_id(1)

        @pl.when(s == 0)
        def _():
            xext[:, 0:SUBLANES, :] = jnp.zeros((nc, SUBLANES, LANES), F32)
            hcar[...] = jnp.zeros_like(hcar)

        @pl.when(s > 0)
        def _():
            xext[:, 0:SUBLANES, :] = xext[:, tm:tm + SUBLANES, :]

        for c in range(nc):
            xext[c, SUBLANES:SUBLANES + tm, :] = p_ref[:, c * LANES:(c + 1) * LANES].astype(F32)
        gl = p_ref[:, lw:2 * lw].astype(F32)

        def slab(ref3, start):
            return jnp.concatenate([ref3[c, pl.ds(start, G, stride=SUBLANES), :] for c in range(nc)], axis=1)

        xs = {st: slab(xext, st) for st in range(SUBLANES - 3, 2 * SUBLANES)}
        xc_slabs = []
        for j in range(SUBLANES):
            acc = bc_ref[...] + xs[SUBLANES + j] * wc_ref[3:4, :]
            for k in (1, 2, 3):
                acc = acc + xs[SUBLANES + j - k] * wc_ref[3 - k:4 - k, :]
            xc_slabs.append(acc)
        xc = jnp.concatenate(xc_slabs, axis=0)

        xcb = xc.astype(BF16)
        pa = jnp.concatenate([jnp.dot(xcb[:, h * hd:(h + 1) * hd], wa_ref[h], preferred_element_type=F32)
                              for h in range(heads)], axis=1) + ba_ref[...]
        px = jnp.concatenate([jnp.dot(xcb[:, h * hd:(h + 1) * hd], wx_ref[h], preferred_element_type=F32)
                              for h in range(heads)], axis=1) + bx_ref[...]
        r = _sigmoid(pa)
        ig = _sigmoid(px)
        nl = -lam_ref[...]
        big_l = -LRU_C * (jnp.maximum(nl, 0.0) + _log1p_pos(jnp.exp(-jnp.abs(nl))))
        la = big_l * r
        a = jnp.exp(la)
        th = jnp.tanh(la)
        msq = (-2.0 * th) / (1.0 - th)
        m = msq * lax.rsqrt(jnp.maximum(msq, 1e-30))
        bin_ = m * (ig * xc)
        xc_ref[...] = xc
        r_ref[...] = r
        ig_ref[...] = ig
        a_ref[...] = a
        m_ref[...] = m

        h0 = [bin_[0:G]]
        cp = [a[0:G]]
        for j in range(1, SUBLANES):
            aj = a[j * G:(j + 1) * G]
            h0.append(aj * h0[j - 1] + bin_[j * G:(j + 1) * G])
            cp.append(aj * cp[j - 1])
        h7_scr[...] = h0[SUBLANES - 1]
        a7_scr[...] = cp[SUBLANES - 1]
        carry = hcar[0:1, :]
        for g in range(G):
            hp_scr[g:g + 1, :] = carry
            carry = h7_scr[g:g + 1, :] + a7_scr[g:g + 1, :] * carry
        hcar[0:1, :] = carry
        hprev = hp_scr[...]
        for j in range(SUBLANES):
            hj = h0[j] + cp[j] * hprev
            for c in range(nc):
                hnat[c, pl.ds(j, G, stride=SUBLANES), :] = hj[:, c * LANES:(c + 1) * LANES]
        hs = jnp.concatenate([hnat[c] for c in range(nc)], axis=1)
        hs_ref[...] = hs
        ya_ref[...] = (hs * _gelu(gl)).astype(BF16)

        gu = _gelu(p_ref[:, 2 * lw:2 * lw + sw].astype(F32))
        gv = _gelu(p_ref[:, 2 * lw + sw:cw].astype(F32))
        xhat, _ = _ln_stats(gv)
        vn = (xhat * lg_ref[...] + lb_ref[...]).astype(BF16)
        tpos = lax.broadcasted_iota(jnp.int32, (SGU_BLOCK, SGU_BLOCK), 0) // CHUNK
        spos = lax.broadcasted_iota(jnp.int32, (SGU_BLOCK, SGU_BLOCK), 1) // CHUNK
        gw = sw // groups
        rows_out = []
        for blk in range(nblk):
            r0 = blk * SGU_BLOCK
            cols = []
            for g in range(groups):
                wm = jnp.where(spos <= tpos, wsp_ref[g], 0.0).astype(BF16)
                mixed = jnp.dot(wm, vn[r0:r0 + SGU_BLOCK, g * gw:(g + 1) * gw], preferred_element_type=F32)
                cols.append(mixed + bsp_ref[:, g:g + 1])
            rows_out.append(jnp.concatenate(cols, axis=1))
        mixed_all = jnp.concatenate(rows_out, axis=0) if nblk > 1 else rows_out[0]
        ys_ref[...] = (gu * mixed_all).astype(BF16)

    full = lambda shp: pl.BlockSpec(shp, lambda b, s: (0,) * len(shp))
    return pl.pallas_call(
        body, name="mix_fwd", grid=(Bl, S // tm),
        in_specs=[_tok_spec(tm, cw), full(w_conv.shape), full(b_conv.shape), full(w_rg_a.shape), full(b_rg_a.shape),
                  full(w_rg_x.shape), full(b_rg_x.shape), full(lam.shape), full(w_sp.shape), full(b_sp_t.shape),
                  full(ln_v_g.shape), full(ln_v_b.shape)],
        out_specs=(_tok_spec(tm, lw), _tok_spec(tm, lw), _tok_spec(tm, sw)) + (_tok_spec(tm, lw),) * 5,
        out_shape=(jax.ShapeDtypeStruct((Bl, S, lw), F32), jax.ShapeDtypeStruct((Bl, S, lw), BF16),
                   jax.ShapeDtypeStruct((Bl, S, sw), BF16)) + (jax.ShapeDtypeStruct((Bl, S, lw), F32),) * 5,
        scratch_shapes=[pltpu.VMEM((nc, tm + SUBLANES, LANES), F32), pltpu.VMEM((nc, tm, LANES), F32),
                        pltpu.VMEM((SUBLANES, lw), F32), pltpu.VMEM((G, lw), F32), pltpu.VMEM((G, lw), F32),
                        pltpu.VMEM((G, lw), F32)],
        compiler_params=_cparams(2, big=True),
    )(proj, w_conv, b_conv, w_rg_a, b_rg_a, w_rg_x, b_rg_x, lam, w_sp, b_sp_t, ln_v_g, ln_v_b)


def _merge_fwd(proj, y_a, y_b, *, ts, d):
    Bl, S, din = proj.shape
    gcol = (din - 2 * d) // (2 * d)
    assert gcol * 2 * d == din - 2 * d

    def body(g_ref, ya_ref, yb_ref, o_ref):
        sa = _sigmoid(g_ref[:, 0:d].astype(F32))
        sb = _sigmoid(g_ref[:, d:2 * d].astype(F32))
        o_ref[...] = (sa * ya_ref[...].astype(F32) + sb * yb_ref[...].astype(F32)).astype(BF16)

    return pl.pallas_call(
        body, name="merge_fwd", grid=(Bl, S // ts),
        in_specs=[_tok_spec(ts, 2 * d, gcol), _tok_spec(ts, d), _tok_spec(ts, d)],
        out_specs=_tok_spec(ts, d), out_shape=jax.ShapeDtypeStruct((Bl, S, d), BF16),
        compiler_params=_cparams(2),
    )(proj, y_a, y_b)


def _ln1_fwd(x, mix, gt1, g1, b1, sc2, sh2, *, ts):
    Bl, S, D = x.shape

    def body(x_ref, mix_ref, gt_ref, g_ref, b_ref, sc_ref, sh_ref, x1_ref, h2_ref):
        z = ALPHA * x_ref[...] + (1.0 + gt_ref[...]) * mix_ref[...].astype(F32)
        xhat, _ = _ln_stats(z)
        x1 = xhat * g_ref[...] + b_ref[...]
        x1_ref[...] = x1
        h2_ref[...] = (x1 * (1.0 + sc_ref[...]) + sh_ref[...]).astype(BF16)

    return pl.pallas_call(
        body, name="ln1_fwd", grid=(Bl, S // ts),
        in_specs=[_tok_spec(ts, D), _tok_spec(ts, D), _brow_spec(D), _vec_spec(D), _vec_spec(D), _brow_spec(D),
                  _brow_spec(D)],
        out_specs=(_tok_spec(ts, D), _tok_spec(ts, D)),
        out_shape=(jax.ShapeDtypeStruct((Bl, S, D), F32), jax.ShapeDtypeStruct((Bl, S, D), BF16)),
        compiler_params=_cparams(2),
    )(x, mix, gt1, g1, b1, sc2, sh2)


def _ln2_loss(x1, f, tgt, gt2, g2, b2, *, ts):
    Bl, S, D = x1.shape

    def body(x1_ref, f_ref, t_ref, gt_ref, g_ref, b_ref, df_ref, dx1_ref, dgt_ref, dg_ref, db_ref, loss_ref):
        s = pl.program_id(1)

        @pl.when(_first_step())
        def _():
            dg_ref[...] = jnp.zeros_like(dg_ref)
            db_ref[...] = jnp.zeros_like(db_ref)
            loss_ref[...] = jnp.zeros_like(loss_ref)

        @pl.when(s == 0)
        def _():
            dgt_ref[...] = jnp.zeros_like(dgt_ref)

        fv = f_ref[...]
        z = ALPHA * x1_ref[...] + (1.0 + gt_ref[...]) * fv
        xhat, rstd = _ln_stats(z)
        x2 = xhat * g_ref[...] + b_ref[...]
        err = x2 - t_ref[...]
        loss_ref[...] += 0.5 * jnp.sum(jnp.mean(err * err, axis=-1, keepdims=True))
        dy = err * (1.0 / D)
        dg_ref[...] += _colsum(dy * xhat)
        db_ref[...] += _colsum(dy)
        dz = _ln_bwd(dy, xhat, rstd, g_ref[...])
        dx1_ref[...] = ALPHA * dz
        dgt_ref[...] += _colsum(dz * fv)
        df_ref[...] = (dz * (1.0 + gt_ref[...])).astype(BF16)

    return pl.pallas_call(
        body, name="ln2_loss", grid=(Bl, S // ts),
        in_specs=[_tok_spec(ts, D), _tok_spec(ts, D), _tok_spec(ts, D), _brow_spec(D), _vec_spec(D), _vec_spec(D)],
        out_specs=(_tok_spec(ts, D), _tok_spec(ts, D), _brow_spec(D), _vec_spec(D), _vec_spec(D),
                   pl.BlockSpec((SUBLANES, LANES), lambda b, s: (0, 0))),
        out_shape=(jax.ShapeDtypeStruct((Bl, S, D), BF16), jax.ShapeDtypeStruct((Bl, S, D), F32),
                   jax.ShapeDtypeStruct((Bl, 1, D), F32), jax.ShapeDtypeStruct((1, D), F32),
                   jax.ShapeDtypeStruct((1, D), F32), jax.ShapeDtypeStruct((SUBLANES, LANES), F32)),
        compiler_params=_cparams(2),
    )(x1, f, tgt, gt2, g2, b2)


def _ln1_bwd(dx1p, dh2, x1, x, mix, sc2, gt1, g1, *, ts):
    Bl, S, D = x.shape

    def body(dx1p_ref, dh2_ref, x1_ref, x_ref, mix_ref, sc_ref, gt_ref, g_ref,
             dxp_ref, dmix_ref, dsc_ref, dsh_ref, dgt_ref, dg_ref, db_ref):
        s = pl.program_id(1)

        @pl.when(_first_step())
        def _():
            dg_ref[...] = jnp.zeros_like(dg_ref)
            db_ref[...] = jnp.zeros_like(db_ref)

        @pl.when(s == 0)
        def _():
            dsc_ref[...] = jnp.zeros_like(dsc_ref)
            dsh_ref[...] = jnp.zeros_like(dsh_ref)
            dgt_ref[...] = jnp.zeros_like(dgt_ref)

        dh2 = dh2_ref[...].astype(F32)
        mixv = mix_ref[...].astype(F32)
        dsc_ref[...] += _colsum(dh2 * x1_ref[...])
        dsh_ref[...] += _colsum(dh2)
        dx1 = dx1p_ref[...] + dh2 * (1.0 + sc_ref[...])
        z = ALPHA * x_ref[...] + (1.0 + gt_ref[...]) * mixv
        xhat, rstd = _ln_stats(z)
        dg_ref[...] += _colsum(dx1 * xhat)
        db_ref[...] += _colsum(dx1)
        dz = _ln_bwd(dx1, xhat, rstd, g_ref[...])
        dxp_ref[...] = ALPHA * dz
        dgt_ref[...] += _colsum(dz * mixv)
        dmix_ref[...] = (dz * (1.0 + gt_ref[...])).astype(BF16)

    return pl.pallas_call(
        body, name="ln1_bwd", grid=(Bl, S // ts),
        in_specs=[_tok_spec(ts, D)] * 5 + [_brow_spec(D), _brow_spec(D), _vec_spec(D)],
        out_specs=(_tok_spec(ts, D), _tok_spec(ts, D), _brow_spec(D), _brow_spec(D), _brow_spec(D), _vec_spec(D),
                   _vec_spec(D)),
        out_shape=(jax.ShapeDtypeStruct((Bl, S, D), F32), jax.ShapeDtypeStruct((Bl, S, D), BF16),
                   jax.ShapeDtypeStruct((Bl, 1, D), F32), jax.ShapeDtypeStruct((Bl, 1, D), F32),
                   jax.ShapeDtypeStruct((Bl, 1, D), F32), jax.ShapeDtypeStruct((1, D), F32),
                   jax.ShapeDtypeStruct((1, D), F32)),
        compiler_params=_cparams(2),
    )(dx1p, dh2, x1, x, mix, sc2, gt1, g1)


def _merge_bwd(dmerged, y_a, y_b, proj, *, ts, d):
    Bl, S, din = proj.shape
    gcol = (din - 2 * d) // (2 * d)

    def body(dm_ref, ya_ref, yb_ref, g_ref, dya_ref, dyb_ref, dp_ref, db_ref):
        @pl.when(_first_step())
        def _():
            db_ref[...] = jnp.zeros_like(db_ref)

        dm = dm_ref[...].astype(F32)
        sa = _sigmoid(g_ref[:, 0:d].astype(F32))
        sb = _sigmoid(g_ref[:, d:2 * d].astype(F32))
        dya_ref[...] = (dm * sa).astype(BF16)
        dyb_ref[...] = (dm * sb).astype(BF16)
        dga = dm * ya_ref[...].astype(F32) * sa * (1.0 - sa)
        dgb = dm * yb_ref[...].astype(F32) * sb * (1.0 - sb)
        dp_ref[:, 0:d] = dga.astype(BF16)
        dp_ref[:, d:2 * d] = dgb.astype(BF16)
        db_ref[:, 0:d] += _colsum(dga)
        db_ref[:, d:2 * d] += _colsum(dgb)

    return pl.pallas_call(
        body, name="merge_bwd", grid=(Bl, S // ts),
        in_specs=[_tok_spec(ts, d), _tok_spec(ts, d), _tok_spec(ts, d), _tok_spec(ts, 2 * d, gcol)],
        out_specs=(_tok_spec(ts, d), _tok_spec(ts, d), _tok_spec(ts, 2 * d, gcol), _vec_spec(2 * d)),
        out_shape=(jax.ShapeDtypeStruct((Bl, S, d), BF16), jax.ShapeDtypeStruct((Bl, S, d), BF16),
                   jax.ShapeDtypeStruct((Bl, S, din), BF16), jax.ShapeDtypeStruct((1, 2 * d), F32)),
        compiler_params=_cparams(2),
    )(dmerged, y_a, y_b, proj)


def _mix_bwd(proj, hs, dya, dys, dproj, saved, w_conv, b_conv, w_rg_a, b_rg_a, w_rg_x, b_rg_x, lam, w_sp, b_sp_t,
             ln_v_g, ln_v_b, *, tm, lw, sw):
    Bl, S, din = proj.shape
    heads, hd = w_rg_a.shape[0], w_rg_a.shape[1]
    groups = w_sp.shape[0]
    gw = sw // groups
    cw = 2 * lw + 2 * sw
    nblk = tm // SGU_BLOCK
    n_s = S // tm
    per8 = tm // SUBLANES
    halo_rows = 2 * SUBLANES

    G = tm // SUBLANES
    nc = lw // LANES

    def body(p_ref, xh_ref, hs_ref, hh_ref, dya_ref, dys_ref, dpin_ref, xc_ref, r_ref, ig_ref, a_ref, m_ref,
             wc_ref, bc_ref, wa_ref, ba_ref, wx_ref, bx_ref, lam_ref, wsp_ref, bsp_ref, lg_ref, lb_ref,
             dp_ref, dbin_ref, dwc_ref, dbc_ref, dwa_ref, dba_ref, dwx_ref, dbx_ref, dlam_ref, dwsp_ref, dbsp_ref,
             dlg_ref, dlb_ref,
             xext, hext, dnat, dxext, dhcar, g00_scr, p0_scr, a0_scr, cin_scr):
        del dpin_ref
        sr = pl.program_id(1)
        first_tile = sr == n_s - 1

        @pl.when(_first_step())
        def _():
            for ref in (dbin_ref, dwc_ref, dbc_ref, dwa_ref, dba_ref, dwx_ref, dbx_ref, dlam_ref, dwsp_ref, dbsp_ref,
                        dlg_ref, dlb_ref):
                ref[...] = jnp.zeros_like(ref)

        @pl.when(sr == 0)
        def _():
            dhcar[...] = jnp.zeros_like(dhcar)
            dxext[:, tm:tm + SUBLANES, :] = jnp.zeros((nc, SUBLANES, LANES), F32)

        @pl.when(sr > 0)
        def _():
            dxext[:, tm:tm + SUBLANES, :] = dxext[:, 0:SUBLANES, :]

        def slab(ref3, start):
            return jnp.concatenate([ref3[c, pl.ds(start, G, stride=SUBLANES), :] for c in range(nc)], axis=1)

        def put_slab(ref3, j, val):
            for c in range(nc):
                ref3[c, pl.ds(j, G, stride=SUBLANES), :] = val[:, c * LANES:(c + 1) * LANES]

        keep = jnp.where(first_tile, 0.0, 1.0)
        xprev = xh_ref[...].astype(F32)[halo_rows - SUBLANES:halo_rows] * keep
        hsv = hs_ref[...]
        hprev8 = hh_ref[...] * keep
        for c in range(nc):
            cs = slice(c * LANES, (c + 1) * LANES)
            xext[c, 0:SUBLANES, :] = xprev[:, cs]
            xext[c, SUBLANES:SUBLANES + tm, :] = p_ref[:, cs].astype(F32)
            hext[c, 0:SUBLANES, :] = hprev8[:, cs]
            hext[c, SUBLANES:SUBLANES + tm, :] = hsv[:, cs]
        gl = p_ref[:, lw:2 * lw].astype(F32)
        ggl, dggl = _gelu_and_grad(gl)
        dyav = dya_ref[...]
        dhs = dyav * ggl
        dgl = dyav * hsv * dggl
        dp_ref[:, lw:2 * lw] = dgl.astype(BF16)
        dbin_ref[:, lw:2 * lw] += _colsum(dgl)
        for c in range(nc):
            dnat[c] = dhs[:, c * LANES:(c + 1) * LANES]

        xc, r, ig, a, m = xc_ref[...], r_ref[...], ig_ref[...], a_ref[...], m_ref[...]
        xcb = xc.astype(BF16)
        nl = -lam_ref[...]
        big_l = -LRU_C * (jnp.maximum(nl, 0.0) + _log1p_pos(jnp.exp(-jnp.abs(nl))))

        g0 = [None] * SUBLANES
        pp = [None] * SUBLANES
        g0[SUBLANES - 1] = slab(dnat, SUBLANES - 1)
        for j in range(SUBLANES - 2, -1, -1):
            an = a[(j + 1) * G:(j + 2) * G]
            g0[j] = slab(dnat, j) + an * g0[j + 1]
            pp[j] = an if j == SUBLANES - 2 else an * pp[j + 1]
        g00_scr[...] = g0[0]
        p0_scr[...] = pp[0]
        a0_scr[...] = a[0:G]
        cin = dhcar[0:1, :]
        for g in range(G - 1, -1, -1):
            cin_scr[g:g + 1, :] = cin
            cin = a0_scr[g:g + 1, :] * (g00_scr[g:g + 1, :] + p0_scr[g:g + 1, :] * cin)
        dhcar[0:1, :] = cin
        cinv = cin_scr[...]
        dh = jnp.concatenate([g0[j] + pp[j] * cinv for j in range(SUBLANES - 1)] + [g0[SUBLANES - 1] + cinv], axis=0)

        hprev = jnp.concatenate([slab(hext, SUBLANES - 1 + j) for j in range(SUBLANES)], axis=0)
        da = dh * hprev
        ixc = ig * xc
        dm = dh * ixc
        dixc = dh * m
        di = dixc * xc
        dxc = dixc * ig
        dla = da * a - dm * (a * a) / m
        dlam_ref[...] += _colsum(dla * r) * (LRU_C * _sigmoid(nl))
        dr = dla * big_l
        dpa = dr * r * (1.0 - r)
        dpx = di * ig * (1.0 - ig)
        dba_ref[...] += _colsum(dpa)
        dbx_ref[...] += _colsum(dpx)
        dpab = dpa.astype(BF16)
        dpxb = dpx.astype(BF16)
        nt = (((1,), (1,)), ((), ()))
        tn = (((0,), (0,)), ((), ()))
        dxc_g = []
        for h in range(heads):
            sl = slice(h * hd, (h + 1) * hd)
            dxc_g.append(lax.dot_general(dpab[:, sl], wa_ref[h], nt, preferred_element_type=F32)
                         + lax.dot_general(dpxb[:, sl], wx_ref[h], nt, preferred_element_type=F32))
            dwa_ref[h] += lax.dot_general(xcb[:, sl], dpab[:, sl], tn, preferred_element_type=F32)
            dwx_ref[h] += lax.dot_general(xcb[:, sl], dpxb[:, sl], tn, preferred_element_type=F32)
        dxc = dxc + jnp.concatenate(dxc_g, axis=1)

        dbc_ref[...] += _colsum(dxc)
        xs = {st: slab(xext, st) for st in range(SUBLANES - 3, 2 * SUBLANES)}
        for k in range(4):
            xsh = jnp.concatenate([xs[SUBLANES + j - (3 - k)] for j in range(SUBLANES)], axis=0)
            dwc_ref[k:k + 1, :] += _colsum(dxc * xsh)
        for j in range(SUBLANES):
            put_slab(dxext, j, dxc[j * G:(j + 1) * G])
        us = {st: slab(dxext, st) for st in range(SUBLANES + 3)}
        for j in range(SUBLANES):
            acc = us[j] * wc_ref[3:4, :]
            for k in (1, 2, 3):
                acc = acc + us[j + k] * wc_ref[3 - k:4 - k, :]
            put_slab(dnat, j, acc)
        dxl = jnp.concatenate([dnat[c] for c in range(nc)], axis=1)
        dp_ref[:, 0:lw] = dxl.astype(BF16)
        dbin_ref[:, 0:lw] += _colsum(dxl)

        gu, dgu_dx = _gelu_and_grad(p_ref[:, 2 * lw:2 * lw + sw].astype(F32))
        gv, dgv_dx = _gelu_and_grad(p_ref[:, 2 * lw + sw:cw].astype(F32))
        xhat, rstd = _ln_stats(gv)
        vn = (xhat * lg_ref[...] + lb_ref[...]).astype(BF16)
        dys = dys_ref[...]
        dmixed = dys * gu
        dmb = dmixed.astype(BF16)
        tpos = lax.broadcasted_iota(jnp.int32, (SGU_BLOCK, SGU_BLOCK), 0) // CHUNK
        spos = lax.broadcasted_iota(jnp.int32, (SGU_BLOCK, SGU_BLOCK), 1) // CHUNK
        causal = spos <= tpos
        mixed_rows, dvn_rows = [], []
        for blk in range(nblk):
            rs = slice(blk * SGU_BLOCK, (blk + 1) * SGU_BLOCK)
            mcols, dcols = [], []
            for g in range(groups):
                cs = slice(g * gw, (g + 1) * gw)
                wm = jnp.where(causal, wsp_ref[g], 0.0).astype(BF16)
                mcols.append(jnp.dot(wm, vn[rs, cs], preferred_element_type=F32) + bsp_ref[:, g:g + 1])
                dcols.append(lax.dot_general(wm, dmb[rs, cs], tn, preferred_element_type=F32))
                dw = lax.dot_general(dmb[rs, cs], vn[rs, cs], nt, preferred_element_type=F32)
                dwsp_ref[g] += jnp.where(causal, dw, 0.0)
                dbsp_ref[:, g:g + 1] += jnp.sum(dmixed[rs, cs], axis=1, keepdims=True)
            mixed_rows.append(jnp.concatenate(mcols, axis=1))
            dvn_rows.append(jnp.concatenate(dcols, axis=1))
        mixed_all = jnp.concatenate(mixed_rows, axis=0) if nblk > 1 else mixed_rows[0]
        dvn = jnp.concatenate(dvn_rows, axis=0) if nblk > 1 else dvn_rows[0]
        du = dys * mixed_all * dgu_dx
        dlg_ref[...] += _colsum(dvn * xhat)
        dlb_ref[...] += _colsum(dvn)
        dv = _ln_bwd(dvn, xhat, rstd, lg_ref[...]) * dgv_dx
        dp_ref[:, 2 * lw:2 * lw + sw] = du.astype(BF16)
        dp_ref[:, 2 * lw + sw:cw] = dv.astype(BF16)
        dbin_ref[:, 2 * lw:2 * lw + sw] += _colsum(du)
        dbin_ref[:, 2 * lw + sw:cw] += _colsum(dv)

    rev = lambda s: n_s - 1 - s
    tile = lambda w: pl.BlockSpec((None, tm, w), lambda b, s: (b, rev(s), 0))
    halo = lambda w: pl.BlockSpec((None, SUBLANES, w), lambda b, s: (b, jnp.maximum(rev(s) * per8 - 1, 0), 0))
    xhalo = pl.BlockSpec((None, halo_rows, lw), lambda b, s: (b, jnp.maximum(rev(s) * (tm // halo_rows) - 1, 0), 0))
    full = lambda shp: pl.BlockSpec(shp, lambda b, s: (0,) * len(shp))
    small = [w_conv, b_conv, w_rg_a, b_rg_a, w_rg_x, b_rg_x, lam, w_sp, b_sp_t, ln_v_g, ln_v_b]
    acc_shapes = [(1, cw), w_conv.shape, b_conv.shape, w_rg_a.shape, b_rg_a.shape, w_rg_x.shape, b_rg_x.shape,
                  lam.shape, w_sp.shape, b_sp_t.shape, ln_v_g.shape, ln_v_b.shape]
    res = pl.pallas_call(
        body, name="mix_bwd", grid=(Bl, n_s),
        in_specs=[tile(cw), xhalo, tile(lw), halo(lw), tile(lw), tile(sw), pl.BlockSpec(memory_space=pl.ANY)]
                 + [tile(lw)] * 5 + [full(w.shape) for w in small],
        out_specs=tuple([tile(cw)] + [full(shp) for shp in acc_shapes]),
        out_shape=tuple([jax.ShapeDtypeStruct((Bl, S, din), BF16)] + [jax.ShapeDtypeStruct(shp, F32) for shp in acc_shapes]),
        input_output_aliases={6: 0},
        scratch_shapes=[pltpu.VMEM((nc, tm + SUBLANES, LANES), F32), pltpu.VMEM((nc, tm + SUBLANES, LANES), F32),
                        pltpu.VMEM((nc, tm, LANES), F32), pltpu.VMEM((nc, tm + SUBLANES, LANES), F32),
                        pltpu.VMEM((SUBLANES, lw), F32), pltpu.VMEM((G, lw), F32), pltpu.VMEM((G, lw), F32),
                        pltpu.VMEM((G, lw), F32), pltpu.VMEM((G, lw), F32)],
        compiler_params=_cparams(2, big=True),
    )(proj, proj, hs, hs, dya, dys, dproj, *saved, *small)
    return res


def _final_dx(dxp, dh, x, sc1, *, ts):
    Bl, S, D = x.shape

    def body(dxp_ref, dh_ref, x_ref, sc_ref, dx_ref, dsc_ref, dsh_ref):
        @pl.when(pl.program_id(1) == 0)
        def _():
            dsc_ref[...] = jnp.zeros_like(dsc_ref)
            dsh_ref[...] = jnp.zeros_like(dsh_ref)

        dh = dh_ref[...]
        dx_ref[...] = dxp_ref[...] + dh * (1.0 + sc_ref[...])
        dsc_ref[...] += _colsum(dh * x_ref[...])
        dsh_ref[...] += _colsum(dh)

    return pl.pallas_call(
        body, name="final_dx", grid=(Bl, S // ts),
        in_specs=[_tok_spec(ts, D), _tok_spec(ts, D), _tok_spec(ts, D), _brow_spec(D)],
        out_specs=(_tok_spec(ts, D), _brow_spec(D), _brow_spec(D)),
        out_shape=(jax.ShapeDtypeStruct((Bl, S, D), F32), jax.ShapeDtypeStruct((Bl, 1, D), F32),
                   jax.ShapeDtypeStruct((Bl, 1, D), F32)),
        compiler_params=_cparams(2),
    )(dxp, dh, x, sc1)


def _ada_fwd(c_all, w_ada):
    R, D = c_all.shape
    nb = w_ada.shape[1]

    def body(c_ref, w_ref, act_ref, o_ref):
        cv = c_ref[...]
        act = (cv * _sigmoid(cv)).astype(BF16)
        act_ref[...] = act
        o_ref[...] = jnp.dot(act, w_ref[...].astype(BF16), preferred_element_type=F32)

    return pl.pallas_call(
        body, name="ada_fwd",
        out_shape=(jax.ShapeDtypeStruct((R, D), BF16), jax.ShapeDtypeStruct((R, nb), F32)),
        compiler_params=pltpu.CompilerParams(vmem_limit_bytes=VMEM_LIMIT),
    )(c_all, w_ada)


def _ada_bwd(c_act, dmod_cols):
    R, D = c_act.shape
    nb = dmod_cols.shape[1]

    def body(act_ref, d_ref, o_ref, b_ref):
        o_ref[...] = lax.dot_general(act_ref[...], d_ref[...].astype(BF16), (((0,), (0,)), ((), ())),
                                     preferred_element_type=F32)
        b_ref[...] = _colsum(d_ref[...])

    return pl.pallas_call(
        body, name="ada_bwd", out_shape=(jax.ShapeDtypeStruct((D, nb), F32), jax.ShapeDtypeStruct((1, nb), F32)),
        compiler_params=pltpu.CompilerParams(vmem_limit_bytes=VMEM_LIMIT),
    )(c_act, dmod_cols)


def _adamw(w, g_slots, m, v, *, tr, name):
    R, C = w.shape
    n_slot = g_slots.shape[0]
    tr = min(tr, R)
    assert R % tr == 0, (name, R, tr)
    c1 = 1.0 / (1.0 - ADAM_B1 ** ADAM_STEP)
    c2 = 1.0 / (1.0 - ADAM_B2 ** ADAM_STEP)

    def body(w_ref, g_ref, m_ref, v_ref, go_ref, d_ref, mo_ref, vo_ref):
        g = g_ref[0].astype(F32)
        for i in range(1, n_slot):
            g = g + g_ref[i].astype(F32)
        mn = ADAM_B1 * m_ref[...] + (1.0 - ADAM_B1) * g
        vn = ADAM_B2 * v_ref[...] + (1.0 - ADAM_B2) * (g * g)
        go_ref[...] = g
        mo_ref[...] = mn
        vo_ref[...] = vn
        d_ref[...] = -ADAM_LR * ((mn * c1) / (jnp.sqrt(vn * c2) + ADAM_EPS) + ADAM_WD * w_ref[...])

    blk = pl.BlockSpec((tr, C), lambda i: (i, 0))
    return pl.pallas_call(
        body, name=name, grid=(R // tr,),
        in_specs=[blk, pl.BlockSpec((n_slot, tr, C), lambda i: (0, i, 0)), blk, blk],
        out_specs=(blk, blk, blk, blk),
        out_shape=tuple(jax.ShapeDtypeStruct((R, C), F32) for _ in range(4)),
        compiler_params=_cparams(1, big=True),
    )(w, g_slots, m, v)


def _adamw_many(ws, g_slots, g_owns, ms, vs, *, name):
    n = len(ws)
    c1 = 1.0 / (1.0 - ADAM_B1 ** ADAM_STEP)
    c2 = 1.0 / (1.0 - ADAM_B2 ** ADAM_STEP)

    def body(*refs):
        w_refs, g_refs, o_refs = refs[:n], refs[n:2 * n], refs[2 * n:3 * n]
        m_refs, v_refs = refs[3 * n:4 * n], refs[4 * n:5 * n]
        outs = refs[5 * n:]
        me = 4 * lax.axis_index("x") + 2 * lax.axis_index("y") + lax.axis_index("c")
        for i in range(n):
            own = o_refs[i][...]
            g = jnp.where(me == 0, own, g_refs[i][0])
            for d in range(1, N_DEV):
                g = g + jnp.where(me == d, own, g_refs[i][d])
            mn = ADAM_B1 * m_refs[i][...] + (1.0 - ADAM_B1) * g
            vn = ADAM_B2 * v_refs[i][...] + (1.0 - ADAM_B2) * (g * g)
            outs[i][...] = g
            outs[n + i][...] = -ADAM_LR * ((mn * c1) / (jnp.sqrt(vn * c2) + ADAM_EPS) + ADAM_WD * w_refs[i][...])
            outs[2 * n + i][...] = mn
            outs[3 * n + i][...] = vn

    res = pl.pallas_call(
        body, name=name, out_shape=tuple(jax.ShapeDtypeStruct(w.shape, F32) for _ in range(4) for w in ws),
        compiler_params=pltpu.CompilerParams(vmem_limit_bytes=VMEM_LIMIT),
    )(*ws, *g_slots, *g_owns, *ms, *vs)
    return res[:n], res[n:2 * n], res[2 * n:3 * n], res[3 * n:]


SMALL_NAMES = ("b_ada", "b_in", "b_conv", "w_rg_a", "b_rg_a", "w_rg_x", "b_rg_x", "lru_lambda", "w_sp", "b_sp",
               "ln_v_g", "ln_v_b", "ln1_g", "ln1_b", "ln2_g", "ln2_b")
BIG_NAMES = ("w_ada", "w_in", "w_conv", "w_o_lru", "w_o_sgu", "w_out", "w_up", "w_down")
WEIGHT_ORDER = ("w_ada", "b_ada", "w_in", "b_in", "w_conv", "b_conv", "w_rg_a", "b_rg_a", "w_rg_x", "b_rg_x",
                "lru_lambda", "w_sp", "b_sp", "ln_v_g", "ln_v_b", "w_o_lru", "w_o_sgu", "w_out", "ln1_g", "ln1_b",
                "w_up", "w_down", "ln2_g", "ln2_b")


def _pack_small(d):
    flat = jnp.concatenate([d[n].reshape(-1) for n in SMALL_NAMES])
    rows = -(-flat.shape[0] // LANES)
    rows = -(-rows // (N_DEV * SUBLANES)) * (N_DEV * SUBLANES)
    flat = jnp.pad(flat, (0, rows * LANES - flat.shape[0]))
    return flat.reshape(rows, LANES)


def _unpack_small(packed, like):
    flat = packed.reshape(-1)
    out, off = {}, 0
    for n in SMALL_NAMES:
        sz = like[n].size
        out[n] = flat[off:off + sz].reshape(like[n].shape)
        off += sz
    return out


def _blocked_cols(w2d):
    K, N = w2d.shape
    return jnp.transpose(w2d.reshape(K, N_DEV, N // N_DEV), (1, 0, 2))


def _unblock_cols(wb):
    n, K, nb = wb.shape
    return jnp.transpose(wb, (1, 0, 2)).reshape(K, n * nb)


def kernel(x, c, w_ada, b_ada, w_in, b_in, w_conv, b_conv, w_rg_a, b_rg_a, w_rg_x, b_rg_x, lru_lambda, w_sp, b_sp, ln_v_g, ln_v_b, w_o_lru, w_o_sgu, w_out, ln1_g, ln1_b, w_up, w_down, ln2_g, ln2_b, loss_target, m_w_ada, m_b_ada, m_w_in, m_b_in, m_w_conv, m_b_conv, m_w_rg_a, m_b_rg_a, m_w_rg_x, m_b_rg_x, m_lru_lambda, m_w_sp, m_b_sp, m_ln_v_g, m_ln_v_b, m_w_o_lru, m_w_o_sgu, m_w_out, m_ln1_g, m_ln1_b, m_w_up, m_w_down, m_ln2_g, m_ln2_b, v_w_ada, v_b_ada, v_w_in, v_b_in, v_w_conv, v_b_conv, v_w_rg_a, v_b_rg_a, v_w_rg_x, v_b_rg_x, v_lru_lambda, v_w_sp, v_b_sp, v_ln_v_g, v_ln_v_b, v_w_o_lru, v_w_o_sgu, v_w_out, v_ln1_g, v_ln1_b, v_w_up, v_w_down, v_ln2_g, v_ln2_b):
    W = dict(w_ada=w_ada, b_ada=b_ada, w_in=w_in, b_in=b_in, w_conv=w_conv, b_conv=b_conv, w_rg_a=w_rg_a,
             b_rg_a=b_rg_a, w_rg_x=w_rg_x, b_rg_x=b_rg_x, lru_lambda=lru_lambda, w_sp=w_sp, b_sp=b_sp,
             ln_v_g=ln_v_g, ln_v_b=ln_v_b, w_o_lru=w_o_lru, w_o_sgu=w_o_sgu, w_out=w_out, ln1_g=ln1_g, ln1_b=ln1_b,
             w_up=w_up, w_down=w_down, ln2_g=ln2_g, ln2_b=ln2_b)
    Mo = dict(w_ada=m_w_ada, b_ada=m_b_ada, w_in=m_w_in, b_in=m_b_in, w_conv=m_w_conv, b_conv=m_b_conv,
              w_rg_a=m_w_rg_a, b_rg_a=m_b_rg_a, w_rg_x=m_w_rg_x, b_rg_x=m_b_rg_x, lru_lambda=m_lru_lambda,
              w_sp=m_w_sp, b_sp=m_b_sp, ln_v_g=m_ln_v_g, ln_v_b=m_ln_v_b, w_o_lru=m_w_o_lru, w_o_sgu=m_w_o_sgu,
              w_out=m_w_out, ln1_g=m_ln1_g, ln1_b=m_ln1_b, w_up=m_w_up, w_down=m_w_down, ln2_g=m_ln2_g,
              ln2_b=m_ln2_b)
    Vo = dict(w_ada=v_w_ada, b_ada=v_b_ada, w_in=v_w_in, b_in=v_b_in, w_conv=v_w_conv, b_conv=v_b_conv,
              w_rg_a=v_w_rg_a, b_rg_a=v_b_rg_a, w_rg_x=v_w_rg_x, b_rg_x=v_b_rg_x, lru_lambda=v_lru_lambda,
              w_sp=v_w_sp, b_sp=v_b_sp, ln_v_g=v_ln_v_g, ln_v_b=v_ln_v_b, w_o_lru=v_w_o_lru, w_o_sgu=v_w_o_sgu,
              w_out=v_w_out, ln1_g=v_ln1_g, ln1_b=v_ln1_b, w_up=v_w_up, w_down=v_w_down, ln2_g=v_ln2_g,
              ln2_b=v_ln2_b)

    Bl, S, D = x.shape
    T = Bl * S
    lw = b_conv.shape[-1]
    sw = ln_v_g.shape[-1]
    din = b_in.shape[-1]
    dff = w_up.shape[-1] * N_DEV
    ts = min(512, S)
    tmix = min(256, S)
    trow = min(512, S)

    c_pad = jnp.pad(c, ((0, SUBLANES - Bl), (0, 0)))
    c_g, wconv_g = _exchange([c_pad, w_conv[0]], True, "xchg_c")
    wconv_full = _unblock_cols(wconv_g)
    c_act, modcols = _ada_fwd(c_g.reshape(N_DEV * SUBLANES, D), w_ada[0])
    (mod_slots,) = _exchange([modcols.reshape(N_DEV, SUBLANES, -1)], False, "xchg_mod")

    nbw = din // N_DEV // WIN_PARTS
    wnames = tuple("win%d" % q for q in range(WIN_PARTS)) + ("wol", "wos", "wout", "wup", "wdown")
    w_in_b = w_in[0].astype(BF16)
    shards = [w_in_b[:, q * nbw:(q + 1) * nbw] for q in range(WIN_PARTS)] + [
        w_o_lru[0].astype(BF16), w_o_sgu[0].astype(BF16), w_out[0].astype(BF16), w_up[0].astype(BF16),
        w_down[0].astype(BF16)]
    col_sharded = [True] * WIN_PARTS + [False, True, False, True, False]
    g_send, g_recv, g_src, g_land, g_tok = _xstart(shards, True, mod_slots, "gather_start", cols=col_sharded)
    gidx = {n: i for i, n in enumerate(wnames)}

    def gathered(n, after):
        i = gidx[n]
        return _xwait(g_src[i], g_land[i], g_send[i], g_recv[i], after, True, "gather_wait_" + n, col=col_sharded[i])

    mod = _unblock_cols(mod_slots)[:Bl] + (b_ada + g_tok[0, 0])
    sh1, sc1, gt1, sh2, sc2, gt2 = [mod[:, i * D:(i + 1) * D].reshape(Bl, 1, D) for i in range(6)]

    wa_b, wx_b = w_rg_a[0].astype(BF16), w_rg_x[0].astype(BF16)
    b_sp_t = jnp.transpose(b_sp[0])
    small_mix = (wconv_full, b_conv, wa_b, b_rg_a, wx_b, b_rg_x, lru_lambda, w_sp[0], b_sp_t, ln_v_g, ln_v_b)

    h = _modulate(x, sc1, sh1, ts)
    proj, win_parts = None, []
    for q in range(WIN_PARTS):
        wq = gathered("win%d" % q, h if q == 0 else proj)
        win_parts.append(wq)
        proj = _mm(h.reshape(T, D), wq, mode="nn", tm=2048, tn=nbw, tk=D, outs=[BF16], extras=[(b_in, "row")],
                   epilogue=lambda acc, ex: (acc + ex[0],), scatter=(WIN_PARTS, q, din), into=proj,
                   name="mm_proj%d" % q)
    Win = jnp.stack([wq.reshape(D, N_DEV, nbw) for wq in win_parts], axis=2).reshape(D, din)
    proj3 = proj.reshape(Bl, S, din)
    hs, ya_pre, ysgu, *lru_saved = _mix_fwd(proj3, *small_mix, tm=tmix, lw=lw, sw=sw)
    Wol = gathered("wol", ya_pre).reshape(lw, D)
    Wos = gathered("wos", ysgu)
    y_a = _mm(ya_pre.reshape(T, lw), Wol, mode="nn", tm=2048, tn=D, tk=lw, outs=[BF16], name="mm_ya")
    x2d, tgt2d = x.reshape(T, D), loss_target.reshape(T, D)
    gate_cb = (din - 2 * D) // D

    def ep_merge(y_b, v):
        ya, ga, gb = [t.astype(F32) for t in v]
        yb = y_b.astype(BF16).astype(F32)
        return [yb, _sigmoid(ga) * ya + _sigmoid(gb) * yb]

    y_b, merged = _mm_rows(ysgu.reshape(T, sw), Wos, mode="nn", tm=trow, seq=S,
                           ins=[("tile", y_a), ("tilecol", proj, D, gate_cb), ("tilecol", proj, D, gate_cb + 1)],
                           outs=[("tile", BF16, D), ("tile", BF16, D)], epilogue=ep_merge, name="mm_yb_merge")
    Wout = gathered("wout", merged).reshape(D, D)

    def ep_ln1(mix_acc, v):
        x_, gt, g, b, sc, sh = v
        mixr = mix_acc.astype(BF16).astype(F32)
        xhat, _ = _ln_stats(ALPHA * x_ + (1.0 + gt) * mixr)
        x1_ = xhat * g + b
        return [mixr, x1_, x1_ * (1.0 + sc) + sh]

    mix, x1, h2 = _mm_rows(merged, Wout, mode="nn", tm=trow, seq=S,
                           ins=[("tile", x2d), ("brow", gt1), ("row", ln1_g), ("row", ln1_b), ("brow", sc2),
                                ("brow", sh2)],
                           outs=[("tile", BF16, D), ("tile", F32, D), ("tile", BF16, D)], epilogue=ep_ln1,
                           name="mm_mix_ln1")
    Wup = gathered("wup", h2)
    act = _mm(h2, Wup, mode="nn", tm=2048, tn=1024, tk=D, outs=[BF16],
              epilogue=lambda acc, ex: (jnp.square(jnp.maximum(acc, 0.0)),), name="mm_up")
    Wdown = gathered("wdown", act).reshape(dff, D)

    def ep_ln2(f_acc, v):
        x1_, t_, gt, g, b = v
        xhat, rstd = _ln_stats(ALPHA * x1_ + (1.0 + gt) * f_acc)
        err = xhat * g + b - t_
        loss_t = 0.5 * jnp.sum(jnp.mean(err * err, axis=-1, keepdims=True))
        dy = err * (1.0 / D)
        dz = _ln_bwd(dy, xhat, rstd, g)
        return [dz * (1.0 + gt), ALPHA * dz, _colsum(dz * f_acc), _colsum(dy * xhat), _colsum(dy), loss_t]

    df2, dx1p, dgt2, dg2, db2, loss_part = _mm_rows(
        act, Wdown, mode="nn", tm=trow, seq=S,
        ins=[("tile", x1), ("tile", tgt2d), ("brow", gt2), ("row", ln2_g), ("row", ln2_b)],
        outs=[("tile", BF16, D), ("tile", F32, D), ("acc_brow", D), ("acc_row", D), ("acc_row", D), ("acc_scalar",)],
        epilogue=ep_ln2, name="mm_down_ln2")
    loss = lax.psum(loss_part[0, 0], ("x", "y", "c"))

    def send_grads(parts, name):
        snd, rcv, src, land, tok = _xstart(parts, False, None, name + "_start")
        return [(src[i], land[i], snd[i], rcv[i]) for i in range(len(parts))], tok

    dup = _mm(df2, Wdown, mode="nt", tm=2048, tn=1024, tk=D, outs=[BF16], extras=[(act, "tile")],
              epilogue=lambda acc, ex: (acc * (2.0 * jnp.sqrt(ex[0].astype(F32))),), name="mm_dup")
    g_wdown = _mm(act, df2, mode="tn", tm=1024, tn=D, tk=2048, outs=[BF16], name="mm_gwdown")
    (x_wdown,), tok = send_grads([g_wdown.reshape(N_DEV, dff // N_DEV, D)], "gx_wdown")
    def ep_ln1_bwd(dh2, v):
        dx1p_, x1_, x_, mix_, sc, gt, g = v
        mixv = mix_.astype(F32)
        dx1 = dx1p_ + dh2 * (1.0 + sc)
        xhat, rstd = _ln_stats(ALPHA * x_ + (1.0 + gt) * mixv)
        dz = _ln_bwd(dx1, xhat, rstd, g)
        return [ALPHA * dz, dz * (1.0 + gt), _colsum(dh2 * x1_), _colsum(dh2), _colsum(dz * mixv),
                _colsum(dx1 * xhat), _colsum(dx1)]

    dxp, dmix, dsc2, dsh2, dgt1, dg1, db1 = _mm_rows(
        dup, Wup, mode="nt", tm=trow, seq=S, tok=tok,
        ins=[("tile", dx1p), ("tile", x1), ("tile", x2d), ("tile", mix), ("brow", sc2), ("brow", gt1), ("row", ln1_g)],
        outs=[("tile", F32, D), ("tile", BF16, D), ("acc_brow", D), ("acc_brow", D), ("acc_brow", D), ("acc_row", D),
              ("acc_row", D)],
        epilogue=ep_ln1_bwd, name="mm_dh2_ln1b")
    g_wup = _mm(h2, dup, mode="tn", tm=D, tn=1024, tk=2048, outs=[BF16], nb=dff // N_DEV, name="mm_gwup")
    (x_wup,), tok = send_grads([g_wup], "gx_wup")

    def ep_merge_bwd(dm, v):
        ya, yb, ga, gb = [t.astype(F32) for t in v]
        sa, sb = _sigmoid(ga), _sigmoid(gb)
        dg = jnp.concatenate([dm * ya * sa * (1.0 - sa), dm * yb * sb * (1.0 - sb)], axis=1)
        return [dm * sa, dm * sb, dg, _colsum(dg)]

    dy_a, dy_b, dproj, dbin_hi = _mm_rows(
        dmix, Wout, mode="nt", tm=trow, seq=S, tok=tok,
        ins=[("tile", y_a), ("tile", y_b), ("tilecol", proj, D, gate_cb), ("tilecol", proj, D, gate_cb + 1)],
        outs=[("tile", BF16, D), ("tile", BF16, D), ("tilecol", BF16, 2 * D, gate_cb // 2, din), ("acc_row", 2 * D)],
        epilogue=ep_merge_bwd, name="mm_dmerged_mb")
    g_wout = _mm(merged, dmix, mode="tn", tm=D, tn=D, tk=2048, outs=[BF16], name="mm_gwout")
    (x_wout,), tok = send_grads([g_wout.reshape(N_DEV, D // N_DEV, D)], "gx_wout")
    dya_pre = _mm(dy_a, Wol, mode="nt", tm=2048, tn=lw, tk=D, outs=[F32], tok=tok, name="mm_dya")
    dysgu = _mm(dy_b, Wos, mode="nt", tm=2048, tn=sw, tk=D, outs=[F32], name="mm_dys")
    g_wol = _mm(ya_pre.reshape(T, lw), dy_a, mode="tn", tm=lw, tn=D, tk=2048, outs=[BF16], name="mm_gwol")
    g_wos = _mm(ysgu.reshape(T, sw), dy_b, mode="tn", tm=sw, tn=D, tk=2048, outs=[BF16], nb=D // N_DEV,
                name="mm_gwos")
    (x_wol, x_wos), tok = send_grads([g_wol.reshape(N_DEV, lw // N_DEV, D), g_wos], "gx_wo")
    small_mix_b = (wconv_full, b_conv + tok[0, 0]) + small_mix[2:]
    (dproj, dbin_lo, g_wconv, g_bconv, g_wa, g_ba, g_wx, g_bx, g_lam, g_wsp, g_bsp_t, g_lvg, g_lvb) = _mix_bwd(
        proj3, hs, dya_pre.reshape(Bl, S, lw), dysgu.reshape(Bl, S, sw), dproj.reshape(Bl, S, din), lru_saved,
        *small_mix_b, tm=tmix, lw=lw, sw=sw)
    dproj2 = dproj.reshape(T, din)
    small_names = [n for n in SMALL_NAMES if n != "b_ada"]
    small_g = dict(b_in=jnp.concatenate([dbin_lo, dbin_hi], axis=-1), b_conv=g_bconv, w_rg_a=g_wa[None], b_rg_a=g_ba,
                   w_rg_x=g_wx[None], b_rg_x=g_bx, lru_lambda=g_lam, w_sp=g_wsp[None],
                   b_sp=jnp.transpose(g_bsp_t)[None], ln_v_g=g_lvg, ln_v_b=g_lvb, ln1_g=dg1, ln1_b=db1, ln2_g=dg2,
                   ln2_b=db2)
    gs_snd, gs_rcv, gs_src, gs_land, tok_s = _xstart([small_g[n] for n in small_names], True, None, "gsmall_start",
                                                      fill_own=False)
    g_win = _mm(h.reshape(T, D), dproj2, mode="tn", tm=D, tn=din // 4, tk=2048, outs=[BF16], nb=din // N_DEV,
                tok=tok_s, name="mm_gwin")
    (x_win,), tok = send_grads([g_win], "gx_win")

    def ep_final(dh, v):
        dxp_, x_, sc = v
        return [dxp_ + dh * (1.0 + sc), _colsum(dh * x_), _colsum(dh)]

    grad_x, dsc1, dsh1 = _mm_rows(dproj2, Win, mode="nt", tm=trow, seq=S, tok=tok,
                                  ins=[("tile", dxp), ("tile", x2d), ("brow", sc1)],
                                  outs=[("tile", F32, D), ("acc_brow", D), ("acc_brow", D)], epilogue=ep_final,
                                  name="mm_dh_final")
    grad_x = grad_x.reshape(Bl, S, D)

    dmod = jnp.concatenate([dsh1, dsc1, dgt1, dsh2, dsc2, dgt2], axis=-1).reshape(Bl, 6 * D)
    dmod_b = _blocked_cols(jnp.pad(dmod, ((0, SUBLANES - Bl), (0, 0))))
    dmod_s, gwconv_s = _exchange([dmod_b, _blocked_cols(g_wconv)], False, "xchg_dmod")
    g_wada, g_bada_mine = _ada_bwd(c_act, dmod_s.reshape(N_DEV * SUBLANES, -1))
    (g_bada_all,) = _exchange([g_bada_mine], True, "xchg_bada")

    gwdown_s = _xwait(*x_wdown, g_bada_all, False, "gx_wdown_wait")
    gwup_s = _xwait(*x_wup, g_bada_all, False, "gx_wup_wait")
    gwout_s = _xwait(*x_wout, g_bada_all, False, "gx_wout_wait")
    gwol_s = _xwait(*x_wol, g_bada_all, False, "gx_wol_wait")
    gwos_s = _xwait(*x_wos, g_bada_all, False, "gx_wos_wait")
    gwin_s = _xwait(*x_win, g_bada_all, False, "gx_win_wait")
    gs_own, gs_slots = _xwait_many(gs_src, gs_land, gs_snd, gs_rcv, g_bada_all, "gsmall_wait")
    out_g, out_d, out_m, out_v = {}, {}, {}, {}

    def adam(name, g_slots, tr):
        shp = W[name].shape
        w2, m2, v2 = [t.reshape(g_slots.shape[1:]) for t in (W[name], Mo[name], Vo[name])]
        g, d, mn, vn = _adamw(w2, g_slots, m2, v2, tr=tr, name="adam_" + name)
        out_g[name], out_d[name], out_m[name], out_v[name] = [t.reshape(shp) for t in (g, d, mn, vn)]

    adam("w_ada", g_wada[None], 256)
    adam("b_ada", g_bada_all.reshape(1, 1, 6 * D), 1)
    adam("w_in", gwin_s, 256)
    adam("w_conv", gwconv_s, 8)
    adam("w_o_lru", gwol_s, 160)
    adam("w_o_sgu", gwos_s, 256)
    adam("w_out", gwout_s, 128)
    adam("w_up", gwup_s, 256)
    adam("w_down", gwdown_s, 256)
    res_small = _adamw_many([W[n] for n in small_names], gs_slots, gs_own, [Mo[n] for n in small_names],
                            [Vo[n] for n in small_names], name="adam_small")
    for dst, vals in zip((out_g, out_d, out_m, out_v), res_small):
        dst.update(dict(zip(small_names, vals)))

    return (loss, grad_x, *[out_g[n] for n in WEIGHT_ORDER], *[out_d[n] for n in WEIGHT_ORDER],
            *[out_m[n] for n in WEIGHT_ORDER], *[out_v[n] for n in WEIGHT_ORDER])
```

```python
import functools
import math

import jax
import jax.numpy as jnp
from jax import lax
from jax.experimental import pallas as pl
from jax.experimental.pallas import tpu as pltpu

N_DEV = 8
LN_EPS = 1e-5
LRU_C = 8.0
CHUNK = 64
SGU_BLOCK = 128
ALPHA = 2.0 ** 0.25
ADAM_LR = 0.001
ADAM_B1 = 0.9
ADAM_B2 = 0.999
ADAM_EPS = 1e-08
ADAM_WD = 0.01
ADAM_STEP = 10
GELU_K0 = math.sqrt(2.0 / math.pi)
GELU_K1 = 0.044715

SUBLANES = 8
LANES = 128
VMEM_LIMIT = 56 * 1024 * 1024
WIN_PARTS = 3

F32 = jnp.float32
BF16 = jnp.bfloat16
MESH = pl.DeviceIdType.MESH


def _cparams(n_axes, big=False):
    return pltpu.CompilerParams(dimension_semantics=("arbitrary",) * n_axes,
                                vmem_limit_bytes=VMEM_LIMIT if big else None)


def _sigmoid(x):
    return 0.5 * jnp.tanh(0.5 * x) + 0.5


def _gelu(x):
    t = jnp.tanh(GELU_K0 * (x + GELU_K1 * (x * x * x)))
    return 0.5 * x * (1.0 + t)


def _gelu_and_grad(x):
    x2 = x * x
    t = jnp.tanh(GELU_K0 * (x + GELU_K1 * (x2 * x)))
    g = 0.5 * x * (1.0 + t)
    dg = 0.5 * (1.0 + t) + 0.5 * x * (1.0 - t * t) * (GELU_K0 * (1.0 + 3.0 * GELU_K1 * x2))
    return g, dg


def _expm1(x):
    p = x * (1.0 + x * (1.0 / 2.0 + x * (1.0 / 6.0 + x * (1.0 / 24.0 + x * (1.0 / 120.0)))))
    return jnp.where(jnp.abs(x) < 0.0625, p, jnp.exp(x) - 1.0)


def _log1p_pos(e):
    p = e * (1.0 - e * (1.0 / 2.0) + e * e * (1.0 / 3.0) - e * e * e * (1.0 / 4.0))
    return jnp.where(e < 1e-2, p, jnp.log(1.0 + e))


def _ln_stats(z):
    mu = jnp.mean(z, axis=-1, keepdims=True)
    zc = z - mu
    var = jnp.mean(zc * zc, axis=-1, keepdims=True)
    rstd = lax.rsqrt(var + LN_EPS)
    return zc * rstd, rstd


def _ln_bwd(dy, xhat, rstd, g):
    dxh = dy * g
    m1 = jnp.mean(dxh, axis=-1, keepdims=True)
    m2 = jnp.mean(dxh * xhat, axis=-1, keepdims=True)
    return rstd * (dxh - m1 - xhat * m2)


def _colsum(v):
    return jnp.sum(v, axis=0, keepdims=True)


def _first_step():
    return jnp.logical_and(pl.program_id(0) == 0, pl.program_id(1) == 0)


def _exchange(arrs, gather, name):
    n = len(arrs)
    n_peer = N_DEV - 1

    def body(*refs):
        ins, outs = refs[:n], refs[n:2 * n]
        send_sems, recv_sems, loc_sems = refs[2 * n:]
        x, y, c = lax.axis_index("x"), lax.axis_index("y"), lax.axis_index("c")
        me = 4 * x + 2 * y + c
        started = []
        for a in range(n):
            src_me = ins[a] if gather else ins[a].at[me]
            lc = pltpu.make_async_copy(src_me, outs[a].at[me], loc_sems.at[a])
            lc.start()
            started.append((lc, None))
        for p in range(1, N_DEV):
            px, py, pc = x ^ ((p >> 2) & 1), y ^ ((p >> 1) & 1), c ^ (p & 1)
            peer = 4 * px + 2 * py + pc
            for a in range(n):
                k = a * n_peer + (p - 1)
                src = ins[a] if gather else ins[a].at[peer]
                cp = pltpu.make_async_remote_copy(src_ref=src, dst_ref=outs[a].at[me],
                                                  send_sem=send_sems.at[k], recv_sem=recv_sems.at[k],
                                                  device_id=(px, py, pc), device_id_type=MESH)
                cp.start()
                rc = pltpu.make_async_remote_copy(src_ref=src, dst_ref=outs[a].at[peer],
                                                  send_sem=send_sems.at[k], recv_sem=recv_sems.at[k],
                                                  device_id=(px, py, pc), device_id_type=MESH)
                started.append((cp, rc))
        for cp, rc in started:
            if rc is None:
                cp.wait()
            else:
                cp.wait_send()
                rc.wait_recv()

    hbm = pl.BlockSpec(memory_space=pltpu.HBM)
    out_shape = tuple(
        jax.ShapeDtypeStruct(((N_DEV,) + a.shape) if gather else a.shape, a.dtype) for a in arrs)
    return pl.pallas_call(
        body, name=name, out_shape=out_shape,
        in_specs=[hbm] * n, out_specs=tuple([hbm] * n),
        scratch_shapes=[pltpu.SemaphoreType.DMA((n * n_peer,)), pltpu.SemaphoreType.DMA((n * n_peer,)),
                        pltpu.SemaphoreType.DMA((n,))],
        compiler_params=pltpu.CompilerParams(has_side_effects=True),
    )(*arrs)


_HBM = pl.BlockSpec(memory_space=pltpu.HBM)
_SEM = pl.BlockSpec(memory_space=pltpu.SEMAPHORE)
_EFFECT = pltpu.SideEffectType.DATAFLOW_SIDE_EFFECTING


def _peer_of(p):
    x, y, c = lax.axis_index("x"), lax.axis_index("y"), lax.axis_index("c")
    px, py, pc = x ^ ((p >> 2) & 1), y ^ ((p >> 1) & 1), c ^ (p & 1)
    return (px, py, pc), 4 * px + 2 * py + pc


def _slot(land_ref, idx, width):
    if width is None:
        return land_ref.at[idx]
    return land_ref.at[:, pl.ds(pl.multiple_of(idx * width, LANES), width)]


def _xstart(srcs, gather, after, name, cols=None, fill_own=False):
    n = len(srcs)
    cols = cols or [False] * n
    widths = [t.shape[1] if cols[a] else None for a, t in enumerate(srcs)]
    me_out = 4 * lax.axis_index("x") + 2 * lax.axis_index("y") + lax.axis_index("c")
    lands = []
    for a, t in enumerate(srcs):
        if cols[a]:
            zone, own, at = lax.empty((t.shape[0], N_DEV * t.shape[1]), t.dtype), t, (0, me_out * t.shape[1])
        elif gather:
            zone, own, at = lax.empty((N_DEV,) + t.shape, t.dtype), t[None], (me_out,) + (0,) * t.ndim
        else:
            zone, own = lax.empty(t.shape, t.dtype), lax.dynamic_index_in_dim(t, me_out, 0, keepdims=True)
            at = (me_out,) + (0,) * (t.ndim - 1)
        lands.append(lax.dynamic_update_slice(zone, own, at) if fill_own else zone)
    n_after = 0 if after is None else 1

    def body(*refs):
        src_refs, land_refs = refs[:n], refs[n:2 * n]
        refs = refs[n_after:]
        send_sems, recv_sems = refs[2 * n:3 * n], refs[3 * n:4 * n]
        token = refs[6 * n]
        me = 4 * lax.axis_index("x") + 2 * lax.axis_index("y") + lax.axis_index("c")
        for a in range(n):
            for p in range(1, N_DEV):
                dev, peer = _peer_of(p)
                pltpu.make_async_remote_copy(
                    src_ref=src_refs[a] if gather else src_refs[a].at[peer], dst_ref=_slot(land_refs[a], me, widths[a]),
                    send_sem=send_sems[a].at[p - 1], recv_sem=recv_sems[a].at[p - 1],
                    device_id=dev, device_id_type=MESH).start()
        token[...] = jnp.zeros_like(token)

    sems = tuple(pltpu.SemaphoreType.DMA((N_DEV - 1,)) for _ in range(2 * n))
    thru = tuple(pltpu.HBM(t.shape, t.dtype) for t in list(srcs) + list(lands))
    res = pl.pallas_call(
        body, name=name,
        out_shape=sems + thru + (jax.ShapeDtypeStruct((SUBLANES, LANES), F32),),
        in_specs=[_HBM] * (2 * n) + [pl.BlockSpec(memory_space=pl.ANY)] * n_after,
        out_specs=tuple([_SEM] * (2 * n) + [_HBM] * (2 * n) + [pl.BlockSpec(memory_space=pltpu.VMEM)]),
        input_output_aliases={i: 2 * n + i for i in range(2 * n)},
        compiler_params=pltpu.CompilerParams(has_side_effects=_EFFECT),
    )(*[pltpu.with_memory_space_constraint(t, pltpu.HBM) for t in list(srcs) + list(lands)],
      *([after] if n_after else []))
    return res[:n], res[n:2 * n], res[2 * n:3 * n], res[3 * n:4 * n], res[4 * n]


def _xwait(src, land, send_sem, recv_sem, after, gather, name, col=False):
    width = src.shape[1] if col else None

    def body(src_ref, land_ref, send_ref, recv_ref, after_ref, src_dead, land_out):
        del after_ref, src_dead, land_out
        for p in range(1, N_DEV):
            dev, peer = _peer_of(p)
            cp = pltpu.make_async_remote_copy(
                src_ref=src_ref if gather else src_ref.at[peer], dst_ref=_slot(land_ref, peer, width),
                send_sem=send_ref.at[p - 1], recv_sem=recv_ref.at[p - 1], device_id=dev, device_id_type=MESH)
            cp.wait_send()
            cp.wait_recv()

    src_done, landed = pl.pallas_call(
        body, name=name, out_shape=(pltpu.HBM(src.shape, src.dtype), pltpu.HBM(land.shape, land.dtype)),
        in_specs=[_HBM, _HBM, _SEM, _SEM, pl.BlockSpec(memory_space=pl.ANY)], out_specs=(_HBM, _HBM),
        input_output_aliases={0: 0, 1: 1},
        compiler_params=pltpu.CompilerParams(has_side_effects=_EFFECT),
    )(src, land, send_sem, recv_sem, after)
    me = 4 * lax.axis_index("x") + 2 * lax.axis_index("y") + lax.axis_index("c")
    return _place_own(landed, src_done, me, col, gather, name + "_own")


def _place_own(zone, src, me, col, gather, name):
    if col:
        R, C = src.shape
        src_spec = lambda tr: pl.BlockSpec((tr, C), lambda i, me_ref: (i, 0))
        out_spec = lambda tr: pl.BlockSpec((tr, C), lambda i, me_ref: (i, me_ref[0]))
    else:
        R, C = zone.shape[1:]
        src_spec = ((lambda tr: pl.BlockSpec((tr, C), lambda i, me_ref: (i, 0))) if gather else
                    (lambda tr: pl.BlockSpec((None, tr, C), lambda i, me_ref: (me_ref[0], i, 0))))
        out_spec = lambda tr: pl.BlockSpec((None, tr, C), lambda i, me_ref: (me_ref[0], i, 0))
    tr = R if R <= 512 else 256
    assert R % tr == 0, (name, R, tr)

    def body(me_ref, src_ref, zone_ref, out_ref):
        del me_ref, zone_ref
        out_ref[...] = src_ref[...]

    return pl.pallas_call(
        body, name=name, out_shape=jax.ShapeDtypeStruct(zone.shape, zone.dtype),
        grid_spec=pltpu.PrefetchScalarGridSpec(
            num_scalar_prefetch=1, grid=(R // tr,),
            in_specs=[src_spec(tr), pl.BlockSpec(memory_space=pl.ANY)], out_specs=out_spec(tr)),
        input_output_aliases={2: 0},
    )(jnp.reshape(me, (1,)).astype(jnp.int32), src, zone)


def _xwait_many(srcs, lands, send_sems, recv_sems, after, name):
    n = len(srcs)

    def body(*refs):
        src_refs, land_refs = refs[:n], refs[n:2 * n]
        snd, rcv = refs[2 * n:3 * n], refs[3 * n:4 * n]
        for a in range(n):
            for p in range(1, N_DEV):
                dev, peer = _peer_of(p)
                cp = pltpu.make_async_remote_copy(
                    src_ref=src_refs[a], dst_ref=land_refs[a].at[peer], send_sem=snd[a].at[p - 1],
                    recv_sem=rcv[a].at[p - 1], device_id=dev, device_id_type=MESH)
                cp.wait_send()
                cp.wait_recv()

    res = pl.pallas_call(
        body, name=name, out_shape=tuple(pltpu.HBM(t.shape, t.dtype) for t in list(srcs) + list(lands)),
        in_specs=[_HBM] * (2 * n) + [_SEM] * (2 * n) + [pl.BlockSpec(memory_space=pl.ANY)],
        out_specs=tuple([_HBM] * (2 * n)), input_output_aliases={i: i for i in range(2 * n)},
        compiler_params=pltpu.CompilerParams(has_side_effects=_EFFECT),
    )(*srcs, *lands, *send_sems, *recv_sems, after)
    return res[:n], res[n:]


def _mm(a, b, *, mode, tm, tn, tk, outs, epilogue=None, extras=(), nb=None, tok=None, scatter=None, into=None, name):
    if mode == "nn":
        (M, K), (_, N) = a.shape, b.shape
    elif mode == "nt":
        (M, K), (N, _) = a.shape, b.shape
    else:
        (K, M), (_, N) = a.shape, b.shape
    tm, tn, tk = min(tm, M), min(tn, N), min(tk, K)
    assert M % tm == 0 and N % tn == 0 and K % tk == 0, (name, M, N, K, tm, tn, tk)
    if mode == "nn":
        a_spec = pl.BlockSpec((tm, tk), lambda i, j, k: (i, k))
        b_spec = pl.BlockSpec((tk, tn), lambda i, j, k: (k, j))
        dims = (((1,), (0,)), ((), ()))
    elif mode == "nt":
        a_spec = pl.BlockSpec((tm, tk), lambda i, j, k: (i, k))
        b_spec = pl.BlockSpec((tn, tk), lambda i, j, k: (j, k))
        dims = (((1,), (1,)), ((), ()))
    else:
        a_spec = pl.BlockSpec((tk, tm), lambda i, j, k: (k, i))
        b_spec = pl.BlockSpec((tk, tn), lambda i, j, k: (k, j))
        dims = (((0,), (0,)), ((), ()))
    nk = K // tk
    n_ex, n_out = len(extras), len(outs)
    n_tok = 0 if tok is None else 1
    nbytes = lambda d: jnp.dtype(d).itemsize
    vmem_est = (2 * (tm * tk * nbytes(a.dtype) + tk * tn * nbytes(b.dtype)
                     + sum(tm * tn * nbytes(e.dtype) for e, kind in extras if kind == "tile")
                     + sum(tm * tn * nbytes(d) for d in outs)) + tm * tn * 4)
    assert vmem_est <= VMEM_LIMIT, (name, vmem_est)
    if epilogue is None:
        epilogue = lambda acc, ex: tuple(acc.astype(d) for d in outs)

    n_into = 0 if into is None else 1

    def body(a_ref, b_ref, *refs):
        refs = refs[n_tok:]
        ex_refs, out_refs = refs[:n_ex], refs[n_ex + n_into:n_ex + n_into + n_out]

        def finish(acc):
            res = epilogue(acc, [r[...] for r in ex_refs])
            for o_ref, v in zip(out_refs, res):
                if nb is None:
                    o_ref[...] = v.astype(o_ref.dtype)
                else:
                    for q in range(tn // nb):
                        o_ref[q] = v[:, q * nb:(q + 1) * nb].astype(o_ref.dtype)

        part = lax.dot_general(a_ref[...], b_ref[...], dims, preferred_element_type=F32)
        if nk == 1:
            finish(part)
        else:
            acc_ref = refs[n_ex + n_into + n_out]
            k = pl.program_id(2)

            @pl.when(k == 0)
            def _():
                acc_ref[...] = part

            @pl.when(k > 0)
            def _():
                acc_ref[...] += part

            @pl.when(k == nk - 1)
            def _():
                finish(acc_ref[...])

    col = (lambda j: j) if scatter is None else (lambda j: scatter[0] * j + scatter[1])
    ex_specs = [pl.BlockSpec((tm, tn), lambda i, j, k: (i, j)) if kind == "tile"
                else pl.BlockSpec((1, tn), lambda i, j, k: (0, col(j))) for _, kind in extras]
    if nb is not None:
        assert tn % nb == 0, (name, tn, nb)
        o_spec = pl.BlockSpec((tn // nb, tm, nb), lambda i, j, k: (j, i, 0))
        o_shape = (N // nb, M, nb)
    else:
        o_spec = pl.BlockSpec((tm, tn), lambda i, j, k: (i, col(j)))
        o_shape = (M, N if scatter is None else scatter[2])
    assert n_into == 0 or n_out == 1
    res = pl.pallas_call(
        body, name=name, grid=(M // tm, N // tn, nk),
        in_specs=[a_spec, b_spec] + [pl.BlockSpec((SUBLANES, LANES), lambda i, j, k: (0, 0))] * n_tok + ex_specs
                 + [pl.BlockSpec(memory_space=pl.ANY)] * n_into,
        out_specs=tuple([o_spec] * n_out),
        out_shape=tuple(jax.ShapeDtypeStruct(o_shape, d) for d in outs),
        input_output_aliases={2 + n_tok + n_ex: 0} if n_into else {},
        scratch_shapes=[pltpu.VMEM((tm, tn), F32)] if nk > 1 else [],
        compiler_params=_cparams(3, big=True),
    )(a, b, *([tok] if n_tok else []), *[e for e, _ in extras], *([into] if n_into else []))
    return res[0] if n_out == 1 else res


def _mm_rows(a, b, *, mode, tm, seq, ins, outs, epilogue, tok=None, name):
    M, K = a.shape
    b_parts = list(b) if isinstance(b, (list, tuple)) else [b]
    n_part = len(b_parts)
    assert n_part == 1 or mode == "nt"
    N = b_parts[0].shape[1] if mode == "nn" else b_parts[0].shape[0]
    tm = min(tm, M)
    assert M % tm == 0 and seq % tm == 0, (name, M, seq, tm)
    tpb = seq // tm
    n_b = M // seq
    dims = (((1,), (0,)), ((), ())) if mode == "nn" else (((1,), (1,)), ((), ()))
    n_tok = 0 if tok is None else 1
    n_in, n_out = len(ins), len(outs)

    in_specs, in_arrs = [], []
    for spec in ins:
        kind, arr = spec[0], spec[1]
        in_arrs.append(arr)
        if kind == "tile":
            in_specs.append(pl.BlockSpec((tm, arr.shape[1]), lambda i: (i, 0)))
        elif kind == "tilecol":
            in_specs.append(pl.BlockSpec((tm, spec[2]), lambda i, cb=spec[3]: (i, cb)))
        elif kind == "row":
            in_specs.append(pl.BlockSpec(arr.shape, lambda i: (0, 0)))
        else:
            in_specs.append(pl.BlockSpec((None, 1, arr.shape[2]), lambda i: (i // tpb, 0, 0)))
    out_specs, out_shapes = [], []
    for spec in outs:
        kind = spec[0]
        if kind == "tile":
            out_specs.append(pl.BlockSpec((tm, spec[2]), lambda i: (i, 0)))
            out_shapes.append(jax.ShapeDtypeStruct((M, spec[2]), spec[1]))
        elif kind == "tilecol":
            out_specs.append(pl.BlockSpec((tm, spec[2]), lambda i, cb=spec[3]: (i, cb)))
            out_shapes.append(jax.ShapeDtypeStruct((M, spec[4]), spec[1]))
        elif kind == "acc_row":
            out_specs.append(pl.BlockSpec((1, spec[1]), lambda i: (0, 0)))
            out_shapes.append(jax.ShapeDtypeStruct((1, spec[1]), F32))
        elif kind == "acc_brow":
            out_specs.append(pl.BlockSpec((None, 1, spec[1]), lambda i: (i // tpb, 0, 0)))
            out_shapes.append(jax.ShapeDtypeStruct((n_b, 1, spec[1]), F32))
        else:
            out_specs.append(pl.BlockSpec((SUBLANES, LANES), lambda i: (0, 0)))
            out_shapes.append(jax.ShapeDtypeStruct((SUBLANES, LANES), F32))

    def body(a_ref, *refs):
        b_refs, refs = refs[:n_part], refs[n_part + n_tok:]
        in_refs, out_refs = refs[:n_in], refs[n_in:n_in + n_out]
        i = pl.program_id(0)
        if n_part == 1:
            prod = lax.dot_general(a_ref[...], b_refs[0][...], dims, preferred_element_type=F32)
        else:
            w = b_parts[0].shape[1] // N_DEV
            prod = None
            for q in range(n_part):
                a_q = jnp.concatenate([a_ref[:, (n_part * j + q) * w:(n_part * j + q + 1) * w] for j in range(N_DEV)],
                                      axis=1)
                pq = lax.dot_general(a_q, b_refs[q][...], dims, preferred_element_type=F32)
                prod = pq if prod is None else prod + pq
        vals = epilogue(prod, [r[...] for r in in_refs])
        for spec, o_ref, v in zip(outs, out_refs, vals):
            kind = spec[0]
            if kind in ("tile", "tilecol"):
                o_ref[...] = v.astype(o_ref.dtype)
            else:
                first = (i % tpb == 0) if kind == "acc_brow" else (i == 0)

                @pl.when(first)
                def _(o_ref=o_ref, v=v):
                    o_ref[...] = jnp.broadcast_to(v, o_ref.shape)

                @pl.when(jnp.logical_not(first))
                def _(o_ref=o_ref, v=v):
                    o_ref[...] += v

    res = pl.pallas_call(
        body, name=name, grid=(M // tm,),
        in_specs=[pl.BlockSpec((tm, K), lambda i: (i, 0))]
                 + [pl.BlockSpec(bp.shape, lambda i: (0, 0), pipeline_mode=pl.Buffered(1)) for bp in b_parts]
                 + [pl.BlockSpec((SUBLANES, LANES), lambda i: (0, 0))] * n_tok + in_specs,
        out_specs=tuple(out_specs), out_shape=tuple(out_shapes),
        compiler_params=_cparams(1, big=True),
    )(a, *b_parts, *([tok] if n_tok else []), *in_arrs)
    return res


def _tok_spec(ts, width, col_block=0):
    return pl.BlockSpec((None, ts, width), lambda b, s: (b, s, col_block))


def _brow_spec(width):
    return pl.BlockSpec((None, 1, width), lambda b, s: (b, 0, 0))


def _vec_spec(width):
    return pl.BlockSpec((1, width), lambda b, s: (0, 0))


def _modulate(x, sc, sh, ts):
    Bl, S, D = x.shape

    def body(x_ref, sc_ref, sh_ref, o_ref):
        o_ref[...] = (x_ref[...] * (1.0 + sc_ref[...]) + sh_ref[...]).astype(BF16)

    return pl.pallas_call(
        body, name="modulate", grid=(Bl, S // ts),
        in_specs=[_tok_spec(ts, D), _brow_spec(D), _brow_spec(D)],
        out_specs=_tok_spec(ts, D), out_shape=jax.ShapeDtypeStruct((Bl, S, D), BF16),
        compiler_params=_cparams(2),
    )(x, sc, sh)


def _mix_fwd(proj, w_conv, b_conv, w_rg_a, b_rg_a, w_rg_x, b_rg_x, lam, w_sp, b_sp_t, ln_v_g, ln_v_b, *, tm, lw, sw):
    Bl, S, _ = proj.shape
    heads, hd = w_rg_a.shape[0], w_rg_a.shape[1]
    groups = w_sp.shape[0]
    cw = 2 * lw + 2 * sw
    nblk = tm // SGU_BLOCK

    G = tm // SUBLANES
    nc = lw // LANES

    def body(p_ref, wc_ref, bc_ref, wa_ref, ba_ref, wx_ref, bx_ref, lam_ref, wsp_ref, bsp_ref, lg_ref, lb_ref,
             hs_ref, ya_ref, ys_ref, xc_ref, r_ref, ig_ref, a_ref, m_ref,
             xext, hnat, hcar, h7_scr, a7_scr, hp_scr):
        s = pl.program_id(1)

        @pl.when(s == 0)
        def _():
            xext[:, 0:SUBLANES, :] = jnp.zeros((nc, SUBLANES, LANES), F32)
            hcar[...] = jnp.zeros_like(hcar)

        @pl.when(s > 0)
        def _():
            xext[:, 0:SUBLANES, :] = xext[:, tm:tm + SUBLANES, :]

        for c in range(nc):
            xext[c, SUBLANES:SUBLANES + tm, :] = p_ref[:, c * LANES:(c + 1) * LANES].astype(F32)
        gl = p_ref[:, lw:2 * lw].astype(F32)

        def slab(ref3, start):
            return jnp.concatenate([ref3[c, pl.ds(start, G, stride=SUBLANES), :] for c in range(nc)], axis=1)

        xs = {st: slab(xext, st) for st in range(SUBLANES - 3, 2 * SUBLANES)}
        xc_slabs = []
        for j in range(SUBLANES):
            acc = bc_ref[...] + xs[SUBLANES + j] * wc_ref[3:4, :]
            for k in (1, 2, 3):
                acc = acc + xs[SUBLANES + j - k] * wc_ref[3 - k:4 - k, :]
            xc_slabs.append(acc)
        xc = jnp.concatenate(xc_slabs, axis=0)

        xcb = xc.astype(BF16)
        pa = jnp.concatenate([jnp.dot(xcb[:, h * hd:(h + 1) * hd], wa_ref[h], preferred_element_type=F32)
                              for h in range(heads)], axis=1) + ba_ref[...]
        px = jnp.concatenate([jnp.dot(xcb[:, h * hd:(h + 1) * hd], wx_ref[h], preferred_element_type=F32)
                              for h in range(heads)], axis=1) + bx_ref[...]
        r = _sigmoid(pa)
        ig = _sigmoid(px)
        nl = -lam_ref[...]
        big_l = -LRU_C * (jnp.maximum(nl, 0.0) + _log1p_pos(jnp.exp(-jnp.abs(nl))))
        la = big_l * r
        a = jnp.exp(la)
        th = jnp.tanh(la)
        msq = (-2.0 * th) / (1.0 - th)
        m = msq * lax.rsqrt(jnp.maximum(msq, 1e-30))
        bin_ = m * (ig * xc)
        xc_ref[...] = xc
        r_ref[...] = r
        ig_ref[...] = ig
        a_ref[...] = a
        m_ref[...] = m

        h0 = [bin_[0:G]]
        cp = [a[0:G]]
        for j in range(1, SUBLANES):
            aj = a[j * G:(j + 1) * G]
            h0.append(aj * h0[j - 1] + bin_[j * G:(j + 1) * G])
            cp.append(aj * cp[j - 1])
        h7_scr[...] = h0[SUBLANES - 1]
        a7_scr[...] = cp[SUBLANES - 1]
        carry = hcar[0:1, :]
        for g in range(G):
            hp_scr[g:g + 1, :] = carry
            carry = h7_scr[g:g + 1, :] + a7_scr[g:g + 1, :] * carry
        hcar[0:1, :] = carry
        hprev = hp_scr[...]
        for j in range(SUBLANES):
            hj = h0[j] + cp[j] * hprev
            for c in range(nc):
                hnat[c, pl.ds(j, G, stride=SUBLANES), :] = hj[:, c * LANES:(c + 1) * LANES]
        hs = jnp.concatenate([hnat[c] for c in range(nc)], axis=1)
        hs_ref[...] = hs
        ya_ref[...] = (hs * _gelu(gl)).astype(BF16)

        gu = _gelu(p_ref[:, 2 * lw:2 * lw + sw].astype(F32))
        gv = _gelu(p_ref[:, 2 * lw + sw:cw].astype(F32))
        xhat, _ = _ln_stats(gv)
        vn = (xhat * lg_ref[...] + lb_ref[...]).astype(BF16)
        tpos = lax.broadcasted_iota(jnp.int32, (SGU_BLOCK, SGU_BLOCK), 0) // CHUNK
        spos = lax.broadcasted_iota(jnp.int32, (SGU_BLOCK, SGU_BLOCK), 1) // CHUNK
        gw = sw // groups
        rows_out = []
        for blk in range(nblk):
            r0 = blk * SGU_BLOCK
            cols = []
            for g in range(groups):
                wm = jnp.where(spos <= tpos, wsp_ref[g], 0.0).astype(BF16)
                mixed = jnp.dot(wm, vn[r0:r0 + SGU_BLOCK, g * gw:(g + 1) * gw], preferred_element_type=F32)
                cols.append(mixed + bsp_ref[:, g:g + 1])
            rows_out.append(jnp.concatenate(cols, axis=1))
        mixed_all = jnp.concatenate(rows_out, axis=0) if nblk > 1 else rows_out[0]
        ys_ref[...] = (gu * mixed_all).astype(BF16)

    full = lambda shp: pl.BlockSpec(shp, lambda b, s: (0,) * len(shp))
    return pl.pallas_call(
        body, name="mix_fwd", grid=(Bl, S // tm),
        in_specs=[_tok_spec(tm, cw), full(w_conv.shape), full(b_conv.shape), full(w_rg_a.shape), full(b_rg_a.shape),
                  full(w_rg_x.shape), full(b_rg_x.shape), full(lam.shape), full(w_sp.shape), full(b_sp_t.shape),
                  full(ln_v_g.shape), full(ln_v_b.shape)],
        out_specs=(_tok_spec(tm, lw), _tok_spec(tm, lw), _tok_spec(tm, sw)) + (_tok_spec(tm, lw),) * 5,
        out_shape=(jax.ShapeDtypeStruct((Bl, S, lw), F32), jax.ShapeDtypeStruct((Bl, S, lw), BF16),
                   jax.ShapeDtypeStruct((Bl, S, sw), BF16)) + (jax.ShapeDtypeStruct((Bl, S, lw), F32),) * 5,
        scratch_shapes=[pltpu.VMEM((nc, tm + SUBLANES, LANES), F32), pltpu.VMEM((nc, tm, LANES), F32),
                        pltpu.VMEM((SUBLANES, lw), F32), pltpu.VMEM((G, lw), F32), pltpu.VMEM((G, lw), F32),
                        pltpu.VMEM((G, lw), F32)],
        compiler_params=_cparams(2, big=True),
    )(proj, w_conv, b_conv, w_rg_a, b_rg_a, w_rg_x, b_rg_x, lam, w_sp, b_sp_t, ln_v_g, ln_v_b)


def _merge_fwd(proj, y_a, y_b, *, ts, d):
    Bl, S, din = proj.shape
    gcol = (din - 2 * d) // (2 * d)
    assert gcol * 2 * d == din - 2 * d

    def body(g_ref, ya_ref, yb_ref, o_ref):
        sa = _sigmoid(g_ref[:, 0:d].astype(F32))
        sb = _sigmoid(g_ref[:, d:2 * d].astype(F32))
        o_ref[...] = (sa * ya_ref[...].astype(F32) + sb * yb_ref[...].astype(F32)).astype(BF16)

    return pl.pallas_call(
        body, name="merge_fwd", grid=(Bl, S // ts),
        in_specs=[_tok_spec(ts, 2 * d, gcol), _tok_spec(ts, d), _tok_spec(ts, d)],
        out_specs=_tok_spec(ts, d), out_shape=jax.ShapeDtypeStruct((Bl, S, d), BF16),
        compiler_params=_cparams(2),
    )(proj, y_a, y_b)


def _ln1_fwd(x, mix, gt1, g1, b1, sc2, sh2, *, ts):
    Bl, S, D = x.shape

    def body(x_ref, mix_ref, gt_ref, g_ref, b_ref, sc_ref, sh_ref, x1_ref, h2_ref):
        z = ALPHA * x_ref[...] + (1.0 + gt_ref[...]) * mix_ref[...].astype(F32)
        xhat, _ = _ln_stats(z)
        x1 = xhat * g_ref[...] + b_ref[...]
        x1_ref[...] = x1
        h2_ref[...] = (x1 * (1.0 + sc_ref[...]) + sh_ref[...]).astype(BF16)

    return pl.pallas_call(
        body, name="ln1_fwd", grid=(Bl, S // ts),
        in_specs=[_tok_spec(ts, D), _tok_spec(ts, D), _brow_spec(D), _vec_spec(D), _vec_spec(D), _brow_spec(D),
                  _brow_spec(D)],
        out_specs=(_tok_spec(ts, D), _tok_spec(ts, D)),
        out_shape=(jax.ShapeDtypeStruct((Bl, S, D), F32), jax.ShapeDtypeStruct((Bl, S, D), BF16)),
        compiler_params=_cparams(2),
    )(x, mix, gt1, g1, b1, sc2, sh2)


def _ln2_loss(x1, f, tgt, gt2, g2, b2, *, ts):
    Bl, S, D = x1.shape

    def body(x1_ref, f_ref, t_ref, gt_ref, g_ref, b_ref, df_ref, dx1_ref, dgt_ref, dg_ref, db_ref, loss_ref):
        s = pl.program_id(1)

        @pl.when(_first_step())
        def _():
            dg_ref[...] = jnp.zeros_like(dg_ref)
            db_ref[...] = jnp.zeros_like(db_ref)
            loss_ref[...] = jnp.zeros_like(loss_ref)

        @pl.when(s == 0)
        def _():
            dgt_ref[...] = jnp.zeros_like(dgt_ref)

        fv = f_ref[...]
        z = ALPHA * x1_ref[...] + (1.0 + gt_ref[...]) * fv
        xhat, rstd = _ln_stats(z)
        x2 = xhat * g_ref[...] + b_ref[...]
        err = x2 - t_ref[...]
        loss_ref[...] += 0.5 * jnp.sum(jnp.mean(err * err, axis=-1, keepdims=True))
        dy = err * (1.0 / D)
        dg_ref[...] += _colsum(dy * xhat)
        db_ref[...] += _colsum(dy)
        dz = _ln_bwd(dy, xhat, rstd, g_ref[...])
        dx1_ref[...] = ALPHA * dz
        dgt_ref[...] += _colsum(dz * fv)
        df_ref[...] = (dz * (1.0 + gt_ref[...])).astype(BF16)

    return pl.pallas_call(
        body, name="ln2_loss", grid=(Bl, S // ts),
        in_specs=[_tok_spec(ts, D), _tok_spec(ts, D), _tok_spec(ts, D), _brow_spec(D), _vec_spec(D), _vec_spec(D)],
        out_specs=(_tok_spec(ts, D), _tok_spec(ts, D), _brow_spec(D), _vec_spec(D), _vec_spec(D),
                   pl.BlockSpec((SUBLANES, LANES), lambda b, s: (0, 0))),
        out_shape=(jax.ShapeDtypeStruct((Bl, S, D), BF16), jax.ShapeDtypeStruct((Bl, S, D), F32),
                   jax.ShapeDtypeStruct((Bl, 1, D), F32), jax.ShapeDtypeStruct((1, D), F32),
                   jax.ShapeDtypeStruct((1, D), F32), jax.ShapeDtypeStruct((SUBLANES, LANES), F32)),
        compiler_params=_cparams(2),
    )(x1, f, tgt, gt2, g2, b2)


def _ln1_bwd(dx1p, dh2, x1, x, mix, sc2, gt1, g1, *, ts):
    Bl, S, D = x.shape

    def body(dx1p_ref, dh2_ref, x1_ref, x_ref, mix_ref, sc_ref, gt_ref, g_ref,
             dxp_ref, dmix_ref, dsc_ref, dsh_ref, dgt_ref, dg_ref, db_ref):
        s = pl.program_id(1)

        @pl.when(_first_step())
        def _():
            dg_ref[...] = jnp.zeros_like(dg_ref)
            db_ref[...] = jnp.zeros_like(db_ref)

        @pl.when(s == 0)
        def _():
            dsc_ref[...] = jnp.zeros_like(dsc_ref)
            dsh_ref[...] = jnp.zeros_like(dsh_ref)
            dgt_ref[...] = jnp.zeros_like(dgt_ref)

        dh2 = dh2_ref[...].astype(F32)
        mixv = mix_ref[...].astype(F32)
        dsc_ref[...] += _colsum(dh2 * x1_ref[...])
        dsh_ref[...] += _colsum(dh2)
        dx1 = dx1p_ref[...] + dh2 * (1.0 + sc_ref[...])
        z = ALPHA * x_ref[...] + (1.0 + gt_ref[...]) * mixv
        xhat, rstd = _ln_stats(z)
        dg_ref[...] += _colsum(dx1 * xhat)
        db_ref[...] += _colsum(dx1)
        dz = _ln_bwd(dx1, xhat, rstd, g_ref[...])
        dxp_ref[...] = ALPHA * dz
        dgt_ref[...] += _colsum(dz * mixv)
        dmix_ref[...] = (dz * (1.0 + gt_ref[...])).astype(BF16)

    return pl.pallas_call(
        body, name="ln1_bwd", grid=(Bl, S // ts),
        in_specs=[_tok_spec(ts, D)] * 5 + [_brow_spec(D), _brow_spec(D), _vec_spec(D)],
        out_specs=(_tok_spec(ts, D), _tok_spec(ts, D), _brow_spec(D), _brow_spec(D), _brow_spec(D), _vec_spec(D),
                   _vec_spec(D)),
        out_shape=(jax.ShapeDtypeStruct((Bl, S, D), F32), jax.ShapeDtypeStruct((Bl, S, D), BF16),
                   jax.ShapeDtypeStruct((Bl, 1, D), F32), jax.ShapeDtypeStruct((Bl, 1, D), F32),
                   jax.ShapeDtypeStruct((Bl, 1, D), F32), jax.ShapeDtypeStruct((1, D), F32),
                   jax.ShapeDtypeStruct((1, D), F32)),
        compiler_params=_cparams(2),
    )(dx1p, dh2, x1, x, mix, sc2, gt1, g1)


def _merge_bwd(dmerged, y_a, y_b, proj, *, ts, d):
    Bl, S, din = proj.shape
    gcol = (din - 2 * d) // (2 * d)

    def body(dm_ref, ya_ref, yb_ref, g_ref, dya_ref, dyb_ref, dp_ref, db_ref):
        @pl.when(_first_step())
        def _():
            db_ref[...] = jnp.zeros_like(db_ref)

        dm = dm_ref[...].astype(F32)
        sa = _sigmoid(g_ref[:, 0:d].astype(F32))
        sb = _sigmoid(g_ref[:, d:2 * d].astype(F32))
        dya_ref[...] = (dm * sa).astype(BF16)
        dyb_ref[...] = (dm * sb).astype(BF16)
        dga = dm * ya_ref[...].astype(F32) * sa * (1.0 - sa)
        dgb = dm * yb_ref[...].astype(F32) * sb * (1.0 - sb)
        dp_ref[:, 0:d] = dga.astype(BF16)
        dp_ref[:, d:2 * d] = dgb.astype(BF16)
        db_ref[:, 0:d] += _colsum(dga)
        db_ref[:, d:2 * d] += _colsum(dgb)

    return pl.pallas_call(
        body, name="merge_bwd", grid=(Bl, S // ts),
        in_specs=[_tok_spec(ts, d), _tok_spec(ts, d), _tok_spec(ts, d), _tok_spec(ts, 2 * d, gcol)],
        out_specs=(_tok_spec(ts, d), _tok_spec(ts, d), _tok_spec(ts, 2 * d, gcol), _vec_spec(2 * d)),
        out_shape=(jax.ShapeDtypeStruct((Bl, S, d), BF16), jax.ShapeDtypeStruct((Bl, S, d), BF16),
                   jax.ShapeDtypeStruct((Bl, S, din), BF16), jax.ShapeDtypeStruct((1, 2 * d), F32)),
        compiler_params=_cparams(2),
    )(dmerged, y_a, y_b, proj)


def _mix_bwd(proj, hs, dya, dys, dproj, saved, w_conv, b_conv, w_rg_a, b_rg_a, w_rg_x, b_rg_x, lam, w_sp, b_sp_t,
             ln_v_g, ln_v_b, *, tm, lw, sw):
    Bl, S, din = proj.shape
    heads, hd = w_rg_a.shape[0], w_rg_a.shape[1]
    groups = w_sp.shape[0]
    gw = sw // groups
    cw = 2 * lw + 2 * sw
    nblk = tm // SGU_BLOCK
    n_s = S // tm
    per8 = tm // SUBLANES
    halo_rows = 2 * SUBLANES

    G = tm // SUBLANES
    nc = lw // LANES

    def body(p_ref, xh_ref, hs_ref, hh_ref, dya_ref, dys_ref, dpin_ref, xc_ref, r_ref, ig_ref, a_ref, m_ref,
             wc_ref, bc_ref, wa_ref, ba_ref, wx_ref, bx_ref, lam_ref, wsp_ref, bsp_ref, lg_ref, lb_ref,
             dp_ref, dbin_ref, dwc_ref, dbc_ref, dwa_ref, dba_ref, dwx_ref, dbx_ref, dlam_ref, dwsp_ref, dbsp_ref,
             dlg_ref, dlb_ref,
             xext, hext, dnat, dxext, dhcar, g00_scr, p0_scr, a0_scr, cin_scr):
        del dpin_ref
        sr = pl.program_id(1)
        first_tile = sr == n_s - 1

        @pl.when(_first_step())
        def _():
            for ref in (dbin_ref, dwc_ref, dbc_ref, dwa_ref, dba_ref, dwx_ref, dbx_ref, dlam_ref, dwsp_ref, dbsp_ref,
                        dlg_ref, dlb_ref):
                ref[...] = jnp.zeros_like(ref)

        @pl.when(sr == 0)
        def _():
            dhcar[...] = jnp.zeros_like(dhcar)
            dxext[:, tm:tm + SUBLANES, :] = jnp.zeros((nc, SUBLANES, LANES), F32)

        @pl.when(sr > 0)
        def _():
            dxext[:, tm:tm + SUBLANES, :] = dxext[:, 0:SUBLANES, :]

        def slab(ref3, start):
            return jnp.concatenate([ref3[c, pl.ds(start, G, stride=SUBLANES), :] for c in range(nc)], axis=1)

        def put_slab(ref3, j, val):
            for c in range(nc):
                ref3[c, pl.ds(j, G, stride=SUBLANES), :] = val[:, c * LANES:(c + 1) * LANES]

        keep = jnp.where(first_tile, 0.0, 1.0)
        xprev = xh_ref[...].astype(F32)[halo_rows - SUBLANES:halo_rows] * keep
        hsv = hs_ref[...]
        hprev8 = hh_ref[...] * keep
        for c in range(nc):
            cs = slice(c * LANES, (c + 1) * LANES)
            xext[c, 0:SUBLANES, :] = xprev[:, cs]
            xext[c, SUBLANES:SUBLANES + tm, :] = p_ref[:, cs].astype(F32)
            hext[c, 0:SUBLANES, :] = hprev8[:, cs]
            hext[c, SUBLANES:SUBLANES + tm, :] = hsv[:, cs]
        gl = p_ref[:, lw:2 * lw].astype(F32)
        ggl, dggl = _gelu_and_grad(gl)
        dyav = dya_ref[...]
        dhs = dyav * ggl
        dgl = dyav * hsv * dggl
        dp_ref[:, lw:2 * lw] = dgl.astype(BF16)
        dbin_ref[:, lw:2 * lw] += _colsum(dgl)
        for c in range(nc):
            dnat[c] = dhs[:, c * LANES:(c + 1) * LANES]

        xc, r, ig, a, m = xc_ref[...], r_ref[...], ig_ref[...], a_ref[...], m_ref[...]
        xcb = xc.astype(BF16)
        nl = -lam_ref[...]
        big_l = -LRU_C * (jnp.maximum(nl, 0.0) + _log1p_pos(jnp.exp(-jnp.abs(nl))))

        g0 = [None] * SUBLANES
        pp = [None] * SUBLANES
        g0[SUBLANES - 1] = slab(dnat, SUBLANES - 1)
        for j in range(SUBLANES - 2, -1, -1):
            an = a[(j + 1) * G:(j + 2) * G]
            g0[j] = slab(dnat, j) + an * g0[j + 1]
            pp[j] = an if j == SUBLANES - 2 else an * pp[j + 1]
        g00_scr[...] = g0[0]
        p0_scr[...] = pp[0]
        a0_scr[...] = a[0:G]
        cin = dhcar[0:1, :]
        for g in range(G - 1, -1, -1):
            cin_scr[g:g + 1, :] = cin
            cin = a0_scr[g:g + 1, :] * (g00_scr[g:g + 1, :] + p0_scr[g:g + 1, :] * cin)
        dhcar[0:1, :] = cin
        cinv = cin_scr[...]
        dh = jnp.concatenate([g0[j] + pp[j] * cinv for j in range(SUBLANES - 1)] + [g0[SUBLANES - 1] + cinv], axis=0)

        hprev = jnp.concatenate([slab(hext, SUBLANES - 1 + j) for j in range(SUBLANES)], axis=0)
        da = dh * hprev
        ixc = ig * xc
        dm = dh * ixc
        dixc = dh * m
        di = dixc * xc
        dxc = dixc * ig
        dla = da * a - dm * (a * a) / m
        dlam_ref[...] += _colsum(dla * r) * (LRU_C * _sigmoid(nl))
        dr = dla * big_l
        dpa = dr * r * (1.0 - r)
        dpx = di * ig * (1.0 - ig)
        dba_ref[...] += _colsum(dpa)
        dbx_ref[...] += _colsum(dpx)
        dpab = dpa.astype(BF16)
        dpxb = dpx.astype(BF16)
        nt = (((1,), (1,)), ((), ()))
        tn = (((0,), (0,)), ((), ()))
        dxc_g = []
        for h in range(heads):
            sl = slice(h * hd, (h + 1) * hd)
            dxc_g.append(lax.dot_general(dpab[:, sl], wa_ref[h], nt, preferred_element_type=F32)
                         + lax.dot_general(dpxb[:, sl], wx_ref[h], nt, preferred_element_type=F32))
            dwa_ref[h] += lax.dot_general(xcb[:, sl], dpab[:, sl], tn, preferred_element_type=F32)
            dwx_ref[h] += lax.dot_general(xcb[:, sl], dpxb[:, sl], tn, preferred_element_type=F32)
        dxc = dxc + jnp.concatenate(dxc_g, axis=1)

        dbc_ref[...] += _colsum(dxc)
        xs = {st: slab(xext, st) for st in range(SUBLANES - 3, 2 * SUBLANES)}
        for k in range(4):
            xsh = jnp.concatenate([xs[SUBLANES + j - (3 - k)] for j in range(SUBLANES)], axis=0)
            dwc_ref[k:k + 1, :] += _colsum(dxc * xsh)
        for j in range(SUBLANES):
            put_slab(dxext, j, dxc[j * G:(j + 1) * G])
        us = {st: slab(dxext, st) for st in range(SUBLANES + 3)}
        for j in range(SUBLANES):
            acc = us[j] * wc_ref[3:4, :]
            for k in (1, 2, 3):
                acc = acc + us[j + k] * wc_ref[3 - k:4 - k, :]
            put_slab(dnat, j, acc)
        dxl = jnp.concatenate([dnat[c] for c in range(nc)], axis=1)
        dp_ref[:, 0:lw] = dxl.astype(BF16)
        dbin_ref[:, 0:lw] += _colsum(dxl)

        gu, dgu_dx = _gelu_and_grad(p_ref[:, 2 * lw:2 * lw + sw].astype(F32))
        gv, dgv_dx = _gelu_and_grad(p_ref[:, 2 * lw + sw:cw].astype(F32))
        xhat, rstd = _ln_stats(gv)
        vn = (xhat * lg_ref[...] + lb_ref[...]).astype(BF16)
        dys = dys_ref[...]
        dmixed = dys * gu
        dmb = dmixed.astype(BF16)
        tpos = lax.broadcasted_iota(jnp.int32, (SGU_BLOCK, SGU_BLOCK), 0) // CHUNK
        spos = lax.broadcasted_iota(jnp.int32, (SGU_BLOCK, SGU_BLOCK), 1) // CHUNK
        causal = spos <= tpos
        mixed_rows, dvn_rows = [], []
        for blk in range(nblk):
            rs = slice(blk * SGU_BLOCK, (blk + 1) * SGU_BLOCK)
            mcols, dcols = [], []
            for g in range(groups):
                cs = slice(g * gw, (g + 1) * gw)
                wm = jnp.where(causal, wsp_ref[g], 0.0).astype(BF16)
                mcols.append(jnp.dot(wm, vn[rs, cs], preferred_element_type=F32) + bsp_ref[:, g:g + 1])
                dcols.append(lax.dot_general(wm, dmb[rs, cs], tn, preferred_element_type=F32))
                dw = lax.dot_general(dmb[rs, cs], vn[rs, cs], nt, preferred_element_type=F32)
                dwsp_ref[g] += jnp.where(causal, dw, 0.0)
                dbsp_ref[:, g:g + 1] += jnp.sum(dmixed[rs, cs], axis=1, keepdims=True)
            mixed_rows.append(jnp.concatenate(mcols, axis=1))
            dvn_rows.append(jnp.concatenate(dcols, axis=1))
        mixed_all = jnp.concatenate(mixed_rows, axis=0) if nblk > 1 else mixed_rows[0]
        dvn = jnp.concatenate(dvn_rows, axis=0) if nblk > 1 else dvn_rows[0]
        du = dys * mixed_all * dgu_dx
        dlg_ref[...] += _colsum(dvn * xhat)
        dlb_ref[...] += _colsum(dvn)
        dv = _ln_bwd(dvn, xhat, rstd, lg_ref[...]) * dgv_dx
        dp_ref[:, 2 * lw:2 * lw + sw] = du.astype(BF16)
        dp_ref[:, 2 * lw + sw:cw] = dv.astype(BF16)
        dbin_ref[:, 2 * lw:2 * lw + sw] += _colsum(du)
        dbin_ref[:, 2 * lw + sw:cw] += _colsum(dv)

    rev = lambda s: n_s - 1 - s
    tile = lambda w: pl.BlockSpec((None, tm, w), lambda b, s: (b, rev(s), 0))
    halo = lambda w: pl.BlockSpec((None, SUBLANES, w), lambda b, s: (b, jnp.maximum(rev(s) * per8 - 1, 0), 0))
    xhalo = pl.BlockSpec((None, halo_rows, lw), lambda b, s: (b, jnp.maximum(rev(s) * (tm // halo_rows) - 1, 0), 0))
    full = lambda shp: pl.BlockSpec(shp, lambda b, s: (0,) * len(shp))
    small = [w_conv, b_conv, w_rg_a, b_rg_a, w_rg_x, b_rg_x, lam, w_sp, b_sp_t, ln_v_g, ln_v_b]
    acc_shapes = [(1, cw), w_conv.shape, b_conv.shape, w_rg_a.shape, b_rg_a.shape, w_rg_x.shape, b_rg_x.shape,
                  lam.shape, w_sp.shape, b_sp_t.shape, ln_v_g.shape, ln_v_b.shape]
    res = pl.pallas_call(
        body, name="mix_bwd", grid=(Bl, n_s),
        in_specs=[tile(cw), xhalo, tile(lw), halo(lw), tile(lw), tile(sw), pl.BlockSpec(memory_space=pl.ANY)]
                 + [tile(lw)] * 5 + [full(w.shape) for w in small],
        out_specs=tuple([tile(cw)] + [full(shp) for shp in acc_shapes]),
        out_shape=tuple([jax.ShapeDtypeStruct((Bl, S, din), BF16)] + [jax.ShapeDtypeStruct(shp, F32) for shp in acc_shapes]),
        input_output_aliases={6: 0},
        scratch_shapes=[pltpu.VMEM((nc, tm + SUBLANES, LANES), F32), pltpu.VMEM((nc, tm + SUBLANES, LANES), F32),
                        pltpu.VMEM((nc, tm, LANES), F32), pltpu.VMEM((nc, tm + SUBLANES, LANES), F32),
                        pltpu.VMEM((SUBLANES, lw), F32), pltpu.VMEM((G, lw), F32), pltpu.VMEM((G, lw), F32),
                        pltpu.VMEM((G, lw), F32), pltpu.VMEM((G, lw), F32)],
        compiler_params=_cparams(2, big=True),
    )(proj, proj, hs, hs, dya, dys, dproj, *saved, *small)
    return res


def _final_dx(dxp, dh, x, sc1, *, ts):
    Bl, S, D = x.shape

    def body(dxp_ref, dh_ref, x_ref, sc_ref, dx_ref, dsc_ref, dsh_ref):
        @pl.when(pl.program_id(1) == 0)
        def _():
            dsc_ref[...] = jnp.zeros_like(dsc_ref)
            dsh_ref[...] = jnp.zeros_like(dsh_ref)

        dh = dh_ref[...]
        dx_ref[...] = dxp_ref[...] + dh * (1.0 + sc_ref[...])
        dsc_ref[...] += _colsum(dh * x_ref[...])
        dsh_ref[...] += _colsum(dh)

    return pl.pallas_call(
        body, name="final_dx", grid=(Bl, S // ts),
        in_specs=[_tok_spec(ts, D), _tok_spec(ts, D), _tok_spec(ts, D), _brow_spec(D)],
        out_specs=(_tok_spec(ts, D), _brow_spec(D), _brow_spec(D)),
        out_shape=(jax.ShapeDtypeStruct((Bl, S, D), F32), jax.ShapeDtypeStruct((Bl, 1, D), F32),
                   jax.ShapeDtypeStruct((Bl, 1, D), F32)),
        compiler_params=_cparams(2),
    )(dxp, dh, x, sc1)


def _ada_fwd(c_all, w_ada):
    R, D = c_all.shape
    nb = w_ada.shape[1]

    def body(c_ref, w_ref, act_ref, o_ref):
        cv = c_ref[...]
        act = (cv * _sigmoid(cv)).astype(BF16)
        act_ref[...] = act
        o_ref[...] = jnp.dot(act, w_ref[...].astype(BF16), preferred_element_type=F32)

    return pl.pallas_call(
        body, name="ada_fwd",
        out_shape=(jax.ShapeDtypeStruct((R, D), BF16), jax.ShapeDtypeStruct((R, nb), F32)),
        compiler_params=pltpu.CompilerParams(vmem_limit_bytes=VMEM_LIMIT),
    )(c_all, w_ada)


def _ada_bwd(c_act, dmod_cols):
    R, D = c_act.shape
    nb = dmod_cols.shape[1]

    def body(act_ref, d_ref, o_ref, b_ref):
        o_ref[...] = lax.dot_general(act_ref[...], d_ref[...].astype(BF16), (((0,), (0,)), ((), ())),
                                     preferred_element_type=F32)
        b_ref[...] = _colsum(d_ref[...])

    return pl.pallas_call(
        body, name="ada_bwd", out_shape=(jax.ShapeDtypeStruct((D, nb), F32), jax.ShapeDtypeStruct((1, nb), F32)),
        compiler_params=pltpu.CompilerParams(vmem_limit_bytes=VMEM_LIMIT),
    )(c_act, dmod_cols)


def _adamw(w, g_slots, m, v, *, tr, name):
    R, C = w.shape
    n_slot = g_slots.shape[0]
    tr = min(tr, R)
    assert R % tr == 0, (name, R, tr)
    c1 = 1.0 / (1.0 - ADAM_B1 ** ADAM_STEP)
    c2 = 1.0 / (1.0 - ADAM_B2 ** ADAM_STEP)

    def body(w_ref, g_ref, m_ref, v_ref, go_ref, d_ref, mo_ref, vo_ref):
        g = g_ref[0].astype(F32)
        for i in range(1, n_slot):
            g = g + g_ref[i].astype(F32)
        mn = ADAM_B1 * m_ref[...] + (1.0 - ADAM_B1) * g
        vn = ADAM_B2 * v_ref[...] + (1.0 - ADAM_B2) * (g * g)
        go_ref[...] = g
        mo_ref[...] = mn
        vo_ref[...] = vn
        d_ref[...] = -ADAM_LR * ((mn * c1) / (jnp.sqrt(vn * c2) + ADAM_EPS) + ADAM_WD * w_ref[...])

    blk = pl.BlockSpec((tr, C), lambda i: (i, 0))
    return pl.pallas_call(
        body, name=name, grid=(R // tr,),
        in_specs=[blk, pl.BlockSpec((n_slot, tr, C), lambda i: (0, i, 0)), blk, blk],
        out_specs=(blk, blk, blk, blk),
        out_shape=tuple(jax.ShapeDtypeStruct((R, C), F32) for _ in range(4)),
        compiler_params=_cparams(1, big=True),
    )(w, g_slots, m, v)


def _adamw_many(ws, g_slots, g_owns, ms, vs, *, name):
    n = len(ws)
    c1 = 1.0 / (1.0 - ADAM_B1 ** ADAM_STEP)
    c2 = 1.0 / (1.0 - ADAM_B2 ** ADAM_STEP)

    def body(*refs):
        w_refs, g_refs, o_refs = refs[:n], refs[n:2 * n], refs[2 * n:3 * n]
        m_refs, v_refs = refs[3 * n:4 * n], refs[4 * n:5 * n]
        outs = refs[5 * n:]
        me = 4 * lax.axis_index("x") + 2 * lax.axis_index("y") + lax.axis_index("c")
        for i in range(n):
            own = o_refs[i][...]
            g = jnp.where(me == 0, own, g_refs[i][0])
            for d in range(1, N_DEV):
                g = g + jnp.where(me == d, own, g_refs[i][d])
            mn = ADAM_B1 * m_refs[i][...] + (1.0 - ADAM_B1) * g
            vn = ADAM_B2 * v_refs[i][...] + (1.0 - ADAM_B2) * (g * g)
            outs[i][...] = g
            outs[n + i][...] = -ADAM_LR * ((mn * c1) / (jnp.sqrt(vn * c2) + ADAM_EPS) + ADAM_WD * w_refs[i][...])
            outs[2 * n + i][...] = mn
            outs[3 * n + i][...] = vn

    res = pl.pallas_call(
        body, name=name, out_shape=tuple(jax.ShapeDtypeStruct(w.shape, F32) for _ in range(4) for w in ws),
        compiler_params=pltpu.CompilerParams(vmem_limit_bytes=VMEM_LIMIT),
    )(*ws, *g_slots, *g_owns, *ms, *vs)
    return res[:n], res[n:2 * n], res[2 * n:3 * n], res[3 * n:]


SMALL_NAMES = ("b_ada", "b_in", "b_conv", "w_rg_a", "b_rg_a", "w_rg_x", "b_rg_x", "lru_lambda", "w_sp", "b_sp",
               "ln_v_g", "ln_v_b", "ln1_g", "ln1_b", "ln2_g", "ln2_b")
BIG_NAMES = ("w_ada", "w_in", "w_conv", "w_o_lru", "w_o_sgu", "w_out", "w_up", "w_down")
WEIGHT_ORDER = ("w_ada", "b_ada", "w_in", "b_in", "w_conv", "b_conv", "w_rg_a", "b_rg_a", "w_rg_x", "b_rg_x",
                "lru_lambda", "w_sp", "b_sp", "ln_v_g", "ln_v_b", "w_o_lru", "w_o_sgu", "w_out", "ln1_g", "ln1_b",
                "w_up", "w_down", "ln2_g", "ln2_b")


def _pack_small(d):
    flat = jnp.concatenate([d[n].reshape(-1) for n in SMALL_NAMES])
    rows = -(-flat.shape[0] // LANES)
    rows = -(-rows // (N_DEV * SUBLANES)) * (N_DEV * SUBLANES)
    flat = jnp.pad(flat, (0, rows * LANES - flat.shape[0]))
    return flat.reshape(rows, LANES)


def _unpack_small(packed, like):
    flat = packed.reshape(-1)
    out, off = {}, 0
    for n in SMALL_NAMES:
        sz = like[n].size
        out[n] = flat[off:off + sz].reshape(like[n].shape)
        off += sz
    return out


def _blocked_cols(w2d):
    K, N = w2d.shape
    return jnp.transpose(w2d.reshape(K, N_DEV, N // N_DEV), (1, 0, 2))


def _unblock_cols(wb):
    n, K, nb = wb.shape
    return jnp.transpose(wb, (1, 0, 2)).reshape(K, n * nb)


def kernel(x, c, w_ada, b_ada, w_in, b_in, w_conv, b_conv, w_rg_a, b_rg_a, w_rg_x, b_rg_x, lru_lambda, w_sp, b_sp, ln_v_g, ln_v_b, w_o_lru, w_o_sgu, w_out, ln1_g, ln1_b, w_up, w_down, ln2_g, ln2_b, loss_target, m_w_ada, m_b_ada, m_w_in, m_b_in, m_w_conv, m_b_conv, m_w_rg_a, m_b_rg_a, m_w_rg_x, m_b_rg_x, m_lru_lambda, m_w_sp, m_b_sp, m_ln_v_g, m_ln_v_b, m_w_o_lru, m_w_o_sgu, m_w_out, m_ln1_g, m_ln1_b, m_w_up, m_w_down, m_ln2_g, m_ln2_b, v_w_ada, v_b_ada, v_w_in, v_b_in, v_w_conv, v_b_conv, v_w_rg_a, v_b_rg_a, v_w_rg_x, v_b_rg_x, v_lru_lambda, v_w_sp, v_b_sp, v_ln_v_g, v_ln_v_b, v_w_o_lru, v_w_o_sgu, v_w_out, v_ln1_g, v_ln1_b, v_w_up, v_w_down, v_ln2_g, v_ln2_b):
    W = dict(w_ada=w_ada, b_ada=b_ada, w_in=w_in, b_in=b_in, w_conv=w_conv, b_conv=b_conv, w_rg_a=w_rg_a,
             b_rg_a=b_rg_a, w_rg_x=w_rg_x, b_rg_x=b_rg_x, lru_lambda=lru_lambda, w_sp=w_sp, b_sp=b_sp,
             ln_v_g=ln_v_g, ln_v_b=ln_v_b, w_o_lru=w_o_lru, w_o_sgu=w_o_sgu, w_out=w_out, ln1_g=ln1_g, ln1_b=ln1_b,
             w_up=w_up, w_down=w_down, ln2_g=ln2_g, ln2_b=ln2_b)
    Mo = dict(w_ada=m_w_ada, b_ada=m_b_ada, w_in=m_w_in, b_in=m_b_in, w_conv=m_w_conv, b_conv=m_b_conv,
              w_rg_a=m_w_rg_a, b_rg_a=m_b_rg_a, w_rg_x=m_w_rg_x, b_rg_x=m_b_rg_x, lru_lambda=m_lru_lambda,
              w_sp=m_w_sp, b_sp=m_b_sp, ln_v_g=m_ln_v_g, ln_v_b=m_ln_v_b, w_o_lru=m_w_o_lru, w_o_sgu=m_w_o_sgu,
              w_out=m_w_out, ln1_g=m_ln1_g, ln1_b=m_ln1_b, w_up=m_w_up, w_down=m_w_down, ln2_g=m_ln2_g,
              ln2_b=m_ln2_b)
    Vo = dict(w_ada=v_w_ada, b_ada=v_b_ada, w_in=v_w_in, b_in=v_b_in, w_conv=v_w_conv, b_conv=v_b_conv,
              w_rg_a=v_w_rg_a, b_rg_a=v_b_rg_a, w_rg_x=v_w_rg_x, b_rg_x=v_b_rg_x, lru_lambda=v_lru_lambda,
              w_sp=v_w_sp, b_sp=v_b_sp, ln_v_g=v_ln_v_g, ln_v_b=v_ln_v_b, w_o_lru=v_w_o_lru, w_o_sgu=v_w_o_sgu,
              w_out=v_w_out, ln1_g=v_ln1_g, ln1_b=v_ln1_b, w_up=v_w_up, w_down=v_w_down, ln2_g=v_ln2_g,
              ln2_b=v_ln2_b)

    Bl, S, D = x.shape
    T = Bl * S
    lw = b_conv.shape[-1]
    sw = ln_v_g.shape[-1]
    din = b_in.shape[-1]
    dff = w_up.shape[-1] * N_DEV
    ts = min(512, S)
    tmix = min(256, S)
    trow = min(512, S)

    c_pad = jnp.pad(c, ((0, SUBLANES - Bl), (0, 0)))
    c_g, wconv_g = _exchange([c_pad, w_conv[0]], True, "xchg_c")
    wconv_full = _unblock_cols(wconv_g)
    c_act, modcols = _ada_fwd(c_g.reshape(N_DEV * SUBLANES, D), w_ada[0])
    (mod_slots,) = _exchange([modcols.reshape(N_DEV, SUBLANES, -1)], False, "xchg_mod")

    nbw = din // N_DEV // WIN_PARTS
    wnames = tuple("win%d" % q for q in range(WIN_PARTS)) + ("wol", "wos", "wout", "wup", "wdown")
    shards = [w_in[0][:, q * nbw:(q + 1) * nbw].astype(BF16) for q in range(WIN_PARTS)] + [
        w_o_lru[0].astype(BF16), w_o_sgu[0].astype(BF16), w_out[0].astype(BF16), w_up[0].astype(BF16),
        w_down[0].astype(BF16)]
    col_sharded = [True] * WIN_PARTS + [False, True, False, True, False]
    g_send, g_recv, g_src, g_land, g_tok = _xstart(shards, True, mod_slots, "gather_start", cols=col_sharded)
    gidx = {n: i for i, n in enumerate(wnames)}

    def gathered(n, after):
        i = gidx[n]
        return _xwait(g_src[i], g_land[i], g_send[i], g_recv[i], after, True, "gather_wait_" + n, col=col_sharded[i])

    mod = _unblock_cols(mod_slots)[:Bl] + (b_ada + g_tok[0, 0])
    sh1, sc1, gt1, sh2, sc2, gt2 = [mod[:, i * D:(i + 1) * D].reshape(Bl, 1, D) for i in range(6)]

    wa_b, wx_b = w_rg_a[0].astype(BF16), w_rg_x[0].astype(BF16)
    b_sp_t = jnp.transpose(b_sp[0])
    small_mix = (wconv_full, b_conv, wa_b, b_rg_a, wx_b, b_rg_x, lru_lambda, w_sp[0], b_sp_t, ln_v_g, ln_v_b)

    h = _modulate(x, sc1, sh1, ts)
    proj, win_parts = None, []
    for q in range(WIN_PARTS):
        wq = gathered("win%d" % q, h if q == 0 else proj)
        win_parts.append(wq)
        proj = _mm(h.reshape(T, D), wq, mode="nn", tm=4096, tn=nbw, tk=D, outs=[BF16], extras=[(b_in, "row")],
                   epilogue=lambda acc, ex: (acc + ex[0],), scatter=(WIN_PARTS, q, din), into=proj,
                   name="mm_proj%d" % q)
    proj3 = proj.reshape(Bl, S, din)
    hs, ya_pre, ysgu, *lru_saved = _mix_fwd(proj3, *small_mix, tm=tmix, lw=lw, sw=sw)
    Wol = gathered("wol", ya_pre).reshape(lw, D)
    Wos = gathered("wos", ysgu)
    y_a = _mm(ya_pre.reshape(T, lw), Wol, mode="nn", tm=2048, tn=D, tk=lw, outs=[BF16], name="mm_ya")
    x2d, tgt2d = x.reshape(T, D), loss_target.reshape(T, D)
    gate_cb = (din - 2 * D) // D

    def ep_merge(y_b, v):
        ya, ga, gb = [t.astype(F32) for t in v]
        yb = y_b.astype(BF16).astype(F32)
        return [yb, _sigmoid(ga) * ya + _sigmoid(gb) * yb]

    y_b, merged = _mm_rows(ysgu.reshape(T, sw), Wos, mode="nn", tm=trow, seq=S,
                           ins=[("tile", y_a), ("tilecol", proj, D, gate_cb), ("tilecol", proj, D, gate_cb + 1)],
                           outs=[("tile", BF16, D), ("tile", BF16, D)], epilogue=ep_merge, name="mm_yb_merge")
    Wout = gathered("wout", merged).reshape(D, D)

    def ep_ln1(mix_acc, v):
        x_, gt, g, b, sc, sh = v
        mixr = mix_acc.astype(BF16).astype(F32)
        xhat, _ = _ln_stats(ALPHA * x_ + (1.0 + gt) * mixr)
        x1_ = xhat * g + b
        return [mixr, x1_, x1_ * (1.0 + sc) + sh]

    mix, x1, h2 = _mm_rows(merged, Wout, mode="nn", tm=trow, seq=S,
                           ins=[("tile", x2d), ("brow", gt1), ("row", ln1_g), ("row", ln1_b), ("brow", sc2),
                                ("brow", sh2)],
                           outs=[("tile", BF16, D), ("tile", F32, D), ("tile", BF16, D)], epilogue=ep_ln1,
                           name="mm_mix_ln1")
    Wup = gathered("wup", h2)
    act = _mm(h2, Wup, mode="nn", tm=2048, tn=1024, tk=D, outs=[BF16],
              epilogue=lambda acc, ex: (jnp.square(jnp.maximum(acc, 0.0)),), name="mm_up")
    Wdown = gathered("wdown", act).reshape(dff, D)

    def ep_ln2(f_acc, v):
        x1_, t_, gt, g, b = v
        xhat, rstd = _ln_stats(ALPHA * x1_ + (1.0 + gt) * f_acc)
        err = xhat * g + b - t_
        loss_t = 0.5 * jnp.sum(jnp.mean(err * err, axis=-1, keepdims=True))
        dy = err * (1.0 / D)
        dz = _ln_bwd(dy, xhat, rstd, g)
        return [dz * (1.0 + gt), ALPHA * dz, _colsum(dz * f_acc), _colsum(dy * xhat), _colsum(dy), loss_t]

    df2, dx1p, dgt2, dg2, db2, loss_part = _mm_rows(
        act, Wdown, mode="nn", tm=trow, seq=S,
        ins=[("tile", x1), ("tile", tgt2d), ("brow", gt2), ("row", ln2_g), ("row", ln2_b)],
        outs=[("tile", BF16, D), ("tile", F32, D), ("acc_brow", D), ("acc_row", D), ("acc_row", D), ("acc_scalar",)],
        epilogue=ep_ln2, name="mm_down_ln2")
    loss = lax.psum(loss_part[0, 0], ("x", "y", "c"))

    def send_grads(parts, name):
        snd, rcv, src, land, tok = _xstart(parts, False, None, name + "_start")
        return [(src[i], land[i], snd[i], rcv[i]) for i in range(len(parts))], tok

    dup = _mm(df2, Wdown, mode="nt", tm=2048, tn=1024, tk=D, outs=[BF16], extras=[(act, "tile")],
              epilogue=lambda acc, ex: (acc * (2.0 * jnp.sqrt(ex[0].astype(F32))),), name="mm_dup")
    g_wdown = _mm(act, df2, mode="tn", tm=1024, tn=D, tk=2048, outs=[BF16], name="mm_gwdown")
    (x_wdown,), tok = send_grads([g_wdown.reshape(N_DEV, dff // N_DEV, D)], "gx_wdown")
    def ep_ln1_bwd(dh2, v):
        dx1p_, x1_, x_, mix_, sc, gt, g = v
        mixv = mix_.astype(F32)
        dx1 = dx1p_ + dh2 * (1.0 + sc)
        xhat, rstd = _ln_stats(ALPHA * x_ + (1.0 + gt) * mixv)
        dz = _ln_bwd(dx1, xhat, rstd, g)
        return [ALPHA * dz, dz * (1.0 + gt), _colsum(dh2 * x1_), _colsum(dh2), _colsum(dz * mixv),
                _colsum(dx1 * xhat), _colsum(dx1)]

    dxp, dmix, dsc2, dsh2, dgt1, dg1, db1 = _mm_rows(
        dup, Wup, mode="nt", tm=trow, seq=S, tok=tok,
        ins=[("tile", dx1p), ("tile", x1), ("tile", x2d), ("tile", mix), ("brow", sc2), ("brow", gt1), ("row", ln1_g)],
        outs=[("tile", F32, D), ("tile", BF16, D), ("acc_brow", D), ("acc_brow", D), ("acc_brow", D), ("acc_row", D),
              ("acc_row", D)],
        epilogue=ep_ln1_bwd, name="mm_dh2_ln1b")
    g_wup = _mm(h2, dup, mode="tn", tm=D, tn=1024, tk=2048, outs=[BF16], nb=dff // N_DEV, name="mm_gwup")
    (x_wup,), tok = send_grads([g_wup], "gx_wup")

    def ep_merge_bwd(dm, v):
        ya, yb, ga, gb = [t.astype(F32) for t in v]
        sa, sb = _sigmoid(ga), _sigmoid(gb)
        dg = jnp.concatenate([dm * ya * sa * (1.0 - sa), dm * yb * sb * (1.0 - sb)], axis=1)
        return [dm * sa, dm * sb, dg, _colsum(dg)]

    dy_a, dy_b, dproj, dbin_hi = _mm_rows(
        dmix, Wout, mode="nt", tm=trow, seq=S, tok=tok,
        ins=[("tile", y_a), ("tile", y_b), ("tilecol", proj, D, gate_cb), ("tilecol", proj, D, gate_cb + 1)],
        outs=[("tile", BF16, D), ("tile", BF16, D), ("tilecol", BF16, 2 * D, gate_cb // 2, din), ("acc_row", 2 * D)],
        epilogue=ep_merge_bwd, name="mm_dmerged_mb")
    g_wout = _mm(merged, dmix, mode="tn", tm=D, tn=D, tk=2048, outs=[BF16], name="mm_gwout")
    (x_wout,), tok = send_grads([g_wout.reshape(N_DEV, D // N_DEV, D)], "gx_wout")
    dya_pre = _mm(dy_a, Wol, mode="nt", tm=2048, tn=lw, tk=D, outs=[F32], tok=tok, name="mm_dya")
    dysgu = _mm(dy_b, Wos, mode="nt", tm=2048, tn=sw, tk=D, outs=[F32], name="mm_dys")
    g_wol = _mm(ya_pre.reshape(T, lw), dy_a, mode="tn", tm=lw, tn=D, tk=2048, outs=[BF16], name="mm_gwol")
    g_wos = _mm(ysgu.reshape(T, sw), dy_b, mode="tn", tm=sw, tn=D, tk=2048, outs=[BF16], nb=D // N_DEV,
                name="mm_gwos")
    (x_wol, x_wos), tok = send_grads([g_wol.reshape(N_DEV, lw // N_DEV, D), g_wos], "gx_wo")
    small_mix_b = (wconv_full, b_conv + tok[0, 0]) + small_mix[2:]
    (dproj, dbin_lo, g_wconv, g_bconv, g_wa, g_ba, g_wx, g_bx, g_lam, g_wsp, g_bsp_t, g_lvg, g_lvb) = _mix_bwd(
        proj3, hs, dya_pre.reshape(Bl, S, lw), dysgu.reshape(Bl, S, sw), dproj.reshape(Bl, S, din), lru_saved,
        *small_mix_b, tm=tmix, lw=lw, sw=sw)
    dproj2 = dproj.reshape(T, din)
    small_names = [n for n in SMALL_NAMES if n != "b_ada"]
    small_g = dict(b_in=jnp.concatenate([dbin_lo, dbin_hi], axis=-1), b_conv=g_bconv, w_rg_a=g_wa[None], b_rg_a=g_ba,
                   w_rg_x=g_wx[None], b_rg_x=g_bx, lru_lambda=g_lam, w_sp=g_wsp[None],
                   b_sp=jnp.transpose(g_bsp_t)[None], ln_v_g=g_lvg, ln_v_b=g_lvb, ln1_g=dg1, ln1_b=db1, ln2_g=dg2,
                   ln2_b=db2)
    gs_snd, gs_rcv, gs_src, gs_land, tok_s = _xstart([small_g[n] for n in small_names], True, None, "gsmall_start",
                                                      fill_own=False)
    g_win = _mm(h.reshape(T, D), dproj2, mode="tn", tm=D, tn=din // 4, tk=2048, outs=[BF16], nb=din // N_DEV,
                tok=tok_s, name="mm_gwin")
    (x_win,), tok = send_grads([g_win], "gx_win")

    def ep_final(dh, v):
        dxp_, x_, sc = v
        return [dxp_ + dh * (1.0 + sc), _colsum(dh * x_), _colsum(dh)]

    grad_x, dsc1, dsh1 = _mm_rows(dproj2, win_parts, mode="nt", tm=trow, seq=S, tok=tok,
                                  ins=[("tile", dxp), ("tile", x2d), ("brow", sc1)],
                                  outs=[("tile", F32, D), ("acc_brow", D), ("acc_brow", D)], epilogue=ep_final,
                                  name="mm_dh_final")
    grad_x = grad_x.reshape(Bl, S, D)

    dmod = jnp.concatenate([dsh1, dsc1, dgt1, dsh2, dsc2, dgt2], axis=-1).reshape(Bl, 6 * D)
    dmod_b = _blocked_cols(jnp.pad(dmod, ((0, SUBLANES - Bl), (0, 0))))
    dmod_s, gwconv_s = _exchange([dmod_b, _blocked_cols(g_wconv)], False, "xchg_dmod")
    g_wada, g_bada_mine = _ada_bwd(c_act, dmod_s.reshape(N_DEV * SUBLANES, -1))
    (g_bada_all,) = _exchange([g_bada_mine], True, "xchg_bada")

    gwdown_s = _xwait(*x_wdown, g_bada_all, False, "gx_wdown_wait")
    gwup_s = _xwait(*x_wup, g_bada_all, False, "gx_wup_wait")
    gwout_s = _xwait(*x_wout, g_bada_all, False, "gx_wout_wait")
    gwol_s = _xwait(*x_wol, g_bada_all, False, "gx_wol_wait")
    gwos_s = _xwait(*x_wos, g_bada_all, False, "gx_wos_wait")
    gwin_s = _xwait(*x_win, g_bada_all, False, "gx_win_wait")
    gs_own, gs_slots = _xwait_many(gs_src, gs_land, gs_snd, gs_rcv, g_bada_all, "gsmall_wait")
    out_g, out_d, out_m, out_v = {}, {}, {}, {}

    def adam(name, g_slots, tr):
        shp = W[name].shape
        w2, m2, v2 = [t.reshape(g_slots.shape[1:]) for t in (W[name], Mo[name], Vo[name])]
        g, d, mn, vn = _adamw(w2, g_slots, m2, v2, tr=tr, name="adam_" + name)
        out_g[name], out_d[name], out_m[name], out_v[name] = [t.reshape(shp) for t in (g, d, mn, vn)]

    adam("w_ada", g_wada[None], 256)
    adam("b_ada", g_bada_all.reshape(1, 1, 6 * D), 1)
    adam("w_in", gwin_s, 256)
    adam("w_conv", gwconv_s, 8)
    adam("w_o_lru", gwol_s, 160)
    adam("w_o_sgu", gwos_s, 256)
    adam("w_out", gwout_s, 128)
    adam("w_up", gwup_s, 256)
    adam("w_down", gwdown_s, 256)
    res_small = _adamw_many([W[n] for n in small_names], gs_slots, gs_own, [Mo[n] for n in small_names],
                            [Vo[n] for n in small_names], name="adam_small")
    for dst, vals in zip((out_g, out_d, out_m, out_v), res_small):
        dst.update(dict(zip(small_names, vals)))

    return (loss, grad_x, *[out_g[n] for n in WEIGHT_ORDER], *[out_d[n] for n in WEIGHT_ORDER],
            *[out_m[n] for n in WEIGHT_ORDER], *[out_v[n] for n in WEIGHT_ORDER])
```

```python
import functools
import math

import jax
import jax.numpy as jnp
from jax import lax
from jax.experimental import pallas as pl
from jax.experimental.pallas import tpu as pltpu

N_DEV = 8
LN_EPS = 1e-5
LRU_C = 8.0
CHUNK = 64
SGU_BLOCK = 128
ALPHA = 2.0 ** 0.25
ADAM_LR = 0.001
ADAM_B1 = 0.9
ADAM_B2 = 0.999
ADAM_EPS = 1e-08
ADAM_WD = 0.01
ADAM_STEP = 10
GELU_K0 = math.sqrt(2.0 / math.pi)
GELU_K1 = 0.044715

SUBLANES = 8
LANES = 128
VMEM_LIMIT = 56 * 1024 * 1024
WIN_PARTS = 3

F32 = jnp.float32
BF16 = jnp.bfloat16
MESH = pl.DeviceIdType.MESH


def _cparams(n_axes, big=False):
    return pltpu.CompilerParams(dimension_semantics=("arbitrary",) * n_axes,
                                vmem_limit_bytes=VMEM_LIMIT if big else None)


def _sigmoid(x):
    return 0.5 * jnp.tanh(0.5 * x) + 0.5


def _gelu(x):
    t = jnp.tanh(GELU_K0 * (x + GELU_K1 * (x * x * x)))
    return 0.5 * x * (1.0 + t)


def _gelu_and_grad(x):
    x2 = x * x
    t = jnp.tanh(GELU_K0 * (x + GELU_K1 * (x2 * x)))
    g = 0.5 * x * (1.0 + t)
    dg = 0.5 * (1.0 + t) + 0.5 * x * (1.0 - t * t) * (GELU_K0 * (1.0 + 3.0 * GELU_K1 * x2))
    return g, dg


def _expm1(x):
    p = x * (1.0 + x * (1.0 / 2.0 + x * (1.0 / 6.0 + x * (1.0 / 24.0 + x * (1.0 / 120.0)))))
    return jnp.where(jnp.abs(x) < 0.0625, p, jnp.exp(x) - 1.0)


def _log1p_pos(e):
    p = e * (1.0 - e * (1.0 / 2.0) + e * e * (1.0 / 3.0) - e * e * e * (1.0 / 4.0))
    return jnp.where(e < 1e-2, p, jnp.log(1.0 + e))


def _ln_stats(z):
    mu = jnp.mean(z, axis=-1, keepdims=True)
    zc = z - mu
    var = jnp.mean(zc * zc, axis=-1, keepdims=True)
    rstd = lax.rsqrt(var + LN_EPS)
    return zc * rstd, rstd


def _ln_bwd(dy, xhat, rstd, g):
    dxh = dy * g
    m1 = jnp.mean(dxh, axis=-1, keepdims=True)
    m2 = jnp.mean(dxh * xhat, axis=-1, keepdims=True)
    return rstd * (dxh - m1 - xhat * m2)


def _colsum(v):
    return jnp.sum(v, axis=0, keepdims=True)


def _first_step():
    return jnp.logical_and(pl.program_id(0) == 0, pl.program_id(1) == 0)


def _exchange(arrs, gather, name, after=None):
    n = len(arrs)
    n_peer = N_DEV - 1
    n_after = 0 if after is None else 1

    def body(*refs):
        ins, outs = refs[:n], refs[n + n_after:2 * n + n_after]
        send_sems, recv_sems, loc_sems = refs[2 * n + n_after:]
        x, y, c = lax.axis_index("x"), lax.axis_index("y"), lax.axis_index("c")
        me = 4 * x + 2 * y + c
        started = []
        for a in range(n):
            src_me = ins[a] if gather else ins[a].at[me]
            lc = pltpu.make_async_copy(src_me, outs[a].at[me], loc_sems.at[a])
            lc.start()
            started.append((lc, None))
        for p in range(1, N_DEV):
            px, py, pc = x ^ ((p >> 2) & 1), y ^ ((p >> 1) & 1), c ^ (p & 1)
            peer = 4 * px + 2 * py + pc
            for a in range(n):
                k = a * n_peer + (p - 1)
                src = ins[a] if gather else ins[a].at[peer]
                cp = pltpu.make_async_remote_copy(src_ref=src, dst_ref=outs[a].at[me],
                                                  send_sem=send_sems.at[k], recv_sem=recv_sems.at[k],
                                                  device_id=(px, py, pc), device_id_type=MESH)
                cp.start()
                rc = pltpu.make_async_remote_copy(src_ref=src, dst_ref=outs[a].at[peer],
                                                  send_sem=send_sems.at[k], recv_sem=recv_sems.at[k],
                                                  device_id=(px, py, pc), device_id_type=MESH)
                started.append((cp, rc))
        for cp, rc in started:
            if rc is None:
                cp.wait()
            else:
                cp.wait_send()
                rc.wait_recv()

    hbm = pl.BlockSpec(memory_space=pltpu.HBM)
    out_shape = tuple(
        jax.ShapeDtypeStruct(((N_DEV,) + a.shape) if gather else a.shape, a.dtype) for a in arrs)
    return pl.pallas_call(
        body, name=name, out_shape=out_shape,
        in_specs=[hbm] * n + [pl.BlockSpec(memory_space=pl.ANY)] * n_after, out_specs=tuple([hbm] * n),
        scratch_shapes=[pltpu.SemaphoreType.DMA((n * n_peer,)), pltpu.SemaphoreType.DMA((n * n_peer,)),
                        pltpu.SemaphoreType.DMA((n,))],
        compiler_params=pltpu.CompilerParams(has_side_effects=True),
    )(*arrs, *([after] if n_after else []))


_HBM = pl.BlockSpec(memory_space=pltpu.HBM)
_SEM = pl.BlockSpec(memory_space=pltpu.SEMAPHORE)
_EFFECT = pltpu.SideEffectType.DATAFLOW_SIDE_EFFECTING


def _peer_of(p):
    x, y, c = lax.axis_index("x"), lax.axis_index("y"), lax.axis_index("c")
    px, py, pc = x ^ ((p >> 2) & 1), y ^ ((p >> 1) & 1), c ^ (p & 1)
    return (px, py, pc), 4 * px + 2 * py + pc


def _slot(land_ref, idx, width):
    if width is None:
        return land_ref.at[idx]
    return land_ref.at[:, pl.ds(pl.multiple_of(idx * width, LANES), width)]


def _xstart(srcs, gather, after, name, cols=None, fill_own=False):
    n = len(srcs)
    cols = cols or [False] * n
    widths = [t.shape[1] if cols[a] else None for a, t in enumerate(srcs)]
    me_out = 4 * lax.axis_index("x") + 2 * lax.axis_index("y") + lax.axis_index("c")
    lands = []
    for a, t in enumerate(srcs):
        if cols[a]:
            zone, own, at = lax.empty((t.shape[0], N_DEV * t.shape[1]), t.dtype), t, (0, me_out * t.shape[1])
        elif gather:
            zone, own, at = lax.empty((N_DEV,) + t.shape, t.dtype), t[None], (me_out,) + (0,) * t.ndim
        else:
            zone, own = lax.empty(t.shape, t.dtype), lax.dynamic_index_in_dim(t, me_out, 0, keepdims=True)
            at = (me_out,) + (0,) * (t.ndim - 1)
        lands.append(lax.dynamic_update_slice(zone, own, at) if fill_own else zone)
    n_after = 0 if after is None else 1

    def body(*refs):
        src_refs, land_refs = refs[:n], refs[n:2 * n]
        refs = refs[n_after:]
        send_sems, recv_sems = refs[2 * n:3 * n], refs[3 * n:4 * n]
        token = refs[6 * n]
        me = 4 * lax.axis_index("x") + 2 * lax.axis_index("y") + lax.axis_index("c")
        for a in range(n):
            for p in range(1, N_DEV):
                dev, peer = _peer_of(p)
                pltpu.make_async_remote_copy(
                    src_ref=src_refs[a] if gather else src_refs[a].at[peer], dst_ref=_slot(land_refs[a], me, widths[a]),
                    send_sem=send_sems[a].at[p - 1], recv_sem=recv_sems[a].at[p - 1],
                    device_id=dev, device_id_type=MESH).start()
        token[...] = jnp.zeros_like(token)

    sems = tuple(pltpu.SemaphoreType.DMA((N_DEV - 1,)) for _ in range(2 * n))
    thru = tuple(pltpu.HBM(t.shape, t.dtype) for t in list(srcs) + list(lands))
    res = pl.pallas_call(
        body, name=name,
        out_shape=sems + thru + (jax.ShapeDtypeStruct((SUBLANES, LANES), F32),),
        in_specs=[_HBM] * (2 * n) + [pl.BlockSpec(memory_space=pl.ANY)] * n_after,
        out_specs=tuple([_SEM] * (2 * n) + [_HBM] * (2 * n) + [pl.BlockSpec(memory_space=pltpu.VMEM)]),
        input_output_aliases={i: 2 * n + i for i in range(2 * n)},
        compiler_params=pltpu.CompilerParams(has_side_effects=_EFFECT),
    )(*[pltpu.with_memory_space_constraint(t, pltpu.HBM) for t in list(srcs) + list(lands)],
      *([after] if n_after else []))
    return res[:n], res[n:2 * n], res[2 * n:3 * n], res[3 * n:4 * n], res[4 * n]


def _xwait(src, land, send_sem, recv_sem, after, gather, name, col=False, place=True):
    width = src.shape[1] if col else None

    def body(src_ref, land_ref, send_ref, recv_ref, after_ref, src_dead, land_out):
        del after_ref, src_dead, land_out
        for p in range(1, N_DEV):
            dev, peer = _peer_of(p)
            cp = pltpu.make_async_remote_copy(
                src_ref=src_ref if gather else src_ref.at[peer], dst_ref=_slot(land_ref, peer, width),
                send_sem=send_ref.at[p - 1], recv_sem=recv_ref.at[p - 1], device_id=dev, device_id_type=MESH)
            cp.wait_send()
            cp.wait_recv()

    src_done, landed = pl.pallas_call(
        body, name=name, out_shape=(pltpu.HBM(src.shape, src.dtype), pltpu.HBM(land.shape, land.dtype)),
        in_specs=[_HBM, _HBM, _SEM, _SEM, pl.BlockSpec(memory_space=pl.ANY)], out_specs=(_HBM, _HBM),
        input_output_aliases={0: 0, 1: 1},
        compiler_params=pltpu.CompilerParams(has_side_effects=_EFFECT),
    )(src, land, send_sem, recv_sem, after)
    if not place:
        return src_done, landed
    me = 4 * lax.axis_index("x") + 2 * lax.axis_index("y") + lax.axis_index("c")
    return _place_own(landed, src_done, me, col, gather, name + "_own")


def _place_own(zone, src, me, col, gather, name):
    if col:
        R, C = src.shape
        src_spec = lambda tr: pl.BlockSpec((tr, C), lambda i, me_ref: (i, 0))
        out_spec = lambda tr: pl.BlockSpec((tr, C), lambda i, me_ref: (i, me_ref[0]))
    else:
        R, C = zone.shape[1:]
        src_spec = ((lambda tr: pl.BlockSpec((tr, C), lambda i, me_ref: (i, 0))) if gather else
                    (lambda tr: pl.BlockSpec((None, tr, C), lambda i, me_ref: (me_ref[0], i, 0))))
        out_spec = lambda tr: pl.BlockSpec((None, tr, C), lambda i, me_ref: (me_ref[0], i, 0))
    tr = R if R <= 512 else 256
    assert R % tr == 0, (name, R, tr)

    def body(me_ref, src_ref, zone_ref, out_ref):
        del me_ref, zone_ref
        out_ref[...] = src_ref[...]

    return pl.pallas_call(
        body, name=name, out_shape=jax.ShapeDtypeStruct(zone.shape, zone.dtype),
        grid_spec=pltpu.PrefetchScalarGridSpec(
            num_scalar_prefetch=1, grid=(R // tr,),
            in_specs=[src_spec(tr), pl.BlockSpec(memory_space=pl.ANY)], out_specs=out_spec(tr)),
        input_output_aliases={2: 0},
    )(jnp.reshape(me, (1,)).astype(jnp.int32), src, zone)


def _xwait_many(srcs, lands, send_sems, recv_sems, after, name):
    n = len(srcs)

    def body(*refs):
        src_refs, land_refs = refs[:n], refs[n:2 * n]
        snd, rcv = refs[2 * n:3 * n], refs[3 * n:4 * n]
        for a in range(n):
            for p in range(1, N_DEV):
                dev, peer = _peer_of(p)
                cp = pltpu.make_async_remote_copy(
                    src_ref=src_refs[a], dst_ref=land_refs[a].at[peer], send_sem=snd[a].at[p - 1],
                    recv_sem=rcv[a].at[p - 1], device_id=dev, device_id_type=MESH)
                cp.wait_send()
                cp.wait_recv()

    res = pl.pallas_call(
        body, name=name, out_shape=tuple(pltpu.HBM(t.shape, t.dtype) for t in list(srcs) + list(lands)),
        in_specs=[_HBM] * (2 * n) + [_SEM] * (2 * n) + [pl.BlockSpec(memory_space=pl.ANY)],
        out_specs=tuple([_HBM] * (2 * n)), input_output_aliases={i: i for i in range(2 * n)},
        compiler_params=pltpu.CompilerParams(has_side_effects=_EFFECT),
    )(*srcs, *lands, *send_sems, *recv_sems, after)
    return res[:n], res[n:]


def _mm(a, b, *, mode, tm, tn, tk, outs, epilogue=None, extras=(), nb=None, tok=None, scatter=None, into=None, name):
    if mode == "nn":
        (M, K), (_, N) = a.shape, b.shape
    elif mode == "nt":
        (M, K), (N, _) = a.shape, b.shape
    else:
        (K, M), (_, N) = a.shape, b.shape
    tm, tn, tk = min(tm, M), min(tn, N), min(tk, K)
    assert M % tm == 0 and N % tn == 0 and K % tk == 0, (name, M, N, K, tm, tn, tk)
    if mode == "nn":
        a_spec = pl.BlockSpec((tm, tk), lambda i, j, k: (i, k))
        b_spec = pl.BlockSpec((tk, tn), lambda i, j, k: (k, j))
        dims = (((1,), (0,)), ((), ()))
    elif mode == "nt":
        a_spec = pl.BlockSpec((tm, tk), lambda i, j, k: (i, k))
        b_spec = pl.BlockSpec((tn, tk), lambda i, j, k: (j, k))
        dims = (((1,), (1,)), ((), ()))
    else:
        a_spec = pl.BlockSpec((tk, tm), lambda i, j, k: (k, i))
        b_spec = pl.BlockSpec((tk, tn), lambda i, j, k: (k, j))
        dims = (((0,), (0,)), ((), ()))
    nk = K // tk
    n_ex, n_out = len(extras), len(outs)
    n_tok = 0 if tok is None else 1
    nbytes = lambda d: jnp.dtype(d).itemsize
    vmem_est = (2 * (tm * tk * nbytes(a.dtype) + tk * tn * nbytes(b.dtype)
                     + sum(tm * tn * nbytes(e.dtype) for e, kind in extras if kind == "tile")
                     + sum(tm * tn * nbytes(d) for d in outs)) + tm * tn * 4)
    assert vmem_est <= VMEM_LIMIT, (name, vmem_est)
    if epilogue is None:
        epilogue = lambda acc, ex: tuple(acc.astype(d) for d in outs)

    n_into = 0 if into is None else 1

    def body(a_ref, b_ref, *refs):
        refs = refs[n_tok:]
        ex_refs, out_refs = refs[:n_ex], refs[n_ex + n_into:n_ex + n_into + n_out]

        def finish(acc):
            res = epilogue(acc, [r[...] for r in ex_refs])
            for o_ref, v in zip(out_refs, res):
                if nb is None:
                    o_ref[...] = v.astype(o_ref.dtype)
                else:
                    for q in range(tn // nb):
                        o_ref[q] = v[:, q * nb:(q + 1) * nb].astype(o_ref.dtype)

        part = lax.dot_general(a_ref[...], b_ref[...], dims, preferred_element_type=F32)
        if nk == 1:
            finish(part)
        else:
            acc_ref = refs[n_ex + n_into + n_out]
            k = pl.program_id(2)

            @pl.when(k == 0)
            def _():
                acc_ref[...] = part

            @pl.when(k > 0)
            def _():
                acc_ref[...] += part

            @pl.when(k == nk - 1)
            def _():
                finish(acc_ref[...])

    col = (lambda j: j) if scatter is None else (lambda j: scatter[0] * j + scatter[1])
    ex_specs = [pl.BlockSpec((tm, tn), lambda i, j, k: (i, j)) if kind == "tile"
                else pl.BlockSpec((1, tn), lambda i, j, k: (0, col(j))) for _, kind in extras]
    if nb is not None:
        assert tn % nb == 0, (name, tn, nb)
        o_spec = pl.BlockSpec((tn // nb, tm, nb), lambda i, j, k: (j, i, 0))
        o_shape = (N // nb, M, nb)
    else:
        o_spec = pl.BlockSpec((tm, tn), lambda i, j, k: (i, col(j)))
        o_shape = (M, N if scatter is None else scatter[2])
    assert n_into == 0 or n_out == 1
    res = pl.pallas_call(
        body, name=name, grid=(M // tm, N // tn, nk),
        in_specs=[a_spec, b_spec] + [pl.BlockSpec((SUBLANES, LANES), lambda i, j, k: (0, 0))] * n_tok + ex_specs
                 + [pl.BlockSpec(memory_space=pl.ANY)] * n_into,
        out_specs=tuple([o_spec] * n_out),
        out_shape=tuple(jax.ShapeDtypeStruct(o_shape, d) for d in outs),
        input_output_aliases={2 + n_tok + n_ex: 0} if n_into else {},
        scratch_shapes=[pltpu.VMEM((tm, tn), F32)] if nk > 1 else [],
        compiler_params=_cparams(3, big=True),
    )(a, b, *([tok] if n_tok else []), *[e for e, _ in extras], *([into] if n_into else []))
    return res[0] if n_out == 1 else res


def _mm_rows(a, b, *, mode, tm, seq, ins, outs, epilogue, tok=None, name):
    M, K = a.shape
    b_parts = list(b) if isinstance(b, (list, tuple)) else [b]
    n_part = len(b_parts)
    assert n_part == 1 or mode == "nt"
    N = b_parts[0].shape[1] if mode == "nn" else b_parts[0].shape[0]
    tm = min(tm, M)
    assert M % tm == 0 and seq % tm == 0, (name, M, seq, tm)
    tpb = seq // tm
    n_b = M // seq
    dims = (((1,), (0,)), ((), ())) if mode == "nn" else (((1,), (1,)), ((), ()))
    n_tok = 0 if tok is None else 1
    n_in, n_out = len(ins), len(outs)

    in_specs, in_arrs = [], []
    for spec in ins:
        kind, arr = spec[0], spec[1]
        in_arrs.append(arr)
        if kind == "tile":
            in_specs.append(pl.BlockSpec((tm, arr.shape[1]), lambda i: (i, 0)))
        elif kind == "tilecol":
            in_specs.append(pl.BlockSpec((tm, spec[2]), lambda i, cb=spec[3]: (i, cb)))
        elif kind == "row":
            in_specs.append(pl.BlockSpec(arr.shape, lambda i: (0, 0)))
        else:
            in_specs.append(pl.BlockSpec((None, 1, arr.shape[2]), lambda i: (i // tpb, 0, 0)))
    out_specs, out_shapes = [], []
    for spec in outs:
        kind = spec[0]
        if kind == "tile":
            out_specs.append(pl.BlockSpec((tm, spec[2]), lambda i: (i, 0)))
            out_shapes.append(jax.ShapeDtypeStruct((M, spec[2]), spec[1]))
        elif kind == "tilecol":
            out_specs.append(pl.BlockSpec((tm, spec[2]), lambda i, cb=spec[3]: (i, cb)))
            out_shapes.append(jax.ShapeDtypeStruct((M, spec[4]), spec[1]))
        elif kind == "acc_row":
            out_specs.append(pl.BlockSpec((1, spec[1]), lambda i: (0, 0)))
            out_shapes.append(jax.ShapeDtypeStruct((1, spec[1]), F32))
        elif kind == "acc_brow":
            out_specs.append(pl.BlockSpec((None, 1, spec[1]), lambda i: (i // tpb, 0, 0)))
            out_shapes.append(jax.ShapeDtypeStruct((n_b, 1, spec[1]), F32))
        else:
            out_specs.append(pl.BlockSpec((SUBLANES, LANES), lambda i: (0, 0)))
            out_shapes.append(jax.ShapeDtypeStruct((SUBLANES, LANES), F32))

    def body(a_ref, *refs):
        b_refs, refs = refs[:n_part], refs[n_part + n_tok:]
        in_refs, out_refs = refs[:n_in], refs[n_in:n_in + n_out]
        i = pl.program_id(0)
        if n_part == 1:
            prod = lax.dot_general(a_ref[...], b_refs[0][...], dims, preferred_element_type=F32)
        else:
            w = b_parts[0].shape[1] // N_DEV
            prod = None
            for q in range(n_part):
                a_q = jnp.concatenate([a_ref[:, (n_part * j + q) * w:(n_part * j + q + 1) * w] for j in range(N_DEV)],
                                      axis=1)
                pq = lax.dot_general(a_q, b_refs[q][...], dims, preferred_element_type=F32)
                prod = pq if prod is None else prod + pq
        vals = epilogue(prod, [r[...] for r in in_refs])
        for spec, o_ref, v in zip(outs, out_refs, vals):
            kind = spec[0]
            if kind in ("tile", "tilecol"):
                o_ref[...] = v.astype(o_ref.dtype)
            else:
                first = (i % tpb == 0) if kind == "acc_brow" else (i == 0)

                @pl.when(first)
                def _(o_ref=o_ref, v=v):
                    o_ref[...] = jnp.broadcast_to(v, o_ref.shape)

                @pl.when(jnp.logical_not(first))
                def _(o_ref=o_ref, v=v):
                    o_ref[...] += v

    res = pl.pallas_call(
        body, name=name, grid=(M // tm,),
        in_specs=[pl.BlockSpec((tm, K), lambda i: (i, 0))]
                 + [pl.BlockSpec(bp.shape, lambda i: (0, 0), pipeline_mode=pl.Buffered(1)) for bp in b_parts]
                 + [pl.BlockSpec((SUBLANES, LANES), lambda i: (0, 0))] * n_tok + in_specs,
        out_specs=tuple(out_specs), out_shape=tuple(out_shapes),
        compiler_params=_cparams(1, big=True),
    )(a, *b_parts, *([tok] if n_tok else []), *in_arrs)
    return res


def _tok_spec(ts, width, col_block=0):
    return pl.BlockSpec((None, ts, width), lambda b, s: (b, s, col_block))


def _brow_spec(width):
    return pl.BlockSpec((None, 1, width), lambda b, s: (b, 0, 0))


def _vec_spec(width):
    return pl.BlockSpec((1, width), lambda b, s: (0, 0))


def _modulate(x, sc, sh, ts):
    Bl, S, D = x.shape

    def body(x_ref, sc_ref, sh_ref, o_ref):
        o_ref[...] = (x_ref[...] * (1.0 + sc_ref[...]) + sh_ref[...]).astype(BF16)

    return pl.pallas_call(
        body, name="modulate", grid=(Bl, S // ts),
        in_specs=[_tok_spec(ts, D), _brow_spec(D), _brow_spec(D)],
        out_specs=_tok_spec(ts, D), out_shape=jax.ShapeDtypeStruct((Bl, S, D), BF16),
        compiler_params=_cparams(2),
    )(x, sc, sh)


def _mix_fwd(proj, w_conv, b_conv, w_rg_a, b_rg_a, w_rg_x, b_rg_x, lam, w_sp, b_sp_t, ln_v_g, ln_v_b, *, tm, lw, sw):
    Bl, S, _ = proj.shape
    heads, hd = w_rg_a.shape[0], w_rg_a.shape[1]
    groups = w_sp.shape[0]
    cw = 2 * lw + 2 * sw
    nblk = tm // SGU_BLOCK

    G = tm // SUBLANES
    nc = lw // LANES

    def body(p_ref, wc_ref, bc_ref, wa_ref, ba_ref, wx_ref, bx_ref, lam_ref, wsp_ref, bsp_ref, lg_ref, lb_ref,
             hs_ref, ya_ref, ys_ref, xc_ref, r_ref, ig_ref, a_ref, m_ref,
             xext, hnat, hcar, h7_scr, a7_scr, hp_scr):
        s = pl.program_id(1)

        @pl.when(s == 0)
        def _():
            xext[:, 0:SUBLANES, :] = jnp.zeros((nc, SUBLANES, LANES), F32)
            hcar[...] = jnp.zeros_like(hcar)

        @pl.when(s > 0)
        def _():
            xext[:, 0:SUBLANES, :] = xext[:, tm:tm + SUBLANES, :]

        for c in range(nc):
            xext[c, SUBLANES:SUBLANES + tm, :] = p_ref[:, c * LANES:(c + 1) * LANES].astype(F32)
        gl = p_ref[:, lw:2 * lw].astype(F32)

        def slab(ref3, start):
            return jnp.concatenate([ref3[c, pl.ds(start, G, stride=SUBLANES), :] for c in range(nc)], axis=1)

        xs = {st: slab(xext, st) for st in range(SUBLANES - 3, 2 * SUBLANES)}
        xc_slabs = []
        for j in range(SUBLANES):
            acc = bc_ref[...] + xs[SUBLANES + j] * wc_ref[3:4, :]
            for k in (1, 2, 3):
                acc = acc + xs[SUBLANES + j - k] * wc_ref[3 - k:4 - k, :]
            xc_slabs.append(acc)
        xc = jnp.concatenate(xc_slabs, axis=0)

        xcb = xc.astype(BF16)
        pa = jnp.concatenate([jnp.dot(xcb[:, h * hd:(h + 1) * hd], wa_ref[h], preferred_element_type=F32)
                              for h in range(heads)], axis=1) + ba_ref[...]
        px = jnp.concatenate([jnp.dot(xcb[:, h * hd:(h + 1) * hd], wx_ref[h], preferred_element_type=F32)
                              for h in range(heads)], axis=1) + bx_ref[...]
        r = _sigmoid(pa)
        ig = _sigmoid(px)
        nl = -lam_ref[...]
        big_l = -LRU_C * (jnp.maximum(nl, 0.0) + _log1p_pos(jnp.exp(-jnp.abs(nl))))
        la = big_l * r
        a = jnp.exp(la)
        th = jnp.tanh(la)
        msq = (-2.0 * th) / (1.0 - th)
        m = msq * lax.rsqrt(jnp.maximum(msq, 1e-30))
        bin_ = m * (ig * xc)
        xc_ref[...] = xc
        r_ref[...] = r
        ig_ref[...] = ig
        a_ref[...] = a
        m_ref[...] = m

        h0 = [bin_[0:G]]
        cp = [a[0:G]]
        for j in range(1, SUBLANES):
            aj = a[j * G:(j + 1) * G]
            h0.append(aj * h0[j - 1] + bin_[j * G:(j + 1) * G])
            cp.append(aj * cp[j - 1])
        h7_scr[...] = h0[SUBLANES - 1]
        a7_scr[...] = cp[SUBLANES - 1]
        carry = hcar[0:1, :]
        for g in range(G):
            hp_scr[g:g + 1, :] = carry
            carry = h7_scr[g:g + 1, :] + a7_scr[g:g + 1, :] * carry
        hcar[0:1, :] = carry
        hprev = hp_scr[...]
        for j in range(SUBLANES):
            hj = h0[j] + cp[j] * hprev
            for c in range(nc):
                hnat[c, pl.ds(j, G, stride=SUBLANES), :] = hj[:, c * LANES:(c + 1) * LANES]
        hs = jnp.concatenate([hnat[c] for c in range(nc)], axis=1)
        hs_ref[...] = hs
        ya_ref[...] = (hs * _gelu(gl)).astype(BF16)

        gu = _gelu(p_ref[:, 2 * lw:2 * lw + sw].astype(F32))
        gv = _gelu(p_ref[:, 2 * lw + sw:cw].astype(F32))
        xhat, _ = _ln_stats(gv)
        vn = (xhat * lg_ref[...] + lb_ref[...]).astype(BF16)
        tpos = lax.broadcasted_iota(jnp.int32, (SGU_BLOCK, SGU_BLOCK), 0) // CHUNK
        spos = lax.broadcasted_iota(jnp.int32, (SGU_BLOCK, SGU_BLOCK), 1) // CHUNK
        gw = sw // groups
        rows_out = []
        for blk in range(nblk):
            r0 = blk * SGU_BLOCK
            cols = []
            for g in range(groups):
                wm = jnp.where(spos <= tpos, wsp_ref[g], 0.0).astype(BF16)
                mixed = jnp.dot(wm, vn[r0:r0 + SGU_BLOCK, g * gw:(g + 1) * gw], preferred_element_type=F32)
                cols.append(mixed + bsp_ref[:, g:g + 1])
            rows_out.append(jnp.concatenate(cols, axis=1))
        mixed_all = jnp.concatenate(rows_out, axis=0) if nblk > 1 else rows_out[0]
        ys_ref[...] = (gu * mixed_all).astype(BF16)

    full = lambda shp: pl.BlockSpec(shp, lambda b, s: (0,) * len(shp))
    return pl.pallas_call(
        body, name="mix_fwd", grid=(Bl, S // tm),
        in_specs=[_tok_spec(tm, cw), full(w_conv.shape), full(b_conv.shape), full(w_rg_a.shape), full(b_rg_a.shape),
                  full(w_rg_x.shape), full(b_rg_x.shape), full(lam.shape), full(w_sp.shape), full(b_sp_t.shape),
                  full(ln_v_g.shape), full(ln_v_b.shape)],
        out_specs=(_tok_spec(tm, lw), _tok_spec(tm, lw), _tok_spec(tm, sw)) + (_tok_spec(tm, lw),) * 5,
        out_shape=(jax.ShapeDtypeStruct((Bl, S, lw), F32), jax.ShapeDtypeStruct((Bl, S, lw), BF16),
                   jax.ShapeDtypeStruct((Bl, S, sw), BF16)) + (jax.ShapeDtypeStruct((Bl, S, lw), F32),) * 5,
        scratch_shapes=[pltpu.VMEM((nc, tm + SUBLANES, LANES), F32), pltpu.VMEM((nc, tm, LANES), F32),
                        pltpu.VMEM((SUBLANES, lw), F32), pltpu.VMEM((G, lw), F32), pltpu.VMEM((G, lw), F32),
                        pltpu.VMEM((G, lw), F32)],
        compiler_params=_cparams(2, big=True),
    )(proj, w_conv, b_conv, w_rg_a, b_rg_a, w_rg_x, b_rg_x, lam, w_sp, b_sp_t, ln_v_g, ln_v_b)


def _merge_fwd(proj, y_a, y_b, *, ts, d):
    Bl, S, din = proj.shape
    gcol = (din - 2 * d) // (2 * d)
    assert gcol * 2 * d == din - 2 * d

    def body(g_ref, ya_ref, yb_ref, o_ref):
        sa = _sigmoid(g_ref[:, 0:d].astype(F32))
        sb = _sigmoid(g_ref[:, d:2 * d].astype(F32))
        o_ref[...] = (sa * ya_ref[...].astype(F32) + sb * yb_ref[...].astype(F32)).astype(BF16)

    return pl.pallas_call(
        body, name="merge_fwd", grid=(Bl, S // ts),
        in_specs=[_tok_spec(ts, 2 * d, gcol), _tok_spec(ts, d), _tok_spec(ts, d)],
        out_specs=_tok_spec(ts, d), out_shape=jax.ShapeDtypeStruct((Bl, S, d), BF16),
        compiler_params=_cparams(2),
    )(proj, y_a, y_b)


def _ln1_fwd(x, mix, gt1, g1, b1, sc2, sh2, *, ts):
    Bl, S, D = x.shape

    def body(x_ref, mix_ref, gt_ref, g_ref, b_ref, sc_ref, sh_ref, x1_ref, h2_ref):
        z = ALPHA * x_ref[...] + (1.0 + gt_ref[...]) * mix_ref[...].astype(F32)
        xhat, _ = _ln_stats(z)
        x1 = xhat * g_ref[...] + b_ref[...]
        x1_ref[...] = x1
        h2_ref[...] = (x1 * (1.0 + sc_ref[...]) + sh_ref[...]).astype(BF16)

    return pl.pallas_call(
        body, name="ln1_fwd", grid=(Bl, S // ts),
        in_specs=[_tok_spec(ts, D), _tok_spec(ts, D), _brow_spec(D), _vec_spec(D), _vec_spec(D), _brow_spec(D),
                  _brow_spec(D)],
        out_specs=(_tok_spec(ts, D), _tok_spec(ts, D)),
        out_shape=(jax.ShapeDtypeStruct((Bl, S, D), F32), jax.ShapeDtypeStruct((Bl, S, D), BF16)),
        compiler_params=_cparams(2),
    )(x, mix, gt1, g1, b1, sc2, sh2)


def _ln2_loss(x1, f, tgt, gt2, g2, b2, *, ts):
    Bl, S, D = x1.shape

    def body(x1_ref, f_ref, t_ref, gt_ref, g_ref, b_ref, df_ref, dx1_ref, dgt_ref, dg_ref, db_ref, loss_ref):
        s = pl.program_id(1)

        @pl.when(_first_step())
        def _():
            dg_ref[...] = jnp.zeros_like(dg_ref)
            db_ref[...] = jnp.zeros_like(db_ref)
            loss_ref[...] = jnp.zeros_like(loss_ref)

        @pl.when(s == 0)
        def _():
            dgt_ref[...] = jnp.zeros_like(dgt_ref)

        fv = f_ref[...]
        z = ALPHA * x1_ref[...] + (1.0 + gt_ref[...]) * fv
        xhat, rstd = _ln_stats(z)
        x2 = xhat * g_ref[...] + b_ref[...]
        err = x2 - t_ref[...]
        loss_ref[...] += 0.5 * jnp.sum(jnp.mean(err * err, axis=-1, keepdims=True))
        dy = err * (1.0 / D)
        dg_ref[...] += _colsum(dy * xhat)
        db_ref[...] += _colsum(dy)
        dz = _ln_bwd(dy, xhat, rstd, g_ref[...])
        dx1_ref[...] = ALPHA * dz
        dgt_ref[...] += _colsum(dz * fv)
        df_ref[...] = (dz * (1.0 + gt_ref[...])).astype(BF16)

    return pl.pallas_call(
        body, name="ln2_loss", grid=(Bl, S // ts),
        in_specs=[_tok_spec(ts, D), _tok_spec(ts, D), _tok_spec(ts, D), _brow_spec(D), _vec_spec(D), _vec_spec(D)],
        out_specs=(_tok_spec(ts, D), _tok_spec(ts, D), _brow_spec(D), _vec_spec(D), _vec_spec(D),
                   pl.BlockSpec((SUBLANES, LANES), lambda b, s: (0, 0))),
        out_shape=(jax.ShapeDtypeStruct((Bl, S, D), BF16), jax.ShapeDtypeStruct((Bl, S, D), F32),
                   jax.ShapeDtypeStruct((Bl, 1, D), F32), jax.ShapeDtypeStruct((1, D), F32),
                   jax.ShapeDtypeStruct((1, D), F32), jax.ShapeDtypeStruct((SUBLANES, LANES), F32)),
        compiler_params=_cparams(2),
    )(x1, f, tgt, gt2, g2, b2)


def _ln1_bwd(dx1p, dh2, x1, x, mix, sc2, gt1, g1, *, ts):
    Bl, S, D = x.shape

    def body(dx1p_ref, dh2_ref, x1_ref, x_ref, mix_ref, sc_ref, gt_ref, g_ref,
             dxp_ref, dmix_ref, dsc_ref, dsh_ref, dgt_ref, dg_ref, db_ref):
        s = pl.program_id(1)

        @pl.when(_first_step())
        def _():
            dg_ref[...] = jnp.zeros_like(dg_ref)
            db_ref[...] = jnp.zeros_like(db_ref)

        @pl.when(s == 0)
        def _():
            dsc_ref[...] = jnp.zeros_like(dsc_ref)
            dsh_ref[...] = jnp.zeros_like(dsh_ref)
            dgt_ref[...] = jnp.zeros_like(dgt_ref)

        dh2 = dh2_ref[...].astype(F32)
        mixv = mix_ref[...].astype(F32)
        dsc_ref[...] += _colsum(dh2 * x1_ref[...])
        dsh_ref[...] += _colsum(dh2)
        dx1 = dx1p_ref[...] + dh2 * (1.0 + sc_ref[...])
        z = ALPHA * x_ref[...] + (1.0 + gt_ref[...]) * mixv
        xhat, rstd = _ln_stats(z)
        dg_ref[...] += _colsum(dx1 * xhat)
        db_ref[...] += _colsum(dx1)
        dz = _ln_bwd(dx1, xhat, rstd, g_ref[...])
        dxp_ref[...] = ALPHA * dz
        dgt_ref[...] += _colsum(dz * mixv)
        dmix_ref[...] = (dz * (1.0 + gt_ref[...])).astype(BF16)

    return pl.pallas_call(
        body, name="ln1_bwd", grid=(Bl, S // ts),
        in_specs=[_tok_spec(ts, D)] * 5 + [_brow_spec(D), _brow_spec(D), _vec_spec(D)],
        out_specs=(_tok_spec(ts, D), _tok_spec(ts, D), _brow_spec(D), _brow_spec(D), _brow_spec(D), _vec_spec(D),
                   _vec_spec(D)),
        out_shape=(jax.ShapeDtypeStruct((Bl, S, D), F32), jax.ShapeDtypeStruct((Bl, S, D), BF16),
                   jax.ShapeDtypeStruct((Bl, 1, D), F32), jax.ShapeDtypeStruct((Bl, 1, D), F32),
                   jax.ShapeDtypeStruct((Bl, 1, D), F32), jax.ShapeDtypeStruct((1, D), F32),
                   jax.ShapeDtypeStruct((1, D), F32)),
        compiler_params=_cparams(2),
    )(dx1p, dh2, x1, x, mix, sc2, gt1, g1)


def _merge_bwd(dmerged, y_a, y_b, proj, *, ts, d):
    Bl, S, din = proj.shape
    gcol = (din - 2 * d) // (2 * d)

    def body(dm_ref, ya_ref, yb_ref, g_ref, dya_ref, dyb_ref, dp_ref, db_ref):
        @pl.when(_first_step())
        def _():
            db_ref[...] = jnp.zeros_like(db_ref)

        dm = dm_ref[...].astype(F32)
        sa = _sigmoid(g_ref[:, 0:d].astype(F32))
        sb = _sigmoid(g_ref[:, d:2 * d].astype(F32))
        dya_ref[...] = (dm * sa).astype(BF16)
        dyb_ref[...] = (dm * sb).astype(BF16)
        dga = dm * ya_ref[...].astype(F32) * sa * (1.0 - sa)
        dgb = dm * yb_ref[...].astype(F32) * sb * (1.0 - sb)
        dp_ref[:, 0:d] = dga.astype(BF16)
        dp_ref[:, d:2 * d] = dgb.astype(BF16)
        db_ref[:, 0:d] += _colsum(dga)
        db_ref[:, d:2 * d] += _colsum(dgb)

    return pl.pallas_call(
        body, name="merge_bwd", grid=(Bl, S // ts),
        in_specs=[_tok_spec(ts, d), _tok_spec(ts, d), _tok_spec(ts, d), _tok_spec(ts, 2 * d, gcol)],
        out_specs=(_tok_spec(ts, d), _tok_spec(ts, d), _tok_spec(ts, 2 * d, gcol), _vec_spec(2 * d)),
        out_shape=(jax.ShapeDtypeStruct((Bl, S, d), BF16), jax.ShapeDtypeStruct((Bl, S, d), BF16),
                   jax.ShapeDtypeStruct((Bl, S, din), BF16), jax.ShapeDtypeStruct((1, 2 * d), F32)),
        compiler_params=_cparams(2),
    )(dmerged, y_a, y_b, proj)


def _mix_bwd(proj, hs, dya, dys, dproj, saved, w_conv, b_conv, w_rg_a, b_rg_a, w_rg_x, b_rg_x, lam, w_sp, b_sp_t,
             ln_v_g, ln_v_b, *, tm, lw, sw):
    Bl, S, din = proj.shape
    heads, hd = w_rg_a.shape[0], w_rg_a.shape[1]
    groups = w_sp.shape[0]
    gw = sw // groups
    cw = 2 * lw + 2 * sw
    nblk = tm // SGU_BLOCK
    n_s = S // tm
    per8 = tm // SUBLANES
    halo_rows = 2 * SUBLANES

    G = tm // SUBLANES
    nc = lw // LANES

    def body(p_ref, xh_ref, hs_ref, hh_ref, dya_ref, dys_ref, dpin_ref, xc_ref, r_ref, ig_ref, a_ref, m_ref,
             wc_ref, bc_ref, wa_ref, ba_ref, wx_ref, bx_ref, lam_ref, wsp_ref, bsp_ref, lg_ref, lb_ref,
             dp_ref, dbin_ref, dwc_ref, dbc_ref, dwa_ref, dba_ref, dwx_ref, dbx_ref, dlam_ref, dwsp_ref, dbsp_ref,
             dlg_ref, dlb_ref,
             xext, hext, dnat, dxext, dhcar, g00_scr, p0_scr, a0_scr, cin_scr):
        del dpin_ref
        sr = pl.program_id(1)
        first_tile = sr == n_s - 1

        @pl.when(_first_step())
        def _():
            for ref in (dbin_ref, dwc_ref, dbc_ref, dwa_ref, dba_ref, dwx_ref, dbx_ref, dlam_ref, dwsp_ref, dbsp_ref,
                        dlg_ref, dlb_ref):
                ref[...] = jnp.zeros_like(ref)

        @pl.when(sr == 0)
        def _():
            dhcar[...] = jnp.zeros_like(dhcar)
            dxext[:, tm:tm + SUBLANES, :] = jnp.zeros((nc, SUBLANES, LANES), F32)

        @pl.when(sr > 0)
        def _():
            dxext[:, tm:tm + SUBLANES, :] = dxext[:, 0:SUBLANES, :]

        def slab(ref3, start):
            return jnp.concatenate([ref3[c, pl.ds(start, G, stride=SUBLANES), :] for c in range(nc)], axis=1)

        def put_slab(ref3, j, val):
            for c in range(nc):
                ref3[c, pl.ds(j, G, stride=SUBLANES), :] = val[:, c * LANES:(c + 1) * LANES]

        keep = jnp.where(first_tile, 0.0, 1.0)
        xprev = xh_ref[...].astype(F32)[halo_rows - SUBLANES:halo_rows] * keep
        hsv = hs_ref[...]
        hprev8 = hh_ref[...] * keep
        for c in range(nc):
            cs = slice(c * LANES, (c + 1) * LANES)
            xext[c, 0:SUBLANES, :] = xprev[:, cs]
            xext[c, SUBLANES:SUBLANES + tm, :] = p_ref[:, cs].astype(F32)
            hext[c, 0:SUBLANES, :] = hprev8[:, cs]
            hext[c, SUBLANES:SUBLANES + tm, :] = hsv[:, cs]
        gl = p_ref[:, lw:2 * lw].astype(F32)
        ggl, dggl = _gelu_and_grad(gl)
        dyav = dya_ref[...].astype(F32)
        dhs = dyav * ggl
        dgl = dyav * hsv * dggl
        dp_ref[:, lw:2 * lw] = dgl.astype(BF16)
        dbin_ref[:, lw:2 * lw] += _colsum(dgl)
        for c in range(nc):
            dnat[c] = dhs[:, c * LANES:(c + 1) * LANES]

        xc, r, ig, a, m = xc_ref[...], r_ref[...], ig_ref[...], a_ref[...], m_ref[...]
        xcb = xc.astype(BF16)
        nl = -lam_ref[...]
        big_l = -LRU_C * (jnp.maximum(nl, 0.0) + _log1p_pos(jnp.exp(-jnp.abs(nl))))

        g0 = [None] * SUBLANES
        pp = [None] * SUBLANES
        g0[SUBLANES - 1] = slab(dnat, SUBLANES - 1)
        for j in range(SUBLANES - 2, -1, -1):
            an = a[(j + 1) * G:(j + 2) * G]
            g0[j] = slab(dnat, j) + an * g0[j + 1]
            pp[j] = an if j == SUBLANES - 2 else an * pp[j + 1]
        g00_scr[...] = g0[0]
        p0_scr[...] = pp[0]
        a0_scr[...] = a[0:G]
        cin = dhcar[0:1, :]
        for g in range(G - 1, -1, -1):
            cin_scr[g:g + 1, :] = cin
            cin = a0_scr[g:g + 1, :] * (g00_scr[g:g + 1, :] + p0_scr[g:g + 1, :] * cin)
        dhcar[0:1, :] = cin
        cinv = cin_scr[...]
        dh = jnp.concatenate([g0[j] + pp[j] * cinv for j in range(SUBLANES - 1)] + [g0[SUBLANES - 1] + cinv], axis=0)

        hprev = jnp.concatenate([slab(hext, SUBLANES - 1 + j) for j in range(SUBLANES)], axis=0)
        da = dh * hprev
        ixc = ig * xc
        dm = dh * ixc
        dixc = dh * m
        di = dixc * xc
        dxc = dixc * ig
        dla = da * a - dm * (a * a) / m
        dlam_ref[...] += _colsum(dla * r) * (LRU_C * _sigmoid(nl))
        dr = dla * big_l
        dpa = dr * r * (1.0 - r)
        dpx = di * ig * (1.0 - ig)
        dba_ref[...] += _colsum(dpa)
        dbx_ref[...] += _colsum(dpx)
        dpab = dpa.astype(BF16)
        dpxb = dpx.astype(BF16)
        nt = (((1,), (1,)), ((), ()))
        tn = (((0,), (0,)), ((), ()))
        dxc_g = []
        for h in range(heads):
            sl = slice(h * hd, (h + 1) * hd)
            dxc_g.append(lax.dot_general(dpab[:, sl], wa_ref[h], nt, preferred_element_type=F32)
                         + lax.dot_general(dpxb[:, sl], wx_ref[h], nt, preferred_element_type=F32))
            dwa_ref[h] += lax.dot_general(xcb[:, sl], dpab[:, sl], tn, preferred_element_type=F32)
            dwx_ref[h] += lax.dot_general(xcb[:, sl], dpxb[:, sl], tn, preferred_element_type=F32)
        dxc = dxc + jnp.concatenate(dxc_g, axis=1)

        dbc_ref[...] += _colsum(dxc)
        xs = {st: slab(xext, st) for st in range(SUBLANES - 3, 2 * SUBLANES)}
        for k in range(4):
            xsh = jnp.concatenate([xs[SUBLANES + j - (3 - k)] for j in range(SUBLANES)], axis=0)
            dwc_ref[k:k + 1, :] += _colsum(dxc * xsh)
        for j in range(SUBLANES):
            put_slab(dxext, j, dxc[j * G:(j + 1) * G])
        us = {st: slab(dxext, st) for st in range(SUBLANES + 3)}
        for j in range(SUBLANES):
            acc = us[j] * wc_ref[3:4, :]
            for k in (1, 2, 3):
                acc = acc + us[j + k] * wc_ref[3 - k:4 - k, :]
            put_slab(dnat, j, acc)
        dxl = jnp.concatenate([dnat[c] for c in range(nc)], axis=1)
        dp_ref[:, 0:lw] = dxl.astype(BF16)
        dbin_ref[:, 0:lw] += _colsum(dxl)

        gu, dgu_dx = _gelu_and_grad(p_ref[:, 2 * lw:2 * lw + sw].astype(F32))
        gv, dgv_dx = _gelu_and_grad(p_ref[:, 2 * lw + sw:cw].astype(F32))
        xhat, rstd = _ln_stats(gv)
        vn = (xhat * lg_ref[...] + lb_ref[...]).astype(BF16)
        dys = dys_ref[...].astype(F32)
        dmixed = dys * gu
        dmb = dmixed.astype(BF16)
        tpos = lax.broadcasted_iota(jnp.int32, (SGU_BLOCK, SGU_BLOCK), 0) // CHUNK
        spos = lax.broadcasted_iota(jnp.int32, (SGU_BLOCK, SGU_BLOCK), 1) // CHUNK
        causal = spos <= tpos
        mixed_rows, dvn_rows = [], []
        for blk in range(nblk):
            rs = slice(blk * SGU_BLOCK, (blk + 1) * SGU_BLOCK)
            mcols, dcols = [], []
            for g in range(groups):
                cs = slice(g * gw, (g + 1) * gw)
                wm = jnp.where(causal, wsp_ref[g], 0.0).astype(BF16)
                mcols.append(jnp.dot(wm, vn[rs, cs], preferred_element_type=F32) + bsp_ref[:, g:g + 1])
                dcols.append(lax.dot_general(wm, dmb[rs, cs], tn, preferred_element_type=F32))
                dw = lax.dot_general(dmb[rs, cs], vn[rs, cs], nt, preferred_element_type=F32)
                dwsp_ref[g] += jnp.where(causal, dw, 0.0)
                dbsp_ref[:, g:g + 1] += jnp.sum(dmixed[rs, cs], axis=1, keepdims=True)
            mixed_rows.append(jnp.concatenate(mcols, axis=1))
            dvn_rows.append(jnp.concatenate(dcols, axis=1))
        mixed_all = jnp.concatenate(mixed_rows, axis=0) if nblk > 1 else mixed_rows[0]
        dvn = jnp.concatenate(dvn_rows, axis=0) if nblk > 1 else dvn_rows[0]
        du = dys * mixed_all * dgu_dx
        dlg_ref[...] += _colsum(dvn * xhat)
        dlb_ref[...] += _colsum(dvn)
        dv = _ln_bwd(dvn, xhat, rstd, lg_ref[...]) * dgv_dx
        dp_ref[:, 2 * lw:2 * lw + sw] = du.astype(BF16)
        dp_ref[:, 2 * lw + sw:cw] = dv.astype(BF16)
        dbin_ref[:, 2 * lw:2 * lw + sw] += _colsum(du)
        dbin_ref[:, 2 * lw + sw:cw] += _colsum(dv)

    rev = lambda s: n_s - 1 - s
    tile = lambda w: pl.BlockSpec((None, tm, w), lambda b, s: (b, rev(s), 0))
    halo = lambda w: pl.BlockSpec((None, SUBLANES, w), lambda b, s: (b, jnp.maximum(rev(s) * per8 - 1, 0), 0))
    xhalo = pl.BlockSpec((None, halo_rows, lw), lambda b, s: (b, jnp.maximum(rev(s) * (tm // halo_rows) - 1, 0), 0))
    full = lambda shp: pl.BlockSpec(shp, lambda b, s: (0,) * len(shp))
    small = [w_conv, b_conv, w_rg_a, b_rg_a, w_rg_x, b_rg_x, lam, w_sp, b_sp_t, ln_v_g, ln_v_b]
    acc_shapes = [(1, cw), w_conv.shape, b_conv.shape, w_rg_a.shape, b_rg_a.shape, w_rg_x.shape, b_rg_x.shape,
                  lam.shape, w_sp.shape, b_sp_t.shape, ln_v_g.shape, ln_v_b.shape]
    res = pl.pallas_call(
        body, name="mix_bwd", grid=(Bl, n_s),
        in_specs=[tile(cw), xhalo, tile(lw), halo(lw), tile(lw), tile(sw), pl.BlockSpec(memory_space=pl.ANY)]
                 + [tile(lw)] * 5 + [full(w.shape) for w in small],
        out_specs=tuple([tile(cw)] + [full(shp) for shp in acc_shapes]),
        out_shape=tuple([jax.ShapeDtypeStruct((Bl, S, din), BF16)] + [jax.ShapeDtypeStruct(shp, F32) for shp in acc_shapes]),
        input_output_aliases={6: 0},
        scratch_shapes=[pltpu.VMEM((nc, tm + SUBLANES, LANES), F32), pltpu.VMEM((nc, tm + SUBLANES, LANES), F32),
                        pltpu.VMEM((nc, tm, LANES), F32), pltpu.VMEM((nc, tm + SUBLANES, LANES), F32),
                        pltpu.VMEM((SUBLANES, lw), F32), pltpu.VMEM((G, lw), F32), pltpu.VMEM((G, lw), F32),
                        pltpu.VMEM((G, lw), F32), pltpu.VMEM((G, lw), F32)],
        compiler_params=_cparams(2, big=True),
    )(proj, proj, hs, hs, dya, dys, dproj, *saved, *small)
    return res


def _final_dx(dxp, dh, x, sc1, *, ts):
    Bl, S, D = x.shape

    def body(dxp_ref, dh_ref, x_ref, sc_ref, dx_ref, dsc_ref, dsh_ref):
        @pl.when(pl.program_id(1) == 0)
        def _():
            dsc_ref[...] = jnp.zeros_like(dsc_ref)
            dsh_ref[...] = jnp.zeros_like(dsh_ref)

        dh = dh_ref[...]
        dx_ref[...] = dxp_ref[...] + dh * (1.0 + sc_ref[...])
        dsc_ref[...] += _colsum(dh * x_ref[...])
        dsh_ref[...] += _colsum(dh)

    return pl.pallas_call(
        body, name="final_dx", grid=(Bl, S // ts),
        in_specs=[_tok_spec(ts, D), _tok_spec(ts, D), _tok_spec(ts, D), _brow_spec(D)],
        out_specs=(_tok_spec(ts, D), _brow_spec(D), _brow_spec(D)),
        out_shape=(jax.ShapeDtypeStruct((Bl, S, D), F32), jax.ShapeDtypeStruct((Bl, 1, D), F32),
                   jax.ShapeDtypeStruct((Bl, 1, D), F32)),
        compiler_params=_cparams(2),
    )(dxp, dh, x, sc1)


def _ada_fwd(c_all, w_ada):
    R, D = c_all.shape
    nb = w_ada.shape[1]

    def body(c_ref, w_ref, act_ref, o_ref):
        cv = c_ref[...]
        act = (cv * _sigmoid(cv)).astype(BF16)
        act_ref[...] = act
        o_ref[...] = jnp.dot(act, w_ref[...].astype(BF16), preferred_element_type=F32)

    return pl.pallas_call(
        body, name="ada_fwd",
        out_shape=(jax.ShapeDtypeStruct((R, D), BF16), jax.ShapeDtypeStruct((R, nb), F32)),
        compiler_params=pltpu.CompilerParams(vmem_limit_bytes=VMEM_LIMIT),
    )(c_all, w_ada)


def _ada_bwd(c_act, dmod_cols):
    R, D = c_act.shape
    nb = dmod_cols.shape[1]

    def body(act_ref, d_ref, o_ref, b_ref):
        o_ref[...] = lax.dot_general(act_ref[...], d_ref[...].astype(BF16), (((0,), (0,)), ((), ())),
                                     preferred_element_type=F32)
        b_ref[...] = _colsum(d_ref[...])

    return pl.pallas_call(
        body, name="ada_bwd", out_shape=(jax.ShapeDtypeStruct((D, nb), F32), jax.ShapeDtypeStruct((1, nb), F32)),
        compiler_params=pltpu.CompilerParams(vmem_limit_bytes=VMEM_LIMIT),
    )(c_act, dmod_cols)


def _adamw(w, g_slots, m, v, *, tr, name, own=None):
    R, C = w.shape
    n_slot = g_slots.shape[0]
    tr = min(tr, R)
    assert R % tr == 0, (name, R, tr)
    c1 = 1.0 / (1.0 - ADAM_B1 ** ADAM_STEP)
    c2 = 1.0 / (1.0 - ADAM_B2 ** ADAM_STEP)
    n_own = 0 if own is None else 1

    def body(me_ref, w_ref, g_ref, *refs):
        m_ref, v_ref, go_ref, d_ref, mo_ref, vo_ref = refs[n_own:]
        slot = lambda d: (jnp.where(me_ref[0] == d, refs[0][...], g_ref[d]) if n_own else g_ref[d]).astype(F32)
        g = slot(0)
        for d in range(1, n_slot):
            g = g + slot(d)
        mn = ADAM_B1 * m_ref[...] + (1.0 - ADAM_B1) * g
        vn = ADAM_B2 * v_ref[...] + (1.0 - ADAM_B2) * (g * g)
        go_ref[...] = g
        mo_ref[...] = mn
        vo_ref[...] = vn
        d_ref[...] = -ADAM_LR * ((mn * c1) / (jnp.sqrt(vn * c2) + ADAM_EPS) + ADAM_WD * w_ref[...])

    me = 4 * lax.axis_index("x") + 2 * lax.axis_index("y") + lax.axis_index("c")
    blk = pl.BlockSpec((tr, C), lambda i, me_ref: (i, 0))
    own_specs = [pl.BlockSpec((None, tr, C), lambda i, me_ref: (me_ref[0], i, 0))] * n_own
    return pl.pallas_call(
        body, name=name, out_shape=tuple(jax.ShapeDtypeStruct((R, C), F32) for _ in range(4)),
        grid_spec=pltpu.PrefetchScalarGridSpec(
            num_scalar_prefetch=1, grid=(R // tr,),
            in_specs=[blk, pl.BlockSpec((n_slot, tr, C), lambda i, me_ref: (0, i, 0))] + own_specs + [blk, blk],
            out_specs=(blk, blk, blk, blk)),
        compiler_params=_cparams(1, big=True),
    )(jnp.reshape(me, (1,)).astype(jnp.int32), w, g_slots, *([own] if n_own else []), m, v)


def _adamw_many(ws, g_slots, g_owns, ms, vs, *, name):
    n = len(ws)
    c1 = 1.0 / (1.0 - ADAM_B1 ** ADAM_STEP)
    c2 = 1.0 / (1.0 - ADAM_B2 ** ADAM_STEP)

    def body(*refs):
        w_refs, g_refs, o_refs = refs[:n], refs[n:2 * n], refs[2 * n:3 * n]
        m_refs, v_refs = refs[3 * n:4 * n], refs[4 * n:5 * n]
        outs = refs[5 * n:]
        me = 4 * lax.axis_index("x") + 2 * lax.axis_index("y") + lax.axis_index("c")
        for i in range(n):
            own = o_refs[i][...]
            g = jnp.where(me == 0, own, g_refs[i][0])
            for d in range(1, N_DEV):
                g = g + jnp.where(me == d, own, g_refs[i][d])
            mn = ADAM_B1 * m_refs[i][...] + (1.0 - ADAM_B1) * g
            vn = ADAM_B2 * v_refs[i][...] + (1.0 - ADAM_B2) * (g * g)
            outs[i][...] = g
            outs[n + i][...] = -ADAM_LR * ((mn * c1) / (jnp.sqrt(vn * c2) + ADAM_EPS) + ADAM_WD * w_refs[i][...])
            outs[2 * n + i][...] = mn
            outs[3 * n + i][...] = vn

    res = pl.pallas_call(
        body, name=name, out_shape=tuple(jax.ShapeDtypeStruct(w.shape, F32) for _ in range(4) for w in ws),
        compiler_params=pltpu.CompilerParams(vmem_limit_bytes=VMEM_LIMIT),
    )(*ws, *g_slots, *g_owns, *ms, *vs)
    return res[:n], res[n:2 * n], res[2 * n:3 * n], res[3 * n:]


SMALL_NAMES = ("b_ada", "b_in", "b_conv", "w_rg_a", "b_rg_a", "w_rg_x", "b_rg_x", "lru_lambda", "w_sp", "b_sp",
               "ln_v_g", "ln_v_b", "ln1_g", "ln1_b", "ln2_g", "ln2_b")
BIG_NAMES = ("w_ada", "w_in", "w_conv", "w_o_lru", "w_o_sgu", "w_out", "w_up", "w_down")
WEIGHT_ORDER = ("w_ada", "b_ada", "w_in", "b_in", "w_conv", "b_conv", "w_rg_a", "b_rg_a", "w_rg_x", "b_rg_x",
                "lru_lambda", "w_sp", "b_sp", "ln_v_g", "ln_v_b", "w_o_lru", "w_o_sgu", "w_out", "ln1_g", "ln1_b",
                "w_up", "w_down", "ln2_g", "ln2_b")


def _pack_small(d):
    flat = jnp.concatenate([d[n].reshape(-1) for n in SMALL_NAMES])
    rows = -(-flat.shape[0] // LANES)
    rows = -(-rows // (N_DEV * SUBLANES)) * (N_DEV * SUBLANES)
    flat = jnp.pad(flat, (0, rows * LANES - flat.shape[0]))
    return flat.reshape(rows, LANES)


def _unpack_small(packed, like):
    flat = packed.reshape(-1)
    out, off = {}, 0
    for n in SMALL_NAMES:
        sz = like[n].size
        out[n] = flat[off:off + sz].reshape(like[n].shape)
        off += sz
    return out


def _blocked_cols(w2d):
    K, N = w2d.shape
    return jnp.transpose(w2d.reshape(K, N_DEV, N // N_DEV), (1, 0, 2))


def _unblock_cols(wb):
    n, K, nb = wb.shape
    return jnp.transpose(wb, (1, 0, 2)).reshape(K, n * nb)


def kernel(x, c, w_ada, b_ada, w_in, b_in, w_conv, b_conv, w_rg_a, b_rg_a, w_rg_x, b_rg_x, lru_lambda, w_sp, b_sp, ln_v_g, ln_v_b, w_o_lru, w_o_sgu, w_out, ln1_g, ln1_b, w_up, w_down, ln2_g, ln2_b, loss_target, m_w_ada, m_b_ada, m_w_in, m_b_in, m_w_conv, m_b_conv, m_w_rg_a, m_b_rg_a, m_w_rg_x, m_b_rg_x, m_lru_lambda, m_w_sp, m_b_sp, m_ln_v_g, m_ln_v_b, m_w_o_lru, m_w_o_sgu, m_w_out, m_ln1_g, m_ln1_b, m_w_up, m_w_down, m_ln2_g, m_ln2_b, v_w_ada, v_b_ada, v_w_in, v_b_in, v_w_conv, v_b_conv, v_w_rg_a, v_b_rg_a, v_w_rg_x, v_b_rg_x, v_lru_lambda, v_w_sp, v_b_sp, v_ln_v_g, v_ln_v_b, v_w_o_lru, v_w_o_sgu, v_w_out, v_ln1_g, v_ln1_b, v_w_up, v_w_down, v_ln2_g, v_ln2_b):
    W = dict(w_ada=w_ada, b_ada=b_ada, w_in=w_in, b_in=b_in, w_conv=w_conv, b_conv=b_conv, w_rg_a=w_rg_a,
             b_rg_a=b_rg_a, w_rg_x=w_rg_x, b_rg_x=b_rg_x, lru_lambda=lru_lambda, w_sp=w_sp, b_sp=b_sp,
             ln_v_g=ln_v_g, ln_v_b=ln_v_b, w_o_lru=w_o_lru, w_o_sgu=w_o_sgu, w_out=w_out, ln1_g=ln1_g, ln1_b=ln1_b,
             w_up=w_up, w_down=w_down, ln2_g=ln2_g, ln2_b=ln2_b)
    Mo = dict(w_ada=m_w_ada, b_ada=m_b_ada, w_in=m_w_in, b_in=m_b_in, w_conv=m_w_conv, b_conv=m_b_conv,
              w_rg_a=m_w_rg_a, b_rg_a=m_b_rg_a, w_rg_x=m_w_rg_x, b_rg_x=m_b_rg_x, lru_lambda=m_lru_lambda,
              w_sp=m_w_sp, b_sp=m_b_sp, ln_v_g=m_ln_v_g, ln_v_b=m_ln_v_b, w_o_lru=m_w_o_lru, w_o_sgu=m_w_o_sgu,
              w_out=m_w_out, ln1_g=m_ln1_g, ln1_b=m_ln1_b, w_up=m_w_up, w_down=m_w_down, ln2_g=m_ln2_g,
              ln2_b=m_ln2_b)
    Vo = dict(w_ada=v_w_ada, b_ada=v_b_ada, w_in=v_w_in, b_in=v_b_in, w_conv=v_w_conv, b_conv=v_b_conv,
              w_rg_a=v_w_rg_a, b_rg_a=v_b_rg_a, w_rg_x=v_w_rg_x, b_rg_x=v_b_rg_x, lru_lambda=v_lru_lambda,
              w_sp=v_w_sp, b_sp=v_b_sp, ln_v_g=v_ln_v_g, ln_v_b=v_ln_v_b, w_o_lru=v_w_o_lru, w_o_sgu=v_w_o_sgu,
              w_out=v_w_out, ln1_g=v_ln1_g, ln1_b=v_ln1_b, w_up=v_w_up, w_down=v_w_down, ln2_g=v_ln2_g,
              ln2_b=v_ln2_b)

    Bl, S, D = x.shape
    T = Bl * S
    lw = b_conv.shape[-1]
    sw = ln_v_g.shape[-1]
    din = b_in.shape[-1]
    dff = w_up.shape[-1] * N_DEV
    ts = min(2048, S)
    tmix = min(256, S)
    trow = min(512, S)

    c_pad = jnp.pad(c, ((0, SUBLANES - Bl), (0, 0)))
    c_g, wconv_g = _exchange([c_pad, w_conv[0]], True, "xchg_c")
    wconv_full = _unblock_cols(wconv_g)
    c_act, modcols = _ada_fwd(c_g.reshape(N_DEV * SUBLANES, D), w_ada[0])
    (mod_slots,) = _exchange([modcols.reshape(N_DEV, SUBLANES, -1)], False, "xchg_mod")

    nbw = din // N_DEV // WIN_PARTS
    wnames = tuple("win%d" % q for q in range(WIN_PARTS)) + ("wol", "wos", "wout", "wup", "wdown")
    shards = [w_in[0][:, q * nbw:(q + 1) * nbw].astype(BF16) for q in range(WIN_PARTS)] + [
        w_o_lru[0].astype(BF16), w_o_sgu[0].astype(BF16), w_out[0].astype(BF16), w_up[0].astype(BF16),
        w_down[0].astype(BF16)]
    col_sharded = [True] * WIN_PARTS + [False, True, False, True, False]
    g_send, g_recv, g_src, g_land, g_tok = _xstart(shards, True, mod_slots, "gather_start", cols=col_sharded)
    gidx = {n: i for i, n in enumerate(wnames)}

    def gathered(n, after):
        i = gidx[n]
        return _xwait(g_src[i], g_land[i], g_send[i], g_recv[i], after, True, "gather_wait_" + n, col=col_sharded[i])

    mod = _unblock_cols(mod_slots)[:Bl] + (b_ada + g_tok[0, 0])
    sh1, sc1, gt1, sh2, sc2, gt2 = [mod[:, i * D:(i + 1) * D].reshape(Bl, 1, D) for i in range(6)]

    wa_b, wx_b = w_rg_a[0].astype(BF16), w_rg_x[0].astype(BF16)
    b_sp_t = jnp.transpose(b_sp[0])
    small_mix = (wconv_full, b_conv, wa_b, b_rg_a, wx_b, b_rg_x, lru_lambda, w_sp[0], b_sp_t, ln_v_g, ln_v_b)

    h = _modulate(x, sc1, sh1, ts)
    proj, win_parts = None, []
    for q in range(WIN_PARTS):
        wq = gathered("win%d" % q, h if q == 0 else proj)
        win_parts.append(wq)
        proj = _mm(h.reshape(T, D), wq, mode="nn", tm=4096, tn=nbw, tk=D, outs=[BF16], extras=[(b_in, "row")],
                   epilogue=lambda acc, ex: (acc + ex[0],), scatter=(WIN_PARTS, q, din), into=proj,
                   name="mm_proj%d" % q)
    proj3 = proj.reshape(Bl, S, din)
    hs, ya_pre, ysgu, *lru_saved = _mix_fwd(proj3, *small_mix, tm=tmix, lw=lw, sw=sw)
    Wol = gathered("wol", ya_pre).reshape(lw, D)
    Wos = gathered("wos", ysgu)
    y_a = _mm(ya_pre.reshape(T, lw), Wol, mode="nn", tm=2048, tn=D, tk=lw, outs=[BF16], name="mm_ya")
    x2d, tgt2d = x.reshape(T, D), loss_target.reshape(T, D)
    gate_cb = (din - 2 * D) // D

    def ep_merge(y_b, v):
        ya, ga, gb = [t.astype(F32) for t in v]
        yb = y_b.astype(BF16).astype(F32)
        return [yb, _sigmoid(ga) * ya + _sigmoid(gb) * yb]

    y_b, merged = _mm_rows(ysgu.reshape(T, sw), Wos, mode="nn", tm=trow, seq=S,
                           ins=[("tile", y_a), ("tilecol", proj, D, gate_cb), ("tilecol", proj, D, gate_cb + 1)],
                           outs=[("tile", BF16, D), ("tile", BF16, D)], epilogue=ep_merge, name="mm_yb_merge")
    Wout = gathered("wout", merged).reshape(D, D)

    def ep_ln1(mix_acc, v):
        x_, gt, g, b, sc, sh = v
        mixr = mix_acc.astype(BF16).astype(F32)
        xhat, _ = _ln_stats(ALPHA * x_ + (1.0 + gt) * mixr)
        x1_ = xhat * g + b
        return [mixr, x1_, x1_ * (1.0 + sc) + sh]

    mix, x1, h2 = _mm_rows(merged, Wout, mode="nn", tm=trow, seq=S,
                           ins=[("tile", x2d), ("brow", gt1), ("row", ln1_g), ("row", ln1_b), ("brow", sc2),
                                ("brow", sh2)],
                           outs=[("tile", BF16, D), ("tile", F32, D), ("tile", BF16, D)], epilogue=ep_ln1,
                           name="mm_mix_ln1")
    Wup = gathered("wup", h2)
    act = _mm(h2, Wup, mode="nn", tm=2048, tn=1024, tk=D, outs=[BF16],
              epilogue=lambda acc, ex: (jnp.square(jnp.maximum(acc, 0.0)),), name="mm_up")
    Wdown = gathered("wdown", act).reshape(dff, D)

    def ep_ln2(f_acc, v):
        x1_, t_, gt, g, b = v
        xhat, rstd = _ln_stats(ALPHA * x1_ + (1.0 + gt) * f_acc)
        err = xhat * g + b - t_
        loss_t = 0.5 * jnp.sum(jnp.mean(err * err, axis=-1, keepdims=True))
        dy = err * (1.0 / D)
        dz = _ln_bwd(dy, xhat, rstd, g)
        return [dz * (1.0 + gt), ALPHA * dz, _colsum(dz * f_acc), _colsum(dy * xhat), _colsum(dy), loss_t]

    df2, dx1p, dgt2, dg2, db2, loss_part = _mm_rows(
        act, Wdown, mode="nn", tm=trow, seq=S,
        ins=[("tile", x1), ("tile", tgt2d), ("brow", gt2), ("row", ln2_g), ("row", ln2_b)],
        outs=[("tile", BF16, D), ("tile", F32, D), ("acc_brow", D), ("acc_row", D), ("acc_row", D), ("acc_scalar",)],
        epilogue=ep_ln2, name="mm_down_ln2")
    loss = lax.psum(loss_part[0, 0], ("x", "y", "c"))

    def send_grads(parts, name):
        snd, rcv, src, land, tok = _xstart(parts, False, None, name + "_start")
        return [(src[i], land[i], snd[i], rcv[i]) for i in range(len(parts))], tok

    dup = _mm(df2, Wdown, mode="nt", tm=2048, tn=1024, tk=D, outs=[BF16], extras=[(act, "tile")],
              epilogue=lambda acc, ex: (acc * (2.0 * jnp.sqrt(ex[0].astype(F32))),), name="mm_dup")
    g_wdown = _mm(act, df2, mode="tn", tm=1024, tn=D, tk=2048, outs=[BF16], name="mm_gwdown")
    (x_wdown,), tok = send_grads([g_wdown.reshape(N_DEV, dff // N_DEV, D)], "gx_wdown")
    def ep_ln1_bwd(dh2, v):
        dx1p_, x1_, x_, mix_, sc, gt, g = v
        mixv = mix_.astype(F32)
        dx1 = dx1p_ + dh2 * (1.0 + sc)
        xhat, rstd = _ln_stats(ALPHA * x_ + (1.0 + gt) * mixv)
        dz = _ln_bwd(dx1, xhat, rstd, g)
        return [ALPHA * dz, dz * (1.0 + gt), _colsum(dh2 * x1_), _colsum(dh2), _colsum(dz * mixv),
                _colsum(dx1 * xhat), _colsum(dx1)]

    dxp, dmix, dsc2, dsh2, dgt1, dg1, db1 = _mm_rows(
        dup, Wup, mode="nt", tm=trow, seq=S, tok=tok,
        ins=[("tile", dx1p), ("tile", x1), ("tile", x2d), ("tile", mix), ("brow", sc2), ("brow", gt1), ("row", ln1_g)],
        outs=[("tile", F32, D), ("tile", BF16, D), ("acc_brow", D), ("acc_brow", D), ("acc_brow", D), ("acc_row", D),
              ("acc_row", D)],
        epilogue=ep_ln1_bwd, name="mm_dh2_ln1b")
    g_wup = _mm(h2, dup, mode="tn", tm=D, tn=1024, tk=2048, outs=[BF16], nb=dff // N_DEV, name="mm_gwup")
    (x_wup,), tok = send_grads([g_wup], "gx_wup")

    def ep_merge_bwd(dm, v):
        ya, yb, ga, gb = [t.astype(F32) for t in v]
        sa, sb = _sigmoid(ga), _sigmoid(gb)
        dg = jnp.concatenate([dm * ya * sa * (1.0 - sa), dm * yb * sb * (1.0 - sb)], axis=1)
        return [dm * sa, dm * sb, dg, _colsum(dg)]

    dy_a, dy_b, dproj, dbin_hi = _mm_rows(
        dmix, Wout, mode="nt", tm=trow, seq=S, tok=tok,
        ins=[("tile", y_a), ("tile", y_b), ("tilecol", proj, D, gate_cb), ("tilecol", proj, D, gate_cb + 1)],
        outs=[("tile", BF16, D), ("tile", BF16, D), ("tilecol", BF16, 2 * D, gate_cb // 2, din), ("acc_row", 2 * D)],
        epilogue=ep_merge_bwd, name="mm_dmerged_mb")
    g_wout = _mm(merged, dmix, mode="tn", tm=D, tn=D, tk=2048, outs=[BF16], name="mm_gwout")
    (x_wout,), tok = send_grads([g_wout.reshape(N_DEV, D // N_DEV, D)], "gx_wout")
    dya_pre = _mm(dy_a, Wol, mode="nt", tm=2048, tn=lw, tk=D, outs=[BF16], tok=tok, name="mm_dya")
    dysgu = _mm(dy_b, Wos, mode="nt", tm=2048, tn=sw, tk=D, outs=[BF16], name="mm_dys")
    g_wol = _mm(ya_pre.reshape(T, lw), dy_a, mode="tn", tm=lw, tn=D, tk=2048, outs=[BF16], name="mm_gwol")
    g_wos = _mm(ysgu.reshape(T, sw), dy_b, mode="tn", tm=sw, tn=D, tk=2048, outs=[BF16], nb=D // N_DEV,
                name="mm_gwos")
    (x_wol, x_wos), tok = send_grads([g_wol.reshape(N_DEV, lw // N_DEV, D), g_wos], "gx_wo")
    small_mix_b = (wconv_full, b_conv + tok[0, 0]) + small_mix[2:]
    (dproj, dbin_lo, g_wconv, g_bconv, g_wa, g_ba, g_wx, g_bx, g_lam, g_wsp, g_bsp_t, g_lvg, g_lvb) = _mix_bwd(
        proj3, hs, dya_pre.reshape(Bl, S, lw), dysgu.reshape(Bl, S, sw), dproj.reshape(Bl, S, din), lru_saved,
        *small_mix_b, tm=tmix, lw=lw, sw=sw)
    dproj2 = dproj.reshape(T, din)
    small_names = [n for n in SMALL_NAMES if n != "b_ada"]
    small_g = dict(b_in=jnp.concatenate([dbin_lo, dbin_hi], axis=-1), b_conv=g_bconv, w_rg_a=g_wa[None], b_rg_a=g_ba,
                   w_rg_x=g_wx[None], b_rg_x=g_bx, lru_lambda=g_lam, w_sp=g_wsp[None],
                   b_sp=jnp.transpose(g_bsp_t)[None], ln_v_g=g_lvg, ln_v_b=g_lvb, ln1_g=dg1, ln1_b=db1, ln2_g=dg2,
                   ln2_b=db2)
    gs_snd, gs_rcv, gs_src, gs_land, tok_s = _xstart([small_g[n] for n in small_names], True, None, "gsmall_start",
                                                      fill_own=False)
    g_win = _mm(h.reshape(T, D), dproj2, mode="tn", tm=D, tn=din // 4, tk=2048, outs=[BF16], nb=din // N_DEV,
                tok=tok_s, name="mm_gwin")
    (x_win,), tok = send_grads([g_win], "gx_win")

    def ep_final(dh, v):
        dxp_, x_, sc = v
        return [dxp_ + dh * (1.0 + sc), _colsum(dh * x_), _colsum(dh)]

    grad_x, dsc1, dsh1 = _mm_rows(dproj2, win_parts, mode="nt", tm=trow, seq=S, tok=tok,
                                  ins=[("tile", dxp), ("tile", x2d), ("brow", sc1)],
                                  outs=[("tile", F32, D), ("acc_brow", D), ("acc_brow", D)], epilogue=ep_final,
                                  name="mm_dh_final")
    grad_x = grad_x.reshape(Bl, S, D)

    out_g, out_d, out_m, out_v = {}, {}, {}, {}

    def adam(name, g_slots, tr, own=None):
        shp = W[name].shape
        w2, m2, v2 = [t.reshape(g_slots.shape[1:]) for t in (W[name], Mo[name], Vo[name])]
        g, d, mn, vn = _adamw(w2, g_slots, m2, v2, tr=tr, name="adam_" + name, own=own)
        out_g[name], out_d[name], out_m[name], out_v[name] = [t.reshape(shp) for t in (g, d, mn, vn)]

    def adam_exchanged(name, handle, tr, after):
        own, slots = _xwait(*handle, after, False, "gx_%s_wait" % name, place=False)
        adam(name, slots, tr, own=own)

    adam_exchanged("w_down", x_wdown, 256, dsh1)
    adam_exchanged("w_up", x_wup, 256, dsh1)
    adam_exchanged("w_out", x_wout, 128, dsh1)
    adam_exchanged("w_o_lru", x_wol, 160, dsh1)
    adam_exchanged("w_o_sgu", x_wos, 256, dsh1)
    gs_own, gs_slots = _xwait_many(gs_src, gs_land, gs_snd, gs_rcv, dsh1, "gsmall_wait")
    res_small = _adamw_many([W[n] for n in small_names], gs_slots, gs_own, [Mo[n] for n in small_names],
                            [Vo[n] for n in small_names], name="adam_small")
    for dst, vals in zip((out_g, out_d, out_m, out_v), res_small):
        dst.update(dict(zip(small_names, vals)))

    dmod = jnp.concatenate([dsh1, dsc1, dgt1, dsh2, dsc2, dgt2], axis=-1).reshape(Bl, 6 * D)
    dmod_b = _blocked_cols(jnp.pad(dmod, ((0, SUBLANES - Bl), (0, 0))))
    dmod_s, gwconv_s = _exchange([dmod_b, _blocked_cols(g_wconv)], False, "xchg_dmod", after=out_g["ln2_b"])
    g_wada, g_bada_mine = _ada_bwd(c_act, dmod_s.reshape(N_DEV * SUBLANES, -1))
    (g_bada_all,) = _exchange([g_bada_mine], True, "xchg_bada")
    adam("w_ada", g_wada[None], 256)
    adam("b_ada", g_bada_all.reshape(1, 1, 6 * D), 1)
    adam("w_conv", gwconv_s, 8)
    adam_exchanged("w_in", x_win, 256, g_bada_all)

    return (loss, grad_x, *[out_g[n] for n in WEIGHT_ORDER], *[out_d[n] for n in WEIGHT_ORDER],
            *[out_m[n] for n in WEIGHT_ORDER], *[out_v[n] for n in WEIGHT_ORDER])
```

```python
import math

import jax
import jax.numpy as jnp
from jax import lax
from jax.experimental import pallas as pl
from jax.experimental.pallas import tpu as pltpu

N_DEV = 8
LN_EPS = 1e-5
LRU_C = 8.0
CHUNK = 64
SGU_BLOCK = 128
ALPHA = 2.0 ** 0.25
ADAM_LR = 0.001
ADAM_B1 = 0.9
ADAM_B2 = 0.999
ADAM_EPS = 1e-08
ADAM_WD = 0.01
ADAM_STEP = 10
GELU_K0 = math.sqrt(2.0 / math.pi)
GELU_K1 = 0.044715

SUBLANES = 8
LANES = 128
VMEM_LIMIT = 56 * 1024 * 1024
WIN_PARTS = 3

F32 = jnp.float32
BF16 = jnp.bfloat16
MESH = pl.DeviceIdType.MESH


def _cparams(n_axes, big=False):
    return pltpu.CompilerParams(dimension_semantics=("arbitrary",) * n_axes,
                                vmem_limit_bytes=VMEM_LIMIT if big else None)


def _sigmoid(x):
    return 0.5 * jnp.tanh(0.5 * x) + 0.5


def _gelu(x):
    t = jnp.tanh(x * (GELU_K0 + (GELU_K0 * GELU_K1) * (x * x)))
    hx = 0.5 * x
    return hx + hx * t


def _gelu_and_grad(x):
    x2 = x * x
    t = jnp.tanh(x * (GELU_K0 + (GELU_K0 * GELU_K1) * x2))
    hx = 0.5 * x
    g = hx + hx * t
    dg = (0.5 + 0.5 * t) + (hx * (1.0 - t * t)) * (GELU_K0 + (3.0 * GELU_K0 * GELU_K1) * x2)
    return g, dg


def _log1p_pos(e):
    p = e * (1.0 - e * (1.0 / 2.0) + e * e * (1.0 / 3.0) - e * e * e * (1.0 / 4.0))
    return jnp.where(e < 1e-2, p, jnp.log(1.0 + e))


def _ln_stats(z):
    mu = jnp.mean(z, axis=-1, keepdims=True)
    zc = z - mu
    var = jnp.mean(zc * zc, axis=-1, keepdims=True)
    rstd = lax.rsqrt(var + LN_EPS)
    return zc * rstd, rstd


def _ln_bwd(dy, xhat, rstd, g):
    dxh = dy * g
    m1 = jnp.mean(dxh, axis=-1, keepdims=True)
    m2 = jnp.mean(dxh * xhat, axis=-1, keepdims=True)
    return rstd * (dxh - m1 - xhat * m2)


def _colsum(v):
    return jnp.sum(v, axis=0, keepdims=True)


def _first_step():
    return jnp.logical_and(pl.program_id(0) == 0, pl.program_id(1) == 0)


def _exchange(arrs, gather, name, after=None):
    n = len(arrs)
    n_peer = N_DEV - 1
    n_after = 0 if after is None else 1

    def body(*refs):
        ins, outs = refs[:n], refs[n + n_after:2 * n + n_after]
        send_sems, recv_sems, loc_sems = refs[2 * n + n_after:]
        x, y, c = lax.axis_index("x"), lax.axis_index("y"), lax.axis_index("c")
        me = 4 * x + 2 * y + c
        started = []
        for a in range(n):
            src_me = ins[a] if gather else ins[a].at[me]
            lc = pltpu.make_async_copy(src_me, outs[a].at[me], loc_sems.at[a])
            lc.start()
            started.append((lc, None))
        for p in range(1, N_DEV):
            px, py, pc = x ^ ((p >> 2) & 1), y ^ ((p >> 1) & 1), c ^ (p & 1)
            peer = 4 * px + 2 * py + pc
            for a in range(n):
                k = a * n_peer + (p - 1)
                src = ins[a] if gather else ins[a].at[peer]
                cp = pltpu.make_async_remote_copy(src_ref=src, dst_ref=outs[a].at[me],
                                                  send_sem=send_sems.at[k], recv_sem=recv_sems.at[k],
                                                  device_id=(px, py, pc), device_id_type=MESH)
                cp.start()
                rc = pltpu.make_async_remote_copy(src_ref=src, dst_ref=outs[a].at[peer],
                                                  send_sem=send_sems.at[k], recv_sem=recv_sems.at[k],
                                                  device_id=(px, py, pc), device_id_type=MESH)
                started.append((cp, rc))
        for cp, rc in started:
            if rc is None:
                cp.wait()
            else:
                cp.wait_send()
                rc.wait_recv()

    hbm = pl.BlockSpec(memory_space=pltpu.HBM)
    out_shape = tuple(
        jax.ShapeDtypeStruct(((N_DEV,) + a.shape) if gather else a.shape, a.dtype) for a in arrs)
    return pl.pallas_call(
        body, name=name, out_shape=out_shape,
        in_specs=[hbm] * n + [pl.BlockSpec(memory_space=pl.ANY)] * n_after, out_specs=tuple([hbm] * n),
        scratch_shapes=[pltpu.SemaphoreType.DMA((n * n_peer,)), pltpu.SemaphoreType.DMA((n * n_peer,)),
                        pltpu.SemaphoreType.DMA((n,))],
        compiler_params=pltpu.CompilerParams(has_side_effects=True),
    )(*arrs, *([after] if n_after else []))


_HBM = pl.BlockSpec(memory_space=pltpu.HBM)
_SEM = pl.BlockSpec(memory_space=pltpu.SEMAPHORE)
_EFFECT = pltpu.SideEffectType.DATAFLOW_SIDE_EFFECTING


def _peer_of(p):
    x, y, c = lax.axis_index("x"), lax.axis_index("y"), lax.axis_index("c")
    px, py, pc = x ^ ((p >> 2) & 1), y ^ ((p >> 1) & 1), c ^ (p & 1)
    return (px, py, pc), 4 * px + 2 * py + pc


def _slot(land_ref, idx, width):
    if width is None:
        return land_ref.at[idx]
    return land_ref.at[:, pl.ds(pl.multiple_of(idx * width, LANES), width)]


def _xstart(srcs, gather, after, name, cols=None, fill_own=False):
    n = len(srcs)
    cols = cols or [False] * n
    widths = [t.shape[1] if cols[a] else None for a, t in enumerate(srcs)]
    me_out = 4 * lax.axis_index("x") + 2 * lax.axis_index("y") + lax.axis_index("c")
    lands = []
    for a, t in enumerate(srcs):
        if cols[a]:
            zone, own, at = lax.empty((t.shape[0], N_DEV * t.shape[1]), t.dtype), t, (0, me_out * t.shape[1])
        elif gather:
            zone, own, at = lax.empty((N_DEV,) + t.shape, t.dtype), t[None], (me_out,) + (0,) * t.ndim
        else:
            zone, own = lax.empty(t.shape, t.dtype), lax.dynamic_index_in_dim(t, me_out, 0, keepdims=True)
            at = (me_out,) + (0,) * (t.ndim - 1)
        lands.append(lax.dynamic_update_slice(zone, own, at) if fill_own else zone)
    n_after = 0 if after is None else 1

    def body(*refs):
        src_refs, land_refs = refs[:n], refs[n:2 * n]
        refs = refs[n_after:]
        send_sems, recv_sems = refs[2 * n:3 * n], refs[3 * n:4 * n]
        token = refs[6 * n]
        me = 4 * lax.axis_index("x") + 2 * lax.axis_index("y") + lax.axis_index("c")
        for a in range(n):
            for p in range(1, N_DEV):
                dev, peer = _peer_of(p)
                pltpu.make_async_remote_copy(
                    src_ref=src_refs[a] if gather else src_refs[a].at[peer], dst_ref=_slot(land_refs[a], me, widths[a]),
                    send_sem=send_sems[a].at[p - 1], recv_sem=recv_sems[a].at[p - 1],
                    device_id=dev, device_id_type=MESH).start()
        token[...] = jnp.zeros_like(token)

    sems = tuple(pltpu.SemaphoreType.DMA((N_DEV - 1,)) for _ in range(2 * n))
    thru = tuple(pltpu.HBM(t.shape, t.dtype) for t in list(srcs) + list(lands))
    res = pl.pallas_call(
        body, name=name,
        out_shape=sems + thru + (jax.ShapeDtypeStruct((SUBLANES, LANES), F32),),
        in_specs=[_HBM] * (2 * n) + [pl.BlockSpec(memory_space=pl.ANY)] * n_after,
        out_specs=tuple([_SEM] * (2 * n) + [_HBM] * (2 * n) + [pl.BlockSpec(memory_space=pltpu.VMEM)]),
        input_output_aliases={i: 2 * n + i for i in range(2 * n)},
        compiler_params=pltpu.CompilerParams(has_side_effects=_EFFECT),
    )(*[pltpu.with_memory_space_constraint(t, pltpu.HBM) for t in list(srcs) + list(lands)],
      *([after] if n_after else []))
    return res[:n], res[n:2 * n], res[2 * n:3 * n], res[3 * n:4 * n], res[4 * n]


def _xwait(src, land, send_sem, recv_sem, after, gather, name, col=False, place=True):
    width = src.shape[1] if col else None

    def body(src_ref, land_ref, send_ref, recv_ref, after_ref, src_dead, land_out):
        del after_ref, src_dead, land_out
        for p in range(1, N_DEV):
            dev, peer = _peer_of(p)
            cp = pltpu.make_async_remote_copy(
                src_ref=src_ref if gather else src_ref.at[peer], dst_ref=_slot(land_ref, peer, width),
                send_sem=send_ref.at[p - 1], recv_sem=recv_ref.at[p - 1], device_id=dev, device_id_type=MESH)
            cp.wait_send()
            cp.wait_recv()

    src_done, landed = pl.pallas_call(
        body, name=name, out_shape=(pltpu.HBM(src.shape, src.dtype), pltpu.HBM(land.shape, land.dtype)),
        in_specs=[_HBM, _HBM, _SEM, _SEM, pl.BlockSpec(memory_space=pl.ANY)], out_specs=(_HBM, _HBM),
        input_output_aliases={0: 0, 1: 1},
        compiler_params=pltpu.CompilerParams(has_side_effects=_EFFECT),
    )(src, land, send_sem, recv_sem, after)
    if not place:
        return src_done, landed
    me = 4 * lax.axis_index("x") + 2 * lax.axis_index("y") + lax.axis_index("c")
    return _place_own(landed, src_done, me, col, gather, name + "_own")


def _place_own(zone, src, me, col, gather, name):
    if col:
        R, C = src.shape
        src_spec = lambda tr: pl.BlockSpec((tr, C), lambda i, me_ref: (i, 0))
        out_spec = lambda tr: pl.BlockSpec((tr, C), lambda i, me_ref: (i, me_ref[0]))
    else:
        R, C = zone.shape[1:]
        src_spec = ((lambda tr: pl.BlockSpec((tr, C), lambda i, me_ref: (i, 0))) if gather else
                    (lambda tr: pl.BlockSpec((None, tr, C), lambda i, me_ref: (me_ref[0], i, 0))))
        out_spec = lambda tr: pl.BlockSpec((None, tr, C), lambda i, me_ref: (me_ref[0], i, 0))
    tr = R if R <= 512 else 256
    assert R % tr == 0, (name, R, tr)

    def body(me_ref, src_ref, zone_ref, out_ref):
        del me_ref, zone_ref
        out_ref[...] = src_ref[...]

    return pl.pallas_call(
        body, name=name, out_shape=jax.ShapeDtypeStruct(zone.shape, zone.dtype),
        grid_spec=pltpu.PrefetchScalarGridSpec(
            num_scalar_prefetch=1, grid=(R // tr,),
            in_specs=[src_spec(tr), pl.BlockSpec(memory_space=pl.ANY)], out_specs=out_spec(tr)),
        input_output_aliases={2: 0},
    )(jnp.reshape(me, (1,)).astype(jnp.int32), src, zone)


def _xwait_many(srcs, lands, send_sems, recv_sems, after, name):
    n = len(srcs)

    def body(*refs):
        src_refs, land_refs = refs[:n], refs[n:2 * n]
        snd, rcv = refs[2 * n:3 * n], refs[3 * n:4 * n]
        for a in range(n):
            for p in range(1, N_DEV):
                dev, peer = _peer_of(p)
                cp = pltpu.make_async_remote_copy(
                    src_ref=src_refs[a], dst_ref=land_refs[a].at[peer], send_sem=snd[a].at[p - 1],
                    recv_sem=rcv[a].at[p - 1], device_id=dev, device_id_type=MESH)
                cp.wait_send()
                cp.wait_recv()

    res = pl.pallas_call(
        body, name=name, out_shape=tuple(pltpu.HBM(t.shape, t.dtype) for t in list(srcs) + list(lands)),
        in_specs=[_HBM] * (2 * n) + [_SEM] * (2 * n) + [pl.BlockSpec(memory_space=pl.ANY)],
        out_specs=tuple([_HBM] * (2 * n)), input_output_aliases={i: i for i in range(2 * n)},
        compiler_params=pltpu.CompilerParams(has_side_effects=_EFFECT),
    )(*srcs, *lands, *send_sems, *recv_sems, after)
    return res[:n], res[n:]


def _mm(a, b, *, mode, tm, tn, tk, outs, epilogue=None, extras=(), nb=None, tok=None, scatter=None, into=None, name):
    if mode == "nn":
        (M, K), (_, N) = a.shape, b.shape
    elif mode == "nt":
        (M, K), (N, _) = a.shape, b.shape
    else:
        (K, M), (_, N) = a.shape, b.shape
    tm, tn, tk = min(tm, M), min(tn, N), min(tk, K)
    assert M % tm == 0 and N % tn == 0 and K % tk == 0, (name, M, N, K, tm, tn, tk)
    if mode == "nn":
        a_spec = pl.BlockSpec((tm, tk), lambda i, j, k: (i, k))
        b_spec = pl.BlockSpec((tk, tn), lambda i, j, k: (k, j))
        dims = (((1,), (0,)), ((), ()))
    elif mode == "nt":
        a_spec = pl.BlockSpec((tm, tk), lambda i, j, k: (i, k))
        b_spec = pl.BlockSpec((tn, tk), lambda i, j, k: (j, k))
        dims = (((1,), (1,)), ((), ()))
    else:
        a_spec = pl.BlockSpec((tk, tm), lambda i, j, k: (k, i))
        b_spec = pl.BlockSpec((tk, tn), lambda i, j, k: (k, j))
        dims = (((0,), (0,)), ((), ()))
    nk = K // tk
    n_ex, n_out = len(extras), len(outs)
    n_tok = 0 if tok is None else 1
    nbytes = lambda d: jnp.dtype(d).itemsize
    vmem_est = (2 * (tm * tk * nbytes(a.dtype) + tk * tn * nbytes(b.dtype)
                     + sum(tm * tn * nbytes(e.dtype) for e, kind in extras if kind == "tile")
                     + sum(tm * tn * nbytes(d) for d in outs)) + tm * tn * 4)
    assert vmem_est <= VMEM_LIMIT, (name, vmem_est)
    if epilogue is None:
        epilogue = lambda acc, ex: tuple(acc.astype(d) for d in outs)

    n_into = 0 if into is None else 1

    def body(a_ref, b_ref, *refs):
        refs = refs[n_tok:]
        ex_refs, out_refs = refs[:n_ex], refs[n_ex + n_into:n_ex + n_into + n_out]

        def finish(acc):
            res = epilogue(acc, [r[...] for r in ex_refs])
            for o_ref, v in zip(out_refs, res):
                if nb is None:
                    o_ref[...] = v.astype(o_ref.dtype)
                else:
                    for q in range(tn // nb):
                        o_ref[q] = v[:, q * nb:(q + 1) * nb].astype(o_ref.dtype)

        part = lax.dot_general(a_ref[...], b_ref[...], dims, preferred_element_type=F32)
        if nk == 1:
            finish(part)
        else:
            acc_ref = refs[n_ex + n_into + n_out]
            k = pl.program_id(2)

            @pl.when(k == 0)
            def _():
                acc_ref[...] = part

            @pl.when(k > 0)
            def _():
                acc_ref[...] += part

            @pl.when(k == nk - 1)
            def _():
                finish(acc_ref[...])

    col = (lambda j: j) if scatter is None else (lambda j: scatter[0] * j + scatter[1])
    ex_specs = [pl.BlockSpec((tm, tn), lambda i, j, k: (i, j)) if kind == "tile"
                else pl.BlockSpec((1, tn), lambda i, j, k: (0, col(j))) for _, kind in extras]
    if nb is not None:
        assert tn % nb == 0, (name, tn, nb)
        o_spec = pl.BlockSpec((tn // nb, tm, nb), lambda i, j, k: (j, i, 0))
        o_shape = (N // nb, M, nb)
    else:
        o_spec = pl.BlockSpec((tm, tn), lambda i, j, k: (i, col(j)))
        o_shape = (M, N if scatter is None else scatter[2])
    assert n_into == 0 or n_out == 1
    res = pl.pallas_call(
        body, name=name, grid=(M // tm, N // tn, nk),
        in_specs=[a_spec, b_spec] + [pl.BlockSpec((SUBLANES, LANES), lambda i, j, k: (0, 0))] * n_tok + ex_specs
                 + [pl.BlockSpec(memory_space=pl.ANY)] * n_into,
        out_specs=tuple([o_spec] * n_out),
        out_shape=tuple(jax.ShapeDtypeStruct(o_shape, d) for d in outs),
        input_output_aliases={2 + n_tok + n_ex: 0} if n_into else {},
        scratch_shapes=[pltpu.VMEM((tm, tn), F32)] if nk > 1 else [],
        compiler_params=_cparams(3, big=True),
    )(a, b, *([tok] if n_tok else []), *[e for e, _ in extras], *([into] if n_into else []))
    return res[0] if n_out == 1 else res


def _mm_rows(a, b, *, mode, tm, seq, ins, outs, epilogue, tok=None, name):
    M, K = a.shape
    b_parts = list(b) if isinstance(b, (list, tuple)) else [b]
    n_part = len(b_parts)
    assert n_part == 1 or mode == "nt"
    N = b_parts[0].shape[1] if mode == "nn" else b_parts[0].shape[0]
    tm = min(tm, M)
    assert M % tm == 0 and seq % tm == 0, (name, M, seq, tm)
    tpb = seq // tm
    n_b = M // seq
    dims = (((1,), (0,)), ((), ())) if mode == "nn" else (((1,), (1,)), ((), ()))
    n_tok = 0 if tok is None else 1
    n_in, n_out = len(ins), len(outs)

    in_specs, in_arrs = [], []
    for spec in ins:
        kind, arr = spec[0], spec[1]
        in_arrs.append(arr)
        if kind == "tile":
            in_specs.append(pl.BlockSpec((tm, arr.shape[1]), lambda i: (i, 0)))
        elif kind == "tilecol":
            in_specs.append(pl.BlockSpec((tm, spec[2]), lambda i, cb=spec[3]: (i, cb)))
        elif kind == "row":
            in_specs.append(pl.BlockSpec(arr.shape, lambda i: (0, 0)))
        else:
            in_specs.append(pl.BlockSpec((None, 1, arr.shape[2]), lambda i: (i // tpb, 0, 0)))
    out_specs, out_shapes = [], []
    for spec in outs:
        kind = spec[0]
        if kind == "tile":
            out_specs.append(pl.BlockSpec((tm, spec[2]), lambda i: (i, 0)))
            out_shapes.append(jax.ShapeDtypeStruct((M, spec[2]), spec[1]))
        elif kind == "tilecol":
            out_specs.append(pl.BlockSpec((tm, spec[2]), lambda i, cb=spec[3]: (i, cb)))
            out_shapes.append(jax.ShapeDtypeStruct((M, spec[4]), spec[1]))
        elif kind == "acc_row":
            out_specs.append(pl.BlockSpec((1, spec[1]), lambda i: (0, 0)))
            out_shapes.append(jax.ShapeDtypeStruct((1, spec[1]), F32))
        elif kind == "acc_brow":
            out_specs.append(pl.BlockSpec((None, 1, spec[1]), lambda i: (i // tpb, 0, 0)))
            out_shapes.append(jax.ShapeDtypeStruct((n_b, 1, spec[1]), F32))
        else:
            out_specs.append(pl.BlockSpec((SUBLANES, LANES), lambda i: (0, 0)))
            out_shapes.append(jax.ShapeDtypeStruct((SUBLANES, LANES), F32))

    def body(a_ref, *refs):
        b_refs, refs = refs[:n_part], refs[n_part + n_tok:]
        in_refs, out_refs = refs[:n_in], refs[n_in:n_in + n_out]
        i = pl.program_id(0)
        if n_part == 1:
            prod = lax.dot_general(a_ref[...], b_refs[0][...], dims, preferred_element_type=F32)
        else:
            w = b_parts[0].shape[1] // N_DEV
            prod = None
            for q in range(n_part):
                a_q = jnp.concatenate([a_ref[:, (n_part * j + q) * w:(n_part * j + q + 1) * w] for j in range(N_DEV)],
                                      axis=1)
                pq = lax.dot_general(a_q, b_refs[q][...], dims, preferred_element_type=F32)
                prod = pq if prod is None else prod + pq
        vals = epilogue(prod, [r[...] for r in in_refs])
        for spec, o_ref, v in zip(outs, out_refs, vals):
            kind = spec[0]
            if kind in ("tile", "tilecol"):
                o_ref[...] = v.astype(o_ref.dtype)
            else:
                first = (i % tpb == 0) if kind == "acc_brow" else (i == 0)

                @pl.when(first)
                def _(o_ref=o_ref, v=v):
                    o_ref[...] = jnp.broadcast_to(v, o_ref.shape)

                @pl.when(jnp.logical_not(first))
                def _(o_ref=o_ref, v=v):
                    o_ref[...] += v

    res = pl.pallas_call(
        body, name=name, grid=(M // tm,),
        in_specs=[pl.BlockSpec((tm, K), lambda i: (i, 0))]
                 + [pl.BlockSpec(bp.shape, lambda i: (0, 0), pipeline_mode=pl.Buffered(1)) for bp in b_parts]
                 + [pl.BlockSpec((SUBLANES, LANES), lambda i: (0, 0))] * n_tok + in_specs,
        out_specs=tuple(out_specs), out_shape=tuple(out_shapes),
        compiler_params=_cparams(1, big=True),
    )(a, *b_parts, *([tok] if n_tok else []), *in_arrs)
    return res


def _tok_spec(ts, width, col_block=0):
    return pl.BlockSpec((None, ts, width), lambda b, s: (b, s, col_block))


def _brow_spec(width):
    return pl.BlockSpec((None, 1, width), lambda b, s: (b, 0, 0))


def _modulate(x, sc, sh, ts):
    Bl, S, D = x.shape

    def body(x_ref, sc_ref, sh_ref, o_ref):
        o_ref[...] = (x_ref[...] * (1.0 + sc_ref[...]) + sh_ref[...]).astype(BF16)

    return pl.pallas_call(
        body, name="modulate", grid=(Bl, S // ts),
        in_specs=[_tok_spec(ts, D), _brow_spec(D), _brow_spec(D)],
        out_specs=_tok_spec(ts, D), out_shape=jax.ShapeDtypeStruct((Bl, S, D), BF16),
        compiler_params=_cparams(2),
    )(x, sc, sh)


def _mix_fwd(proj, w_conv, b_conv, w_rg_a, b_rg_a, w_rg_x, b_rg_x, lam, w_sp, b_sp_t, ln_v_g, ln_v_b, *, tm, lw, sw):
    Bl, S, _ = proj.shape
    heads, hd = w_rg_a.shape[0], w_rg_a.shape[1]
    groups = w_sp.shape[0]
    cw = 2 * lw + 2 * sw
    nblk = tm // SGU_BLOCK

    G = tm // SUBLANES
    nc = lw // LANES

    def body(p_ref, wc_ref, bc_ref, wa_ref, ba_ref, wx_ref, bx_ref, lam_ref, wsp_ref, bsp_ref, lg_ref, lb_ref,
             hs_ref, ya_ref, ys_ref, xc_ref, r_ref, ig_ref, a_ref, m_ref,
             xext, hnat, hcar, h7_scr, a7_scr, hp_scr):
        s = pl.program_id(1)

        @pl.when(s == 0)
        def _():
            xext[:, 0:SUBLANES, :] = jnp.zeros((nc, SUBLANES, LANES), F32)
            hcar[...] = jnp.zeros_like(hcar)

        @pl.when(s > 0)
        def _():
            xext[:, 0:SUBLANES, :] = xext[:, tm:tm + SUBLANES, :]

        for c in range(nc):
            xext[c, SUBLANES:SUBLANES + tm, :] = p_ref[:, c * LANES:(c + 1) * LANES].astype(F32)
        gl = p_ref[:, lw:2 * lw].astype(F32)

        def slab(ref3, start):
            return jnp.concatenate([ref3[c, pl.ds(start, G, stride=SUBLANES), :] for c in range(nc)], axis=1)

        xs = {st: slab(xext, st) for st in range(SUBLANES - 3, 2 * SUBLANES)}
        xc_slabs = []
        for j in range(SUBLANES):
            acc = bc_ref[...] + xs[SUBLANES + j] * wc_ref[3:4, :]
            for k in (1, 2, 3):
                acc = acc + xs[SUBLANES + j - k] * wc_ref[3 - k:4 - k, :]
            xc_slabs.append(acc)
        xc = jnp.concatenate(xc_slabs, axis=0)

        xcb = xc.astype(BF16)
        pa = jnp.concatenate([jnp.dot(xcb[:, h * hd:(h + 1) * hd], wa_ref[h], preferred_element_type=F32)
                              for h in range(heads)], axis=1) + ba_ref[...]
        px = jnp.concatenate([jnp.dot(xcb[:, h * hd:(h + 1) * hd], wx_ref[h], preferred_element_type=F32)
                              for h in range(heads)], axis=1) + bx_ref[...]
        r = _sigmoid(pa)
        ig = _sigmoid(px)
        nl = -lam_ref[...]
        big_l = -LRU_C * (jnp.maximum(nl, 0.0) + _log1p_pos(jnp.exp(-jnp.abs(nl))))
        la = big_l * r
        a = jnp.exp(la)
        th = jnp.tanh(la)
        msq = (-2.0 * th) / (1.0 - th)
        m = msq * lax.rsqrt(jnp.maximum(msq, 1e-30))
        bin_ = m * (ig * xc)
        xc_ref[...] = xc
        r_ref[...] = r
        ig_ref[...] = ig
        a_ref[...] = a
        m_ref[...] = m

        h0 = [bin_[0:G]]
        cp = [a[0:G]]
        for j in range(1, SUBLANES):
            aj = a[j * G:(j + 1) * G]
            h0.append(aj * h0[j - 1] + bin_[j * G:(j + 1) * G])
            cp.append(aj * cp[j - 1])
        h7_scr[...] = h0[SUBLANES - 1]
        a7_scr[...] = cp[SUBLANES - 1]
        carry = hcar[0:1, :]
        for g in range(G):
            hp_scr[g:g + 1, :] = carry
            carry = h7_scr[g:g + 1, :] + a7_scr[g:g + 1, :] * carry
        hcar[0:1, :] = carry
        hprev = hp_scr[...]
        for j in range(SUBLANES):
            hj = h0[j] + cp[j] * hprev
            for c in range(nc):
                hnat[c, pl.ds(j, G, stride=SUBLANES), :] = hj[:, c * LANES:(c + 1) * LANES]
        hs = jnp.concatenate([hnat[c] for c in range(nc)], axis=1)
        hs_ref[...] = hs
        ya_ref[...] = (hs * _gelu(gl)).astype(BF16)

        gu = _gelu(p_ref[:, 2 * lw:2 * lw + sw].astype(F32))
        gv = _gelu(p_ref[:, 2 * lw + sw:cw].astype(F32))
        xhat, _ = _ln_stats(gv)
        vn = (xhat * lg_ref[...] + lb_ref[...]).astype(BF16)
        tpos = lax.broadcasted_iota(jnp.int32, (SGU_BLOCK, SGU_BLOCK), 0) // CHUNK
        spos = lax.broadcasted_iota(jnp.int32, (SGU_BLOCK, SGU_BLOCK), 1) // CHUNK
        gw = sw // groups
        rows_out = []
        for blk in range(nblk):
            r0 = blk * SGU_BLOCK
            cols = []
            for g in range(groups):
                wm = jnp.where(spos <= tpos, wsp_ref[g], 0.0).astype(BF16)
                mixed = jnp.dot(wm, vn[r0:r0 + SGU_BLOCK, g * gw:(g + 1) * gw], preferred_element_type=F32)
                cols.append(mixed + bsp_ref[:, g:g + 1])
            rows_out.append(jnp.concatenate(cols, axis=1))
        mixed_all = jnp.concatenate(rows_out, axis=0) if nblk > 1 else rows_out[0]
        ys_ref[...] = (gu * mixed_all).astype(BF16)

    full = lambda shp: pl.BlockSpec(shp, lambda b, s: (0,) * len(shp))
    return pl.pallas_call(
        body, name="mix_fwd", grid=(Bl, S // tm),
        in_specs=[_tok_spec(tm, cw), full(w_conv.shape), full(b_conv.shape), full(w_rg_a.shape), full(b_rg_a.shape),
                  full(w_rg_x.shape), full(b_rg_x.shape), full(lam.shape), full(w_sp.shape), full(b_sp_t.shape),
                  full(ln_v_g.shape), full(ln_v_b.shape)],
        out_specs=(_tok_spec(tm, lw), _tok_spec(tm, lw), _tok_spec(tm, sw)) + (_tok_spec(tm, lw),) * 5,
        out_shape=(jax.ShapeDtypeStruct((Bl, S, lw), F32), jax.ShapeDtypeStruct((Bl, S, lw), BF16),
                   jax.ShapeDtypeStruct((Bl, S, sw), BF16)) + (jax.ShapeDtypeStruct((Bl, S, lw), F32),) * 5,
        scratch_shapes=[pltpu.VMEM((nc, tm + SUBLANES, LANES), F32), pltpu.VMEM((nc, tm, LANES), F32),
                        pltpu.VMEM((SUBLANES, lw), F32), pltpu.VMEM((G, lw), F32), pltpu.VMEM((G, lw), F32),
                        pltpu.VMEM((G, lw), F32)],
        compiler_params=_cparams(2, big=True),
    )(proj, w_conv, b_conv, w_rg_a, b_rg_a, w_rg_x, b_rg_x, lam, w_sp, b_sp_t, ln_v_g, ln_v_b)


def _mix_bwd(proj, hs, dya, dys, dproj, saved, w_conv, b_conv, w_rg_a, b_rg_a, w_rg_x, b_rg_x, lam, w_sp, b_sp_t,
             ln_v_g, ln_v_b, *, tm, lw, sw):
    Bl, S, din = proj.shape
    heads, hd = w_rg_a.shape[0], w_rg_a.shape[1]
    groups = w_sp.shape[0]
    gw = sw // groups
    cw = 2 * lw + 2 * sw
    nblk = tm // SGU_BLOCK
    n_s = S // tm
    per8 = tm // SUBLANES
    halo_rows = 2 * SUBLANES

    G = tm // SUBLANES
    nc = lw // LANES

    def body(p_ref, xh_ref, hs_ref, hh_ref, dya_ref, dys_ref, dpin_ref, xc_ref, r_ref, ig_ref, a_ref, m_ref,
             wc_ref, bc_ref, wa_ref, ba_ref, wx_ref, bx_ref, lam_ref, wsp_ref, bsp_ref, lg_ref, lb_ref,
             dp_ref, dbin_ref, dwc_ref, dbc_ref, dwa_ref, dba_ref, dwx_ref, dbx_ref, dlam_ref, dwsp_ref, dbsp_ref,
             dlg_ref, dlb_ref,
             xext, hext, dnat, dxext, dhcar, g00_scr, p0_scr, a0_scr, cin_scr):
        del dpin_ref
        sr = pl.program_id(1)
        first_tile = sr == n_s - 1

        @pl.when(_first_step())
        def _():
            for ref in (dbin_ref, dwc_ref, dbc_ref, dwa_ref, dba_ref, dwx_ref, dbx_ref, dlam_ref, dwsp_ref, dbsp_ref,
                        dlg_ref, dlb_ref):
                ref[...] = jnp.zeros_like(ref)

        @pl.when(sr == 0)
        def _():
            dhcar[...] = jnp.zeros_like(dhcar)
            dxext[:, tm:tm + SUBLANES, :] = jnp.zeros((nc, SUBLANES, LANES), F32)

        @pl.when(sr > 0)
        def _():
            dxext[:, tm:tm + SUBLANES, :] = dxext[:, 0:SUBLANES, :]

        def slab(ref3, start):
            return jnp.concatenate([ref3[c, pl.ds(start, G, stride=SUBLANES), :] for c in range(nc)], axis=1)

        def put_slab(ref3, j, val):
            for c in range(nc):
                ref3[c, pl.ds(j, G, stride=SUBLANES), :] = val[:, c * LANES:(c + 1) * LANES]

        keep = jnp.where(first_tile, 0.0, 1.0)
        xprev = xh_ref[...].astype(F32)[halo_rows - SUBLANES:halo_rows] * keep
        hsv = hs_ref[...]
        hprev8 = hh_ref[...] * keep
        for c in range(nc):
            cs = slice(c * LANES, (c + 1) * LANES)
            xext[c, 0:SUBLANES, :] = xprev[:, cs]
            xext[c, SUBLANES:SUBLANES + tm, :] = p_ref[:, cs].astype(F32)
            hext[c, 0:SUBLANES, :] = hprev8[:, cs]
            hext[c, SUBLANES:SUBLANES + tm, :] = hsv[:, cs]
        gl = p_ref[:, lw:2 * lw].astype(F32)
        ggl, dggl = _gelu_and_grad(gl)
        dyav = dya_ref[...].astype(F32)
        dhs = dyav * ggl
        dgl = dyav * hsv * dggl
        dp_ref[:, lw:2 * lw] = dgl.astype(BF16)
        dbin_ref[:, lw:2 * lw] += _colsum(dgl)
        for c in range(nc):
            dnat[c] = dhs[:, c * LANES:(c + 1) * LANES]

        xc, r, ig, a, m = xc_ref[...], r_ref[...], ig_ref[...], a_ref[...], m_ref[...]
        xcb = xc.astype(BF16)
        nl = -lam_ref[...]
        big_l = -LRU_C * (jnp.maximum(nl, 0.0) + _log1p_pos(jnp.exp(-jnp.abs(nl))))

        g0 = [None] * SUBLANES
        pp = [None] * SUBLANES
        g0[SUBLANES - 1] = slab(dnat, SUBLANES - 1)
        for j in range(SUBLANES - 2, -1, -1):
            an = a[(j + 1) * G:(j + 2) * G]
            g0[j] = slab(dnat, j) + an * g0[j + 1]
            pp[j] = an if j == SUBLANES - 2 else an * pp[j + 1]
        g00_scr[...] = g0[0]
        p0_scr[...] = pp[0]
        a0_scr[...] = a[0:G]
        cin = dhcar[0:1, :]
        for g in range(G - 1, -1, -1):
            cin_scr[g:g + 1, :] = cin
            cin = a0_scr[g:g + 1, :] * (g00_scr[g:g + 1, :] + p0_scr[g:g + 1, :] * cin)
        dhcar[0:1, :] = cin
        cinv = cin_scr[...]
        dh = jnp.concatenate([g0[j] + pp[j] * cinv for j in range(SUBLANES - 1)] + [g0[SUBLANES - 1] + cinv], axis=0)

        hprev = jnp.concatenate([slab(hext, SUBLANES - 1 + j) for j in range(SUBLANES)], axis=0)
        da = dh * hprev
        ixc = ig * xc
        dm = dh * ixc
        dixc = dh * m
        di = dixc * xc
        dxc = dixc * ig
        dla = da * a - dm * (a * a) / m
        dlam_ref[...] += _colsum(dla * r) * (LRU_C * _sigmoid(nl))
        dr = dla * big_l
        dpa = dr * r * (1.0 - r)
        dpx = di * ig * (1.0 - ig)
        dba_ref[...] += _colsum(dpa)
        dbx_ref[...] += _colsum(dpx)
        dpab = dpa.astype(BF16)
        dpxb = dpx.astype(BF16)
        nt = (((1,), (1,)), ((), ()))
        tn = (((0,), (0,)), ((), ()))
        dxc_g = []
        for h in range(heads):
            sl = slice(h * hd, (h + 1) * hd)
            dxc_g.append(lax.dot_general(dpab[:, sl], wa_ref[h], nt, preferred_element_type=F32)
                         + lax.dot_general(dpxb[:, sl], wx_ref[h], nt, preferred_element_type=F32))
            dwa_ref[h] += lax.dot_general(xcb[:, sl], dpab[:, sl], tn, preferred_element_type=F32)
            dwx_ref[h] += lax.dot_general(xcb[:, sl], dpxb[:, sl], tn, preferred_element_type=F32)
        dxc = dxc + jnp.concatenate(dxc_g, axis=1)

        dbc_ref[...] += _colsum(dxc)
        xs = {st: slab(xext, st) for st in range(SUBLANES - 3, 2 * SUBLANES)}
        for k in range(4):
            xsh = jnp.concatenate([xs[SUBLANES + j - (3 - k)] for j in range(SUBLANES)], axis=0)
            dwc_ref[k:k + 1, :] += _colsum(dxc * xsh)
        for j in range(SUBLANES):
            put_slab(dxext, j, dxc[j * G:(j + 1) * G])
        us = {st: slab(dxext, st) for st in range(SUBLANES + 3)}
        for j in range(SUBLANES):
            acc = us[j] * wc_ref[3:4, :]
            for k in (1, 2, 3):
                acc = acc + us[j + k] * wc_ref[3 - k:4 - k, :]
            put_slab(dnat, j, acc)
        dxl = jnp.concatenate([dnat[c] for c in range(nc)], axis=1)
        dp_ref[:, 0:lw] = dxl.astype(BF16)
        dbin_ref[:, 0:lw] += _colsum(dxl)

        gu, dgu_dx = _gelu_and_grad(p_ref[:, 2 * lw:2 * lw + sw].astype(F32))
        gv, dgv_dx = _gelu_and_grad(p_ref[:, 2 * lw + sw:cw].astype(F32))
        xhat, rstd = _ln_stats(gv)
        vn = (xhat * lg_ref[...] + lb_ref[...]).astype(BF16)
        dys = dys_ref[...].astype(F32)
        dmixed = dys * gu
        dmb = dmixed.astype(BF16)
        tpos = lax.broadcasted_iota(jnp.int32, (SGU_BLOCK, SGU_BLOCK), 0) // CHUNK
        spos = lax.broadcasted_iota(jnp.int32, (SGU_BLOCK, SGU_BLOCK), 1) // CHUNK
        causal = spos <= tpos
        mixed_rows, dvn_rows = [], []
        for blk in range(nblk):
            rs = slice(blk * SGU_BLOCK, (blk + 1) * SGU_BLOCK)
            mcols, dcols = [], []
            for g in range(groups):
                cs = slice(g * gw, (g + 1) * gw)
                wm = jnp.where(causal, wsp_ref[g], 0.0).astype(BF16)
                mcols.append(jnp.dot(wm, vn[rs, cs], preferred_element_type=F32) + bsp_ref[:, g:g + 1])
                dcols.append(lax.dot_general(wm, dmb[rs, cs], tn, preferred_element_type=F32))
                dw = lax.dot_general(dmb[rs, cs], vn[rs, cs], nt, preferred_element_type=F32)
                dwsp_ref[g] += jnp.where(causal, dw, 0.0)
                dbsp_ref[:, g:g + 1] += jnp.sum(dmixed[rs, cs], axis=1, keepdims=True)
            mixed_rows.append(jnp.concatenate(mcols, axis=1))
            dvn_rows.append(jnp.concatenate(dcols, axis=1))
        mixed_all = jnp.concatenate(mixed_rows, axis=0) if nblk > 1 else mixed_rows[0]
        dvn = jnp.concatenate(dvn_rows, axis=0) if nblk > 1 else dvn_rows[0]
        du = dys * mixed_all * dgu_dx
        dlg_ref[...] += _colsum(dvn * xhat)
        dlb_ref[...] += _colsum(dvn)
        dv = _ln_bwd(dvn, xhat, rstd, lg_ref[...]) * dgv_dx
        dp_ref[:, 2 * lw:2 * lw + sw] = du.astype(BF16)
        dp_ref[:, 2 * lw + sw:cw] = dv.astype(BF16)
        dbin_ref[:, 2 * lw:2 * lw + sw] += _colsum(du)
        dbin_ref[:, 2 * lw + sw:cw] += _colsum(dv)

    rev = lambda s: n_s - 1 - s
    tile = lambda w: pl.BlockSpec((None, tm, w), lambda b, s: (b, rev(s), 0))
    halo = lambda w: pl.BlockSpec((None, SUBLANES, w), lambda b, s: (b, jnp.maximum(rev(s) * per8 - 1, 0), 0))
    xhalo = pl.BlockSpec((None, halo_rows, lw), lambda b, s: (b, jnp.maximum(rev(s) * (tm // halo_rows) - 1, 0), 0))
    full = lambda shp: pl.BlockSpec(shp, lambda b, s: (0,) * len(shp))
    small = [w_conv, b_conv, w_rg_a, b_rg_a, w_rg_x, b_rg_x, lam, w_sp, b_sp_t, ln_v_g, ln_v_b]
    acc_shapes = [(1, cw), w_conv.shape, b_conv.shape, w_rg_a.shape, b_rg_a.shape, w_rg_x.shape, b_rg_x.shape,
                  lam.shape, w_sp.shape, b_sp_t.shape, ln_v_g.shape, ln_v_b.shape]
    res = pl.pallas_call(
        body, name="mix_bwd", grid=(Bl, n_s),
        in_specs=[tile(cw), xhalo, tile(lw), halo(lw), tile(lw), tile(sw), pl.BlockSpec(memory_space=pl.ANY)]
                 + [tile(lw)] * 5 + [full(w.shape) for w in small],
        out_specs=tuple([tile(cw)] + [full(shp) for shp in acc_shapes]),
        out_shape=tuple([jax.ShapeDtypeStruct((Bl, S, din), BF16)] + [jax.ShapeDtypeStruct(shp, F32) for shp in acc_shapes]),
        input_output_aliases={6: 0},
        scratch_shapes=[pltpu.VMEM((nc, tm + SUBLANES, LANES), F32), pltpu.VMEM((nc, tm + SUBLANES, LANES), F32),
                        pltpu.VMEM((nc, tm, LANES), F32), pltpu.VMEM((nc, tm + SUBLANES, LANES), F32),
                        pltpu.VMEM((SUBLANES, lw), F32), pltpu.VMEM((G, lw), F32), pltpu.VMEM((G, lw), F32),
                        pltpu.VMEM((G, lw), F32), pltpu.VMEM((G, lw), F32)],
        compiler_params=_cparams(2, big=True),
    )(proj, proj, hs, hs, dya, dys, dproj, *saved, *small)
    return res


def _ada_fwd(c_all, w_ada):
    R, D = c_all.shape
    nb = w_ada.shape[1]

    def body(c_ref, w_ref, act_ref, o_ref):
        cv = c_ref[...]
        act = (cv * _sigmoid(cv)).astype(BF16)
        act_ref[...] = act
        o_ref[...] = jnp.dot(act, w_ref[...].astype(BF16), preferred_element_type=F32)

    return pl.pallas_call(
        body, name="ada_fwd",
        out_shape=(jax.ShapeDtypeStruct((R, D), BF16), jax.ShapeDtypeStruct((R, nb), F32)),
        compiler_params=pltpu.CompilerParams(vmem_limit_bytes=VMEM_LIMIT),
    )(c_all, w_ada)


def _ada_bwd(c_act, dmod_cols):
    R, D = c_act.shape
    nb = dmod_cols.shape[1]

    def body(act_ref, d_ref, o_ref, b_ref):
        o_ref[...] = lax.dot_general(act_ref[...], d_ref[...].astype(BF16), (((0,), (0,)), ((), ())),
                                     preferred_element_type=F32)
        b_ref[...] = _colsum(d_ref[...])

    return pl.pallas_call(
        body, name="ada_bwd", out_shape=(jax.ShapeDtypeStruct((D, nb), F32), jax.ShapeDtypeStruct((1, nb), F32)),
        compiler_params=pltpu.CompilerParams(vmem_limit_bytes=VMEM_LIMIT),
    )(c_act, dmod_cols)


def _adamw(w, g_slots, m, v, *, tr, name, own=None):
    R, C = w.shape
    n_slot = g_slots.shape[0]
    tr = min(tr, R)
    assert R % tr == 0, (name, R, tr)
    c1 = 1.0 / (1.0 - ADAM_B1 ** ADAM_STEP)
    c2 = 1.0 / (1.0 - ADAM_B2 ** ADAM_STEP)
    n_own = 0 if own is None else 1

    def body(me_ref, w_ref, g_ref, *refs):
        m_ref, v_ref, go_ref, d_ref, mo_ref, vo_ref = refs[n_own:]
        slot = lambda d: (jnp.where(me_ref[0] == d, refs[0][...], g_ref[d]) if n_own else g_ref[d]).astype(F32)
        g = slot(0)
        for d in range(1, n_slot):
            g = g + slot(d)
        mn = ADAM_B1 * m_ref[...] + (1.0 - ADAM_B1) * g
        vn = ADAM_B2 * v_ref[...] + (1.0 - ADAM_B2) * (g * g)
        go_ref[...] = g
        mo_ref[...] = mn
        vo_ref[...] = vn
        d_ref[...] = -ADAM_LR * ((mn * c1) / (jnp.sqrt(vn * c2) + ADAM_EPS) + ADAM_WD * w_ref[...])

    me = 4 * lax.axis_index("x") + 2 * lax.axis_index("y") + lax.axis_index("c")
    blk = pl.BlockSpec((tr, C), lambda i, me_ref: (i, 0))
    own_specs = [pl.BlockSpec((None, tr, C), lambda i, me_ref: (me_ref[0], i, 0))] * n_own
    return pl.pallas_call(
        body, name=name, out_shape=tuple(jax.ShapeDtypeStruct((R, C), F32) for _ in range(4)),
        grid_spec=pltpu.PrefetchScalarGridSpec(
            num_scalar_prefetch=1, grid=(R // tr,),
            in_specs=[blk, pl.BlockSpec((n_slot, tr, C), lambda i, me_ref: (0, i, 0))] + own_specs + [blk, blk],
            out_specs=(blk, blk, blk, blk)),
        compiler_params=_cparams(1, big=True),
    )(jnp.reshape(me, (1,)).astype(jnp.int32), w, g_slots, *([own] if n_own else []), m, v)


def _adamw_many(ws, g_slots, g_owns, ms, vs, *, name):
    n = len(ws)
    c1 = 1.0 / (1.0 - ADAM_B1 ** ADAM_STEP)
    c2 = 1.0 / (1.0 - ADAM_B2 ** ADAM_STEP)

    def body(*refs):
        w_refs, g_refs, o_refs = refs[:n], refs[n:2 * n], refs[2 * n:3 * n]
        m_refs, v_refs = refs[3 * n:4 * n], refs[4 * n:5 * n]
        outs = refs[5 * n:]
        me = 4 * lax.axis_index("x") + 2 * lax.axis_index("y") + lax.axis_index("c")
        for i in range(n):
            own = o_refs[i][...]
            g = jnp.where(me == 0, own, g_refs[i][0])
            for d in range(1, N_DEV):
                g = g + jnp.where(me == d, own, g_refs[i][d])
            mn = ADAM_B1 * m_refs[i][...] + (1.0 - ADAM_B1) * g
            vn = ADAM_B2 * v_refs[i][...] + (1.0 - ADAM_B2) * (g * g)
            outs[i][...] = g
            outs[n + i][...] = -ADAM_LR * ((mn * c1) / (jnp.sqrt(vn * c2) + ADAM_EPS) + ADAM_WD * w_refs[i][...])
            outs[2 * n + i][...] = mn
            outs[3 * n + i][...] = vn

    res = pl.pallas_call(
        body, name=name, out_shape=tuple(jax.ShapeDtypeStruct(w.shape, F32) for _ in range(4) for w in ws),
        compiler_params=pltpu.CompilerParams(vmem_limit_bytes=VMEM_LIMIT),
    )(*ws, *g_slots, *g_owns, *ms, *vs)
    return res[:n], res[n:2 * n], res[2 * n:3 * n], res[3 * n:]


SMALL_NAMES = ("b_ada", "b_in", "b_conv", "w_rg_a", "b_rg_a", "w_rg_x", "b_rg_x", "lru_lambda", "w_sp", "b_sp",
               "ln_v_g", "ln_v_b", "ln1_g", "ln1_b", "ln2_g", "ln2_b")
WEIGHT_ORDER = ("w_ada", "b_ada", "w_in", "b_in", "w_conv", "b_conv", "w_rg_a", "b_rg_a", "w_rg_x", "b_rg_x",
                "lru_lambda", "w_sp", "b_sp", "ln_v_g", "ln_v_b", "w_o_lru", "w_o_sgu", "w_out", "ln1_g", "ln1_b",
                "w_up", "w_down", "ln2_g", "ln2_b")


def _blocked_cols(w2d):
    K, N = w2d.shape
    return jnp.transpose(w2d.reshape(K, N_DEV, N // N_DEV), (1, 0, 2))


def _unblock_cols(wb):
    n, K, nb = wb.shape
    return jnp.transpose(wb, (1, 0, 2)).reshape(K, n * nb)


def kernel(x, c, w_ada, b_ada, w_in, b_in, w_conv, b_conv, w_rg_a, b_rg_a, w_rg_x, b_rg_x, lru_lambda, w_sp, b_sp, ln_v_g, ln_v_b, w_o_lru, w_o_sgu, w_out, ln1_g, ln1_b, w_up, w_down, ln2_g, ln2_b, loss_target, m_w_ada, m_b_ada, m_w_in, m_b_in, m_w_conv, m_b_conv, m_w_rg_a, m_b_rg_a, m_w_rg_x, m_b_rg_x, m_lru_lambda, m_w_sp, m_b_sp, m_ln_v_g, m_ln_v_b, m_w_o_lru, m_w_o_sgu, m_w_out, m_ln1_g, m_ln1_b, m_w_up, m_w_down, m_ln2_g, m_ln2_b, v_w_ada, v_b_ada, v_w_in, v_b_in, v_w_conv, v_b_conv, v_w_rg_a, v_b_rg_a, v_w_rg_x, v_b_rg_x, v_lru_lambda, v_w_sp, v_b_sp, v_ln_v_g, v_ln_v_b, v_w_o_lru, v_w_o_sgu, v_w_out, v_ln1_g, v_ln1_b, v_w_up, v_w_down, v_ln2_g, v_ln2_b):
    W = dict(w_ada=w_ada, b_ada=b_ada, w_in=w_in, b_in=b_in, w_conv=w_conv, b_conv=b_conv, w_rg_a=w_rg_a,
             b_rg_a=b_rg_a, w_rg_x=w_rg_x, b_rg_x=b_rg_x, lru_lambda=lru_lambda, w_sp=w_sp, b_sp=b_sp,
             ln_v_g=ln_v_g, ln_v_b=ln_v_b, w_o_lru=w_o_lru, w_o_sgu=w_o_sgu, w_out=w_out, ln1_g=ln1_g, ln1_b=ln1_b,
             w_up=w_up, w_down=w_down, ln2_g=ln2_g, ln2_b=ln2_b)
    Mo = dict(w_ada=m_w_ada, b_ada=m_b_ada, w_in=m_w_in, b_in=m_b_in, w_conv=m_w_conv, b_conv=m_b_conv,
              w_rg_a=m_w_rg_a, b_rg_a=m_b_rg_a, w_rg_x=m_w_rg_x, b_rg_x=m_b_rg_x, lru_lambda=m_lru_lambda,
              w_sp=m_w_sp, b_sp=m_b_sp, ln_v_g=m_ln_v_g, ln_v_b=m_ln_v_b, w_o_lru=m_w_o_lru, w_o_sgu=m_w_o_sgu,
              w_out=m_w_out, ln1_g=m_ln1_g, ln1_b=m_ln1_b, w_up=m_w_up, w_down=m_w_down, ln2_g=m_ln2_g,
              ln2_b=m_ln2_b)
    Vo = dict(w_ada=v_w_ada, b_ada=v_b_ada, w_in=v_w_in, b_in=v_b_in, w_conv=v_w_conv, b_conv=v_b_conv,
              w_rg_a=v_w_rg_a, b_rg_a=v_b_rg_a, w_rg_x=v_w_rg_x, b_rg_x=v_b_rg_x, lru_lambda=v_lru_lambda,
              w_sp=v_w_sp, b_sp=v_b_sp, ln_v_g=v_ln_v_g, ln_v_b=v_ln_v_b, w_o_lru=v_w_o_lru, w_o_sgu=v_w_o_sgu,
              w_out=v_w_out, ln1_g=v_ln1_g, ln1_b=v_ln1_b, w_up=v_w_up, w_down=v_w_down, ln2_g=v_ln2_g,
              ln2_b=v_ln2_b)

    Bl, S, D = x.shape
    T = Bl * S
    lw = b_conv.shape[-1]
    sw = ln_v_g.shape[-1]
    din = b_in.shape[-1]
    dff = w_up.shape[-1] * N_DEV
    ts = min(2048, S)
    tmix = min(256, S)
    trow = min(512, S)

    c_pad = jnp.pad(c, ((0, SUBLANES - Bl), (0, 0)))
    c_g, wconv_g = _exchange([c_pad, w_conv[0]], True, "xchg_c")
    wconv_full = _unblock_cols(wconv_g)
    c_act, modcols = _ada_fwd(c_g.reshape(N_DEV * SUBLANES, D), w_ada[0])
    (mod_slots,) = _exchange([modcols.reshape(N_DEV, SUBLANES, -1)], False, "xchg_mod")

    nbw = din // N_DEV // WIN_PARTS
    wnames = tuple("win%d" % q for q in range(WIN_PARTS)) + ("wol", "wos", "wout", "wup", "wdown")
    shards = [w_in[0][:, q * nbw:(q + 1) * nbw].astype(BF16) for q in range(WIN_PARTS)] + [
        w_o_lru[0].astype(BF16), w_o_sgu[0].astype(BF16), w_out[0].astype(BF16), w_up[0].astype(BF16),
        w_down[0].astype(BF16)]
    col_sharded = [True] * WIN_PARTS + [False, True, False, True, False]
    g_send, g_recv, g_src, g_land, g_tok = _xstart(shards, True, mod_slots, "gather_start", cols=col_sharded)
    gidx = {n: i for i, n in enumerate(wnames)}

    def gathered(n, after):
        i = gidx[n]
        return _xwait(g_src[i], g_land[i], g_send[i], g_recv[i], after, True, "gather_wait_" + n, col=col_sharded[i])

    mod = _unblock_cols(mod_slots)[:Bl] + (b_ada + g_tok[0, 0])
    sh1, sc1, gt1, sh2, sc2, gt2 = [mod[:, i * D:(i + 1) * D].reshape(Bl, 1, D) for i in range(6)]

    wa_b, wx_b = w_rg_a[0].astype(BF16), w_rg_x[0].astype(BF16)
    b_sp_t = jnp.transpose(b_sp[0])
    small_mix = (wconv_full, b_conv, wa_b, b_rg_a, wx_b, b_rg_x, lru_lambda, w_sp[0], b_sp_t, ln_v_g, ln_v_b)

    h = _modulate(x, sc1, sh1, ts)
    proj, win_parts = None, []
    for q in range(WIN_PARTS):
        wq = gathered("win%d" % q, h if q == 0 else proj)
        win_parts.append(wq)
        proj = _mm(h.reshape(T, D), wq, mode="nn", tm=4096, tn=nbw, tk=D, outs=[BF16], extras=[(b_in, "row")],
                   epilogue=lambda acc, ex: (acc + ex[0],), scatter=(WIN_PARTS, q, din), into=proj,
                   name="mm_proj%d" % q)
    proj3 = proj.reshape(Bl, S, din)
    hs, ya_pre, ysgu, *lru_saved = _mix_fwd(proj3, *small_mix, tm=tmix, lw=lw, sw=sw)
    Wol = gathered("wol", ya_pre).reshape(lw, D)
    Wos = gathered("wos", ysgu)
    y_a = _mm(ya_pre.reshape(T, lw), Wol, mode="nn", tm=2048, tn=D, tk=lw, outs=[BF16], name="mm_ya")
    x2d, tgt2d = x.reshape(T, D), loss_target.reshape(T, D)
    gate_cb = (din - 2 * D) // D

    def ep_merge(y_b, v):
        ya, ga, gb = [t.astype(F32) for t in v]
        yb = y_b.astype(BF16).astype(F32)
        return [yb, _sigmoid(ga) * ya + _sigmoid(gb) * yb]

    y_b, merged = _mm_rows(ysgu.reshape(T, sw), Wos, mode="nn", tm=trow, seq=S,
                           ins=[("tile", y_a), ("tilecol", proj, D, gate_cb), ("tilecol", proj, D, gate_cb + 1)],
                           outs=[("tile", BF16, D), ("tile", BF16, D)], epilogue=ep_merge, name="mm_yb_merge")
    Wout = gathered("wout", merged).reshape(D, D)

    def ep_ln1(mix_acc, v):
        x_, gt, g, b, sc, sh = v
        mixr = mix_acc.astype(BF16).astype(F32)
        xhat, _ = _ln_stats(ALPHA * x_ + (1.0 + gt) * mixr)
        x1_ = xhat * g + b
        return [mixr, x1_, x1_ * (1.0 + sc) + sh]

    mix, x1, h2 = _mm_rows(merged, Wout, mode="nn", tm=trow, seq=S,
                           ins=[("tile", x2d), ("brow", gt1), ("row", ln1_g), ("row", ln1_b), ("brow", sc2),
                                ("brow", sh2)],
                           outs=[("tile", BF16, D), ("tile", F32, D), ("tile", BF16, D)], epilogue=ep_ln1,
                           name="mm_mix_ln1")
    Wup = gathered("wup", h2)
    act = _mm(h2, Wup, mode="nn", tm=2048, tn=1024, tk=D, outs=[BF16],
              epilogue=lambda acc, ex: (jnp.square(jnp.maximum(acc, 0.0)),), name="mm_up")
    Wdown = gathered("wdown", act).reshape(dff, D)

    def ep_ln2(f_acc, v):
        x1_, t_, gt, g, b = v
        xhat, rstd = _ln_stats(ALPHA * x1_ + (1.0 + gt) * f_acc)
        err = xhat * g + b - t_
        loss_t = 0.5 * jnp.sum(jnp.mean(err * err, axis=-1, keepdims=True))
        dy = err * (1.0 / D)
        dz = _ln_bwd(dy, xhat, rstd, g)
        return [dz * (1.0 + gt), ALPHA * dz, _colsum(dz * f_acc), _colsum(dy * xhat), _colsum(dy), loss_t]

    df2, dx1p, dgt2, dg2, db2, loss_part = _mm_rows(
        act, Wdown, mode="nn", tm=trow, seq=S,
        ins=[("tile", x1), ("tile", tgt2d), ("brow", gt2), ("row", ln2_g), ("row", ln2_b)],
        outs=[("tile", BF16, D), ("tile", F32, D), ("acc_brow", D), ("acc_row", D), ("acc_row", D), ("acc_scalar",)],
        epilogue=ep_ln2, name="mm_down_ln2")
    loss = lax.psum(loss_part[0, 0], ("x", "y", "c"))

    def send_grads(parts, name):
        snd, rcv, src, land, tok = _xstart(parts, False, None, name + "_start")
        return [(src[i], land[i], snd[i], rcv[i]) for i in range(len(parts))], tok

    dup = _mm(df2, Wdown, mode="nt", tm=2048, tn=1024, tk=D, outs=[BF16], extras=[(act, "tile")],
              epilogue=lambda acc, ex: (acc * (2.0 * jnp.sqrt(ex[0].astype(F32))),), name="mm_dup")
    g_wdown = _mm(act, df2, mode="tn", tm=1024, tn=D, tk=2048, outs=[BF16], name="mm_gwdown")
    (x_wdown,), tok = send_grads([g_wdown.reshape(N_DEV, dff // N_DEV, D)], "gx_wdown")
    def ep_ln1_bwd(dh2, v):
        dx1p_, x1_, x_, mix_, sc, gt, g = v
        mixv = mix_.astype(F32)
        dx1 = dx1p_ + dh2 * (1.0 + sc)
        xhat, rstd = _ln_stats(ALPHA * x_ + (1.0 + gt) * mixv)
        dz = _ln_bwd(dx1, xhat, rstd, g)
        return [ALPHA * dz, dz * (1.0 + gt), _colsum(dh2 * x1_), _colsum(dh2), _colsum(dz * mixv),
                _colsum(dx1 * xhat), _colsum(dx1)]

    dxp, dmix, dsc2, dsh2, dgt1, dg1, db1 = _mm_rows(
        dup, Wup, mode="nt", tm=trow, seq=S, tok=tok,
        ins=[("tile", dx1p), ("tile", x1), ("tile", x2d), ("tile", mix), ("brow", sc2), ("brow", gt1), ("row", ln1_g)],
        outs=[("tile", F32, D), ("tile", BF16, D), ("acc_brow", D), ("acc_brow", D), ("acc_brow", D), ("acc_row", D),
              ("acc_row", D)],
        epilogue=ep_ln1_bwd, name="mm_dh2_ln1b")
    g_wup = _mm(h2, dup, mode="tn", tm=D, tn=1024, tk=2048, outs=[BF16], nb=dff // N_DEV, name="mm_gwup")
    (x_wup,), tok = send_grads([g_wup], "gx_wup")

    def ep_merge_bwd(dm, v):
        ya, yb, ga, gb = [t.astype(F32) for t in v]
        sa, sb = _sigmoid(ga), _sigmoid(gb)
        dg = jnp.concatenate([dm * ya * sa * (1.0 - sa), dm * yb * sb * (1.0 - sb)], axis=1)
        return [dm * sa, dm * sb, dg, _colsum(dg)]

    dy_a, dy_b, dproj, dbin_hi = _mm_rows(
        dmix, Wout, mode="nt", tm=trow, seq=S, tok=tok,
        ins=[("tile", y_a), ("tile", y_b), ("tilecol", proj, D, gate_cb), ("tilecol", proj, D, gate_cb + 1)],
        outs=[("tile", BF16, D), ("tile", BF16, D), ("tilecol", BF16, 2 * D, gate_cb // 2, din), ("acc_row", 2 * D)],
        epilogue=ep_merge_bwd, name="mm_dmerged_mb")
    g_wout = _mm(merged, dmix, mode="tn", tm=D, tn=D, tk=2048, outs=[BF16], name="mm_gwout")
    (x_wout,), tok = send_grads([g_wout.reshape(N_DEV, D // N_DEV, D)], "gx_wout")
    dya_pre = _mm(dy_a, Wol, mode="nt", tm=2048, tn=lw, tk=D, outs=[BF16], tok=tok, name="mm_dya")
    dysgu = _mm(dy_b, Wos, mode="nt", tm=2048, tn=sw, tk=D, outs=[BF16], name="mm_dys")
    g_wol = _mm(ya_pre.reshape(T, lw), dy_a, mode="tn", tm=lw, tn=D, tk=2048, outs=[BF16], name="mm_gwol")
    g_wos = _mm(ysgu.reshape(T, sw), dy_b, mode="tn", tm=sw, tn=D, tk=2048, outs=[BF16], nb=D // N_DEV,
                name="mm_gwos")
    (x_wol, x_wos), tok = send_grads([g_wol.reshape(N_DEV, lw // N_DEV, D), g_wos], "gx_wo")
    small_mix_b = (wconv_full, b_conv + tok[0, 0]) + small_mix[2:]
    (dproj, dbin_lo, g_wconv, g_bconv, g_wa, g_ba, g_wx, g_bx, g_lam, g_wsp, g_bsp_t, g_lvg, g_lvb) = _mix_bwd(
        proj3, hs, dya_pre.reshape(Bl, S, lw), dysgu.reshape(Bl, S, sw), dproj.reshape(Bl, S, din), lru_saved,
        *small_mix_b, tm=tmix, lw=lw, sw=sw)
    dproj2 = dproj.reshape(T, din)
    small_names = [n for n in SMALL_NAMES if n != "b_ada"]
    small_g = dict(b_in=jnp.concatenate([dbin_lo, dbin_hi], axis=-1), b_conv=g_bconv, w_rg_a=g_wa[None], b_rg_a=g_ba,
                   w_rg_x=g_wx[None], b_rg_x=g_bx, lru_lambda=g_lam, w_sp=g_wsp[None],
                   b_sp=jnp.transpose(g_bsp_t)[None], ln_v_g=g_lvg, ln_v_b=g_lvb, ln1_g=dg1, ln1_b=db1, ln2_g=dg2,
                   ln2_b=db2)
    gs_snd, gs_rcv, gs_src, gs_land, tok_s = _xstart([small_g[n] for n in small_names], True, None, "gsmall_start",
                                                      fill_own=False)
    g_win = _mm(h.reshape(T, D), dproj2, mode="tn", tm=D, tn=din // 4, tk=2048, outs=[BF16], nb=din // N_DEV,
                tok=tok_s, name="mm_gwin")
    (x_win,), tok = send_grads([g_win], "gx_win")

    def ep_final(dh, v):
        dxp_, x_, sc = v
        return [dxp_ + dh * (1.0 + sc), _colsum(dh * x_), _colsum(dh)]

    grad_x, dsc1, dsh1 = _mm_rows(dproj2, win_parts, mode="nt", tm=trow, seq=S, tok=tok,
                                  ins=[("tile", dxp), ("tile", x2d), ("brow", sc1)],
                                  outs=[("tile", F32, D), ("acc_brow", D), ("acc_brow", D)], epilogue=ep_final,
                                  name="mm_dh_final")
    grad_x = grad_x.reshape(Bl, S, D)

    out_g, out_d, out_m, out_v = {}, {}, {}, {}

    def adam(name, g_slots, tr, own=None):
        shp = W[name].shape
        w2, m2, v2 = [t.reshape(g_slots.shape[1:]) for t in (W[name], Mo[name], Vo[name])]
        g, d, mn, vn = _adamw(w2, g_slots, m2, v2, tr=tr, name="adam_" + name, own=own)
        out_g[name], out_d[name], out_m[name], out_v[name] = [t.reshape(shp) for t in (g, d, mn, vn)]

    def adam_exchanged(name, handle, tr, after):
        own, slots = _xwait(*handle, after, False, "gx_%s_wait" % name, place=False)
        adam(name, slots, tr, own=own)

    adam_exchanged("w_down", x_wdown, 256, dsh1)
    adam_exchanged("w_up", x_wup, 256, dsh1)
    adam_exchanged("w_out", x_wout, 128, dsh1)
    adam_exchanged("w_o_lru", x_wol, 160, dsh1)
    adam_exchanged("w_o_sgu", x_wos, 256, dsh1)
    gs_own, gs_slots = _xwait_many(gs_src, gs_land, gs_snd, gs_rcv, dsh1, "gsmall_wait")
    res_small = _adamw_many([W[n] for n in small_names], gs_slots, gs_own, [Mo[n] for n in small_names],
                            [Vo[n] for n in small_names], name="adam_small")
    for dst, vals in zip((out_g, out_d, out_m, out_v), res_small):
        dst.update(dict(zip(small_names, vals)))

    dmod = jnp.concatenate([dsh1, dsc1, dgt1, dsh2, dsc2, dgt2], axis=-1).reshape(Bl, 6 * D)
    dmod_b = _blocked_cols(jnp.pad(dmod, ((0, SUBLANES - Bl), (0, 0))))
    dmod_s, gwconv_s = _exchange([dmod_b, _blocked_cols(g_wconv)], False, "xchg_dmod", after=out_g["ln2_b"])
    g_wada, g_bada_mine = _ada_bwd(c_act, dmod_s.reshape(N_DEV * SUBLANES, -1))
    (g_bada_all,) = _exchange([g_bada_mine], True, "xchg_bada")
    adam("w_ada", g_wada[None], 256)
    adam("b_ada", g_bada_all.reshape(1, 1, 6 * D), 1)
    adam("w_conv", gwconv_s, 8)
    adam_exchanged("w_in", x_win, 256, g_bada_all)

    return (loss, grad_x, *[out_g[n] for n in WEIGHT_ORDER], *[out_d[n] for n in WEIGHT_ORDER],
            *[out_m[n] for n in WEIGHT_ORDER], *[out_v[n] for n in WEIGHT_ORDER])
```

```python
import math

import jax
import jax.numpy as jnp
from jax import lax
from jax.experimental import pallas as pl
from jax.experimental.pallas import tpu as pltpu

N_DEV = 8
LN_EPS = 1e-5
LRU_C = 8.0
CHUNK = 64
SGU_BLOCK = 128
ALPHA = 2.0 ** 0.25
ADAM_LR = 0.001
ADAM_B1 = 0.9
ADAM_B2 = 0.999
ADAM_EPS = 1e-08
ADAM_WD = 0.01
ADAM_STEP = 10
GELU_K0 = math.sqrt(2.0 / math.pi)
GELU_K1 = 0.044715

SUBLANES = 8
LANES = 128
VMEM_LIMIT = 56 * 1024 * 1024
WIN_PARTS = 3

F32 = jnp.float32
BF16 = jnp.bfloat16
MESH = pl.DeviceIdType.MESH


def _cparams(n_axes, big=False):
    return pltpu.CompilerParams(dimension_semantics=("arbitrary",) * n_axes,
                                vmem_limit_bytes=VMEM_LIMIT if big else None)


def _sigmoid(x):
    return 0.5 * jnp.tanh(0.5 * x) + 0.5


def _gelu(x):
    t = jnp.tanh(x * (GELU_K0 + (GELU_K0 * GELU_K1) * (x * x)))
    hx = 0.5 * x
    return hx + hx * t


def _gelu_and_grad(x):
    x2 = x * x
    t = jnp.tanh(x * (GELU_K0 + (GELU_K0 * GELU_K1) * x2))
    hx = 0.5 * x
    g = hx + hx * t
    dg = (0.5 + 0.5 * t) + (hx * (1.0 - t * t)) * (GELU_K0 + (3.0 * GELU_K0 * GELU_K1) * x2)
    return g, dg


def _log1p_pos(e):
    p = e * (1.0 - e * (1.0 / 2.0) + e * e * (1.0 / 3.0) - e * e * e * (1.0 / 4.0))
    return jnp.where(e < 1e-2, p, jnp.log(1.0 + e))


def _ln_stats(z):
    mu = jnp.mean(z, axis=-1, keepdims=True)
    zc = z - mu
    var = jnp.mean(zc * zc, axis=-1, keepdims=True)
    rstd = lax.rsqrt(var + LN_EPS)
    return zc * rstd, rstd


def _ln_bwd(dy, xhat, rstd, g):
    dxh = dy * g
    m1 = jnp.mean(dxh, axis=-1, keepdims=True)
    m2 = jnp.mean(dxh * xhat, axis=-1, keepdims=True)
    return rstd * (dxh - m1 - xhat * m2)


def _colsum(v):
    return jnp.sum(v, axis=0, keepdims=True)


def _first_step():
    return jnp.logical_and(pl.program_id(0) == 0, pl.program_id(1) == 0)


def _exchange(arrs, gather, name, after=None):
    n = len(arrs)
    n_peer = N_DEV - 1
    n_after = 0 if after is None else 1

    def body(*refs):
        ins, outs = refs[:n], refs[n + n_after:2 * n + n_after]
        send_sems, recv_sems, loc_sems = refs[2 * n + n_after:]
        x, y, c = lax.axis_index("x"), lax.axis_index("y"), lax.axis_index("c")
        me = 4 * x + 2 * y + c
        started = []
        for a in range(n):
            src_me = ins[a] if gather else ins[a].at[me]
            lc = pltpu.make_async_copy(src_me, outs[a].at[me], loc_sems.at[a])
            lc.start()
            started.append((lc, None))
        for p in range(1, N_DEV):
            px, py, pc = x ^ ((p >> 2) & 1), y ^ ((p >> 1) & 1), c ^ (p & 1)
            peer = 4 * px + 2 * py + pc
            for a in range(n):
                k = a * n_peer + (p - 1)
                src = ins[a] if gather else ins[a].at[peer]
                cp = pltpu.make_async_remote_copy(src_ref=src, dst_ref=outs[a].at[me],
                                                  send_sem=send_sems.at[k], recv_sem=recv_sems.at[k],
                                                  device_id=(px, py, pc), device_id_type=MESH)
                cp.start()
                rc = pltpu.make_async_remote_copy(src_ref=src, dst_ref=outs[a].at[peer],
                                                  send_sem=send_sems.at[k], recv_sem=recv_sems.at[k],
                                                  device_id=(px, py, pc), device_id_type=MESH)
                started.append((cp, rc))
        for cp, rc in started:
            if rc is None:
                cp.wait()
            else:
                cp.wait_send()
                rc.wait_recv()

    hbm = pl.BlockSpec(memory_space=pltpu.HBM)
    out_shape = tuple(
        jax.ShapeDtypeStruct(((N_DEV,) + a.shape) if gather else a.shape, a.dtype) for a in arrs)
    return pl.pallas_call(
        body, name=name, out_shape=out_shape,
        in_specs=[hbm] * n + [pl.BlockSpec(memory_space=pl.ANY)] * n_after, out_specs=tuple([hbm] * n),
        scratch_shapes=[pltpu.SemaphoreType.DMA((n * n_peer,)), pltpu.SemaphoreType.DMA((n * n_peer,)),
                        pltpu.SemaphoreType.DMA((n,))],
        compiler_params=pltpu.CompilerParams(has_side_effects=True),
    )(*arrs, *([after] if n_after else []))


_HBM = pl.BlockSpec(memory_space=pltpu.HBM)
_SEM = pl.BlockSpec(memory_space=pltpu.SEMAPHORE)
_EFFECT = pltpu.SideEffectType.DATAFLOW_SIDE_EFFECTING


def _peer_of(p):
    x, y, c = lax.axis_index("x"), lax.axis_index("y"), lax.axis_index("c")
    px, py, pc = x ^ ((p >> 2) & 1), y ^ ((p >> 1) & 1), c ^ (p & 1)
    return (px, py, pc), 4 * px + 2 * py + pc


def _slot(land_ref, idx, width):
    if width is None:
        return land_ref.at[idx]
    return land_ref.at[:, pl.ds(pl.multiple_of(idx * width, LANES), width)]


def _xstart(srcs, gather, after, name, cols=None, fill_own=False):
    n = len(srcs)
    cols = cols or [False] * n
    widths = [t.shape[1] if cols[a] else None for a, t in enumerate(srcs)]
    me_out = 4 * lax.axis_index("x") + 2 * lax.axis_index("y") + lax.axis_index("c")
    lands = []
    for a, t in enumerate(srcs):
        if cols[a]:
            zone, own, at = lax.empty((t.shape[0], N_DEV * t.shape[1]), t.dtype), t, (0, me_out * t.shape[1])
        elif gather:
            zone, own, at = lax.empty((N_DEV,) + t.shape, t.dtype), t[None], (me_out,) + (0,) * t.ndim
        else:
            zone, own = lax.empty(t.shape, t.dtype), lax.dynamic_index_in_dim(t, me_out, 0, keepdims=True)
            at = (me_out,) + (0,) * (t.ndim - 1)
        lands.append(lax.dynamic_update_slice(zone, own, at) if fill_own else zone)
    n_after = 0 if after is None else 1

    def body(*refs):
        src_refs, land_refs = refs[:n], refs[n:2 * n]
        refs = refs[n_after:]
        send_sems, recv_sems = refs[2 * n:3 * n], refs[3 * n:4 * n]
        token = refs[6 * n]
        me = 4 * lax.axis_index("x") + 2 * lax.axis_index("y") + lax.axis_index("c")
        for a in range(n):
            for p in range(1, N_DEV):
                dev, peer = _peer_of(p)
                pltpu.make_async_remote_copy(
                    src_ref=src_refs[a] if gather else src_refs[a].at[peer], dst_ref=_slot(land_refs[a], me, widths[a]),
                    send_sem=send_sems[a].at[p - 1], recv_sem=recv_sems[a].at[p - 1],
                    device_id=dev, device_id_type=MESH).start()
        token[...] = jnp.zeros_like(token)

    sems = tuple(pltpu.SemaphoreType.DMA((N_DEV - 1,)) for _ in range(2 * n))
    thru = tuple(pltpu.HBM(t.shape, t.dtype) for t in list(srcs) + list(lands))
    res = pl.pallas_call(
        body, name=name,
        out_shape=sems + thru + (jax.ShapeDtypeStruct((SUBLANES, LANES), F32),),
        in_specs=[_HBM] * (2 * n) + [pl.BlockSpec(memory_space=pl.ANY)] * n_after,
        out_specs=tuple([_SEM] * (2 * n) + [_HBM] * (2 * n) + [pl.BlockSpec(memory_space=pltpu.VMEM)]),
        input_output_aliases={i: 2 * n + i for i in range(2 * n)},
        compiler_params=pltpu.CompilerParams(has_side_effects=_EFFECT),
    )(*[pltpu.with_memory_space_constraint(t, pltpu.HBM) for t in list(srcs) + list(lands)],
      *([after] if n_after else []))
    return res[:n], res[n:2 * n], res[2 * n:3 * n], res[3 * n:4 * n], res[4 * n]


def _xwait(src, land, send_sem, recv_sem, after, gather, name, col=False, place=True):
    width = src.shape[1] if col else None

    def body(src_ref, land_ref, send_ref, recv_ref, after_ref, src_dead, land_out):
        del after_ref, src_dead, land_out
        for p in range(1, N_DEV):
            dev, peer = _peer_of(p)
            cp = pltpu.make_async_remote_copy(
                src_ref=src_ref if gather else src_ref.at[peer], dst_ref=_slot(land_ref, peer, width),
                send_sem=send_ref.at[p - 1], recv_sem=recv_ref.at[p - 1], device_id=dev, device_id_type=MESH)
            cp.wait_send()
            cp.wait_recv()

    src_done, landed = pl.pallas_call(
        body, name=name, out_shape=(pltpu.HBM(src.shape, src.dtype), pltpu.HBM(land.shape, land.dtype)),
        in_specs=[_HBM, _HBM, _SEM, _SEM, pl.BlockSpec(memory_space=pl.ANY)], out_specs=(_HBM, _HBM),
        input_output_aliases={0: 0, 1: 1},
        compiler_params=pltpu.CompilerParams(has_side_effects=_EFFECT),
    )(src, land, send_sem, recv_sem, after)
    if not place:
        return src_done, landed
    me = 4 * lax.axis_index("x") + 2 * lax.axis_index("y") + lax.axis_index("c")
    return _place_own(landed, src_done, me, col, gather, name + "_own")


def _place_own(zone, src, me, col, gather, name):
    if col:
        R, C = src.shape
        src_spec = lambda tr: pl.BlockSpec((tr, C), lambda i, me_ref: (i, 0))
        out_spec = lambda tr: pl.BlockSpec((tr, C), lambda i, me_ref: (i, me_ref[0]))
    else:
        R, C = zone.shape[1:]
        src_spec = ((lambda tr: pl.BlockSpec((tr, C), lambda i, me_ref: (i, 0))) if gather else
                    (lambda tr: pl.BlockSpec((None, tr, C), lambda i, me_ref: (me_ref[0], i, 0))))
        out_spec = lambda tr: pl.BlockSpec((None, tr, C), lambda i, me_ref: (me_ref[0], i, 0))
    tr = R if R <= 512 else 256
    assert R % tr == 0, (name, R, tr)

    def body(me_ref, src_ref, zone_ref, out_ref):
        del me_ref, zone_ref
        out_ref[...] = src_ref[...]

    return pl.pallas_call(
        body, name=name, out_shape=jax.ShapeDtypeStruct(zone.shape, zone.dtype),
        grid_spec=pltpu.PrefetchScalarGridSpec(
            num_scalar_prefetch=1, grid=(R // tr,),
            in_specs=[src_spec(tr), pl.BlockSpec(memory_space=pl.ANY)], out_specs=out_spec(tr)),
        input_output_aliases={2: 0},
    )(jnp.reshape(me, (1,)).astype(jnp.int32), src, zone)


def _xwait_many(srcs, lands, send_sems, recv_sems, after, name):
    n = len(srcs)

    def body(*refs):
        src_refs, land_refs = refs[:n], refs[n:2 * n]
        snd, rcv = refs[2 * n:3 * n], refs[3 * n:4 * n]
        for a in range(n):
            for p in range(1, N_DEV):
                dev, peer = _peer_of(p)
                cp = pltpu.make_async_remote_copy(
                    src_ref=src_refs[a], dst_ref=land_refs[a].at[peer], send_sem=snd[a].at[p - 1],
                    recv_sem=rcv[a].at[p - 1], device_id=dev, device_id_type=MESH)
                cp.wait_send()
                cp.wait_recv()

    res = pl.pallas_call(
        body, name=name, out_shape=tuple(pltpu.HBM(t.shape, t.dtype) for t in list(srcs) + list(lands)),
        in_specs=[_HBM] * (2 * n) + [_SEM] * (2 * n) + [pl.BlockSpec(memory_space=pl.ANY)],
        out_specs=tuple([_HBM] * (2 * n)), input_output_aliases={i: i for i in range(2 * n)},
        compiler_params=pltpu.CompilerParams(has_side_effects=_EFFECT),
    )(*srcs, *lands, *send_sems, *recv_sems, after)
    return res[:n], res[n:]


def _mm(a, b, *, mode, tm, tn, tk, outs, epilogue=None, extras=(), nb=None, tok=None, scatter=None, into=None, name):
    if mode == "nn":
        (M, K), (_, N) = a.shape, b.shape
    elif mode == "nt":
        (M, K), (N, _) = a.shape, b.shape
    else:
        (K, M), (_, N) = a.shape, b.shape
    tm, tn, tk = min(tm, M), min(tn, N), min(tk, K)
    assert M % tm == 0 and N % tn == 0 and K % tk == 0, (name, M, N, K, tm, tn, tk)
    if mode == "nn":
        a_spec = pl.BlockSpec((tm, tk), lambda i, j, k: (i, k))
        b_spec = pl.BlockSpec((tk, tn), lambda i, j, k: (k, j))
        dims = (((1,), (0,)), ((), ()))
    elif mode == "nt":
        a_spec = pl.BlockSpec((tm, tk), lambda i, j, k: (i, k))
        b_spec = pl.BlockSpec((tn, tk), lambda i, j, k: (j, k))
        dims = (((1,), (1,)), ((), ()))
    else:
        a_spec = pl.BlockSpec((tk, tm), lambda i, j, k: (k, i))
        b_spec = pl.BlockSpec((tk, tn), lambda i, j, k: (k, j))
        dims = (((0,), (0,)), ((), ()))
    nk = K // tk
    n_ex, n_out = len(extras), len(outs)
    n_tok = 0 if tok is None else 1
    nbytes = lambda d: jnp.dtype(d).itemsize
    vmem_est = (2 * (tm * tk * nbytes(a.dtype) + tk * tn * nbytes(b.dtype)
                     + sum(tm * tn * nbytes(e.dtype) for e, kind in extras if kind == "tile")
                     + sum(tm * tn * nbytes(d) for d in outs)) + tm * tn * 4)
    assert vmem_est <= VMEM_LIMIT, (name, vmem_est)
    if epilogue is None:
        epilogue = lambda acc, ex: tuple(acc.astype(d) for d in outs)

    n_into = 0 if into is None else 1

    def body(a_ref, b_ref, *refs):
        refs = refs[n_tok:]
        ex_refs, out_refs = refs[:n_ex], refs[n_ex + n_into:n_ex + n_into + n_out]

        def finish(acc):
            res = epilogue(acc, [r[...] for r in ex_refs])
            for o_ref, v in zip(out_refs, res):
                if nb is None:
                    o_ref[...] = v.astype(o_ref.dtype)
                else:
                    for q in range(tn // nb):
                        o_ref[q] = v[:, q * nb:(q + 1) * nb].astype(o_ref.dtype)

        part = lax.dot_general(a_ref[...], b_ref[...], dims, preferred_element_type=F32)
        if nk == 1:
            finish(part)
        else:
            acc_ref = refs[n_ex + n_into + n_out]
            k = pl.program_id(2)

            @pl.when(k == 0)
            def _():
                acc_ref[...] = part

            @pl.when(k > 0)
            def _():
                acc_ref[...] += part

            @pl.when(k == nk - 1)
            def _():
                finish(acc_ref[...])

    col = (lambda j: j) if scatter is None else (lambda j: scatter[0] * j + scatter[1])
    ex_specs = [pl.BlockSpec((tm, tn), lambda i, j, k: (i, j)) if kind == "tile"
                else pl.BlockSpec((1, tn), lambda i, j, k: (0, col(j))) for _, kind in extras]
    if nb is not None:
        assert tn % nb == 0, (name, tn, nb)
        o_spec = pl.BlockSpec((tn // nb, tm, nb), lambda i, j, k: (j, i, 0))
        o_shape = (N // nb, M, nb)
    else:
        o_spec = pl.BlockSpec((tm, tn), lambda i, j, k: (i, col(j)))
        o_shape = (M, N if scatter is None else scatter[2])
    assert n_into == 0 or n_out == 1
    res = pl.pallas_call(
        body, name=name, grid=(M // tm, N // tn, nk),
        in_specs=[a_spec, b_spec] + [pl.BlockSpec((SUBLANES, LANES), lambda i, j, k: (0, 0))] * n_tok + ex_specs
                 + [pl.BlockSpec(memory_space=pl.ANY)] * n_into,
        out_specs=tuple([o_spec] * n_out),
        out_shape=tuple(jax.ShapeDtypeStruct(o_shape, d) for d in outs),
        input_output_aliases={2 + n_tok + n_ex: 0} if n_into else {},
        scratch_shapes=[pltpu.VMEM((tm, tn), F32)] if nk > 1 else [],
        compiler_params=_cparams(3, big=True),
    )(a, b, *([tok] if n_tok else []), *[e for e, _ in extras], *([into] if n_into else []))
    return res[0] if n_out == 1 else res


def _mm_rows(a, b, *, mode, tm, seq, ins, outs, epilogue, tok=None, name):
    M, K = a.shape
    b_parts = list(b) if isinstance(b, (list, tuple)) else [b]
    n_part = len(b_parts)
    assert n_part == 1 or mode == "nt"
    N = b_parts[0].shape[1] if mode == "nn" else b_parts[0].shape[0]
    tm = min(tm, M)
    assert M % tm == 0 and seq % tm == 0, (name, M, seq, tm)
    tpb = seq // tm
    n_b = M // seq
    dims = (((1,), (0,)), ((), ())) if mode == "nn" else (((1,), (1,)), ((), ()))
    n_tok = 0 if tok is None else 1
    n_in, n_out = len(ins), len(outs)

    in_specs, in_arrs = [], []
    for spec in ins:
        kind, arr = spec[0], spec[1]
        in_arrs.append(arr)
        if kind == "tile":
            in_specs.append(pl.BlockSpec((tm, arr.shape[1]), lambda i: (i, 0)))
        elif kind == "tilecol":
            in_specs.append(pl.BlockSpec((tm, spec[2]), lambda i, cb=spec[3]: (i, cb)))
        elif kind == "row":
            in_specs.append(pl.BlockSpec(arr.shape, lambda i: (0, 0)))
        else:
            in_specs.append(pl.BlockSpec((None, 1, arr.shape[2]), lambda i: (i // tpb, 0, 0)))
    out_specs, out_shapes = [], []
    for spec in outs:
        kind = spec[0]
        if kind == "tile":
            out_specs.append(pl.BlockSpec((tm, spec[2]), lambda i: (i, 0)))
            out_shapes.append(jax.ShapeDtypeStruct((M, spec[2]), spec[1]))
        elif kind == "tilecol":
            out_specs.append(pl.BlockSpec((tm, spec[2]), lambda i, cb=spec[3]: (i, cb)))
            out_shapes.append(jax.ShapeDtypeStruct((M, spec[4]), spec[1]))
        elif kind == "acc_row":
            out_specs.append(pl.BlockSpec((1, spec[1]), lambda i: (0, 0)))
            out_shapes.append(jax.ShapeDtypeStruct((1, spec[1]), F32))
        elif kind == "acc_brow":
            out_specs.append(pl.BlockSpec((None, 1, spec[1]), lambda i: (i // tpb, 0, 0)))
            out_shapes.append(jax.ShapeDtypeStruct((n_b, 1, spec[1]), F32))
        else:
            out_specs.append(pl.BlockSpec((SUBLANES, LANES), lambda i: (0, 0)))
            out_shapes.append(jax.ShapeDtypeStruct((SUBLANES, LANES), F32))

    def body(a_ref, *refs):
        b_refs, refs = refs[:n_part], refs[n_part + n_tok:]
        in_refs, out_refs = refs[:n_in], refs[n_in:n_in + n_out]
        i = pl.program_id(0)
        if n_part == 1:
            prod = lax.dot_general(a_ref[...], b_refs[0][...], dims, preferred_element_type=F32)
        else:
            w = b_parts[0].shape[1] // N_DEV
            prod = None
            for q in range(n_part):
                a_q = jnp.concatenate([a_ref[:, (n_part * j + q) * w:(n_part * j + q + 1) * w] for j in range(N_DEV)],
                                      axis=1)
                pq = lax.dot_general(a_q, b_refs[q][...], dims, preferred_element_type=F32)
                prod = pq if prod is None else prod + pq
        vals = epilogue(prod, [r[...] for r in in_refs])
        for spec, o_ref, v in zip(outs, out_refs, vals):
            kind = spec[0]
            if kind in ("tile", "tilecol"):
                o_ref[...] = v.astype(o_ref.dtype)
            else:
                first = (i % tpb == 0) if kind == "acc_brow" else (i == 0)

                @pl.when(first)
                def _(o_ref=o_ref, v=v):
                    o_ref[...] = jnp.broadcast_to(v, o_ref.shape)

                @pl.when(jnp.logical_not(first))
                def _(o_ref=o_ref, v=v):
                    o_ref[...] += v

    res = pl.pallas_call(
        body, name=name, grid=(M // tm,),
        in_specs=[pl.BlockSpec((tm, K), lambda i: (i, 0))]
                 + [pl.BlockSpec(bp.shape, lambda i: (0, 0), pipeline_mode=pl.Buffered(1)) for bp in b_parts]
                 + [pl.BlockSpec((SUBLANES, LANES), lambda i: (0, 0))] * n_tok + in_specs,
        out_specs=tuple(out_specs), out_shape=tuple(out_shapes),
        compiler_params=_cparams(1, big=True),
    )(a, *b_parts, *([tok] if n_tok else []), *in_arrs)
    return res


def _tok_spec(ts, width, col_block=0):
    return pl.BlockSpec((None, ts, width), lambda b, s: (b, s, col_block))


def _brow_spec(width):
    return pl.BlockSpec((None, 1, width), lambda b, s: (b, 0, 0))


def _modulate(x, sc, sh, ts):
    Bl, S, D = x.shape

    def body(x_ref, sc_ref, sh_ref, o_ref):
        o_ref[...] = (x_ref[...] * (1.0 + sc_ref[...]) + sh_ref[...]).astype(BF16)

    return pl.pallas_call(
        body, name="modulate", grid=(Bl, S // ts),
        in_specs=[_tok_spec(ts, D), _brow_spec(D), _brow_spec(D)],
        out_specs=_tok_spec(ts, D), out_shape=jax.ShapeDtypeStruct((Bl, S, D), BF16),
        compiler_params=_cparams(2),
    )(x, sc, sh)


def _mix_fwd(proj, w_conv, b_conv, w_rg_a, b_rg_a, w_rg_x, b_rg_x, lam, w_sp, b_sp_t, ln_v_g, ln_v_b, *, tm, lw, sw):
    Bl, S, _ = proj.shape
    heads, hd = w_rg_a.shape[0], w_rg_a.shape[1]
    groups = w_sp.shape[0]
    cw = 2 * lw + 2 * sw
    nblk = tm // SGU_BLOCK

    G = tm // SUBLANES
    nc = lw // LANES

    def body(p_ref, wc_ref, bc_ref, wa_ref, ba_ref, wx_ref, bx_ref, lam_ref, wsp_ref, bsp_ref, lg_ref, lb_ref,
             hs_ref, ya_ref, ys_ref, xc_ref, r_ref, ig_ref, a_ref, m_ref,
             xext, hnat, hcar, h7_scr, a7_scr, hp_scr):
        s = pl.program_id(1)

        @pl.when(s == 0)
        def _():
            xext[:, 0:SUBLANES, :] = jnp.zeros((nc, SUBLANES, LANES), F32)
            hcar[...] = jnp.zeros_like(hcar)

        @pl.when(s > 0)
        def _():
            xext[:, 0:SUBLANES, :] = xext[:, tm:tm + SUBLANES, :]

        for c in range(nc):
            xext[c, SUBLANES:SUBLANES + tm, :] = p_ref[:, c * LANES:(c + 1) * LANES].astype(F32)
        gl = p_ref[:, lw:2 * lw].astype(F32)

        def slab(ref3, start):
            return jnp.concatenate([ref3[c, pl.ds(start, G, stride=SUBLANES), :] for c in range(nc)], axis=1)

        xs = {st: slab(xext, st) for st in range(SUBLANES - 3, 2 * SUBLANES)}
        xc_slabs = []
        for j in range(SUBLANES):
            acc = bc_ref[...] + xs[SUBLANES + j] * wc_ref[3:4, :]
            for k in (1, 2, 3):
                acc = acc + xs[SUBLANES + j - k] * wc_ref[3 - k:4 - k, :]
            xc_slabs.append(acc)
        xc = jnp.concatenate(xc_slabs, axis=0)

        xcb = xc.astype(BF16)
        pa = jnp.concatenate([jnp.dot(xcb[:, h * hd:(h + 1) * hd], wa_ref[h], preferred_element_type=F32)
                              for h in range(heads)], axis=1) + ba_ref[...]
        px = jnp.concatenate([jnp.dot(xcb[:, h * hd:(h + 1) * hd], wx_ref[h], preferred_element_type=F32)
                              for h in range(heads)], axis=1) + bx_ref[...]
        r = _sigmoid(pa)
        ig = _sigmoid(px)
        nl = -lam_ref[...]
        big_l = -LRU_C * (jnp.maximum(nl, 0.0) + _log1p_pos(jnp.exp(-jnp.abs(nl))))
        la = big_l * r
        a = jnp.exp(la)
        th = jnp.tanh(la)
        msq = (-2.0 * th) / (1.0 - th)
        m = msq * lax.rsqrt(jnp.maximum(msq, 1e-30))
        bin_ = m * (ig * xc)
        xc_ref[...] = xc
        r_ref[...] = r
        ig_ref[...] = ig
        a_ref[...] = a
        m_ref[...] = m

        h0 = [bin_[0:G]]
        cp = [a[0:G]]
        for j in range(1, SUBLANES):
            aj = a[j * G:(j + 1) * G]
            h0.append(aj * h0[j - 1] + bin_[j * G:(j + 1) * G])
            cp.append(aj * cp[j - 1])
        h7_scr[...] = h0[SUBLANES - 1]
        a7_scr[...] = cp[SUBLANES - 1]
        carry = hcar[0:1, :]
        for g in range(G):
            hp_scr[g:g + 1, :] = carry
            carry = h7_scr[g:g + 1, :] + a7_scr[g:g + 1, :] * carry
        hcar[0:1, :] = carry
        hprev = hp_scr[...]
        for j in range(SUBLANES):
            hj = h0[j] + cp[j] * hprev
            for c in range(nc):
                hnat[c, pl.ds(j, G, stride=SUBLANES), :] = hj[:, c * LANES:(c + 1) * LANES]
        hs = jnp.concatenate([hnat[c] for c in range(nc)], axis=1)
        hs_ref[...] = hs
        ya_ref[...] = (hs * _gelu(gl)).astype(BF16)

        gu = _gelu(p_ref[:, 2 * lw:2 * lw + sw].astype(F32))
        gv = _gelu(p_ref[:, 2 * lw + sw:cw].astype(F32))
        xhat, _ = _ln_stats(gv)
        vn = (xhat * lg_ref[...] + lb_ref[...]).astype(BF16)
        tpos = lax.broadcasted_iota(jnp.int32, (SGU_BLOCK, SGU_BLOCK), 0) // CHUNK
        spos = lax.broadcasted_iota(jnp.int32, (SGU_BLOCK, SGU_BLOCK), 1) // CHUNK
        gw = sw // groups
        rows_out = []
        for blk in range(nblk):
            r0 = blk * SGU_BLOCK
            cols = []
            for g in range(groups):
                wm = jnp.where(spos <= tpos, wsp_ref[g], 0.0).astype(BF16)
                mixed = jnp.dot(wm, vn[r0:r0 + SGU_BLOCK, g * gw:(g + 1) * gw], preferred_element_type=F32)
                cols.append(mixed + bsp_ref[:, g:g + 1])
            rows_out.append(jnp.concatenate(cols, axis=1))
        mixed_all = jnp.concatenate(rows_out, axis=0) if nblk > 1 else rows_out[0]
        ys_ref[...] = (gu * mixed_all).astype(BF16)

    full = lambda shp: pl.BlockSpec(shp, lambda b, s: (0,) * len(shp))
    return pl.pallas_call(
        body, name="mix_fwd", grid=(Bl, S // tm),
        in_specs=[_tok_spec(tm, cw), full(w_conv.shape), full(b_conv.shape), full(w_rg_a.shape), full(b_rg_a.shape),
                  full(w_rg_x.shape), full(b_rg_x.shape), full(lam.shape), full(w_sp.shape), full(b_sp_t.shape),
                  full(ln_v_g.shape), full(ln_v_b.shape)],
        out_specs=(_tok_spec(tm, lw), _tok_spec(tm, lw), _tok_spec(tm, sw)) + (_tok_spec(tm, lw),) * 5,
        out_shape=(jax.ShapeDtypeStruct((Bl, S, lw), F32), jax.ShapeDtypeStruct((Bl, S, lw), BF16),
                   jax.ShapeDtypeStruct((Bl, S, sw), BF16)) + (jax.ShapeDtypeStruct((Bl, S, lw), F32),) * 5,
        scratch_shapes=[pltpu.VMEM((nc, tm + SUBLANES, LANES), F32), pltpu.VMEM((nc, tm, LANES), F32),
                        pltpu.VMEM((SUBLANES, lw), F32), pltpu.VMEM((G, lw), F32), pltpu.VMEM((G, lw), F32),
                        pltpu.VMEM((G, lw), F32)],
        compiler_params=_cparams(2, big=True),
    )(proj, w_conv, b_conv, w_rg_a, b_rg_a, w_rg_x, b_rg_x, lam, w_sp, b_sp_t, ln_v_g, ln_v_b)


def _mix_bwd(proj, hs, dya, dys, dproj, saved, w_conv, b_conv, w_rg_a, b_rg_a, w_rg_x, b_rg_x, lam, w_sp, b_sp_t,
             ln_v_g, ln_v_b, *, tm, lw, sw):
    Bl, S, din = proj.shape
    heads, hd = w_rg_a.shape[0], w_rg_a.shape[1]
    groups = w_sp.shape[0]
    gw = sw // groups
    cw = 2 * lw + 2 * sw
    nblk = tm // SGU_BLOCK
    n_s = S // tm
    per8 = tm // SUBLANES
    halo_rows = 2 * SUBLANES

    G = tm // SUBLANES
    nc = lw // LANES

    def body(p_ref, xh_ref, hs_ref, hh_ref, dya_ref, dys_ref, dpin_ref, xc_ref, r_ref, ig_ref, a_ref, m_ref,
             wc_ref, bc_ref, wa_ref, ba_ref, wx_ref, bx_ref, lam_ref, wsp_ref, bsp_ref, lg_ref, lb_ref,
             dp_ref, dbin_ref, dwc_ref, dbc_ref, dwa_ref, dba_ref, dwx_ref, dbx_ref, dlam_ref, dwsp_ref, dbsp_ref,
             dlg_ref, dlb_ref,
             xext, hext, dnat, dxext, dhcar, g00_scr, p0_scr, a0_scr, cin_scr):
        del dpin_ref
        sr = pl.program_id(1)
        first_tile = sr == n_s - 1

        @pl.when(_first_step())
        def _():
            for ref in (dbin_ref, dwc_ref, dbc_ref, dwa_ref, dba_ref, dwx_ref, dbx_ref, dlam_ref, dwsp_ref, dbsp_ref,
                        dlg_ref, dlb_ref):
                ref[...] = jnp.zeros_like(ref)

        @pl.when(sr == 0)
        def _():
            dhcar[...] = jnp.zeros_like(dhcar)
            dxext[:, tm:tm + SUBLANES, :] = jnp.zeros((nc, SUBLANES, LANES), F32)

        @pl.when(sr > 0)
        def _():
            dxext[:, tm:tm + SUBLANES, :] = dxext[:, 0:SUBLANES, :]

        def slab(ref3, start):
            return jnp.concatenate([ref3[c, pl.ds(start, G, stride=SUBLANES), :] for c in range(nc)], axis=1)

        def put_slab(ref3, j, val):
            for c in range(nc):
                ref3[c, pl.ds(j, G, stride=SUBLANES), :] = val[:, c * LANES:(c + 1) * LANES]

        keep = jnp.where(first_tile, 0.0, 1.0)
        xprev = xh_ref[...].astype(F32)[halo_rows - SUBLANES:halo_rows] * keep
        hsv = hs_ref[...]
        hprev8 = hh_ref[...] * keep
        for c in range(nc):
            cs = slice(c * LANES, (c + 1) * LANES)
            xext[c, 0:SUBLANES, :] = xprev[:, cs]
            xext[c, SUBLANES:SUBLANES + tm, :] = p_ref[:, cs].astype(F32)
            hext[c, 0:SUBLANES, :] = hprev8[:, cs]
            hext[c, SUBLANES:SUBLANES + tm, :] = hsv[:, cs]
        gl = p_ref[:, lw:2 * lw].astype(F32)
        ggl, dggl = _gelu_and_grad(gl)
        dyav = dya_ref[...].astype(F32)
        dhs = dyav * ggl
        dgl = dyav * hsv * dggl
        dp_ref[:, lw:2 * lw] = dgl.astype(BF16)
        dbin_ref[:, lw:2 * lw] += _colsum(dgl)
        for c in range(nc):
            dnat[c] = dhs[:, c * LANES:(c + 1) * LANES]

        xc, r, ig, a, m = xc_ref[...], r_ref[...], ig_ref[...], a_ref[...], m_ref[...]
        xcb = xc.astype(BF16)
        nl = -lam_ref[...]
        big_l = -LRU_C * (jnp.maximum(nl, 0.0) + _log1p_pos(jnp.exp(-jnp.abs(nl))))

        g0 = [None] * SUBLANES
        pp = [None] * SUBLANES
        g0[SUBLANES - 1] = slab(dnat, SUBLANES - 1)
        for j in range(SUBLANES - 2, -1, -1):
            an = a[(j + 1) * G:(j + 2) * G]
            g0[j] = slab(dnat, j) + an * g0[j + 1]
            pp[j] = an if j == SUBLANES - 2 else an * pp[j + 1]
        g00_scr[...] = g0[0]
        p0_scr[...] = pp[0]
        a0_scr[...] = a[0:G]
        cin = dhcar[0:1, :]
        for g in range(G - 1, -1, -1):
            cin_scr[g:g + 1, :] = cin
            cin = a0_scr[g:g + 1, :] * (g00_scr[g:g + 1, :] + p0_scr[g:g + 1, :] * cin)
        dhcar[0:1, :] = cin
        cinv = cin_scr[...]
        dh = jnp.concatenate([g0[j] + pp[j] * cinv for j in range(SUBLANES - 1)] + [g0[SUBLANES - 1] + cinv], axis=0)

        hprev = jnp.concatenate([slab(hext, SUBLANES - 1 + j) for j in range(SUBLANES)], axis=0)
        da = dh * hprev
        ixc = ig * xc
        dm = dh * ixc
        dixc = dh * m
        di = dixc * xc
        dxc = dixc * ig
        dla = da * a - dm * (a * a) / m
        dlam_ref[...] += _colsum(dla * r) * (LRU_C * _sigmoid(nl))
        dr = dla * big_l
        dpa = dr * r * (1.0 - r)
        dpx = di * ig * (1.0 - ig)
        dba_ref[...] += _colsum(dpa)
        dbx_ref[...] += _colsum(dpx)
        dpab = dpa.astype(BF16)
        dpxb = dpx.astype(BF16)
        nt = (((1,), (1,)), ((), ()))
        tn = (((0,), (0,)), ((), ()))
        dxc_g = []
        for h in range(heads):
            sl = slice(h * hd, (h + 1) * hd)
            dxc_g.append(lax.dot_general(dpab[:, sl], wa_ref[h], nt, preferred_element_type=F32)
                         + lax.dot_general(dpxb[:, sl], wx_ref[h], nt, preferred_element_type=F32))
            dwa_ref[h] += lax.dot_general(xcb[:, sl], dpab[:, sl], tn, preferred_element_type=F32)
            dwx_ref[h] += lax.dot_general(xcb[:, sl], dpxb[:, sl], tn, preferred_element_type=F32)
        dxc = dxc + jnp.concatenate(dxc_g, axis=1)

        dbc_ref[...] += _colsum(dxc)
        xs = {st: slab(xext, st) for st in range(SUBLANES - 3, 2 * SUBLANES)}
        for k in range(4):
            xsh = jnp.concatenate([xs[SUBLANES + j - (3 - k)] for j in range(SUBLANES)], axis=0)
            dwc_ref[k:k + 1, :] += _colsum(dxc * xsh)
        for j in range(SUBLANES):
            put_slab(dxext, j, dxc[j * G:(j + 1) * G])
        us = {st: slab(dxext, st) for st in range(SUBLANES + 3)}
        for j in range(SUBLANES):
            acc = us[j] * wc_ref[3:4, :]
            for k in (1, 2, 3):
                acc = acc + us[j + k] * wc_ref[3 - k:4 - k, :]
            put_slab(dnat, j, acc)
        dxl = jnp.concatenate([dnat[c] for c in range(nc)], axis=1)
        dp_ref[:, 0:lw] = dxl.astype(BF16)
        dbin_ref[:, 0:lw] += _colsum(dxl)

        gu, dgu_dx = _gelu_and_grad(p_ref[:, 2 * lw:2 * lw + sw].astype(F32))
        gv, dgv_dx = _gelu_and_grad(p_ref[:, 2 * lw + sw:cw].astype(F32))
        xhat, rstd = _ln_stats(gv)
        vn = (xhat * lg_ref[...] + lb_ref[...]).astype(BF16)
        dys = dys_ref[...].astype(F32)
        dmixed = dys * gu
        dmb = dmixed.astype(BF16)
        tpos = lax.broadcasted_iota(jnp.int32, (SGU_BLOCK, SGU_BLOCK), 0) // CHUNK
        spos = lax.broadcasted_iota(jnp.int32, (SGU_BLOCK, SGU_BLOCK), 1) // CHUNK
        causal = spos <= tpos
        mixed_rows, dvn_rows = [], []
        for blk in range(nblk):
            rs = slice(blk * SGU_BLOCK, (blk + 1) * SGU_BLOCK)
            mcols, dcols = [], []
            for g in range(groups):
                cs = slice(g * gw, (g + 1) * gw)
                wm = jnp.where(causal, wsp_ref[g], 0.0).astype(BF16)
                mcols.append(jnp.dot(wm, vn[rs, cs], preferred_element_type=F32) + bsp_ref[:, g:g + 1])
                dcols.append(lax.dot_general(wm, dmb[rs, cs], tn, preferred_element_type=F32))
                dw = lax.dot_general(dmb[rs, cs], vn[rs, cs], nt, preferred_element_type=F32)
                dwsp_ref[g] += jnp.where(causal, dw, 0.0)
                dbsp_ref[:, g:g + 1] += jnp.sum(dmixed[rs, cs], axis=1, keepdims=True)
            mixed_rows.append(jnp.concatenate(mcols, axis=1))
            dvn_rows.append(jnp.concatenate(dcols, axis=1))
        mixed_all = jnp.concatenate(mixed_rows, axis=0) if nblk > 1 else mixed_rows[0]
        dvn = jnp.concatenate(dvn_rows, axis=0) if nblk > 1 else dvn_rows[0]
        du = dys * mixed_all * dgu_dx
        dlg_ref[...] += _colsum(dvn * xhat)
        dlb_ref[...] += _colsum(dvn)
        dv = _ln_bwd(dvn, xhat, rstd, lg_ref[...]) * dgv_dx
        dp_ref[:, 2 * lw:2 * lw + sw] = du.astype(BF16)
        dp_ref[:, 2 * lw + sw:cw] = dv.astype(BF16)
        dbin_ref[:, 2 * lw:2 * lw + sw] += _colsum(du)
        dbin_ref[:, 2 * lw + sw:cw] += _colsum(dv)

    rev = lambda s: n_s - 1 - s
    tile = lambda w: pl.BlockSpec((None, tm, w), lambda b, s: (b, rev(s), 0))
    halo = lambda w: pl.BlockSpec((None, SUBLANES, w), lambda b, s: (b, jnp.maximum(rev(s) * per8 - 1, 0), 0))
    xhalo = pl.BlockSpec((None, halo_rows, lw), lambda b, s: (b, jnp.maximum(rev(s) * (tm // halo_rows) - 1, 0), 0))
    full = lambda shp: pl.BlockSpec(shp, lambda b, s: (0,) * len(shp))
    small = [w_conv, b_conv, w_rg_a, b_rg_a, w_rg_x, b_rg_x, lam, w_sp, b_sp_t, ln_v_g, ln_v_b]
    acc_shapes = [(1, cw), w_conv.shape, b_conv.shape, w_rg_a.shape, b_rg_a.shape, w_rg_x.shape, b_rg_x.shape,
                  lam.shape, w_sp.shape, b_sp_t.shape, ln_v_g.shape, ln_v_b.shape]
    res = pl.pallas_call(
        body, name="mix_bwd", grid=(Bl, n_s),
        in_specs=[tile(cw), xhalo, tile(lw), halo(lw), tile(lw), tile(sw), pl.BlockSpec(memory_space=pl.ANY)]
                 + [tile(lw)] * 5 + [full(w.shape) for w in small],
        out_specs=tuple([tile(cw)] + [full(shp) for shp in acc_shapes]),
        out_shape=tuple([jax.ShapeDtypeStruct((Bl, S, din), BF16)] + [jax.ShapeDtypeStruct(shp, F32) for shp in acc_shapes]),
        input_output_aliases={6: 0},
        scratch_shapes=[pltpu.VMEM((nc, tm + SUBLANES, LANES), F32), pltpu.VMEM((nc, tm + SUBLANES, LANES), F32),
                        pltpu.VMEM((nc, tm, LANES), F32), pltpu.VMEM((nc, tm + SUBLANES, LANES), F32),
                        pltpu.VMEM((SUBLANES, lw), F32), pltpu.VMEM((G, lw), F32), pltpu.VMEM((G, lw), F32),
                        pltpu.VMEM((G, lw), F32), pltpu.VMEM((G, lw), F32)],
        compiler_params=_cparams(2, big=True),
    )(proj, proj, hs, hs, dya, dys, dproj, *saved, *small)
    return res


def _ada_fwd(c_all, w_ada):
    R, D = c_all.shape
    nb = w_ada.shape[1]

    def body(c_ref, w_ref, act_ref, o_ref):
        cv = c_ref[...]
        act = (cv * _sigmoid(cv)).astype(BF16)
        act_ref[...] = act
        o_ref[...] = jnp.dot(act, w_ref[...].astype(BF16), preferred_element_type=F32)

    return pl.pallas_call(
        body, name="ada_fwd",
        out_shape=(jax.ShapeDtypeStruct((R, D), BF16), jax.ShapeDtypeStruct((R, nb), F32)),
        compiler_params=pltpu.CompilerParams(vmem_limit_bytes=VMEM_LIMIT),
    )(c_all, w_ada)


def _ada_bwd(c_act, dmod_cols):
    R, D = c_act.shape
    nb = dmod_cols.shape[1]

    def body(act_ref, d_ref, o_ref, b_ref):
        o_ref[...] = lax.dot_general(act_ref[...], d_ref[...].astype(BF16), (((0,), (0,)), ((), ())),
                                     preferred_element_type=F32)
        b_ref[...] = _colsum(d_ref[...])

    return pl.pallas_call(
        body, name="ada_bwd", out_shape=(jax.ShapeDtypeStruct((D, nb), F32), jax.ShapeDtypeStruct((1, nb), F32)),
        compiler_params=pltpu.CompilerParams(vmem_limit_bytes=VMEM_LIMIT),
    )(c_act, dmod_cols)


def _adamw(w, g_slots, m, v, *, tr, name, own=None):
    R, C = w.shape
    n_slot = g_slots.shape[0]
    tr = min(tr, R)
    assert R % tr == 0, (name, R, tr)
    c1 = 1.0 / (1.0 - ADAM_B1 ** ADAM_STEP)
    c2 = 1.0 / (1.0 - ADAM_B2 ** ADAM_STEP)
    n_own = 0 if own is None else 1

    def body(me_ref, w_ref, g_ref, *refs):
        m_ref, v_ref, go_ref, d_ref, mo_ref, vo_ref = refs[n_own:]
        slot = lambda d: (jnp.where(me_ref[0] == d, refs[0][...], g_ref[d]) if n_own else g_ref[d]).astype(F32)
        g = slot(0)
        for d in range(1, n_slot):
            g = g + slot(d)
        mn = ADAM_B1 * m_ref[...] + (1.0 - ADAM_B1) * g
        vn = ADAM_B2 * v_ref[...] + (1.0 - ADAM_B2) * (g * g)
        go_ref[...] = g
        mo_ref[...] = mn
        vo_ref[...] = vn
        d_ref[...] = -ADAM_LR * ((mn * c1) / (jnp.sqrt(vn * c2) + ADAM_EPS) + ADAM_WD * w_ref[...])

    me = 4 * lax.axis_index("x") + 2 * lax.axis_index("y") + lax.axis_index("c")
    blk = pl.BlockSpec((tr, C), lambda i, me_ref: (i, 0))
    own_specs = [pl.BlockSpec((None, tr, C), lambda i, me_ref: (me_ref[0], i, 0))] * n_own
    return pl.pallas_call(
        body, name=name, out_shape=tuple(jax.ShapeDtypeStruct((R, C), F32) for _ in range(4)),
        grid_spec=pltpu.PrefetchScalarGridSpec(
            num_scalar_prefetch=1, grid=(R // tr,),
            in_specs=[blk, pl.BlockSpec((n_slot, tr, C), lambda i, me_ref: (0, i, 0))] + own_specs + [blk, blk],
            out_specs=(blk, blk, blk, blk)),
        compiler_params=_cparams(1, big=True),
    )(jnp.reshape(me, (1,)).astype(jnp.int32), w, g_slots, *([own] if n_own else []), m, v)


def _adamw_many(ws, g_slots, g_owns, ms, vs, *, name):
    n = len(ws)
    c1 = 1.0 / (1.0 - ADAM_B1 ** ADAM_STEP)
    c2 = 1.0 / (1.0 - ADAM_B2 ** ADAM_STEP)

    def body(*refs):
        w_refs, g_refs, o_refs = refs[:n], refs[n:2 * n], refs[2 * n:3 * n]
        m_refs, v_refs = refs[3 * n:4 * n], refs[4 * n:5 * n]
        outs = refs[5 * n:]
        me = 4 * lax.axis_index("x") + 2 * lax.axis_index("y") + lax.axis_index("c")
        for i in range(n):
            own = o_refs[i][...]
            g = jnp.where(me == 0, own, g_refs[i][0])
            for d in range(1, N_DEV):
                g = g + jnp.where(me == d, own, g_refs[i][d])
            mn = ADAM_B1 * m_refs[i][...] + (1.0 - ADAM_B1) * g
            vn = ADAM_B2 * v_refs[i][...] + (1.0 - ADAM_B2) * (g * g)
            outs[i][...] = g
            outs[n + i][...] = -ADAM_LR * ((mn * c1) / (jnp.sqrt(vn * c2) + ADAM_EPS) + ADAM_WD * w_refs[i][...])
            outs[2 * n + i][...] = mn
            outs[3 * n + i][...] = vn

    res = pl.pallas_call(
        body, name=name, out_shape=tuple(jax.ShapeDtypeStruct(w.shape, F32) for _ in range(4) for w in ws),
        compiler_params=pltpu.CompilerParams(vmem_limit_bytes=VMEM_LIMIT),
    )(*ws, *g_slots, *g_owns, *ms, *vs)
    return res[:n], res[n:2 * n], res[2 * n:3 * n], res[3 * n:]


SMALL_NAMES = ("b_ada", "b_in", "b_conv", "w_rg_a", "b_rg_a", "w_rg_x", "b_rg_x", "lru_lambda", "w_sp", "b_sp",
               "ln_v_g", "ln_v_b", "ln1_g", "ln1_b", "ln2_g", "ln2_b")
WEIGHT_ORDER = ("w_ada", "b_ada", "w_in", "b_in", "w_conv", "b_conv", "w_rg_a", "b_rg_a", "w_rg_x", "b_rg_x",
                "lru_lambda", "w_sp", "b_sp", "ln_v_g", "ln_v_b", "w_o_lru", "w_o_sgu", "w_out", "ln1_g", "ln1_b",
                "w_up", "w_down", "ln2_g", "ln2_b")


def _blocked_cols(w2d):
    K, N = w2d.shape
    return jnp.transpose(w2d.reshape(K, N_DEV, N // N_DEV), (1, 0, 2))


def _unblock_cols(wb):
    n, K, nb = wb.shape
    return jnp.transpose(wb, (1, 0, 2)).reshape(K, n * nb)


def kernel(x, c, w_ada, b_ada, w_in, b_in, w_conv, b_conv, w_rg_a, b_rg_a, w_rg_x, b_rg_x, lru_lambda, w_sp, b_sp, ln_v_g, ln_v_b, w_o_lru, w_o_sgu, w_out, ln1_g, ln1_b, w_up, w_down, ln2_g, ln2_b, loss_target, m_w_ada, m_b_ada, m_w_in, m_b_in, m_w_conv, m_b_conv, m_w_rg_a, m_b_rg_a, m_w_rg_x, m_b_rg_x, m_lru_lambda, m_w_sp, m_b_sp, m_ln_v_g, m_ln_v_b, m_w_o_lru, m_w_o_sgu, m_w_out, m_ln1_g, m_ln1_b, m_w_up, m_w_down, m_ln2_g, m_ln2_b, v_w_ada, v_b_ada, v_w_in, v_b_in, v_w_conv, v_b_conv, v_w_rg_a, v_b_rg_a, v_w_rg_x, v_b_rg_x, v_lru_lambda, v_w_sp, v_b_sp, v_ln_v_g, v_ln_v_b, v_w_o_lru, v_w_o_sgu, v_w_out, v_ln1_g, v_ln1_b, v_w_up, v_w_down, v_ln2_g, v_ln2_b):
    W = dict(w_ada=w_ada, b_ada=b_ada, w_in=w_in, b_in=b_in, w_conv=w_conv, b_conv=b_conv, w_rg_a=w_rg_a,
             b_rg_a=b_rg_a, w_rg_x=w_rg_x, b_rg_x=b_rg_x, lru_lambda=lru_lambda, w_sp=w_sp, b_sp=b_sp,
             ln_v_g=ln_v_g, ln_v_b=ln_v_b, w_o_lru=w_o_lru, w_o_sgu=w_o_sgu, w_out=w_out, ln1_g=ln1_g, ln1_b=ln1_b,
             w_up=w_up, w_down=w_down, ln2_g=ln2_g, ln2_b=ln2_b)
    Mo = dict(w_ada=m_w_ada, b_ada=m_b_ada, w_in=m_w_in, b_in=m_b_in, w_conv=m_w_conv, b_conv=m_b_conv,
              w_rg_a=m_w_rg_a, b_rg_a=m_b_rg_a, w_rg_x=m_w_rg_x, b_rg_x=m_b_rg_x, lru_lambda=m_lru_lambda,
              w_sp=m_w_sp, b_sp=m_b_sp, ln_v_g=m_ln_v_g, ln_v_b=m_ln_v_b, w_o_lru=m_w_o_lru, w_o_sgu=m_w_o_sgu,
              w_out=m_w_out, ln1_g=m_ln1_g, ln1_b=m_ln1_b, w_up=m_w_up, w_down=m_w_down, ln2_g=m_ln2_g,
              ln2_b=m_ln2_b)
    Vo = dict(w_ada=v_w_ada, b_ada=v_b_ada, w_in=v_w_in, b_in=v_b_in, w_conv=v_w_conv, b_conv=v_b_conv,
              w_rg_a=v_w_rg_a, b_rg_a=v_b_rg_a, w_rg_x=v_w_rg_x, b_rg_x=v_b_rg_x, lru_lambda=v_lru_lambda,
              w_sp=v_w_sp, b_sp=v_b_sp, ln_v_g=v_ln_v_g, ln_v_b=v_ln_v_b, w_o_lru=v_w_o_lru, w_o_sgu=v_w_o_sgu,
              w_out=v_w_out, ln1_g=v_ln1_g, ln1_b=v_ln1_b, w_up=v_w_up, w_down=v_w_down, ln2_g=v_ln2_g,
              ln2_b=v_ln2_b)

    Bl, S, D = x.shape
    T = Bl * S
    lw = b_conv.shape[-1]
    sw = ln_v_g.shape[-1]
    din = b_in.shape[-1]
    dff = w_up.shape[-1] * N_DEV
    ts = min(2048, S)
    tmix = min(256, S)
    trow = min(512, S)

    c_pad = jnp.pad(c, ((0, SUBLANES - Bl), (0, 0)))
    c_g, wconv_g = _exchange([c_pad, w_conv[0]], True, "xchg_c")
    wconv_full = _unblock_cols(wconv_g)
    c_act, modcols = _ada_fwd(c_g.reshape(N_DEV * SUBLANES, D), w_ada[0])
    (mod_slots,) = _exchange([modcols.reshape(N_DEV, SUBLANES, -1)], False, "xchg_mod")

    nbw = din // N_DEV // WIN_PARTS
    wnames = tuple("win%d" % q for q in range(WIN_PARTS)) + ("wol", "wos", "wout", "wup", "wdown")
    shards = [w_in[0][:, q * nbw:(q + 1) * nbw].astype(BF16) for q in range(WIN_PARTS)] + [
        w_o_lru[0].astype(BF16), w_o_sgu[0].astype(BF16), w_out[0].astype(BF16), w_up[0].astype(BF16),
        w_down[0].astype(BF16)]
    col_sharded = [True] * WIN_PARTS + [False, True, False, True, False]
    g_send, g_recv, g_src, g_land, g_tok = _xstart(shards, True, mod_slots, "gather_start", cols=col_sharded)
    gidx = {n: i for i, n in enumerate(wnames)}

    def gathered(n, after):
        i = gidx[n]
        return _xwait(g_src[i], g_land[i], g_send[i], g_recv[i], after, True, "gather_wait_" + n, col=col_sharded[i])

    mod = _unblock_cols(mod_slots)[:Bl] + (b_ada + g_tok[0, 0])
    sh1, sc1, gt1, sh2, sc2, gt2 = [mod[:, i * D:(i + 1) * D].reshape(Bl, 1, D) for i in range(6)]

    wa_b, wx_b = w_rg_a[0].astype(BF16), w_rg_x[0].astype(BF16)
    b_sp_t = jnp.transpose(b_sp[0])
    small_mix = (wconv_full, b_conv, wa_b, b_rg_a, wx_b, b_rg_x, lru_lambda, w_sp[0], b_sp_t, ln_v_g, ln_v_b)

    h = _modulate(x, sc1, sh1, ts)
    proj, win_parts = None, []
    for q in range(WIN_PARTS):
        wq = gathered("win%d" % q, h if q == 0 else proj)
        win_parts.append(wq)
        proj = _mm(h.reshape(T, D), wq, mode="nn", tm=8192, tn=nbw, tk=D, outs=[BF16], extras=[(b_in, "row")],
                   epilogue=lambda acc, ex: (acc + ex[0],), scatter=(WIN_PARTS, q, din), into=proj,
                   name="mm_proj%d" % q)
    proj3 = proj.reshape(Bl, S, din)
    hs, ya_pre, ysgu, *lru_saved = _mix_fwd(proj3, *small_mix, tm=tmix, lw=lw, sw=sw)
    Wol = gathered("wol", ya_pre).reshape(lw, D)
    Wos = gathered("wos", ysgu)
    y_a = _mm(ya_pre.reshape(T, lw), Wol, mode="nn", tm=2048, tn=D, tk=lw, outs=[BF16], name="mm_ya")
    x2d, tgt2d = x.reshape(T, D), loss_target.reshape(T, D)
    gate_cb = (din - 2 * D) // D

    def ep_merge(y_b, v):
        ya, ga, gb = [t.astype(F32) for t in v]
        yb = y_b.astype(BF16).astype(F32)
        return [yb, _sigmoid(ga) * ya + _sigmoid(gb) * yb]

    y_b, merged = _mm_rows(ysgu.reshape(T, sw), Wos, mode="nn", tm=trow, seq=S,
                           ins=[("tile", y_a), ("tilecol", proj, D, gate_cb), ("tilecol", proj, D, gate_cb + 1)],
                           outs=[("tile", BF16, D), ("tile", BF16, D)], epilogue=ep_merge, name="mm_yb_merge")
    Wout = gathered("wout", merged).reshape(D, D)

    def ep_ln1(mix_acc, v):
        x_, gt, g, b, sc, sh = v
        mixr = mix_acc.astype(BF16).astype(F32)
        xhat, _ = _ln_stats(ALPHA * x_ + (1.0 + gt) * mixr)
        x1_ = xhat * g + b
        return [mixr, x1_, x1_ * (1.0 + sc) + sh]

    mix, x1, h2 = _mm_rows(merged, Wout, mode="nn", tm=trow, seq=S,
                           ins=[("tile", x2d), ("brow", gt1), ("row", ln1_g), ("row", ln1_b), ("brow", sc2),
                                ("brow", sh2)],
                           outs=[("tile", BF16, D), ("tile", F32, D), ("tile", BF16, D)], epilogue=ep_ln1,
                           name="mm_mix_ln1")
    Wup = gathered("wup", h2)
    def ep_up(up, ex):
        r = jnp.maximum(up, 0.0)
        return r * r, r + r

    act, dact_dup = _mm(h2, Wup, mode="nn", tm=2048, tn=1024, tk=D, outs=[BF16, BF16], epilogue=ep_up, name="mm_up")
    Wdown = gathered("wdown", act).reshape(dff, D)

    def ep_ln2(f_acc, v):
        x1_, t_, gt, g, b = v
        xhat, rstd = _ln_stats(ALPHA * x1_ + (1.0 + gt) * f_acc)
        err = xhat * g + b - t_
        loss_t = 0.5 * jnp.sum(jnp.mean(err * err, axis=-1, keepdims=True))
        dy = err * (1.0 / D)
        dz = _ln_bwd(dy, xhat, rstd, g)
        return [dz * (1.0 + gt), ALPHA * dz, _colsum(dz * f_acc), _colsum(dy * xhat), _colsum(dy), loss_t]

    df2, dx1p, dgt2, dg2, db2, loss_part = _mm_rows(
        act, Wdown, mode="nn", tm=trow, seq=S,
        ins=[("tile", x1), ("tile", tgt2d), ("brow", gt2), ("row", ln2_g), ("row", ln2_b)],
        outs=[("tile", BF16, D), ("tile", F32, D), ("acc_brow", D), ("acc_row", D), ("acc_row", D), ("acc_scalar",)],
        epilogue=ep_ln2, name="mm_down_ln2")
    loss = lax.psum(loss_part[0, 0], ("x", "y", "c"))

    def send_grads(parts, name):
        snd, rcv, src, land, tok = _xstart(parts, False, None, name + "_start")
        return [(src[i], land[i], snd[i], rcv[i]) for i in range(len(parts))], tok

    dup = _mm(df2, Wdown, mode="nt", tm=2048, tn=1024, tk=D, outs=[BF16], extras=[(dact_dup, "tile")],
              epilogue=lambda acc, ex: (acc * ex[0].astype(F32),), name="mm_dup")
    g_wdown = _mm(act, df2, mode="tn", tm=1024, tn=D, tk=2048, outs=[BF16], name="mm_gwdown")
    (x_wdown,), tok = send_grads([g_wdown.reshape(N_DEV, dff // N_DEV, D)], "gx_wdown")
    def ep_ln1_bwd(dh2, v):
        dx1p_, x1_, x_, mix_, sc, gt, g = v
        mixv = mix_.astype(F32)
        dx1 = dx1p_ + dh2 * (1.0 + sc)
        xhat, rstd = _ln_stats(ALPHA * x_ + (1.0 + gt) * mixv)
        dz = _ln_bwd(dx1, xhat, rstd, g)
        return [ALPHA * dz, dz * (1.0 + gt), _colsum(dh2 * x1_), _colsum(dh2), _colsum(dz * mixv),
                _colsum(dx1 * xhat), _colsum(dx1)]

    dxp, dmix, dsc2, dsh2, dgt1, dg1, db1 = _mm_rows(
        dup, Wup, mode="nt", tm=trow, seq=S, tok=tok,
        ins=[("tile", dx1p), ("tile", x1), ("tile", x2d), ("tile", mix), ("brow", sc2), ("brow", gt1), ("row", ln1_g)],
        outs=[("tile", F32, D), ("tile", BF16, D), ("acc_brow", D), ("acc_brow", D), ("acc_brow", D), ("acc_row", D),
              ("acc_row", D)],
        epilogue=ep_ln1_bwd, name="mm_dh2_ln1b")
    g_wup = _mm(h2, dup, mode="tn", tm=D, tn=1024, tk=2048, outs=[BF16], nb=dff // N_DEV, name="mm_gwup")
    (x_wup,), tok = send_grads([g_wup], "gx_wup")

    def ep_merge_bwd(dm, v):
        ya, yb, ga, gb = [t.astype(F32) for t in v]
        sa, sb = _sigmoid(ga), _sigmoid(gb)
        dg = jnp.concatenate([dm * ya * sa * (1.0 - sa), dm * yb * sb * (1.0 - sb)], axis=1)
        return [dm * sa, dm * sb, dg, _colsum(dg)]

    dy_a, dy_b, dproj, dbin_hi = _mm_rows(
        dmix, Wout, mode="nt", tm=trow, seq=S, tok=tok,
        ins=[("tile", y_a), ("tile", y_b), ("tilecol", proj, D, gate_cb), ("tilecol", proj, D, gate_cb + 1)],
        outs=[("tile", BF16, D), ("tile", BF16, D), ("tilecol", BF16, 2 * D, gate_cb // 2, din), ("acc_row", 2 * D)],
        epilogue=ep_merge_bwd, name="mm_dmerged_mb")
    g_wout = _mm(merged, dmix, mode="tn", tm=D, tn=D, tk=2048, outs=[BF16], name="mm_gwout")
    (x_wout,), tok = send_grads([g_wout.reshape(N_DEV, D // N_DEV, D)], "gx_wout")
    dya_pre = _mm(dy_a, Wol, mode="nt", tm=2048, tn=lw, tk=D, outs=[BF16], tok=tok, name="mm_dya")
    dysgu = _mm(dy_b, Wos, mode="nt", tm=2048, tn=sw, tk=D, outs=[BF16], name="mm_dys")
    g_wol = _mm(ya_pre.reshape(T, lw), dy_a, mode="tn", tm=lw, tn=D, tk=2048, outs=[BF16], name="mm_gwol")
    g_wos = _mm(ysgu.reshape(T, sw), dy_b, mode="tn", tm=sw, tn=D, tk=2048, outs=[BF16], nb=D // N_DEV,
                name="mm_gwos")
    (x_wol, x_wos), tok = send_grads([g_wol.reshape(N_DEV, lw // N_DEV, D), g_wos], "gx_wo")
    small_mix_b = (wconv_full, b_conv + tok[0, 0]) + small_mix[2:]
    (dproj, dbin_lo, g_wconv, g_bconv, g_wa, g_ba, g_wx, g_bx, g_lam, g_wsp, g_bsp_t, g_lvg, g_lvb) = _mix_bwd(
        proj3, hs, dya_pre.reshape(Bl, S, lw), dysgu.reshape(Bl, S, sw), dproj.reshape(Bl, S, din), lru_saved,
        *small_mix_b, tm=tmix, lw=lw, sw=sw)
    dproj2 = dproj.reshape(T, din)
    small_names = [n for n in SMALL_NAMES if n != "b_ada"]
    small_g = dict(b_in=jnp.concatenate([dbin_lo, dbin_hi], axis=-1), b_conv=g_bconv, w_rg_a=g_wa[None], b_rg_a=g_ba,
                   w_rg_x=g_wx[None], b_rg_x=g_bx, lru_lambda=g_lam, w_sp=g_wsp[None],
                   b_sp=jnp.transpose(g_bsp_t)[None], ln_v_g=g_lvg, ln_v_b=g_lvb, ln1_g=dg1, ln1_b=db1, ln2_g=dg2,
                   ln2_b=db2)
    gs_snd, gs_rcv, gs_src, gs_land, tok_s = _xstart([small_g[n] for n in small_names], True, None, "gsmall_start",
                                                      fill_own=False)
    g_win = _mm(h.reshape(T, D), dproj2, mode="tn", tm=D, tn=din // 4, tk=2048, outs=[BF16], nb=din // N_DEV,
                tok=tok_s, name="mm_gwin")
    (x_win,), tok = send_grads([g_win], "gx_win")

    def ep_final(dh, v):
        dxp_, x_, sc = v
        return [dxp_ + dh * (1.0 + sc), _colsum(dh * x_), _colsum(dh)]

    grad_x, dsc1, dsh1 = _mm_rows(dproj2, win_parts, mode="nt", tm=trow, seq=S, tok=tok,
                                  ins=[("tile", dxp), ("tile", x2d), ("brow", sc1)],
                                  outs=[("tile", F32, D), ("acc_brow", D), ("acc_brow", D)], epilogue=ep_final,
                                  name="mm_dh_final")
    grad_x = grad_x.reshape(Bl, S, D)

    out_g, out_d, out_m, out_v = {}, {}, {}, {}

    def adam(name, g_slots, tr, own=None):
        shp = W[name].shape
        w2, m2, v2 = [t.reshape(g_slots.shape[1:]) for t in (W[name], Mo[name], Vo[name])]
        g, d, mn, vn = _adamw(w2, g_slots, m2, v2, tr=tr, name="adam_" + name, own=own)
        out_g[name], out_d[name], out_m[name], out_v[name] = [t.reshape(shp) for t in (g, d, mn, vn)]

    def adam_exchanged(name, handle, tr, after):
        own, slots = _xwait(*handle, after, False, "gx_%s_wait" % name, place=False)
        adam(name, slots, tr, own=own)

    adam_exchanged("w_down", x_wdown, 256, dsh1)
    adam_exchanged("w_up", x_wup, 256, dsh1)
    adam_exchanged("w_out", x_wout, 128, dsh1)
    adam_exchanged("w_o_lru", x_wol, 160, dsh1)
    adam_exchanged("w_o_sgu", x_wos, 256, dsh1)
    gs_own, gs_slots = _xwait_many(gs_src, gs_land, gs_snd, gs_rcv, dsh1, "gsmall_wait")
    res_small = _adamw_many([W[n] for n in small_names], gs_slots, gs_own, [Mo[n] for n in small_names],
                            [Vo[n] for n in small_names], name="adam_small")
    for dst, vals in zip((out_g, out_d, out_m, out_v), res_small):
        dst.update(dict(zip(small_names, vals)))

    dmod = jnp.concatenate([dsh1, dsc1, dgt1, dsh2, dsc2, dgt2], axis=-1).reshape(Bl, 6 * D)
    dmod_b = _blocked_cols(jnp.pad(dmod, ((0, SUBLANES - Bl), (0, 0))))
    dmod_s, gwconv_s = _exchange([dmod_b, _blocked_cols(g_wconv)], False, "xchg_dmod", after=out_g["ln2_b"])
    g_wada, g_bada_mine = _ada_bwd(c_act, dmod_s.reshape(N_DEV * SUBLANES, -1))
    (g_bada_all,) = _exchange([g_bada_mine], True, "xchg_bada")
    adam("w_ada", g_wada[None], 256)
    adam("b_ada", g_bada_all.reshape(1, 1, 6 * D), 1)
    adam("w_conv", gwconv_s, 8)
    adam_exchanged("w_in", x_win, 256, g_bada_all)

    return (loss, grad_x, *[out_g[n] for n in WEIGHT_ORDER], *[out_d[n] for n in WEIGHT_ORDER],
            *[out_m[n] for n in WEIGHT_ORDER], *[out_v[n] for n in WEIGHT_ORDER])
```

```python
import math

import jax
import jax.numpy as jnp
from jax import lax
from jax.experimental import pallas as pl
from jax.experimental.pallas import tpu as pltpu

N_DEV = 8
LN_EPS = 1e-5
LRU_C = 8.0
CHUNK = 64
SGU_BLOCK = 128
ALPHA = 2.0 ** 0.25
ADAM_LR = 0.001
ADAM_B1 = 0.9
ADAM_B2 = 0.999
ADAM_EPS = 1e-08
ADAM_WD = 0.01
ADAM_STEP = 10
GELU_K0 = math.sqrt(2.0 / math.pi)
GELU_K1 = 0.044715

SUBLANES = 8
LANES = 128
VMEM_LIMIT = 56 * 1024 * 1024
WIN_PARTS = 3

F32 = jnp.float32
BF16 = jnp.bfloat16
MESH = pl.DeviceIdType.MESH


def _cparams(n_axes, big=False):
    return pltpu.CompilerParams(dimension_semantics=("arbitrary",) * n_axes,
                                vmem_limit_bytes=VMEM_LIMIT if big else None)


def _sigmoid(x):
    return 0.5 * jnp.tanh(0.5 * x) + 0.5


def _gelu(x):
    t = jnp.tanh(x * (GELU_K0 + (GELU_K0 * GELU_K1) * (x * x)))
    hx = 0.5 * x
    return hx + hx * t


def _gelu_and_grad(x):
    x2 = x * x
    t = jnp.tanh(x * (GELU_K0 + (GELU_K0 * GELU_K1) * x2))
    hx = 0.5 * x
    g = hx + hx * t
    dg = (0.5 + 0.5 * t) + (hx * (1.0 - t * t)) * (GELU_K0 + (3.0 * GELU_K0 * GELU_K1) * x2)
    return g, dg


def _log1p_pos(e):
    p = e * (1.0 - e * (1.0 / 2.0) + e * e * (1.0 / 3.0) - e * e * e * (1.0 / 4.0))
    return jnp.where(e < 1e-2, p, jnp.log(1.0 + e))


def _ln_stats(z):
    mu = jnp.mean(z, axis=-1, keepdims=True)
    zc = z - mu
    var = jnp.mean(zc * zc, axis=-1, keepdims=True)
    rstd = lax.rsqrt(var + LN_EPS)
    return zc * rstd, rstd


def _ln_bwd(dy, xhat, rstd, g):
    dxh = dy * g
    m1 = jnp.mean(dxh, axis=-1, keepdims=True)
    m2 = jnp.mean(dxh * xhat, axis=-1, keepdims=True)
    return rstd * (dxh - m1 - xhat * m2)


def _colsum(v):
    return jnp.sum(v, axis=0, keepdims=True)


def _first_step():
    return jnp.logical_and(pl.program_id(0) == 0, pl.program_id(1) == 0)


def _exchange(arrs, gather, name, after=None):
    n = len(arrs)
    n_peer = N_DEV - 1
    n_after = 0 if after is None else 1

    def body(*refs):
        ins, outs = refs[:n], refs[n + n_after:2 * n + n_after]
        send_sems, recv_sems, loc_sems = refs[2 * n + n_after:]
        x, y, c = lax.axis_index("x"), lax.axis_index("y"), lax.axis_index("c")
        me = 4 * x + 2 * y + c
        started = []
        for a in range(n):
            src_me = ins[a] if gather else ins[a].at[me]
            lc = pltpu.make_async_copy(src_me, outs[a].at[me], loc_sems.at[a])
            lc.start()
            started.append((lc, None))
        for p in range(1, N_DEV):
            px, py, pc = x ^ ((p >> 2) & 1), y ^ ((p >> 1) & 1), c ^ (p & 1)
            peer = 4 * px + 2 * py + pc
            for a in range(n):
                k = a * n_peer + (p - 1)
                src = ins[a] if gather else ins[a].at[peer]
                cp = pltpu.make_async_remote_copy(src_ref=src, dst_ref=outs[a].at[me],
                                                  send_sem=send_sems.at[k], recv_sem=recv_sems.at[k],
                                                  device_id=(px, py, pc), device_id_type=MESH)
                cp.start()
                rc = pltpu.make_async_remote_copy(src_ref=src, dst_ref=outs[a].at[peer],
                                                  send_sem=send_sems.at[k], recv_sem=recv_sems.at[k],
                                                  device_id=(px, py, pc), device_id_type=MESH)
                started.append((cp, rc))
        for cp, rc in started:
            if rc is None:
                cp.wait()
            else:
                cp.wait_send()
                rc.wait_recv()

    hbm = pl.BlockSpec(memory_space=pltpu.HBM)
    out_shape = tuple(
        jax.ShapeDtypeStruct(((N_DEV,) + a.shape) if gather else a.shape, a.dtype) for a in arrs)
    return pl.pallas_call(
        body, name=name, out_shape=out_shape,
        in_specs=[hbm] * n + [pl.BlockSpec(memory_space=pl.ANY)] * n_after, out_specs=tuple([hbm] * n),
        scratch_shapes=[pltpu.SemaphoreType.DMA((n * n_peer,)), pltpu.SemaphoreType.DMA((n * n_peer,)),
                        pltpu.SemaphoreType.DMA((n,))],
        compiler_params=pltpu.CompilerParams(has_side_effects=True),
    )(*arrs, *([after] if n_after else []))


_HBM = pl.BlockSpec(memory_space=pltpu.HBM)
_SEM = pl.BlockSpec(memory_space=pltpu.SEMAPHORE)
_EFFECT = pltpu.SideEffectType.DATAFLOW_SIDE_EFFECTING


def _peer_of(p):
    x, y, c = lax.axis_index("x"), lax.axis_index("y"), lax.axis_index("c")
    px, py, pc = x ^ ((p >> 2) & 1), y ^ ((p >> 1) & 1), c ^ (p & 1)
    return (px, py, pc), 4 * px + 2 * py + pc


def _slot(land_ref, idx, width):
    if width is None:
        return land_ref.at[idx]
    return land_ref.at[:, pl.ds(pl.multiple_of(idx * width, LANES), width)]


def _xstart(srcs, gather, after, name, cols=None, fill_own=False):
    n = len(srcs)
    cols = cols or [False] * n
    widths = [t.shape[1] if cols[a] else None for a, t in enumerate(srcs)]
    me_out = 4 * lax.axis_index("x") + 2 * lax.axis_index("y") + lax.axis_index("c")
    lands = []
    for a, t in enumerate(srcs):
        if cols[a]:
            zone, own, at = lax.empty((t.shape[0], N_DEV * t.shape[1]), t.dtype), t, (0, me_out * t.shape[1])
        elif gather:
            zone, own, at = lax.empty((N_DEV,) + t.shape, t.dtype), t[None], (me_out,) + (0,) * t.ndim
        else:
            zone, own = lax.empty(t.shape, t.dtype), lax.dynamic_index_in_dim(t, me_out, 0, keepdims=True)
            at = (me_out,) + (0,) * (t.ndim - 1)
        lands.append(lax.dynamic_update_slice(zone, own, at) if fill_own else zone)
    n_after = 0 if after is None else 1

    def body(*refs):
        src_refs, land_refs = refs[:n], refs[n:2 * n]
        refs = refs[n_after:]
        send_sems, recv_sems = refs[2 * n:3 * n], refs[3 * n:4 * n]
        token = refs[6 * n]
        me = 4 * lax.axis_index("x") + 2 * lax.axis_index("y") + lax.axis_index("c")
        for a in range(n):
            for p in range(1, N_DEV):
                dev, peer = _peer_of(p)
                pltpu.make_async_remote_copy(
                    src_ref=src_refs[a] if gather else src_refs[a].at[peer], dst_ref=_slot(land_refs[a], me, widths[a]),
                    send_sem=send_sems[a].at[p - 1], recv_sem=recv_sems[a].at[p - 1],
                    device_id=dev, device_id_type=MESH).start()
        token[...] = jnp.zeros_like(token)

    sems = tuple(pltpu.SemaphoreType.DMA((N_DEV - 1,)) for _ in range(2 * n))
    thru = tuple(pltpu.HBM(t.shape, t.dtype) for t in list(srcs) + list(lands))
    res = pl.pallas_call(
        body, name=name,
        out_shape=sems + thru + (jax.ShapeDtypeStruct((SUBLANES, LANES), F32),),
        in_specs=[_HBM] * (2 * n) + [pl.BlockSpec(memory_space=pl.ANY)] * n_after,
        out_specs=tuple([_SEM] * (2 * n) + [_HBM] * (2 * n) + [pl.BlockSpec(memory_space=pltpu.VMEM)]),
        input_output_aliases={i: 2 * n + i for i in range(2 * n)},
        compiler_params=pltpu.CompilerParams(has_side_effects=_EFFECT),
    )(*[pltpu.with_memory_space_constraint(t, pltpu.HBM) for t in list(srcs) + list(lands)],
      *([after] if n_after else []))
    return res[:n], res[n:2 * n], res[2 * n:3 * n], res[3 * n:4 * n], res[4 * n]


def _xwait(src, land, send_sem, recv_sem, after, gather, name, col=False, place=True):
    width = src.shape[1] if col else None

    def body(src_ref, land_ref, send_ref, recv_ref, after_ref, src_dead, land_out):
        del after_ref, src_dead, land_out
        for p in range(1, N_DEV):
            dev, peer = _peer_of(p)
            cp = pltpu.make_async_remote_copy(
                src_ref=src_ref if gather else src_ref.at[peer], dst_ref=_slot(land_ref, peer, width),
                send_sem=send_ref.at[p - 1], recv_sem=recv_ref.at[p - 1], device_id=dev, device_id_type=MESH)
            cp.wait_send()
            cp.wait_recv()

    src_done, landed = pl.pallas_call(
        body, name=name, out_shape=(pltpu.HBM(src.shape, src.dtype), pltpu.HBM(land.shape, land.dtype)),
        in_specs=[_HBM, _HBM, _SEM, _SEM, pl.BlockSpec(memory_space=pl.ANY)], out_specs=(_HBM, _HBM),
        input_output_aliases={0: 0, 1: 1},
        compiler_params=pltpu.CompilerParams(has_side_effects=_EFFECT),
    )(src, land, send_sem, recv_sem, after)
    if not place:
        return src_done, landed
    me = 4 * lax.axis_index("x") + 2 * lax.axis_index("y") + lax.axis_index("c")
    return _place_own(landed, src_done, me, col, gather, name + "_own")


def _place_own(zone, src, me, col, gather, name):
    if col:
        R, C = src.shape
        src_spec = lambda tr: pl.BlockSpec((tr, C), lambda i, me_ref: (i, 0))
        out_spec = lambda tr: pl.BlockSpec((tr, C), lambda i, me_ref: (i, me_ref[0]))
    else:
        R, C = zone.shape[1:]
        src_spec = ((lambda tr: pl.BlockSpec((tr, C), lambda i, me_ref: (i, 0))) if gather else
                    (lambda tr: pl.BlockSpec((None, tr, C), lambda i, me_ref: (me_ref[0], i, 0))))
        out_spec = lambda tr: pl.BlockSpec((None, tr, C), lambda i, me_ref: (me_ref[0], i, 0))
    tr = R if R <= 512 else 256
    assert R % tr == 0, (name, R, tr)

    def body(me_ref, src_ref, zone_ref, out_ref):
        del me_ref, zone_ref
        out_ref[...] = src_ref[...]

    return pl.pallas_call(
        body, name=name, out_shape=jax.ShapeDtypeStruct(zone.shape, zone.dtype),
        grid_spec=pltpu.PrefetchScalarGridSpec(
            num_scalar_prefetch=1, grid=(R // tr,),
            in_specs=[src_spec(tr), pl.BlockSpec(memory_space=pl.ANY)], out_specs=out_spec(tr)),
        input_output_aliases={2: 0},
    )(jnp.reshape(me, (1,)).astype(jnp.int32), src, zone)


def _xwait_many(srcs, lands, send_sems, recv_sems, after, name):
    n = len(srcs)

    def body(*refs):
        src_refs, land_refs = refs[:n], refs[n:2 * n]
        snd, rcv = refs[2 * n:3 * n], refs[3 * n:4 * n]
        for a in range(n):
            for p in range(1, N_DEV):
                dev, peer = _peer_of(p)
                cp = pltpu.make_async_remote_copy(
                    src_ref=src_refs[a], dst_ref=land_refs[a].at[peer], send_sem=snd[a].at[p - 1],
                    recv_sem=rcv[a].at[p - 1], device_id=dev, device_id_type=MESH)
                cp.wait_send()
                cp.wait_recv()

    res = pl.pallas_call(
        body, name=name, out_shape=tuple(pltpu.HBM(t.shape, t.dtype) for t in list(srcs) + list(lands)),
        in_specs=[_HBM] * (2 * n) + [_SEM] * (2 * n) + [pl.BlockSpec(memory_space=pl.ANY)],
        out_specs=tuple([_HBM] * (2 * n)), input_output_aliases={i: i for i in range(2 * n)},
        compiler_params=pltpu.CompilerParams(has_side_effects=_EFFECT),
    )(*srcs, *lands, *send_sems, *recv_sems, after)
    return res[:n], res[n:]


def _mm(a, b, *, mode, tm, tn, tk, outs, epilogue=None, extras=(), nb=None, tok=None, scatter=None, into=None, name):
    if mode == "nn":
        (M, K), (_, N) = a.shape, b.shape
    elif mode == "nt":
        (M, K), (N, _) = a.shape, b.shape
    else:
        (K, M), (_, N) = a.shape, b.shape
    tm, tn, tk = min(tm, M), min(tn, N), min(tk, K)
    assert M % tm == 0 and N % tn == 0 and K % tk == 0, (name, M, N, K, tm, tn, tk)
    if mode == "nn":
        a_spec = pl.BlockSpec((tm, tk), lambda i, j, k: (i, k))
        b_spec = pl.BlockSpec((tk, tn), lambda i, j, k: (k, j))
        dims = (((1,), (0,)), ((), ()))
    elif mode == "nt":
        a_spec = pl.BlockSpec((tm, tk), lambda i, j, k: (i, k))
        b_spec = pl.BlockSpec((tn, tk), lambda i, j, k: (j, k))
        dims = (((1,), (1,)), ((), ()))
    else:
        a_spec = pl.BlockSpec((tk, tm), lambda i, j, k: (k, i))
        b_spec = pl.BlockSpec((tk, tn), lambda i, j, k: (k, j))
        dims = (((0,), (0,)), ((), ()))
    nk = K // tk
    n_ex, n_out = len(extras), len(outs)
    n_tok = 0 if tok is None else 1
    nbytes = lambda d: jnp.dtype(d).itemsize
    vmem_est = (2 * (tm * tk * nbytes(a.dtype) + tk * tn * nbytes(b.dtype)
                     + sum(tm * tn * nbytes(e.dtype) for e, kind in extras if kind == "tile")
                     + sum(tm * tn * nbytes(d) for d in outs)) + tm * tn * 4)
    assert vmem_est <= VMEM_LIMIT, (name, vmem_est)
    if epilogue is None:
        epilogue = lambda acc, ex: tuple(acc.astype(d) for d in outs)

    n_into = 0 if into is None else 1

    def body(a_ref, b_ref, *refs):
        refs = refs[n_tok:]
        ex_refs, out_refs = refs[:n_ex], refs[n_ex + n_into:n_ex + n_into + n_out]

        def finish(acc):
            res = epilogue(acc, [r[...] for r in ex_refs])
            for o_ref, v in zip(out_refs, res):
                if nb is None:
                    o_ref[...] = v.astype(o_ref.dtype)
                else:
                    for q in range(tn // nb):
                        o_ref[q] = v[:, q * nb:(q + 1) * nb].astype(o_ref.dtype)

        part = lax.dot_general(a_ref[...], b_ref[...], dims, preferred_element_type=F32)
        if nk == 1:
            finish(part)
        else:
            acc_ref = refs[n_ex + n_into + n_out]
            k = pl.program_id(2)

            @pl.when(k == 0)
            def _():
                acc_ref[...] = part

            @pl.when(k > 0)
            def _():
                acc_ref[...] += part

            @pl.when(k == nk - 1)
            def _():
                finish(acc_ref[...])

    col = (lambda j: j) if scatter is None else (lambda j: scatter[0] * j + scatter[1])
    ex_specs = [pl.BlockSpec((tm, tn), lambda i, j, k: (i, j)) if kind == "tile"
                else pl.BlockSpec((1, tn), lambda i, j, k: (0, col(j))) for _, kind in extras]
    if nb is not None:
        assert tn % nb == 0, (name, tn, nb)
        o_spec = pl.BlockSpec((tn // nb, tm, nb), lambda i, j, k: (j, i, 0))
        o_shape = (N // nb, M, nb)
    else:
        o_spec = pl.BlockSpec((tm, tn), lambda i, j, k: (i, col(j)))
        o_shape = (M, N if scatter is None else scatter[2])
    assert n_into == 0 or n_out == 1
    res = pl.pallas_call(
        body, name=name, grid=(M // tm, N // tn, nk),
        in_specs=[a_spec, b_spec] + [pl.BlockSpec((SUBLANES, LANES), lambda i, j, k: (0, 0))] * n_tok + ex_specs
                 + [pl.BlockSpec(memory_space=pl.ANY)] * n_into,
        out_specs=tuple([o_spec] * n_out),
        out_shape=tuple(jax.ShapeDtypeStruct(o_shape, d) for d in outs),
        input_output_aliases={2 + n_tok + n_ex: 0} if n_into else {},
        scratch_shapes=[pltpu.VMEM((tm, tn), F32)] if nk > 1 else [],
        compiler_params=_cparams(3, big=True),
    )(a, b, *([tok] if n_tok else []), *[e for e, _ in extras], *([into] if n_into else []))
    return res[0] if n_out == 1 else res


def _mm_rows(a, b, *, mode, tm, seq, ins, outs, epilogue, tok=None, name):
    M, K = a.shape
    b_parts = list(b) if isinstance(b, (list, tuple)) else [b]
    n_part = len(b_parts)
    assert n_part == 1 or mode == "nt"
    N = b_parts[0].shape[1] if mode == "nn" else b_parts[0].shape[0]
    tm = min(tm, M)
    assert M % tm == 0 and seq % tm == 0, (name, M, seq, tm)
    tpb = seq // tm
    n_b = M // seq
    dims = (((1,), (0,)), ((), ())) if mode == "nn" else (((1,), (1,)), ((), ()))
    n_tok = 0 if tok is None else 1
    n_in, n_out = len(ins), len(outs)

    in_specs, in_arrs = [], []
    for spec in ins:
        kind, arr = spec[0], spec[1]
        in_arrs.append(arr)
        if kind == "tile":
            in_specs.append(pl.BlockSpec((tm, arr.shape[1]), lambda i: (i, 0)))
        elif kind == "tilecol":
            in_specs.append(pl.BlockSpec((tm, spec[2]), lambda i, cb=spec[3]: (i, cb)))
        elif kind == "row":
            in_specs.append(pl.BlockSpec(arr.shape, lambda i: (0, 0)))
        else:
            in_specs.append(pl.BlockSpec((None, 1, arr.shape[2]), lambda i: (i // tpb, 0, 0)))
    out_specs, out_shapes = [], []
    for spec in outs:
        kind = spec[0]
        if kind == "tile":
            out_specs.append(pl.BlockSpec((tm, spec[2]), lambda i: (i, 0)))
            out_shapes.append(jax.ShapeDtypeStruct((M, spec[2]), spec[1]))
        elif kind == "tilecol":
            out_specs.append(pl.BlockSpec((tm, spec[2]), lambda i, cb=spec[3]: (i, cb)))
            out_shapes.append(jax.ShapeDtypeStruct((M, spec[4]), spec[1]))
        elif kind == "acc_row":
            out_specs.append(pl.BlockSpec((1, spec[1]), lambda i: (0, 0)))
            out_shapes.append(jax.ShapeDtypeStruct((1, spec[1]), F32))
        elif kind == "acc_brow":
            out_specs.append(pl.BlockSpec((None, 1, spec[1]), lambda i: (i // tpb, 0, 0)))
            out_shapes.append(jax.ShapeDtypeStruct((n_b, 1, spec[1]), F32))
        else:
            out_specs.append(pl.BlockSpec((SUBLANES, LANES), lambda i: (0, 0)))
            out_shapes.append(jax.ShapeDtypeStruct((SUBLANES, LANES), F32))

    def body(a_ref, *refs):
        b_refs, refs = refs[:n_part], refs[n_part + n_tok:]
        in_refs, out_refs = refs[:n_in], refs[n_in:n_in + n_out]
        i = pl.program_id(0)
        if n_part == 1:
            prod = lax.dot_general(a_ref[...], b_refs[0][...], dims, preferred_element_type=F32)
        else:
            w = b_parts[0].shape[1] // N_DEV
            prod = None
            for q in range(n_part):
                a_q = jnp.concatenate([a_ref[:, (n_part * j + q) * w:(n_part * j + q + 1) * w] for j in range(N_DEV)],
                                      axis=1)
                pq = lax.dot_general(a_q, b_refs[q][...], dims, preferred_element_type=F32)
                prod = pq if prod is None else prod + pq
        vals = epilogue(prod, [r[...] for r in in_refs])
        for spec, o_ref, v in zip(outs, out_refs, vals):
            kind = spec[0]
            if kind in ("tile", "tilecol"):
                o_ref[...] = v.astype(o_ref.dtype)
            else:
                first = (i % tpb == 0) if kind == "acc_brow" else (i == 0)

                @pl.when(first)
                def _(o_ref=o_ref, v=v):
                    o_ref[...] = jnp.broadcast_to(v, o_ref.shape)

                @pl.when(jnp.logical_not(first))
                def _(o_ref=o_ref, v=v):
                    o_ref[...] += v

    res = pl.pallas_call(
        body, name=name, grid=(M // tm,),
        in_specs=[pl.BlockSpec((tm, K), lambda i: (i, 0))]
                 + [pl.BlockSpec(bp.shape, lambda i: (0, 0), pipeline_mode=pl.Buffered(1)) for bp in b_parts]
                 + [pl.BlockSpec((SUBLANES, LANES), lambda i: (0, 0))] * n_tok + in_specs,
        out_specs=tuple(out_specs), out_shape=tuple(out_shapes),
        compiler_params=_cparams(1, big=True),
    )(a, *b_parts, *([tok] if n_tok else []), *in_arrs)
    return res


def _tok_spec(ts, width, col_block=0):
    return pl.BlockSpec((None, ts, width), lambda b, s: (b, s, col_block))


def _brow_spec(width):
    return pl.BlockSpec((None, 1, width), lambda b, s: (b, 0, 0))


def _modulate(x, sc, sh, ts):
    Bl, S, D = x.shape

    def body(x_ref, sc_ref, sh_ref, o_ref):
        o_ref[...] = (x_ref[...] * (1.0 + sc_ref[...]) + sh_ref[...]).astype(BF16)

    return pl.pallas_call(
        body, name="modulate", grid=(Bl, S // ts),
        in_specs=[_tok_spec(ts, D), _brow_spec(D), _brow_spec(D)],
        out_specs=_tok_spec(ts, D), out_shape=jax.ShapeDtypeStruct((Bl, S, D), BF16),
        compiler_params=_cparams(2),
    )(x, sc, sh)


def _mix_fwd(proj, w_conv, b_conv, w_rg_a, b_rg_a, w_rg_x, b_rg_x, lam, w_sp, b_sp_t, ln_v_g, ln_v_b, *, tm, lw, sw):
    Bl, S, _ = proj.shape
    heads, hd = w_rg_a.shape[0], w_rg_a.shape[1]
    groups = w_sp.shape[0]
    cw = 2 * lw + 2 * sw
    nblk = tm // SGU_BLOCK

    G = tm // SUBLANES
    nc = lw // LANES

    def body(p_ref, wc_ref, bc_ref, wa_ref, ba_ref, wx_ref, bx_ref, lam_ref, wsp_ref, bsp_ref, lg_ref, lb_ref,
             hs_ref, ya_ref, ys_ref, xc_ref, r_ref, ig_ref, a_ref, m_ref,
             xext, hnat, hcar, h7_scr, a7_scr, hp_scr):
        s = pl.program_id(1)

        @pl.when(s == 0)
        def _():
            xext[:, 0:SUBLANES, :] = jnp.zeros((nc, SUBLANES, LANES), F32)
            hcar[...] = jnp.zeros_like(hcar)

        @pl.when(s > 0)
        def _():
            xext[:, 0:SUBLANES, :] = xext[:, tm:tm + SUBLANES, :]

        for c in range(nc):
            xext[c, SUBLANES:SUBLANES + tm, :] = p_ref[:, c * LANES:(c + 1) * LANES].astype(F32)
        gl = p_ref[:, lw:2 * lw].astype(F32)

        def slab(ref3, start):
            return jnp.concatenate([ref3[c, pl.ds(start, G, stride=SUBLANES), :] for c in range(nc)], axis=1)

        xs = {st: slab(xext, st) for st in range(SUBLANES - 3, 2 * SUBLANES)}
        xc_slabs = []
        for j in range(SUBLANES):
            acc = bc_ref[...] + xs[SUBLANES + j] * wc_ref[3:4, :]
            for k in (1, 2, 3):
                acc = acc + xs[SUBLANES + j - k] * wc_ref[3 - k:4 - k, :]
            xc_slabs.append(acc)
        xc = jnp.concatenate(xc_slabs, axis=0)

        xcb = xc.astype(BF16)
        pa = jnp.concatenate([jnp.dot(xcb[:, h * hd:(h + 1) * hd], wa_ref[h], preferred_element_type=F32)
                              for h in range(heads)], axis=1) + ba_ref[...]
        px = jnp.concatenate([jnp.dot(xcb[:, h * hd:(h + 1) * hd], wx_ref[h], preferred_element_type=F32)
                              for h in range(heads)], axis=1) + bx_ref[...]
        r = _sigmoid(pa)
        ig = _sigmoid(px)
        nl = -lam_ref[...]
        big_l = -LRU_C * (jnp.maximum(nl, 0.0) + _log1p_pos(jnp.exp(-jnp.abs(nl))))
        la = big_l * r
        a = jnp.exp(la)
        th = jnp.tanh(la)
        msq = (-2.0 * th) * pl.reciprocal(1.0 - th, approx=True)
        m = msq * lax.rsqrt(jnp.maximum(msq, 1e-30))
        bin_ = m * (ig * xc)
        xc_ref[...] = xc
        r_ref[...] = r
        ig_ref[...] = ig
        a_ref[...] = a
        m_ref[...] = m

        h0 = [bin_[0:G]]
        cp = [a[0:G]]
        for j in range(1, SUBLANES):
            aj = a[j * G:(j + 1) * G]
            h0.append(aj * h0[j - 1] + bin_[j * G:(j + 1) * G])
            cp.append(aj * cp[j - 1])
        h7_scr[...] = h0[SUBLANES - 1]
        a7_scr[...] = cp[SUBLANES - 1]
        carry = hcar[0:1, :]
        for g in range(G):
            hp_scr[g:g + 1, :] = carry
            carry = h7_scr[g:g + 1, :] + a7_scr[g:g + 1, :] * carry
        hcar[0:1, :] = carry
        hprev = hp_scr[...]
        for j in range(SUBLANES):
            hj = h0[j] + cp[j] * hprev
            for c in range(nc):
                hnat[c, pl.ds(j, G, stride=SUBLANES), :] = hj[:, c * LANES:(c + 1) * LANES]
        hs = jnp.concatenate([hnat[c] for c in range(nc)], axis=1)
        hs_ref[...] = hs
        ya_ref[...] = (hs * _gelu(gl)).astype(BF16)

        gu = _gelu(p_ref[:, 2 * lw:2 * lw + sw].astype(F32))
        gv = _gelu(p_ref[:, 2 * lw + sw:cw].astype(F32))
        xhat, _ = _ln_stats(gv)
        vn = (xhat * lg_ref[...] + lb_ref[...]).astype(BF16)
        tpos = lax.broadcasted_iota(jnp.int32, (SGU_BLOCK, SGU_BLOCK), 0) // CHUNK
        spos = lax.broadcasted_iota(jnp.int32, (SGU_BLOCK, SGU_BLOCK), 1) // CHUNK
        gw = sw // groups
        rows_out = []
        for blk in range(nblk):
            r0 = blk * SGU_BLOCK
            cols = []
            for g in range(groups):
                wm = jnp.where(spos <= tpos, wsp_ref[g], 0.0).astype(BF16)
                mixed = jnp.dot(wm, vn[r0:r0 + SGU_BLOCK, g * gw:(g + 1) * gw], preferred_element_type=F32)
                cols.append(mixed + bsp_ref[:, g:g + 1])
            rows_out.append(jnp.concatenate(cols, axis=1))
        mixed_all = jnp.concatenate(rows_out, axis=0) if nblk > 1 else rows_out[0]
        ys_ref[...] = (gu * mixed_all).astype(BF16)

    full = lambda shp: pl.BlockSpec(shp, lambda b, s: (0,) * len(shp))
    return pl.pallas_call(
        body, name="mix_fwd", grid=(Bl, S // tm),
        in_specs=[_tok_spec(tm, cw), full(w_conv.shape), full(b_conv.shape), full(w_rg_a.shape), full(b_rg_a.shape),
                  full(w_rg_x.shape), full(b_rg_x.shape), full(lam.shape), full(w_sp.shape), full(b_sp_t.shape),
                  full(ln_v_g.shape), full(ln_v_b.shape)],
        out_specs=(_tok_spec(tm, lw), _tok_spec(tm, lw), _tok_spec(tm, sw)) + (_tok_spec(tm, lw),) * 5,
        out_shape=(jax.ShapeDtypeStruct((Bl, S, lw), F32), jax.ShapeDtypeStruct((Bl, S, lw), BF16),
                   jax.ShapeDtypeStruct((Bl, S, sw), BF16)) + (jax.ShapeDtypeStruct((Bl, S, lw), F32),) * 5,
        scratch_shapes=[pltpu.VMEM((nc, tm + SUBLANES, LANES), F32), pltpu.VMEM((nc, tm, LANES), F32),
                        pltpu.VMEM((SUBLANES, lw), F32), pltpu.VMEM((G, lw), F32), pltpu.VMEM((G, lw), F32),
                        pltpu.VMEM((G, lw), F32)],
        compiler_params=_cparams(2, big=True),
    )(proj, w_conv, b_conv, w_rg_a, b_rg_a, w_rg_x, b_rg_x, lam, w_sp, b_sp_t, ln_v_g, ln_v_b)


def _mix_bwd(proj, hs, dya, dys, dproj, saved, w_conv, b_conv, w_rg_a, b_rg_a, w_rg_x, b_rg_x, lam, w_sp, b_sp_t,
             ln_v_g, ln_v_b, *, tm, lw, sw):
    Bl, S, din = proj.shape
    heads, hd = w_rg_a.shape[0], w_rg_a.shape[1]
    groups = w_sp.shape[0]
    gw = sw // groups
    cw = 2 * lw + 2 * sw
    nblk = tm // SGU_BLOCK
    n_s = S // tm
    per8 = tm // SUBLANES
    halo_rows = 2 * SUBLANES

    G = tm // SUBLANES
    nc = lw // LANES

    def body(p_ref, xh_ref, hs_ref, hh_ref, dya_ref, dys_ref, dpin_ref, xc_ref, r_ref, ig_ref, a_ref, m_ref,
             wc_ref, bc_ref, wa_ref, ba_ref, wx_ref, bx_ref, lam_ref, wsp_ref, bsp_ref, lg_ref, lb_ref,
             dp_ref, dbin_ref, dwc_ref, dbc_ref, dwa_ref, dba_ref, dwx_ref, dbx_ref, dlam_ref, dwsp_ref, dbsp_ref,
             dlg_ref, dlb_ref,
             xext, hext, dnat, dxext, dhcar, g00_scr, p0_scr, a0_scr, cin_scr):
        del dpin_ref
        sr = pl.program_id(1)
        first_tile = sr == n_s - 1

        @pl.when(_first_step())
        def _():
            for ref in (dbin_ref, dwc_ref, dbc_ref, dwa_ref, dba_ref, dwx_ref, dbx_ref, dlam_ref, dwsp_ref, dbsp_ref,
                        dlg_ref, dlb_ref):
                ref[...] = jnp.zeros_like(ref)

        @pl.when(sr == 0)
        def _():
            dhcar[...] = jnp.zeros_like(dhcar)
            dxext[:, tm:tm + SUBLANES, :] = jnp.zeros((nc, SUBLANES, LANES), F32)

        @pl.when(sr > 0)
        def _():
            dxext[:, tm:tm + SUBLANES, :] = dxext[:, 0:SUBLANES, :]

        def slab(ref3, start):
            return jnp.concatenate([ref3[c, pl.ds(start, G, stride=SUBLANES), :] for c in range(nc)], axis=1)

        def put_slab(ref3, j, val):
            for c in range(nc):
                ref3[c, pl.ds(j, G, stride=SUBLANES), :] = val[:, c * LANES:(c + 1) * LANES]

        keep = jnp.where(first_tile, 0.0, 1.0)
        xprev = xh_ref[...].astype(F32)[halo_rows - SUBLANES:halo_rows] * keep
        hsv = hs_ref[...]
        hprev8 = hh_ref[...] * keep
        for c in range(nc):
            cs = slice(c * LANES, (c + 1) * LANES)
            xext[c, 0:SUBLANES, :] = xprev[:, cs]
            xext[c, SUBLANES:SUBLANES + tm, :] = p_ref[:, cs].astype(F32)
            hext[c, 0:SUBLANES, :] = hprev8[:, cs]
            hext[c, SUBLANES:SUBLANES + tm, :] = hsv[:, cs]
        gl = p_ref[:, lw:2 * lw].astype(F32)
        ggl, dggl = _gelu_and_grad(gl)
        dyav = dya_ref[...].astype(F32)
        dhs = dyav * ggl
        dgl = dyav * hsv * dggl
        dp_ref[:, lw:2 * lw] = dgl.astype(BF16)
        dbin_ref[:, lw:2 * lw] += _colsum(dgl)
        for c in range(nc):
            dnat[c] = dhs[:, c * LANES:(c + 1) * LANES]

        xc, r, ig, a, m = xc_ref[...], r_ref[...], ig_ref[...], a_ref[...], m_ref[...]
        xcb = xc.astype(BF16)
        nl = -lam_ref[...]
        big_l = -LRU_C * (jnp.maximum(nl, 0.0) + _log1p_pos(jnp.exp(-jnp.abs(nl))))

        g0 = [None] * SUBLANES
        pp = [None] * SUBLANES
        g0[SUBLANES - 1] = slab(dnat, SUBLANES - 1)
        for j in range(SUBLANES - 2, -1, -1):
            an = a[(j + 1) * G:(j + 2) * G]
            g0[j] = slab(dnat, j) + an * g0[j + 1]
            pp[j] = an if j == SUBLANES - 2 else an * pp[j + 1]
        g00_scr[...] = g0[0]
        p0_scr[...] = pp[0]
        a0_scr[...] = a[0:G]
        cin = dhcar[0:1, :]
        for g in range(G - 1, -1, -1):
            cin_scr[g:g + 1, :] = cin
            cin = a0_scr[g:g + 1, :] * (g00_scr[g:g + 1, :] + p0_scr[g:g + 1, :] * cin)
        dhcar[0:1, :] = cin
        cinv = cin_scr[...]
        dh = jnp.concatenate([g0[j] + pp[j] * cinv for j in range(SUBLANES - 1)] + [g0[SUBLANES - 1] + cinv], axis=0)

        hprev = jnp.concatenate([slab(hext, SUBLANES - 1 + j) for j in range(SUBLANES)], axis=0)
        da = dh * hprev
        ixc = ig * xc
        dm = dh * ixc
        dixc = dh * m
        di = dixc * xc
        dxc = dixc * ig
        dla = da * a - dm * ((a * a) * pl.reciprocal(m, approx=True))
        dlam_ref[...] += _colsum(dla * r) * (LRU_C * _sigmoid(nl))
        dr = dla * big_l
        dpa = dr * r * (1.0 - r)
        dpx = di * ig * (1.0 - ig)
        dba_ref[...] += _colsum(dpa)
        dbx_ref[...] += _colsum(dpx)
        dpab = dpa.astype(BF16)
        dpxb = dpx.astype(BF16)
        nt = (((1,), (1,)), ((), ()))
        tn = (((0,), (0,)), ((), ()))
        dxc_g = []
        for h in range(heads):
            sl = slice(h * hd, (h + 1) * hd)
            dxc_g.append(lax.dot_general(dpab[:, sl], wa_ref[h], nt, preferred_element_type=F32)
                         + lax.dot_general(dpxb[:, sl], wx_ref[h], nt, preferred_element_type=F32))
            dwa_ref[h] += lax.dot_general(xcb[:, sl], dpab[:, sl], tn, preferred_element_type=F32)
            dwx_ref[h] += lax.dot_general(xcb[:, sl], dpxb[:, sl], tn, preferred_element_type=F32)
        dxc = dxc + jnp.concatenate(dxc_g, axis=1)

        dbc_ref[...] += _colsum(dxc)
        xs = {st: slab(xext, st) for st in range(SUBLANES - 3, 2 * SUBLANES)}
        for k in range(4):
            xsh = jnp.concatenate([xs[SUBLANES + j - (3 - k)] for j in range(SUBLANES)], axis=0)
            dwc_ref[k:k + 1, :] += _colsum(dxc * xsh)
        for j in range(SUBLANES):
            put_slab(dxext, j, dxc[j * G:(j + 1) * G])
        us = {st: slab(dxext, st) for st in range(SUBLANES + 3)}
        for j in range(SUBLANES):
            acc = us[j] * wc_ref[3:4, :]
            for k in (1, 2, 3):
                acc = acc + us[j + k] * wc_ref[3 - k:4 - k, :]
            put_slab(dnat, j, acc)
        dxl = jnp.concatenate([dnat[c] for c in range(nc)], axis=1)
        dp_ref[:, 0:lw] = dxl.astype(BF16)
        dbin_ref[:, 0:lw] += _colsum(dxl)

        gu, dgu_dx = _gelu_and_grad(p_ref[:, 2 * lw:2 * lw + sw].astype(F32))
        gv, dgv_dx = _gelu_and_grad(p_ref[:, 2 * lw + sw:cw].astype(F32))
        xhat, rstd = _ln_stats(gv)
        vn = (xhat * lg_ref[...] + lb_ref[...]).astype(BF16)
        dys = dys_ref[...].astype(F32)
        dmixed = dys * gu
        dmb = dmixed.astype(BF16)
        tpos = lax.broadcasted_iota(jnp.int32, (SGU_BLOCK, SGU_BLOCK), 0) // CHUNK
        spos = lax.broadcasted_iota(jnp.int32, (SGU_BLOCK, SGU_BLOCK), 1) // CHUNK
        causal = spos <= tpos
        mixed_rows, dvn_rows = [], []
        for blk in range(nblk):
            rs = slice(blk * SGU_BLOCK, (blk + 1) * SGU_BLOCK)
            mcols, dcols = [], []
            for g in range(groups):
                cs = slice(g * gw, (g + 1) * gw)
                wm = jnp.where(causal, wsp_ref[g], 0.0).astype(BF16)
                mcols.append(jnp.dot(wm, vn[rs, cs], preferred_element_type=F32) + bsp_ref[:, g:g + 1])
                dcols.append(lax.dot_general(wm, dmb[rs, cs], tn, preferred_element_type=F32))
                dw = lax.dot_general(dmb[rs, cs], vn[rs, cs], nt, preferred_element_type=F32)
                dwsp_ref[g] += jnp.where(causal, dw, 0.0)
                dbsp_ref[:, g:g + 1] += jnp.sum(dmixed[rs, cs], axis=1, keepdims=True)
            mixed_rows.append(jnp.concatenate(mcols, axis=1))
            dvn_rows.append(jnp.concatenate(dcols, axis=1))
        mixed_all = jnp.concatenate(mixed_rows, axis=0) if nblk > 1 else mixed_rows[0]
        dvn = jnp.concatenate(dvn_rows, axis=0) if nblk > 1 else dvn_rows[0]
        du = dys * mixed_all * dgu_dx
        dlg_ref[...] += _colsum(dvn * xhat)
        dlb_ref[...] += _colsum(dvn)
        dv = _ln_bwd(dvn, xhat, rstd, lg_ref[...]) * dgv_dx
        dp_ref[:, 2 * lw:2 * lw + sw] = du.astype(BF16)
        dp_ref[:, 2 * lw + sw:cw] = dv.astype(BF16)
        dbin_ref[:, 2 * lw:2 * lw + sw] += _colsum(du)
        dbin_ref[:, 2 * lw + sw:cw] += _colsum(dv)

    rev = lambda s: n_s - 1 - s
    tile = lambda w: pl.BlockSpec((None, tm, w), lambda b, s: (b, rev(s), 0))
    halo = lambda w: pl.BlockSpec((None, SUBLANES, w), lambda b, s: (b, jnp.maximum(rev(s) * per8 - 1, 0), 0))
    xhalo = pl.BlockSpec((None, halo_rows, lw), lambda b, s: (b, jnp.maximum(rev(s) * (tm // halo_rows) - 1, 0), 0))
    full = lambda shp: pl.BlockSpec(shp, lambda b, s: (0,) * len(shp))
    small = [w_conv, b_conv, w_rg_a, b_rg_a, w_rg_x, b_rg_x, lam, w_sp, b_sp_t, ln_v_g, ln_v_b]
    acc_shapes = [(1, cw), w_conv.shape, b_conv.shape, w_rg_a.shape, b_rg_a.shape, w_rg_x.shape, b_rg_x.shape,
                  lam.shape, w_sp.shape, b_sp_t.shape, ln_v_g.shape, ln_v_b.shape]
    res = pl.pallas_call(
        body, name="mix_bwd", grid=(Bl, n_s),
        in_specs=[tile(cw), xhalo, tile(lw), halo(lw), tile(lw), tile(sw), pl.BlockSpec(memory_space=pl.ANY)]
                 + [tile(lw)] * 5 + [full(w.shape) for w in small],
        out_specs=tuple([tile(cw)] + [full(shp) for shp in acc_shapes]),
        out_shape=tuple([jax.ShapeDtypeStruct((Bl, S, din), BF16)] + [jax.ShapeDtypeStruct(shp, F32) for shp in acc_shapes]),
        input_output_aliases={6: 0},
        scratch_shapes=[pltpu.VMEM((nc, tm + SUBLANES, LANES), F32), pltpu.VMEM((nc, tm + SUBLANES, LANES), F32),
                        pltpu.VMEM((nc, tm, LANES), F32), pltpu.VMEM((nc, tm + SUBLANES, LANES), F32),
                        pltpu.VMEM((SUBLANES, lw), F32), pltpu.VMEM((G, lw), F32), pltpu.VMEM((G, lw), F32),
                        pltpu.VMEM((G, lw), F32), pltpu.VMEM((G, lw), F32)],
        compiler_params=_cparams(2, big=True),
    )(proj, proj, hs, hs, dya, dys, dproj, *saved, *small)
    return res


def _ada_fwd(c_all, w_ada):
    R, D = c_all.shape
    nb = w_ada.shape[1]

    def body(c_ref, w_ref, act_ref, o_ref):
        cv = c_ref[...]
        act = (cv * _sigmoid(cv)).astype(BF16)
        act_ref[...] = act
        o_ref[...] = jnp.dot(act, w_ref[...].astype(BF16), preferred_element_type=F32)

    return pl.pallas_call(
        body, name="ada_fwd",
        out_shape=(jax.ShapeDtypeStruct((R, D), BF16), jax.ShapeDtypeStruct((R, nb), F32)),
        compiler_params=pltpu.CompilerParams(vmem_limit_bytes=VMEM_LIMIT),
    )(c_all, w_ada)


def _ada_bwd(c_act, dmod_cols):
    R, D = c_act.shape
    nb = dmod_cols.shape[1]

    def body(act_ref, d_ref, o_ref, b_ref):
        o_ref[...] = lax.dot_general(act_ref[...], d_ref[...].astype(BF16), (((0,), (0,)), ((), ())),
                                     preferred_element_type=F32)
        b_ref[...] = _colsum(d_ref[...])

    return pl.pallas_call(
        body, name="ada_bwd", out_shape=(jax.ShapeDtypeStruct((D, nb), F32), jax.ShapeDtypeStruct((1, nb), F32)),
        compiler_params=pltpu.CompilerParams(vmem_limit_bytes=VMEM_LIMIT),
    )(c_act, dmod_cols)


def _adamw(w, g_slots, m, v, *, tr, name, own=None):
    R, C = w.shape
    n_slot = g_slots.shape[0]
    tr = min(tr, R)
    assert R % tr == 0, (name, R, tr)
    c1 = 1.0 / (1.0 - ADAM_B1 ** ADAM_STEP)
    c2 = 1.0 / (1.0 - ADAM_B2 ** ADAM_STEP)
    n_own = 0 if own is None else 1

    def body(me_ref, w_ref, g_ref, *refs):
        m_ref, v_ref, go_ref, d_ref, mo_ref, vo_ref = refs[n_own:]
        slot = lambda d: (jnp.where(me_ref[0] == d, refs[0][...], g_ref[d]) if n_own else g_ref[d]).astype(F32)
        g = slot(0)
        for d in range(1, n_slot):
            g = g + slot(d)
        mn = ADAM_B1 * m_ref[...] + (1.0 - ADAM_B1) * g
        vn = ADAM_B2 * v_ref[...] + (1.0 - ADAM_B2) * (g * g)
        go_ref[...] = g
        mo_ref[...] = mn
        vo_ref[...] = vn
        d_ref[...] = -ADAM_LR * ((mn * c1) / (jnp.sqrt(vn * c2) + ADAM_EPS) + ADAM_WD * w_ref[...])

    me = 4 * lax.axis_index("x") + 2 * lax.axis_index("y") + lax.axis_index("c")
    blk = pl.BlockSpec((tr, C), lambda i, me_ref: (i, 0))
    own_specs = [pl.BlockSpec((None, tr, C), lambda i, me_ref: (me_ref[0], i, 0))] * n_own
    return pl.pallas_call(
        body, name=name, out_shape=tuple(jax.ShapeDtypeStruct((R, C), F32) for _ in range(4)),
        grid_spec=pltpu.PrefetchScalarGridSpec(
            num_scalar_prefetch=1, grid=(R // tr,),
            in_specs=[blk, pl.BlockSpec((n_slot, tr, C), lambda i, me_ref: (0, i, 0))] + own_specs + [blk, blk],
            out_specs=(blk, blk, blk, blk)),
        compiler_params=_cparams(1, big=True),
    )(jnp.reshape(me, (1,)).astype(jnp.int32), w, g_slots, *([own] if n_own else []), m, v)


def _adamw_many(ws, g_slots, g_owns, ms, vs, *, name):
    n = len(ws)
    c1 = 1.0 / (1.0 - ADAM_B1 ** ADAM_STEP)
    c2 = 1.0 / (1.0 - ADAM_B2 ** ADAM_STEP)

    def body(*refs):
        w_refs, g_refs, o_refs = refs[:n], refs[n:2 * n], refs[2 * n:3 * n]
        m_refs, v_refs = refs[3 * n:4 * n], refs[4 * n:5 * n]
        outs = refs[5 * n:]
        me = 4 * lax.axis_index("x") + 2 * lax.axis_index("y") + lax.axis_index("c")
        for i in range(n):
            own = o_refs[i][...]
            g = jnp.where(me == 0, own, g_refs[i][0])
            for d in range(1, N_DEV):
                g = g + jnp.where(me == d, own, g_refs[i][d])
            mn = ADAM_B1 * m_refs[i][...] + (1.0 - ADAM_B1) * g
            vn = ADAM_B2 * v_refs[i][...] + (1.0 - ADAM_B2) * (g * g)
            outs[i][...] = g
            outs[n + i][...] = -ADAM_LR * ((mn * c1) / (jnp.sqrt(vn * c2) + ADAM_EPS) + ADAM_WD * w_refs[i][...])
            outs[2 * n + i][...] = mn
            outs[3 * n + i][...] = vn

    res = pl.pallas_call(
        body, name=name, out_shape=tuple(jax.ShapeDtypeStruct(w.shape, F32) for _ in range(4) for w in ws),
        compiler_params=pltpu.CompilerParams(vmem_limit_bytes=VMEM_LIMIT),
    )(*ws, *g_slots, *g_owns, *ms, *vs)
    return res[:n], res[n:2 * n], res[2 * n:3 * n], res[3 * n:]


SMALL_NAMES = ("b_ada", "b_in", "b_conv", "w_rg_a", "b_rg_a", "w_rg_x", "b_rg_x", "lru_lambda", "w_sp", "b_sp",
               "ln_v_g", "ln_v_b", "ln1_g", "ln1_b", "ln2_g", "ln2_b")
WEIGHT_ORDER = ("w_ada", "b_ada", "w_in", "b_in", "w_conv", "b_conv", "w_rg_a", "b_rg_a", "w_rg_x", "b_rg_x",
                "lru_lambda", "w_sp", "b_sp", "ln_v_g", "ln_v_b", "w_o_lru", "w_o_sgu", "w_out", "ln1_g", "ln1_b",
                "w_up", "w_down", "ln2_g", "ln2_b")


def _blocked_cols(w2d):
    K, N = w2d.shape
    return jnp.transpose(w2d.reshape(K, N_DEV, N // N_DEV), (1, 0, 2))


def _unblock_cols(wb):
    n, K, nb = wb.shape
    return jnp.transpose(wb, (1, 0, 2)).reshape(K, n * nb)


def kernel(x, c, w_ada, b_ada, w_in, b_in, w_conv, b_conv, w_rg_a, b_rg_a, w_rg_x, b_rg_x, lru_lambda, w_sp, b_sp, ln_v_g, ln_v_b, w_o_lru, w_o_sgu, w_out, ln1_g, ln1_b, w_up, w_down, ln2_g, ln2_b, loss_target, m_w_ada, m_b_ada, m_w_in, m_b_in, m_w_conv, m_b_conv, m_w_rg_a, m_b_rg_a, m_w_rg_x, m_b_rg_x, m_lru_lambda, m_w_sp, m_b_sp, m_ln_v_g, m_ln_v_b, m_w_o_lru, m_w_o_sgu, m_w_out, m_ln1_g, m_ln1_b, m_w_up, m_w_down, m_ln2_g, m_ln2_b, v_w_ada, v_b_ada, v_w_in, v_b_in, v_w_conv, v_b_conv, v_w_rg_a, v_b_rg_a, v_w_rg_x, v_b_rg_x, v_lru_lambda, v_w_sp, v_b_sp, v_ln_v_g, v_ln_v_b, v_w_o_lru, v_w_o_sgu, v_w_out, v_ln1_g, v_ln1_b, v_w_up, v_w_down, v_ln2_g, v_ln2_b):
    W = dict(w_ada=w_ada, b_ada=b_ada, w_in=w_in, b_in=b_in, w_conv=w_conv, b_conv=b_conv, w_rg_a=w_rg_a,
             b_rg_a=b_rg_a, w_rg_x=w_rg_x, b_rg_x=b_rg_x, lru_lambda=lru_lambda, w_sp=w_sp, b_sp=b_sp,
             ln_v_g=ln_v_g, ln_v_b=ln_v_b, w_o_lru=w_o_lru, w_o_sgu=w_o_sgu, w_out=w_out, ln1_g=ln1_g, ln1_b=ln1_b,
             w_up=w_up, w_down=w_down, ln2_g=ln2_g, ln2_b=ln2_b)
    Mo = dict(w_ada=m_w_ada, b_ada=m_b_ada, w_in=m_w_in, b_in=m_b_in, w_conv=m_w_conv, b_conv=m_b_conv,
              w_rg_a=m_w_rg_a, b_rg_a=m_b_rg_a, w_rg_x=m_w_rg_x, b_rg_x=m_b_rg_x, lru_lambda=m_lru_lambda,
              w_sp=m_w_sp, b_sp=m_b_sp, ln_v_g=m_ln_v_g, ln_v_b=m_ln_v_b, w_o_lru=m_w_o_lru, w_o_sgu=m_w_o_sgu,
              w_out=m_w_out, ln1_g=m_ln1_g, ln1_b=m_ln1_b, w_up=m_w_up, w_down=m_w_down, ln2_g=m_ln2_g,
              ln2_b=m_ln2_b)
    Vo = dict(w_ada=v_w_ada, b_ada=v_b_ada, w_in=v_w_in, b_in=v_b_in, w_conv=v_w_conv, b_conv=v_b_conv,
              w_rg_a=v_w_rg_a, b_rg_a=v_b_rg_a, w_rg_x=v_w_rg_x, b_rg_x=v_b_rg_x, lru_lambda=v_lru_lambda,
              w_sp=v_w_sp, b_sp=v_b_sp, ln_v_g=v_ln_v_g, ln_v_b=v_ln_v_b, w_o_lru=v_w_o_lru, w_o_sgu=v_w_o_sgu,
              w_out=v_w_out, ln1_g=v_ln1_g, ln1_b=v_ln1_b, w_up=v_w_up, w_down=v_w_down, ln2_g=v_ln2_g,
              ln2_b=v_ln2_b)

    Bl, S, D = x.shape
    T = Bl * S
    lw = b_conv.shape[-1]
    sw = ln_v_g.shape[-1]
    din = b_in.shape[-1]
    dff = w_up.shape[-1] * N_DEV
    ts = min(2048, S)
    tmix = min(256, S)
    trow = min(512, S)

    c_pad = jnp.pad(c, ((0, SUBLANES - Bl), (0, 0)))
    c_g, wconv_g = _exchange([c_pad, w_conv[0]], True, "xchg_c")
    wconv_full = _unblock_cols(wconv_g)
    c_act, modcols = _ada_fwd(c_g.reshape(N_DEV * SUBLANES, D), w_ada[0])
    (mod_slots,) = _exchange([modcols.reshape(N_DEV, SUBLANES, -1)], False, "xchg_mod")

    nbw = din // N_DEV // WIN_PARTS
    wnames = tuple("win%d" % q for q in range(WIN_PARTS)) + ("wol", "wos", "wout", "wup", "wdown")
    shards = [w_in[0][:, q * nbw:(q + 1) * nbw].astype(BF16) for q in range(WIN_PARTS)] + [
        w_o_lru[0].astype(BF16), w_o_sgu[0].astype(BF16), w_out[0].astype(BF16), w_up[0].astype(BF16),
        w_down[0].astype(BF16)]
    col_sharded = [True] * WIN_PARTS + [False, True, False, True, False]
    g_send, g_recv, g_src, g_land, g_tok = _xstart(shards, True, mod_slots, "gather_start", cols=col_sharded)
    gidx = {n: i for i, n in enumerate(wnames)}

    def gathered(n, after):
        i = gidx[n]
        return _xwait(g_src[i], g_land[i], g_send[i], g_recv[i], after, True, "gather_wait_" + n, col=col_sharded[i])

    mod = _unblock_cols(mod_slots)[:Bl] + (b_ada + g_tok[0, 0])
    sh1, sc1, gt1, sh2, sc2, gt2 = [mod[:, i * D:(i + 1) * D].reshape(Bl, 1, D) for i in range(6)]

    wa_b, wx_b = w_rg_a[0].astype(BF16), w_rg_x[0].astype(BF16)
    b_sp_t = jnp.transpose(b_sp[0])
    small_mix = (wconv_full, b_conv, wa_b, b_rg_a, wx_b, b_rg_x, lru_lambda, w_sp[0], b_sp_t, ln_v_g, ln_v_b)

    h = _modulate(x, sc1, sh1, ts)
    proj, win_parts = None, []
    for q in range(WIN_PARTS):
        wq = gathered("win%d" % q, h if q == 0 else proj)
        win_parts.append(wq)
        proj = _mm(h.reshape(T, D), wq, mode="nn", tm=8192, tn=nbw, tk=D, outs=[BF16], extras=[(b_in, "row")],
                   epilogue=lambda acc, ex: (acc + ex[0],), scatter=(WIN_PARTS, q, din), into=proj,
                   name="mm_proj%d" % q)
    proj3 = proj.reshape(Bl, S, din)
    hs, ya_pre, ysgu, *lru_saved = _mix_fwd(proj3, *small_mix, tm=tmix, lw=lw, sw=sw)
    Wol = gathered("wol", ya_pre).reshape(lw, D)
    Wos = gathered("wos", ysgu)
    y_a = _mm(ya_pre.reshape(T, lw), Wol, mode="nn", tm=2048, tn=D, tk=lw, outs=[BF16], name="mm_ya")
    x2d, tgt2d = x.reshape(T, D), loss_target.reshape(T, D)
    gate_cb = (din - 2 * D) // D

    def ep_merge(y_b, v):
        ya, ga, gb = [t.astype(F32) for t in v]
        yb = y_b.astype(BF16).astype(F32)
        return [yb, _sigmoid(ga) * ya + _sigmoid(gb) * yb]

    y_b, merged = _mm_rows(ysgu.reshape(T, sw), Wos, mode="nn", tm=trow, seq=S,
                           ins=[("tile", y_a), ("tilecol", proj, D, gate_cb), ("tilecol", proj, D, gate_cb + 1)],
                           outs=[("tile", BF16, D), ("tile", BF16, D)], epilogue=ep_merge, name="mm_yb_merge")
    Wout = gathered("wout", merged).reshape(D, D)

    def ep_ln1(mix_acc, v):
        x_, gt, g, b, sc, sh = v
        mixr = mix_acc.astype(BF16).astype(F32)
        xhat, rstd = _ln_stats(ALPHA * x_ + (1.0 + gt) * mixr)
        x1_ = xhat * g + b
        return [mixr, x1_, x1_ * (1.0 + sc) + sh, xhat, jnp.broadcast_to(rstd, (rstd.shape[0], LANES))]

    mix, x1, h2, xhat1, rstd1 = _mm_rows(
        merged, Wout, mode="nn", tm=trow, seq=S,
        ins=[("tile", x2d), ("brow", gt1), ("row", ln1_g), ("row", ln1_b), ("brow", sc2), ("brow", sh2)],
        outs=[("tile", BF16, D), ("tile", F32, D), ("tile", BF16, D), ("tile", BF16, D), ("tile", F32, LANES)],
        epilogue=ep_ln1, name="mm_mix_ln1")
    Wup = gathered("wup", h2)
    def ep_up(up, ex):
        r = jnp.maximum(up, 0.0)
        return r * r, r + r

    act, dact_dup = _mm(h2, Wup, mode="nn", tm=2048, tn=1024, tk=D, outs=[BF16, BF16], epilogue=ep_up, name="mm_up")
    Wdown = gathered("wdown", act).reshape(dff, D)

    def ep_ln2(f_acc, v):
        x1_, t_, gt, g, b = v
        xhat, rstd = _ln_stats(ALPHA * x1_ + (1.0 + gt) * f_acc)
        err = xhat * g + b - t_
        loss_t = 0.5 * jnp.sum(jnp.mean(err * err, axis=-1, keepdims=True))
        dy = err * (1.0 / D)
        dz = _ln_bwd(dy, xhat, rstd, g)
        return [dz * (1.0 + gt), ALPHA * dz, _colsum(dz * f_acc), _colsum(dy * xhat), _colsum(dy), loss_t]

    df2, dx1p, dgt2, dg2, db2, loss_part = _mm_rows(
        act, Wdown, mode="nn", tm=trow, seq=S,
        ins=[("tile", x1), ("tile", tgt2d), ("brow", gt2), ("row", ln2_g), ("row", ln2_b)],
        outs=[("tile", BF16, D), ("tile", F32, D), ("acc_brow", D), ("acc_row", D), ("acc_row", D), ("acc_scalar",)],
        epilogue=ep_ln2, name="mm_down_ln2")
    loss = lax.psum(loss_part[0, 0], ("x", "y", "c"))

    def send_grads(parts, name):
        snd, rcv, src, land, tok = _xstart(parts, False, None, name + "_start")
        return [(src[i], land[i], snd[i], rcv[i]) for i in range(len(parts))], tok

    dup = _mm(df2, Wdown, mode="nt", tm=2048, tn=1024, tk=D, outs=[BF16], extras=[(dact_dup, "tile")],
              epilogue=lambda acc, ex: (acc * ex[0].astype(F32),), name="mm_dup")
    g_wdown = _mm(act, df2, mode="tn", tm=1024, tn=D, tk=2048, outs=[BF16], name="mm_gwdown")
    (x_wdown,), tok = send_grads([g_wdown.reshape(N_DEV, dff // N_DEV, D)], "gx_wdown")
    def ep_ln1_bwd(dh2, v):
        dx1p_, x1_, xh_, rs_, mix_, sc, gt, g = v
        mixv = mix_.astype(F32)
        dx1 = dx1p_ + dh2 * (1.0 + sc)
        xhat, rstd = xh_.astype(F32), rs_[:, 0:1]
        dz = _ln_bwd(dx1, xhat, rstd, g)
        return [ALPHA * dz, dz * (1.0 + gt), _colsum(dh2 * x1_), _colsum(dh2), _colsum(dz * mixv),
                _colsum(dx1 * xhat), _colsum(dx1)]

    dxp, dmix, dsc2, dsh2, dgt1, dg1, db1 = _mm_rows(
        dup, Wup, mode="nt", tm=trow, seq=S, tok=tok,
        ins=[("tile", dx1p), ("tile", x1), ("tile", xhat1), ("tile", rstd1), ("tile", mix), ("brow", sc2), ("brow", gt1),
             ("row", ln1_g)],
        outs=[("tile", F32, D), ("tile", BF16, D), ("acc_brow", D), ("acc_brow", D), ("acc_brow", D), ("acc_row", D),
              ("acc_row", D)],
        epilogue=ep_ln1_bwd, name="mm_dh2_ln1b")
    g_wup = _mm(h2, dup, mode="tn", tm=D, tn=1024, tk=2048, outs=[BF16], nb=dff // N_DEV, name="mm_gwup")
    (x_wup,), tok = send_grads([g_wup], "gx_wup")

    def ep_merge_bwd(dm, v):
        ya, yb, ga, gb = [t.astype(F32) for t in v]
        sa, sb = _sigmoid(ga), _sigmoid(gb)
        dg = jnp.concatenate([dm * ya * sa * (1.0 - sa), dm * yb * sb * (1.0 - sb)], axis=1)
        return [dm * sa, dm * sb, dg, _colsum(dg)]

    dy_a, dy_b, dproj, dbin_hi = _mm_rows(
        dmix, Wout, mode="nt", tm=trow, seq=S, tok=tok,
        ins=[("tile", y_a), ("tile", y_b), ("tilecol", proj, D, gate_cb), ("tilecol", proj, D, gate_cb + 1)],
        outs=[("tile", BF16, D), ("tile", BF16, D), ("tilecol", BF16, 2 * D, gate_cb // 2, din), ("acc_row", 2 * D)],
        epilogue=ep_merge_bwd, name="mm_dmerged_mb")
    g_wout = _mm(merged, dmix, mode="tn", tm=D, tn=D, tk=2048, outs=[BF16], name="mm_gwout")
    (x_wout,), tok = send_grads([g_wout.reshape(N_DEV, D // N_DEV, D)], "gx_wout")
    dya_pre = _mm(dy_a, Wol, mode="nt", tm=2048, tn=lw, tk=D, outs=[BF16], tok=tok, name="mm_dya")
    dysgu = _mm(dy_b, Wos, mode="nt", tm=2048, tn=sw, tk=D, outs=[BF16], name="mm_dys")
    g_wol = _mm(ya_pre.reshape(T, lw), dy_a, mode="tn", tm=lw, tn=D, tk=2048, outs=[BF16], name="mm_gwol")
    g_wos = _mm(ysgu.reshape(T, sw), dy_b, mode="tn", tm=sw, tn=D, tk=2048, outs=[BF16], nb=D // N_DEV,
                name="mm_gwos")
    (x_wol, x_wos), tok = send_grads([g_wol.reshape(N_DEV, lw // N_DEV, D), g_wos], "gx_wo")
    small_mix_b = (wconv_full, b_conv + tok[0, 0]) + small_mix[2:]
    (dproj, dbin_lo, g_wconv, g_bconv, g_wa, g_ba, g_wx, g_bx, g_lam, g_wsp, g_bsp_t, g_lvg, g_lvb) = _mix_bwd(
        proj3, hs, dya_pre.reshape(Bl, S, lw), dysgu.reshape(Bl, S, sw), dproj.reshape(Bl, S, din), lru_saved,
        *small_mix_b, tm=tmix, lw=lw, sw=sw)
    dproj2 = dproj.reshape(T, din)
    small_names = [n for n in SMALL_NAMES if n != "b_ada"]
    small_g = dict(b_in=jnp.concatenate([dbin_lo, dbin_hi], axis=-1), b_conv=g_bconv, w_rg_a=g_wa[None], b_rg_a=g_ba,
                   w_rg_x=g_wx[None], b_rg_x=g_bx, lru_lambda=g_lam, w_sp=g_wsp[None],
                   b_sp=jnp.transpose(g_bsp_t)[None], ln_v_g=g_lvg, ln_v_b=g_lvb, ln1_g=dg1, ln1_b=db1, ln2_g=dg2,
                   ln2_b=db2)
    gs_snd, gs_rcv, gs_src, gs_land, tok_s = _xstart([small_g[n] for n in small_names], True, None, "gsmall_start",
                                                      fill_own=False)
    g_win = _mm(h.reshape(T, D), dproj2, mode="tn", tm=D, tn=din // 4, tk=2048, outs=[BF16], nb=din // N_DEV,
                tok=tok_s, name="mm_gwin")
    (x_win,), tok = send_grads([g_win], "gx_win")

    def ep_final(dh, v):
        dxp_, x_, sc = v
        return [dxp_ + dh * (1.0 + sc), _colsum(dh * x_), _colsum(dh)]

    grad_x, dsc1, dsh1 = _mm_rows(dproj2, win_parts, mode="nt", tm=trow, seq=S, tok=tok,
                                  ins=[("tile", dxp), ("tile", x2d), ("brow", sc1)],
                                  outs=[("tile", F32, D), ("acc_brow", D), ("acc_brow", D)], epilogue=ep_final,
                                  name="mm_dh_final")
    grad_x = grad_x.reshape(Bl, S, D)

    out_g, out_d, out_m, out_v = {}, {}, {}, {}

    def adam(name, g_slots, tr, own=None):
        shp = W[name].shape
        w2, m2, v2 = [t.reshape(g_slots.shape[1:]) for t in (W[name], Mo[name], Vo[name])]
        g, d, mn, vn = _adamw(w2, g_slots, m2, v2, tr=tr, name="adam_" + name, own=own)
        out_g[name], out_d[name], out_m[name], out_v[name] = [t.reshape(shp) for t in (g, d, mn, vn)]

    def adam_exchanged(name, handle, tr, after):
        own, slots = _xwait(*handle, after, False, "gx_%s_wait" % name, place=False)
        adam(name, slots, tr, own=own)

    adam_exchanged("w_down", x_wdown, 256, dsh1)
    adam_exchanged("w_up", x_wup, 256, dsh1)
    adam_exchanged("w_out", x_wout, 128, dsh1)
    adam_exchanged("w_o_lru", x_wol, 160, dsh1)
    adam_exchanged("w_o_sgu", x_wos, 256, dsh1)
    gs_own, gs_slots = _xwait_many(gs_src, gs_land, gs_snd, gs_rcv, dsh1, "gsmall_wait")
    res_small = _adamw_many([W[n] for n in small_names], gs_slots, gs_own, [Mo[n] for n in small_names],
                            [Vo[n] for n in small_names], name="adam_small")
    for dst, vals in zip((out_g, out_d, out_m, out_v), res_small):
        dst.update(dict(zip(small_names, vals)))

    dmod = jnp.concatenate([dsh1, dsc1, dgt1, dsh2, dsc2, dgt2], axis=-1).reshape(Bl, 6 * D)
    dmod_b = _blocked_cols(jnp.pad(dmod, ((0, SUBLANES - Bl), (0, 0))))
    dmod_s, gwconv_s = _exchange([dmod_b, _blocked_cols(g_wconv)], False, "xchg_dmod", after=out_g["ln2_b"])
    g_wada, g_bada_mine = _ada_bwd(c_act, dmod_s.reshape(N_DEV * SUBLANES, -1))
    (g_bada_all,) = _exchange([g_bada_mine], True, "xchg_bada")
    adam("w_ada", g_wada[None], 256)
    adam("b_ada", g_bada_all.reshape(1, 1, 6 * D), 1)
    adam("w_conv", gwconv_s, 8)
    adam_exchanged("w_in", x_win, 256, g_bada_all)

    return (loss, grad_x, *[out_g[n] for n in WEIGHT_ORDER], *[out_d[n] for n in WEIGHT_ORDER],
            *[out_m[n] for n in WEIGHT_ORDER], *[out_v[n] for n in WEIGHT_ORDER])
```

```python
import math

import jax
import jax.numpy as jnp
from jax import lax
from jax.experimental import pallas as pl
from jax.experimental.pallas import tpu as pltpu

N_DEV = 8
LN_EPS = 1e-5
LRU_C = 8.0
CHUNK = 64
SGU_BLOCK = 128
ALPHA = 2.0 ** 0.25
ADAM_LR = 0.001
ADAM_B1 = 0.9
ADAM_B2 = 0.999
ADAM_EPS = 1e-08
ADAM_WD = 0.01
ADAM_STEP = 10
GELU_K0 = math.sqrt(2.0 / math.pi)
GELU_K1 = 0.044715

SUBLANES = 8
LANES = 128
VMEM_LIMIT = 56 * 1024 * 1024
WIN_PARTS = 3

F32 = jnp.float32
BF16 = jnp.bfloat16
MESH = pl.DeviceIdType.MESH


def _cparams(n_axes, big=False):
    return pltpu.CompilerParams(dimension_semantics=("arbitrary",) * n_axes,
                                vmem_limit_bytes=VMEM_LIMIT if big else None)


def _sigmoid(x):
    return 0.5 * jnp.tanh(0.5 * x) + 0.5


def _gelu(x):
    t = jnp.tanh(x * (GELU_K0 + (GELU_K0 * GELU_K1) * (x * x)))
    hx = 0.5 * x
    return hx + hx * t


def _gelu_and_grad(x):
    x2 = x * x
    t = jnp.tanh(x * (GELU_K0 + (GELU_K0 * GELU_K1) * x2))
    hx = 0.5 * x
    g = hx + hx * t
    dg = (0.5 + 0.5 * t) + (hx * (1.0 - t * t)) * (GELU_K0 + (3.0 * GELU_K0 * GELU_K1) * x2)
    return g, dg


def _log1p_pos(e):
    p = e * (1.0 - e * (1.0 / 2.0) + e * e * (1.0 / 3.0) - e * e * e * (1.0 / 4.0))
    return jnp.where(e < 1e-2, p, jnp.log(1.0 + e))


def _ln_stats(z):
    mu = jnp.mean(z, axis=-1, keepdims=True)
    zc = z - mu
    var = jnp.mean(zc * zc, axis=-1, keepdims=True)
    rstd = lax.rsqrt(var + LN_EPS)
    return zc * rstd, rstd


def _ln_bwd(dy, xhat, rstd, g):
    dxh = dy * g
    m1 = jnp.mean(dxh, axis=-1, keepdims=True)
    m2 = jnp.mean(dxh * xhat, axis=-1, keepdims=True)
    return rstd * (dxh - m1 - xhat * m2)


def _colsum(v):
    return jnp.sum(v, axis=0, keepdims=True)


def _first_step():
    return jnp.logical_and(pl.program_id(0) == 0, pl.program_id(1) == 0)


def _exchange(arrs, gather, name, after=None):
    n = len(arrs)
    n_peer = N_DEV - 1
    n_after = 0 if after is None else 1

    def body(*refs):
        ins, outs = refs[:n], refs[n + n_after:2 * n + n_after]
        send_sems, recv_sems, loc_sems = refs[2 * n + n_after:]
        x, y, c = lax.axis_index("x"), lax.axis_index("y"), lax.axis_index("c")
        me = 4 * x + 2 * y + c
        started = []
        for a in range(n):
            src_me = ins[a] if gather else ins[a].at[me]
            lc = pltpu.make_async_copy(src_me, outs[a].at[me], loc_sems.at[a])
            lc.start()
            started.append((lc, None))
        for p in range(1, N_DEV):
            px, py, pc = x ^ ((p >> 2) & 1), y ^ ((p >> 1) & 1), c ^ (p & 1)
            peer = 4 * px + 2 * py + pc
            for a in range(n):
                k = a * n_peer + (p - 1)
                src = ins[a] if gather else ins[a].at[peer]
                cp = pltpu.make_async_remote_copy(src_ref=src, dst_ref=outs[a].at[me],
                                                  send_sem=send_sems.at[k], recv_sem=recv_sems.at[k],
                                                  device_id=(px, py, pc), device_id_type=MESH)
                cp.start()
                rc = pltpu.make_async_remote_copy(src_ref=src, dst_ref=outs[a].at[peer],
                                                  send_sem=send_sems.at[k], recv_sem=recv_sems.at[k],
                                                  device_id=(px, py, pc), device_id_type=MESH)
                started.append((cp, rc))
        for cp, rc in started:
            if rc is None:
                cp.wait()
            else:
                cp.wait_send()
                rc.wait_recv()

    hbm = pl.BlockSpec(memory_space=pltpu.HBM)
    out_shape = tuple(
        jax.ShapeDtypeStruct(((N_DEV,) + a.shape) if gather else a.shape, a.dtype) for a in arrs)
    return pl.pallas_call(
        body, name=name, out_shape=out_shape,
        in_specs=[hbm] * n + [pl.BlockSpec(memory_space=pl.ANY)] * n_after, out_specs=tuple([hbm] * n),
        scratch_shapes=[pltpu.SemaphoreType.DMA((n * n_peer,)), pltpu.SemaphoreType.DMA((n * n_peer,)),
                        pltpu.SemaphoreType.DMA((n,))],
        compiler_params=pltpu.CompilerParams(has_side_effects=True),
    )(*arrs, *([after] if n_after else []))


_HBM = pl.BlockSpec(memory_space=pltpu.HBM)
_SEM = pl.BlockSpec(memory_space=pltpu.SEMAPHORE)
_EFFECT = pltpu.SideEffectType.DATAFLOW_SIDE_EFFECTING


def _peer_of(p):
    x, y, c = lax.axis_index("x"), lax.axis_index("y"), lax.axis_index("c")
    px, py, pc = x ^ ((p >> 2) & 1), y ^ ((p >> 1) & 1), c ^ (p & 1)
    return (px, py, pc), 4 * px + 2 * py + pc


def _slot(land_ref, idx, width):
    if width is None:
        return land_ref.at[idx]
    return land_ref.at[:, pl.ds(pl.multiple_of(idx * width, LANES), width)]


def _xstart(srcs, gather, after, name, cols=None, fill_own=False):
    n = len(srcs)
    cols = cols or [False] * n
    widths = [t.shape[1] if cols[a] else None for a, t in enumerate(srcs)]
    me_out = 4 * lax.axis_index("x") + 2 * lax.axis_index("y") + lax.axis_index("c")
    lands = []
    for a, t in enumerate(srcs):
        if cols[a]:
            zone, own, at = lax.empty((t.shape[0], N_DEV * t.shape[1]), t.dtype), t, (0, me_out * t.shape[1])
        elif gather:
            zone, own, at = lax.empty((N_DEV,) + t.shape, t.dtype), t[None], (me_out,) + (0,) * t.ndim
        else:
            zone, own = lax.empty(t.shape, t.dtype), lax.dynamic_index_in_dim(t, me_out, 0, keepdims=True)
            at = (me_out,) + (0,) * (t.ndim - 1)
        lands.append(lax.dynamic_update_slice(zone, own, at) if fill_own else zone)
    n_after = 0 if after is None else 1

    def body(*refs):
        src_refs, land_refs = refs[:n], refs[n:2 * n]
        refs = refs[n_after:]
        send_sems, recv_sems = refs[2 * n:3 * n], refs[3 * n:4 * n]
        token = refs[6 * n]
        me = 4 * lax.axis_index("x") + 2 * lax.axis_index("y") + lax.axis_index("c")
        for a in range(n):
            for p in range(1, N_DEV):
                dev, peer = _peer_of(p)
                pltpu.make_async_remote_copy(
                    src_ref=src_refs[a] if gather else src_refs[a].at[peer], dst_ref=_slot(land_refs[a], me, widths[a]),
                    send_sem=send_sems[a].at[p - 1], recv_sem=recv_sems[a].at[p - 1],
                    device_id=dev, device_id_type=MESH).start()
        token[...] = jnp.zeros_like(token)

    sems = tuple(pltpu.SemaphoreType.DMA((N_DEV - 1,)) for _ in range(2 * n))
    thru = tuple(pltpu.HBM(t.shape, t.dtype) for t in list(srcs) + list(lands))
    res = pl.pallas_call(
        body, name=name,
        out_shape=sems + thru + (jax.ShapeDtypeStruct((SUBLANES, LANES), F32),),
        in_specs=[_HBM] * (2 * n) + [pl.BlockSpec(memory_space=pl.ANY)] * n_after,
        out_specs=tuple([_SEM] * (2 * n) + [_HBM] * (2 * n) + [pl.BlockSpec(memory_space=pltpu.VMEM)]),
        input_output_aliases={i: 2 * n + i for i in range(2 * n)},
        compiler_params=pltpu.CompilerParams(has_side_effects=_EFFECT),
    )(*[pltpu.with_memory_space_constraint(t, pltpu.HBM) for t in list(srcs) + list(lands)],
      *([after] if n_after else []))
    return res[:n], res[n:2 * n], res[2 * n:3 * n], res[3 * n:4 * n], res[4 * n]


def _xwait(src, land, send_sem, recv_sem, after, gather, name, col=False, place=True):
    width = src.shape[1] if col else None

    def body(src_ref, land_ref, send_ref, recv_ref, after_ref, src_dead, land_out):
        del after_ref, src_dead, land_out
        for p in range(1, N_DEV):
            dev, peer = _peer_of(p)
            cp = pltpu.make_async_remote_copy(
                src_ref=src_ref if gather else src_ref.at[peer], dst_ref=_slot(land_ref, peer, width),
                send_sem=send_ref.at[p - 1], recv_sem=recv_ref.at[p - 1], device_id=dev, device_id_type=MESH)
            cp.wait_send()
            cp.wait_recv()

    src_done, landed = pl.pallas_call(
        body, name=name, out_shape=(pltpu.HBM(src.shape, src.dtype), pltpu.HBM(land.shape, land.dtype)),
        in_specs=[_HBM, _HBM, _SEM, _SEM, pl.BlockSpec(memory_space=pl.ANY)], out_specs=(_HBM, _HBM),
        input_output_aliases={0: 0, 1: 1},
        compiler_params=pltpu.CompilerParams(has_side_effects=_EFFECT),
    )(src, land, send_sem, recv_sem, after)
    if not place:
        return src_done, landed
    me = 4 * lax.axis_index("x") + 2 * lax.axis_index("y") + lax.axis_index("c")
    return _place_own(landed, src_done, me, col, gather, name + "_own")


def _place_own(zone, src, me, col, gather, name):
    if col:
        R, C = src.shape
        src_spec = lambda tr: pl.BlockSpec((tr, C), lambda i, me_ref: (i, 0))
        out_spec = lambda tr: pl.BlockSpec((tr, C), lambda i, me_ref: (i, me_ref[0]))
    else:
        R, C = zone.shape[1:]
        src_spec = ((lambda tr: pl.BlockSpec((tr, C), lambda i, me_ref: (i, 0))) if gather else
                    (lambda tr: pl.BlockSpec((None, tr, C), lambda i, me_ref: (me_ref[0], i, 0))))
        out_spec = lambda tr: pl.BlockSpec((None, tr, C), lambda i, me_ref: (me_ref[0], i, 0))
    tr = R if R <= 512 else 256
    assert R % tr == 0, (name, R, tr)

    def body(me_ref, src_ref, zone_ref, out_ref):
        del me_ref, zone_ref
        out_ref[...] = src_ref[...]

    return pl.pallas_call(
        body, name=name, out_shape=jax.ShapeDtypeStruct(zone.shape, zone.dtype),
        grid_spec=pltpu.PrefetchScalarGridSpec(
            num_scalar_prefetch=1, grid=(R // tr,),
            in_specs=[src_spec(tr), pl.BlockSpec(memory_space=pl.ANY)], out_specs=out_spec(tr)),
        input_output_aliases={2: 0},
    )(jnp.reshape(me, (1,)).astype(jnp.int32), src, zone)


def _xwait_many(srcs, lands, send_sems, recv_sems, after, name):
    n = len(srcs)

    def body(*refs):
        src_refs, land_refs = refs[:n], refs[n:2 * n]
        snd, rcv = refs[2 * n:3 * n], refs[3 * n:4 * n]
        for a in range(n):
            for p in range(1, N_DEV):
                dev, peer = _peer_of(p)
                cp = pltpu.make_async_remote_copy(
                    src_ref=src_refs[a], dst_ref=land_refs[a].at[peer], send_sem=snd[a].at[p - 1],
                    recv_sem=rcv[a].at[p - 1], device_id=dev, device_id_type=MESH)
                cp.wait_send()
                cp.wait_recv()

    res = pl.pallas_call(
        body, name=name, out_shape=tuple(pltpu.HBM(t.shape, t.dtype) for t in list(srcs) + list(lands)),
        in_specs=[_HBM] * (2 * n) + [_SEM] * (2 * n) + [pl.BlockSpec(memory_space=pl.ANY)],
        out_specs=tuple([_HBM] * (2 * n)), input_output_aliases={i: i for i in range(2 * n)},
        compiler_params=pltpu.CompilerParams(has_side_effects=_EFFECT),
    )(*srcs, *lands, *send_sems, *recv_sems, after)
    return res[:n], res[n:]


def _mm(a, b, *, mode, tm, tn, tk, outs, epilogue=None, extras=(), nb=None, tok=None, scatter=None, into=None, name):
    if mode == "nn":
        (M, K), (_, N) = a.shape, b.shape
    elif mode == "nt":
        (M, K), (N, _) = a.shape, b.shape
    else:
        (K, M), (_, N) = a.shape, b.shape
    tm, tn, tk = min(tm, M), min(tn, N), min(tk, K)
    assert M % tm == 0 and N % tn == 0 and K % tk == 0, (name, M, N, K, tm, tn, tk)
    if mode == "nn":
        a_spec = pl.BlockSpec((tm, tk), lambda i, j, k: (i, k))
        b_spec = pl.BlockSpec((tk, tn), lambda i, j, k: (k, j))
        dims = (((1,), (0,)), ((), ()))
    elif mode == "nt":
        a_spec = pl.BlockSpec((tm, tk), lambda i, j, k: (i, k))
        b_spec = pl.BlockSpec((tn, tk), lambda i, j, k: (j, k))
        dims = (((1,), (1,)), ((), ()))
    else:
        a_spec = pl.BlockSpec((tk, tm), lambda i, j, k: (k, i))
        b_spec = pl.BlockSpec((tk, tn), lambda i, j, k: (k, j))
        dims = (((0,), (0,)), ((), ()))
    nk = K // tk
    n_ex, n_out = len(extras), len(outs)
    n_tok = 0 if tok is None else 1
    nbytes = lambda d: jnp.dtype(d).itemsize
    vmem_est = (2 * (tm * tk * nbytes(a.dtype) + tk * tn * nbytes(b.dtype)
                     + sum(tm * tn * nbytes(e.dtype) for e, kind in extras if kind == "tile")
                     + sum(tm * tn * nbytes(d) for d in outs)) + tm * tn * 4)
    assert vmem_est <= VMEM_LIMIT, (name, vmem_est)
    if epilogue is None:
        epilogue = lambda acc, ex: tuple(acc.astype(d) for d in outs)

    n_into = 0 if into is None else 1

    def body(a_ref, b_ref, *refs):
        refs = refs[n_tok:]
        ex_refs, out_refs = refs[:n_ex], refs[n_ex + n_into:n_ex + n_into + n_out]

        def finish(acc):
            res = epilogue(acc, [r[...] for r in ex_refs])
            for o_ref, v in zip(out_refs, res):
                if nb is None:
                    o_ref[...] = v.astype(o_ref.dtype)
                else:
                    for q in range(tn // nb):
                        o_ref[q] = v[:, q * nb:(q + 1) * nb].astype(o_ref.dtype)

        part = lax.dot_general(a_ref[...], b_ref[...], dims, preferred_element_type=F32)
        if nk == 1:
            finish(part)
        else:
            acc_ref = refs[n_ex + n_into + n_out]
            k = pl.program_id(2)

            @pl.when(k == 0)
            def _():
                acc_ref[...] = part

            @pl.when(k > 0)
            def _():
                acc_ref[...] += part

            @pl.when(k == nk - 1)
            def _():
                finish(acc_ref[...])

    col = (lambda j: j) if scatter is None else (lambda j: scatter[0] * j + scatter[1])
    ex_specs = [pl.BlockSpec((tm, tn), lambda i, j, k: (i, j)) if kind == "tile"
                else pl.BlockSpec((1, tn), lambda i, j, k: (0, col(j))) for _, kind in extras]
    if nb is not None:
        assert tn % nb == 0, (name, tn, nb)
        o_spec = pl.BlockSpec((tn // nb, tm, nb), lambda i, j, k: (j, i, 0))
        o_shape = (N // nb, M, nb)
    else:
        o_spec = pl.BlockSpec((tm, tn), lambda i, j, k: (i, col(j)))
        o_shape = (M, N if scatter is None else scatter[2])
    assert n_into == 0 or n_out == 1
    res = pl.pallas_call(
        body, name=name, grid=(M // tm, N // tn, nk),
        in_specs=[a_spec, b_spec] + [pl.BlockSpec((SUBLANES, LANES), lambda i, j, k: (0, 0))] * n_tok + ex_specs
                 + [pl.BlockSpec(memory_space=pl.ANY)] * n_into,
        out_specs=tuple([o_spec] * n_out),
        out_shape=tuple(jax.ShapeDtypeStruct(o_shape, d) for d in outs),
        input_output_aliases={2 + n_tok + n_ex: 0} if n_into else {},
        scratch_shapes=[pltpu.VMEM((tm, tn), F32)] if nk > 1 else [],
        compiler_params=_cparams(3, big=True),
    )(a, b, *([tok] if n_tok else []), *[e for e, _ in extras], *([into] if n_into else []))
    return res[0] if n_out == 1 else res


def _mm_rows(a, b, *, mode, tm, seq, ins, outs, epilogue, tok=None, name):
    M, K = a.shape
    b_parts = list(b) if isinstance(b, (list, tuple)) else [b]
    n_part = len(b_parts)
    assert n_part == 1 or mode == "nt"
    N = b_parts[0].shape[1] if mode == "nn" else b_parts[0].shape[0]
    tm = min(tm, M)
    assert M % tm == 0 and seq % tm == 0, (name, M, seq, tm)
    tpb = seq // tm
    n_b = M // seq
    dims = (((1,), (0,)), ((), ())) if mode == "nn" else (((1,), (1,)), ((), ()))
    n_tok = 0 if tok is None else 1
    n_in, n_out = len(ins), len(outs)

    in_specs, in_arrs = [], []
    for spec in ins:
        kind, arr = spec[0], spec[1]
        in_arrs.append(arr)
        if kind == "tile":
            in_specs.append(pl.BlockSpec((tm, arr.shape[1]), lambda i: (i, 0)))
        elif kind == "tilecol":
            in_specs.append(pl.BlockSpec((tm, spec[2]), lambda i, cb=spec[3]: (i, cb)))
        elif kind == "row":
            in_specs.append(pl.BlockSpec(arr.shape, lambda i: (0, 0)))
        else:
            in_specs.append(pl.BlockSpec((None, 1, arr.shape[2]), lambda i: (i // tpb, 0, 0)))
    out_specs, out_shapes = [], []
    for spec in outs:
        kind = spec[0]
        if kind == "tile":
            out_specs.append(pl.BlockSpec((tm, spec[2]), lambda i: (i, 0)))
            out_shapes.append(jax.ShapeDtypeStruct((M, spec[2]), spec[1]))
        elif kind == "tilecol":
            out_specs.append(pl.BlockSpec((tm, spec[2]), lambda i, cb=spec[3]: (i, cb)))
            out_shapes.append(jax.ShapeDtypeStruct((M, spec[4]), spec[1]))
        elif kind == "acc_row":
            out_specs.append(pl.BlockSpec((1, spec[1]), lambda i: (0, 0)))
            out_shapes.append(jax.ShapeDtypeStruct((1, spec[1]), F32))
        elif kind == "acc_brow":
            out_specs.append(pl.BlockSpec((None, 1, spec[1]), lambda i: (i // tpb, 0, 0)))
            out_shapes.append(jax.ShapeDtypeStruct((n_b, 1, spec[1]), F32))
        else:
            out_specs.append(pl.BlockSpec((SUBLANES, LANES), lambda i: (0, 0)))
            out_shapes.append(jax.ShapeDtypeStruct((SUBLANES, LANES), F32))

    def body(a_ref, *refs):
        b_refs, refs = refs[:n_part], refs[n_part + n_tok:]
        in_refs, out_refs = refs[:n_in], refs[n_in:n_in + n_out]
        i = pl.program_id(0)
        if n_part == 1:
            prod = lax.dot_general(a_ref[...], b_refs[0][...], dims, preferred_element_type=F32)
        else:
            w = b_parts[0].shape[1] // N_DEV
            prod = None
            for q in range(n_part):
                a_q = jnp.concatenate([a_ref[:, (n_part * j + q) * w:(n_part * j + q + 1) * w] for j in range(N_DEV)],
                                      axis=1)
                pq = lax.dot_general(a_q, b_refs[q][...], dims, preferred_element_type=F32)
                prod = pq if prod is None else prod + pq
        vals = epilogue(prod, [r[...] for r in in_refs])
        for spec, o_ref, v in zip(outs, out_refs, vals):
            kind = spec[0]
            if kind in ("tile", "tilecol"):
                o_ref[...] = v.astype(o_ref.dtype)
            else:
                first = (i % tpb == 0) if kind == "acc_brow" else (i == 0)

                @pl.when(first)
                def _(o_ref=o_ref, v=v):
                    o_ref[...] = jnp.broadcast_to(v, o_ref.shape)

                @pl.when(jnp.logical_not(first))
                def _(o_ref=o_ref, v=v):
                    o_ref[...] += v

    res = pl.pallas_call(
        body, name=name, grid=(M // tm,),
        in_specs=[pl.BlockSpec((tm, K), lambda i: (i, 0))]
                 + [pl.BlockSpec(bp.shape, lambda i: (0, 0), pipeline_mode=pl.Buffered(1)) for bp in b_parts]
                 + [pl.BlockSpec((SUBLANES, LANES), lambda i: (0, 0))] * n_tok + in_specs,
        out_specs=tuple(out_specs), out_shape=tuple(out_shapes),
        compiler_params=_cparams(1, big=True),
    )(a, *b_parts, *([tok] if n_tok else []), *in_arrs)
    return res


def _tok_spec(ts, width, col_block=0):
    return pl.BlockSpec((None, ts, width), lambda b, s: (b, s, col_block))


def _brow_spec(width):
    return pl.BlockSpec((None, 1, width), lambda b, s: (b, 0, 0))


def _modulate(x, sc, sh, ts):
    Bl, S, D = x.shape

    def body(x_ref, sc_ref, sh_ref, o_ref):
        o_ref[...] = (x_ref[...] * (1.0 + sc_ref[...]) + sh_ref[...]).astype(BF16)

    return pl.pallas_call(
        body, name="modulate", grid=(Bl, S // ts),
        in_specs=[_tok_spec(ts, D), _brow_spec(D), _brow_spec(D)],
        out_specs=_tok_spec(ts, D), out_shape=jax.ShapeDtypeStruct((Bl, S, D), BF16),
        compiler_params=_cparams(2),
    )(x, sc, sh)


def _mix_fwd(proj, w_conv, b_conv, w_rg_a, b_rg_a, w_rg_x, b_rg_x, lam, w_sp, b_sp_t, ln_v_g, ln_v_b, *, tm, lw, sw):
    Bl, S, _ = proj.shape
    heads, hd = w_rg_a.shape[0], w_rg_a.shape[1]
    groups = w_sp.shape[0]
    cw = 2 * lw + 2 * sw
    nblk = tm // SGU_BLOCK

    G = tm // SUBLANES
    nc = lw // LANES

    def body(p_ref, wc_ref, bc_ref, wa_ref, ba_ref, wx_ref, bx_ref, lam_ref, wsp_ref, bsp_ref, lg_ref, lb_ref,
             hs_ref, ya_ref, ys_ref, xc_ref, r_ref, ig_ref, a_ref, m_ref,
             xext, hnat, hcar, h7_scr, a7_scr, hp_scr, h0_scr, cp_scr):
        s = pl.program_id(1)

        @pl.when(s == 0)
        def _():
            xext[:, 0:SUBLANES, :] = jnp.zeros((nc, SUBLANES, LANES), F32)
            hcar[...] = jnp.zeros_like(hcar)

        @pl.when(s > 0)
        def _():
            xext[:, 0:SUBLANES, :] = xext[:, tm:tm + SUBLANES, :]

        nl = -lam_ref[...]
        big_l = -LRU_C * (jnp.maximum(nl, 0.0) + _log1p_pos(jnp.exp(-jnp.abs(nl))))

        for c in range(nc):
            cs = slice(c * LANES, (c + 1) * LANES)
            xext[c, SUBLANES:SUBLANES + tm, :] = p_ref[:, cs].astype(F32)
            xs = {st: xext[c, pl.ds(st, G, stride=SUBLANES), :] for st in range(SUBLANES - 3, 2 * SUBLANES)}
            wcs = [wc_ref[k:k + 1, cs] for k in range(4)]
            xc_j = []
            for j in range(SUBLANES):
                acc = bc_ref[:, cs] + xs[SUBLANES + j] * wcs[3]
                for k in (1, 2, 3):
                    acc = acc + xs[SUBLANES + j - k] * wcs[3 - k]
                xc_j.append(acc)
                xc_ref[j * G:(j + 1) * G, cs] = acc
            xcb = jnp.concatenate(xc_j, axis=0).astype(BF16)
            pa = jnp.dot(xcb, wa_ref[c], preferred_element_type=F32)
            px = jnp.dot(xcb, wx_ref[c], preferred_element_type=F32)
            h0 = cp = None
            for j in range(SUBLANES):
                rs = slice(j * G, (j + 1) * G)
                r = _sigmoid(pa[rs] + ba_ref[:, cs])
                ig = _sigmoid(px[rs] + bx_ref[:, cs])
                la = big_l[:, cs] * r
                a = jnp.exp(la)
                th = jnp.tanh(la)
                msq = (-2.0 * th) * pl.reciprocal(1.0 - th, approx=True)
                m = msq * lax.rsqrt(jnp.maximum(msq, 1e-30))
                b = m * (ig * xc_j[j])
                r_ref[rs, cs] = r
                ig_ref[rs, cs] = ig
                a_ref[rs, cs] = a
                m_ref[rs, cs] = m
                h0 = b if j == 0 else a * h0 + b
                cp = a if j == 0 else a * cp
                h0_scr[rs, cs] = h0
                cp_scr[rs, cs] = cp
            h7_scr[:, cs] = h0
            a7_scr[:, cs] = cp
        carry = hcar[0:1, :]
        for g in range(G):
            hp_scr[g:g + 1, :] = carry
            carry = h7_scr[g:g + 1, :] + a7_scr[g:g + 1, :] * carry
        hcar[0:1, :] = carry
        for c in range(nc):
            cs = slice(c * LANES, (c + 1) * LANES)
            hprev = hp_scr[:, cs]
            for j in range(SUBLANES):
                rs = slice(j * G, (j + 1) * G)
                hnat[c, pl.ds(j, G, stride=SUBLANES), :] = h0_scr[rs, cs] + cp_scr[rs, cs] * hprev
            hs = hnat[c]
            hs_ref[:, cs] = hs
            ya_ref[:, cs] = (hs * _gelu(p_ref[:, lw + c * LANES:lw + (c + 1) * LANES].astype(F32))).astype(BF16)

        gu = _gelu(p_ref[:, 2 * lw:2 * lw + sw].astype(F32))
        gv = _gelu(p_ref[:, 2 * lw + sw:cw].astype(F32))
        xhat, _ = _ln_stats(gv)
        vn = (xhat * lg_ref[...] + lb_ref[...]).astype(BF16)
        tpos = lax.broadcasted_iota(jnp.int32, (SGU_BLOCK, SGU_BLOCK), 0) // CHUNK
        spos = lax.broadcasted_iota(jnp.int32, (SGU_BLOCK, SGU_BLOCK), 1) // CHUNK
        gw = sw // groups
        rows_out = []
        for blk in range(nblk):
            r0 = blk * SGU_BLOCK
            cols = []
            for g in range(groups):
                wm = jnp.where(spos <= tpos, wsp_ref[g], 0.0).astype(BF16)
                mixed = jnp.dot(wm, vn[r0:r0 + SGU_BLOCK, g * gw:(g + 1) * gw], preferred_element_type=F32)
                cols.append(mixed + bsp_ref[:, g:g + 1])
            rows_out.append(jnp.concatenate(cols, axis=1))
        mixed_all = jnp.concatenate(rows_out, axis=0) if nblk > 1 else rows_out[0]
        ys_ref[...] = (gu * mixed_all).astype(BF16)

    full = lambda shp: pl.BlockSpec(shp, lambda b, s: (0,) * len(shp))
    return pl.pallas_call(
        body, name="mix_fwd", grid=(Bl, S // tm),
        in_specs=[_tok_spec(tm, cw), full(w_conv.shape), full(b_conv.shape), full(w_rg_a.shape), full(b_rg_a.shape),
                  full(w_rg_x.shape), full(b_rg_x.shape), full(lam.shape), full(w_sp.shape), full(b_sp_t.shape),
                  full(ln_v_g.shape), full(ln_v_b.shape)],
        out_specs=(_tok_spec(tm, lw), _tok_spec(tm, lw), _tok_spec(tm, sw)) + (_tok_spec(tm, lw),) * 5,
        out_shape=(jax.ShapeDtypeStruct((Bl, S, lw), F32), jax.ShapeDtypeStruct((Bl, S, lw), BF16),
                   jax.ShapeDtypeStruct((Bl, S, sw), BF16)) + (jax.ShapeDtypeStruct((Bl, S, lw), F32),) * 5,
        scratch_shapes=[pltpu.VMEM((nc, tm + SUBLANES, LANES), F32), pltpu.VMEM((nc, tm, LANES), F32),
                        pltpu.VMEM((SUBLANES, lw), F32), pltpu.VMEM((G, lw), F32), pltpu.VMEM((G, lw), F32),
                        pltpu.VMEM((G, lw), F32), pltpu.VMEM((tm, lw), F32), pltpu.VMEM((tm, lw), F32)],
        compiler_params=_cparams(2, big=True),
    )(proj, w_conv, b_conv, w_rg_a, b_rg_a, w_rg_x, b_rg_x, lam, w_sp, b_sp_t, ln_v_g, ln_v_b)


def _mix_bwd(proj, hs, dya, dys, dproj, saved, w_conv, b_conv, w_rg_a, b_rg_a, w_rg_x, b_rg_x, lam, w_sp, b_sp_t,
             ln_v_g, ln_v_b, *, tm, lw, sw):
    Bl, S, din = proj.shape
    heads, hd = w_rg_a.shape[0], w_rg_a.shape[1]
    groups = w_sp.shape[0]
    gw = sw // groups
    cw = 2 * lw + 2 * sw
    nblk = tm // SGU_BLOCK
    n_s = S // tm
    per8 = tm // SUBLANES
    halo_rows = 2 * SUBLANES

    G = tm // SUBLANES
    nc = lw // LANES

    def body(p_ref, xh_ref, hs_ref, hh_ref, dya_ref, dys_ref, dpin_ref, xc_ref, r_ref, ig_ref, a_ref, m_ref,
             wc_ref, bc_ref, wa_ref, ba_ref, wx_ref, bx_ref, lam_ref, wsp_ref, bsp_ref, lg_ref, lb_ref,
             dp_ref, dbin_ref, dwc_ref, dbc_ref, dwa_ref, dba_ref, dwx_ref, dbx_ref, dlam_ref, dwsp_ref, dbsp_ref,
             dlg_ref, dlb_ref,
             xext, hext, dnat, dxext, dhcar, g00_scr, p0_scr, a0_scr, cin_scr, g0_scr, pp_scr):
        del dpin_ref
        sr = pl.program_id(1)
        first_tile = sr == n_s - 1

        @pl.when(_first_step())
        def _():
            for ref in (dbin_ref, dwc_ref, dbc_ref, dwa_ref, dba_ref, dwx_ref, dbx_ref, dlam_ref, dwsp_ref, dbsp_ref,
                        dlg_ref, dlb_ref):
                ref[...] = jnp.zeros_like(ref)

        @pl.when(sr == 0)
        def _():
            dhcar[...] = jnp.zeros_like(dhcar)
            dxext[:, tm:tm + SUBLANES, :] = jnp.zeros((nc, SUBLANES, LANES), F32)

        @pl.when(sr > 0)
        def _():
            dxext[:, tm:tm + SUBLANES, :] = dxext[:, 0:SUBLANES, :]

        keep = jnp.where(first_tile, 0.0, 1.0)
        xprev = xh_ref[...].astype(F32)[halo_rows - SUBLANES:halo_rows] * keep
        hprev8 = hh_ref[...] * keep
        nl = -lam_ref[...]
        big_l = -LRU_C * (jnp.maximum(nl, 0.0) + _log1p_pos(jnp.exp(-jnp.abs(nl))))
        dlam_scale = LRU_C * _sigmoid(nl)
        nt = (((1,), (1,)), ((), ()))
        tn = (((0,), (0,)), ((), ()))
        last = SUBLANES - 1

        for c in range(nc):
            cs = slice(c * LANES, (c + 1) * LANES)
            gcs = slice(lw + c * LANES, lw + (c + 1) * LANES)
            xext[c, 0:SUBLANES, :] = xprev[:, cs]
            xext[c, SUBLANES:SUBLANES + tm, :] = p_ref[:, cs].astype(F32)
            hext[c, 0:SUBLANES, :] = hprev8[:, cs]
            dgl_sum = None
            for i in range(SUBLANES):
                rs = slice(i * G, (i + 1) * G)
                ggl, dggl = _gelu_and_grad(p_ref[rs, gcs].astype(F32))
                dy = dya_ref[rs, cs].astype(F32)
                hsv = hs_ref[rs, cs]
                hext[c, SUBLANES + i * G:SUBLANES + (i + 1) * G, :] = hsv
                dgl = dy * hsv * dggl
                dp_ref[rs, gcs] = dgl.astype(BF16)
                dnat[c, rs, :] = dy * ggl
                dgl_sum = _colsum(dgl) if i == 0 else dgl_sum + _colsum(dgl)
            dbin_ref[:, gcs] += dgl_sum
            g0 = pp = None
            for j in range(last, -1, -1):
                rs = slice(j * G, (j + 1) * G)
                dhs_j = dnat[c, pl.ds(j, G, stride=SUBLANES), :]
                if j == last:
                    g0 = dhs_j
                else:
                    an = a_ref[(j + 1) * G:(j + 2) * G, cs]
                    g0 = dhs_j + an * g0
                    pp = an if j == last - 1 else an * pp
                    pp_scr[rs, cs] = pp
                g0_scr[rs, cs] = g0
            g00_scr[:, cs] = g0
            p0_scr[:, cs] = pp
            a0_scr[:, cs] = a_ref[0:G, cs]
        cin = dhcar[0:1, :]
        for g in range(G - 1, -1, -1):
            cin_scr[g:g + 1, :] = cin
            cin = a0_scr[g:g + 1, :] * (g00_scr[g:g + 1, :] + p0_scr[g:g + 1, :] * cin)
        dhcar[0:1, :] = cin
        for c in range(nc):
            cs = slice(c * LANES, (c + 1) * LANES)
            cinv = cin_scr[:, cs]
            dpa_j, dpx_j, dxc_j = [], [], []
            dlam_sum = dba_sum = dbx_sum = None
            for j in range(SUBLANES):
                rs = slice(j * G, (j + 1) * G)
                dh = g0_scr[rs, cs] + (cinv if j == last else pp_scr[rs, cs] * cinv)
                hprev = hext[c, pl.ds(last + j, G, stride=SUBLANES), :]
                xc, r, ig, a, m = xc_ref[rs, cs], r_ref[rs, cs], ig_ref[rs, cs], a_ref[rs, cs], m_ref[rs, cs]
                dixc = dh * m
                dla = (dh * hprev) * a - (dh * (ig * xc)) * ((a * a) * pl.reciprocal(m, approx=True))
                dpa = (dla * big_l[:, cs]) * r * (1.0 - r)
                dpx = (dixc * xc) * ig * (1.0 - ig)
                dpa_j.append(dpa)
                dpx_j.append(dpx)
                dxc_j.append(dixc * ig)
                sums = (_colsum(dla * r), _colsum(dpa), _colsum(dpx))
                dlam_sum, dba_sum, dbx_sum = sums if j == 0 else (dlam_sum + sums[0], dba_sum + sums[1], dbx_sum + sums[2])
            dlam_ref[:, cs] += dlam_sum * dlam_scale[:, cs]
            dba_ref[:, cs] += dba_sum
            dbx_ref[:, cs] += dbx_sum
            dpab = jnp.concatenate(dpa_j, axis=0).astype(BF16)
            dpxb = jnp.concatenate(dpx_j, axis=0).astype(BF16)
            xcb = xc_ref[:, cs].astype(BF16)
            dxc = (jnp.concatenate(dxc_j, axis=0)
                   + lax.dot_general(dpab, wa_ref[c], nt, preferred_element_type=F32)
                   + lax.dot_general(dpxb, wx_ref[c], nt, preferred_element_type=F32))
            dwa_ref[c] += lax.dot_general(xcb, dpab, tn, preferred_element_type=F32)
            dwx_ref[c] += lax.dot_general(xcb, dpxb, tn, preferred_element_type=F32)

            dbc_ref[:, cs] += _colsum(dxc)
            xs = {st: xext[c, pl.ds(st, G, stride=SUBLANES), :] for st in range(SUBLANES - 3, 2 * SUBLANES)}
            for k in range(4):
                tot = None
                for j in range(SUBLANES):
                    part = _colsum(dxc[j * G:(j + 1) * G] * xs[SUBLANES + j - (3 - k)])
                    tot = part if tot is None else tot + part
                dwc_ref[k:k + 1, cs] += tot
            for j in range(SUBLANES):
                dxext[c, pl.ds(j, G, stride=SUBLANES), :] = dxc[j * G:(j + 1) * G]
            us = {st: dxext[c, pl.ds(st, G, stride=SUBLANES), :] for st in range(SUBLANES + 3)}
            wcs = [wc_ref[k:k + 1, cs] for k in range(4)]
            for j in range(SUBLANES):
                acc = us[j] * wcs[3]
                for k in (1, 2, 3):
                    acc = acc + us[j + k] * wcs[3 - k]
                dnat[c, pl.ds(j, G, stride=SUBLANES), :] = acc
            dxl = dnat[c]
            dp_ref[:, cs] = dxl.astype(BF16)
            dbin_ref[:, cs] += _colsum(dxl)

        gu, dgu_dx = _gelu_and_grad(p_ref[:, 2 * lw:2 * lw + sw].astype(F32))
        gv, dgv_dx = _gelu_and_grad(p_ref[:, 2 * lw + sw:cw].astype(F32))
        xhat, rstd = _ln_stats(gv)
        vn = (xhat * lg_ref[...] + lb_ref[...]).astype(BF16)
        dys = dys_ref[...].astype(F32)
        dmixed = dys * gu
        dmb = dmixed.astype(BF16)
        tpos = lax.broadcasted_iota(jnp.int32, (SGU_BLOCK, SGU_BLOCK), 0) // CHUNK
        spos = lax.broadcasted_iota(jnp.int32, (SGU_BLOCK, SGU_BLOCK), 1) // CHUNK
        causal = spos <= tpos
        mixed_rows, dvn_rows = [], []
        for blk in range(nblk):
            rs = slice(blk * SGU_BLOCK, (blk + 1) * SGU_BLOCK)
            mcols, dcols = [], []
            for g in range(groups):
                cs = slice(g * gw, (g + 1) * gw)
                wm = jnp.where(causal, wsp_ref[g], 0.0).astype(BF16)
                mcols.append(jnp.dot(wm, vn[rs, cs], preferred_element_type=F32) + bsp_ref[:, g:g + 1])
                dcols.append(lax.dot_general(wm, dmb[rs, cs], tn, preferred_element_type=F32))
                dw = lax.dot_general(dmb[rs, cs], vn[rs, cs], nt, preferred_element_type=F32)
                dwsp_ref[g] += jnp.where(causal, dw, 0.0)
                dbsp_ref[:, g:g + 1] += jnp.sum(dmixed[rs, cs], axis=1, keepdims=True)
            mixed_rows.append(jnp.concatenate(mcols, axis=1))
            dvn_rows.append(jnp.concatenate(dcols, axis=1))
        mixed_all = jnp.concatenate(mixed_rows, axis=0) if nblk > 1 else mixed_rows[0]
        dvn = jnp.concatenate(dvn_rows, axis=0) if nblk > 1 else dvn_rows[0]
        du = dys * mixed_all * dgu_dx
        dlg_ref[...] += _colsum(dvn * xhat)
        dlb_ref[...] += _colsum(dvn)
        dv = _ln_bwd(dvn, xhat, rstd, lg_ref[...]) * dgv_dx
        dp_ref[:, 2 * lw:2 * lw + sw] = du.astype(BF16)
        dp_ref[:, 2 * lw + sw:cw] = dv.astype(BF16)
        dbin_ref[:, 2 * lw:2 * lw + sw] += _colsum(du)
        dbin_ref[:, 2 * lw + sw:cw] += _colsum(dv)

    rev = lambda s: n_s - 1 - s
    tile = lambda w: pl.BlockSpec((None, tm, w), lambda b, s: (b, rev(s), 0))
    halo = lambda w: pl.BlockSpec((None, SUBLANES, w), lambda b, s: (b, jnp.maximum(rev(s) * per8 - 1, 0), 0))
    xhalo = pl.BlockSpec((None, halo_rows, lw), lambda b, s: (b, jnp.maximum(rev(s) * (tm // halo_rows) - 1, 0), 0))
    full = lambda shp: pl.BlockSpec(shp, lambda b, s: (0,) * len(shp))
    small = [w_conv, b_conv, w_rg_a, b_rg_a, w_rg_x, b_rg_x, lam, w_sp, b_sp_t, ln_v_g, ln_v_b]
    acc_shapes = [(1, cw), w_conv.shape, b_conv.shape, w_rg_a.shape, b_rg_a.shape, w_rg_x.shape, b_rg_x.shape,
                  lam.shape, w_sp.shape, b_sp_t.shape, ln_v_g.shape, ln_v_b.shape]
    res = pl.pallas_call(
        body, name="mix_bwd", grid=(Bl, n_s),
        in_specs=[tile(cw), xhalo, tile(lw), halo(lw), tile(lw), tile(sw), pl.BlockSpec(memory_space=pl.ANY)]
                 + [tile(lw)] * 5 + [full(w.shape) for w in small],
        out_specs=tuple([tile(cw)] + [full(shp) for shp in acc_shapes]),
        out_shape=tuple([jax.ShapeDtypeStruct((Bl, S, din), BF16)] + [jax.ShapeDtypeStruct(shp, F32) for shp in acc_shapes]),
        input_output_aliases={6: 0},
        scratch_shapes=[pltpu.VMEM((nc, tm + SUBLANES, LANES), F32), pltpu.VMEM((nc, tm + SUBLANES, LANES), F32),
                        pltpu.VMEM((nc, tm, LANES), F32), pltpu.VMEM((nc, tm + SUBLANES, LANES), F32),
                        pltpu.VMEM((SUBLANES, lw), F32), pltpu.VMEM((G, lw), F32), pltpu.VMEM((G, lw), F32),
                        pltpu.VMEM((G, lw), F32), pltpu.VMEM((G, lw), F32), pltpu.VMEM((tm, lw), F32),
                        pltpu.VMEM((tm, lw), F32)],
        compiler_params=_cparams(2, big=True),
    )(proj, proj, hs, hs, dya, dys, dproj, *saved, *small)
    return res


def _ada_fwd(c_all, w_ada):
    R, D = c_all.shape
    nb = w_ada.shape[1]

    def body(c_ref, w_ref, act_ref, o_ref):
        cv = c_ref[...]
        act = (cv * _sigmoid(cv)).astype(BF16)
        act_ref[...] = act
        o_ref[...] = jnp.dot(act, w_ref[...].astype(BF16), preferred_element_type=F32)

    return pl.pallas_call(
        body, name="ada_fwd",
        out_shape=(jax.ShapeDtypeStruct((R, D), BF16), jax.ShapeDtypeStruct((R, nb), F32)),
        compiler_params=pltpu.CompilerParams(vmem_limit_bytes=VMEM_LIMIT),
    )(c_all, w_ada)


def _ada_bwd(c_act, dmod_cols):
    R, D = c_act.shape
    nb = dmod_cols.shape[1]

    def body(act_ref, d_ref, o_ref, b_ref):
        o_ref[...] = lax.dot_general(act_ref[...], d_ref[...].astype(BF16), (((0,), (0,)), ((), ())),
                                     preferred_element_type=F32)
        b_ref[...] = _colsum(d_ref[...])

    return pl.pallas_call(
        body, name="ada_bwd", out_shape=(jax.ShapeDtypeStruct((D, nb), F32), jax.ShapeDtypeStruct((1, nb), F32)),
        compiler_params=pltpu.CompilerParams(vmem_limit_bytes=VMEM_LIMIT),
    )(c_act, dmod_cols)


def _adamw(w, g_slots, m, v, *, tr, name, own=None):
    R, C = w.shape
    n_slot = g_slots.shape[0]
    tr = min(tr, R)
    assert R % tr == 0, (name, R, tr)
    c1 = 1.0 / (1.0 - ADAM_B1 ** ADAM_STEP)
    c2 = 1.0 / (1.0 - ADAM_B2 ** ADAM_STEP)
    n_own = 0 if own is None else 1

    def body(me_ref, w_ref, g_ref, *refs):
        m_ref, v_ref, go_ref, d_ref, mo_ref, vo_ref = refs[n_own:]
        slot = lambda d: (jnp.where(me_ref[0] == d, refs[0][...], g_ref[d]) if n_own else g_ref[d]).astype(F32)
        g = slot(0)
        for d in range(1, n_slot):
            g = g + slot(d)
        mn = ADAM_B1 * m_ref[...] + (1.0 - ADAM_B1) * g
        vn = ADAM_B2 * v_ref[...] + (1.0 - ADAM_B2) * (g * g)
        go_ref[...] = g
        mo_ref[...] = mn
        vo_ref[...] = vn
        d_ref[...] = -ADAM_LR * ((mn * c1) / (jnp.sqrt(vn * c2) + ADAM_EPS) + ADAM_WD * w_ref[...])

    me = 4 * lax.axis_index("x") + 2 * lax.axis_index("y") + lax.axis_index("c")
    blk = pl.BlockSpec((tr, C), lambda i, me_ref: (i, 0))
    own_specs = [pl.BlockSpec((None, tr, C), lambda i, me_ref: (me_ref[0], i, 0))] * n_own
    return pl.pallas_call(
        body, name=name, out_shape=tuple(jax.ShapeDtypeStruct((R, C), F32) for _ in range(4)),
        grid_spec=pltpu.PrefetchScalarGridSpec(
            num_scalar_prefetch=1, grid=(R // tr,),
            in_specs=[blk, pl.BlockSpec((n_slot, tr, C), lambda i, me_ref: (0, i, 0))] + own_specs + [blk, blk],
            out_specs=(blk, blk, blk, blk)),
        compiler_params=_cparams(1, big=True),
    )(jnp.reshape(me, (1,)).astype(jnp.int32), w, g_slots, *([own] if n_own else []), m, v)


def _adamw_many(ws, g_slots, g_owns, ms, vs, *, name):
    n = len(ws)
    c1 = 1.0 / (1.0 - ADAM_B1 ** ADAM_STEP)
    c2 = 1.0 / (1.0 - ADAM_B2 ** ADAM_STEP)

    def body(*refs):
        w_refs, g_refs, o_refs = refs[:n], refs[n:2 * n], refs[2 * n:3 * n]
        m_refs, v_refs = refs[3 * n:4 * n], refs[4 * n:5 * n]
        outs = refs[5 * n:]
        me = 4 * lax.axis_index("x") + 2 * lax.axis_index("y") + lax.axis_index("c")
        for i in range(n):
            own = o_refs[i][...]
            g = jnp.where(me == 0, own, g_refs[i][0])
            for d in range(1, N_DEV):
                g = g + jnp.where(me == d, own, g_refs[i][d])
            mn = ADAM_B1 * m_refs[i][...] + (1.0 - ADAM_B1) * g
            vn = ADAM_B2 * v_refs[i][...] + (1.0 - ADAM_B2) * (g * g)
            outs[i][...] = g
            outs[n + i][...] = -ADAM_LR * ((mn * c1) / (jnp.sqrt(vn * c2) + ADAM_EPS) + ADAM_WD * w_refs[i][...])
            outs[2 * n + i][...] = mn
            outs[3 * n + i][...] = vn

    res = pl.pallas_call(
        body, name=name, out_shape=tuple(jax.ShapeDtypeStruct(w.shape, F32) for _ in range(4) for w in ws),
        compiler_params=pltpu.CompilerParams(vmem_limit_bytes=VMEM_LIMIT),
    )(*ws, *g_slots, *g_owns, *ms, *vs)
    return res[:n], res[n:2 * n], res[2 * n:3 * n], res[3 * n:]


SMALL_NAMES = ("b_ada", "b_in", "b_conv", "w_rg_a", "b_rg_a", "w_rg_x", "b_rg_x", "lru_lambda", "w_sp", "b_sp",
               "ln_v_g", "ln_v_b", "ln1_g", "ln1_b", "ln2_g", "ln2_b")
WEIGHT_ORDER = ("w_ada", "b_ada", "w_in", "b_in", "w_conv", "b_conv", "w_rg_a", "b_rg_a", "w_rg_x", "b_rg_x",
                "lru_lambda", "w_sp", "b_sp", "ln_v_g", "ln_v_b", "w_o_lru", "w_o_sgu", "w_out", "ln1_g", "ln1_b",
                "w_up", "w_down", "ln2_g", "ln2_b")


def _blocked_cols(w2d):
    K, N = w2d.shape
    return jnp.transpose(w2d.reshape(K, N_DEV, N // N_DEV), (1, 0, 2))


def _unblock_cols(wb):
    n, K, nb = wb.shape
    return jnp.transpose(wb, (1, 0, 2)).reshape(K, n * nb)


def kernel(x, c, w_ada, b_ada, w_in, b_in, w_conv, b_conv, w_rg_a, b_rg_a, w_rg_x, b_rg_x, lru_lambda, w_sp, b_sp, ln_v_g, ln_v_b, w_o_lru, w_o_sgu, w_out, ln1_g, ln1_b, w_up, w_down, ln2_g, ln2_b, loss_target, m_w_ada, m_b_ada, m_w_in, m_b_in, m_w_conv, m_b_conv, m_w_rg_a, m_b_rg_a, m_w_rg_x, m_b_rg_x, m_lru_lambda, m_w_sp, m_b_sp, m_ln_v_g, m_ln_v_b, m_w_o_lru, m_w_o_sgu, m_w_out, m_ln1_g, m_ln1_b, m_w_up, m_w_down, m_ln2_g, m_ln2_b, v_w_ada, v_b_ada, v_w_in, v_b_in, v_w_conv, v_b_conv, v_w_rg_a, v_b_rg_a, v_w_rg_x, v_b_rg_x, v_lru_lambda, v_w_sp, v_b_sp, v_ln_v_g, v_ln_v_b, v_w_o_lru, v_w_o_sgu, v_w_out, v_ln1_g, v_ln1_b, v_w_up, v_w_down, v_ln2_g, v_ln2_b):
    W = dict(w_ada=w_ada, b_ada=b_ada, w_in=w_in, b_in=b_in, w_conv=w_conv, b_conv=b_conv, w_rg_a=w_rg_a,
             b_rg_a=b_rg_a, w_rg_x=w_rg_x, b_rg_x=b_rg_x, lru_lambda=lru_lambda, w_sp=w_sp, b_sp=b_sp,
             ln_v_g=ln_v_g, ln_v_b=ln_v_b, w_o_lru=w_o_lru, w_o_sgu=w_o_sgu, w_out=w_out, ln1_g=ln1_g, ln1_b=ln1_b,
             w_up=w_up, w_down=w_down, ln2_g=ln2_g, ln2_b=ln2_b)
    Mo = dict(w_ada=m_w_ada, b_ada=m_b_ada, w_in=m_w_in, b_in=m_b_in, w_conv=m_w_conv, b_conv=m_b_conv,
              w_rg_a=m_w_rg_a, b_rg_a=m_b_rg_a, w_rg_x=m_w_rg_x, b_rg_x=m_b_rg_x, lru_lambda=m_lru_lambda,
              w_sp=m_w_sp, b_sp=m_b_sp, ln_v_g=m_ln_v_g, ln_v_b=m_ln_v_b, w_o_lru=m_w_o_lru, w_o_sgu=m_w_o_sgu,
              w_out=m_w_out, ln1_g=m_ln1_g, ln1_b=m_ln1_b, w_up=m_w_up, w_down=m_w_down, ln2_g=m_ln2_g,
              ln2_b=m_ln2_b)
    Vo = dict(w_ada=v_w_ada, b_ada=v_b_ada, w_in=v_w_in, b_in=v_b_in, w_conv=v_w_conv, b_conv=v_b_conv,
              w_rg_a=v_w_rg_a, b_rg_a=v_b_rg_a, w_rg_x=v_w_rg_x, b_rg_x=v_b_rg_x, lru_lambda=v_lru_lambda,
              w_sp=v_w_sp, b_sp=v_b_sp, ln_v_g=v_ln_v_g, ln_v_b=v_ln_v_b, w_o_lru=v_w_o_lru, w_o_sgu=v_w_o_sgu,
              w_out=v_w_out, ln1_g=v_ln1_g, ln1_b=v_ln1_b, w_up=v_w_up, w_down=v_w_down, ln2_g=v_ln2_g,
              ln2_b=v_ln2_b)

    Bl, S, D = x.shape
    T = Bl * S
    lw = b_conv.shape[-1]
    sw = ln_v_g.shape[-1]
    din = b_in.shape[-1]
    dff = w_up.shape[-1] * N_DEV
    ts = min(2048, S)
    tmix = min(256, S)
    trow = min(512, S)

    c_pad = jnp.pad(c, ((0, SUBLANES - Bl), (0, 0)))
    c_g, wconv_g = _exchange([c_pad, w_conv[0]], True, "xchg_c")
    wconv_full = _unblock_cols(wconv_g)
    c_act, modcols = _ada_fwd(c_g.reshape(N_DEV * SUBLANES, D), w_ada[0])
    (mod_slots,) = _exchange([modcols.reshape(N_DEV, SUBLANES, -1)], False, "xchg_mod")

    nbw = din // N_DEV // WIN_PARTS
    wnames = tuple("win%d" % q for q in range(WIN_PARTS)) + ("wol", "wos", "wout", "wup", "wdown")
    shards = [w_in[0][:, q * nbw:(q + 1) * nbw].astype(BF16) for q in range(WIN_PARTS)] + [
        w_o_lru[0].astype(BF16), w_o_sgu[0].astype(BF16), w_out[0].astype(BF16), w_up[0].astype(BF16),
        w_down[0].astype(BF16)]
    col_sharded = [True] * WIN_PARTS + [False, True, False, True, False]
    g_send, g_recv, g_src, g_land, g_tok = _xstart(shards, True, mod_slots, "gather_start", cols=col_sharded)
    gidx = {n: i for i, n in enumerate(wnames)}

    def gathered(n, after):
        i = gidx[n]
        return _xwait(g_src[i], g_land[i], g_send[i], g_recv[i], after, True, "gather_wait_" + n, col=col_sharded[i])

    mod = _unblock_cols(mod_slots)[:Bl] + (b_ada + g_tok[0, 0])
    sh1, sc1, gt1, sh2, sc2, gt2 = [mod[:, i * D:(i + 1) * D].reshape(Bl, 1, D) for i in range(6)]

    wa_b, wx_b = w_rg_a[0].astype(BF16), w_rg_x[0].astype(BF16)
    b_sp_t = jnp.transpose(b_sp[0])
    small_mix = (wconv_full, b_conv, wa_b, b_rg_a, wx_b, b_rg_x, lru_lambda, w_sp[0], b_sp_t, ln_v_g, ln_v_b)

    h = _modulate(x, sc1, sh1, ts)
    proj, win_parts = None, []
    for q in range(WIN_PARTS):
        wq = gathered("win%d" % q, h if q == 0 else proj)
        win_parts.append(wq)
        proj = _mm(h.reshape(T, D), wq, mode="nn", tm=8192, tn=nbw, tk=D, outs=[BF16], extras=[(b_in, "row")],
                   epilogue=lambda acc, ex: (acc + ex[0],), scatter=(WIN_PARTS, q, din), into=proj,
                   name="mm_proj%d" % q)
    proj3 = proj.reshape(Bl, S, din)
    hs, ya_pre, ysgu, *lru_saved = _mix_fwd(proj3, *small_mix, tm=tmix, lw=lw, sw=sw)
    Wol = gathered("wol", ya_pre).reshape(lw, D)
    Wos = gathered("wos", ysgu)
    y_a = _mm(ya_pre.reshape(T, lw), Wol, mode="nn", tm=2048, tn=D, tk=lw, outs=[BF16], name="mm_ya")
    x2d, tgt2d = x.reshape(T, D), loss_target.reshape(T, D)
    gate_cb = (din - 2 * D) // D

    def ep_merge(y_b, v):
        ya, ga, gb = [t.astype(F32) for t in v]
        yb = y_b.astype(BF16).astype(F32)
        return [yb, _sigmoid(ga) * ya + _sigmoid(gb) * yb]

    y_b, merged = _mm_rows(ysgu.reshape(T, sw), Wos, mode="nn", tm=trow, seq=S,
                           ins=[("tile", y_a), ("tilecol", proj, D, gate_cb), ("tilecol", proj, D, gate_cb + 1)],
                           outs=[("tile", BF16, D), ("tile", BF16, D)], epilogue=ep_merge, name="mm_yb_merge")
    Wout = gathered("wout", merged).reshape(D, D)

    def ep_ln1(mix_acc, v):
        x_, gt, g, b, sc, sh = v
        mixr = mix_acc.astype(BF16).astype(F32)
        xhat, rstd = _ln_stats(ALPHA * x_ + (1.0 + gt) * mixr)
        x1_ = xhat * g + b
        return [mixr, x1_, x1_ * (1.0 + sc) + sh, xhat, jnp.broadcast_to(rstd, (rstd.shape[0], LANES))]

    mix, x1, h2, xhat1, rstd1 = _mm_rows(
        merged, Wout, mode="nn", tm=trow, seq=S,
        ins=[("tile", x2d), ("brow", gt1), ("row", ln1_g), ("row", ln1_b), ("brow", sc2), ("brow", sh2)],
        outs=[("tile", BF16, D), ("tile", F32, D), ("tile", BF16, D), ("tile", BF16, D), ("tile", F32, LANES)],
        epilogue=ep_ln1, name="mm_mix_ln1")
    Wup = gathered("wup", h2)
    def ep_up(up, ex):
        r = jnp.maximum(up, 0.0)
        return r * r, r + r

    act, dact_dup = _mm(h2, Wup, mode="nn", tm=2048, tn=1024, tk=D, outs=[BF16, BF16], epilogue=ep_up, name="mm_up")
    Wdown = gathered("wdown", act).reshape(dff, D)

    def ep_ln2(f_acc, v):
        x1_, t_, gt, g, b = v
        xhat, rstd = _ln_stats(ALPHA * x1_ + (1.0 + gt) * f_acc)
        err = xhat * g + b - t_
        loss_t = 0.5 * jnp.sum(jnp.mean(err * err, axis=-1, keepdims=True))
        dy = err * (1.0 / D)
        dz = _ln_bwd(dy, xhat, rstd, g)
        return [dz * (1.0 + gt), ALPHA * dz, _colsum(dz * f_acc), _colsum(dy * xhat), _colsum(dy), loss_t]

    df2, dx1p, dgt2, dg2, db2, loss_part = _mm_rows(
        act, Wdown, mode="nn", tm=trow, seq=S,
        ins=[("tile", x1), ("tile", tgt2d), ("brow", gt2), ("row", ln2_g), ("row", ln2_b)],
        outs=[("tile", BF16, D), ("tile", F32, D), ("acc_brow", D), ("acc_row", D), ("acc_row", D), ("acc_scalar",)],
        epilogue=ep_ln2, name="mm_down_ln2")
    loss = lax.psum(loss_part[0, 0], ("x", "y", "c"))

    def send_grads(parts, name):
        snd, rcv, src, land, tok = _xstart(parts, False, None, name + "_start")
        return [(src[i], land[i], snd[i], rcv[i]) for i in range(len(parts))], tok

    dup = _mm(df2, Wdown, mode="nt", tm=2048, tn=1024, tk=D, outs=[BF16], extras=[(dact_dup, "tile")],
              epilogue=lambda acc, ex: (acc * ex[0].astype(F32),), name="mm_dup")
    g_wdown = _mm(act, df2, mode="tn", tm=1024, tn=D, tk=2048, outs=[BF16], name="mm_gwdown")
    (x_wdown,), tok = send_grads([g_wdown.reshape(N_DEV, dff // N_DEV, D)], "gx_wdown")
    def ep_ln1_bwd(dh2, v):
        dx1p_, x1_, xh_, rs_, mix_, sc, gt, g = v
        mixv = mix_.astype(F32)
        dx1 = dx1p_ + dh2 * (1.0 + sc)
        xhat, rstd = xh_.astype(F32), rs_[:, 0:1]
        dz = _ln_bwd(dx1, xhat, rstd, g)
        return [ALPHA * dz, dz * (1.0 + gt), _colsum(dh2 * x1_), _colsum(dh2), _colsum(dz * mixv),
                _colsum(dx1 * xhat), _colsum(dx1)]

    dxp, dmix, dsc2, dsh2, dgt1, dg1, db1 = _mm_rows(
        dup, Wup, mode="nt", tm=trow, seq=S, tok=tok,
        ins=[("tile", dx1p), ("tile", x1), ("tile", xhat1), ("tile", rstd1), ("tile", mix), ("brow", sc2), ("brow", gt1),
             ("row", ln1_g)],
        outs=[("tile", F32, D), ("tile", BF16, D), ("acc_brow", D), ("acc_brow", D), ("acc_brow", D), ("acc_row", D),
              ("acc_row", D)],
        epilogue=ep_ln1_bwd, name="mm_dh2_ln1b")
    g_wup = _mm(h2, dup, mode="tn", tm=D, tn=1024, tk=2048, outs=[BF16], nb=dff // N_DEV, name="mm_gwup")
    (x_wup,), tok = send_grads([g_wup], "gx_wup")

    def ep_merge_bwd(dm, v):
        ya, yb, ga, gb = [t.astype(F32) for t in v]
        sa, sb = _sigmoid(ga), _sigmoid(gb)
        dg = jnp.concatenate([dm * ya * sa * (1.0 - sa), dm * yb * sb * (1.0 - sb)], axis=1)
        return [dm * sa, dm * sb, dg, _colsum(dg)]

    dy_a, dy_b, dproj, dbin_hi = _mm_rows(
        dmix, Wout, mode="nt", tm=trow, seq=S, tok=tok,
        ins=[("tile", y_a), ("tile", y_b), ("tilecol", proj, D, gate_cb), ("tilecol", proj, D, gate_cb + 1)],
        outs=[("tile", BF16, D), ("tile", BF16, D), ("tilecol", BF16, 2 * D, gate_cb // 2, din), ("acc_row", 2 * D)],
        epilogue=ep_merge_bwd, name="mm_dmerged_mb")
    g_wout = _mm(merged, dmix, mode="tn", tm=D, tn=D, tk=2048, outs=[BF16], name="mm_gwout")
    (x_wout,), tok = send_grads([g_wout.reshape(N_DEV, D // N_DEV, D)], "gx_wout")
    dya_pre = _mm(dy_a, Wol, mode="nt", tm=2048, tn=lw, tk=D, outs=[BF16], tok=tok, name="mm_dya")
    dysgu = _mm(dy_b, Wos, mode="nt", tm=2048, tn=sw, tk=D, outs=[BF16], name="mm_dys")
    g_wol = _mm(ya_pre.reshape(T, lw), dy_a, mode="tn", tm=lw, tn=D, tk=2048, outs=[BF16], name="mm_gwol")
    g_wos = _mm(ysgu.reshape(T, sw), dy_b, mode="tn", tm=sw, tn=D, tk=2048, outs=[BF16], nb=D // N_DEV,
                name="mm_gwos")
    (x_wol, x_wos), tok = send_grads([g_wol.reshape(N_DEV, lw // N_DEV, D), g_wos], "gx_wo")
    small_mix_b = (wconv_full, b_conv + tok[0, 0]) + small_mix[2:]
    (dproj, dbin_lo, g_wconv, g_bconv, g_wa, g_ba, g_wx, g_bx, g_lam, g_wsp, g_bsp_t, g_lvg, g_lvb) = _mix_bwd(
        proj3, hs, dya_pre.reshape(Bl, S, lw), dysgu.reshape(Bl, S, sw), dproj.reshape(Bl, S, din), lru_saved,
        *small_mix_b, tm=tmix, lw=lw, sw=sw)
    dproj2 = dproj.reshape(T, din)
    small_names = [n for n in SMALL_NAMES if n != "b_ada"]
    small_g = dict(b_in=jnp.concatenate([dbin_lo, dbin_hi], axis=-1), b_conv=g_bconv, w_rg_a=g_wa[None], b_rg_a=g_ba,
                   w_rg_x=g_wx[None], b_rg_x=g_bx, lru_lambda=g_lam, w_sp=g_wsp[None],
                   b_sp=jnp.transpose(g_bsp_t)[None], ln_v_g=g_lvg, ln_v_b=g_lvb, ln1_g=dg1, ln1_b=db1, ln2_g=dg2,
                   ln2_b=db2)
    gs_snd, gs_rcv, gs_src, gs_land, tok_s = _xstart([small_g[n] for n in small_names], True, None, "gsmall_start",
                                                      fill_own=False)
    g_win = _mm(h.reshape(T, D), dproj2, mode="tn", tm=D, tn=din // 4, tk=2048, outs=[BF16], nb=din // N_DEV,
                tok=tok_s, name="mm_gwin")
    (x_win,), tok = send_grads([g_win], "gx_win")

    def ep_final(dh, v):
        dxp_, x_, sc = v
        return [dxp_ + dh * (1.0 + sc), _colsum(dh * x_), _colsum(dh)]

    grad_x, dsc1, dsh1 = _mm_rows(dproj2, win_parts, mode="nt", tm=trow, seq=S, tok=tok,
                                  ins=[("tile", dxp), ("tile", x2d), ("brow", sc1)],
                                  outs=[("tile", F32, D), ("acc_brow", D), ("acc_brow", D)], epilogue=ep_final,
                                  name="mm_dh_final")
    grad_x = grad_x.reshape(Bl, S, D)

    out_g, out_d, out_m, out_v = {}, {}, {}, {}

    def adam(name, g_slots, tr, own=None):
        shp = W[name].shape
        w2, m2, v2 = [t.reshape(g_slots.shape[1:]) for t in (W[name], Mo[name], Vo[name])]
        g, d, mn, vn = _adamw(w2, g_slots, m2, v2, tr=tr, name="adam_" + name, own=own)
        out_g[name], out_d[name], out_m[name], out_v[name] = [t.reshape(shp) for t in (g, d, mn, vn)]

    def adam_exchanged(name, handle, tr, after):
        own, slots = _xwait(*handle, after, False, "gx_%s_wait" % name, place=False)
        adam(name, slots, tr, own=own)

    adam_exchanged("w_down", x_wdown, 256, dsh1)
    adam_exchanged("w_up", x_wup, 256, dsh1)
    adam_exchanged("w_out", x_wout, 128, dsh1)
    adam_exchanged("w_o_lru", x_wol, 160, dsh1)
    adam_exchanged("w_o_sgu", x_wos, 256, dsh1)
    gs_own, gs_slots = _xwait_many(gs_src, gs_land, gs_snd, gs_rcv, dsh1, "gsmall_wait")
    res_small = _adamw_many([W[n] for n in small_names], gs_slots, gs_own, [Mo[n] for n in small_names],
                            [Vo[n] for n in small_names], name="adam_small")
    for dst, vals in zip((out_g, out_d, out_m, out_v), res_small):
        dst.update(dict(zip(small_names, vals)))

    dmod = jnp.concatenate([dsh1, dsc1, dgt1, dsh2, dsc2, dgt2], axis=-1).reshape(Bl, 6 * D)
    dmod_b = _blocked_cols(jnp.pad(dmod, ((0, SUBLANES - Bl), (0, 0))))
    dmod_s, gwconv_s = _exchange([dmod_b, _blocked_cols(g_wconv)], False, "xchg_dmod", after=out_g["ln2_b"])
    g_wada, g_bada_mine = _ada_bwd(c_act, dmod_s.reshape(N_DEV * SUBLANES, -1))
    (g_bada_all,) = _exchange([g_bada_mine], True, "xchg_bada")
    adam("w_ada", g_wada[None], 256)
    adam("b_ada", g_bada_all.reshape(1, 1, 6 * D), 1)
    adam("w_conv", gwconv_s, 8)
    adam_exchanged("w_in", x_win, 256, g_bada_all)

    return (loss, grad_x, *[out_g[n] for n in WEIGHT_ORDER], *[out_d[n] for n in WEIGHT_ORDER],
            *[out_m[n] for n in WEIGHT_ORDER], *[out_v[n] for n in WEIGHT_ORDER])
```

```python
import math

import jax
import jax.numpy as jnp
from jax import lax
from jax.experimental import pallas as pl
from jax.experimental.pallas import tpu as pltpu

N_DEV = 8
LN_EPS = 1e-5
LRU_C = 8.0
CHUNK = 64
SGU_BLOCK = 128
ALPHA = 2.0 ** 0.25
ADAM_LR = 0.001
ADAM_B1 = 0.9
ADAM_B2 = 0.999
ADAM_EPS = 1e-08
ADAM_WD = 0.01
ADAM_STEP = 10
GELU_K0 = math.sqrt(2.0 / math.pi)
GELU_K1 = 0.044715

SUBLANES = 8
LANES = 128
VMEM_LIMIT = 56 * 1024 * 1024
WIN_PARTS = 3

F32 = jnp.float32
BF16 = jnp.bfloat16
MESH = pl.DeviceIdType.MESH


def _cparams(n_axes, big=False):
    return pltpu.CompilerParams(dimension_semantics=("arbitrary",) * n_axes,
                                vmem_limit_bytes=VMEM_LIMIT if big else None)


def _sigmoid(x):
    return 0.5 * jnp.tanh(0.5 * x) + 0.5


def _gelu(x):
    t = jnp.tanh(x * (GELU_K0 + (GELU_K0 * GELU_K1) * (x * x)))
    hx = 0.5 * x
    return hx + hx * t


def _gelu_and_grad(x):
    x2 = x * x
    t = jnp.tanh(x * (GELU_K0 + (GELU_K0 * GELU_K1) * x2))
    hx = 0.5 * x
    g = hx + hx * t
    dg = (0.5 + 0.5 * t) + (hx * (1.0 - t * t)) * (GELU_K0 + (3.0 * GELU_K0 * GELU_K1) * x2)
    return g, dg


def _log1p_pos(e):
    p = e * (1.0 - e * (1.0 / 2.0) + e * e * (1.0 / 3.0) - e * e * e * (1.0 / 4.0))
    return jnp.where(e < 1e-2, p, jnp.log(1.0 + e))


def _ln_stats(z):
    mu = jnp.mean(z, axis=-1, keepdims=True)
    zc = z - mu
    var = jnp.mean(zc * zc, axis=-1, keepdims=True)
    rstd = lax.rsqrt(var + LN_EPS)
    return zc * rstd, rstd


def _ln_bwd(dy, xhat, rstd, g):
    dxh = dy * g
    m1 = jnp.mean(dxh, axis=-1, keepdims=True)
    m2 = jnp.mean(dxh * xhat, axis=-1, keepdims=True)
    return rstd * (dxh - m1 - xhat * m2)


def _colsum(v):
    return jnp.sum(v, axis=0, keepdims=True)


def _fold8(v):
    out = v[0:SUBLANES]
    for i in range(1, v.shape[0] // SUBLANES):
        out = out + v[i * SUBLANES:(i + 1) * SUBLANES]
    return out


def _first_step():
    return jnp.logical_and(pl.program_id(0) == 0, pl.program_id(1) == 0)


def _exchange(arrs, gather, name, after=None):
    n = len(arrs)
    n_peer = N_DEV - 1
    n_after = 0 if after is None else 1

    def body(*refs):
        ins, outs = refs[:n], refs[n + n_after:2 * n + n_after]
        send_sems, recv_sems, loc_sems = refs[2 * n + n_after:]
        x, y, c = lax.axis_index("x"), lax.axis_index("y"), lax.axis_index("c")
        me = 4 * x + 2 * y + c
        started = []
        for a in range(n):
            src_me = ins[a] if gather else ins[a].at[me]
            lc = pltpu.make_async_copy(src_me, outs[a].at[me], loc_sems.at[a])
            lc.start()
            started.append((lc, None))
        for p in range(1, N_DEV):
            px, py, pc = x ^ ((p >> 2) & 1), y ^ ((p >> 1) & 1), c ^ (p & 1)
            peer = 4 * px + 2 * py + pc
            for a in range(n):
                k = a * n_peer + (p - 1)
                src = ins[a] if gather else ins[a].at[peer]
                cp = pltpu.make_async_remote_copy(src_ref=src, dst_ref=outs[a].at[me],
                                                  send_sem=send_sems.at[k], recv_sem=recv_sems.at[k],
                                                  device_id=(px, py, pc), device_id_type=MESH)
                cp.start()
                rc = pltpu.make_async_remote_copy(src_ref=src, dst_ref=outs[a].at[peer],
                                                  send_sem=send_sems.at[k], recv_sem=recv_sems.at[k],
                                                  device_id=(px, py, pc), device_id_type=MESH)
                started.append((cp, rc))
        for cp, rc in started:
            if rc is None:
                cp.wait()
            else:
                cp.wait_send()
                rc.wait_recv()

    hbm = pl.BlockSpec(memory_space=pltpu.HBM)
    out_shape = tuple(
        jax.ShapeDtypeStruct(((N_DEV,) + a.shape) if gather else a.shape, a.dtype) for a in arrs)
    return pl.pallas_call(
        body, name=name, out_shape=out_shape,
        in_specs=[hbm] * n + [pl.BlockSpec(memory_space=pl.ANY)] * n_after, out_specs=tuple([hbm] * n),
        scratch_shapes=[pltpu.SemaphoreType.DMA((n * n_peer,)), pltpu.SemaphoreType.DMA((n * n_peer,)),
                        pltpu.SemaphoreType.DMA((n,))],
        compiler_params=pltpu.CompilerParams(has_side_effects=True),
    )(*arrs, *([after] if n_after else []))


_HBM = pl.BlockSpec(memory_space=pltpu.HBM)
_SEM = pl.BlockSpec(memory_space=pltpu.SEMAPHORE)
_EFFECT = pltpu.SideEffectType.DATAFLOW_SIDE_EFFECTING


def _peer_of(p):
    x, y, c = lax.axis_index("x"), lax.axis_index("y"), lax.axis_index("c")
    px, py, pc = x ^ ((p >> 2) & 1), y ^ ((p >> 1) & 1), c ^ (p & 1)
    return (px, py, pc), 4 * px + 2 * py + pc


def _slot(land_ref, idx, width):
    if width is None:
        return land_ref.at[idx]
    return land_ref.at[:, pl.ds(pl.multiple_of(idx * width, LANES), width)]


def _xstart(srcs, gather, after, name, cols=None, fill_own=False):
    n = len(srcs)
    cols = cols or [False] * n
    widths = [t.shape[1] if cols[a] else None for a, t in enumerate(srcs)]
    me_out = 4 * lax.axis_index("x") + 2 * lax.axis_index("y") + lax.axis_index("c")
    lands = []
    for a, t in enumerate(srcs):
        if cols[a]:
            zone, own, at = lax.empty((t.shape[0], N_DEV * t.shape[1]), t.dtype), t, (0, me_out * t.shape[1])
        elif gather:
            zone, own, at = lax.empty((N_DEV,) + t.shape, t.dtype), t[None], (me_out,) + (0,) * t.ndim
        else:
            zone, own = lax.empty(t.shape, t.dtype), lax.dynamic_index_in_dim(t, me_out, 0, keepdims=True)
            at = (me_out,) + (0,) * (t.ndim - 1)
        lands.append(lax.dynamic_update_slice(zone, own, at) if fill_own else zone)
    n_after = 0 if after is None else 1

    def body(*refs):
        src_refs, land_refs = refs[:n], refs[n:2 * n]
        refs = refs[n_after:]
        send_sems, recv_sems = refs[2 * n:3 * n], refs[3 * n:4 * n]
        token = refs[6 * n]
        me = 4 * lax.axis_index("x") + 2 * lax.axis_index("y") + lax.axis_index("c")
        for a in range(n):
            for p in range(1, N_DEV):
                dev, peer = _peer_of(p)
                pltpu.make_async_remote_copy(
                    src_ref=src_refs[a] if gather else src_refs[a].at[peer], dst_ref=_slot(land_refs[a], me, widths[a]),
                    send_sem=send_sems[a].at[p - 1], recv_sem=recv_sems[a].at[p - 1],
                    device_id=dev, device_id_type=MESH).start()
        token[...] = jnp.zeros_like(token)

    sems = tuple(pltpu.SemaphoreType.DMA((N_DEV - 1,)) for _ in range(2 * n))
    thru = tuple(pltpu.HBM(t.shape, t.dtype) for t in list(srcs) + list(lands))
    res = pl.pallas_call(
        body, name=name,
        out_shape=sems + thru + (jax.ShapeDtypeStruct((SUBLANES, LANES), F32),),
        in_specs=[_HBM] * (2 * n) + [pl.BlockSpec(memory_space=pl.ANY)] * n_after,
        out_specs=tuple([_SEM] * (2 * n) + [_HBM] * (2 * n) + [pl.BlockSpec(memory_space=pltpu.VMEM)]),
        input_output_aliases={i: 2 * n + i for i in range(2 * n)},
        compiler_params=pltpu.CompilerParams(has_side_effects=_EFFECT),
    )(*[pltpu.with_memory_space_constraint(t, pltpu.HBM) for t in list(srcs) + list(lands)],
      *([after] if n_after else []))
    return res[:n], res[n:2 * n], res[2 * n:3 * n], res[3 * n:4 * n], res[4 * n]


def _xwait(src, land, send_sem, recv_sem, after, gather, name, col=False, place=True):
    width = src.shape[1] if col else None

    def body(src_ref, land_ref, send_ref, recv_ref, after_ref, src_dead, land_out):
        del after_ref, src_dead, land_out
        for p in range(1, N_DEV):
            dev, peer = _peer_of(p)
            cp = pltpu.make_async_remote_copy(
                src_ref=src_ref if gather else src_ref.at[peer], dst_ref=_slot(land_ref, peer, width),
                send_sem=send_ref.at[p - 1], recv_sem=recv_ref.at[p - 1], device_id=dev, device_id_type=MESH)
            cp.wait_send()
            cp.wait_recv()

    src_done, landed = pl.pallas_call(
        body, name=name, out_shape=(pltpu.HBM(src.shape, src.dtype), pltpu.HBM(land.shape, land.dtype)),
        in_specs=[_HBM, _HBM, _SEM, _SEM, pl.BlockSpec(memory_space=pl.ANY)], out_specs=(_HBM, _HBM),
        input_output_aliases={0: 0, 1: 1},
        compiler_params=pltpu.CompilerParams(has_side_effects=_EFFECT),
    )(src, land, send_sem, recv_sem, after)
    if not place:
        return src_done, landed
    me = 4 * lax.axis_index("x") + 2 * lax.axis_index("y") + lax.axis_index("c")
    return _place_own(landed, src_done, me, col, gather, name + "_own")


def _place_own(zone, src, me, col, gather, name):
    if col:
        R, C = src.shape
        src_spec = lambda tr: pl.BlockSpec((tr, C), lambda i, me_ref: (i, 0))
        out_spec = lambda tr: pl.BlockSpec((tr, C), lambda i, me_ref: (i, me_ref[0]))
    else:
        R, C = zone.shape[1:]
        src_spec = ((lambda tr: pl.BlockSpec((tr, C), lambda i, me_ref: (i, 0))) if gather else
                    (lambda tr: pl.BlockSpec((None, tr, C), lambda i, me_ref: (me_ref[0], i, 0))))
        out_spec = lambda tr: pl.BlockSpec((None, tr, C), lambda i, me_ref: (me_ref[0], i, 0))
    tr = R if R <= 512 else 256
    assert R % tr == 0, (name, R, tr)

    def body(me_ref, src_ref, zone_ref, out_ref):
        del me_ref, zone_ref
        out_ref[...] = src_ref[...]

    return pl.pallas_call(
        body, name=name, out_shape=jax.ShapeDtypeStruct(zone.shape, zone.dtype),
        grid_spec=pltpu.PrefetchScalarGridSpec(
            num_scalar_prefetch=1, grid=(R // tr,),
            in_specs=[src_spec(tr), pl.BlockSpec(memory_space=pl.ANY)], out_specs=out_spec(tr)),
        input_output_aliases={2: 0},
    )(jnp.reshape(me, (1,)).astype(jnp.int32), src, zone)


def _xwait_many(srcs, lands, send_sems, recv_sems, after, name):
    n = len(srcs)

    def body(*refs):
        src_refs, land_refs = refs[:n], refs[n:2 * n]
        snd, rcv = refs[2 * n:3 * n], refs[3 * n:4 * n]
        for a in range(n):
            for p in range(1, N_DEV):
                dev, peer = _peer_of(p)
                cp = pltpu.make_async_remote_copy(
                    src_ref=src_refs[a], dst_ref=land_refs[a].at[peer], send_sem=snd[a].at[p - 1],
                    recv_sem=rcv[a].at[p - 1], device_id=dev, device_id_type=MESH)
                cp.wait_send()
                cp.wait_recv()

    res = pl.pallas_call(
        body, name=name, out_shape=tuple(pltpu.HBM(t.shape, t.dtype) for t in list(srcs) + list(lands)),
        in_specs=[_HBM] * (2 * n) + [_SEM] * (2 * n) + [pl.BlockSpec(memory_space=pl.ANY)],
        out_specs=tuple([_HBM] * (2 * n)), input_output_aliases={i: i for i in range(2 * n)},
        compiler_params=pltpu.CompilerParams(has_side_effects=_EFFECT),
    )(*srcs, *lands, *send_sems, *recv_sems, after)
    return res[:n], res[n:]


def _mm(a, b, *, mode, tm, tn, tk, outs, epilogue=None, extras=(), nb=None, tok=None, scatter=None, into=None, name):
    if mode == "nn":
        (M, K), (_, N) = a.shape, b.shape
    elif mode == "nt":
        (M, K), (N, _) = a.shape, b.shape
    else:
        (K, M), (_, N) = a.shape, b.shape
    tm, tn, tk = min(tm, M), min(tn, N), min(tk, K)
    assert M % tm == 0 and N % tn == 0 and K % tk == 0, (name, M, N, K, tm, tn, tk)
    if mode == "nn":
        a_spec = pl.BlockSpec((tm, tk), lambda i, j, k: (i, k))
        b_spec = pl.BlockSpec((tk, tn), lambda i, j, k: (k, j))
        dims = (((1,), (0,)), ((), ()))
    elif mode == "nt":
        a_spec = pl.BlockSpec((tm, tk), lambda i, j, k: (i, k))
        b_spec = pl.BlockSpec((tn, tk), lambda i, j, k: (j, k))
        dims = (((1,), (1,)), ((), ()))
    else:
        a_spec = pl.BlockSpec((tk, tm), lambda i, j, k: (k, i))
        b_spec = pl.BlockSpec((tk, tn), lambda i, j, k: (k, j))
        dims = (((0,), (0,)), ((), ()))
    nk = K // tk
    n_ex, n_out = len(extras), len(outs)
    n_tok = 0 if tok is None else 1
    nbytes = lambda d: jnp.dtype(d).itemsize
    vmem_est = (2 * (tm * tk * nbytes(a.dtype) + tk * tn * nbytes(b.dtype)
                     + sum(tm * tn * nbytes(e.dtype) for e, kind in extras if kind == "tile")
                     + sum(tm * tn * nbytes(d) for d in outs)) + tm * tn * 4)
    assert vmem_est <= VMEM_LIMIT, (name, vmem_est)
    if epilogue is None:
        epilogue = lambda acc, ex: tuple(acc.astype(d) for d in outs)

    n_into = 0 if into is None else 1

    def body(a_ref, b_ref, *refs):
        refs = refs[n_tok:]
        ex_refs, out_refs = refs[:n_ex], refs[n_ex + n_into:n_ex + n_into + n_out]

        def finish(acc):
            res = epilogue(acc, [r[...] for r in ex_refs])
            for o_ref, v in zip(out_refs, res):
                if nb is None:
                    o_ref[...] = v.astype(o_ref.dtype)
                else:
                    for q in range(tn // nb):
                        o_ref[q] = v[:, q * nb:(q + 1) * nb].astype(o_ref.dtype)

        part = lax.dot_general(a_ref[...], b_ref[...], dims, preferred_element_type=F32)
        if nk == 1:
            finish(part)
        else:
            acc_ref = refs[n_ex + n_into + n_out]
            k = pl.program_id(2)

            @pl.when(k == 0)
            def _():
                acc_ref[...] = part

            @pl.when(k > 0)
            def _():
                acc_ref[...] += part

            @pl.when(k == nk - 1)
            def _():
                finish(acc_ref[...])

    col = (lambda j: j) if scatter is None else (lambda j: scatter[0] * j + scatter[1])
    ex_specs = [pl.BlockSpec((tm, tn), lambda i, j, k: (i, j)) if kind == "tile"
                else pl.BlockSpec((1, tn), lambda i, j, k: (0, col(j))) for _, kind in extras]
    if nb is not None:
        assert tn % nb == 0, (name, tn, nb)
        o_spec = pl.BlockSpec((tn // nb, tm, nb), lambda i, j, k: (j, i, 0))
        o_shape = (N // nb, M, nb)
    else:
        o_spec = pl.BlockSpec((tm, tn), lambda i, j, k: (i, col(j)))
        o_shape = (M, N if scatter is None else scatter[2])
    assert n_into == 0 or n_out == 1
    res = pl.pallas_call(
        body, name=name, grid=(M // tm, N // tn, nk),
        in_specs=[a_spec, b_spec] + [pl.BlockSpec((SUBLANES, LANES), lambda i, j, k: (0, 0))] * n_tok + ex_specs
                 + [pl.BlockSpec(memory_space=pl.ANY)] * n_into,
        out_specs=tuple([o_spec] * n_out),
        out_shape=tuple(jax.ShapeDtypeStruct(o_shape, d) for d in outs),
        input_output_aliases={2 + n_tok + n_ex: 0} if n_into else {},
        scratch_shapes=[pltpu.VMEM((tm, tn), F32)] if nk > 1 else [],
        compiler_params=_cparams(3, big=True),
    )(a, b, *([tok] if n_tok else []), *[e for e, _ in extras], *([into] if n_into else []))
    return res[0] if n_out == 1 else res


def _mm_rows(a, b, *, mode, tm, seq, ins, outs, epilogue, tok=None, name):
    M, K = a.shape
    b_parts = list(b) if isinstance(b, (list, tuple)) else [b]
    n_part = len(b_parts)
    assert n_part == 1 or mode == "nt"
    N = b_parts[0].shape[1] if mode == "nn" else b_parts[0].shape[0]
    tm = min(tm, M)
    assert M % tm == 0 and seq % tm == 0, (name, M, seq, tm)
    tpb = seq // tm
    n_b = M // seq
    dims = (((1,), (0,)), ((), ())) if mode == "nn" else (((1,), (1,)), ((), ()))
    n_tok = 0 if tok is None else 1
    n_in, n_out = len(ins), len(outs)

    in_specs, in_arrs = [], []
    for spec in ins:
        kind, arr = spec[0], spec[1]
        in_arrs.append(arr)
        if kind == "tile":
            in_specs.append(pl.BlockSpec((tm, arr.shape[1]), lambda i: (i, 0)))
        elif kind == "tilecol":
            in_specs.append(pl.BlockSpec((tm, spec[2]), lambda i, cb=spec[3]: (i, cb)))
        elif kind == "row":
            in_specs.append(pl.BlockSpec(arr.shape, lambda i: (0, 0)))
        else:
            in_specs.append(pl.BlockSpec((None, 1, arr.shape[2]), lambda i: (i // tpb, 0, 0)))
    out_specs, out_shapes = [], []
    for spec in outs:
        kind = spec[0]
        if kind == "tile":
            out_specs.append(pl.BlockSpec((tm, spec[2]), lambda i: (i, 0)))
            out_shapes.append(jax.ShapeDtypeStruct((M, spec[2]), spec[1]))
        elif kind == "tilecol":
            out_specs.append(pl.BlockSpec((tm, spec[2]), lambda i, cb=spec[3]: (i, cb)))
            out_shapes.append(jax.ShapeDtypeStruct((M, spec[4]), spec[1]))
        elif kind == "acc_row":
            out_specs.append(pl.BlockSpec((1, spec[1]), lambda i: (0, 0)))
            out_shapes.append(jax.ShapeDtypeStruct((1, spec[1]), F32))
        elif kind == "acc_brow":
            out_specs.append(pl.BlockSpec((None, 1, spec[1]), lambda i: (i // tpb, 0, 0)))
            out_shapes.append(jax.ShapeDtypeStruct((n_b, 1, spec[1]), F32))
        else:
            out_specs.append(pl.BlockSpec((SUBLANES, LANES), lambda i: (0, 0)))
            out_shapes.append(jax.ShapeDtypeStruct((SUBLANES, LANES), F32))

    def body(a_ref, *refs):
        b_refs, refs = refs[:n_part], refs[n_part + n_tok:]
        in_refs, out_refs = refs[:n_in], refs[n_in:n_in + n_out]
        i = pl.program_id(0)
        if n_part == 1:
            prod = lax.dot_general(a_ref[...], b_refs[0][...], dims, preferred_element_type=F32)
        else:
            w = b_parts[0].shape[1] // N_DEV
            prod = None
            for q in range(n_part):
                a_q = jnp.concatenate([a_ref[:, (n_part * j + q) * w:(n_part * j + q + 1) * w] for j in range(N_DEV)],
                                      axis=1)
                pq = lax.dot_general(a_q, b_refs[q][...], dims, preferred_element_type=F32)
                prod = pq if prod is None else prod + pq
        vals = epilogue(prod, [r[...] for r in in_refs])
        for spec, o_ref, v in zip(outs, out_refs, vals):
            kind = spec[0]
            if kind in ("tile", "tilecol"):
                off = 0
                for part in (v if isinstance(v, tuple) else (v,)):
                    o_ref[:, off:off + part.shape[1]] = part.astype(o_ref.dtype)
                    off += part.shape[1]
            else:
                first = (i % tpb == 0) if kind == "acc_brow" else (i == 0)

                @pl.when(first)
                def _(o_ref=o_ref, v=v):
                    o_ref[...] = jnp.broadcast_to(v, o_ref.shape)

                @pl.when(jnp.logical_not(first))
                def _(o_ref=o_ref, v=v):
                    o_ref[...] += v

    res = pl.pallas_call(
        body, name=name, grid=(M // tm,),
        in_specs=[pl.BlockSpec((tm, K), lambda i: (i, 0))]
                 + [pl.BlockSpec(bp.shape, lambda i: (0, 0), pipeline_mode=pl.Buffered(1)) for bp in b_parts]
                 + [pl.BlockSpec((SUBLANES, LANES), lambda i: (0, 0))] * n_tok + in_specs,
        out_specs=tuple(out_specs), out_shape=tuple(out_shapes),
        compiler_params=_cparams(1, big=True),
    )(a, *b_parts, *([tok] if n_tok else []), *in_arrs)
    return res


def _tok_spec(ts, width, col_block=0):
    return pl.BlockSpec((None, ts, width), lambda b, s: (b, s, col_block))


def _brow_spec(width):
    return pl.BlockSpec((None, 1, width), lambda b, s: (b, 0, 0))


def _modulate(x, sc, sh, ts):
    Bl, S, D = x.shape

    def body(x_ref, sc_ref, sh_ref, o_ref):
        o_ref[...] = (x_ref[...] * (1.0 + sc_ref[...]) + sh_ref[...]).astype(BF16)

    return pl.pallas_call(
        body, name="modulate", grid=(Bl, S // ts),
        in_specs=[_tok_spec(ts, D), _brow_spec(D), _brow_spec(D)],
        out_specs=_tok_spec(ts, D), out_shape=jax.ShapeDtypeStruct((Bl, S, D), BF16),
        compiler_params=_cparams(2),
    )(x, sc, sh)


def _mix_fwd(proj, w_conv, b_conv, w_rg_a, b_rg_a, w_rg_x, b_rg_x, lam, w_sp, b_sp_t, ln_v_g, ln_v_b, *, tm, lw, sw):
    Bl, S, _ = proj.shape
    heads, hd = w_rg_a.shape[0], w_rg_a.shape[1]
    groups = w_sp.shape[0]
    cw = 2 * lw + 2 * sw
    nblk = tm // SGU_BLOCK

    G = tm // SUBLANES
    nc = lw // LANES

    def body(p_ref, wc_ref, bc_ref, wa_ref, ba_ref, wx_ref, bx_ref, lam_ref, wsp_ref, bsp_ref, lg_ref, lb_ref,
             hs_ref, ya_ref, ys_ref, xc_ref, r_ref, ig_ref, a_ref, m_ref,
             xext, hnat, hcar, h7_scr, a7_scr, hp_scr, h0_scr, cp_scr):
        s = pl.program_id(1)

        @pl.when(s == 0)
        def _():
            xext[:, 0:SUBLANES, :] = jnp.zeros((nc, SUBLANES, LANES), F32)
            hcar[...] = jnp.zeros_like(hcar)

        @pl.when(s > 0)
        def _():
            xext[:, 0:SUBLANES, :] = xext[:, tm:tm + SUBLANES, :]

        nl = -lam_ref[...]
        big_l = -LRU_C * (jnp.maximum(nl, 0.0) + _log1p_pos(jnp.exp(-jnp.abs(nl))))

        for c in range(nc):
            cs = slice(c * LANES, (c + 1) * LANES)
            xext[c, SUBLANES:SUBLANES + tm, :] = p_ref[:, cs].astype(F32)
            xs = {st: xext[c, pl.ds(st, G, stride=SUBLANES), :] for st in range(SUBLANES - 3, 2 * SUBLANES)}
            wcs = [wc_ref[k:k + 1, cs] for k in range(4)]
            xc_j = []
            for j in range(SUBLANES):
                acc = bc_ref[:, cs] + xs[SUBLANES + j] * wcs[3]
                for k in (1, 2, 3):
                    acc = acc + xs[SUBLANES + j - k] * wcs[3 - k]
                xc_j.append(acc)
                xc_ref[j * G:(j + 1) * G, cs] = acc
            xcb = jnp.concatenate(xc_j, axis=0).astype(BF16)
            pa = jnp.dot(xcb, wa_ref[c], preferred_element_type=F32)
            px = jnp.dot(xcb, wx_ref[c], preferred_element_type=F32)
            h0 = cp = None
            for j in range(SUBLANES):
                rs = slice(j * G, (j + 1) * G)
                r = _sigmoid(pa[rs] + ba_ref[:, cs])
                ig = _sigmoid(px[rs] + bx_ref[:, cs])
                la = big_l[:, cs] * r
                a = jnp.exp(la)
                th = jnp.tanh(la)
                msq = (-2.0 * th) * pl.reciprocal(1.0 - th, approx=True)
                m = msq * lax.rsqrt(jnp.maximum(msq, 1e-30))
                b = m * (ig * xc_j[j])
                r_ref[rs, cs] = r
                ig_ref[rs, cs] = ig
                a_ref[rs, cs] = a
                m_ref[rs, cs] = m
                h0 = b if j == 0 else a * h0 + b
                cp = a if j == 0 else a * cp
                h0_scr[rs, cs] = h0
                cp_scr[rs, cs] = cp
            h7_scr[:, cs] = h0
            a7_scr[:, cs] = cp
        carry = hcar[0:1, :]
        for g in range(G):
            hp_scr[g:g + 1, :] = carry
            carry = h7_scr[g:g + 1, :] + a7_scr[g:g + 1, :] * carry
        hcar[0:1, :] = carry
        for c in range(nc):
            cs = slice(c * LANES, (c + 1) * LANES)
            hprev = hp_scr[:, cs]
            for j in range(SUBLANES):
                rs = slice(j * G, (j + 1) * G)
                hnat[c, pl.ds(j, G, stride=SUBLANES), :] = h0_scr[rs, cs] + cp_scr[rs, cs] * hprev
            hs = hnat[c]
            hs_ref[:, cs] = hs
            ya_ref[:, cs] = (hs * _gelu(p_ref[:, lw + c * LANES:lw + (c + 1) * LANES].astype(F32))).astype(BF16)

        gu = _gelu(p_ref[:, 2 * lw:2 * lw + sw].astype(F32))
        gv = _gelu(p_ref[:, 2 * lw + sw:cw].astype(F32))
        xhat, _ = _ln_stats(gv)
        vn = (xhat * lg_ref[...] + lb_ref[...]).astype(BF16)
        tpos = lax.broadcasted_iota(jnp.int32, (SGU_BLOCK, SGU_BLOCK), 0) // CHUNK
        spos = lax.broadcasted_iota(jnp.int32, (SGU_BLOCK, SGU_BLOCK), 1) // CHUNK
        gw = sw // groups
        rows_out = []
        for blk in range(nblk):
            r0 = blk * SGU_BLOCK
            cols = []
            for g in range(groups):
                wm = jnp.where(spos <= tpos, wsp_ref[g], 0.0).astype(BF16)
                mixed = jnp.dot(wm, vn[r0:r0 + SGU_BLOCK, g * gw:(g + 1) * gw], preferred_element_type=F32)
                cols.append(mixed + bsp_ref[:, g:g + 1])
            rows_out.append(jnp.concatenate(cols, axis=1))
        mixed_all = jnp.concatenate(rows_out, axis=0) if nblk > 1 else rows_out[0]
        ys_ref[...] = (gu * mixed_all).astype(BF16)

    full = lambda shp: pl.BlockSpec(shp, lambda b, s: (0,) * len(shp))
    return pl.pallas_call(
        body, name="mix_fwd", grid=(Bl, S // tm),
        in_specs=[_tok_spec(tm, cw), full(w_conv.shape), full(b_conv.shape), full(w_rg_a.shape), full(b_rg_a.shape),
                  full(w_rg_x.shape), full(b_rg_x.shape), full(lam.shape), full(w_sp.shape), full(b_sp_t.shape),
                  full(ln_v_g.shape), full(ln_v_b.shape)],
        out_specs=(_tok_spec(tm, lw), _tok_spec(tm, lw), _tok_spec(tm, sw)) + (_tok_spec(tm, lw),) * 5,
        out_shape=(jax.ShapeDtypeStruct((Bl, S, lw), F32), jax.ShapeDtypeStruct((Bl, S, lw), BF16),
                   jax.ShapeDtypeStruct((Bl, S, sw), BF16)) + (jax.ShapeDtypeStruct((Bl, S, lw), F32),) * 5,
        scratch_shapes=[pltpu.VMEM((nc, tm + SUBLANES, LANES), F32), pltpu.VMEM((nc, tm, LANES), F32),
                        pltpu.VMEM((SUBLANES, lw), F32), pltpu.VMEM((G, lw), F32), pltpu.VMEM((G, lw), F32),
                        pltpu.VMEM((G, lw), F32), pltpu.VMEM((tm, lw), F32), pltpu.VMEM((tm, lw), F32)],
        compiler_params=_cparams(2, big=True),
    )(proj, w_conv, b_conv, w_rg_a, b_rg_a, w_rg_x, b_rg_x, lam, w_sp, b_sp_t, ln_v_g, ln_v_b)


def _mix_bwd(proj, hs, dya, dys, dproj, saved, w_conv, b_conv, w_rg_a, b_rg_a, w_rg_x, b_rg_x, lam, w_sp, b_sp_t,
             ln_v_g, ln_v_b, *, tm, lw, sw):
    Bl, S, din = proj.shape
    heads, hd = w_rg_a.shape[0], w_rg_a.shape[1]
    groups = w_sp.shape[0]
    gw = sw // groups
    cw = 2 * lw + 2 * sw
    nblk = tm // SGU_BLOCK
    n_s = S // tm
    per8 = tm // SUBLANES
    halo_rows = 2 * SUBLANES

    G = tm // SUBLANES
    nc = lw // LANES

    def body(p_ref, xh_ref, hs_ref, hh_ref, dya_ref, dys_ref, dpin_ref, xc_ref, r_ref, ig_ref, a_ref, m_ref,
             wc_ref, bc_ref, wa_ref, ba_ref, wx_ref, bx_ref, lam_ref, wsp_ref, bsp_ref, lg_ref, lb_ref,
             dp_ref, dbin_ref, dwc_ref, dbc_ref, dwa_ref, dba_ref, dwx_ref, dbx_ref, dlam_ref, dwsp_ref, dbsp_ref,
             dlg_ref, dlb_ref,
             xext, hext, dnat, dxext, dhcar, g00_scr, p0_scr, a0_scr, cin_scr, g0_scr, pp_scr):
        del dpin_ref
        sr = pl.program_id(1)
        first_tile = sr == n_s - 1

        @pl.when(_first_step())
        def _():
            for ref in (dbin_ref, dwc_ref, dbc_ref, dwa_ref, dba_ref, dwx_ref, dbx_ref, dlam_ref, dwsp_ref, dbsp_ref,
                        dlg_ref, dlb_ref):
                ref[...] = jnp.zeros_like(ref)

        @pl.when(sr == 0)
        def _():
            dhcar[...] = jnp.zeros_like(dhcar)
            dxext[:, tm:tm + SUBLANES, :] = jnp.zeros((nc, SUBLANES, LANES), F32)

        @pl.when(sr > 0)
        def _():
            dxext[:, tm:tm + SUBLANES, :] = dxext[:, 0:SUBLANES, :]

        keep = jnp.where(first_tile, 0.0, 1.0)
        xprev = xh_ref[...].astype(F32)[halo_rows - SUBLANES:halo_rows] * keep
        hprev8 = hh_ref[...] * keep
        nl = -lam_ref[...]
        big_l = -LRU_C * (jnp.maximum(nl, 0.0) + _log1p_pos(jnp.exp(-jnp.abs(nl))))
        dlam_scale = LRU_C * _sigmoid(nl)
        nt = (((1,), (1,)), ((), ()))
        tn = (((0,), (0,)), ((), ()))
        last = SUBLANES - 1

        for c in range(nc):
            cs = slice(c * LANES, (c + 1) * LANES)
            gcs = slice(lw + c * LANES, lw + (c + 1) * LANES)
            xext[c, 0:SUBLANES, :] = xprev[:, cs]
            xext[c, SUBLANES:SUBLANES + tm, :] = p_ref[:, cs].astype(F32)
            hext[c, 0:SUBLANES, :] = hprev8[:, cs]
            dgl_sum = None
            for i in range(SUBLANES):
                rs = slice(i * G, (i + 1) * G)
                ggl, dggl = _gelu_and_grad(p_ref[rs, gcs].astype(F32))
                dy = dya_ref[rs, cs].astype(F32)
                hsv = hs_ref[rs, cs]
                hext[c, SUBLANES + i * G:SUBLANES + (i + 1) * G, :] = hsv
                dgl = dy * hsv * dggl
                dp_ref[rs, gcs] = dgl.astype(BF16)
                dnat[c, rs, :] = dy * ggl
                dgl_sum = _fold8(dgl) if i == 0 else dgl_sum + _fold8(dgl)
            dbin_ref[:, gcs] += _colsum(dgl_sum)
            g0 = pp = None
            for j in range(last, -1, -1):
                rs = slice(j * G, (j + 1) * G)
                dhs_j = dnat[c, pl.ds(j, G, stride=SUBLANES), :]
                if j == last:
                    g0 = dhs_j
                else:
                    an = a_ref[(j + 1) * G:(j + 2) * G, cs]
                    g0 = dhs_j + an * g0
                    pp = an if j == last - 1 else an * pp
                    pp_scr[rs, cs] = pp
                g0_scr[rs, cs] = g0
            g00_scr[:, cs] = g0
            p0_scr[:, cs] = pp
            a0_scr[:, cs] = a_ref[0:G, cs]
        cin = dhcar[0:1, :]
        for g in range(G - 1, -1, -1):
            cin_scr[g:g + 1, :] = cin
            cin = a0_scr[g:g + 1, :] * (g00_scr[g:g + 1, :] + p0_scr[g:g + 1, :] * cin)
        dhcar[0:1, :] = cin
        for c in range(nc):
            cs = slice(c * LANES, (c + 1) * LANES)
            cinv = cin_scr[:, cs]
            dpa_j, dpx_j, dxc_j = [], [], []
            dlam_sum = dba_sum = dbx_sum = None
            for j in range(SUBLANES):
                rs = slice(j * G, (j + 1) * G)
                dh = g0_scr[rs, cs] + (cinv if j == last else pp_scr[rs, cs] * cinv)
                hprev = hext[c, pl.ds(last + j, G, stride=SUBLANES), :]
                xc, r, ig, a, m = xc_ref[rs, cs], r_ref[rs, cs], ig_ref[rs, cs], a_ref[rs, cs], m_ref[rs, cs]
                dixc = dh * m
                dla = (dh * hprev) * a - (dh * (ig * xc)) * ((a * a) * pl.reciprocal(m, approx=True))
                dpa = (dla * big_l[:, cs]) * r * (1.0 - r)
                dpx = (dixc * xc) * ig * (1.0 - ig)
                dpa_j.append(dpa)
                dpx_j.append(dpx)
                dxc_j.append(dixc * ig)
                sums = (_fold8(dla * r), _fold8(dpa), _fold8(dpx))
                dlam_sum, dba_sum, dbx_sum = sums if j == 0 else (dlam_sum + sums[0], dba_sum + sums[1], dbx_sum + sums[2])
            dlam_ref[:, cs] += _colsum(dlam_sum) * dlam_scale[:, cs]
            dba_ref[:, cs] += _colsum(dba_sum)
            dbx_ref[:, cs] += _colsum(dbx_sum)
            dpab = jnp.concatenate(dpa_j, axis=0).astype(BF16)
            dpxb = jnp.concatenate(dpx_j, axis=0).astype(BF16)
            xcb = xc_ref[:, cs].astype(BF16)
            dxc = (jnp.concatenate(dxc_j, axis=0)
                   + lax.dot_general(dpab, wa_ref[c], nt, preferred_element_type=F32)
                   + lax.dot_general(dpxb, wx_ref[c], nt, preferred_element_type=F32))
            dwa_ref[c] += lax.dot_general(xcb, dpab, tn, preferred_element_type=F32)
            dwx_ref[c] += lax.dot_general(xcb, dpxb, tn, preferred_element_type=F32)

            dbc_ref[:, cs] += _colsum(dxc)
            xs = {st: xext[c, pl.ds(st, G, stride=SUBLANES), :] for st in range(SUBLANES - 3, 2 * SUBLANES)}
            for k in range(4):
                tot = None
                for j in range(SUBLANES):
                    part = _fold8(dxc[j * G:(j + 1) * G] * xs[SUBLANES + j - (3 - k)])
                    tot = part if tot is None else tot + part
                dwc_ref[k:k + 1, cs] += _colsum(tot)
            for j in range(SUBLANES):
                dxext[c, pl.ds(j, G, stride=SUBLANES), :] = dxc[j * G:(j + 1) * G]
            us = {st: dxext[c, pl.ds(st, G, stride=SUBLANES), :] for st in range(SUBLANES + 3)}
            wcs = [wc_ref[k:k + 1, cs] for k in range(4)]
            for j in range(SUBLANES):
                acc = us[j] * wcs[3]
                for k in (1, 2, 3):
                    acc = acc + us[j + k] * wcs[3 - k]
                dnat[c, pl.ds(j, G, stride=SUBLANES), :] = acc
            dxl = dnat[c]
            dp_ref[:, cs] = dxl.astype(BF16)
            dbin_ref[:, cs] += _colsum(dxl)

        gu, dgu_dx = _gelu_and_grad(p_ref[:, 2 * lw:2 * lw + sw].astype(F32))
        gv, dgv_dx = _gelu_and_grad(p_ref[:, 2 * lw + sw:cw].astype(F32))
        xhat, rstd = _ln_stats(gv)
        vn = (xhat * lg_ref[...] + lb_ref[...]).astype(BF16)
        dys = dys_ref[...].astype(F32)
        dmixed = dys * gu
        dmb = dmixed.astype(BF16)
        tpos = lax.broadcasted_iota(jnp.int32, (SGU_BLOCK, SGU_BLOCK), 0) // CHUNK
        spos = lax.broadcasted_iota(jnp.int32, (SGU_BLOCK, SGU_BLOCK), 1) // CHUNK
        causal = spos <= tpos
        mixed_rows, dvn_rows = [], []
        for blk in range(nblk):
            rs = slice(blk * SGU_BLOCK, (blk + 1) * SGU_BLOCK)
            mcols, dcols = [], []
            for g in range(groups):
                cs = slice(g * gw, (g + 1) * gw)
                wm = jnp.where(causal, wsp_ref[g], 0.0).astype(BF16)
                mcols.append(jnp.dot(wm, vn[rs, cs], preferred_element_type=F32) + bsp_ref[:, g:g + 1])
                dcols.append(lax.dot_general(wm, dmb[rs, cs], tn, preferred_element_type=F32))
                dw = lax.dot_general(dmb[rs, cs], vn[rs, cs], nt, preferred_element_type=F32)
                dwsp_ref[g] += jnp.where(causal, dw, 0.0)
                dbsp_ref[:, g:g + 1] += jnp.sum(dmixed[rs, cs], axis=1, keepdims=True)
            mixed_rows.append(jnp.concatenate(mcols, axis=1))
            dvn_rows.append(jnp.concatenate(dcols, axis=1))
        mixed_all = jnp.concatenate(mixed_rows, axis=0) if nblk > 1 else mixed_rows[0]
        dvn = jnp.concatenate(dvn_rows, axis=0) if nblk > 1 else dvn_rows[0]
        du = dys * mixed_all * dgu_dx
        dlg_ref[...] += _colsum(dvn * xhat)
        dlb_ref[...] += _colsum(dvn)
        dv = _ln_bwd(dvn, xhat, rstd, lg_ref[...]) * dgv_dx
        dp_ref[:, 2 * lw:2 * lw + sw] = du.astype(BF16)
        dp_ref[:, 2 * lw + sw:cw] = dv.astype(BF16)
        dbin_ref[:, 2 * lw:2 * lw + sw] += _colsum(du)
        dbin_ref[:, 2 * lw + sw:cw] += _colsum(dv)

    rev = lambda s: n_s - 1 - s
    tile = lambda w: pl.BlockSpec((None, tm, w), lambda b, s: (b, rev(s), 0))
    halo = lambda w: pl.BlockSpec((None, SUBLANES, w), lambda b, s: (b, jnp.maximum(rev(s) * per8 - 1, 0), 0))
    xhalo = pl.BlockSpec((None, halo_rows, lw), lambda b, s: (b, jnp.maximum(rev(s) * (tm // halo_rows) - 1, 0), 0))
    full = lambda shp: pl.BlockSpec(shp, lambda b, s: (0,) * len(shp))
    small = [w_conv, b_conv, w_rg_a, b_rg_a, w_rg_x, b_rg_x, lam, w_sp, b_sp_t, ln_v_g, ln_v_b]
    acc_shapes = [(1, cw), w_conv.shape, b_conv.shape, w_rg_a.shape, b_rg_a.shape, w_rg_x.shape, b_rg_x.shape,
                  lam.shape, w_sp.shape, b_sp_t.shape, ln_v_g.shape, ln_v_b.shape]
    res = pl.pallas_call(
        body, name="mix_bwd", grid=(Bl, n_s),
        in_specs=[tile(cw), xhalo, tile(lw), halo(lw), tile(lw), tile(sw), pl.BlockSpec(memory_space=pl.ANY)]
                 + [tile(lw)] * 5 + [full(w.shape) for w in small],
        out_specs=tuple([tile(cw)] + [full(shp) for shp in acc_shapes]),
        out_shape=tuple([jax.ShapeDtypeStruct((Bl, S, din), BF16)] + [jax.ShapeDtypeStruct(shp, F32) for shp in acc_shapes]),
        input_output_aliases={6: 0},
        scratch_shapes=[pltpu.VMEM((nc, tm + SUBLANES, LANES), F32), pltpu.VMEM((nc, tm + SUBLANES, LANES), F32),
                        pltpu.VMEM((nc, tm, LANES), F32), pltpu.VMEM((nc, tm + SUBLANES, LANES), F32),
                        pltpu.VMEM((SUBLANES, lw), F32), pltpu.VMEM((G, lw), F32), pltpu.VMEM((G, lw), F32),
                        pltpu.VMEM((G, lw), F32), pltpu.VMEM((G, lw), F32), pltpu.VMEM((tm, lw), F32),
                        pltpu.VMEM((tm, lw), F32)],
        compiler_params=_cparams(2, big=True),
    )(proj, proj, hs, hs, dya, dys, dproj, *saved, *small)
    return res


def _ada_fwd(c_all, w_ada):
    R, D = c_all.shape
    nb = w_ada.shape[1]

    def body(c_ref, w_ref, act_ref, o_ref):
        cv = c_ref[...]
        act = (cv * _sigmoid(cv)).astype(BF16)
        act_ref[...] = act
        o_ref[...] = jnp.dot(act, w_ref[...].astype(BF16), preferred_element_type=F32)

    return pl.pallas_call(
        body, name="ada_fwd",
        out_shape=(jax.ShapeDtypeStruct((R, D), BF16), jax.ShapeDtypeStruct((R, nb), F32)),
        compiler_params=pltpu.CompilerParams(vmem_limit_bytes=VMEM_LIMIT),
    )(c_all, w_ada)


def _ada_bwd(c_act, dmod_cols):
    R, D = c_act.shape
    nb = dmod_cols.shape[1]

    def body(act_ref, d_ref, o_ref, b_ref):
        o_ref[...] = lax.dot_general(act_ref[...], d_ref[...].astype(BF16), (((0,), (0,)), ((), ())),
                                     preferred_element_type=F32)
        b_ref[...] = _colsum(d_ref[...])

    return pl.pallas_call(
        body, name="ada_bwd", out_shape=(jax.ShapeDtypeStruct((D, nb), F32), jax.ShapeDtypeStruct((1, nb), F32)),
        compiler_params=pltpu.CompilerParams(vmem_limit_bytes=VMEM_LIMIT),
    )(c_act, dmod_cols)


def _adamw(w, g_slots, m, v, *, tr, name, own=None):
    R, C = w.shape
    n_slot = g_slots.shape[0]
    tr = min(tr, R)
    assert R % tr == 0, (name, R, tr)
    c1 = 1.0 / (1.0 - ADAM_B1 ** ADAM_STEP)
    c2 = 1.0 / (1.0 - ADAM_B2 ** ADAM_STEP)
    n_own = 0 if own is None else 1

    def body(me_ref, w_ref, g_ref, *refs):
        m_ref, v_ref, go_ref, d_ref, mo_ref, vo_ref = refs[n_own:]
        slot = lambda d: (jnp.where(me_ref[0] == d, refs[0][...], g_ref[d]) if n_own else g_ref[d]).astype(F32)
        g = slot(0)
        for d in range(1, n_slot):
            g = g + slot(d)
        mn = ADAM_B1 * m_ref[...] + (1.0 - ADAM_B1) * g
        vn = ADAM_B2 * v_ref[...] + (1.0 - ADAM_B2) * (g * g)
        go_ref[...] = g
        mo_ref[...] = mn
        vo_ref[...] = vn
        d_ref[...] = -ADAM_LR * ((mn * c1) / (jnp.sqrt(vn * c2) + ADAM_EPS) + ADAM_WD * w_ref[...])

    me = 4 * lax.axis_index("x") + 2 * lax.axis_index("y") + lax.axis_index("c")
    blk = pl.BlockSpec((tr, C), lambda i, me_ref: (i, 0))
    own_specs = [pl.BlockSpec((None, tr, C), lambda i, me_ref: (me_ref[0], i, 0))] * n_own
    return pl.pallas_call(
        body, name=name, out_shape=tuple(jax.ShapeDtypeStruct((R, C), F32) for _ in range(4)),
        grid_spec=pltpu.PrefetchScalarGridSpec(
            num_scalar_prefetch=1, grid=(R // tr,),
            in_specs=[blk, pl.BlockSpec((n_slot, tr, C), lambda i, me_ref: (0, i, 0))] + own_specs + [blk, blk],
            out_specs=(blk, blk, blk, blk)),
        compiler_params=_cparams(1, big=True),
    )(jnp.reshape(me, (1,)).astype(jnp.int32), w, g_slots, *([own] if n_own else []), m, v)


def _adamw_many(ws, g_slots, g_owns, ms, vs, *, name):
    n = len(ws)
    c1 = 1.0 / (1.0 - ADAM_B1 ** ADAM_STEP)
    c2 = 1.0 / (1.0 - ADAM_B2 ** ADAM_STEP)

    def body(*refs):
        w_refs, g_refs, o_refs = refs[:n], refs[n:2 * n], refs[2 * n:3 * n]
        m_refs, v_refs = refs[3 * n:4 * n], refs[4 * n:5 * n]
        outs = refs[5 * n:]
        me = 4 * lax.axis_index("x") + 2 * lax.axis_index("y") + lax.axis_index("c")
        for i in range(n):
            own = o_refs[i][...]
            g = jnp.where(me == 0, own, g_refs[i][0])
            for d in range(1, N_DEV):
                g = g + jnp.where(me == d, own, g_refs[i][d])
            mn = ADAM_B1 * m_refs[i][...] + (1.0 - ADAM_B1) * g
            vn = ADAM_B2 * v_refs[i][...] + (1.0 - ADAM_B2) * (g * g)
            outs[i][...] = g
            outs[n + i][...] = -ADAM_LR * ((mn * c1) / (jnp.sqrt(vn * c2) + ADAM_EPS) + ADAM_WD * w_refs[i][...])
            outs[2 * n + i][...] = mn
            outs[3 * n + i][...] = vn

    res = pl.pallas_call(
        body, name=name, out_shape=tuple(jax.ShapeDtypeStruct(w.shape, F32) for _ in range(4) for w in ws),
        compiler_params=pltpu.CompilerParams(vmem_limit_bytes=VMEM_LIMIT),
    )(*ws, *g_slots, *g_owns, *ms, *vs)
    return res[:n], res[n:2 * n], res[2 * n:3 * n], res[3 * n:]


SMALL_NAMES = ("b_ada", "b_in", "b_conv", "w_rg_a", "b_rg_a", "w_rg_x", "b_rg_x", "lru_lambda", "w_sp", "b_sp",
               "ln_v_g", "ln_v_b", "ln1_g", "ln1_b", "ln2_g", "ln2_b")
WEIGHT_ORDER = ("w_ada", "b_ada", "w_in", "b_in", "w_conv", "b_conv", "w_rg_a", "b_rg_a", "w_rg_x", "b_rg_x",
                "lru_lambda", "w_sp", "b_sp", "ln_v_g", "ln_v_b", "w_o_lru", "w_o_sgu", "w_out", "ln1_g", "ln1_b",
                "w_up", "w_down", "ln2_g", "ln2_b")


def _blocked_cols(w2d):
    K, N = w2d.shape
    return jnp.transpose(w2d.reshape(K, N_DEV, N // N_DEV), (1, 0, 2))


def _unblock_cols(wb):
    n, K, nb = wb.shape
    return jnp.transpose(wb, (1, 0, 2)).reshape(K, n * nb)


def kernel(x, c, w_ada, b_ada, w_in, b_in, w_conv, b_conv, w_rg_a, b_rg_a, w_rg_x, b_rg_x, lru_lambda, w_sp, b_sp, ln_v_g, ln_v_b, w_o_lru, w_o_sgu, w_out, ln1_g, ln1_b, w_up, w_down, ln2_g, ln2_b, loss_target, m_w_ada, m_b_ada, m_w_in, m_b_in, m_w_conv, m_b_conv, m_w_rg_a, m_b_rg_a, m_w_rg_x, m_b_rg_x, m_lru_lambda, m_w_sp, m_b_sp, m_ln_v_g, m_ln_v_b, m_w_o_lru, m_w_o_sgu, m_w_out, m_ln1_g, m_ln1_b, m_w_up, m_w_down, m_ln2_g, m_ln2_b, v_w_ada, v_b_ada, v_w_in, v_b_in, v_w_conv, v_b_conv, v_w_rg_a, v_b_rg_a, v_w_rg_x, v_b_rg_x, v_lru_lambda, v_w_sp, v_b_sp, v_ln_v_g, v_ln_v_b, v_w_o_lru, v_w_o_sgu, v_w_out, v_ln1_g, v_ln1_b, v_w_up, v_w_down, v_ln2_g, v_ln2_b):
    W = dict(w_ada=w_ada, b_ada=b_ada, w_in=w_in, b_in=b_in, w_conv=w_conv, b_conv=b_conv, w_rg_a=w_rg_a,
             b_rg_a=b_rg_a, w_rg_x=w_rg_x, b_rg_x=b_rg_x, lru_lambda=lru_lambda, w_sp=w_sp, b_sp=b_sp,
             ln_v_g=ln_v_g, ln_v_b=ln_v_b, w_o_lru=w_o_lru, w_o_sgu=w_o_sgu, w_out=w_out, ln1_g=ln1_g, ln1_b=ln1_b,
             w_up=w_up, w_down=w_down, ln2_g=ln2_g, ln2_b=ln2_b)
    Mo = dict(w_ada=m_w_ada, b_ada=m_b_ada, w_in=m_w_in, b_in=m_b_in, w_conv=m_w_conv, b_conv=m_b_conv,
              w_rg_a=m_w_rg_a, b_rg_a=m_b_rg_a, w_rg_x=m_w_rg_x, b_rg_x=m_b_rg_x, lru_lambda=m_lru_lambda,
              w_sp=m_w_sp, b_sp=m_b_sp, ln_v_g=m_ln_v_g, ln_v_b=m_ln_v_b, w_o_lru=m_w_o_lru, w_o_sgu=m_w_o_sgu,
              w_out=m_w_out, ln1_g=m_ln1_g, ln1_b=m_ln1_b, w_up=m_w_up, w_down=m_w_down, ln2_g=m_ln2_g,
              ln2_b=m_ln2_b)
    Vo = dict(w_ada=v_w_ada, b_ada=v_b_ada, w_in=v_w_in, b_in=v_b_in, w_conv=v_w_conv, b_conv=v_b_conv,
              w_rg_a=v_w_rg_a, b_rg_a=v_b_rg_a, w_rg_x=v_w_rg_x, b_rg_x=v_b_rg_x, lru_lambda=v_lru_lambda,
              w_sp=v_w_sp, b_sp=v_b_sp, ln_v_g=v_ln_v_g, ln_v_b=v_ln_v_b, w_o_lru=v_w_o_lru, w_o_sgu=v_w_o_sgu,
              w_out=v_w_out, ln1_g=v_ln1_g, ln1_b=v_ln1_b, w_up=v_w_up, w_down=v_w_down, ln2_g=v_ln2_g,
              ln2_b=v_ln2_b)

    Bl, S, D = x.shape
    T = Bl * S
    lw = b_conv.shape[-1]
    sw = ln_v_g.shape[-1]
    din = b_in.shape[-1]
    dff = w_up.shape[-1] * N_DEV
    ts = min(2048, S)
    tmix = min(256, S)
    trow = min(512, S)

    c_pad = jnp.pad(c, ((0, SUBLANES - Bl), (0, 0)))
    c_g, wconv_g = _exchange([c_pad, w_conv[0]], True, "xchg_c")
    wconv_full = _unblock_cols(wconv_g)
    c_act, modcols = _ada_fwd(c_g.reshape(N_DEV * SUBLANES, D), w_ada[0])
    (mod_slots,) = _exchange([modcols.reshape(N_DEV, SUBLANES, -1)], False, "xchg_mod")

    nbw = din // N_DEV // WIN_PARTS
    wnames = tuple("win%d" % q for q in range(WIN_PARTS)) + ("wol", "wos", "wout", "wup", "wdown")
    shards = [w_in[0][:, q * nbw:(q + 1) * nbw].astype(BF16) for q in range(WIN_PARTS)] + [
        w_o_lru[0].astype(BF16), w_o_sgu[0].astype(BF16), w_out[0].astype(BF16), w_up[0].astype(BF16),
        w_down[0].astype(BF16)]
    col_sharded = [True] * WIN_PARTS + [False, True, False, True, False]
    g_send, g_recv, g_src, g_land, g_tok = _xstart(shards, True, mod_slots, "gather_start", cols=col_sharded)
    gidx = {n: i for i, n in enumerate(wnames)}

    def gathered(n, after):
        i = gidx[n]
        return _xwait(g_src[i], g_land[i], g_send[i], g_recv[i], after, True, "gather_wait_" + n, col=col_sharded[i])

    mod = _unblock_cols(mod_slots)[:Bl] + (b_ada + g_tok[0, 0])
    sh1, sc1, gt1, sh2, sc2, gt2 = [mod[:, i * D:(i + 1) * D].reshape(Bl, 1, D) for i in range(6)]

    wa_b, wx_b = w_rg_a[0].astype(BF16), w_rg_x[0].astype(BF16)
    b_sp_t = jnp.transpose(b_sp[0])
    small_mix = (wconv_full, b_conv, wa_b, b_rg_a, wx_b, b_rg_x, lru_lambda, w_sp[0], b_sp_t, ln_v_g, ln_v_b)

    h = _modulate(x, sc1, sh1, ts)
    proj, win_parts = None, []
    for q in range(WIN_PARTS):
        wq = gathered("win%d" % q, h if q == 0 else proj)
        win_parts.append(wq)
        proj = _mm(h.reshape(T, D), wq, mode="nn", tm=8192, tn=nbw, tk=D, outs=[BF16], extras=[(b_in, "row")],
                   epilogue=lambda acc, ex: (acc + ex[0],), scatter=(WIN_PARTS, q, din), into=proj,
                   name="mm_proj%d" % q)
    proj3 = proj.reshape(Bl, S, din)
    hs, ya_pre, ysgu, *lru_saved = _mix_fwd(proj3, *small_mix, tm=tmix, lw=lw, sw=sw)
    Wol = gathered("wol", ya_pre).reshape(lw, D)
    Wos = gathered("wos", ysgu)
    y_a = _mm(ya_pre.reshape(T, lw), Wol, mode="nn", tm=2048, tn=D, tk=lw, outs=[BF16], name="mm_ya")
    x2d, tgt2d = x.reshape(T, D), loss_target.reshape(T, D)
    gate_cb = (din - 2 * D) // D

    def ep_merge(y_b, v):
        ya, ga, gb = [t.astype(F32) for t in v]
        yb = y_b.astype(BF16).astype(F32)
        return [yb, _sigmoid(ga) * ya + _sigmoid(gb) * yb]

    y_b, merged = _mm_rows(ysgu.reshape(T, sw), Wos, mode="nn", tm=trow, seq=S,
                           ins=[("tile", y_a), ("tilecol", proj, D, gate_cb), ("tilecol", proj, D, gate_cb + 1)],
                           outs=[("tile", BF16, D), ("tile", BF16, D)], epilogue=ep_merge, name="mm_yb_merge")
    Wout = gathered("wout", merged).reshape(D, D)

    def ep_ln1(mix_acc, v):
        x_, gt, g, b, sc, sh = v
        mixr = mix_acc.astype(BF16).astype(F32)
        xhat, rstd = _ln_stats(ALPHA * x_ + (1.0 + gt) * mixr)
        x1_ = xhat * g + b
        return [mixr, x1_, x1_ * (1.0 + sc) + sh, xhat, jnp.broadcast_to(rstd, (rstd.shape[0], LANES))]

    mix, x1, h2, xhat1, rstd1 = _mm_rows(
        merged, Wout, mode="nn", tm=trow, seq=S,
        ins=[("tile", x2d), ("brow", gt1), ("row", ln1_g), ("row", ln1_b), ("brow", sc2), ("brow", sh2)],
        outs=[("tile", BF16, D), ("tile", F32, D), ("tile", BF16, D), ("tile", BF16, D), ("tile", F32, LANES)],
        epilogue=ep_ln1, name="mm_mix_ln1")
    Wup = gathered("wup", h2)
    def ep_up(up, ex):
        r = jnp.maximum(up, 0.0)
        return r * r, r + r

    act, dact_dup = _mm(h2, Wup, mode="nn", tm=2048, tn=1024, tk=D, outs=[BF16, BF16], epilogue=ep_up, name="mm_up")
    Wdown = gathered("wdown", act).reshape(dff, D)

    def ep_ln2(f_acc, v):
        x1_, t_, gt, g, b = v
        xhat, rstd = _ln_stats(ALPHA * x1_ + (1.0 + gt) * f_acc)
        err = xhat * g + b - t_
        loss_t = 0.5 * jnp.sum(jnp.mean(err * err, axis=-1, keepdims=True))
        dy = err * (1.0 / D)
        dz = _ln_bwd(dy, xhat, rstd, g)
        return [dz * (1.0 + gt), ALPHA * dz, _colsum(dz * f_acc), _colsum(dy * xhat), _colsum(dy), loss_t]

    df2, dx1p, dgt2, dg2, db2, loss_part = _mm_rows(
        act, Wdown, mode="nn", tm=trow, seq=S,
        ins=[("tile", x1), ("tile", tgt2d), ("brow", gt2), ("row", ln2_g), ("row", ln2_b)],
        outs=[("tile", BF16, D), ("tile", F32, D), ("acc_brow", D), ("acc_row", D), ("acc_row", D), ("acc_scalar",)],
        epilogue=ep_ln2, name="mm_down_ln2")
    loss = lax.psum(loss_part[0, 0], ("x", "y", "c"))

    def send_grads(parts, name):
        snd, rcv, src, land, tok = _xstart(parts, False, None, name + "_start")
        return [(src[i], land[i], snd[i], rcv[i]) for i in range(len(parts))], tok

    dup = _mm(df2, Wdown, mode="nt", tm=2048, tn=1024, tk=D, outs=[BF16], extras=[(dact_dup, "tile")],
              epilogue=lambda acc, ex: (acc * ex[0].astype(F32),), name="mm_dup")
    g_wdown = _mm(act, df2, mode="tn", tm=1024, tn=D, tk=2048, outs=[BF16], name="mm_gwdown")
    (x_wdown,), tok = send_grads([g_wdown.reshape(N_DEV, dff // N_DEV, D)], "gx_wdown")
    def ep_ln1_bwd(dh2, v):
        dx1p_, x1_, xh_, rs_, mix_, sc, gt, g = v
        mixv = mix_.astype(F32)
        dx1 = dx1p_ + dh2 * (1.0 + sc)
        xhat, rstd = xh_.astype(F32), rs_[:, 0:1]
        dz = _ln_bwd(dx1, xhat, rstd, g)
        return [ALPHA * dz, dz * (1.0 + gt), _colsum(dh2 * x1_), _colsum(dh2), _colsum(dz * mixv),
                _colsum(dx1 * xhat), _colsum(dx1)]

    dxp, dmix, dsc2, dsh2, dgt1, dg1, db1 = _mm_rows(
        dup, Wup, mode="nt", tm=trow, seq=S, tok=tok,
        ins=[("tile", dx1p), ("tile", x1), ("tile", xhat1), ("tile", rstd1), ("tile", mix), ("brow", sc2), ("brow", gt1),
             ("row", ln1_g)],
        outs=[("tile", F32, D), ("tile", BF16, D), ("acc_brow", D), ("acc_brow", D), ("acc_brow", D), ("acc_row", D),
              ("acc_row", D)],
        epilogue=ep_ln1_bwd, name="mm_dh2_ln1b")
    g_wup = _mm(h2, dup, mode="tn", tm=D, tn=1024, tk=2048, outs=[BF16], nb=dff // N_DEV, name="mm_gwup")
    (x_wup,), tok = send_grads([g_wup], "gx_wup")

    def ep_merge_bwd(dm, v):
        ya, yb, ga, gb = [t.astype(F32) for t in v]
        sa, sb = _sigmoid(ga), _sigmoid(gb)
        dga, dgb = dm * ya * sa * (1.0 - sa), dm * yb * sb * (1.0 - sb)
        return [dm * sa, dm * sb, (dga, dgb), jnp.concatenate([_colsum(dga), _colsum(dgb)], axis=1)]

    dy_a, dy_b, dproj, dbin_hi = _mm_rows(
        dmix, Wout, mode="nt", tm=trow, seq=S, tok=tok,
        ins=[("tile", y_a), ("tile", y_b), ("tilecol", proj, D, gate_cb), ("tilecol", proj, D, gate_cb + 1)],
        outs=[("tile", BF16, D), ("tile", BF16, D), ("tilecol", BF16, 2 * D, gate_cb // 2, din), ("acc_row", 2 * D)],
        epilogue=ep_merge_bwd, name="mm_dmerged_mb")
    g_wout = _mm(merged, dmix, mode="tn", tm=D, tn=D, tk=2048, outs=[BF16], name="mm_gwout")
    (x_wout,), tok = send_grads([g_wout.reshape(N_DEV, D // N_DEV, D)], "gx_wout")
    dya_pre = _mm(dy_a, Wol, mode="nt", tm=2048, tn=lw, tk=D, outs=[BF16], tok=tok, name="mm_dya")
    dysgu = _mm(dy_b, Wos, mode="nt", tm=2048, tn=sw, tk=D, outs=[BF16], name="mm_dys")
    g_wol = _mm(ya_pre.reshape(T, lw), dy_a, mode="tn", tm=lw, tn=D, tk=2048, outs=[BF16], name="mm_gwol")
    g_wos = _mm(ysgu.reshape(T, sw), dy_b, mode="tn", tm=sw, tn=D, tk=2048, outs=[BF16], nb=D // N_DEV,
                name="mm_gwos")
    (x_wol, x_wos), tok = send_grads([g_wol.reshape(N_DEV, lw // N_DEV, D), g_wos], "gx_wo")
    small_mix_b = (wconv_full, b_conv + tok[0, 0]) + small_mix[2:]
    (dproj, dbin_lo, g_wconv, g_bconv, g_wa, g_ba, g_wx, g_bx, g_lam, g_wsp, g_bsp_t, g_lvg, g_lvb) = _mix_bwd(
        proj3, hs, dya_pre.reshape(Bl, S, lw), dysgu.reshape(Bl, S, sw), dproj.reshape(Bl, S, din), lru_saved,
        *small_mix_b, tm=tmix, lw=lw, sw=sw)
    dproj2 = dproj.reshape(T, din)
    small_names = [n for n in SMALL_NAMES if n != "b_ada"]
    small_g = dict(b_in=jnp.concatenate([dbin_lo, dbin_hi], axis=-1), b_conv=g_bconv, w_rg_a=g_wa[None], b_rg_a=g_ba,
                   w_rg_x=g_wx[None], b_rg_x=g_bx, lru_lambda=g_lam, w_sp=g_wsp[None],
                   b_sp=jnp.transpose(g_bsp_t)[None], ln_v_g=g_lvg, ln_v_b=g_lvb, ln1_g=dg1, ln1_b=db1, ln2_g=dg2,
                   ln2_b=db2)
    gs_snd, gs_rcv, gs_src, gs_land, tok_s = _xstart([small_g[n] for n in small_names], True, None, "gsmall_start",
                                                      fill_own=False)
    g_win = _mm(h.reshape(T, D), dproj2, mode="tn", tm=D, tn=din // 4, tk=2048, outs=[BF16], nb=din // N_DEV,
                tok=tok_s, name="mm_gwin")
    (x_win,), tok = send_grads([g_win], "gx_win")

    def ep_final(dh, v):
        dxp_, x_, sc = v
        return [dxp_ + dh * (1.0 + sc), _colsum(dh * x_), _colsum(dh)]

    grad_x, dsc1, dsh1 = _mm_rows(dproj2, win_parts, mode="nt", tm=trow, seq=S, tok=tok,
                                  ins=[("tile", dxp), ("tile", x2d), ("brow", sc1)],
                                  outs=[("tile", F32, D), ("acc_brow", D), ("acc_brow", D)], epilogue=ep_final,
                                  name="mm_dh_final")
    grad_x = grad_x.reshape(Bl, S, D)

    out_g, out_d, out_m, out_v = {}, {}, {}, {}

    def adam(name, g_slots, tr, own=None):
        shp = W[name].shape
        w2, m2, v2 = [t.reshape(g_slots.shape[1:]) for t in (W[name], Mo[name], Vo[name])]
        g, d, mn, vn = _adamw(w2, g_slots, m2, v2, tr=tr, name="adam_" + name, own=own)
        out_g[name], out_d[name], out_m[name], out_v[name] = [t.reshape(shp) for t in (g, d, mn, vn)]

    def adam_exchanged(name, handle, tr, after):
        own, slots = _xwait(*handle, after, False, "gx_%s_wait" % name, place=False)
        adam(name, slots, tr, own=own)

    adam_exchanged("w_down", x_wdown, 256, dsh1)
    adam_exchanged("w_up", x_wup, 256, dsh1)
    adam_exchanged("w_out", x_wout, 128, dsh1)
    adam_exchanged("w_o_lru", x_wol, 160, dsh1)
    adam_exchanged("w_o_sgu", x_wos, 256, dsh1)
    gs_own, gs_slots = _xwait_many(gs_src, gs_land, gs_snd, gs_rcv, dsh1, "gsmall_wait")
    res_small = _adamw_many([W[n] for n in small_names], gs_slots, gs_own, [Mo[n] for n in small_names],
                            [Vo[n] for n in small_names], name="adam_small")
    for dst, vals in zip((out_g, out_d, out_m, out_v), res_small):
        dst.update(dict(zip(small_names, vals)))

    dmod = jnp.concatenate([dsh1, dsc1, dgt1, dsh2, dsc2, dgt2], axis=-1).reshape(Bl, 6 * D)
    dmod_b = _blocked_cols(jnp.pad(dmod, ((0, SUBLANES - Bl), (0, 0))))
    dmod_s, gwconv_s = _exchange([dmod_b, _blocked_cols(g_wconv)], False, "xchg_dmod", after=out_g["ln2_b"])
    g_wada, g_bada_mine = _ada_bwd(c_act, dmod_s.reshape(N_DEV * SUBLANES, -1))
    (g_bada_all,) = _exchange([g_bada_mine], True, "xchg_bada")
    adam("w_ada", g_wada[None], 256)
    adam("b_ada", g_bada_all.reshape(1, 1, 6 * D), 1)
    adam("w_conv", gwconv_s, 8)
    adam_exchanged("w_in", x_win, 256, g_bada_all)

    return (loss, grad_x, *[out_g[n] for n in WEIGHT_ORDER], *[out_d[n] for n in WEIGHT_ORDER],
            *[out_m[n] for n in WEIGHT_ORDER], *[out_v[n] for n in WEIGHT_ORDER])
```

```python
import math

import jax
import jax.numpy as jnp
from jax import lax
from jax.experimental import pallas as pl
from jax.experimental.pallas import tpu as pltpu

N_DEV = 8
LN_EPS = 1e-5
LRU_C = 8.0
CHUNK = 64
SGU_BLOCK = 128
ALPHA = 2.0 ** 0.25
ADAM_LR = 0.001
ADAM_B1 = 0.9
ADAM_B2 = 0.999
ADAM_EPS = 1e-08
ADAM_WD = 0.01
ADAM_STEP = 10
GELU_K0 = math.sqrt(2.0 / math.pi)
GELU_K1 = 0.044715

SUBLANES = 8
LANES = 128
VMEM_LIMIT = 56 * 1024 * 1024
WIN_PARTS = 3

F32 = jnp.float32
BF16 = jnp.bfloat16
MESH = pl.DeviceIdType.MESH


def _cparams(n_axes, big=False):
    return pltpu.CompilerParams(dimension_semantics=("arbitrary",) * n_axes,
                                vmem_limit_bytes=VMEM_LIMIT if big else None)


def _sigmoid(x):
    return 0.5 * jnp.tanh(0.5 * x) + 0.5


def _gelu(x):
    t = jnp.tanh(x * (GELU_K0 + (GELU_K0 * GELU_K1) * (x * x)))
    hx = 0.5 * x
    return hx + hx * t


def _gelu_and_grad(x):
    x2 = x * x
    t = jnp.tanh(x * (GELU_K0 + (GELU_K0 * GELU_K1) * x2))
    hx = 0.5 * x
    g = hx + hx * t
    dg = (0.5 + 0.5 * t) + (hx * (1.0 - t * t)) * (GELU_K0 + (3.0 * GELU_K0 * GELU_K1) * x2)
    return g, dg


def _log1p_pos(e):
    p = e * (1.0 - e * (1.0 / 2.0) + e * e * (1.0 / 3.0) - e * e * e * (1.0 / 4.0))
    return jnp.where(e < 1e-2, p, jnp.log(1.0 + e))


def _ln_stats(z):
    mu = jnp.mean(z, axis=-1, keepdims=True)
    zc = z - mu
    var = jnp.mean(zc * zc, axis=-1, keepdims=True)
    rstd = lax.rsqrt(var + LN_EPS)
    return zc * rstd, rstd


def _ln_bwd(dy, xhat, rstd, g):
    dxh = dy * g
    m1 = jnp.mean(dxh, axis=-1, keepdims=True)
    m2 = jnp.mean(dxh * xhat, axis=-1, keepdims=True)
    return rstd * (dxh - m1 - xhat * m2)


def _colsum(v):
    return jnp.sum(v, axis=0, keepdims=True)


def _fold8(v):
    out = v[0:SUBLANES]
    for i in range(1, v.shape[0] // SUBLANES):
        out = out + v[i * SUBLANES:(i + 1) * SUBLANES]
    return out


def _first_step():
    return jnp.logical_and(pl.program_id(0) == 0, pl.program_id(1) == 0)


def _exchange(arrs, gather, name, after=None):
    n = len(arrs)
    n_peer = N_DEV - 1
    n_after = 0 if after is None else 1

    def body(*refs):
        ins, outs = refs[:n], refs[n + n_after:2 * n + n_after]
        send_sems, recv_sems, loc_sems = refs[2 * n + n_after:]
        x, y, c = lax.axis_index("x"), lax.axis_index("y"), lax.axis_index("c")
        me = 4 * x + 2 * y + c
        started = []
        for a in range(n):
            src_me = ins[a] if gather else ins[a].at[me]
            lc = pltpu.make_async_copy(src_me, outs[a].at[me], loc_sems.at[a])
            lc.start()
            started.append((lc, None))
        for p in range(1, N_DEV):
            px, py, pc = x ^ ((p >> 2) & 1), y ^ ((p >> 1) & 1), c ^ (p & 1)
            peer = 4 * px + 2 * py + pc
            for a in range(n):
                k = a * n_peer + (p - 1)
                src = ins[a] if gather else ins[a].at[peer]
                cp = pltpu.make_async_remote_copy(src_ref=src, dst_ref=outs[a].at[me],
                                                  send_sem=send_sems.at[k], recv_sem=recv_sems.at[k],
                                                  device_id=(px, py, pc), device_id_type=MESH)
                cp.start()
                rc = pltpu.make_async_remote_copy(src_ref=src, dst_ref=outs[a].at[peer],
                                                  send_sem=send_sems.at[k], recv_sem=recv_sems.at[k],
                                                  device_id=(px, py, pc), device_id_type=MESH)
                started.append((cp, rc))
        for cp, rc in started:
            if rc is None:
                cp.wait()
            else:
                cp.wait_send()
                rc.wait_recv()

    hbm = pl.BlockSpec(memory_space=pltpu.HBM)
    out_shape = tuple(
        jax.ShapeDtypeStruct(((N_DEV,) + a.shape) if gather else a.shape, a.dtype) for a in arrs)
    return pl.pallas_call(
        body, name=name, out_shape=out_shape,
        in_specs=[hbm] * n + [pl.BlockSpec(memory_space=pl.ANY)] * n_after, out_specs=tuple([hbm] * n),
        scratch_shapes=[pltpu.SemaphoreType.DMA((n * n_peer,)), pltpu.SemaphoreType.DMA((n * n_peer,)),
                        pltpu.SemaphoreType.DMA((n,))],
        compiler_params=pltpu.CompilerParams(has_side_effects=True),
    )(*arrs, *([after] if n_after else []))


_HBM = pl.BlockSpec(memory_space=pltpu.HBM)
_SEM = pl.BlockSpec(memory_space=pltpu.SEMAPHORE)
_EFFECT = pltpu.SideEffectType.DATAFLOW_SIDE_EFFECTING


def _peer_of(p):
    x, y, c = lax.axis_index("x"), lax.axis_index("y"), lax.axis_index("c")
    px, py, pc = x ^ ((p >> 2) & 1), y ^ ((p >> 1) & 1), c ^ (p & 1)
    return (px, py, pc), 4 * px + 2 * py + pc


def _slot(land_ref, idx, width):
    if width is None:
        return land_ref.at[idx]
    return land_ref.at[:, pl.ds(pl.multiple_of(idx * width, LANES), width)]


def _xstart(srcs, gather, after, name, cols=None):
    n = len(srcs)
    cols = cols or [False] * n
    widths = [t.shape[1] if cols[a] else None for a, t in enumerate(srcs)]
    lands = [lax.empty((t.shape[0], N_DEV * t.shape[1]) if cols[a] else (((N_DEV,) + t.shape) if gather else t.shape),
                       t.dtype) for a, t in enumerate(srcs)]
    n_after = 0 if after is None else 1

    def body(*refs):
        src_refs, land_refs = refs[:n], refs[n:2 * n]
        refs = refs[n_after:]
        send_sems, recv_sems = refs[2 * n:3 * n], refs[3 * n:4 * n]
        token = refs[6 * n]
        me = 4 * lax.axis_index("x") + 2 * lax.axis_index("y") + lax.axis_index("c")
        for a in range(n):
            for p in range(1, N_DEV):
                dev, peer = _peer_of(p)
                pltpu.make_async_remote_copy(
                    src_ref=src_refs[a] if gather else src_refs[a].at[peer], dst_ref=_slot(land_refs[a], me, widths[a]),
                    send_sem=send_sems[a].at[p - 1], recv_sem=recv_sems[a].at[p - 1],
                    device_id=dev, device_id_type=MESH).start()
        token[...] = jnp.zeros_like(token)

    sems = tuple(pltpu.SemaphoreType.DMA((N_DEV - 1,)) for _ in range(2 * n))
    thru = tuple(pltpu.HBM(t.shape, t.dtype) for t in list(srcs) + list(lands))
    res = pl.pallas_call(
        body, name=name,
        out_shape=sems + thru + (jax.ShapeDtypeStruct((SUBLANES, LANES), F32),),
        in_specs=[_HBM] * (2 * n) + [pl.BlockSpec(memory_space=pl.ANY)] * n_after,
        out_specs=tuple([_SEM] * (2 * n) + [_HBM] * (2 * n) + [pl.BlockSpec(memory_space=pltpu.VMEM)]),
        input_output_aliases={i: 2 * n + i for i in range(2 * n)},
        compiler_params=pltpu.CompilerParams(has_side_effects=_EFFECT),
    )(*[pltpu.with_memory_space_constraint(t, pltpu.HBM) for t in list(srcs) + list(lands)],
      *([after] if n_after else []))
    return res[:n], res[n:2 * n], res[2 * n:3 * n], res[3 * n:4 * n], res[4 * n]


def _xwait(src, land, send_sem, recv_sem, after, gather, name, col=False, place=True):
    width = src.shape[1] if col else None

    def body(src_ref, land_ref, send_ref, recv_ref, after_ref, src_dead, land_out):
        del after_ref, src_dead, land_out
        for p in range(1, N_DEV):
            dev, peer = _peer_of(p)
            cp = pltpu.make_async_remote_copy(
                src_ref=src_ref if gather else src_ref.at[peer], dst_ref=_slot(land_ref, peer, width),
                send_sem=send_ref.at[p - 1], recv_sem=recv_ref.at[p - 1], device_id=dev, device_id_type=MESH)
            cp.wait_send()
            cp.wait_recv()

    src_done, landed = pl.pallas_call(
        body, name=name, out_shape=(pltpu.HBM(src.shape, src.dtype), pltpu.HBM(land.shape, land.dtype)),
        in_specs=[_HBM, _HBM, _SEM, _SEM, pl.BlockSpec(memory_space=pl.ANY)], out_specs=(_HBM, _HBM),
        input_output_aliases={0: 0, 1: 1},
        compiler_params=pltpu.CompilerParams(has_side_effects=_EFFECT),
    )(src, land, send_sem, recv_sem, after)
    if not place:
        return src_done, landed
    me = 4 * lax.axis_index("x") + 2 * lax.axis_index("y") + lax.axis_index("c")
    return _place_own(landed, src_done, me, col, gather, name + "_own")


def _place_own(zone, src, me, col, gather, name):
    if col:
        R, C = src.shape
        src_spec = lambda tr: pl.BlockSpec((tr, C), lambda i, me_ref: (i, 0))
        out_spec = lambda tr: pl.BlockSpec((tr, C), lambda i, me_ref: (i, me_ref[0]))
    else:
        R, C = zone.shape[1:]
        src_spec = ((lambda tr: pl.BlockSpec((tr, C), lambda i, me_ref: (i, 0))) if gather else
                    (lambda tr: pl.BlockSpec((None, tr, C), lambda i, me_ref: (me_ref[0], i, 0))))
        out_spec = lambda tr: pl.BlockSpec((None, tr, C), lambda i, me_ref: (me_ref[0], i, 0))
    tr = R if R <= 512 else 256
    assert R % tr == 0, (name, R, tr)

    def body(me_ref, src_ref, zone_ref, out_ref):
        del me_ref, zone_ref
        out_ref[...] = src_ref[...]

    return pl.pallas_call(
        body, name=name, out_shape=jax.ShapeDtypeStruct(zone.shape, zone.dtype),
        grid_spec=pltpu.PrefetchScalarGridSpec(
            num_scalar_prefetch=1, grid=(R // tr,),
            in_specs=[src_spec(tr), pl.BlockSpec(memory_space=pl.ANY)], out_specs=out_spec(tr)),
        input_output_aliases={2: 0},
    )(jnp.reshape(me, (1,)).astype(jnp.int32), src, zone)


def _xwait_many(srcs, lands, send_sems, recv_sems, after, name):
    n = len(srcs)

    def body(*refs):
        src_refs, land_refs = refs[:n], refs[n:2 * n]
        snd, rcv = refs[2 * n:3 * n], refs[3 * n:4 * n]
        for a in range(n):
            for p in range(1, N_DEV):
                dev, peer = _peer_of(p)
                cp = pltpu.make_async_remote_copy(
                    src_ref=src_refs[a], dst_ref=land_refs[a].at[peer], send_sem=snd[a].at[p - 1],
                    recv_sem=rcv[a].at[p - 1], device_id=dev, device_id_type=MESH)
                cp.wait_send()
                cp.wait_recv()

    res = pl.pallas_call(
        body, name=name, out_shape=tuple(pltpu.HBM(t.shape, t.dtype) for t in list(srcs) + list(lands)),
        in_specs=[_HBM] * (2 * n) + [_SEM] * (2 * n) + [pl.BlockSpec(memory_space=pl.ANY)],
        out_specs=tuple([_HBM] * (2 * n)), input_output_aliases={i: i for i in range(2 * n)},
        compiler_params=pltpu.CompilerParams(has_side_effects=_EFFECT),
    )(*srcs, *lands, *send_sems, *recv_sems, after)
    return res[:n], res[n:]


def _mm(a, b, *, mode, tm, tn, tk, outs, epilogue=None, extras=(), nb=None, tok=None, scatter=None, into=None,
        a_fn=None, name):
    if mode == "nn":
        (M, K), (_, N) = a.shape, b.shape
    elif mode == "nt":
        (M, K), (N, _) = a.shape, b.shape
    else:
        (K, M), (_, N) = a.shape, b.shape
    tm, tn, tk = min(tm, M), min(tn, N), min(tk, K)
    assert M % tm == 0 and N % tn == 0 and K % tk == 0, (name, M, N, K, tm, tn, tk)
    if mode == "nn":
        a_spec = pl.BlockSpec((tm, tk), lambda i, j, k: (i, k))
        b_spec = pl.BlockSpec((tk, tn), lambda i, j, k: (k, j))
        dims = (((1,), (0,)), ((), ()))
    elif mode == "nt":
        a_spec = pl.BlockSpec((tm, tk), lambda i, j, k: (i, k))
        b_spec = pl.BlockSpec((tn, tk), lambda i, j, k: (j, k))
        dims = (((1,), (1,)), ((), ()))
    else:
        a_spec = pl.BlockSpec((tk, tm), lambda i, j, k: (k, i))
        b_spec = pl.BlockSpec((tk, tn), lambda i, j, k: (k, j))
        dims = (((0,), (0,)), ((), ()))
    nk = K // tk
    n_ex, n_out = len(extras), len(outs)
    n_tok = 0 if tok is None else 1
    nbytes = lambda d: jnp.dtype(d).itemsize
    vmem_est = (2 * (tm * tk * nbytes(a.dtype) + tk * tn * nbytes(b.dtype)
                     + sum(tm * tn * nbytes(e.dtype) for e, kind in extras if kind == "tile")
                     + sum(tm * tn * nbytes(d) for d in outs)) + tm * tn * 4)
    assert vmem_est <= VMEM_LIMIT, (name, vmem_est)
    if epilogue is None:
        epilogue = lambda acc, ex: tuple(acc.astype(d) for d in outs)

    n_into = 0 if into is None else 1

    def body(a_ref, b_ref, *refs):
        refs = refs[n_tok:]
        ex_refs, out_refs = refs[:n_ex], refs[n_ex + n_into:n_ex + n_into + n_out]

        def finish(acc):
            res = epilogue(acc, [r[...] for r in ex_refs])
            for o_ref, v in zip(out_refs, res):
                if nb is None:
                    o_ref[...] = v.astype(o_ref.dtype)
                else:
                    for q in range(tn // nb):
                        o_ref[q] = v[:, q * nb:(q + 1) * nb].astype(o_ref.dtype)

        a_tile = a_ref[...] if a_fn is None else a_fn(a_ref[...])
        part = lax.dot_general(a_tile, b_ref[...], dims, preferred_element_type=F32)
        if nk == 1:
            finish(part)
        else:
            acc_ref = refs[n_ex + n_into + n_out]
            k = pl.program_id(2)

            @pl.when(k == 0)
            def _():
                acc_ref[...] = part

            @pl.when(k > 0)
            def _():
                acc_ref[...] += part

            @pl.when(k == nk - 1)
            def _():
                finish(acc_ref[...])

    col = (lambda j: j) if scatter is None else (lambda j: scatter[0] * j + scatter[1])
    ex_specs = [pl.BlockSpec((tm, tn), lambda i, j, k: (i, j)) if kind == "tile"
                else pl.BlockSpec((1, tn), lambda i, j, k: (0, col(j))) for _, kind in extras]
    if nb is not None:
        assert tn % nb == 0, (name, tn, nb)
        o_spec = pl.BlockSpec((tn // nb, tm, nb), lambda i, j, k: (j, i, 0))
        o_shape = (N // nb, M, nb)
    else:
        o_spec = pl.BlockSpec((tm, tn), lambda i, j, k: (i, col(j)))
        o_shape = (M, N if scatter is None else scatter[2])
    assert n_into == 0 or n_out == 1
    res = pl.pallas_call(
        body, name=name, grid=(M // tm, N // tn, nk),
        in_specs=[a_spec, b_spec] + [pl.BlockSpec((SUBLANES, LANES), lambda i, j, k: (0, 0))] * n_tok + ex_specs
                 + [pl.BlockSpec(memory_space=pl.ANY)] * n_into,
        out_specs=tuple([o_spec] * n_out),
        out_shape=tuple(jax.ShapeDtypeStruct(o_shape, d) for d in outs),
        input_output_aliases={2 + n_tok + n_ex: 0} if n_into else {},
        scratch_shapes=[pltpu.VMEM((tm, tn), F32)] if nk > 1 else [],
        compiler_params=_cparams(3, big=True),
    )(a, b, *([tok] if n_tok else []), *[e for e, _ in extras], *([into] if n_into else []))
    return res[0] if n_out == 1 else res


def _mm_rows(a, b, *, mode, tm, seq, ins, outs, epilogue, tok=None, a_fn=None, name):
    M, K = a.shape
    b_parts = list(b) if isinstance(b, (list, tuple)) else [b]
    n_part = len(b_parts)
    assert n_part == 1 or mode == "nt"
    N = b_parts[0].shape[1] if mode == "nn" else b_parts[0].shape[0]
    tm = min(tm, M)
    assert M % tm == 0 and seq % tm == 0, (name, M, seq, tm)
    tpb = seq // tm
    n_b = M // seq
    dims = (((1,), (0,)), ((), ())) if mode == "nn" else (((1,), (1,)), ((), ()))
    n_tok = 0 if tok is None else 1
    n_in, n_out = len(ins), len(outs)

    in_specs, in_arrs = [], []
    for spec in ins:
        kind, arr = spec[0], spec[1]
        in_arrs.append(arr)
        if kind == "tile":
            in_specs.append(pl.BlockSpec((tm, arr.shape[1]), lambda i: (i, 0)))
        elif kind == "tilecol":
            in_specs.append(pl.BlockSpec((tm, spec[2]), lambda i, cb=spec[3]: (i, cb)))
        elif kind == "row":
            in_specs.append(pl.BlockSpec(arr.shape, lambda i: (0, 0)))
        else:
            in_specs.append(pl.BlockSpec((None, 1, arr.shape[2]), lambda i: (i // tpb, 0, 0)))
    out_specs, out_shapes = [], []
    for spec in outs:
        kind = spec[0]
        if kind == "tile":
            out_specs.append(pl.BlockSpec((tm, spec[2]), lambda i: (i, 0)))
            out_shapes.append(jax.ShapeDtypeStruct((M, spec[2]), spec[1]))
        elif kind == "tilecol":
            out_specs.append(pl.BlockSpec((tm, spec[2]), lambda i, cb=spec[3]: (i, cb)))
            out_shapes.append(jax.ShapeDtypeStruct((M, spec[4]), spec[1]))
        elif kind == "acc_row":
            out_specs.append(pl.BlockSpec((1, spec[1]), lambda i: (0, 0)))
            out_shapes.append(jax.ShapeDtypeStruct((1, spec[1]), F32))
        elif kind == "acc_brow":
            out_specs.append(pl.BlockSpec((None, 1, spec[1]), lambda i: (i // tpb, 0, 0)))
            out_shapes.append(jax.ShapeDtypeStruct((n_b, 1, spec[1]), F32))
        else:
            out_specs.append(pl.BlockSpec((SUBLANES, LANES), lambda i: (0, 0)))
            out_shapes.append(jax.ShapeDtypeStruct((SUBLANES, LANES), F32))

    def body(a_ref, *refs):
        b_refs, refs = refs[:n_part], refs[n_part + n_tok:]
        in_refs, out_refs = refs[:n_in], refs[n_in:n_in + n_out]
        i = pl.program_id(0)
        if n_part == 1:
            a_tile = a_ref[...] if a_fn is None else a_fn(a_ref[...])
            prod = lax.dot_general(a_tile, b_refs[0][...], dims, preferred_element_type=F32)
        else:
            w = b_parts[0].shape[1] // N_DEV
            prod = None
            for q in range(n_part):
                a_q = jnp.concatenate([a_ref[:, (n_part * j + q) * w:(n_part * j + q + 1) * w] for j in range(N_DEV)],
                                      axis=1)
                pq = lax.dot_general(a_q, b_refs[q][...], dims, preferred_element_type=F32)
                prod = pq if prod is None else prod + pq
        vals = epilogue(prod, [r[...] for r in in_refs])
        for spec, o_ref, v in zip(outs, out_refs, vals):
            kind = spec[0]
            if kind in ("tile", "tilecol"):
                off = 0
                for part in (v if isinstance(v, tuple) else (v,)):
                    o_ref[:, off:off + part.shape[1]] = part.astype(o_ref.dtype)
                    off += part.shape[1]
            else:
                first = (i % tpb == 0) if kind == "acc_brow" else (i == 0)

                @pl.when(first)
                def _(o_ref=o_ref, v=v):
                    o_ref[...] = jnp.broadcast_to(v, o_ref.shape)

                @pl.when(jnp.logical_not(first))
                def _(o_ref=o_ref, v=v):
                    o_ref[...] += v

    res = pl.pallas_call(
        body, name=name, grid=(M // tm,),
        in_specs=[pl.BlockSpec((tm, K), lambda i: (i, 0))]
                 + [pl.BlockSpec(bp.shape, lambda i: (0, 0), pipeline_mode=pl.Buffered(1)) for bp in b_parts]
                 + [pl.BlockSpec((SUBLANES, LANES), lambda i: (0, 0))] * n_tok + in_specs,
        out_specs=tuple(out_specs), out_shape=tuple(out_shapes),
        compiler_params=_cparams(1, big=True),
    )(a, *b_parts, *([tok] if n_tok else []), *in_arrs)
    return res


def _tok_spec(ts, width, col_block=0):
    return pl.BlockSpec((None, ts, width), lambda b, s: (b, s, col_block))


def _brow_spec(width):
    return pl.BlockSpec((None, 1, width), lambda b, s: (b, 0, 0))


def _modulate(x, sc, sh, ts):
    Bl, S, D = x.shape

    def body(x_ref, sc_ref, sh_ref, o_ref):
        o_ref[...] = (x_ref[...] * (1.0 + sc_ref[...]) + sh_ref[...]).astype(BF16)

    return pl.pallas_call(
        body, name="modulate", grid=(Bl, S // ts),
        in_specs=[_tok_spec(ts, D), _brow_spec(D), _brow_spec(D)],
        out_specs=_tok_spec(ts, D), out_shape=jax.ShapeDtypeStruct((Bl, S, D), BF16),
        compiler_params=_cparams(2),
    )(x, sc, sh)


def _mix_fwd(proj, w_conv, b_conv, w_rg_a, b_rg_a, w_rg_x, b_rg_x, lam, w_sp, b_sp_t, ln_v_g, ln_v_b, *, tm, lw, sw):
    Bl, S, _ = proj.shape
    heads, hd = w_rg_a.shape[0], w_rg_a.shape[1]
    groups = w_sp.shape[0]
    cw = 2 * lw + 2 * sw
    nblk = tm // SGU_BLOCK

    G = tm // SUBLANES
    nc = lw // LANES

    def body(p_ref, wc_ref, bc_ref, wa_ref, ba_ref, wx_ref, bx_ref, lam_ref, wsp_ref, bsp_ref, lg_ref, lb_ref,
             hs_ref, ya_ref, ys_ref, xc_ref, r_ref, ig_ref, a_ref, m_ref,
             xext, hnat, hcar, h7_scr, a7_scr, hp_scr, h0_scr, cp_scr):
        s = pl.program_id(1)

        @pl.when(s == 0)
        def _():
            xext[:, 0:SUBLANES, :] = jnp.zeros((nc, SUBLANES, LANES), F32)
            hcar[...] = jnp.zeros_like(hcar)

        @pl.when(s > 0)
        def _():
            xext[:, 0:SUBLANES, :] = xext[:, tm:tm + SUBLANES, :]

        nl = -lam_ref[...]
        big_l = -LRU_C * (jnp.maximum(nl, 0.0) + _log1p_pos(jnp.exp(-jnp.abs(nl))))

        for c in range(nc):
            cs = slice(c * LANES, (c + 1) * LANES)
            xext[c, SUBLANES:SUBLANES + tm, :] = p_ref[:, cs].astype(F32)
            xs = {st: xext[c, pl.ds(st, G, stride=SUBLANES), :] for st in range(SUBLANES - 3, 2 * SUBLANES)}
            wcs = [wc_ref[k:k + 1, cs] for k in range(4)]
            xc_j = []
            for j in range(SUBLANES):
                acc = bc_ref[:, cs] + xs[SUBLANES + j] * wcs[3]
                for k in (1, 2, 3):
                    acc = acc + xs[SUBLANES + j - k] * wcs[3 - k]
                xc_j.append(acc)
                xc_ref[j * G:(j + 1) * G, cs] = acc
            xcb = jnp.concatenate(xc_j, axis=0).astype(BF16)
            pa = jnp.dot(xcb, wa_ref[c], preferred_element_type=F32)
            px = jnp.dot(xcb, wx_ref[c], preferred_element_type=F32)
            h0 = cp = None
            for j in range(SUBLANES):
                rs = slice(j * G, (j + 1) * G)
                r = _sigmoid(pa[rs] + ba_ref[:, cs])
                ig = _sigmoid(px[rs] + bx_ref[:, cs])
                la = big_l[:, cs] * r
                a = jnp.exp(la)
                th = jnp.tanh(la)
                msq = (-2.0 * th) * pl.reciprocal(1.0 - th, approx=True)
                m = msq * lax.rsqrt(jnp.maximum(msq, 1e-30))
                b = m * (ig * xc_j[j])
                r_ref[rs, cs] = r
                ig_ref[rs, cs] = ig
                a_ref[rs, cs] = a
                m_ref[rs, cs] = m
                h0 = b if j == 0 else a * h0 + b
                cp = a if j == 0 else a * cp
                h0_scr[rs, cs] = h0
                cp_scr[rs, cs] = cp
            h7_scr[:, cs] = h0
            a7_scr[:, cs] = cp
        carry = hcar[0:1, :]
        for g in range(G):
            hp_scr[g:g + 1, :] = carry
            carry = h7_scr[g:g + 1, :] + a7_scr[g:g + 1, :] * carry
        hcar[0:1, :] = carry
        for c in range(nc):
            cs = slice(c * LANES, (c + 1) * LANES)
            hprev = hp_scr[:, cs]
            for j in range(SUBLANES):
                rs = slice(j * G, (j + 1) * G)
                hnat[c, pl.ds(j, G, stride=SUBLANES), :] = h0_scr[rs, cs] + cp_scr[rs, cs] * hprev
            hs = hnat[c]
            hs_ref[:, cs] = hs
            ya_ref[:, cs] = (hs * _gelu(p_ref[:, lw + c * LANES:lw + (c + 1) * LANES].astype(F32))).astype(BF16)

        gu = _gelu(p_ref[:, 2 * lw:2 * lw + sw].astype(F32))
        gv = _gelu(p_ref[:, 2 * lw + sw:cw].astype(F32))
        xhat, _ = _ln_stats(gv)
        vn = (xhat * lg_ref[...] + lb_ref[...]).astype(BF16)
        tpos = lax.broadcasted_iota(jnp.int32, (SGU_BLOCK, SGU_BLOCK), 0) // CHUNK
        spos = lax.broadcasted_iota(jnp.int32, (SGU_BLOCK, SGU_BLOCK), 1) // CHUNK
        gw = sw // groups
        rows_out = []
        for blk in range(nblk):
            r0 = blk * SGU_BLOCK
            cols = []
            for g in range(groups):
                wm = jnp.where(spos <= tpos, wsp_ref[g], 0.0).astype(BF16)
                mixed = jnp.dot(wm, vn[r0:r0 + SGU_BLOCK, g * gw:(g + 1) * gw], preferred_element_type=F32)
                cols.append(mixed + bsp_ref[:, g:g + 1])
            rows_out.append(jnp.concatenate(cols, axis=1))
        mixed_all = jnp.concatenate(rows_out, axis=0) if nblk > 1 else rows_out[0]
        ys_ref[...] = (gu * mixed_all).astype(BF16)

    full = lambda shp: pl.BlockSpec(shp, lambda b, s: (0,) * len(shp))
    return pl.pallas_call(
        body, name="mix_fwd", grid=(Bl, S // tm),
        in_specs=[_tok_spec(tm, cw), full(w_conv.shape), full(b_conv.shape), full(w_rg_a.shape), full(b_rg_a.shape),
                  full(w_rg_x.shape), full(b_rg_x.shape), full(lam.shape), full(w_sp.shape), full(b_sp_t.shape),
                  full(ln_v_g.shape), full(ln_v_b.shape)],
        out_specs=(_tok_spec(tm, lw), _tok_spec(tm, lw), _tok_spec(tm, sw)) + (_tok_spec(tm, lw),) * 5,
        out_shape=(jax.ShapeDtypeStruct((Bl, S, lw), F32), jax.ShapeDtypeStruct((Bl, S, lw), BF16),
                   jax.ShapeDtypeStruct((Bl, S, sw), BF16)) + (jax.ShapeDtypeStruct((Bl, S, lw), F32),) * 5,
        scratch_shapes=[pltpu.VMEM((nc, tm + SUBLANES, LANES), F32), pltpu.VMEM((nc, tm, LANES), F32),
                        pltpu.VMEM((SUBLANES, lw), F32), pltpu.VMEM((G, lw), F32), pltpu.VMEM((G, lw), F32),
                        pltpu.VMEM((G, lw), F32), pltpu.VMEM((tm, lw), F32), pltpu.VMEM((tm, lw), F32)],
        compiler_params=_cparams(2, big=True),
    )(proj, w_conv, b_conv, w_rg_a, b_rg_a, w_rg_x, b_rg_x, lam, w_sp, b_sp_t, ln_v_g, ln_v_b)


def _mix_bwd(proj, hs, dya, dys, dproj, saved, w_conv, b_conv, w_rg_a, b_rg_a, w_rg_x, b_rg_x, lam, w_sp, b_sp_t,
             ln_v_g, ln_v_b, *, tm, lw, sw):
    Bl, S, din = proj.shape
    heads, hd = w_rg_a.shape[0], w_rg_a.shape[1]
    groups = w_sp.shape[0]
    gw = sw // groups
    cw = 2 * lw + 2 * sw
    nblk = tm // SGU_BLOCK
    n_s = S // tm
    per8 = tm // SUBLANES
    halo_rows = 2 * SUBLANES

    G = tm // SUBLANES
    nc = lw // LANES

    def body(p_ref, xh_ref, hs_ref, hh_ref, dya_ref, dys_ref, dpin_ref, xc_ref, r_ref, ig_ref, a_ref, m_ref,
             wc_ref, bc_ref, wa_ref, ba_ref, wx_ref, bx_ref, lam_ref, wsp_ref, bsp_ref, lg_ref, lb_ref,
             dp_ref, dbin_ref, dwc_ref, dbc_ref, dwa_ref, dba_ref, dwx_ref, dbx_ref, dlam_ref, dwsp_ref, dbsp_ref,
             dlg_ref, dlb_ref,
             xext, hext, dnat, dxext, dhcar, g00_scr, p0_scr, a0_scr, cin_scr, g0_scr, pp_scr):
        del dpin_ref
        sr = pl.program_id(1)
        first_tile = sr == n_s - 1

        @pl.when(_first_step())
        def _():
            for ref in (dbin_ref, dwc_ref, dbc_ref, dwa_ref, dba_ref, dwx_ref, dbx_ref, dlam_ref, dwsp_ref, dbsp_ref,
                        dlg_ref, dlb_ref):
                ref[...] = jnp.zeros_like(ref)

        @pl.when(sr == 0)
        def _():
            dhcar[...] = jnp.zeros_like(dhcar)
            dxext[:, tm:tm + SUBLANES, :] = jnp.zeros((nc, SUBLANES, LANES), F32)

        @pl.when(sr > 0)
        def _():
            dxext[:, tm:tm + SUBLANES, :] = dxext[:, 0:SUBLANES, :]

        keep = jnp.where(first_tile, 0.0, 1.0)
        xprev = xh_ref[...].astype(F32)[halo_rows - SUBLANES:halo_rows] * keep
        hprev8 = hh_ref[...] * keep
        nl = -lam_ref[...]
        big_l = -LRU_C * (jnp.maximum(nl, 0.0) + _log1p_pos(jnp.exp(-jnp.abs(nl))))
        dlam_scale = LRU_C * _sigmoid(nl)
        nt = (((1,), (1,)), ((), ()))
        tn = (((0,), (0,)), ((), ()))
        last = SUBLANES - 1

        for c in range(nc):
            cs = slice(c * LANES, (c + 1) * LANES)
            gcs = slice(lw + c * LANES, lw + (c + 1) * LANES)
            xext[c, 0:SUBLANES, :] = xprev[:, cs]
            xext[c, SUBLANES:SUBLANES + tm, :] = p_ref[:, cs].astype(F32)
            hext[c, 0:SUBLANES, :] = hprev8[:, cs]
            dgl_sum = None
            for i in range(SUBLANES):
                rs = slice(i * G, (i + 1) * G)
                ggl, dggl = _gelu_and_grad(p_ref[rs, gcs].astype(F32))
                dy = dya_ref[rs, cs].astype(F32)
                hsv = hs_ref[rs, cs]
                hext[c, SUBLANES + i * G:SUBLANES + (i + 1) * G, :] = hsv
                dgl = dy * hsv * dggl
                dp_ref[rs, gcs] = dgl.astype(BF16)
                dnat[c, rs, :] = dy * ggl
                dgl_sum = _fold8(dgl) if i == 0 else dgl_sum + _fold8(dgl)
            dbin_ref[:, gcs] += _colsum(dgl_sum)
            g0 = pp = None
            for j in range(last, -1, -1):
                rs = slice(j * G, (j + 1) * G)
                dhs_j = dnat[c, pl.ds(j, G, stride=SUBLANES), :]
                if j == last:
                    g0 = dhs_j
                else:
                    an = a_ref[(j + 1) * G:(j + 2) * G, cs]
                    g0 = dhs_j + an * g0
                    pp = an if j == last - 1 else an * pp
                    pp_scr[rs, cs] = pp
                g0_scr[rs, cs] = g0
            g00_scr[:, cs] = g0
            p0_scr[:, cs] = pp
            a0_scr[:, cs] = a_ref[0:G, cs]
        cin = dhcar[0:1, :]
        for g in range(G - 1, -1, -1):
            cin_scr[g:g + 1, :] = cin
            cin = a0_scr[g:g + 1, :] * (g00_scr[g:g + 1, :] + p0_scr[g:g + 1, :] * cin)
        dhcar[0:1, :] = cin
        for c in range(nc):
            cs = slice(c * LANES, (c + 1) * LANES)
            cinv = cin_scr[:, cs]
            dpa_j, dpx_j, dxc_j = [], [], []
            dlam_sum = dba_sum = dbx_sum = None
            for j in range(SUBLANES):
                rs = slice(j * G, (j + 1) * G)
                dh = g0_scr[rs, cs] + (cinv if j == last else pp_scr[rs, cs] * cinv)
                hprev = hext[c, pl.ds(last + j, G, stride=SUBLANES), :]
                xc, r, ig, a, m = xc_ref[rs, cs], r_ref[rs, cs], ig_ref[rs, cs], a_ref[rs, cs], m_ref[rs, cs]
                dixc = dh * m
                dla = (dh * hprev) * a - (dh * (ig * xc)) * ((a * a) * pl.reciprocal(m, approx=True))
                dpa = (dla * big_l[:, cs]) * r * (1.0 - r)
                dpx = (dixc * xc) * ig * (1.0 - ig)
                dpa_j.append(dpa)
                dpx_j.append(dpx)
                dxc_j.append(dixc * ig)
                sums = (_fold8(dla * r), _fold8(dpa), _fold8(dpx))
                dlam_sum, dba_sum, dbx_sum = sums if j == 0 else (dlam_sum + sums[0], dba_sum + sums[1], dbx_sum + sums[2])
            dlam_ref[:, cs] += _colsum(dlam_sum) * dlam_scale[:, cs]
            dba_ref[:, cs] += _colsum(dba_sum)
            dbx_ref[:, cs] += _colsum(dbx_sum)
            dpab = jnp.concatenate(dpa_j, axis=0).astype(BF16)
            dpxb = jnp.concatenate(dpx_j, axis=0).astype(BF16)
            xcb = xc_ref[:, cs].astype(BF16)
            dxc = (jnp.concatenate(dxc_j, axis=0)
                   + lax.dot_general(dpab, wa_ref[c], nt, preferred_element_type=F32)
                   + lax.dot_general(dpxb, wx_ref[c], nt, preferred_element_type=F32))
            dwa_ref[c] += lax.dot_general(xcb, dpab, tn, preferred_element_type=F32)
            dwx_ref[c] += lax.dot_general(xcb, dpxb, tn, preferred_element_type=F32)

            dbc_ref[:, cs] += _colsum(dxc)
            xs = {st: xext[c, pl.ds(st, G, stride=SUBLANES), :] for st in range(SUBLANES - 3, 2 * SUBLANES)}
            for k in range(4):
                tot = None
                for j in range(SUBLANES):
                    part = _fold8(dxc[j * G:(j + 1) * G] * xs[SUBLANES + j - (3 - k)])
                    tot = part if tot is None else tot + part
                dwc_ref[k:k + 1, cs] += _colsum(tot)
            for j in range(SUBLANES):
                dxext[c, pl.ds(j, G, stride=SUBLANES), :] = dxc[j * G:(j + 1) * G]
            us = {st: dxext[c, pl.ds(st, G, stride=SUBLANES), :] for st in range(SUBLANES + 3)}
            wcs = [wc_ref[k:k + 1, cs] for k in range(4)]
            for j in range(SUBLANES):
                acc = us[j] * wcs[3]
                for k in (1, 2, 3):
                    acc = acc + us[j + k] * wcs[3 - k]
                dnat[c, pl.ds(j, G, stride=SUBLANES), :] = acc
            dxl = dnat[c]
            dp_ref[:, cs] = dxl.astype(BF16)
            dbin_ref[:, cs] += _colsum(dxl)

        gu, dgu_dx = _gelu_and_grad(p_ref[:, 2 * lw:2 * lw + sw].astype(F32))
        gv, dgv_dx = _gelu_and_grad(p_ref[:, 2 * lw + sw:cw].astype(F32))
        xhat, rstd = _ln_stats(gv)
        vn = (xhat * lg_ref[...] + lb_ref[...]).astype(BF16)
        dys = dys_ref[...].astype(F32)
        dmixed = dys * gu
        dmb = dmixed.astype(BF16)
        tpos = lax.broadcasted_iota(jnp.int32, (SGU_BLOCK, SGU_BLOCK), 0) // CHUNK
        spos = lax.broadcasted_iota(jnp.int32, (SGU_BLOCK, SGU_BLOCK), 1) // CHUNK
        causal = spos <= tpos
        mixed_rows, dvn_rows = [], []
        for blk in range(nblk):
            rs = slice(blk * SGU_BLOCK, (blk + 1) * SGU_BLOCK)
            mcols, dcols = [], []
            for g in range(groups):
                cs = slice(g * gw, (g + 1) * gw)
                wm = jnp.where(causal, wsp_ref[g], 0.0).astype(BF16)
                mcols.append(jnp.dot(wm, vn[rs, cs], preferred_element_type=F32) + bsp_ref[:, g:g + 1])
                dcols.append(lax.dot_general(wm, dmb[rs, cs], tn, preferred_element_type=F32))
                dw = lax.dot_general(dmb[rs, cs], vn[rs, cs], nt, preferred_element_type=F32)
                dwsp_ref[g] += jnp.where(causal, dw, 0.0)
                dbsp_ref[:, g:g + 1] += jnp.sum(dmixed[rs, cs], axis=1, keepdims=True)
            mixed_rows.append(jnp.concatenate(mcols, axis=1))
            dvn_rows.append(jnp.concatenate(dcols, axis=1))
        mixed_all = jnp.concatenate(mixed_rows, axis=0) if nblk > 1 else mixed_rows[0]
        dvn = jnp.concatenate(dvn_rows, axis=0) if nblk > 1 else dvn_rows[0]
        du = dys * mixed_all * dgu_dx
        dlg_ref[...] += _colsum(dvn * xhat)
        dlb_ref[...] += _colsum(dvn)
        dv = _ln_bwd(dvn, xhat, rstd, lg_ref[...]) * dgv_dx
        dp_ref[:, 2 * lw:2 * lw + sw] = du.astype(BF16)
        dp_ref[:, 2 * lw + sw:cw] = dv.astype(BF16)
        dbin_ref[:, 2 * lw:2 * lw + sw] += _colsum(du)
        dbin_ref[:, 2 * lw + sw:cw] += _colsum(dv)

    rev = lambda s: n_s - 1 - s
    tile = lambda w: pl.BlockSpec((None, tm, w), lambda b, s: (b, rev(s), 0))
    halo = lambda w: pl.BlockSpec((None, SUBLANES, w), lambda b, s: (b, jnp.maximum(rev(s) * per8 - 1, 0), 0))
    xhalo = pl.BlockSpec((None, halo_rows, lw), lambda b, s: (b, jnp.maximum(rev(s) * (tm // halo_rows) - 1, 0), 0))
    full = lambda shp: pl.BlockSpec(shp, lambda b, s: (0,) * len(shp))
    small = [w_conv, b_conv, w_rg_a, b_rg_a, w_rg_x, b_rg_x, lam, w_sp, b_sp_t, ln_v_g, ln_v_b]
    acc_shapes = [(1, cw), w_conv.shape, b_conv.shape, w_rg_a.shape, b_rg_a.shape, w_rg_x.shape, b_rg_x.shape,
                  lam.shape, w_sp.shape, b_sp_t.shape, ln_v_g.shape, ln_v_b.shape]
    res = pl.pallas_call(
        body, name="mix_bwd", grid=(Bl, n_s),
        in_specs=[tile(cw), xhalo, tile(lw), halo(lw), tile(lw), tile(sw), pl.BlockSpec(memory_space=pl.ANY)]
                 + [tile(lw)] * 5 + [full(w.shape) for w in small],
        out_specs=tuple([tile(cw)] + [full(shp) for shp in acc_shapes]),
        out_shape=tuple([jax.ShapeDtypeStruct((Bl, S, din), BF16)] + [jax.ShapeDtypeStruct(shp, F32) for shp in acc_shapes]),
        input_output_aliases={6: 0},
        scratch_shapes=[pltpu.VMEM((nc, tm + SUBLANES, LANES), F32), pltpu.VMEM((nc, tm + SUBLANES, LANES), F32),
                        pltpu.VMEM((nc, tm, LANES), F32), pltpu.VMEM((nc, tm + SUBLANES, LANES), F32),
                        pltpu.VMEM((SUBLANES, lw), F32), pltpu.VMEM((G, lw), F32), pltpu.VMEM((G, lw), F32),
                        pltpu.VMEM((G, lw), F32), pltpu.VMEM((G, lw), F32), pltpu.VMEM((tm, lw), F32),
                        pltpu.VMEM((tm, lw), F32)],
        compiler_params=_cparams(2, big=True),
    )(proj, proj, hs, hs, dya, dys, dproj, *saved, *small)
    return res


def _ada_fwd(c_all, w_ada):
    R, D = c_all.shape
    nb = w_ada.shape[1]

    def body(c_ref, w_ref, act_ref, o_ref):
        cv = c_ref[...]
        act = (cv * _sigmoid(cv)).astype(BF16)
        act_ref[...] = act
        o_ref[...] = jnp.dot(act, w_ref[...].astype(BF16), preferred_element_type=F32)

    return pl.pallas_call(
        body, name="ada_fwd",
        out_shape=(jax.ShapeDtypeStruct((R, D), BF16), jax.ShapeDtypeStruct((R, nb), F32)),
        compiler_params=pltpu.CompilerParams(vmem_limit_bytes=VMEM_LIMIT),
    )(c_all, w_ada)


def _ada_bwd(c_act, dmod_cols):
    R, D = c_act.shape
    nb = dmod_cols.shape[1]

    def body(act_ref, d_ref, o_ref, b_ref):
        o_ref[...] = lax.dot_general(act_ref[...], d_ref[...].astype(BF16), (((0,), (0,)), ((), ())),
                                     preferred_element_type=F32)
        b_ref[...] = _colsum(d_ref[...])

    return pl.pallas_call(
        body, name="ada_bwd", out_shape=(jax.ShapeDtypeStruct((D, nb), F32), jax.ShapeDtypeStruct((1, nb), F32)),
        compiler_params=pltpu.CompilerParams(vmem_limit_bytes=VMEM_LIMIT),
    )(c_act, dmod_cols)


def _adamw(w, g_slots, m, v, *, tr, name, own=None):
    R, C = w.shape
    n_slot = g_slots.shape[0]
    tr = min(tr, R)
    assert R % tr == 0, (name, R, tr)
    c1 = 1.0 / (1.0 - ADAM_B1 ** ADAM_STEP)
    c2 = 1.0 / (1.0 - ADAM_B2 ** ADAM_STEP)
    n_own = 0 if own is None else 1

    def body(me_ref, w_ref, g_ref, *refs):
        m_ref, v_ref, go_ref, d_ref, mo_ref, vo_ref = refs[n_own:]
        slot = lambda d: (jnp.where(me_ref[0] == d, refs[0][...], g_ref[d]) if n_own else g_ref[d]).astype(F32)
        g = slot(0)
        for d in range(1, n_slot):
            g = g + slot(d)
        mn = ADAM_B1 * m_ref[...] + (1.0 - ADAM_B1) * g
        vn = ADAM_B2 * v_ref[...] + (1.0 - ADAM_B2) * (g * g)
        go_ref[...] = g
        mo_ref[...] = mn
        vo_ref[...] = vn
        d_ref[...] = -ADAM_LR * ((mn * c1) / (jnp.sqrt(vn * c2) + ADAM_EPS) + ADAM_WD * w_ref[...])

    me = 4 * lax.axis_index("x") + 2 * lax.axis_index("y") + lax.axis_index("c")
    blk = pl.BlockSpec((tr, C), lambda i, me_ref: (i, 0))
    own_specs = [pl.BlockSpec((None, tr, C), lambda i, me_ref: (me_ref[0], i, 0))] * n_own
    return pl.pallas_call(
        body, name=name, out_shape=tuple(jax.ShapeDtypeStruct((R, C), F32) for _ in range(4)),
        grid_spec=pltpu.PrefetchScalarGridSpec(
            num_scalar_prefetch=1, grid=(R // tr,),
            in_specs=[blk, pl.BlockSpec((n_slot, tr, C), lambda i, me_ref: (0, i, 0))] + own_specs + [blk, blk],
            out_specs=(blk, blk, blk, blk)),
        compiler_params=_cparams(1, big=True),
    )(jnp.reshape(me, (1,)).astype(jnp.int32), w, g_slots, *([own] if n_own else []), m, v)


def _adamw_many(ws, g_slots, g_owns, ms, vs, *, name):
    n = len(ws)
    c1 = 1.0 / (1.0 - ADAM_B1 ** ADAM_STEP)
    c2 = 1.0 / (1.0 - ADAM_B2 ** ADAM_STEP)

    def body(*refs):
        w_refs, g_refs, o_refs = refs[:n], refs[n:2 * n], refs[2 * n:3 * n]
        m_refs, v_refs = refs[3 * n:4 * n], refs[4 * n:5 * n]
        outs = refs[5 * n:]
        me = 4 * lax.axis_index("x") + 2 * lax.axis_index("y") + lax.axis_index("c")
        for i in range(n):
            own = o_refs[i][...]
            g = jnp.where(me == 0, own, g_refs[i][0])
            for d in range(1, N_DEV):
                g = g + jnp.where(me == d, own, g_refs[i][d])
            mn = ADAM_B1 * m_refs[i][...] + (1.0 - ADAM_B1) * g
            vn = ADAM_B2 * v_refs[i][...] + (1.0 - ADAM_B2) * (g * g)
            outs[i][...] = g
            outs[n + i][...] = -ADAM_LR * ((mn * c1) / (jnp.sqrt(vn * c2) + ADAM_EPS) + ADAM_WD * w_refs[i][...])
            outs[2 * n + i][...] = mn
            outs[3 * n + i][...] = vn

    res = pl.pallas_call(
        body, name=name, out_shape=tuple(jax.ShapeDtypeStruct(w.shape, F32) for _ in range(4) for w in ws),
        compiler_params=pltpu.CompilerParams(vmem_limit_bytes=VMEM_LIMIT),
    )(*ws, *g_slots, *g_owns, *ms, *vs)
    return res[:n], res[n:2 * n], res[2 * n:3 * n], res[3 * n:]


SMALL_NAMES = ("b_ada", "b_in", "b_conv", "w_rg_a", "b_rg_a", "w_rg_x", "b_rg_x", "lru_lambda", "w_sp", "b_sp",
               "ln_v_g", "ln_v_b", "ln1_g", "ln1_b", "ln2_g", "ln2_b")
WEIGHT_ORDER = ("w_ada", "b_ada", "w_in", "b_in", "w_conv", "b_conv", "w_rg_a", "b_rg_a", "w_rg_x", "b_rg_x",
                "lru_lambda", "w_sp", "b_sp", "ln_v_g", "ln_v_b", "w_o_lru", "w_o_sgu", "w_out", "ln1_g", "ln1_b",
                "w_up", "w_down", "ln2_g", "ln2_b")


def _blocked_cols(w2d):
    K, N = w2d.shape
    return jnp.transpose(w2d.reshape(K, N_DEV, N // N_DEV), (1, 0, 2))


def _unblock_cols(wb):
    n, K, nb = wb.shape
    return jnp.transpose(wb, (1, 0, 2)).reshape(K, n * nb)


def kernel(x, c, w_ada, b_ada, w_in, b_in, w_conv, b_conv, w_rg_a, b_rg_a, w_rg_x, b_rg_x, lru_lambda, w_sp, b_sp, ln_v_g, ln_v_b, w_o_lru, w_o_sgu, w_out, ln1_g, ln1_b, w_up, w_down, ln2_g, ln2_b, loss_target, m_w_ada, m_b_ada, m_w_in, m_b_in, m_w_conv, m_b_conv, m_w_rg_a, m_b_rg_a, m_w_rg_x, m_b_rg_x, m_lru_lambda, m_w_sp, m_b_sp, m_ln_v_g, m_ln_v_b, m_w_o_lru, m_w_o_sgu, m_w_out, m_ln1_g, m_ln1_b, m_w_up, m_w_down, m_ln2_g, m_ln2_b, v_w_ada, v_b_ada, v_w_in, v_b_in, v_w_conv, v_b_conv, v_w_rg_a, v_b_rg_a, v_w_rg_x, v_b_rg_x, v_lru_lambda, v_w_sp, v_b_sp, v_ln_v_g, v_ln_v_b, v_w_o_lru, v_w_o_sgu, v_w_out, v_ln1_g, v_ln1_b, v_w_up, v_w_down, v_ln2_g, v_ln2_b):
    W = dict(w_ada=w_ada, b_ada=b_ada, w_in=w_in, b_in=b_in, w_conv=w_conv, b_conv=b_conv, w_rg_a=w_rg_a,
             b_rg_a=b_rg_a, w_rg_x=w_rg_x, b_rg_x=b_rg_x, lru_lambda=lru_lambda, w_sp=w_sp, b_sp=b_sp,
             ln_v_g=ln_v_g, ln_v_b=ln_v_b, w_o_lru=w_o_lru, w_o_sgu=w_o_sgu, w_out=w_out, ln1_g=ln1_g, ln1_b=ln1_b,
             w_up=w_up, w_down=w_down, ln2_g=ln2_g, ln2_b=ln2_b)
    Mo = dict(w_ada=m_w_ada, b_ada=m_b_ada, w_in=m_w_in, b_in=m_b_in, w_conv=m_w_conv, b_conv=m_b_conv,
              w_rg_a=m_w_rg_a, b_rg_a=m_b_rg_a, w_rg_x=m_w_rg_x, b_rg_x=m_b_rg_x, lru_lambda=m_lru_lambda,
              w_sp=m_w_sp, b_sp=m_b_sp, ln_v_g=m_ln_v_g, ln_v_b=m_ln_v_b, w_o_lru=m_w_o_lru, w_o_sgu=m_w_o_sgu,
              w_out=m_w_out, ln1_g=m_ln1_g, ln1_b=m_ln1_b, w_up=m_w_up, w_down=m_w_down, ln2_g=m_ln2_g,
              ln2_b=m_ln2_b)
    Vo = dict(w_ada=v_w_ada, b_ada=v_b_ada, w_in=v_w_in, b_in=v_b_in, w_conv=v_w_conv, b_conv=v_b_conv,
              w_rg_a=v_w_rg_a, b_rg_a=v_b_rg_a, w_rg_x=v_w_rg_x, b_rg_x=v_b_rg_x, lru_lambda=v_lru_lambda,
              w_sp=v_w_sp, b_sp=v_b_sp, ln_v_g=v_ln_v_g, ln_v_b=v_ln_v_b, w_o_lru=v_w_o_lru, w_o_sgu=v_w_o_sgu,
              w_out=v_w_out, ln1_g=v_ln1_g, ln1_b=v_ln1_b, w_up=v_w_up, w_down=v_w_down, ln2_g=v_ln2_g,
              ln2_b=v_ln2_b)

    Bl, S, D = x.shape
    T = Bl * S
    lw = b_conv.shape[-1]
    sw = ln_v_g.shape[-1]
    din = b_in.shape[-1]
    dff = w_up.shape[-1] * N_DEV
    ts = min(2048, S)
    tmix = min(256, S)
    trow = min(512, S)

    c_pad = jnp.pad(c, ((0, SUBLANES - Bl), (0, 0)))
    c_g, wconv_g = _exchange([c_pad, w_conv[0]], True, "xchg_c")
    wconv_full = _unblock_cols(wconv_g)
    c_act, modcols = _ada_fwd(c_g.reshape(N_DEV * SUBLANES, D), w_ada[0])
    (mod_slots,) = _exchange([modcols.reshape(N_DEV, SUBLANES, -1)], False, "xchg_mod")

    nbw = din // N_DEV // WIN_PARTS
    wnames = tuple("win%d" % q for q in range(WIN_PARTS)) + ("wol", "wos", "wout", "wup", "wdown")
    shards = [w_in[0][:, q * nbw:(q + 1) * nbw].astype(BF16) for q in range(WIN_PARTS)] + [
        w_o_lru[0].astype(BF16), w_o_sgu[0].astype(BF16), w_out[0].astype(BF16), w_up[0].astype(BF16),
        w_down[0].astype(BF16)]
    col_sharded = [True] * WIN_PARTS + [False, True, False, True, False]
    g_send, g_recv, g_src, g_land, g_tok = _xstart(shards, True, mod_slots, "gather_start", cols=col_sharded)
    gidx = {n: i for i, n in enumerate(wnames)}

    def gathered(n, after):
        i = gidx[n]
        return _xwait(g_src[i], g_land[i], g_send[i], g_recv[i], after, True, "gather_wait_" + n, col=col_sharded[i])

    mod = _unblock_cols(mod_slots)[:Bl] + (b_ada + g_tok[0, 0])
    sh1, sc1, gt1, sh2, sc2, gt2 = [mod[:, i * D:(i + 1) * D].reshape(Bl, 1, D) for i in range(6)]

    wa_b, wx_b = w_rg_a[0].astype(BF16), w_rg_x[0].astype(BF16)
    b_sp_t = jnp.transpose(b_sp[0])
    small_mix = (wconv_full, b_conv, wa_b, b_rg_a, wx_b, b_rg_x, lru_lambda, w_sp[0], b_sp_t, ln_v_g, ln_v_b)

    h = _modulate(x, sc1, sh1, ts)
    proj, win_parts = None, []
    for q in range(WIN_PARTS):
        wq = gathered("win%d" % q, h if q == 0 else proj)
        win_parts.append(wq)
        proj = _mm(h.reshape(T, D), wq, mode="nn", tm=8192, tn=nbw, tk=D, outs=[BF16], extras=[(b_in, "row")],
                   epilogue=lambda acc, ex: (acc + ex[0],), scatter=(WIN_PARTS, q, din), into=proj,
                   name="mm_proj%d" % q)
    proj3 = proj.reshape(Bl, S, din)
    hs, ya_pre, ysgu, *lru_saved = _mix_fwd(proj3, *small_mix, tm=tmix, lw=lw, sw=sw)
    Wol = gathered("wol", ya_pre).reshape(lw, D)
    Wos = gathered("wos", ysgu)
    y_a = _mm(ya_pre.reshape(T, lw), Wol, mode="nn", tm=2048, tn=D, tk=lw, outs=[BF16], name="mm_ya")
    x2d, tgt2d = x.reshape(T, D), loss_target.reshape(T, D)
    gate_cb = (din - 2 * D) // D

    def ep_merge(y_b, v):
        ya, ga, gb = [t.astype(F32) for t in v]
        yb = y_b.astype(BF16).astype(F32)
        return [yb, _sigmoid(ga) * ya + _sigmoid(gb) * yb]

    y_b, merged = _mm_rows(ysgu.reshape(T, sw), Wos, mode="nn", tm=trow, seq=S,
                           ins=[("tile", y_a), ("tilecol", proj, D, gate_cb), ("tilecol", proj, D, gate_cb + 1)],
                           outs=[("tile", BF16, D), ("tile", BF16, D)], epilogue=ep_merge, name="mm_yb_merge")
    Wout = gathered("wout", merged).reshape(D, D)

    def ep_ln1(mix_acc, v):
        x_, gt, g, b, sc, sh = v
        mixr = mix_acc.astype(BF16).astype(F32)
        xhat, rstd = _ln_stats(ALPHA * x_ + (1.0 + gt) * mixr)
        x1_ = xhat * g + b
        return [mixr, x1_, x1_ * (1.0 + sc) + sh, xhat, jnp.broadcast_to(rstd, (rstd.shape[0], LANES))]

    mix, x1, h2, xhat1, rstd1 = _mm_rows(
        merged, Wout, mode="nn", tm=trow, seq=S,
        ins=[("tile", x2d), ("brow", gt1), ("row", ln1_g), ("row", ln1_b), ("brow", sc2), ("brow", sh2)],
        outs=[("tile", BF16, D), ("tile", F32, D), ("tile", BF16, D), ("tile", BF16, D), ("tile", F32, LANES)],
        epilogue=ep_ln1, name="mm_mix_ln1")
    Wup = gathered("wup", h2)
    relu_up = _mm(h2, Wup, mode="nn", tm=2048, tn=1024, tk=D, outs=[BF16],
                  epilogue=lambda acc, ex: (jnp.maximum(acc, 0.0),), name="mm_up")
    square = lambda t: t * t
    Wdown = gathered("wdown", relu_up).reshape(dff, D)

    def ep_ln2(f_acc, v):
        x1_, t_, gt, g, b = v
        xhat, rstd = _ln_stats(ALPHA * x1_ + (1.0 + gt) * f_acc)
        err = xhat * g + b - t_
        loss_t = 0.5 * jnp.sum(jnp.mean(err * err, axis=-1, keepdims=True))
        dy = err * (1.0 / D)
        dz = _ln_bwd(dy, xhat, rstd, g)
        return [dz * (2.0 * (1.0 + gt)), ALPHA * dz, _colsum(dz * f_acc), _colsum(dy * xhat), _colsum(dy), loss_t]

    df2x, dx1p, dgt2, dg2, db2, loss_part = _mm_rows(
        relu_up, Wdown, mode="nn", tm=trow, seq=S, a_fn=square,
        ins=[("tile", x1), ("tile", tgt2d), ("brow", gt2), ("row", ln2_g), ("row", ln2_b)],
        outs=[("tile", BF16, D), ("tile", F32, D), ("acc_brow", D), ("acc_row", D), ("acc_row", D), ("acc_scalar",)],
        epilogue=ep_ln2, name="mm_down_ln2")
    loss = lax.psum(loss_part[0, 0], ("x", "y", "c"))

    def send_grads(parts, name):
        snd, rcv, src, land, tok = _xstart(parts, False, None, name + "_start")
        return [(src[i], land[i], snd[i], rcv[i]) for i in range(len(parts))], tok

    dup = _mm(df2x, Wdown, mode="nt", tm=2048, tn=1024, tk=D, outs=[BF16], extras=[(relu_up, "tile")],
              epilogue=lambda acc, ex: (acc * ex[0].astype(F32),), name="mm_dup")
    g_wdown = _mm(relu_up, df2x, mode="tn", tm=1024, tn=D, tk=2048, outs=[BF16], a_fn=square,
                  epilogue=lambda acc, ex: (0.5 * acc,), name="mm_gwdown")
    (x_wdown,), tok = send_grads([g_wdown.reshape(N_DEV, dff // N_DEV, D)], "gx_wdown")
    def ep_ln1_bwd(dh2, v):
        dx1p_, x1_, xh_, rs_, mix_, sc, gt, g = v
        mixv = mix_.astype(F32)
        dx1 = dx1p_ + dh2 * (1.0 + sc)
        xhat, rstd = xh_.astype(F32), rs_[:, 0:1]
        dz = _ln_bwd(dx1, xhat, rstd, g)
        return [ALPHA * dz, dz * (1.0 + gt), _colsum(dh2 * x1_), _colsum(dh2), _colsum(dz * mixv),
                _colsum(dx1 * xhat), _colsum(dx1)]

    dxp, dmix, dsc2, dsh2, dgt1, dg1, db1 = _mm_rows(
        dup, Wup, mode="nt", tm=trow, seq=S, tok=tok,
        ins=[("tile", dx1p), ("tile", x1), ("tile", xhat1), ("tile", rstd1), ("tile", mix), ("brow", sc2), ("brow", gt1),
             ("row", ln1_g)],
        outs=[("tile", F32, D), ("tile", BF16, D), ("acc_brow", D), ("acc_brow", D), ("acc_brow", D), ("acc_row", D),
              ("acc_row", D)],
        epilogue=ep_ln1_bwd, name="mm_dh2_ln1b")
    g_wup = _mm(h2, dup, mode="tn", tm=D, tn=1024, tk=2048, outs=[BF16], nb=dff // N_DEV, name="mm_gwup")
    (x_wup,), tok = send_grads([g_wup], "gx_wup")

    def ep_merge_bwd(dm, v):
        ya, yb, ga, gb = [t.astype(F32) for t in v]
        sa, sb = _sigmoid(ga), _sigmoid(gb)
        dga, dgb = dm * ya * sa * (1.0 - sa), dm * yb * sb * (1.0 - sb)
        return [dm * sa, dm * sb, (dga, dgb), jnp.concatenate([_colsum(dga), _colsum(dgb)], axis=1)]

    dy_a, dy_b, dproj, dbin_hi = _mm_rows(
        dmix, Wout, mode="nt", tm=trow, seq=S, tok=tok,
        ins=[("tile", y_a), ("tile", y_b), ("tilecol", proj, D, gate_cb), ("tilecol", proj, D, gate_cb + 1)],
        outs=[("tile", BF16, D), ("tile", BF16, D), ("tilecol", BF16, 2 * D, gate_cb // 2, din), ("acc_row", 2 * D)],
        epilogue=ep_merge_bwd, name="mm_dmerged_mb")
    g_wout = _mm(merged, dmix, mode="tn", tm=D, tn=D, tk=2048, outs=[BF16], name="mm_gwout")
    (x_wout,), tok = send_grads([g_wout.reshape(N_DEV, D // N_DEV, D)], "gx_wout")
    dya_pre = _mm(dy_a, Wol, mode="nt", tm=2048, tn=lw, tk=D, outs=[BF16], tok=tok, name="mm_dya")
    dysgu = _mm(dy_b, Wos, mode="nt", tm=2048, tn=sw, tk=D, outs=[BF16], name="mm_dys")
    g_wol = _mm(ya_pre.reshape(T, lw), dy_a, mode="tn", tm=lw, tn=D, tk=2048, outs=[BF16], name="mm_gwol")
    g_wos = _mm(ysgu.reshape(T, sw), dy_b, mode="tn", tm=sw, tn=D, tk=2048, outs=[BF16], nb=D // N_DEV,
                name="mm_gwos")
    (x_wol, x_wos), tok = send_grads([g_wol.reshape(N_DEV, lw // N_DEV, D), g_wos], "gx_wo")
    small_mix_b = (wconv_full, b_conv + tok[0, 0]) + small_mix[2:]
    (dproj, dbin_lo, g_wconv, g_bconv, g_wa, g_ba, g_wx, g_bx, g_lam, g_wsp, g_bsp_t, g_lvg, g_lvb) = _mix_bwd(
        proj3, hs, dya_pre.reshape(Bl, S, lw), dysgu.reshape(Bl, S, sw), dproj.reshape(Bl, S, din), lru_saved,
        *small_mix_b, tm=tmix, lw=lw, sw=sw)
    dproj2 = dproj.reshape(T, din)
    small_names = [n for n in SMALL_NAMES if n != "b_ada"]
    small_g = dict(b_in=jnp.concatenate([dbin_lo, dbin_hi], axis=-1), b_conv=g_bconv, w_rg_a=g_wa[None], b_rg_a=g_ba,
                   w_rg_x=g_wx[None], b_rg_x=g_bx, lru_lambda=g_lam, w_sp=g_wsp[None],
                   b_sp=jnp.transpose(g_bsp_t)[None], ln_v_g=g_lvg, ln_v_b=g_lvb, ln1_g=dg1, ln1_b=db1, ln2_g=dg2,
                   ln2_b=db2)
    gs_snd, gs_rcv, gs_src, gs_land, tok_s = _xstart([small_g[n] for n in small_names], True, None, "gsmall_start")
    g_win = _mm(h.reshape(T, D), dproj2, mode="tn", tm=D, tn=din // 4, tk=2048, outs=[BF16], nb=din // N_DEV,
                tok=tok_s, name="mm_gwin")
    (x_win,), tok = send_grads([g_win], "gx_win")

    def ep_final(dh, v):
        dxp_, x_, sc = v
        return [dxp_ + dh * (1.0 + sc), _colsum(dh * x_), _colsum(dh)]

    grad_x, dsc1, dsh1 = _mm_rows(dproj2, win_parts, mode="nt", tm=trow, seq=S, tok=tok,
                                  ins=[("tile", dxp), ("tile", x2d), ("brow", sc1)],
                                  outs=[("tile", F32, D), ("acc_brow", D), ("acc_brow", D)], epilogue=ep_final,
                                  name="mm_dh_final")
    grad_x = grad_x.reshape(Bl, S, D)

    out_g, out_d, out_m, out_v = {}, {}, {}, {}

    def adam(name, g_slots, tr, own=None):
        shp = W[name].shape
        w2, m2, v2 = [t.reshape(g_slots.shape[1:]) for t in (W[name], Mo[name], Vo[name])]
        g, d, mn, vn = _adamw(w2, g_slots, m2, v2, tr=tr, name="adam_" + name, own=own)
        out_g[name], out_d[name], out_m[name], out_v[name] = [t.reshape(shp) for t in (g, d, mn, vn)]

    def adam_exchanged(name, handle, tr, after):
        own, slots = _xwait(*handle, after, False, "gx_%s_wait" % name, place=False)
        adam(name, slots, tr, own=own)

    adam_exchanged("w_down", x_wdown, 256, dsh1)
    adam_exchanged("w_up", x_wup, 256, dsh1)
    adam_exchanged("w_out", x_wout, 128, dsh1)
    adam_exchanged("w_o_lru", x_wol, 160, dsh1)
    adam_exchanged("w_o_sgu", x_wos, 256, dsh1)
    gs_own, gs_slots = _xwait_many(gs_src, gs_land, gs_snd, gs_rcv, dsh1, "gsmall_wait")
    res_small = _adamw_many([W[n] for n in small_names], gs_slots, gs_own, [Mo[n] for n in small_names],
                            [Vo[n] for n in small_names], name="adam_small")
    for dst, vals in zip((out_g, out_d, out_m, out_v), res_small):
        dst.update(dict(zip(small_names, vals)))

    dmod = jnp.concatenate([dsh1, dsc1, dgt1, dsh2, dsc2, dgt2], axis=-1).reshape(Bl, 6 * D)
    dmod_b = _blocked_cols(jnp.pad(dmod, ((0, SUBLANES - Bl), (0, 0))))
    dmod_s, gwconv_s = _exchange([dmod_b, _blocked_cols(g_wconv)], False, "xchg_dmod", after=out_g["ln2_b"])
    g_wada, g_bada_mine = _ada_bwd(c_act, dmod_s.reshape(N_DEV * SUBLANES, -1))
    (g_bada_all,) = _exchange([g_bada_mine], True, "xchg_bada")
    adam("w_ada", g_wada[None], 256)
    adam("b_ada", g_bada_all.reshape(1, 1, 6 * D), 1)
    adam("w_conv", gwconv_s, 8)
    adam_exchanged("w_in", x_win, 256, g_bada_all)

    return (loss, grad_x, *[out_g[n] for n in WEIGHT_ORDER], *[out_d[n] for n in WEIGHT_ORDER],
            *[out_m[n] for n in WEIGHT_ORDER], *[out_v[n] for n in WEIGHT_ORDER])
```

```python
import math

import jax
import jax.numpy as jnp
from jax import lax
from jax.experimental import pallas as pl
from jax.experimental.pallas import tpu as pltpu

N_DEV = 8
LN_EPS = 1e-5
LRU_C = 8.0
CHUNK = 64
SGU_BLOCK = 128
ALPHA = 2.0 ** 0.25
ADAM_LR = 0.001
ADAM_B1 = 0.9
ADAM_B2 = 0.999
ADAM_EPS = 1e-08
ADAM_WD = 0.01
ADAM_STEP = 10
GELU_K0 = math.sqrt(2.0 / math.pi)
GELU_K1 = 0.044715

SUBLANES = 8
LANES = 128
VMEM_LIMIT = 56 * 1024 * 1024
WIN_PARTS = 3

F32 = jnp.float32
BF16 = jnp.bfloat16
MESH = pl.DeviceIdType.MESH


def _cparams(n_axes, big=False):
    return pltpu.CompilerParams(dimension_semantics=("arbitrary",) * n_axes,
                                vmem_limit_bytes=VMEM_LIMIT if big else None)


def _sigmoid(x):
    return 0.5 * jnp.tanh(0.5 * x) + 0.5


def _gelu(x):
    t = jnp.tanh(x * (GELU_K0 + (GELU_K0 * GELU_K1) * (x * x)))
    hx = 0.5 * x
    return hx + hx * t


def _gelu_and_grad(x):
    x2 = x * x
    t = jnp.tanh(x * (GELU_K0 + (GELU_K0 * GELU_K1) * x2))
    hx = 0.5 * x
    g = hx + hx * t
    dg = (0.5 + 0.5 * t) + (hx * (1.0 - t * t)) * (GELU_K0 + (3.0 * GELU_K0 * GELU_K1) * x2)
    return g, dg


def _log1p_pos(e):
    p = e * (1.0 - e * (1.0 / 2.0) + e * e * (1.0 / 3.0) - e * e * e * (1.0 / 4.0))
    return jnp.where(e < 1e-2, p, jnp.log(1.0 + e))


def _ln_stats(z):
    mu = jnp.mean(z, axis=-1, keepdims=True)
    zc = z - mu
    var = jnp.mean(zc * zc, axis=-1, keepdims=True)
    rstd = lax.rsqrt(var + LN_EPS)
    return zc * rstd, rstd


def _ln_bwd(dy, xhat, rstd, g):
    dxh = dy * g
    m1 = jnp.mean(dxh, axis=-1, keepdims=True)
    m2 = jnp.mean(dxh * xhat, axis=-1, keepdims=True)
    return rstd * (dxh - m1 - xhat * m2)


def _colsum(v):
    return jnp.sum(v, axis=0, keepdims=True)


def _fold8(v):
    out = v[0:SUBLANES]
    for i in range(1, v.shape[0] // SUBLANES):
        out = out + v[i * SUBLANES:(i + 1) * SUBLANES]
    return out


def _first_step():
    return jnp.logical_and(pl.program_id(0) == 0, pl.program_id(1) == 0)


def _exchange(arrs, gather, name, after=None):
    n = len(arrs)
    n_peer = N_DEV - 1
    n_after = 0 if after is None else 1

    def body(*refs):
        ins, outs = refs[:n], refs[n + n_after:2 * n + n_after]
        send_sems, recv_sems, loc_sems = refs[2 * n + n_after:]
        x, y, c = lax.axis_index("x"), lax.axis_index("y"), lax.axis_index("c")
        me = 4 * x + 2 * y + c
        started = []
        for a in range(n):
            src_me = ins[a] if gather else ins[a].at[me]
            lc = pltpu.make_async_copy(src_me, outs[a].at[me], loc_sems.at[a])
            lc.start()
            started.append((lc, None))
        for p in range(1, N_DEV):
            px, py, pc = x ^ ((p >> 2) & 1), y ^ ((p >> 1) & 1), c ^ (p & 1)
            peer = 4 * px + 2 * py + pc
            for a in range(n):
                k = a * n_peer + (p - 1)
                src = ins[a] if gather else ins[a].at[peer]
                cp = pltpu.make_async_remote_copy(src_ref=src, dst_ref=outs[a].at[me],
                                                  send_sem=send_sems.at[k], recv_sem=recv_sems.at[k],
                                                  device_id=(px, py, pc), device_id_type=MESH)
                cp.start()
                rc = pltpu.make_async_remote_copy(src_ref=src, dst_ref=outs[a].at[peer],
                                                  send_sem=send_sems.at[k], recv_sem=recv_sems.at[k],
                                                  device_id=(px, py, pc), device_id_type=MESH)
                started.append((cp, rc))
        for cp, rc in started:
            if rc is None:
                cp.wait()
            else:
                cp.wait_send()
                rc.wait_recv()

    hbm = pl.BlockSpec(memory_space=pltpu.HBM)
    out_shape = tuple(
        jax.ShapeDtypeStruct(((N_DEV,) + a.shape) if gather else a.shape, a.dtype) for a in arrs)
    return pl.pallas_call(
        body, name=name, out_shape=out_shape,
        in_specs=[hbm] * n + [pl.BlockSpec(memory_space=pl.ANY)] * n_after, out_specs=tuple([hbm] * n),
        scratch_shapes=[pltpu.SemaphoreType.DMA((n * n_peer,)), pltpu.SemaphoreType.DMA((n * n_peer,)),
                        pltpu.SemaphoreType.DMA((n,))],
        compiler_params=pltpu.CompilerParams(has_side_effects=True),
    )(*arrs, *([after] if n_after else []))


_HBM = pl.BlockSpec(memory_space=pltpu.HBM)
_SEM = pl.BlockSpec(memory_space=pltpu.SEMAPHORE)
_EFFECT = pltpu.SideEffectType.DATAFLOW_SIDE_EFFECTING


def _peer_of(p):
    x, y, c = lax.axis_index("x"), lax.axis_index("y"), lax.axis_index("c")
    px, py, pc = x ^ ((p >> 2) & 1), y ^ ((p >> 1) & 1), c ^ (p & 1)
    return (px, py, pc), 4 * px + 2 * py + pc


def _slot(land_ref, idx, width):
    if width is None:
        return land_ref.at[idx]
    return land_ref.at[:, pl.ds(pl.multiple_of(idx * width, LANES), width)]


def _xstart(srcs, gather, after, name, cols=None):
    n = len(srcs)
    cols = cols or [False] * n
    widths = [t.shape[1] if cols[a] else None for a, t in enumerate(srcs)]
    lands = [lax.empty((t.shape[0], N_DEV * t.shape[1]) if cols[a] else (((N_DEV,) + t.shape) if gather else t.shape),
                       t.dtype) for a, t in enumerate(srcs)]
    n_after = 0 if after is None else 1

    def body(*refs):
        src_refs, land_refs = refs[:n], refs[n:2 * n]
        refs = refs[n_after:]
        send_sems, recv_sems = refs[2 * n:3 * n], refs[3 * n:4 * n]
        token = refs[6 * n]
        me = 4 * lax.axis_index("x") + 2 * lax.axis_index("y") + lax.axis_index("c")
        for a in range(n):
            for p in range(1, N_DEV):
                dev, peer = _peer_of(p)
                pltpu.make_async_remote_copy(
                    src_ref=src_refs[a] if gather else src_refs[a].at[peer], dst_ref=_slot(land_refs[a], me, widths[a]),
                    send_sem=send_sems[a].at[p - 1], recv_sem=recv_sems[a].at[p - 1],
                    device_id=dev, device_id_type=MESH).start()
        token[...] = jnp.zeros_like(token)

    sems = tuple(pltpu.SemaphoreType.DMA((N_DEV - 1,)) for _ in range(2 * n))
    thru = tuple(pltpu.HBM(t.shape, t.dtype) for t in list(srcs) + list(lands))
    res = pl.pallas_call(
        body, name=name,
        out_shape=sems + thru + (jax.ShapeDtypeStruct((SUBLANES, LANES), F32),),
        in_specs=[_HBM] * (2 * n) + [pl.BlockSpec(memory_space=pl.ANY)] * n_after,
        out_specs=tuple([_SEM] * (2 * n) + [_HBM] * (2 * n) + [pl.BlockSpec(memory_space=pltpu.VMEM)]),
        input_output_aliases={i: 2 * n + i for i in range(2 * n)},
        compiler_params=pltpu.CompilerParams(has_side_effects=_EFFECT),
    )(*[pltpu.with_memory_space_constraint(t, pltpu.HBM) for t in list(srcs) + list(lands)],
      *([after] if n_after else []))
    return res[:n], res[n:2 * n], res[2 * n:3 * n], res[3 * n:4 * n], res[4 * n]


def _xwait(src, land, send_sem, recv_sem, after, gather, name, col=False, place=True):
    width = src.shape[1] if col else None

    def body(src_ref, land_ref, send_ref, recv_ref, after_ref, src_dead, land_out):
        del after_ref, src_dead, land_out
        for p in range(1, N_DEV):
            dev, peer = _peer_of(p)
            cp = pltpu.make_async_remote_copy(
                src_ref=src_ref if gather else src_ref.at[peer], dst_ref=_slot(land_ref, peer, width),
                send_sem=send_ref.at[p - 1], recv_sem=recv_ref.at[p - 1], device_id=dev, device_id_type=MESH)
            cp.wait_send()
            cp.wait_recv()

    src_done, landed = pl.pallas_call(
        body, name=name, out_shape=(pltpu.HBM(src.shape, src.dtype), pltpu.HBM(land.shape, land.dtype)),
        in_specs=[_HBM, _HBM, _SEM, _SEM, pl.BlockSpec(memory_space=pl.ANY)], out_specs=(_HBM, _HBM),
        input_output_aliases={0: 0, 1: 1},
        compiler_params=pltpu.CompilerParams(has_side_effects=_EFFECT),
    )(src, land, send_sem, recv_sem, after)
    if not place:
        return src_done, landed
    me = 4 * lax.axis_index("x") + 2 * lax.axis_index("y") + lax.axis_index("c")
    return _place_own(landed, src_done, me, col, gather, name + "_own")


def _place_own(zone, src, me, col, gather, name):
    if col:
        R, C = src.shape
        src_spec = lambda tr: pl.BlockSpec((tr, C), lambda i, me_ref: (i, 0))
        out_spec = lambda tr: pl.BlockSpec((tr, C), lambda i, me_ref: (i, me_ref[0]))
    else:
        R, C = zone.shape[1:]
        src_spec = ((lambda tr: pl.BlockSpec((tr, C), lambda i, me_ref: (i, 0))) if gather else
                    (lambda tr: pl.BlockSpec((None, tr, C), lambda i, me_ref: (me_ref[0], i, 0))))
        out_spec = lambda tr: pl.BlockSpec((None, tr, C), lambda i, me_ref: (me_ref[0], i, 0))
    tr = R if R <= 512 else 256
    assert R % tr == 0, (name, R, tr)

    def body(me_ref, src_ref, zone_ref, out_ref):
        del me_ref, zone_ref
        out_ref[...] = src_ref[...]

    return pl.pallas_call(
        body, name=name, out_shape=jax.ShapeDtypeStruct(zone.shape, zone.dtype),
        grid_spec=pltpu.PrefetchScalarGridSpec(
            num_scalar_prefetch=1, grid=(R // tr,),
            in_specs=[src_spec(tr), pl.BlockSpec(memory_space=pl.ANY)], out_specs=out_spec(tr)),
        input_output_aliases={2: 0},
    )(jnp.reshape(me, (1,)).astype(jnp.int32), src, zone)


def _xwait_many(srcs, lands, send_sems, recv_sems, after, name):
    n = len(srcs)

    def body(*refs):
        src_refs, land_refs = refs[:n], refs[n:2 * n]
        snd, rcv = refs[2 * n:3 * n], refs[3 * n:4 * n]
        for a in range(n):
            for p in range(1, N_DEV):
                dev, peer = _peer_of(p)
                cp = pltpu.make_async_remote_copy(
                    src_ref=src_refs[a], dst_ref=land_refs[a].at[peer], send_sem=snd[a].at[p - 1],
                    recv_sem=rcv[a].at[p - 1], device_id=dev, device_id_type=MESH)
                cp.wait_send()
                cp.wait_recv()

    res = pl.pallas_call(
        body, name=name, out_shape=tuple(pltpu.HBM(t.shape, t.dtype) for t in list(srcs) + list(lands)),
        in_specs=[_HBM] * (2 * n) + [_SEM] * (2 * n) + [pl.BlockSpec(memory_space=pl.ANY)],
        out_specs=tuple([_HBM] * (2 * n)), input_output_aliases={i: i for i in range(2 * n)},
        compiler_params=pltpu.CompilerParams(has_side_effects=_EFFECT),
    )(*srcs, *lands, *send_sems, *recv_sems, after)
    return res[:n], res[n:]


def _mm(a, b, *, mode, tm, tn, tk, outs, epilogue=None, extras=(), nb=None, tok=None, scatter=None, into=None,
        a_fn=None, name):
    if mode == "nn":
        (M, K), (_, N) = a.shape, b.shape
    elif mode == "nt":
        (M, K), (N, _) = a.shape, b.shape
    else:
        (K, M), (_, N) = a.shape, b.shape
    tm, tn, tk = min(tm, M), min(tn, N), min(tk, K)
    assert M % tm == 0 and N % tn == 0 and K % tk == 0, (name, M, N, K, tm, tn, tk)
    if mode == "nn":
        a_spec = pl.BlockSpec((tm, tk), lambda i, j, k: (i, k))
        b_spec = pl.BlockSpec((tk, tn), lambda i, j, k: (k, j))
        dims = (((1,), (0,)), ((), ()))
    elif mode == "nt":
        a_spec = pl.BlockSpec((tm, tk), lambda i, j, k: (i, k))
        b_spec = pl.BlockSpec((tn, tk), lambda i, j, k: (j, k))
        dims = (((1,), (1,)), ((), ()))
    else:
        a_spec = pl.BlockSpec((tk, tm), lambda i, j, k: (k, i))
        b_spec = pl.BlockSpec((tk, tn), lambda i, j, k: (k, j))
        dims = (((0,), (0,)), ((), ()))
    nk = K // tk
    n_ex, n_out = len(extras), len(outs)
    n_tok = 0 if tok is None else 1
    nbytes = lambda d: jnp.dtype(d).itemsize
    vmem_est = (2 * (tm * tk * nbytes(a.dtype) + tk * tn * nbytes(b.dtype)
                     + sum(tm * tn * nbytes(e.dtype) for e, kind in extras if kind == "tile")
                     + sum(tm * tn * nbytes(d) for d in outs)) + tm * tn * 4)
    assert vmem_est <= VMEM_LIMIT, (name, vmem_est)
    if epilogue is None:
        epilogue = lambda acc, ex: tuple(acc.astype(d) for d in outs)

    n_into = 0 if into is None else 1

    def body(a_ref, b_ref, *refs):
        refs = refs[n_tok:]
        ex_refs, out_refs = refs[:n_ex], refs[n_ex + n_into:n_ex + n_into + n_out]

        def finish(acc):
            res = epilogue(acc, [r[...] for r in ex_refs])
            for o_ref, v in zip(out_refs, res):
                if nb is None:
                    o_ref[...] = v.astype(o_ref.dtype)
                else:
                    for q in range(tn // nb):
                        o_ref[q] = v[:, q * nb:(q + 1) * nb].astype(o_ref.dtype)

        a_tile = a_ref[...] if a_fn is None else a_fn(a_ref[...])
        part = lax.dot_general(a_tile, b_ref[...], dims, preferred_element_type=F32)
        if nk == 1:
            finish(part)
        else:
            acc_ref = refs[n_ex + n_into + n_out]
            k = pl.program_id(2)

            @pl.when(k == 0)
            def _():
                acc_ref[...] = part

            @pl.when(k > 0)
            def _():
                acc_ref[...] += part

            @pl.when(k == nk - 1)
            def _():
                finish(acc_ref[...])

    col = (lambda j: j) if scatter is None else (lambda j: scatter[0] * j + scatter[1])
    ex_specs = [pl.BlockSpec((tm, tn), lambda i, j, k: (i, j)) if kind == "tile"
                else pl.BlockSpec((1, tn), lambda i, j, k: (0, col(j))) for _, kind in extras]
    if nb is not None:
        assert tn % nb == 0, (name, tn, nb)
        o_spec = pl.BlockSpec((tn // nb, tm, nb), lambda i, j, k: (j, i, 0))
        o_shape = (N // nb, M, nb)
    else:
        o_spec = pl.BlockSpec((tm, tn), lambda i, j, k: (i, col(j)))
        o_shape = (M, N if scatter is None else scatter[2])
    assert n_into == 0 or n_out == 1
    res = pl.pallas_call(
        body, name=name, grid=(M // tm, N // tn, nk),
        in_specs=[a_spec, b_spec] + [pl.BlockSpec((SUBLANES, LANES), lambda i, j, k: (0, 0))] * n_tok + ex_specs
                 + [pl.BlockSpec(memory_space=pl.ANY)] * n_into,
        out_specs=tuple([o_spec] * n_out),
        out_shape=tuple(jax.ShapeDtypeStruct(o_shape, d) for d in outs),
        input_output_aliases={2 + n_tok + n_ex: 0} if n_into else {},
        scratch_shapes=[pltpu.VMEM((tm, tn), F32)] if nk > 1 else [],
        compiler_params=_cparams(3, big=True),
    )(a, b, *([tok] if n_tok else []), *[e for e, _ in extras], *([into] if n_into else []))
    return res[0] if n_out == 1 else res


def _mm_rows(a, b, *, mode, tm, seq, ins, outs, epilogue, tok=None, a_fn=None, name):
    M, K = a.shape
    b_parts = list(b) if isinstance(b, (list, tuple)) else [b]
    n_part = len(b_parts)
    assert n_part == 1 or mode == "nt"
    N = b_parts[0].shape[1] if mode == "nn" else b_parts[0].shape[0]
    tm = min(tm, M)
    assert M % tm == 0 and seq % tm == 0, (name, M, seq, tm)
    tpb = seq // tm
    n_b = M // seq
    dims = (((1,), (0,)), ((), ())) if mode == "nn" else (((1,), (1,)), ((), ()))
    n_tok = 0 if tok is None else 1
    n_in, n_out = len(ins), len(outs)

    in_specs, in_arrs = [], []
    for spec in ins:
        kind, arr = spec[0], spec[1]
        in_arrs.append(arr)
        if kind == "tile":
            in_specs.append(pl.BlockSpec((tm, arr.shape[1]), lambda i: (i, 0)))
        elif kind == "tilecol":
            in_specs.append(pl.BlockSpec((tm, spec[2]), lambda i, cb=spec[3]: (i, cb)))
        elif kind == "row":
            in_specs.append(pl.BlockSpec(arr.shape, lambda i: (0, 0)))
        else:
            in_specs.append(pl.BlockSpec((None, 1, arr.shape[2]), lambda i: (i // tpb, 0, 0)))
    out_specs, out_shapes = [], []
    for spec in outs:
        kind = spec[0]
        if kind == "tile":
            out_specs.append(pl.BlockSpec((tm, spec[2]), lambda i: (i, 0)))
            out_shapes.append(jax.ShapeDtypeStruct((M, spec[2]), spec[1]))
        elif kind == "tilecol":
            out_specs.append(pl.BlockSpec((tm, spec[2]), lambda i, cb=spec[3]: (i, cb)))
            out_shapes.append(jax.ShapeDtypeStruct((M, spec[4]), spec[1]))
        elif kind == "acc_row":
            out_specs.append(pl.BlockSpec((1, spec[1]), lambda i: (0, 0)))
            out_shapes.append(jax.ShapeDtypeStruct((1, spec[1]), F32))
        elif kind == "acc_brow":
            out_specs.append(pl.BlockSpec((None, 1, spec[1]), lambda i: (i // tpb, 0, 0)))
            out_shapes.append(jax.ShapeDtypeStruct((n_b, 1, spec[1]), F32))
        else:
            out_specs.append(pl.BlockSpec((SUBLANES, LANES), lambda i: (0, 0)))
            out_shapes.append(jax.ShapeDtypeStruct((SUBLANES, LANES), F32))

    def body(a_ref, *refs):
        b_refs, refs = refs[:n_part], refs[n_part + n_tok:]
        in_refs, out_refs = refs[:n_in], refs[n_in:n_in + n_out]
        i = pl.program_id(0)
        if n_part == 1:
            a_tile = a_ref[...] if a_fn is None else a_fn(a_ref[...])
            prod = lax.dot_general(a_tile, b_refs[0][...], dims, preferred_element_type=F32)
        else:
            w = b_parts[0].shape[1] // N_DEV
            prod = None
            for q in range(n_part):
                a_q = jnp.concatenate([a_ref[:, (n_part * j + q) * w:(n_part * j + q + 1) * w] for j in range(N_DEV)],
                                      axis=1)
                pq = lax.dot_general(a_q, b_refs[q][...], dims, preferred_element_type=F32)
                prod = pq if prod is None else prod + pq
        vals = epilogue(prod, [r[...] for r in in_refs])
        for spec, o_ref, v in zip(outs, out_refs, vals):
            kind = spec[0]
            if kind in ("tile", "tilecol"):
                off = 0
                for part in (v if isinstance(v, tuple) else (v,)):
                    o_ref[:, off:off + part.shape[1]] = part.astype(o_ref.dtype)
                    off += part.shape[1]
            else:
                first = (i % tpb == 0) if kind == "acc_brow" else (i == 0)

                @pl.when(first)
                def _(o_ref=o_ref, v=v):
                    o_ref[...] = jnp.broadcast_to(v, o_ref.shape)

                @pl.when(jnp.logical_not(first))
                def _(o_ref=o_ref, v=v):
                    o_ref[...] += v

    res = pl.pallas_call(
        body, name=name, grid=(M // tm,),
        in_specs=[pl.BlockSpec((tm, K), lambda i: (i, 0))]
                 + [pl.BlockSpec(bp.shape, lambda i: (0, 0), pipeline_mode=pl.Buffered(1)) for bp in b_parts]
                 + [pl.BlockSpec((SUBLANES, LANES), lambda i: (0, 0))] * n_tok + in_specs,
        out_specs=tuple(out_specs), out_shape=tuple(out_shapes),
        compiler_params=_cparams(1, big=True),
    )(a, *b_parts, *([tok] if n_tok else []), *in_arrs)
    return res


def _tok_spec(ts, width, col_block=0):
    return pl.BlockSpec((None, ts, width), lambda b, s: (b, s, col_block))


def _brow_spec(width):
    return pl.BlockSpec((None, 1, width), lambda b, s: (b, 0, 0))


def _modulate(x, sc, sh, ts):
    Bl, S, D = x.shape

    def body(x_ref, sc_ref, sh_ref, o_ref):
        o_ref[...] = (x_ref[...] * (1.0 + sc_ref[...]) + sh_ref[...]).astype(BF16)

    return pl.pallas_call(
        body, name="modulate", grid=(Bl, S // ts),
        in_specs=[_tok_spec(ts, D), _brow_spec(D), _brow_spec(D)],
        out_specs=_tok_spec(ts, D), out_shape=jax.ShapeDtypeStruct((Bl, S, D), BF16),
        compiler_params=_cparams(2),
    )(x, sc, sh)


def _mix_fwd(proj, w_conv, b_conv, w_rg_a, b_rg_a, w_rg_x, b_rg_x, lam, w_sp, b_sp_t, ln_v_g, ln_v_b, *, tm, lw, sw):
    Bl, S, _ = proj.shape
    heads, hd = w_rg_a.shape[0], w_rg_a.shape[1]
    groups = w_sp.shape[0]
    cw = 2 * lw + 2 * sw
    nblk = tm // SGU_BLOCK

    G = tm // SUBLANES
    nc = lw // LANES

    def body(p_ref, wc_ref, bc_ref, wa_ref, ba_ref, wx_ref, bx_ref, lam_ref, wsp_ref, bsp_ref, lg_ref, lb_ref,
             hs_ref, ya_ref, ys_ref, xc_ref, r_ref, ig_ref, a_ref, m_ref,
             xext, hnat, hcar, h7_scr, a7_scr, hp_scr, h0_scr, cp_scr):
        s = pl.program_id(1)

        @pl.when(s == 0)
        def _():
            xext[:, 0:SUBLANES, :] = jnp.zeros((nc, SUBLANES, LANES), F32)
            hcar[...] = jnp.zeros_like(hcar)

        @pl.when(s > 0)
        def _():
            xext[:, 0:SUBLANES, :] = xext[:, tm:tm + SUBLANES, :]

        nl = -lam_ref[...]
        big_l = -LRU_C * (jnp.maximum(nl, 0.0) + _log1p_pos(jnp.exp(-jnp.abs(nl))))

        for c in range(nc):
            cs = slice(c * LANES, (c + 1) * LANES)
            xext[c, SUBLANES:SUBLANES + tm, :] = p_ref[:, cs].astype(F32)
            xs = {st: xext[c, pl.ds(st, G, stride=SUBLANES), :] for st in range(SUBLANES - 3, 2 * SUBLANES)}
            wcs = [wc_ref[k:k + 1, cs] for k in range(4)]
            xc_j = []
            for j in range(SUBLANES):
                acc = bc_ref[:, cs] + xs[SUBLANES + j] * wcs[3]
                for k in (1, 2, 3):
                    acc = acc + xs[SUBLANES + j - k] * wcs[3 - k]
                xc_j.append(acc)
                xc_ref[j * G:(j + 1) * G, cs] = acc
            xcb = jnp.concatenate(xc_j, axis=0).astype(BF16)
            pa = jnp.dot(xcb, wa_ref[c], preferred_element_type=F32)
            px = jnp.dot(xcb, wx_ref[c], preferred_element_type=F32)
            h0 = cp = None
            for j in range(SUBLANES):
                rs = slice(j * G, (j + 1) * G)
                r = _sigmoid(pa[rs] + ba_ref[:, cs])
                ig = _sigmoid(px[rs] + bx_ref[:, cs])
                la = big_l[:, cs] * r
                a = jnp.exp(la)
                th = jnp.tanh(la)
                msq = (-2.0 * th) * pl.reciprocal(1.0 - th, approx=True)
                m = msq * lax.rsqrt(jnp.maximum(msq, 1e-30))
                b = m * (ig * xc_j[j])
                r_ref[rs, cs] = r
                ig_ref[rs, cs] = ig
                a_ref[rs, cs] = a
                m_ref[rs, cs] = m
                h0 = b if j == 0 else a * h0 + b
                cp = a if j == 0 else a * cp
                h0_scr[rs, cs] = h0
                cp_scr[rs, cs] = cp
            h7_scr[:, cs] = h0
            a7_scr[:, cs] = cp
        carry = hcar[0:1, :]
        for g in range(G):
            hp_scr[g:g + 1, :] = carry
            carry = h7_scr[g:g + 1, :] + a7_scr[g:g + 1, :] * carry
        hcar[0:1, :] = carry
        for c in range(nc):
            cs = slice(c * LANES, (c + 1) * LANES)
            hprev = hp_scr[:, cs]
            for j in range(SUBLANES):
                rs = slice(j * G, (j + 1) * G)
                hnat[c, pl.ds(j, G, stride=SUBLANES), :] = h0_scr[rs, cs] + cp_scr[rs, cs] * hprev
            hs = hnat[c]
            hs_ref[:, cs] = hs
            ya_ref[:, cs] = (hs * _gelu(p_ref[:, lw + c * LANES:lw + (c + 1) * LANES].astype(F32))).astype(BF16)

        gu = _gelu(p_ref[:, 2 * lw:2 * lw + sw].astype(F32))
        gv = _gelu(p_ref[:, 2 * lw + sw:cw].astype(F32))
        xhat, _ = _ln_stats(gv)
        vn = (xhat * lg_ref[...] + lb_ref[...]).astype(BF16)
        tpos = lax.broadcasted_iota(jnp.int32, (SGU_BLOCK, SGU_BLOCK), 0) // CHUNK
        spos = lax.broadcasted_iota(jnp.int32, (SGU_BLOCK, SGU_BLOCK), 1) // CHUNK
        gw = sw // groups
        rows_out = []
        for blk in range(nblk):
            r0 = blk * SGU_BLOCK
            cols = []
            for g in range(groups):
                wm = jnp.where(spos <= tpos, wsp_ref[g], 0.0).astype(BF16)
                mixed = jnp.dot(wm, vn[r0:r0 + SGU_BLOCK, g * gw:(g + 1) * gw], preferred_element_type=F32)
                cols.append(mixed + bsp_ref[:, g:g + 1])
            rows_out.append(jnp.concatenate(cols, axis=1))
        mixed_all = jnp.concatenate(rows_out, axis=0) if nblk > 1 else rows_out[0]
        ys_ref[...] = (gu * mixed_all).astype(BF16)

    full = lambda shp: pl.BlockSpec(shp, lambda b, s: (0,) * len(shp))
    return pl.pallas_call(
        body, name="mix_fwd", grid=(Bl, S // tm),
        in_specs=[_tok_spec(tm, cw), full(w_conv.shape), full(b_conv.shape), full(w_rg_a.shape), full(b_rg_a.shape),
                  full(w_rg_x.shape), full(b_rg_x.shape), full(lam.shape), full(w_sp.shape), full(b_sp_t.shape),
                  full(ln_v_g.shape), full(ln_v_b.shape)],
        out_specs=(_tok_spec(tm, lw), _tok_spec(tm, lw), _tok_spec(tm, sw)) + (_tok_spec(tm, lw),) * 5,
        out_shape=(jax.ShapeDtypeStruct((Bl, S, lw), F32), jax.ShapeDtypeStruct((Bl, S, lw), BF16),
                   jax.ShapeDtypeStruct((Bl, S, sw), BF16)) + (jax.ShapeDtypeStruct((Bl, S, lw), F32),) * 5,
        scratch_shapes=[pltpu.VMEM((nc, tm + SUBLANES, LANES), F32), pltpu.VMEM((nc, tm, LANES), F32),
                        pltpu.VMEM((SUBLANES, lw), F32), pltpu.VMEM((G, lw), F32), pltpu.VMEM((G, lw), F32),
                        pltpu.VMEM((G, lw), F32), pltpu.VMEM((tm, lw), F32), pltpu.VMEM((tm, lw), F32)],
        compiler_params=_cparams(2, big=True),
    )(proj, w_conv, b_conv, w_rg_a, b_rg_a, w_rg_x, b_rg_x, lam, w_sp, b_sp_t, ln_v_g, ln_v_b)


def _mix_bwd(proj, hs, dya, dys, dproj, saved, w_conv, b_conv, w_rg_a, b_rg_a, w_rg_x, b_rg_x, lam, w_sp, b_sp_t,
             ln_v_g, ln_v_b, *, tm, lw, sw):
    Bl, S, din = proj.shape
    heads, hd = w_rg_a.shape[0], w_rg_a.shape[1]
    groups = w_sp.shape[0]
    gw = sw // groups
    cw = 2 * lw + 2 * sw
    nblk = tm // SGU_BLOCK
    n_s = S // tm
    per8 = tm // SUBLANES
    halo_rows = 2 * SUBLANES

    G = tm // SUBLANES
    nc = lw // LANES

    def body(p_ref, xh_ref, hs_ref, hh_ref, dya_ref, dys_ref, dpin_ref, xc_ref, r_ref, ig_ref, a_ref, m_ref,
             wc_ref, bc_ref, wa_ref, ba_ref, wx_ref, bx_ref, lam_ref, wsp_ref, bsp_ref, lg_ref, lb_ref,
             dp_ref, dbin_ref, dwc_ref, dbc_ref, dwa_ref, dba_ref, dwx_ref, dbx_ref, dlam_ref, dwsp_ref, dbsp_ref,
             dlg_ref, dlb_ref,
             xext, hext, dnat, dxext, dhcar, g00_scr, p0_scr, a0_scr, cin_scr, g0_scr, pp_scr):
        del dpin_ref
        sr = pl.program_id(1)
        first_tile = sr == n_s - 1

        @pl.when(_first_step())
        def _():
            for ref in (dbin_ref, dwc_ref, dbc_ref, dwa_ref, dba_ref, dwx_ref, dbx_ref, dlam_ref, dwsp_ref, dbsp_ref,
                        dlg_ref, dlb_ref):
                ref[...] = jnp.zeros_like(ref)

        @pl.when(sr == 0)
        def _():
            dhcar[...] = jnp.zeros_like(dhcar)
            dxext[:, tm:tm + SUBLANES, :] = jnp.zeros((nc, SUBLANES, LANES), F32)

        @pl.when(sr > 0)
        def _():
            dxext[:, tm:tm + SUBLANES, :] = dxext[:, 0:SUBLANES, :]

        keep = jnp.where(first_tile, 0.0, 1.0)
        xprev = xh_ref[...].astype(F32)[halo_rows - SUBLANES:halo_rows] * keep
        hprev8 = hh_ref[...] * keep
        nl = -lam_ref[...]
        big_l = -LRU_C * (jnp.maximum(nl, 0.0) + _log1p_pos(jnp.exp(-jnp.abs(nl))))
        dlam_scale = LRU_C * _sigmoid(nl)
        nt = (((1,), (1,)), ((), ()))
        tn = (((0,), (0,)), ((), ()))
        last = SUBLANES - 1

        for c in range(nc):
            cs = slice(c * LANES, (c + 1) * LANES)
            gcs = slice(lw + c * LANES, lw + (c + 1) * LANES)
            xext[c, 0:SUBLANES, :] = xprev[:, cs]
            xext[c, SUBLANES:SUBLANES + tm, :] = p_ref[:, cs].astype(F32)
            hext[c, 0:SUBLANES, :] = hprev8[:, cs]
            dgl_sum = None
            for i in range(SUBLANES):
                rs = slice(i * G, (i + 1) * G)
                ggl, dggl = _gelu_and_grad(p_ref[rs, gcs].astype(F32))
                dy = dya_ref[rs, cs].astype(F32)
                hsv = hs_ref[rs, cs]
                hext[c, SUBLANES + i * G:SUBLANES + (i + 1) * G, :] = hsv
                dgl = dy * hsv * dggl
                dp_ref[rs, gcs] = dgl.astype(BF16)
                dnat[c, rs, :] = dy * ggl
                dgl_sum = _fold8(dgl) if i == 0 else dgl_sum + _fold8(dgl)
            dbin_ref[:, gcs] += _colsum(dgl_sum)
            g0 = pp = None
            for j in range(last, -1, -1):
                rs = slice(j * G, (j + 1) * G)
                dhs_j = dnat[c, pl.ds(j, G, stride=SUBLANES), :]
                if j == last:
                    g0 = dhs_j
                else:
                    an = a_ref[(j + 1) * G:(j + 2) * G, cs]
                    g0 = dhs_j + an * g0
                    pp = an if j == last - 1 else an * pp
                    pp_scr[rs, cs] = pp
                g0_scr[rs, cs] = g0
            g00_scr[:, cs] = g0
            p0_scr[:, cs] = pp
            a0_scr[:, cs] = a_ref[0:G, cs]
        cin = dhcar[0:1, :]
        for g in range(G - 1, -1, -1):
            cin_scr[g:g + 1, :] = cin
            cin = a0_scr[g:g + 1, :] * (g00_scr[g:g + 1, :] + p0_scr[g:g + 1, :] * cin)
        dhcar[0:1, :] = cin
        for c in range(nc):
            cs = slice(c * LANES, (c + 1) * LANES)
            cinv = cin_scr[:, cs]
            dpa_j, dpx_j, dxc_j = [], [], []
            dlam_sum = dba_sum = dbx_sum = None
            for j in range(SUBLANES):
                rs = slice(j * G, (j + 1) * G)
                dh = g0_scr[rs, cs] + (cinv if j == last else pp_scr[rs, cs] * cinv)
                hprev = hext[c, pl.ds(last + j, G, stride=SUBLANES), :]
                xc, r, ig, a, m = xc_ref[rs, cs], r_ref[rs, cs], ig_ref[rs, cs], a_ref[rs, cs], m_ref[rs, cs]
                dixc = dh * m
                dla = (dh * hprev) * a - (dh * (ig * xc)) * ((a * a) * pl.reciprocal(m, approx=True))
                dpa = (dla * big_l[:, cs]) * r * (1.0 - r)
                dpx = (dixc * xc) * ig * (1.0 - ig)
                dpa_j.append(dpa)
                dpx_j.append(dpx)
                dxc_j.append(dixc * ig)
                sums = (_fold8(dla * r), _fold8(dpa), _fold8(dpx))
                dlam_sum, dba_sum, dbx_sum = sums if j == 0 else (dlam_sum + sums[0], dba_sum + sums[1], dbx_sum + sums[2])
            dlam_ref[:, cs] += _colsum(dlam_sum) * dlam_scale[:, cs]
            dba_ref[:, cs] += _colsum(dba_sum)
            dbx_ref[:, cs] += _colsum(dbx_sum)
            dpab = jnp.concatenate(dpa_j, axis=0).astype(BF16)
            dpxb = jnp.concatenate(dpx_j, axis=0).astype(BF16)
            xcb = xc_ref[:, cs].astype(BF16)
            dxc = (jnp.concatenate(dxc_j, axis=0)
                   + lax.dot_general(dpab, wa_ref[c], nt, preferred_element_type=F32)
                   + lax.dot_general(dpxb, wx_ref[c], nt, preferred_element_type=F32))
            dwa_ref[c] += lax.dot_general(xcb, dpab, tn, preferred_element_type=F32)
            dwx_ref[c] += lax.dot_general(xcb, dpxb, tn, preferred_element_type=F32)

            dbc_ref[:, cs] += _colsum(dxc)
            xs = {st: xext[c, pl.ds(st, G, stride=SUBLANES), :] for st in range(SUBLANES - 3, 2 * SUBLANES)}
            for k in range(4):
                tot = None
                for j in range(SUBLANES):
                    part = _fold8(dxc[j * G:(j + 1) * G] * xs[SUBLANES + j - (3 - k)])
                    tot = part if tot is None else tot + part
                dwc_ref[k:k + 1, cs] += _colsum(tot)
            for j in range(SUBLANES):
                dxext[c, pl.ds(j, G, stride=SUBLANES), :] = dxc[j * G:(j + 1) * G]
            us = {st: dxext[c, pl.ds(st, G, stride=SUBLANES), :] for st in range(SUBLANES + 3)}
            wcs = [wc_ref[k:k + 1, cs] for k in range(4)]
            for j in range(SUBLANES):
                acc = us[j] * wcs[3]
                for k in (1, 2, 3):
                    acc = acc + us[j + k] * wcs[3 - k]
                dnat[c, pl.ds(j, G, stride=SUBLANES), :] = acc
            dxl = dnat[c]
            dp_ref[:, cs] = dxl.astype(BF16)
            dbin_ref[:, cs] += _colsum(dxl)

        gu, dgu_dx = _gelu_and_grad(p_ref[:, 2 * lw:2 * lw + sw].astype(F32))
        gv, dgv_dx = _gelu_and_grad(p_ref[:, 2 * lw + sw:cw].astype(F32))
        xhat, rstd = _ln_stats(gv)
        vn = (xhat * lg_ref[...] + lb_ref[...]).astype(BF16)
        dys = dys_ref[...].astype(F32)
        dmixed = dys * gu
        dmb = dmixed.astype(BF16)
        tpos = lax.broadcasted_iota(jnp.int32, (SGU_BLOCK, SGU_BLOCK), 0) // CHUNK
        spos = lax.broadcasted_iota(jnp.int32, (SGU_BLOCK, SGU_BLOCK), 1) // CHUNK
        causal = spos <= tpos
        mixed_rows, dvn_rows = [], []
        for blk in range(nblk):
            rs = slice(blk * SGU_BLOCK, (blk + 1) * SGU_BLOCK)
            mcols, dcols = [], []
            for g in range(groups):
                cs = slice(g * gw, (g + 1) * gw)
                wm = jnp.where(causal, wsp_ref[g], 0.0).astype(BF16)
                mcols.append(jnp.dot(wm, vn[rs, cs], preferred_element_type=F32) + bsp_ref[:, g:g + 1])
                dcols.append(lax.dot_general(wm, dmb[rs, cs], tn, preferred_element_type=F32))
                dw = lax.dot_general(dmb[rs, cs], vn[rs, cs], nt, preferred_element_type=F32)
                dwsp_ref[g] += jnp.where(causal, dw, 0.0)
                dbsp_ref[:, g:g + 1] += jnp.sum(dmixed[rs, cs], axis=1, keepdims=True)
            mixed_rows.append(jnp.concatenate(mcols, axis=1))
            dvn_rows.append(jnp.concatenate(dcols, axis=1))
        mixed_all = jnp.concatenate(mixed_rows, axis=0) if nblk > 1 else mixed_rows[0]
        dvn = jnp.concatenate(dvn_rows, axis=0) if nblk > 1 else dvn_rows[0]
        du = dys * mixed_all * dgu_dx
        dlg_ref[...] += _colsum(dvn * xhat)
        dlb_ref[...] += _colsum(dvn)
        dv = _ln_bwd(dvn, xhat, rstd, lg_ref[...]) * dgv_dx
        dp_ref[:, 2 * lw:2 * lw + sw] = du.astype(BF16)
        dp_ref[:, 2 * lw + sw:cw] = dv.astype(BF16)
        dbin_ref[:, 2 * lw:2 * lw + sw] += _colsum(du)
        dbin_ref[:, 2 * lw + sw:cw] += _colsum(dv)

    rev = lambda s: n_s - 1 - s
    tile = lambda w: pl.BlockSpec((None, tm, w), lambda b, s: (b, rev(s), 0))
    halo = lambda w: pl.BlockSpec((None, SUBLANES, w), lambda b, s: (b, jnp.maximum(rev(s) * per8 - 1, 0), 0))
    xhalo = pl.BlockSpec((None, halo_rows, lw), lambda b, s: (b, jnp.maximum(rev(s) * (tm // halo_rows) - 1, 0), 0))
    full = lambda shp: pl.BlockSpec(shp, lambda b, s: (0,) * len(shp))
    small = [w_conv, b_conv, w_rg_a, b_rg_a, w_rg_x, b_rg_x, lam, w_sp, b_sp_t, ln_v_g, ln_v_b]
    acc_shapes = [(1, cw), w_conv.shape, b_conv.shape, w_rg_a.shape, b_rg_a.shape, w_rg_x.shape, b_rg_x.shape,
                  lam.shape, w_sp.shape, b_sp_t.shape, ln_v_g.shape, ln_v_b.shape]
    res = pl.pallas_call(
        body, name="mix_bwd", grid=(Bl, n_s),
        in_specs=[tile(cw), xhalo, tile(lw), halo(lw), tile(lw), tile(sw), pl.BlockSpec(memory_space=pl.ANY)]
                 + [tile(lw)] * 5 + [full(w.shape) for w in small],
        out_specs=tuple([tile(cw)] + [full(shp) for shp in acc_shapes]),
        out_shape=tuple([jax.ShapeDtypeStruct((Bl, S, din), BF16)] + [jax.ShapeDtypeStruct(shp, F32) for shp in acc_shapes]),
        input_output_aliases={6: 0},
        scratch_shapes=[pltpu.VMEM((nc, tm + SUBLANES, LANES), F32), pltpu.VMEM((nc, tm + SUBLANES, LANES), F32),
                        pltpu.VMEM((nc, tm, LANES), F32), pltpu.VMEM((nc, tm + SUBLANES, LANES), F32),
                        pltpu.VMEM((SUBLANES, lw), F32), pltpu.VMEM((G, lw), F32), pltpu.VMEM((G, lw), F32),
                        pltpu.VMEM((G, lw), F32), pltpu.VMEM((G, lw), F32), pltpu.VMEM((tm, lw), F32),
                        pltpu.VMEM((tm, lw), F32)],
        compiler_params=_cparams(2, big=True),
    )(proj, proj, hs, hs, dya, dys, dproj, *saved, *small)
    return res


def _ada_fwd(c_all, w_ada):
    R, D = c_all.shape
    nb = w_ada.shape[1]

    def body(c_ref, w_ref, act_ref, o_ref):
        cv = c_ref[...]
        act = (cv * _sigmoid(cv)).astype(BF16)
        act_ref[...] = act
        o_ref[...] = jnp.dot(act, w_ref[...].astype(BF16), preferred_element_type=F32)

    return pl.pallas_call(
        body, name="ada_fwd",
        out_shape=(jax.ShapeDtypeStruct((R, D), BF16), jax.ShapeDtypeStruct((R, nb), F32)),
        compiler_params=pltpu.CompilerParams(vmem_limit_bytes=VMEM_LIMIT),
    )(c_all, w_ada)


def _ada_bwd(c_act, dmod_cols):
    R, D = c_act.shape
    nb = dmod_cols.shape[1]

    def body(act_ref, d_ref, o_ref, b_ref):
        o_ref[...] = lax.dot_general(act_ref[...], d_ref[...].astype(BF16), (((0,), (0,)), ((), ())),
                                     preferred_element_type=F32)
        b_ref[...] = _colsum(d_ref[...])

    return pl.pallas_call(
        body, name="ada_bwd", out_shape=(jax.ShapeDtypeStruct((D, nb), F32), jax.ShapeDtypeStruct((1, nb), F32)),
        compiler_params=pltpu.CompilerParams(vmem_limit_bytes=VMEM_LIMIT),
    )(c_act, dmod_cols)


def _adamw(w, g_slots, m, v, *, tr, name, own=None):
    R, C = w.shape
    n_slot = g_slots.shape[0]
    tr = min(tr, R)
    assert R % tr == 0, (name, R, tr)
    c1 = 1.0 / (1.0 - ADAM_B1 ** ADAM_STEP)
    c2 = 1.0 / (1.0 - ADAM_B2 ** ADAM_STEP)
    n_own = 0 if own is None else 1

    def body(me_ref, w_ref, g_ref, *refs):
        m_ref, v_ref, go_ref, d_ref, mo_ref, vo_ref = refs[n_own:]
        slot = lambda d: (jnp.where(me_ref[0] == d, refs[0][...], g_ref[d]) if n_own else g_ref[d]).astype(F32)
        g = slot(0)
        for d in range(1, n_slot):
            g = g + slot(d)
        mn = ADAM_B1 * m_ref[...] + (1.0 - ADAM_B1) * g
        vn = ADAM_B2 * v_ref[...] + (1.0 - ADAM_B2) * (g * g)
        go_ref[...] = g
        mo_ref[...] = mn
        vo_ref[...] = vn
        d_ref[...] = -ADAM_LR * ((mn * c1) / (jnp.sqrt(vn * c2) + ADAM_EPS) + ADAM_WD * w_ref[...])

    me = 4 * lax.axis_index("x") + 2 * lax.axis_index("y") + lax.axis_index("c")
    blk = pl.BlockSpec((tr, C), lambda i, me_ref: (i, 0))
    own_specs = [pl.BlockSpec((None, tr, C), lambda i, me_ref: (me_ref[0], i, 0))] * n_own
    return pl.pallas_call(
        body, name=name, out_shape=tuple(jax.ShapeDtypeStruct((R, C), F32) for _ in range(4)),
        grid_spec=pltpu.PrefetchScalarGridSpec(
            num_scalar_prefetch=1, grid=(R // tr,),
            in_specs=[blk, pl.BlockSpec((n_slot, tr, C), lambda i, me_ref: (0, i, 0))] + own_specs + [blk, blk],
            out_specs=(blk, blk, blk, blk)),
        compiler_params=_cparams(1, big=True),
    )(jnp.reshape(me, (1,)).astype(jnp.int32), w, g_slots, *([own] if n_own else []), m, v)


def _adamw_many(ws, g_slots, g_owns, ms, vs, *, name):
    n = len(ws)
    c1 = 1.0 / (1.0 - ADAM_B1 ** ADAM_STEP)
    c2 = 1.0 / (1.0 - ADAM_B2 ** ADAM_STEP)

    def body(*refs):
        w_refs, g_refs, o_refs = refs[:n], refs[n:2 * n], refs[2 * n:3 * n]
        m_refs, v_refs = refs[3 * n:4 * n], refs[4 * n:5 * n]
        outs = refs[5 * n:]
        me = 4 * lax.axis_index("x") + 2 * lax.axis_index("y") + lax.axis_index("c")
        for i in range(n):
            own = o_refs[i][...]
            g = jnp.where(me == 0, own, g_refs[i][0])
            for d in range(1, N_DEV):
                g = g + jnp.where(me == d, own, g_refs[i][d])
            mn = ADAM_B1 * m_refs[i][...] + (1.0 - ADAM_B1) * g
            vn = ADAM_B2 * v_refs[i][...] + (1.0 - ADAM_B2) * (g * g)
            outs[i][...] = g
            outs[n + i][...] = -ADAM_LR * ((mn * c1) / (jnp.sqrt(vn * c2) + ADAM_EPS) + ADAM_WD * w_refs[i][...])
            outs[2 * n + i][...] = mn
            outs[3 * n + i][...] = vn

    res = pl.pallas_call(
        body, name=name, out_shape=tuple(jax.ShapeDtypeStruct(w.shape, F32) for _ in range(4) for w in ws),
        compiler_params=pltpu.CompilerParams(vmem_limit_bytes=VMEM_LIMIT),
    )(*ws, *g_slots, *g_owns, *ms, *vs)
    return res[:n], res[n:2 * n], res[2 * n:3 * n], res[3 * n:]


SMALL_NAMES = ("b_ada", "b_in", "b_conv", "w_rg_a", "b_rg_a", "w_rg_x", "b_rg_x", "lru_lambda", "w_sp", "b_sp",
               "ln_v_g", "ln_v_b", "ln1_g", "ln1_b", "ln2_g", "ln2_b")
WEIGHT_ORDER = ("w_ada", "b_ada", "w_in", "b_in", "w_conv", "b_conv", "w_rg_a", "b_rg_a", "w_rg_x", "b_rg_x",
                "lru_lambda", "w_sp", "b_sp", "ln_v_g", "ln_v_b", "w_o_lru", "w_o_sgu", "w_out", "ln1_g", "ln1_b",
                "w_up", "w_down", "ln2_g", "ln2_b")


def _blocked_cols(w2d):
    K, N = w2d.shape
    return jnp.transpose(w2d.reshape(K, N_DEV, N // N_DEV), (1, 0, 2))


def _unblock_cols(wb):
    n, K, nb = wb.shape
    return jnp.transpose(wb, (1, 0, 2)).reshape(K, n * nb)


def kernel(x, c, w_ada, b_ada, w_in, b_in, w_conv, b_conv, w_rg_a, b_rg_a, w_rg_x, b_rg_x, lru_lambda, w_sp, b_sp, ln_v_g, ln_v_b, w_o_lru, w_o_sgu, w_out, ln1_g, ln1_b, w_up, w_down, ln2_g, ln2_b, loss_target, m_w_ada, m_b_ada, m_w_in, m_b_in, m_w_conv, m_b_conv, m_w_rg_a, m_b_rg_a, m_w_rg_x, m_b_rg_x, m_lru_lambda, m_w_sp, m_b_sp, m_ln_v_g, m_ln_v_b, m_w_o_lru, m_w_o_sgu, m_w_out, m_ln1_g, m_ln1_b, m_w_up, m_w_down, m_ln2_g, m_ln2_b, v_w_ada, v_b_ada, v_w_in, v_b_in, v_w_conv, v_b_conv, v_w_rg_a, v_b_rg_a, v_w_rg_x, v_b_rg_x, v_lru_lambda, v_w_sp, v_b_sp, v_ln_v_g, v_ln_v_b, v_w_o_lru, v_w_o_sgu, v_w_out, v_ln1_g, v_ln1_b, v_w_up, v_w_down, v_ln2_g, v_ln2_b):
    W = dict(w_ada=w_ada, b_ada=b_ada, w_in=w_in, b_in=b_in, w_conv=w_conv, b_conv=b_conv, w_rg_a=w_rg_a,
             b_rg_a=b_rg_a, w_rg_x=w_rg_x, b_rg_x=b_rg_x, lru_lambda=lru_lambda, w_sp=w_sp, b_sp=b_sp,
             ln_v_g=ln_v_g, ln_v_b=ln_v_b, w_o_lru=w_o_lru, w_o_sgu=w_o_sgu, w_out=w_out, ln1_g=ln1_g, ln1_b=ln1_b,
             w_up=w_up, w_down=w_down, ln2_g=ln2_g, ln2_b=ln2_b)
    Mo = dict(w_ada=m_w_ada, b_ada=m_b_ada, w_in=m_w_in, b_in=m_b_in, w_conv=m_w_conv, b_conv=m_b_conv,
              w_rg_a=m_w_rg_a, b_rg_a=m_b_rg_a, w_rg_x=m_w_rg_x, b_rg_x=m_b_rg_x, lru_lambda=m_lru_lambda,
              w_sp=m_w_sp, b_sp=m_b_sp, ln_v_g=m_ln_v_g, ln_v_b=m_ln_v_b, w_o_lru=m_w_o_lru, w_o_sgu=m_w_o_sgu,
              w_out=m_w_out, ln1_g=m_ln1_g, ln1_b=m_ln1_b, w_up=m_w_up, w_down=m_w_down, ln2_g=m_ln2_g,
              ln2_b=m_ln2_b)
    Vo = dict(w_ada=v_w_ada, b_ada=v_b_ada, w_in=v_w_in, b_in=v_b_in, w_conv=v_w_conv, b_conv=v_b_conv,
              w_rg_a=v_w_rg_a, b_rg_a=v_b_rg_a, w_rg_x=v_w_rg_x, b_rg_x=v_b_rg_x, lru_lambda=v_lru_lambda,
              w_sp=v_w_sp, b_sp=v_b_sp, ln_v_g=v_ln_v_g, ln_v_b=v_ln_v_b, w_o_lru=v_w_o_lru, w_o_sgu=v_w_o_sgu,
              w_out=v_w_out, ln1_g=v_ln1_g, ln1_b=v_ln1_b, w_up=v_w_up, w_down=v_w_down, ln2_g=v_ln2_g,
              ln2_b=v_ln2_b)

    Bl, S, D = x.shape
    T = Bl * S
    lw = b_conv.shape[-1]
    sw = ln_v_g.shape[-1]
    din = b_in.shape[-1]
    dff = w_up.shape[-1] * N_DEV
    ts = min(2048, S)
    tmix = min(256, S)
    trow = min(512, S)

    c_pad = jnp.pad(c, ((0, SUBLANES - Bl), (0, 0)))
    c_g, wconv_g = _exchange([c_pad, w_conv[0]], True, "xchg_c")
    wconv_full = _unblock_cols(wconv_g)
    c_act, modcols = _ada_fwd(c_g.reshape(N_DEV * SUBLANES, D), w_ada[0])
    (mod_slots,) = _exchange([modcols.reshape(N_DEV, SUBLANES, -1)], False, "xchg_mod")

    nbw = din // N_DEV // WIN_PARTS
    wnames = tuple("win%d" % q for q in range(WIN_PARTS)) + ("wol", "wos", "wout", "wup", "wdown")
    shards = [w_in[0][:, q * nbw:(q + 1) * nbw].astype(BF16) for q in range(WIN_PARTS)] + [
        w_o_lru[0].astype(BF16), w_o_sgu[0].astype(BF16), w_out[0].astype(BF16), w_up[0].astype(BF16),
        w_down[0].astype(BF16)]
    col_sharded = [True] * WIN_PARTS + [False, True, False, True, False]
    g_send, g_recv, g_src, g_land, g_tok = _xstart(shards, True, mod_slots, "gather_start", cols=col_sharded)
    gidx = {n: i for i, n in enumerate(wnames)}

    def gathered(n, after):
        i = gidx[n]
        return _xwait(g_src[i], g_land[i], g_send[i], g_recv[i], after, True, "gather_wait_" + n, col=col_sharded[i])

    mod = _unblock_cols(mod_slots)[:Bl] + (b_ada + g_tok[0, 0])
    sh1, sc1, gt1, sh2, sc2, gt2 = [mod[:, i * D:(i + 1) * D].reshape(Bl, 1, D) for i in range(6)]

    wa_b, wx_b = w_rg_a[0].astype(BF16), w_rg_x[0].astype(BF16)
    b_sp_t = jnp.transpose(b_sp[0])
    small_mix = (wconv_full, b_conv, wa_b, b_rg_a, wx_b, b_rg_x, lru_lambda, w_sp[0], b_sp_t, ln_v_g, ln_v_b)

    h = _modulate(x, sc1, sh1, ts)
    proj, win_parts = None, []
    for q in range(WIN_PARTS):
        wq = gathered("win%d" % q, h if q == 0 else proj)
        win_parts.append(wq)
        proj = _mm(h.reshape(T, D), wq, mode="nn", tm=8192, tn=nbw, tk=D, outs=[BF16], extras=[(b_in, "row")],
                   epilogue=lambda acc, ex: (acc + ex[0],), scatter=(WIN_PARTS, q, din), into=proj,
                   name="mm_proj%d" % q)
    proj3 = proj.reshape(Bl, S, din)
    hs, ya_pre, ysgu, *lru_saved = _mix_fwd(proj3, *small_mix, tm=tmix, lw=lw, sw=sw)
    Wol = gathered("wol", ya_pre).reshape(lw, D)
    Wos = gathered("wos", ysgu)
    y_a = _mm(ya_pre.reshape(T, lw), Wol, mode="nn", tm=2048, tn=D, tk=lw, outs=[BF16], name="mm_ya")
    x2d, tgt2d = x.reshape(T, D), loss_target.reshape(T, D)
    gate_cb = (din - 2 * D) // D

    def ep_merge(y_b, v):
        ya, ga, gb = [t.astype(F32) for t in v]
        yb = y_b.astype(BF16).astype(F32)
        return [yb, _sigmoid(ga) * ya + _sigmoid(gb) * yb]

    y_b, merged = _mm_rows(ysgu.reshape(T, sw), Wos, mode="nn", tm=trow, seq=S,
                           ins=[("tile", y_a), ("tilecol", proj, D, gate_cb), ("tilecol", proj, D, gate_cb + 1)],
                           outs=[("tile", BF16, D), ("tile", BF16, D)], epilogue=ep_merge, name="mm_yb_merge")
    Wout = gathered("wout", merged).reshape(D, D)

    def ep_ln1(mix_acc, v):
        x_, gt, g, b, sc, sh = v
        mixr = mix_acc.astype(BF16).astype(F32)
        xhat, rstd = _ln_stats(ALPHA * x_ + (1.0 + gt) * mixr)
        x1_ = xhat * g + b
        return [mixr, x1_, x1_ * (1.0 + sc) + sh, xhat, jnp.broadcast_to(rstd, (rstd.shape[0], LANES))]

    mix, x1, h2, xhat1, rstd1 = _mm_rows(
        merged, Wout, mode="nn", tm=trow, seq=S,
        ins=[("tile", x2d), ("brow", gt1), ("row", ln1_g), ("row", ln1_b), ("brow", sc2), ("brow", sh2)],
        outs=[("tile", BF16, D), ("tile", F32, D), ("tile", BF16, D), ("tile", BF16, D), ("tile", F32, LANES)],
        epilogue=ep_ln1, name="mm_mix_ln1")
    Wup = gathered("wup", h2)
    relu_up = _mm(h2, Wup, mode="nn", tm=2048, tn=1024, tk=D, outs=[BF16],
                  epilogue=lambda acc, ex: (jnp.maximum(acc, 0.0),), name="mm_up")
    square = lambda t: t * t
    Wdown = gathered("wdown", relu_up).reshape(dff, D)

    def ep_ln2(f_acc, v):
        x1_, t_, gt, g, b = v
        xhat, rstd = _ln_stats(ALPHA * x1_ + (1.0 + gt) * f_acc)
        err = xhat * g + b - t_
        loss_t = 0.5 * jnp.sum(jnp.mean(err * err, axis=-1, keepdims=True))
        dy = err * (1.0 / D)
        dz = _ln_bwd(dy, xhat, rstd, g)
        return [dz * (2.0 * (1.0 + gt)), ALPHA * dz, _colsum(dz * f_acc), _colsum(dy * xhat), _colsum(dy), loss_t]

    df2x, dx1p, dgt2, dg2, db2, loss_part = _mm_rows(
        relu_up, Wdown, mode="nn", tm=trow, seq=S, a_fn=square,
        ins=[("tile", x1), ("tile", tgt2d), ("brow", gt2), ("row", ln2_g), ("row", ln2_b)],
        outs=[("tile", BF16, D), ("tile", F32, D), ("acc_brow", D), ("acc_row", D), ("acc_row", D), ("acc_scalar",)],
        epilogue=ep_ln2, name="mm_down_ln2")
    loss = lax.psum(loss_part[0, 0], ("x", "y", "c"))

    def send_grads(parts, name):
        snd, rcv, src, land, tok = _xstart(parts, False, None, name + "_start")
        return [(src[i], land[i], snd[i], rcv[i]) for i in range(len(parts))], tok

    dup = _mm(df2x, Wdown, mode="nt", tm=2048, tn=1024, tk=D, outs=[BF16], extras=[(relu_up, "tile")],
              epilogue=lambda acc, ex: (acc * ex[0].astype(F32),), name="mm_dup")
    g_wdown = _mm(relu_up, df2x, mode="tn", tm=1024, tn=D, tk=4096, outs=[BF16], a_fn=square,
                  epilogue=lambda acc, ex: (0.5 * acc,), name="mm_gwdown")
    (x_wdown,), tok = send_grads([g_wdown.reshape(N_DEV, dff // N_DEV, D)], "gx_wdown")
    def ep_ln1_bwd(dh2, v):
        dx1p_, x1_, xh_, rs_, mix_, sc, gt, g = v
        mixv = mix_.astype(F32)
        dx1 = dx1p_ + dh2 * (1.0 + sc)
        xhat, rstd = xh_.astype(F32), rs_[:, 0:1]
        dz = _ln_bwd(dx1, xhat, rstd, g)
        return [ALPHA * dz, dz * (1.0 + gt), _colsum(dh2 * x1_), _colsum(dh2), _colsum(dz * mixv),
                _colsum(dx1 * xhat), _colsum(dx1)]

    dxp, dmix, dsc2, dsh2, dgt1, dg1, db1 = _mm_rows(
        dup, Wup, mode="nt", tm=trow, seq=S, tok=tok,
        ins=[("tile", dx1p), ("tile", x1), ("tile", xhat1), ("tile", rstd1), ("tile", mix), ("brow", sc2), ("brow", gt1),
             ("row", ln1_g)],
        outs=[("tile", F32, D), ("tile", BF16, D), ("acc_brow", D), ("acc_brow", D), ("acc_brow", D), ("acc_row", D),
              ("acc_row", D)],
        epilogue=ep_ln1_bwd, name="mm_dh2_ln1b")
    g_wup = _mm(h2, dup, mode="tn", tm=D, tn=1024, tk=4096, outs=[BF16], nb=dff // N_DEV, name="mm_gwup")
    (x_wup,), tok = send_grads([g_wup], "gx_wup")

    def ep_merge_bwd(dm, v):
        ya, yb, ga, gb = [t.astype(F32) for t in v]
        sa, sb = _sigmoid(ga), _sigmoid(gb)
        dga, dgb = dm * ya * sa * (1.0 - sa), dm * yb * sb * (1.0 - sb)
        return [dm * sa, dm * sb, (dga, dgb), jnp.concatenate([_colsum(dga), _colsum(dgb)], axis=1)]

    dy_a, dy_b, dproj, dbin_hi = _mm_rows(
        dmix, Wout, mode="nt", tm=trow, seq=S, tok=tok,
        ins=[("tile", y_a), ("tile", y_b), ("tilecol", proj, D, gate_cb), ("tilecol", proj, D, gate_cb + 1)],
        outs=[("tile", BF16, D), ("tile", BF16, D), ("tilecol", BF16, 2 * D, gate_cb // 2, din), ("acc_row", 2 * D)],
        epilogue=ep_merge_bwd, name="mm_dmerged_mb")
    g_wout = _mm(merged, dmix, mode="tn", tm=D, tn=D, tk=4096, outs=[BF16], name="mm_gwout")
    (x_wout,), tok = send_grads([g_wout.reshape(N_DEV, D // N_DEV, D)], "gx_wout")
    dya_pre = _mm(dy_a, Wol, mode="nt", tm=2048, tn=lw, tk=D, outs=[BF16], tok=tok, name="mm_dya")
    dysgu = _mm(dy_b, Wos, mode="nt", tm=2048, tn=sw, tk=D, outs=[BF16], name="mm_dys")
    g_wol = _mm(ya_pre.reshape(T, lw), dy_a, mode="tn", tm=lw, tn=D, tk=2048, outs=[BF16], name="mm_gwol")
    g_wos = _mm(ysgu.reshape(T, sw), dy_b, mode="tn", tm=sw, tn=D, tk=2048, outs=[BF16], nb=D // N_DEV,
                name="mm_gwos")
    (x_wol, x_wos), tok = send_grads([g_wol.reshape(N_DEV, lw // N_DEV, D), g_wos], "gx_wo")
    small_mix_b = (wconv_full, b_conv + tok[0, 0]) + small_mix[2:]
    (dproj, dbin_lo, g_wconv, g_bconv, g_wa, g_ba, g_wx, g_bx, g_lam, g_wsp, g_bsp_t, g_lvg, g_lvb) = _mix_bwd(
        proj3, hs, dya_pre.reshape(Bl, S, lw), dysgu.reshape(Bl, S, sw), dproj.reshape(Bl, S, din), lru_saved,
        *small_mix_b, tm=tmix, lw=lw, sw=sw)
    dproj2 = dproj.reshape(T, din)
    small_names = [n for n in SMALL_NAMES if n != "b_ada"]
    small_g = dict(b_in=jnp.concatenate([dbin_lo, dbin_hi], axis=-1), b_conv=g_bconv, w_rg_a=g_wa[None], b_rg_a=g_ba,
                   w_rg_x=g_wx[None], b_rg_x=g_bx, lru_lambda=g_lam, w_sp=g_wsp[None],
                   b_sp=jnp.transpose(g_bsp_t)[None], ln_v_g=g_lvg, ln_v_b=g_lvb, ln1_g=dg1, ln1_b=db1, ln2_g=dg2,
                   ln2_b=db2)
    gs_snd, gs_rcv, gs_src, gs_land, tok_s = _xstart([small_g[n] for n in small_names], True, None, "gsmall_start")
    g_win = _mm(h.reshape(T, D), dproj2, mode="tn", tm=D, tn=din // 4, tk=2048, outs=[BF16], nb=din // N_DEV,
                tok=tok_s, name="mm_gwin")
    (x_win,), tok = send_grads([g_win], "gx_win")

    def ep_final(dh, v):
        dxp_, x_, sc = v
        return [dxp_ + dh * (1.0 + sc), _colsum(dh * x_), _colsum(dh)]

    grad_x, dsc1, dsh1 = _mm_rows(dproj2, win_parts, mode="nt", tm=trow, seq=S, tok=tok,
                                  ins=[("tile", dxp), ("tile", x2d), ("brow", sc1)],
                                  outs=[("tile", F32, D), ("acc_brow", D), ("acc_brow", D)], epilogue=ep_final,
                                  name="mm_dh_final")
    grad_x = grad_x.reshape(Bl, S, D)

    out_g, out_d, out_m, out_v = {}, {}, {}, {}

    def adam(name, g_slots, tr, own=None):
        shp = W[name].shape
        w2, m2, v2 = [t.reshape(g_slots.shape[1:]) for t in (W[name], Mo[name], Vo[name])]
        g, d, mn, vn = _adamw(w2, g_slots, m2, v2, tr=tr, name="adam_" + name, own=own)
        out_g[name], out_d[name], out_m[name], out_v[name] = [t.reshape(shp) for t in (g, d, mn, vn)]

    def adam_exchanged(name, handle, tr, after):
        own, slots = _xwait(*handle, after, False, "gx_%s_wait" % name, place=False)
        adam(name, slots, tr, own=own)

    adam_exchanged("w_down", x_wdown, 256, dsh1)
    adam_exchanged("w_up", x_wup, 256, dsh1)
    adam_exchanged("w_out", x_wout, 128, dsh1)
    adam_exchanged("w_o_lru", x_wol, 160, dsh1)
    adam_exchanged("w_o_sgu", x_wos, 256, dsh1)
    gs_own, gs_slots = _xwait_many(gs_src, gs_land, gs_snd, gs_rcv, dsh1, "gsmall_wait")
    res_small = _adamw_many([W[n] for n in small_names], gs_slots, gs_own, [Mo[n] for n in small_names],
                            [Vo[n] for n in small_names], name="adam_small")
    for dst, vals in zip((out_g, out_d, out_m, out_v), res_small):
        dst.update(dict(zip(small_names, vals)))

    dmod = jnp.concatenate([dsh1, dsc1, dgt1, dsh2, dsc2, dgt2], axis=-1).reshape(Bl, 6 * D)
    dmod_b = _blocked_cols(jnp.pad(dmod, ((0, SUBLANES - Bl), (0, 0))))
    dmod_s, gwconv_s = _exchange([dmod_b, _blocked_cols(g_wconv)], False, "xchg_dmod", after=out_g["ln2_b"])
    g_wada, g_bada_mine = _ada_bwd(c_act, dmod_s.reshape(N_DEV * SUBLANES, -1))
    (g_bada_all,) = _exchange([g_bada_mine], True, "xchg_bada")
    adam("w_ada", g_wada[None], 256)
    adam("b_ada", g_bada_all.reshape(1, 1, 6 * D), 1)
    adam("w_conv", gwconv_s, 8)
    adam_exchanged("w_in", x_win, 256, g_bada_all)

    return (loss, grad_x, *[out_g[n] for n in WEIGHT_ORDER], *[out_d[n] for n in WEIGHT_ORDER],
            *[out_m[n] for n in WEIGHT_ORDER], *[out_v[n] for n in WEIGHT_ORDER])
```

```python
import math

import jax
import jax.numpy as jnp
from jax import lax
from jax.experimental import pallas as pl
from jax.experimental.pallas import tpu as pltpu

N_DEV = 8
LN_EPS = 1e-5
LRU_C = 8.0
CHUNK = 64
SGU_BLOCK = 128
ALPHA = 2.0 ** 0.25
ADAM_LR = 0.001
ADAM_B1 = 0.9
ADAM_B2 = 0.999
ADAM_EPS = 1e-08
ADAM_WD = 0.01
ADAM_STEP = 10
GELU_K0 = math.sqrt(2.0 / math.pi)
GELU_K1 = 0.044715

SUBLANES = 8
LANES = 128
VMEM_LIMIT = 56 * 1024 * 1024
WIN_PARTS = 3

F32 = jnp.float32
BF16 = jnp.bfloat16
MESH = pl.DeviceIdType.MESH


def _cparams(n_axes, big=False):
    return pltpu.CompilerParams(dimension_semantics=("arbitrary",) * n_axes,
                                vmem_limit_bytes=VMEM_LIMIT if big else None)


def _sigmoid(x):
    return 0.5 * jnp.tanh(0.5 * x) + 0.5


def _gelu(x):
    t = jnp.tanh(x * (GELU_K0 + (GELU_K0 * GELU_K1) * (x * x)))
    hx = 0.5 * x
    return hx + hx * t


def _gelu_and_grad(x):
    x2 = x * x
    t = jnp.tanh(x * (GELU_K0 + (GELU_K0 * GELU_K1) * x2))
    hx = 0.5 * x
    g = hx + hx * t
    dg = (0.5 + 0.5 * t) + (hx * (1.0 - t * t)) * (GELU_K0 + (3.0 * GELU_K0 * GELU_K1) * x2)
    return g, dg


def _log1p_pos(e):
    p = e * (1.0 - e * (1.0 / 2.0) + e * e * (1.0 / 3.0) - e * e * e * (1.0 / 4.0))
    return jnp.where(e < 1e-2, p, jnp.log(1.0 + e))


def _ln_stats(z):
    mu = jnp.mean(z, axis=-1, keepdims=True)
    zc = z - mu
    var = jnp.mean(zc * zc, axis=-1, keepdims=True)
    rstd = lax.rsqrt(var + LN_EPS)
    return zc * rstd, rstd


def _ln_bwd(dy, xhat, rstd, g):
    dxh = dy * g
    m1 = jnp.mean(dxh, axis=-1, keepdims=True)
    m2 = jnp.mean(dxh * xhat, axis=-1, keepdims=True)
    return rstd * (dxh - m1 - xhat * m2)


def _colsum(v):
    return jnp.sum(v, axis=0, keepdims=True)


def _fold8(v):
    out = v[0:SUBLANES]
    for i in range(1, v.shape[0] // SUBLANES):
        out = out + v[i * SUBLANES:(i + 1) * SUBLANES]
    return out


def _first_step():
    return jnp.logical_and(pl.program_id(0) == 0, pl.program_id(1) == 0)


def _exchange(arrs, gather, name, after=None):
    n = len(arrs)
    n_peer = N_DEV - 1
    n_after = 0 if after is None else 1

    def body(*refs):
        ins, outs = refs[:n], refs[n + n_after:2 * n + n_after]
        send_sems, recv_sems, loc_sems = refs[2 * n + n_after:]
        x, y, c = lax.axis_index("x"), lax.axis_index("y"), lax.axis_index("c")
        me = 4 * x + 2 * y + c
        started = []
        for a in range(n):
            src_me = ins[a] if gather else ins[a].at[me]
            lc = pltpu.make_async_copy(src_me, outs[a].at[me], loc_sems.at[a])
            lc.start()
            started.append((lc, None))
        for p in range(1, N_DEV):
            px, py, pc = x ^ ((p >> 2) & 1), y ^ ((p >> 1) & 1), c ^ (p & 1)
            peer = 4 * px + 2 * py + pc
            for a in range(n):
                k = a * n_peer + (p - 1)
                src = ins[a] if gather else ins[a].at[peer]
                cp = pltpu.make_async_remote_copy(src_ref=src, dst_ref=outs[a].at[me],
                                                  send_sem=send_sems.at[k], recv_sem=recv_sems.at[k],
                                                  device_id=(px, py, pc), device_id_type=MESH)
                cp.start()
                rc = pltpu.make_async_remote_copy(src_ref=src, dst_ref=outs[a].at[peer],
                                                  send_sem=send_sems.at[k], recv_sem=recv_sems.at[k],
                                                  device_id=(px, py, pc), device_id_type=MESH)
                started.append((cp, rc))
        for cp, rc in started:
            if rc is None:
                cp.wait()
            else:
                cp.wait_send()
                rc.wait_recv()

    hbm = pl.BlockSpec(memory_space=pltpu.HBM)
    out_shape = tuple(
        jax.ShapeDtypeStruct(((N_DEV,) + a.shape) if gather else a.shape, a.dtype) for a in arrs)
    return pl.pallas_call(
        body, name=name, out_shape=out_shape,
        in_specs=[hbm] * n + [pl.BlockSpec(memory_space=pl.ANY)] * n_after, out_specs=tuple([hbm] * n),
        scratch_shapes=[pltpu.SemaphoreType.DMA((n * n_peer,)), pltpu.SemaphoreType.DMA((n * n_peer,)),
                        pltpu.SemaphoreType.DMA((n,))],
        compiler_params=pltpu.CompilerParams(has_side_effects=True),
    )(*arrs, *([after] if n_after else []))


_HBM = pl.BlockSpec(memory_space=pltpu.HBM)
_SEM = pl.BlockSpec(memory_space=pltpu.SEMAPHORE)
_EFFECT = pltpu.SideEffectType.DATAFLOW_SIDE_EFFECTING


def _peer_of(p):
    x, y, c = lax.axis_index("x"), lax.axis_index("y"), lax.axis_index("c")
    px, py, pc = x ^ ((p >> 2) & 1), y ^ ((p >> 1) & 1), c ^ (p & 1)
    return (px, py, pc), 4 * px + 2 * py + pc


def _slot(land_ref, idx, width):
    if width is None:
        return land_ref.at[idx]
    return land_ref.at[:, pl.ds(pl.multiple_of(idx * width, LANES), width)]


def _xstart(srcs, gather, after, name, cols=None):
    n = len(srcs)
    cols = cols or [False] * n
    widths = [t.shape[1] if cols[a] else None for a, t in enumerate(srcs)]
    lands = [lax.empty((t.shape[0], N_DEV * t.shape[1]) if cols[a] else (((N_DEV,) + t.shape) if gather else t.shape),
                       t.dtype) for a, t in enumerate(srcs)]
    n_after = 0 if after is None else 1

    def body(*refs):
        src_refs, land_refs = refs[:n], refs[n:2 * n]
        refs = refs[n_after:]
        send_sems, recv_sems = refs[2 * n:3 * n], refs[3 * n:4 * n]
        token = refs[6 * n]
        me = 4 * lax.axis_index("x") + 2 * lax.axis_index("y") + lax.axis_index("c")
        for a in range(n):
            for p in range(1, N_DEV):
                dev, peer = _peer_of(p)
                pltpu.make_async_remote_copy(
                    src_ref=src_refs[a] if gather else src_refs[a].at[peer], dst_ref=_slot(land_refs[a], me, widths[a]),
                    send_sem=send_sems[a].at[p - 1], recv_sem=recv_sems[a].at[p - 1],
                    device_id=dev, device_id_type=MESH).start()
        token[...] = jnp.zeros_like(token)

    sems = tuple(pltpu.SemaphoreType.DMA((N_DEV - 1,)) for _ in range(2 * n))
    thru = tuple(pltpu.HBM(t.shape, t.dtype) for t in list(srcs) + list(lands))
    res = pl.pallas_call(
        body, name=name,
        out_shape=sems + thru + (jax.ShapeDtypeStruct((SUBLANES, LANES), F32),),
        in_specs=[_HBM] * (2 * n) + [pl.BlockSpec(memory_space=pl.ANY)] * n_after,
        out_specs=tuple([_SEM] * (2 * n) + [_HBM] * (2 * n) + [pl.BlockSpec(memory_space=pltpu.VMEM)]),
        input_output_aliases={i: 2 * n + i for i in range(2 * n)},
        compiler_params=pltpu.CompilerParams(has_side_effects=_EFFECT),
    )(*[pltpu.with_memory_space_constraint(t, pltpu.HBM) for t in list(srcs) + list(lands)],
      *([after] if n_after else []))
    return res[:n], res[n:2 * n], res[2 * n:3 * n], res[3 * n:4 * n], res[4 * n]


def _xwait(src, land, send_sem, recv_sem, after, gather, name, col=False, place=True):
    width = src.shape[1] if col else None

    def body(src_ref, land_ref, send_ref, recv_ref, after_ref, src_dead, land_out):
        del after_ref, src_dead, land_out
        for p in range(1, N_DEV):
            dev, peer = _peer_of(p)
            cp = pltpu.make_async_remote_copy(
                src_ref=src_ref if gather else src_ref.at[peer], dst_ref=_slot(land_ref, peer, width),
                send_sem=send_ref.at[p - 1], recv_sem=recv_ref.at[p - 1], device_id=dev, device_id_type=MESH)
            cp.wait_send()
            cp.wait_recv()

    src_done, landed = pl.pallas_call(
        body, name=name, out_shape=(pltpu.HBM(src.shape, src.dtype), pltpu.HBM(land.shape, land.dtype)),
        in_specs=[_HBM, _HBM, _SEM, _SEM, pl.BlockSpec(memory_space=pl.ANY)], out_specs=(_HBM, _HBM),
        input_output_aliases={0: 0, 1: 1},
        compiler_params=pltpu.CompilerParams(has_side_effects=_EFFECT),
    )(src, land, send_sem, recv_sem, after)
    if not place:
        return src_done, landed
    me = 4 * lax.axis_index("x") + 2 * lax.axis_index("y") + lax.axis_index("c")
    return _place_own(landed, src_done, me, col, gather, name + "_own")


def _place_own(zone, src, me, col, gather, name):
    if col:
        R, C = src.shape
        src_spec = lambda tr: pl.BlockSpec((tr, C), lambda i, me_ref: (i, 0))
        out_spec = lambda tr: pl.BlockSpec((tr, C), lambda i, me_ref: (i, me_ref[0]))
    else:
        R, C = zone.shape[1:]
        src_spec = ((lambda tr: pl.BlockSpec((tr, C), lambda i, me_ref: (i, 0))) if gather else
                    (lambda tr: pl.BlockSpec((None, tr, C), lambda i, me_ref: (me_ref[0], i, 0))))
        out_spec = lambda tr: pl.BlockSpec((None, tr, C), lambda i, me_ref: (me_ref[0], i, 0))
    tr = R if R <= 512 else 256
    assert R % tr == 0, (name, R, tr)

    def body(me_ref, src_ref, zone_ref, out_ref):
        del me_ref, zone_ref
        out_ref[...] = src_ref[...]

    return pl.pallas_call(
        body, name=name, out_shape=jax.ShapeDtypeStruct(zone.shape, zone.dtype),
        grid_spec=pltpu.PrefetchScalarGridSpec(
            num_scalar_prefetch=1, grid=(R // tr,),
            in_specs=[src_spec(tr), pl.BlockSpec(memory_space=pl.ANY)], out_specs=out_spec(tr)),
        input_output_aliases={2: 0},
    )(jnp.reshape(me, (1,)).astype(jnp.int32), src, zone)


def _xwait_many(srcs, lands, send_sems, recv_sems, after, name):
    n = len(srcs)

    def body(*refs):
        src_refs, land_refs = refs[:n], refs[n:2 * n]
        snd, rcv = refs[2 * n:3 * n], refs[3 * n:4 * n]
        for a in range(n):
            for p in range(1, N_DEV):
                dev, peer = _peer_of(p)
                cp = pltpu.make_async_remote_copy(
                    src_ref=src_refs[a], dst_ref=land_refs[a].at[peer], send_sem=snd[a].at[p - 1],
                    recv_sem=rcv[a].at[p - 1], device_id=dev, device_id_type=MESH)
                cp.wait_send()
                cp.wait_recv()

    res = pl.pallas_call(
        body, name=name, out_shape=tuple(pltpu.HBM(t.shape, t.dtype) for t in list(srcs) + list(lands)),
        in_specs=[_HBM] * (2 * n) + [_SEM] * (2 * n) + [pl.BlockSpec(memory_space=pl.ANY)],
        out_specs=tuple([_HBM] * (2 * n)), input_output_aliases={i: i for i in range(2 * n)},
        compiler_params=pltpu.CompilerParams(has_side_effects=_EFFECT),
    )(*srcs, *lands, *send_sems, *recv_sems, after)
    return res[:n], res[n:]


def _mm(a, b, *, mode, tm, tn, tk, outs, epilogue=None, extras=(), nb=None, tok=None, scatter=None, into=None,
        a_fn=None, name):
    if mode == "nn":
        (M, K), (_, N) = a.shape, b.shape
    elif mode == "nt":
        (M, K), (N, _) = a.shape, b.shape
    else:
        (K, M), (_, N) = a.shape, b.shape
    tm, tn, tk = min(tm, M), min(tn, N), min(tk, K)
    assert M % tm == 0 and N % tn == 0 and K % tk == 0, (name, M, N, K, tm, tn, tk)
    if mode == "nn":
        a_spec = pl.BlockSpec((tm, tk), lambda i, j, k: (i, k))
        b_spec = pl.BlockSpec((tk, tn), lambda i, j, k: (k, j))
        dims = (((1,), (0,)), ((), ()))
    elif mode == "nt":
        a_spec = pl.BlockSpec((tm, tk), lambda i, j, k: (i, k))
        b_spec = pl.BlockSpec((tn, tk), lambda i, j, k: (j, k))
        dims = (((1,), (1,)), ((), ()))
    else:
        a_spec = pl.BlockSpec((tk, tm), lambda i, j, k: (k, i))
        b_spec = pl.BlockSpec((tk, tn), lambda i, j, k: (k, j))
        dims = (((0,), (0,)), ((), ()))
    nk = K // tk
    n_ex, n_out = len(extras), len(outs)
    n_tok = 0 if tok is None else 1
    nbytes = lambda d: jnp.dtype(d).itemsize
    vmem_est = (2 * (tm * tk * nbytes(a.dtype) + tk * tn * nbytes(b.dtype)
                     + sum(tm * tn * nbytes(e.dtype) for e, kind in extras if kind == "tile")
                     + sum(tm * tn * nbytes(d) for d in outs)) + tm * tn * 4)
    assert vmem_est <= VMEM_LIMIT, (name, vmem_est)
    if epilogue is None:
        epilogue = lambda acc, ex: tuple(acc.astype(d) for d in outs)

    n_into = 0 if into is None else 1

    def body(a_ref, b_ref, *refs):
        refs = refs[n_tok:]
        ex_refs, out_refs = refs[:n_ex], refs[n_ex + n_into:n_ex + n_into + n_out]

        def finish(acc):
            res = epilogue(acc, [r[...] for r in ex_refs])
            for o_ref, v in zip(out_refs, res):
                if nb is None:
                    o_ref[...] = v.astype(o_ref.dtype)
                else:
                    for q in range(tn // nb):
                        o_ref[q] = v[:, q * nb:(q + 1) * nb].astype(o_ref.dtype)

        a_tile = a_ref[...] if a_fn is None else a_fn(a_ref[...])
        part = lax.dot_general(a_tile, b_ref[...], dims, preferred_element_type=F32)
        if nk == 1:
            finish(part)
        else:
            acc_ref = refs[n_ex + n_into + n_out]
            k = pl.program_id(2)

            @pl.when(k == 0)
            def _():
                acc_ref[...] = part

            @pl.when(k > 0)
            def _():
                acc_ref[...] += part

            @pl.when(k == nk - 1)
            def _():
                finish(acc_ref[...])

    col = (lambda j: j) if scatter is None else (lambda j: scatter[0] * j + scatter[1])
    ex_specs = [pl.BlockSpec((tm, tn), lambda i, j, k: (i, j)) if kind == "tile"
                else pl.BlockSpec((1, tn), lambda i, j, k: (0, col(j))) for _, kind in extras]
    if nb is not None:
        assert tn % nb == 0, (name, tn, nb)
        o_spec = pl.BlockSpec((tn // nb, tm, nb), lambda i, j, k: (j, i, 0))
        o_shape = (N // nb, M, nb)
    else:
        o_spec = pl.BlockSpec((tm, tn), lambda i, j, k: (i, col(j)))
        o_shape = (M, N if scatter is None else scatter[2])
    assert n_into == 0 or n_out == 1
    res = pl.pallas_call(
        body, name=name, grid=(M // tm, N // tn, nk),
        in_specs=[a_spec, b_spec] + [pl.BlockSpec((SUBLANES, LANES), lambda i, j, k: (0, 0))] * n_tok + ex_specs
                 + [pl.BlockSpec(memory_space=pl.ANY)] * n_into,
        out_specs=tuple([o_spec] * n_out),
        out_shape=tuple(jax.ShapeDtypeStruct(o_shape, d) for d in outs),
        input_output_aliases={2 + n_tok + n_ex: 0} if n_into else {},
        scratch_shapes=[pltpu.VMEM((tm, tn), F32)] if nk > 1 else [],
        compiler_params=_cparams(3, big=True),
    )(a, b, *([tok] if n_tok else []), *[e for e, _ in extras], *([into] if n_into else []))
    return res[0] if n_out == 1 else res


def _mm_rows(a, b, *, mode, tm, seq, ins, outs, epilogue, tok=None, a_fn=None, name):
    M, K = a.shape
    b_parts = list(b) if isinstance(b, (list, tuple)) else [b]
    n_part = len(b_parts)
    assert n_part == 1 or mode == "nt"
    N = b_parts[0].shape[1] if mode == "nn" else b_parts[0].shape[0]
    tm = min(tm, M)
    assert M % tm == 0 and seq % tm == 0, (name, M, seq, tm)
    tpb = seq // tm
    n_b = M // seq
    dims = (((1,), (0,)), ((), ())) if mode == "nn" else (((1,), (1,)), ((), ()))
    n_tok = 0 if tok is None else 1
    n_in, n_out = len(ins), len(outs)

    in_specs, in_arrs = [], []
    for spec in ins:
        kind, arr = spec[0], spec[1]
        in_arrs.append(arr)
        if kind == "tile":
            in_specs.append(pl.BlockSpec((tm, arr.shape[1]), lambda i: (i, 0)))
        elif kind == "tilecol":
            in_specs.append(pl.BlockSpec((tm, spec[2]), lambda i, cb=spec[3]: (i, cb)))
        elif kind == "row":
            in_specs.append(pl.BlockSpec(arr.shape, lambda i: (0, 0)))
        else:
            in_specs.append(pl.BlockSpec((None, 1, arr.shape[2]), lambda i: (i // tpb, 0, 0)))
    out_specs, out_shapes = [], []
    for spec in outs:
        kind = spec[0]
        if kind == "tile":
            out_specs.append(pl.BlockSpec((tm, spec[2]), lambda i: (i, 0)))
            out_shapes.append(jax.ShapeDtypeStruct((M, spec[2]), spec[1]))
        elif kind == "tilecol":
            out_specs.append(pl.BlockSpec((tm, spec[2]), lambda i, cb=spec[3]: (i, cb)))
            out_shapes.append(jax.ShapeDtypeStruct((M, spec[4]), spec[1]))
        elif kind == "acc_row":
            out_specs.append(pl.BlockSpec((1, spec[1]), lambda i: (0, 0)))
            out_shapes.append(jax.ShapeDtypeStruct((1, spec[1]), F32))
        elif kind == "acc_brow":
            out_specs.append(pl.BlockSpec((None, 1, spec[1]), lambda i: (i // tpb, 0, 0)))
            out_shapes.append(jax.ShapeDtypeStruct((n_b, 1, spec[1]), F32))
        else:
            out_specs.append(pl.BlockSpec((SUBLANES, LANES), lambda i: (0, 0)))
            out_shapes.append(jax.ShapeDtypeStruct((SUBLANES, LANES), F32))

    def body(a_ref, *refs):
        b_refs, refs = refs[:n_part], refs[n_part + n_tok:]
        in_refs, out_refs = refs[:n_in], refs[n_in:n_in + n_out]
        i = pl.program_id(0)
        if n_part == 1:
            a_tile = a_ref[...] if a_fn is None else a_fn(a_ref[...])
            prod = lax.dot_general(a_tile, b_refs[0][...], dims, preferred_element_type=F32)
        else:
            w = b_parts[0].shape[1] // N_DEV
            prod = None
            for q in range(n_part):
                a_q = jnp.concatenate([a_ref[:, (n_part * j + q) * w:(n_part * j + q + 1) * w] for j in range(N_DEV)],
                                      axis=1)
                pq = lax.dot_general(a_q, b_refs[q][...], dims, preferred_element_type=F32)
                prod = pq if prod is None else prod + pq
        vals = epilogue(prod, [r[...] for r in in_refs])
        for spec, o_ref, v in zip(outs, out_refs, vals):
            kind = spec[0]
            if kind in ("tile", "tilecol"):
                off = 0
                for part in (v if isinstance(v, tuple) else (v,)):
                    o_ref[:, off:off + part.shape[1]] = part.astype(o_ref.dtype)
                    off += part.shape[1]
            else:
                first = (i % tpb == 0) if kind == "acc_brow" else (i == 0)

                @pl.when(first)
                def _(o_ref=o_ref, v=v):
                    o_ref[...] = jnp.broadcast_to(v, o_ref.shape)

                @pl.when(jnp.logical_not(first))
                def _(o_ref=o_ref, v=v):
                    o_ref[...] += v

    res = pl.pallas_call(
        body, name=name, grid=(M // tm,),
        in_specs=[pl.BlockSpec((tm, K), lambda i: (i, 0))]
                 + [pl.BlockSpec(bp.shape, lambda i: (0, 0), pipeline_mode=pl.Buffered(1)) for bp in b_parts]
                 + [pl.BlockSpec((SUBLANES, LANES), lambda i: (0, 0))] * n_tok + in_specs,
        out_specs=tuple(out_specs), out_shape=tuple(out_shapes),
        compiler_params=_cparams(1, big=True),
    )(a, *b_parts, *([tok] if n_tok else []), *in_arrs)
    return res


def _tok_spec(ts, width, col_block=0):
    return pl.BlockSpec((None, ts, width), lambda b, s: (b, s, col_block))


def _brow_spec(width):
    return pl.BlockSpec((None, 1, width), lambda b, s: (b, 0, 0))


def _modulate(x, sc, sh, ts):
    Bl, S, D = x.shape

    def body(x_ref, sc_ref, sh_ref, o_ref):
        o_ref[...] = (x_ref[...] * (1.0 + sc_ref[...]) + sh_ref[...]).astype(BF16)

    return pl.pallas_call(
        body, name="modulate", grid=(Bl, S // ts),
        in_specs=[_tok_spec(ts, D), _brow_spec(D), _brow_spec(D)],
        out_specs=_tok_spec(ts, D), out_shape=jax.ShapeDtypeStruct((Bl, S, D), BF16),
        compiler_params=_cparams(2),
    )(x, sc, sh)


def _mix_fwd(proj, w_conv, b_conv, w_rg_a, b_rg_a, w_rg_x, b_rg_x, lam, w_sp, b_sp_t, ln_v_g, ln_v_b, *, tm, lw, sw):
    Bl, S, _ = proj.shape
    heads, hd = w_rg_a.shape[0], w_rg_a.shape[1]
    groups = w_sp.shape[0]
    cw = 2 * lw + 2 * sw
    nblk = tm // SGU_BLOCK

    G = tm // SUBLANES
    nc = lw // LANES

    def body(p_ref, wc_ref, bc_ref, wa_ref, ba_ref, wx_ref, bx_ref, lam_ref, wsp_ref, bsp_ref, lg_ref, lb_ref,
             hs_ref, ya_ref, ys_ref, xc_ref, r_ref, ig_ref, a_ref, m_ref,
             xext, hnat, hcar, h7_scr, a7_scr, hp_scr, h0_scr, cp_scr):
        s = pl.program_id(1)

        @pl.when(s == 0)
        def _():
            xext[:, 0:SUBLANES, :] = jnp.zeros((nc, SUBLANES, LANES), F32)
            hcar[...] = jnp.zeros_like(hcar)

        @pl.when(s > 0)
        def _():
            xext[:, 0:SUBLANES, :] = xext[:, tm:tm + SUBLANES, :]

        nl = -lam_ref[...]
        big_l = -LRU_C * (jnp.maximum(nl, 0.0) + _log1p_pos(jnp.exp(-jnp.abs(nl))))

        for c in range(nc):
            cs = slice(c * LANES, (c + 1) * LANES)
            xext[c, SUBLANES:SUBLANES + tm, :] = p_ref[:, cs].astype(F32)
            xs = {st: xext[c, pl.ds(st, G, stride=SUBLANES), :] for st in range(SUBLANES - 3, 2 * SUBLANES)}
            wcs = [wc_ref[k:k + 1, cs] for k in range(4)]
            xc_j = []
            for j in range(SUBLANES):
                acc = bc_ref[:, cs] + xs[SUBLANES + j] * wcs[3]
                for k in (1, 2, 3):
                    acc = acc + xs[SUBLANES + j - k] * wcs[3 - k]
                xc_j.append(acc)
                xc_ref[j * G:(j + 1) * G, cs] = acc
            xcb = jnp.concatenate(xc_j, axis=0).astype(BF16)
            pa = jnp.dot(xcb, wa_ref[c], preferred_element_type=F32)
            px = jnp.dot(xcb, wx_ref[c], preferred_element_type=F32)
            h0 = cp = None
            for j in range(SUBLANES):
                rs = slice(j * G, (j + 1) * G)
                r = _sigmoid(pa[rs] + ba_ref[:, cs])
                ig = _sigmoid(px[rs] + bx_ref[:, cs])
                la = big_l[:, cs] * r
                a = jnp.exp(la)
                th = jnp.tanh(la)
                msq = (-2.0 * th) * pl.reciprocal(1.0 - th, approx=True)
                m = msq * lax.rsqrt(jnp.maximum(msq, 1e-30))
                b = m * (ig * xc_j[j])
                r_ref[rs, cs] = r
                ig_ref[rs, cs] = ig
                a_ref[rs, cs] = a
                m_ref[rs, cs] = m
                h0 = b if j == 0 else a * h0 + b
                cp = a if j == 0 else a * cp
                h0_scr[rs, cs] = h0
                cp_scr[rs, cs] = cp
            h7_scr[:, cs] = h0
            a7_scr[:, cs] = cp
        carry = hcar[0:1, :]
        for g in range(G):
            hp_scr[g:g + 1, :] = carry
            carry = h7_scr[g:g + 1, :] + a7_scr[g:g + 1, :] * carry
        hcar[0:1, :] = carry
        for c in range(nc):
            cs = slice(c * LANES, (c + 1) * LANES)
            hprev = hp_scr[:, cs]
            for j in range(SUBLANES):
                rs = slice(j * G, (j + 1) * G)
                hnat[c, pl.ds(j, G, stride=SUBLANES), :] = h0_scr[rs, cs] + cp_scr[rs, cs] * hprev
            hs = hnat[c]
            hs_ref[:, cs] = hs
            ya_ref[:, cs] = (hs * _gelu(p_ref[:, lw + c * LANES:lw + (c + 1) * LANES].astype(F32))).astype(BF16)

        gu = _gelu(p_ref[:, 2 * lw:2 * lw + sw].astype(F32))
        gv = _gelu(p_ref[:, 2 * lw + sw:cw].astype(F32))
        xhat, _ = _ln_stats(gv)
        vn = (xhat * lg_ref[...] + lb_ref[...]).astype(BF16)
        tpos = lax.broadcasted_iota(jnp.int32, (SGU_BLOCK, SGU_BLOCK), 0) // CHUNK
        spos = lax.broadcasted_iota(jnp.int32, (SGU_BLOCK, SGU_BLOCK), 1) // CHUNK
        gw = sw // groups
        rows_out = []
        for blk in range(nblk):
            r0 = blk * SGU_BLOCK
            cols = []
            for g in range(groups):
                wm = jnp.where(spos <= tpos, wsp_ref[g], 0.0).astype(BF16)
                mixed = jnp.dot(wm, vn[r0:r0 + SGU_BLOCK, g * gw:(g + 1) * gw], preferred_element_type=F32)
                cols.append(mixed + bsp_ref[:, g:g + 1])
            rows_out.append(jnp.concatenate(cols, axis=1))
        mixed_all = jnp.concatenate(rows_out, axis=0) if nblk > 1 else rows_out[0]
        ys_ref[...] = (gu * mixed_all).astype(BF16)

    full = lambda shp: pl.BlockSpec(shp, lambda b, s: (0,) * len(shp))
    return pl.pallas_call(
        body, name="mix_fwd", grid=(Bl, S // tm),
        in_specs=[_tok_spec(tm, cw), full(w_conv.shape), full(b_conv.shape), full(w_rg_a.shape), full(b_rg_a.shape),
                  full(w_rg_x.shape), full(b_rg_x.shape), full(lam.shape), full(w_sp.shape), full(b_sp_t.shape),
                  full(ln_v_g.shape), full(ln_v_b.shape)],
        out_specs=(_tok_spec(tm, lw), _tok_spec(tm, lw), _tok_spec(tm, sw)) + (_tok_spec(tm, lw),) * 5,
        out_shape=(jax.ShapeDtypeStruct((Bl, S, lw), F32), jax.ShapeDtypeStruct((Bl, S, lw), BF16),
                   jax.ShapeDtypeStruct((Bl, S, sw), BF16)) + (jax.ShapeDtypeStruct((Bl, S, lw), F32),) * 5,
        scratch_shapes=[pltpu.VMEM((nc, tm + SUBLANES, LANES), F32), pltpu.VMEM((nc, tm, LANES), F32),
                        pltpu.VMEM((SUBLANES, lw), F32), pltpu.VMEM((G, lw), F32), pltpu.VMEM((G, lw), F32),
                        pltpu.VMEM((G, lw), F32), pltpu.VMEM((tm, lw), F32), pltpu.VMEM((tm, lw), F32)],
        compiler_params=_cparams(2, big=True),
    )(proj, w_conv, b_conv, w_rg_a, b_rg_a, w_rg_x, b_rg_x, lam, w_sp, b_sp_t, ln_v_g, ln_v_b)


def _mix_bwd(proj, hs, dya, dys, dproj, saved, w_conv, b_conv, w_rg_a, b_rg_a, w_rg_x, b_rg_x, lam, w_sp, b_sp_t,
             ln_v_g, ln_v_b, *, tm, lw, sw):
    Bl, S, din = proj.shape
    heads, hd = w_rg_a.shape[0], w_rg_a.shape[1]
    groups = w_sp.shape[0]
    gw = sw // groups
    cw = 2 * lw + 2 * sw
    nblk = tm // SGU_BLOCK
    n_s = S // tm
    per8 = tm // SUBLANES
    halo_rows = 2 * SUBLANES

    G = tm // SUBLANES
    nc = lw // LANES

    def body(p_ref, xh_ref, hs_ref, hh_ref, dya_ref, dys_ref, dpin_ref, xc_ref, r_ref, ig_ref, a_ref, m_ref,
             wc_ref, bc_ref, wa_ref, ba_ref, wx_ref, bx_ref, lam_ref, wsp_ref, bsp_ref, lg_ref, lb_ref,
             dp_ref, dbin_ref, dwc_ref, dbc_ref, dwa_ref, dba_ref, dwx_ref, dbx_ref, dlam_ref, dwsp_ref, dbsp_ref,
             dlg_ref, dlb_ref,
             xext, hext, dnat, dxext, dhcar, g00_scr, p0_scr, a0_scr, cin_scr, g0_scr, pp_scr):
        del dpin_ref
        sr = pl.program_id(1)
        first_tile = sr == n_s - 1

        @pl.when(_first_step())
        def _():
            for ref in (dbin_ref, dwc_ref, dbc_ref, dwa_ref, dba_ref, dwx_ref, dbx_ref, dlam_ref, dwsp_ref, dbsp_ref,
                        dlg_ref, dlb_ref):
                ref[...] = jnp.zeros_like(ref)

        @pl.when(sr == 0)
        def _():
            dhcar[...] = jnp.zeros_like(dhcar)
            dxext[:, tm:tm + SUBLANES, :] = jnp.zeros((nc, SUBLANES, LANES), F32)

        @pl.when(sr > 0)
        def _():
            dxext[:, tm:tm + SUBLANES, :] = dxext[:, 0:SUBLANES, :]

        keep = jnp.where(first_tile, 0.0, 1.0)
        xprev = xh_ref[...].astype(F32)[halo_rows - SUBLANES:halo_rows] * keep
        hprev8 = hh_ref[...] * keep
        nl = -lam_ref[...]
        big_l = -LRU_C * (jnp.maximum(nl, 0.0) + _log1p_pos(jnp.exp(-jnp.abs(nl))))
        dlam_scale = LRU_C * _sigmoid(nl)
        nt = (((1,), (1,)), ((), ()))
        tn = (((0,), (0,)), ((), ()))
        last = SUBLANES - 1

        for c in range(nc):
            cs = slice(c * LANES, (c + 1) * LANES)
            gcs = slice(lw + c * LANES, lw + (c + 1) * LANES)
            xext[c, 0:SUBLANES, :] = xprev[:, cs]
            xext[c, SUBLANES:SUBLANES + tm, :] = p_ref[:, cs].astype(F32)
            hext[c, 0:SUBLANES, :] = hprev8[:, cs]
            dgl_sum = None
            for i in range(SUBLANES):
                rs = slice(i * G, (i + 1) * G)
                ggl, dggl = _gelu_and_grad(p_ref[rs, gcs].astype(F32))
                dy = dya_ref[rs, cs].astype(F32)
                hsv = hs_ref[rs, cs]
                hext[c, SUBLANES + i * G:SUBLANES + (i + 1) * G, :] = hsv
                dgl = dy * hsv * dggl
                dp_ref[rs, gcs] = dgl.astype(BF16)
                dnat[c, rs, :] = dy * ggl
                dgl_sum = _fold8(dgl) if i == 0 else dgl_sum + _fold8(dgl)
            dbin_ref[:, gcs] += _colsum(dgl_sum)
            g0 = pp = None
            for j in range(last, -1, -1):
                rs = slice(j * G, (j + 1) * G)
                dhs_j = dnat[c, pl.ds(j, G, stride=SUBLANES), :]
                if j == last:
                    g0 = dhs_j
                else:
                    an = a_ref[(j + 1) * G:(j + 2) * G, cs]
                    g0 = dhs_j + an * g0
                    pp = an if j == last - 1 else an * pp
                    pp_scr[rs, cs] = pp
                g0_scr[rs, cs] = g0
            g00_scr[:, cs] = g0
            p0_scr[:, cs] = pp
            a0_scr[:, cs] = a_ref[0:G, cs]
        cin = dhcar[0:1, :]
        for g in range(G - 1, -1, -1):
            cin_scr[g:g + 1, :] = cin
            cin = a0_scr[g:g + 1, :] * (g00_scr[g:g + 1, :] + p0_scr[g:g + 1, :] * cin)
        dhcar[0:1, :] = cin
        for c in range(nc):
            cs = slice(c * LANES, (c + 1) * LANES)
            cinv = cin_scr[:, cs]
            dpa_j, dpx_j, dxc_j = [], [], []
            dlam_sum = dba_sum = dbx_sum = None
            for j in range(SUBLANES):
                rs = slice(j * G, (j + 1) * G)
                dh = g0_scr[rs, cs] + (cinv if j == last else pp_scr[rs, cs] * cinv)
                hprev = hext[c, pl.ds(last + j, G, stride=SUBLANES), :]
                xc, r, ig, a, m = xc_ref[rs, cs], r_ref[rs, cs], ig_ref[rs, cs], a_ref[rs, cs], m_ref[rs, cs]
                dixc = dh * m
                dla = (dh * hprev) * a - (dh * (ig * xc)) * ((a * a) * pl.reciprocal(m, approx=True))
                dpa = (dla * big_l[:, cs]) * r * (1.0 - r)
                dpx = (dixc * xc) * ig * (1.0 - ig)
                dpa_j.append(dpa)
                dpx_j.append(dpx)
                dxc_j.append(dixc * ig)
                sums = (_fold8(dla * r), _fold8(dpa), _fold8(dpx))
                dlam_sum, dba_sum, dbx_sum = sums if j == 0 else (dlam_sum + sums[0], dba_sum + sums[1], dbx_sum + sums[2])
            dlam_ref[:, cs] += _colsum(dlam_sum) * dlam_scale[:, cs]
            dba_ref[:, cs] += _colsum(dba_sum)
            dbx_ref[:, cs] += _colsum(dbx_sum)
            dpab = jnp.concatenate(dpa_j, axis=0).astype(BF16)
            dpxb = jnp.concatenate(dpx_j, axis=0).astype(BF16)
            xcb = xc_ref[:, cs].astype(BF16)
            dxc = (jnp.concatenate(dxc_j, axis=0)
                   + lax.dot_general(dpab, wa_ref[c], nt, preferred_element_type=F32)
                   + lax.dot_general(dpxb, wx_ref[c], nt, preferred_element_type=F32))
            dwa_ref[c] += lax.dot_general(xcb, dpab, tn, preferred_element_type=F32)
            dwx_ref[c] += lax.dot_general(xcb, dpxb, tn, preferred_element_type=F32)

            dbc_ref[:, cs] += _colsum(dxc)
            xs = {st: xext[c, pl.ds(st, G, stride=SUBLANES), :] for st in range(SUBLANES - 3, 2 * SUBLANES)}
            for k in range(4):
                tot = None
                for j in range(SUBLANES):
                    part = _fold8(dxc[j * G:(j + 1) * G] * xs[SUBLANES + j - (3 - k)])
                    tot = part if tot is None else tot + part
                dwc_ref[k:k + 1, cs] += _colsum(tot)
            for j in range(SUBLANES):
                dxext[c, pl.ds(j, G, stride=SUBLANES), :] = dxc[j * G:(j + 1) * G]
            us = {st: dxext[c, pl.ds(st, G, stride=SUBLANES), :] for st in range(SUBLANES + 3)}
            wcs = [wc_ref[k:k + 1, cs] for k in range(4)]
            for j in range(SUBLANES):
                acc = us[j] * wcs[3]
                for k in (1, 2, 3):
                    acc = acc + us[j + k] * wcs[3 - k]
                dnat[c, pl.ds(j, G, stride=SUBLANES), :] = acc
            dxl = dnat[c]
            dp_ref[:, cs] = dxl.astype(BF16)
            dbin_ref[:, cs] += _colsum(dxl)

        gu, dgu_dx = _gelu_and_grad(p_ref[:, 2 * lw:2 * lw + sw].astype(F32))
        gv, dgv_dx = _gelu_and_grad(p_ref[:, 2 * lw + sw:cw].astype(F32))
        xhat, rstd = _ln_stats(gv)
        vn = (xhat * lg_ref[...] + lb_ref[...]).astype(BF16)
        dys = dys_ref[...].astype(F32)
        dmixed = dys * gu
        dmb = dmixed.astype(BF16)
        tpos = lax.broadcasted_iota(jnp.int32, (SGU_BLOCK, SGU_BLOCK), 0) // CHUNK
        spos = lax.broadcasted_iota(jnp.int32, (SGU_BLOCK, SGU_BLOCK), 1) // CHUNK
        causal = spos <= tpos
        mixed_rows, dvn_rows = [], []
        for blk in range(nblk):
            rs = slice(blk * SGU_BLOCK, (blk + 1) * SGU_BLOCK)
            mcols, dcols = [], []
            for g in range(groups):
                cs = slice(g * gw, (g + 1) * gw)
                wm = jnp.where(causal, wsp_ref[g], 0.0).astype(BF16)
                mcols.append(jnp.dot(wm, vn[rs, cs], preferred_element_type=F32) + bsp_ref[:, g:g + 1])
                dcols.append(lax.dot_general(wm, dmb[rs, cs], tn, preferred_element_type=F32))
                dw = lax.dot_general(dmb[rs, cs], vn[rs, cs], nt, preferred_element_type=F32)
                dwsp_ref[g] += jnp.where(causal, dw, 0.0)
                dbsp_ref[:, g:g + 1] += jnp.sum(dmixed[rs, cs], axis=1, keepdims=True)
            mixed_rows.append(jnp.concatenate(mcols, axis=1))
            dvn_rows.append(jnp.concatenate(dcols, axis=1))
        mixed_all = jnp.concatenate(mixed_rows, axis=0) if nblk > 1 else mixed_rows[0]
        dvn = jnp.concatenate(dvn_rows, axis=0) if nblk > 1 else dvn_rows[0]
        du = dys * mixed_all * dgu_dx
        dlg_ref[...] += _colsum(dvn * xhat)
        dlb_ref[...] += _colsum(dvn)
        dv = _ln_bwd(dvn, xhat, rstd, lg_ref[...]) * dgv_dx
        dp_ref[:, 2 * lw:2 * lw + sw] = du.astype(BF16)
        dp_ref[:, 2 * lw + sw:cw] = dv.astype(BF16)
        dbin_ref[:, 2 * lw:2 * lw + sw] += _colsum(du)
        dbin_ref[:, 2 * lw + sw:cw] += _colsum(dv)

    rev = lambda s: n_s - 1 - s
    tile = lambda w: pl.BlockSpec((None, tm, w), lambda b, s: (b, rev(s), 0))
    halo = lambda w: pl.BlockSpec((None, SUBLANES, w), lambda b, s: (b, jnp.maximum(rev(s) * per8 - 1, 0), 0))
    xhalo = pl.BlockSpec((None, halo_rows, lw), lambda b, s: (b, jnp.maximum(rev(s) * (tm // halo_rows) - 1, 0), 0))
    full = lambda shp: pl.BlockSpec(shp, lambda b, s: (0,) * len(shp))
    small = [w_conv, b_conv, w_rg_a, b_rg_a, w_rg_x, b_rg_x, lam, w_sp, b_sp_t, ln_v_g, ln_v_b]
    acc_shapes = [(1, cw), w_conv.shape, b_conv.shape, w_rg_a.shape, b_rg_a.shape, w_rg_x.shape, b_rg_x.shape,
                  lam.shape, w_sp.shape, b_sp_t.shape, ln_v_g.shape, ln_v_b.shape]
    res = pl.pallas_call(
        body, name="mix_bwd", grid=(Bl, n_s),
        in_specs=[tile(cw), xhalo, tile(lw), halo(lw), tile(lw), tile(sw), pl.BlockSpec(memory_space=pl.ANY)]
                 + [tile(lw)] * 5 + [full(w.shape) for w in small],
        out_specs=tuple([tile(cw)] + [full(shp) for shp in acc_shapes]),
        out_shape=tuple([jax.ShapeDtypeStruct((Bl, S, din), BF16)] + [jax.ShapeDtypeStruct(shp, F32) for shp in acc_shapes]),
        input_output_aliases={6: 0},
        scratch_shapes=[pltpu.VMEM((nc, tm + SUBLANES, LANES), F32), pltpu.VMEM((nc, tm + SUBLANES, LANES), F32),
                        pltpu.VMEM((nc, tm, LANES), F32), pltpu.VMEM((nc, tm + SUBLANES, LANES), F32),
                        pltpu.VMEM((SUBLANES, lw), F32), pltpu.VMEM((G, lw), F32), pltpu.VMEM((G, lw), F32),
                        pltpu.VMEM((G, lw), F32), pltpu.VMEM((G, lw), F32), pltpu.VMEM((tm, lw), F32),
                        pltpu.VMEM((tm, lw), F32)],
        compiler_params=_cparams(2, big=True),
    )(proj, proj, hs, hs, dya, dys, dproj, *saved, *small)
    return res


def _ada_fwd(c_all, w_ada):
    R, D = c_all.shape
    nb = w_ada.shape[1]

    def body(c_ref, w_ref, act_ref, o_ref):
        cv = c_ref[...]
        act = (cv * _sigmoid(cv)).astype(BF16)
        act_ref[...] = act
        o_ref[...] = jnp.dot(act, w_ref[...].astype(BF16), preferred_element_type=F32)

    return pl.pallas_call(
        body, name="ada_fwd",
        out_shape=(jax.ShapeDtypeStruct((R, D), BF16), jax.ShapeDtypeStruct((R, nb), F32)),
        compiler_params=pltpu.CompilerParams(vmem_limit_bytes=VMEM_LIMIT),
    )(c_all, w_ada)


def _ada_bwd(c_act, dmod_cols):
    R, D = c_act.shape
    nb = dmod_cols.shape[1]

    def body(act_ref, d_ref, o_ref, b_ref):
        o_ref[...] = lax.dot_general(act_ref[...], d_ref[...].astype(BF16), (((0,), (0,)), ((), ())),
                                     preferred_element_type=F32)
        b_ref[...] = _colsum(d_ref[...])

    return pl.pallas_call(
        body, name="ada_bwd", out_shape=(jax.ShapeDtypeStruct((D, nb), F32), jax.ShapeDtypeStruct((1, nb), F32)),
        compiler_params=pltpu.CompilerParams(vmem_limit_bytes=VMEM_LIMIT),
    )(c_act, dmod_cols)


def _adamw(w, g_slots, m, v, *, tr, name, own=None):
    R, C = w.shape
    n_slot = g_slots.shape[0]
    tr = min(tr, R)
    assert R % tr == 0, (name, R, tr)
    c1 = 1.0 / (1.0 - ADAM_B1 ** ADAM_STEP)
    c2 = 1.0 / (1.0 - ADAM_B2 ** ADAM_STEP)
    n_own = 0 if own is None else 1

    def body(me_ref, w_ref, g_ref, *refs):
        m_ref, v_ref, go_ref, d_ref, mo_ref, vo_ref = refs[n_own:]
        slot = lambda d: (jnp.where(me_ref[0] == d, refs[0][...], g_ref[d]) if n_own else g_ref[d]).astype(F32)
        g = slot(0)
        for d in range(1, n_slot):
            g = g + slot(d)
        mn = ADAM_B1 * m_ref[...] + (1.0 - ADAM_B1) * g
        vn = ADAM_B2 * v_ref[...] + (1.0 - ADAM_B2) * (g * g)
        go_ref[...] = g
        mo_ref[...] = mn
        vo_ref[...] = vn
        d_ref[...] = -ADAM_LR * ((mn * c1) / (jnp.sqrt(vn * c2) + ADAM_EPS) + ADAM_WD * w_ref[...])

    me = 4 * lax.axis_index("x") + 2 * lax.axis_index("y") + lax.axis_index("c")
    blk = pl.BlockSpec((tr, C), lambda i, me_ref: (i, 0))
    own_specs = [pl.BlockSpec((None, tr, C), lambda i, me_ref: (me_ref[0], i, 0))] * n_own
    return pl.pallas_call(
        body, name=name, out_shape=tuple(jax.ShapeDtypeStruct((R, C), F32) for _ in range(4)),
        grid_spec=pltpu.PrefetchScalarGridSpec(
            num_scalar_prefetch=1, grid=(R // tr,),
            in_specs=[blk, pl.BlockSpec((n_slot, tr, C), lambda i, me_ref: (0, i, 0))] + own_specs + [blk, blk],
            out_specs=(blk, blk, blk, blk)),
        compiler_params=_cparams(1, big=True),
    )(jnp.reshape(me, (1,)).astype(jnp.int32), w, g_slots, *([own] if n_own else []), m, v)


def _adamw_many(ws, g_slots, g_owns, ms, vs, *, name):
    n = len(ws)
    c1 = 1.0 / (1.0 - ADAM_B1 ** ADAM_STEP)
    c2 = 1.0 / (1.0 - ADAM_B2 ** ADAM_STEP)

    def body(*refs):
        w_refs, g_refs, o_refs = refs[:n], refs[n:2 * n], refs[2 * n:3 * n]
        m_refs, v_refs = refs[3 * n:4 * n], refs[4 * n:5 * n]
        outs = refs[5 * n:]
        me = 4 * lax.axis_index("x") + 2 * lax.axis_index("y") + lax.axis_index("c")
        for i in range(n):
            own = o_refs[i][...]
            g = jnp.where(me == 0, own, g_refs[i][0])
            for d in range(1, N_DEV):
                g = g + jnp.where(me == d, own, g_refs[i][d])
            mn = ADAM_B1 * m_refs[i][...] + (1.0 - ADAM_B1) * g
            vn = ADAM_B2 * v_refs[i][...] + (1.0 - ADAM_B2) * (g * g)
            outs[i][...] = g
            outs[n + i][...] = -ADAM_LR * ((mn * c1) / (jnp.sqrt(vn * c2) + ADAM_EPS) + ADAM_WD * w_refs[i][...])
            outs[2 * n + i][...] = mn
            outs[3 * n + i][...] = vn

    res = pl.pallas_call(
        body, name=name, out_shape=tuple(jax.ShapeDtypeStruct(w.shape, F32) for _ in range(4) for w in ws),
        compiler_params=pltpu.CompilerParams(vmem_limit_bytes=VMEM_LIMIT),
    )(*ws, *g_slots, *g_owns, *ms, *vs)
    return res[:n], res[n:2 * n], res[2 * n:3 * n], res[3 * n:]


SMALL_NAMES = ("b_ada", "b_in", "b_conv", "w_rg_a", "b_rg_a", "w_rg_x", "b_rg_x", "lru_lambda", "w_sp", "b_sp",
               "ln_v_g", "ln_v_b", "ln1_g", "ln1_b", "ln2_g", "ln2_b")
WEIGHT_ORDER = ("w_ada", "b_ada", "w_in", "b_in", "w_conv", "b_conv", "w_rg_a", "b_rg_a", "w_rg_x", "b_rg_x",
                "lru_lambda", "w_sp", "b_sp", "ln_v_g", "ln_v_b", "w_o_lru", "w_o_sgu", "w_out", "ln1_g", "ln1_b",
                "w_up", "w_down", "ln2_g", "ln2_b")


def _blocked_cols(w2d):
    K, N = w2d.shape
    return jnp.transpose(w2d.reshape(K, N_DEV, N // N_DEV), (1, 0, 2))


def _unblock_cols(wb):
    n, K, nb = wb.shape
    return jnp.transpose(wb, (1, 0, 2)).reshape(K, n * nb)


def kernel(x, c, w_ada, b_ada, w_in, b_in, w_conv, b_conv, w_rg_a, b_rg_a, w_rg_x, b_rg_x, lru_lambda, w_sp, b_sp, ln_v_g, ln_v_b, w_o_lru, w_o_sgu, w_out, ln1_g, ln1_b, w_up, w_down, ln2_g, ln2_b, loss_target, m_w_ada, m_b_ada, m_w_in, m_b_in, m_w_conv, m_b_conv, m_w_rg_a, m_b_rg_a, m_w_rg_x, m_b_rg_x, m_lru_lambda, m_w_sp, m_b_sp, m_ln_v_g, m_ln_v_b, m_w_o_lru, m_w_o_sgu, m_w_out, m_ln1_g, m_ln1_b, m_w_up, m_w_down, m_ln2_g, m_ln2_b, v_w_ada, v_b_ada, v_w_in, v_b_in, v_w_conv, v_b_conv, v_w_rg_a, v_b_rg_a, v_w_rg_x, v_b_rg_x, v_lru_lambda, v_w_sp, v_b_sp, v_ln_v_g, v_ln_v_b, v_w_o_lru, v_w_o_sgu, v_w_out, v_ln1_g, v_ln1_b, v_w_up, v_w_down, v_ln2_g, v_ln2_b):
    W = dict(w_ada=w_ada, b_ada=b_ada, w_in=w_in, b_in=b_in, w_conv=w_conv, b_conv=b_conv, w_rg_a=w_rg_a,
             b_rg_a=b_rg_a, w_rg_x=w_rg_x, b_rg_x=b_rg_x, lru_lambda=lru_lambda, w_sp=w_sp, b_sp=b_sp,
             ln_v_g=ln_v_g, ln_v_b=ln_v_b, w_o_lru=w_o_lru, w_o_sgu=w_o_sgu, w_out=w_out, ln1_g=ln1_g, ln1_b=ln1_b,
             w_up=w_up, w_down=w_down, ln2_g=ln2_g, ln2_b=ln2_b)
    Mo = dict(w_ada=m_w_ada, b_ada=m_b_ada, w_in=m_w_in, b_in=m_b_in, w_conv=m_w_conv, b_conv=m_b_conv,
              w_rg_a=m_w_rg_a, b_rg_a=m_b_rg_a, w_rg_x=m_w_rg_x, b_rg_x=m_b_rg_x, lru_lambda=m_lru_lambda,
              w_sp=m_w_sp, b_sp=m_b_sp, ln_v_g=m_ln_v_g, ln_v_b=m_ln_v_b, w_o_lru=m_w_o_lru, w_o_sgu=m_w_o_sgu,
              w_out=m_w_out, ln1_g=m_ln1_g, ln1_b=m_ln1_b, w_up=m_w_up, w_down=m_w_down, ln2_g=m_ln2_g,
              ln2_b=m_ln2_b)
    Vo = dict(w_ada=v_w_ada, b_ada=v_b_ada, w_in=v_w_in, b_in=v_b_in, w_conv=v_w_conv, b_conv=v_b_conv,
              w_rg_a=v_w_rg_a, b_rg_a=v_b_rg_a, w_rg_x=v_w_rg_x, b_rg_x=v_b_rg_x, lru_lambda=v_lru_lambda,
              w_sp=v_w_sp, b_sp=v_b_sp, ln_v_g=v_ln_v_g, ln_v_b=v_ln_v_b, w_o_lru=v_w_o_lru, w_o_sgu=v_w_o_sgu,
              w_out=v_w_out, ln1_g=v_ln1_g, ln1_b=v_ln1_b, w_up=v_w_up, w_down=v_w_down, ln2_g=v_ln2_g,
              ln2_b=v_ln2_b)

    Bl, S, D = x.shape
    T = Bl * S
    lw = b_conv.shape[-1]
    sw = ln_v_g.shape[-1]
    din = b_in.shape[-1]
    dff = w_up.shape[-1] * N_DEV
    ts = min(2048, S)
    tmix = min(256, S)
    trow = min(512, S)

    c_pad = jnp.pad(c, ((0, SUBLANES - Bl), (0, 0)))
    c_g, wconv_g = _exchange([c_pad, w_conv[0]], True, "xchg_c")
    wconv_full = _unblock_cols(wconv_g)
    c_act, modcols = _ada_fwd(c_g.reshape(N_DEV * SUBLANES, D), w_ada[0])
    (mod_slots,) = _exchange([modcols.reshape(N_DEV, SUBLANES, -1)], False, "xchg_mod")

    nbw = din // N_DEV // WIN_PARTS
    wnames = tuple("win%d" % q for q in range(WIN_PARTS)) + ("wol", "wos", "wout", "wup", "wdown")
    shards = [w_in[0][:, q * nbw:(q + 1) * nbw].astype(BF16) for q in range(WIN_PARTS)] + [
        w_o_lru[0].astype(BF16), w_o_sgu[0].astype(BF16), w_out[0].astype(BF16), w_up[0].astype(BF16),
        w_down[0].astype(BF16)]
    col_sharded = [True] * WIN_PARTS + [False, True, False, True, False]
    g_send, g_recv, g_src, g_land, g_tok = _xstart(shards, True, mod_slots, "gather_start", cols=col_sharded)
    gidx = {n: i for i, n in enumerate(wnames)}

    def gathered(n, after):
        i = gidx[n]
        return _xwait(g_src[i], g_land[i], g_send[i], g_recv[i], after, True, "gather_wait_" + n, col=col_sharded[i])

    mod = _unblock_cols(mod_slots)[:Bl] + (b_ada + g_tok[0, 0])
    sh1, sc1, gt1, sh2, sc2, gt2 = [mod[:, i * D:(i + 1) * D].reshape(Bl, 1, D) for i in range(6)]

    wa_b, wx_b = w_rg_a[0].astype(BF16), w_rg_x[0].astype(BF16)
    b_sp_t = jnp.transpose(b_sp[0])
    small_mix = (wconv_full, b_conv, wa_b, b_rg_a, wx_b, b_rg_x, lru_lambda, w_sp[0], b_sp_t, ln_v_g, ln_v_b)

    h = _modulate(x, sc1, sh1, ts)
    proj, win_parts = None, []
    for q in range(WIN_PARTS):
        wq = gathered("win%d" % q, h if q == 0 else proj)
        win_parts.append(wq)
        proj = _mm(h.reshape(T, D), wq, mode="nn", tm=8192, tn=nbw, tk=D, outs=[BF16], extras=[(b_in, "row")],
                   epilogue=lambda acc, ex: (acc + ex[0],), scatter=(WIN_PARTS, q, din), into=proj,
                   name="mm_proj%d" % q)
    proj3 = proj.reshape(Bl, S, din)
    hs, ya_pre, ysgu, *lru_saved = _mix_fwd(proj3, *small_mix, tm=tmix, lw=lw, sw=sw)
    Wol = gathered("wol", ya_pre).reshape(lw, D)
    Wos = gathered("wos", ysgu)
    y_a = _mm(ya_pre.reshape(T, lw), Wol, mode="nn", tm=2048, tn=D, tk=lw, outs=[BF16], name="mm_ya")
    x2d, tgt2d = x.reshape(T, D), loss_target.reshape(T, D)
    gate_cb = (din - 2 * D) // D

    def ep_merge(y_b, v):
        ya, ga, gb = [t.astype(F32) for t in v]
        yb = y_b.astype(BF16).astype(F32)
        return [yb, _sigmoid(ga) * ya + _sigmoid(gb) * yb]

    y_b, merged = _mm_rows(ysgu.reshape(T, sw), Wos, mode="nn", tm=trow, seq=S,
                           ins=[("tile", y_a), ("tilecol", proj, D, gate_cb), ("tilecol", proj, D, gate_cb + 1)],
                           outs=[("tile", BF16, D), ("tile", BF16, D)], epilogue=ep_merge, name="mm_yb_merge")
    Wout = gathered("wout", merged).reshape(D, D)

    def ep_ln1(mix_acc, v):
        x_, gt, g, b, sc, sh = v
        mixr = mix_acc.astype(BF16).astype(F32)
        xhat, rstd = _ln_stats(ALPHA * x_ + (1.0 + gt) * mixr)
        x1_ = xhat * g + b
        return [mixr, x1_, x1_ * (1.0 + sc) + sh, xhat, jnp.broadcast_to(rstd, (rstd.shape[0], LANES))]

    mix, x1, h2, xhat1, rstd1 = _mm_rows(
        merged, Wout, mode="nn", tm=trow, seq=S,
        ins=[("tile", x2d), ("brow", gt1), ("row", ln1_g), ("row", ln1_b), ("brow", sc2), ("brow", sh2)],
        outs=[("tile", BF16, D), ("tile", F32, D), ("tile", BF16, D), ("tile", BF16, D), ("tile", F32, LANES)],
        epilogue=ep_ln1, name="mm_mix_ln1")
    Wup = gathered("wup", h2)
    relu_up = _mm(h2, Wup, mode="nn", tm=2048, tn=1024, tk=D, outs=[BF16],
                  epilogue=lambda acc, ex: (jnp.maximum(acc, 0.0),), name="mm_up")
    square = lambda t: t * t
    Wdown = gathered("wdown", relu_up).reshape(dff, D)

    def ep_ln2(f_acc, v):
        x1_, t_, gt, g, b = v
        xhat, rstd = _ln_stats(ALPHA * x1_ + (1.0 + gt) * f_acc)
        err = xhat * g + b - t_
        loss_t = 0.5 * jnp.sum(jnp.mean(err * err, axis=-1, keepdims=True))
        dy = err * (1.0 / D)
        dz = _ln_bwd(dy, xhat, rstd, g)
        return [dz * (2.0 * (1.0 + gt)), ALPHA * dz, _colsum(dz * f_acc), _colsum(dy * xhat), _colsum(dy), loss_t]

    df2x, dx1p, dgt2, dg2, db2, loss_part = _mm_rows(
        relu_up, Wdown, mode="nn", tm=trow, seq=S, a_fn=square,
        ins=[("tile", x1), ("tile", tgt2d), ("brow", gt2), ("row", ln2_g), ("row", ln2_b)],
        outs=[("tile", BF16, D), ("tile", F32, D), ("acc_brow", D), ("acc_row", D), ("acc_row", D), ("acc_scalar",)],
        epilogue=ep_ln2, name="mm_down_ln2")
    loss = lax.psum(loss_part[0, 0], ("x", "y", "c"))

    def send_grads(parts, name):
        snd, rcv, src, land, tok = _xstart(parts, False, None, name + "_start")
        return [(src[i], land[i], snd[i], rcv[i]) for i in range(len(parts))], tok

    dup = _mm(df2x, Wdown, mode="nt", tm=2048, tn=1024, tk=D, outs=[BF16], extras=[(relu_up, "tile")],
              epilogue=lambda acc, ex: (acc * ex[0].astype(F32),), name="mm_dup")
    g_wdown = _mm(relu_up, df2x, mode="tn", tm=1024, tn=D, tk=4096, outs=[BF16], a_fn=square,
                  epilogue=lambda acc, ex: (0.5 * acc,), name="mm_gwdown")
    (x_wdown,), tok = send_grads([g_wdown.reshape(N_DEV, dff // N_DEV, D)], "gx_wdown")
    def ep_ln1_bwd(dh2, v):
        dx1p_, x1_, xh_, rs_, mix_, sc, gt, g = v
        mixv = mix_.astype(F32)
        dx1 = dx1p_ + dh2 * (1.0 + sc)
        xhat, rstd = xh_.astype(F32), rs_[:, 0:1]
        dz = _ln_bwd(dx1, xhat, rstd, g)
        return [ALPHA * dz, dz * (1.0 + gt), _colsum(dh2 * x1_), _colsum(dh2), _colsum(dz * mixv),
                _colsum(dx1 * xhat), _colsum(dx1)]

    dxp, dmix, dsc2, dsh2, dgt1, dg1, db1 = _mm_rows(
        dup, Wup, mode="nt", tm=trow, seq=S, tok=tok,
        ins=[("tile", dx1p), ("tile", x1), ("tile", xhat1), ("tile", rstd1), ("tile", mix), ("brow", sc2), ("brow", gt1),
             ("row", ln1_g)],
        outs=[("tile", F32, D), ("tile", BF16, D), ("acc_brow", D), ("acc_brow", D), ("acc_brow", D), ("acc_row", D),
              ("acc_row", D)],
        epilogue=ep_ln1_bwd, name="mm_dh2_ln1b")
    g_wup = _mm(h2, dup, mode="tn", tm=D, tn=1024, tk=4096, outs=[BF16], nb=dff // N_DEV, name="mm_gwup")
    (x_wup,), tok = send_grads([g_wup], "gx_wup")

    def ep_merge_bwd(dm, v):
        ya, yb, ga, gb = [t.astype(F32) for t in v]
        sa, sb = _sigmoid(ga), _sigmoid(gb)
        dga, dgb = dm * ya * sa * (1.0 - sa), dm * yb * sb * (1.0 - sb)
        return [dm * sa, dm * sb, (dga, dgb), jnp.concatenate([_colsum(dga), _colsum(dgb)], axis=1)]

    dy_a, dy_b, dproj, dbin_hi = _mm_rows(
        dmix, Wout, mode="nt", tm=trow, seq=S, tok=tok,
        ins=[("tile", y_a), ("tile", y_b), ("tilecol", proj, D, gate_cb), ("tilecol", proj, D, gate_cb + 1)],
        outs=[("tile", BF16, D), ("tile", BF16, D), ("tilecol", BF16, 2 * D, gate_cb // 2, din), ("acc_row", 2 * D)],
        epilogue=ep_merge_bwd, name="mm_dmerged_mb")
    g_wout = _mm(merged, dmix, mode="tn", tm=D, tn=D, tk=2048, outs=[BF16], name="mm_gwout")
    (x_wout,), tok = send_grads([g_wout.reshape(N_DEV, D // N_DEV, D)], "gx_wout")
    dya_pre = _mm(dy_a, Wol, mode="nt", tm=2048, tn=lw, tk=D, outs=[BF16], tok=tok, name="mm_dya")
    dysgu = _mm(dy_b, Wos, mode="nt", tm=2048, tn=sw, tk=D, outs=[BF16], name="mm_dys")
    g_wol = _mm(ya_pre.reshape(T, lw), dy_a, mode="tn", tm=lw, tn=D, tk=2048, outs=[BF16], name="mm_gwol")
    g_wos = _mm(ysgu.reshape(T, sw), dy_b, mode="tn", tm=sw, tn=D, tk=4096, outs=[BF16], nb=D // N_DEV,
                name="mm_gwos")
    (x_wol, x_wos), tok = send_grads([g_wol.reshape(N_DEV, lw // N_DEV, D), g_wos], "gx_wo")
    small_mix_b = (wconv_full, b_conv + tok[0, 0]) + small_mix[2:]
    (dproj, dbin_lo, g_wconv, g_bconv, g_wa, g_ba, g_wx, g_bx, g_lam, g_wsp, g_bsp_t, g_lvg, g_lvb) = _mix_bwd(
        proj3, hs, dya_pre.reshape(Bl, S, lw), dysgu.reshape(Bl, S, sw), dproj.reshape(Bl, S, din), lru_saved,
        *small_mix_b, tm=tmix, lw=lw, sw=sw)
    dproj2 = dproj.reshape(T, din)
    small_names = [n for n in SMALL_NAMES if n != "b_ada"]
    small_g = dict(b_in=jnp.concatenate([dbin_lo, dbin_hi], axis=-1), b_conv=g_bconv, w_rg_a=g_wa[None], b_rg_a=g_ba,
                   w_rg_x=g_wx[None], b_rg_x=g_bx, lru_lambda=g_lam, w_sp=g_wsp[None],
                   b_sp=jnp.transpose(g_bsp_t)[None], ln_v_g=g_lvg, ln_v_b=g_lvb, ln1_g=dg1, ln1_b=db1, ln2_g=dg2,
                   ln2_b=db2)
    gs_snd, gs_rcv, gs_src, gs_land, tok_s = _xstart([small_g[n] for n in small_names], True, None, "gsmall_start")
    g_win = _mm(h.reshape(T, D), dproj2, mode="tn", tm=D, tn=din // 4, tk=2048, outs=[BF16], nb=din // N_DEV,
                tok=tok_s, name="mm_gwin")
    (x_win,), tok = send_grads([g_win], "gx_win")

    def ep_final(dh, v):
        dxp_, x_, sc = v
        return [dxp_ + dh * (1.0 + sc), _colsum(dh * x_), _colsum(dh)]

    grad_x, dsc1, dsh1 = _mm_rows(dproj2, win_parts, mode="nt", tm=trow, seq=S, tok=tok,
                                  ins=[("tile", dxp), ("tile", x2d), ("brow", sc1)],
                                  outs=[("tile", F32, D), ("acc_brow", D), ("acc_brow", D)], epilogue=ep_final,
                                  name="mm_dh_final")
    grad_x = grad_x.reshape(Bl, S, D)

    out_g, out_d, out_m, out_v = {}, {}, {}, {}

    def adam(name, g_slots, tr, own=None):
        shp = W[name].shape
        w2, m2, v2 = [t.reshape(g_slots.shape[1:]) for t in (W[name], Mo[name], Vo[name])]
        g, d, mn, vn = _adamw(w2, g_slots, m2, v2, tr=tr, name="adam_" + name, own=own)
        out_g[name], out_d[name], out_m[name], out_v[name] = [t.reshape(shp) for t in (g, d, mn, vn)]

    def adam_exchanged(name, handle, tr, after):
        own, slots = _xwait(*handle, after, False, "gx_%s_wait" % name, place=False)
        adam(name, slots, tr, own=own)

    adam_exchanged("w_down", x_wdown, 256, dsh1)
    adam_exchanged("w_up", x_wup, 256, dsh1)
    adam_exchanged("w_out", x_wout, 128, dsh1)
    adam_exchanged("w_o_lru", x_wol, 160, dsh1)
    adam_exchanged("w_o_sgu", x_wos, 256, dsh1)
    gs_own, gs_slots = _xwait_many(gs_src, gs_land, gs_snd, gs_rcv, dsh1, "gsmall_wait")
    res_small = _adamw_many([W[n] for n in small_names], gs_slots, gs_own, [Mo[n] for n in small_names],
                            [Vo[n] for n in small_names], name="adam_small")
    for dst, vals in zip((out_g, out_d, out_m, out_v), res_small):
        dst.update(dict(zip(small_names, vals)))

    dmod = jnp.concatenate([dsh1, dsc1, dgt1, dsh2, dsc2, dgt2], axis=-1).reshape(Bl, 6 * D)
    dmod_b = _blocked_cols(jnp.pad(dmod, ((0, SUBLANES - Bl), (0, 0))))
    dmod_s, gwconv_s = _exchange([dmod_b, _blocked_cols(g_wconv)], False, "xchg_dmod", after=out_g["ln2_b"])
    g_wada, g_bada_mine = _ada_bwd(c_act, dmod_s.reshape(N_DEV * SUBLANES, -1))
    (g_bada_all,) = _exchange([g_bada_mine], True, "xchg_bada")
    adam("w_ada", g_wada[None], 256)
    adam("b_ada", g_bada_all.reshape(1, 1, 6 * D), 1)
    adam("w_conv", gwconv_s, 8)
    adam_exchanged("w_in", x_win, 256, g_bada_all)

    return (loss, grad_x, *[out_g[n] for n in WEIGHT_ORDER], *[out_d[n] for n in WEIGHT_ORDER],
            *[out_m[n] for n in WEIGHT_ORDER], *[out_v[n] for n in WEIGHT_ORDER])
```

```python
import math

import jax
import jax.numpy as jnp
from jax import lax
from jax.experimental import pallas as pl
from jax.experimental.pallas import tpu as pltpu

N_DEV = 8
LN_EPS = 1e-5
LRU_C = 8.0
CHUNK = 64
SGU_BLOCK = 128
ALPHA = 2.0 ** 0.25
ADAM_LR = 0.001
ADAM_B1 = 0.9
ADAM_B2 = 0.999
ADAM_EPS = 1e-08
ADAM_WD = 0.01
ADAM_STEP = 10
GELU_K0 = math.sqrt(2.0 / math.pi)
GELU_K1 = 0.044715

SUBLANES = 8
LANES = 128
VMEM_LIMIT = 56 * 1024 * 1024
WIN_PARTS = 3

F32 = jnp.float32
BF16 = jnp.bfloat16
MESH = pl.DeviceIdType.MESH


def _cparams(n_axes, big=False):
    return pltpu.CompilerParams(dimension_semantics=("arbitrary",) * n_axes,
                                vmem_limit_bytes=VMEM_LIMIT if big else None)


def _sigmoid(x):
    return 0.5 * jnp.tanh(0.5 * x) + 0.5


def _gelu(x):
    t = jnp.tanh(x * (GELU_K0 + (GELU_K0 * GELU_K1) * (x * x)))
    hx = 0.5 * x
    return hx + hx * t


def _gelu_and_grad(x):
    x2 = x * x
    t = jnp.tanh(x * (GELU_K0 + (GELU_K0 * GELU_K1) * x2))
    hx = 0.5 * x
    g = hx + hx * t
    dg = (0.5 + 0.5 * t) + (hx * (1.0 - t * t)) * (GELU_K0 + (3.0 * GELU_K0 * GELU_K1) * x2)
    return g, dg


def _log1p_pos(e):
    p = e * (1.0 - e * (1.0 / 2.0) + e * e * (1.0 / 3.0) - e * e * e * (1.0 / 4.0))
    return jnp.where(e < 1e-2, p, jnp.log(1.0 + e))


def _ln_stats(z):
    mu = jnp.mean(z, axis=-1, keepdims=True)
    zc = z - mu
    var = jnp.mean(zc * zc, axis=-1, keepdims=True)
    rstd = lax.rsqrt(var + LN_EPS)
    return zc * rstd, rstd


def _ln_bwd(dy, xhat, rstd, g):
    dxh = dy * g
    m1 = jnp.mean(dxh, axis=-1, keepdims=True)
    m2 = jnp.mean(dxh * xhat, axis=-1, keepdims=True)
    return rstd * (dxh - m1 - xhat * m2)


def _colsum(v):
    return jnp.sum(v, axis=0, keepdims=True)


def _fold8(v):
    out = v[0:SUBLANES]
    for i in range(1, v.shape[0] // SUBLANES):
        out = out + v[i * SUBLANES:(i + 1) * SUBLANES]
    return out


def _first_step():
    return jnp.logical_and(pl.program_id(0) == 0, pl.program_id(1) == 0)


def _exchange(arrs, gather, name, after=None):
    n = len(arrs)
    n_peer = N_DEV - 1
    n_after = 0 if after is None else 1

    def body(*refs):
        ins, outs = refs[:n], refs[n + n_after:2 * n + n_after]
        send_sems, recv_sems, loc_sems = refs[2 * n + n_after:]
        x, y, c = lax.axis_index("x"), lax.axis_index("y"), lax.axis_index("c")
        me = 4 * x + 2 * y + c
        started = []
        for a in range(n):
            src_me = ins[a] if gather else ins[a].at[me]
            lc = pltpu.make_async_copy(src_me, outs[a].at[me], loc_sems.at[a])
            lc.start()
            started.append((lc, None))
        for p in range(1, N_DEV):
            px, py, pc = x ^ ((p >> 2) & 1), y ^ ((p >> 1) & 1), c ^ (p & 1)
            peer = 4 * px + 2 * py + pc
            for a in range(n):
                k = a * n_peer + (p - 1)
                src = ins[a] if gather else ins[a].at[peer]
                cp = pltpu.make_async_remote_copy(src_ref=src, dst_ref=outs[a].at[me],
                                                  send_sem=send_sems.at[k], recv_sem=recv_sems.at[k],
                                                  device_id=(px, py, pc), device_id_type=MESH)
                cp.start()
                rc = pltpu.make_async_remote_copy(src_ref=src, dst_ref=outs[a].at[peer],
                                                  send_sem=send_sems.at[k], recv_sem=recv_sems.at[k],
                                                  device_id=(px, py, pc), device_id_type=MESH)
                started.append((cp, rc))
        for cp, rc in started:
            if rc is None:
                cp.wait()
            else:
                cp.wait_send()
                rc.wait_recv()

    hbm = pl.BlockSpec(memory_space=pltpu.HBM)
    out_shape = tuple(
        jax.ShapeDtypeStruct(((N_DEV,) + a.shape) if gather else a.shape, a.dtype) for a in arrs)
    return pl.pallas_call(
        body, name=name, out_shape=out_shape,
        in_specs=[hbm] * n + [pl.BlockSpec(memory_space=pl.ANY)] * n_after, out_specs=tuple([hbm] * n),
        scratch_shapes=[pltpu.SemaphoreType.DMA((n * n_peer,)), pltpu.SemaphoreType.DMA((n * n_peer,)),
                        pltpu.SemaphoreType.DMA((n,))],
        compiler_params=pltpu.CompilerParams(has_side_effects=True),
    )(*arrs, *([after] if n_after else []))


_HBM = pl.BlockSpec(memory_space=pltpu.HBM)
_SEM = pl.BlockSpec(memory_space=pltpu.SEMAPHORE)
_EFFECT = pltpu.SideEffectType.DATAFLOW_SIDE_EFFECTING


def _peer_of(p):
    x, y, c = lax.axis_index("x"), lax.axis_index("y"), lax.axis_index("c")
    px, py, pc = x ^ ((p >> 2) & 1), y ^ ((p >> 1) & 1), c ^ (p & 1)
    return (px, py, pc), 4 * px + 2 * py + pc


def _slot(land_ref, idx, width):
    if width is None:
        return land_ref.at[idx]
    return land_ref.at[:, pl.ds(pl.multiple_of(idx * width, LANES), width)]


def _xstart(srcs, gather, after, name, cols=None):
    n = len(srcs)
    cols = cols or [False] * n
    widths = [t.shape[1] if cols[a] else None for a, t in enumerate(srcs)]
    lands = [lax.empty((t.shape[0], N_DEV * t.shape[1]) if cols[a] else (((N_DEV,) + t.shape) if gather else t.shape),
                       t.dtype) for a, t in enumerate(srcs)]
    n_after = 0 if after is None else 1

    def body(*refs):
        src_refs, land_refs = refs[:n], refs[n:2 * n]
        refs = refs[n_after:]
        send_sems, recv_sems = refs[2 * n:3 * n], refs[3 * n:4 * n]
        token = refs[6 * n]
        me = 4 * lax.axis_index("x") + 2 * lax.axis_index("y") + lax.axis_index("c")
        for a in range(n):
            for p in range(1, N_DEV):
                dev, peer = _peer_of(p)
                pltpu.make_async_remote_copy(
                    src_ref=src_refs[a] if gather else src_refs[a].at[peer], dst_ref=_slot(land_refs[a], me, widths[a]),
                    send_sem=send_sems[a].at[p - 1], recv_sem=recv_sems[a].at[p - 1],
                    device_id=dev, device_id_type=MESH).start()
        token[...] = jnp.zeros_like(token)

    sems = tuple(pltpu.SemaphoreType.DMA((N_DEV - 1,)) for _ in range(2 * n))
    thru = tuple(pltpu.HBM(t.shape, t.dtype) for t in list(srcs) + list(lands))
    res = pl.pallas_call(
        body, name=name,
        out_shape=sems + thru + (jax.ShapeDtypeStruct((SUBLANES, LANES), F32),),
        in_specs=[_HBM] * (2 * n) + [pl.BlockSpec(memory_space=pl.ANY)] * n_after,
        out_specs=tuple([_SEM] * (2 * n) + [_HBM] * (2 * n) + [pl.BlockSpec(memory_space=pltpu.VMEM)]),
        input_output_aliases={i: 2 * n + i for i in range(2 * n)},
        compiler_params=pltpu.CompilerParams(has_side_effects=_EFFECT),
    )(*[pltpu.with_memory_space_constraint(t, pltpu.HBM) for t in list(srcs) + list(lands)],
      *([after] if n_after else []))
    return res[:n], res[n:2 * n], res[2 * n:3 * n], res[3 * n:4 * n], res[4 * n]


def _xwait(src, land, send_sem, recv_sem, after, gather, name, col=False, place=True):
    width = src.shape[1] if col else None

    def body(src_ref, land_ref, send_ref, recv_ref, after_ref, src_dead, land_out):
        del after_ref, src_dead, land_out
        for p in range(1, N_DEV):
            dev, peer = _peer_of(p)
            cp = pltpu.make_async_remote_copy(
                src_ref=src_ref if gather else src_ref.at[peer], dst_ref=_slot(land_ref, peer, width),
                send_sem=send_ref.at[p - 1], recv_sem=recv_ref.at[p - 1], device_id=dev, device_id_type=MESH)
            cp.wait_send()
            cp.wait_recv()

    src_done, landed = pl.pallas_call(
        body, name=name, out_shape=(pltpu.HBM(src.shape, src.dtype), pltpu.HBM(land.shape, land.dtype)),
        in_specs=[_HBM, _HBM, _SEM, _SEM, pl.BlockSpec(memory_space=pl.ANY)], out_specs=(_HBM, _HBM),
        input_output_aliases={0: 0, 1: 1},
        compiler_params=pltpu.CompilerParams(has_side_effects=_EFFECT),
    )(src, land, send_sem, recv_sem, after)
    if not place:
        return src_done, landed
    me = 4 * lax.axis_index("x") + 2 * lax.axis_index("y") + lax.axis_index("c")
    return _place_own(landed, src_done, me, col, gather, name + "_own")


def _place_own(zone, src, me, col, gather, name):
    if col:
        R, C = src.shape
        src_spec = lambda tr: pl.BlockSpec((tr, C), lambda i, me_ref: (i, 0))
        out_spec = lambda tr: pl.BlockSpec((tr, C), lambda i, me_ref: (i, me_ref[0]))
    else:
        R, C = zone.shape[1:]
        src_spec = ((lambda tr: pl.BlockSpec((tr, C), lambda i, me_ref: (i, 0))) if gather else
                    (lambda tr: pl.BlockSpec((None, tr, C), lambda i, me_ref: (me_ref[0], i, 0))))
        out_spec = lambda tr: pl.BlockSpec((None, tr, C), lambda i, me_ref: (me_ref[0], i, 0))
    tr = R if R <= 512 else 256
    assert R % tr == 0, (name, R, tr)

    def body(me_ref, src_ref, zone_ref, out_ref):
        del me_ref, zone_ref
        out_ref[...] = src_ref[...]

    return pl.pallas_call(
        body, name=name, out_shape=jax.ShapeDtypeStruct(zone.shape, zone.dtype),
        grid_spec=pltpu.PrefetchScalarGridSpec(
            num_scalar_prefetch=1, grid=(R // tr,),
            in_specs=[src_spec(tr), pl.BlockSpec(memory_space=pl.ANY)], out_specs=out_spec(tr)),
        input_output_aliases={2: 0},
    )(jnp.reshape(me, (1,)).astype(jnp.int32), src, zone)


def _xwait_many(srcs, lands, send_sems, recv_sems, after, name):
    n = len(srcs)

    def body(*refs):
        src_refs, land_refs = refs[:n], refs[n:2 * n]
        snd, rcv = refs[2 * n:3 * n], refs[3 * n:4 * n]
        for a in range(n):
            for p in range(1, N_DEV):
                dev, peer = _peer_of(p)
                cp = pltpu.make_async_remote_copy(
                    src_ref=src_refs[a], dst_ref=land_refs[a].at[peer], send_sem=snd[a].at[p - 1],
                    recv_sem=rcv[a].at[p - 1], device_id=dev, device_id_type=MESH)
                cp.wait_send()
                cp.wait_recv()

    res = pl.pallas_call(
        body, name=name, out_shape=tuple(pltpu.HBM(t.shape, t.dtype) for t in list(srcs) + list(lands)),
        in_specs=[_HBM] * (2 * n) + [_SEM] * (2 * n) + [pl.BlockSpec(memory_space=pl.ANY)],
        out_specs=tuple([_HBM] * (2 * n)), input_output_aliases={i: i for i in range(2 * n)},
        compiler_params=pltpu.CompilerParams(has_side_effects=_EFFECT),
    )(*srcs, *lands, *send_sems, *recv_sems, after)
    return res[:n], res[n:]


def _mm(a, b, *, mode, tm, tn, tk, outs, epilogue=None, extras=(), nb=None, tok=None, scatter=None, into=None,
        a_fn=None, name):
    if mode == "nn":
        (M, K), (_, N) = a.shape, b.shape
    elif mode == "nt":
        (M, K), (N, _) = a.shape, b.shape
    else:
        (K, M), (_, N) = a.shape, b.shape
    tm, tn, tk = min(tm, M), min(tn, N), min(tk, K)
    assert M % tm == 0 and N % tn == 0 and K % tk == 0, (name, M, N, K, tm, tn, tk)
    if mode == "nn":
        a_spec = pl.BlockSpec((tm, tk), lambda i, j, k: (i, k))
        b_spec = pl.BlockSpec((tk, tn), lambda i, j, k: (k, j))
        dims = (((1,), (0,)), ((), ()))
    elif mode == "nt":
        a_spec = pl.BlockSpec((tm, tk), lambda i, j, k: (i, k))
        b_spec = pl.BlockSpec((tn, tk), lambda i, j, k: (j, k))
        dims = (((1,), (1,)), ((), ()))
    else:
        a_spec = pl.BlockSpec((tk, tm), lambda i, j, k: (k, i))
        b_spec = pl.BlockSpec((tk, tn), lambda i, j, k: (k, j))
        dims = (((0,), (0,)), ((), ()))
    nk = K // tk
    n_ex, n_out = len(extras), len(outs)
    n_tok = 0 if tok is None else 1
    nbytes = lambda d: jnp.dtype(d).itemsize
    vmem_est = (2 * (tm * tk * nbytes(a.dtype) + tk * tn * nbytes(b.dtype)
                     + sum(tm * tn * nbytes(e.dtype) for e, kind in extras if kind == "tile")
                     + sum(tm * tn * nbytes(d) for d in outs)) + tm * tn * 4)
    assert vmem_est <= VMEM_LIMIT, (name, vmem_est)
    if epilogue is None:
        epilogue = lambda acc, ex: tuple(acc.astype(d) for d in outs)

    n_into = 0 if into is None else 1

    def body(a_ref, b_ref, *refs):
        refs = refs[n_tok:]
        ex_refs, out_refs = refs[:n_ex], refs[n_ex + n_into:n_ex + n_into + n_out]

        def finish(acc):
            res = epilogue(acc, [r[...] for r in ex_refs])
            for o_ref, v in zip(out_refs, res):
                if nb is None:
                    o_ref[...] = v.astype(o_ref.dtype)
                else:
                    for q in range(tn // nb):
                        o_ref[q] = v[:, q * nb:(q + 1) * nb].astype(o_ref.dtype)

        a_tile = a_ref[...] if a_fn is None else a_fn(a_ref[...])
        part = lax.dot_general(a_tile, b_ref[...], dims, preferred_element_type=F32)
        if nk == 1:
            finish(part)
        else:
            acc_ref = refs[n_ex + n_into + n_out]
            k = pl.program_id(2)

            @pl.when(k == 0)
            def _():
                acc_ref[...] = part

            @pl.when(k > 0)
            def _():
                acc_ref[...] += part

            @pl.when(k == nk - 1)
            def _():
                finish(acc_ref[...])

    col = (lambda j: j) if scatter is None else (lambda j: scatter[0] * j + scatter[1])
    ex_specs = [pl.BlockSpec((tm, tn), lambda i, j, k: (i, j)) if kind == "tile"
                else pl.BlockSpec((1, tn), lambda i, j, k: (0, col(j))) for _, kind in extras]
    if nb is not None:
        assert tn % nb == 0, (name, tn, nb)
        o_spec = pl.BlockSpec((tn // nb, tm, nb), lambda i, j, k: (j, i, 0))
        o_shape = (N // nb, M, nb)
    else:
        o_spec = pl.BlockSpec((tm, tn), lambda i, j, k: (i, col(j)))
        o_shape = (M, N if scatter is None else scatter[2])
    assert n_into == 0 or n_out == 1
    res = pl.pallas_call(
        body, name=name, grid=(M // tm, N // tn, nk),
        in_specs=[a_spec, b_spec] + [pl.BlockSpec((SUBLANES, LANES), lambda i, j, k: (0, 0))] * n_tok + ex_specs
                 + [pl.BlockSpec(memory_space=pl.ANY)] * n_into,
        out_specs=tuple([o_spec] * n_out),
        out_shape=tuple(jax.ShapeDtypeStruct(o_shape, d) for d in outs),
        input_output_aliases={2 + n_tok + n_ex: 0} if n_into else {},
        scratch_shapes=[pltpu.VMEM((tm, tn), F32)] if nk > 1 else [],
        compiler_params=_cparams(3, big=True),
    )(a, b, *([tok] if n_tok else []), *[e for e, _ in extras], *([into] if n_into else []))
    return res[0] if n_out == 1 else res


def _mm_rows(a, b, *, mode, tm, seq, ins, outs, epilogue, tok=None, a_fn=None, name):
    M, K = a.shape
    b_parts = list(b) if isinstance(b, (list, tuple)) else [b]
    n_part = len(b_parts)
    assert n_part == 1 or mode == "nt"
    N = b_parts[0].shape[1] if mode == "nn" else b_parts[0].shape[0]
    tm = min(tm, M)
    assert M % tm == 0 and seq % tm == 0, (name, M, seq, tm)
    tpb = seq // tm
    n_b = M // seq
    dims = (((1,), (0,)), ((), ())) if mode == "nn" else (((1,), (1,)), ((), ()))
    n_tok = 0 if tok is None else 1
    n_in, n_out = len(ins), len(outs)

    in_specs, in_arrs = [], []
    for spec in ins:
        kind, arr = spec[0], spec[1]
        in_arrs.append(arr)
        if kind == "tile":
            in_specs.append(pl.BlockSpec((tm, arr.shape[1]), lambda i: (i, 0)))
        elif kind == "tilecol":
            in_specs.append(pl.BlockSpec((tm, spec[2]), lambda i, cb=spec[3]: (i, cb)))
        elif kind == "row":
            in_specs.append(pl.BlockSpec(arr.shape, lambda i: (0, 0)))
        else:
            in_specs.append(pl.BlockSpec((None, 1, arr.shape[2]), lambda i: (i // tpb, 0, 0)))
    out_specs, out_shapes = [], []
    for spec in outs:
        kind = spec[0]
        if kind == "tile":
            out_specs.append(pl.BlockSpec((tm, spec[2]), lambda i: (i, 0)))
            out_shapes.append(jax.ShapeDtypeStruct((M, spec[2]), spec[1]))
        elif kind == "tilecol":
            out_specs.append(pl.BlockSpec((tm, spec[2]), lambda i, cb=spec[3]: (i, cb)))
            out_shapes.append(jax.ShapeDtypeStruct((M, spec[4]), spec[1]))
        elif kind == "acc_row":
            out_specs.append(pl.BlockSpec((1, spec[1]), lambda i: (0, 0)))
            out_shapes.append(jax.ShapeDtypeStruct((1, spec[1]), F32))
        elif kind == "acc_brow":
            out_specs.append(pl.BlockSpec((None, 1, spec[1]), lambda i: (i // tpb, 0, 0)))
            out_shapes.append(jax.ShapeDtypeStruct((n_b, 1, spec[1]), F32))
        else:
            out_specs.append(pl.BlockSpec((SUBLANES, LANES), lambda i: (0, 0)))
            out_shapes.append(jax.ShapeDtypeStruct((SUBLANES, LANES), F32))

    def body(a_ref, *refs):
        b_refs, refs = refs[:n_part], refs[n_part + n_tok:]
        in_refs, out_refs = refs[:n_in], refs[n_in:n_in + n_out]
        i = pl.program_id(0)
        if n_part == 1:
            a_tile = a_ref[...] if a_fn is None else a_fn(a_ref[...])
            prod = lax.dot_general(a_tile, b_refs[0][...], dims, preferred_element_type=F32)
        else:
            w = b_parts[0].shape[1] // N_DEV
            prod = None
            for q in range(n_part):
                a_q = jnp.concatenate([a_ref[:, (n_part * j + q) * w:(n_part * j + q + 1) * w] for j in range(N_DEV)],
                                      axis=1)
                pq = lax.dot_general(a_q, b_refs[q][...], dims, preferred_element_type=F32)
                prod = pq if prod is None else prod + pq
        vals = epilogue(prod, [r[...] for r in in_refs])
        for spec, o_ref, v in zip(outs, out_refs, vals):
            kind = spec[0]
            if kind in ("tile", "tilecol"):
                off = 0
                for part in (v if isinstance(v, tuple) else (v,)):
                    o_ref[:, off:off + part.shape[1]] = part.astype(o_ref.dtype)
                    off += part.shape[1]
            else:
                first = (i % tpb == 0) if kind == "acc_brow" else (i == 0)

                @pl.when(first)
                def _(o_ref=o_ref, v=v):
                    o_ref[...] = jnp.broadcast_to(v, o_ref.shape)

                @pl.when(jnp.logical_not(first))
                def _(o_ref=o_ref, v=v):
                    o_ref[...] += v

    res = pl.pallas_call(
        body, name=name, grid=(M // tm,),
        in_specs=[pl.BlockSpec((tm, K), lambda i: (i, 0))]
                 + [pl.BlockSpec(bp.shape, lambda i: (0, 0), pipeline_mode=pl.Buffered(1)) for bp in b_parts]
                 + [pl.BlockSpec((SUBLANES, LANES), lambda i: (0, 0))] * n_tok + in_specs,
        out_specs=tuple(out_specs), out_shape=tuple(out_shapes),
        compiler_params=_cparams(1, big=True),
    )(a, *b_parts, *([tok] if n_tok else []), *in_arrs)
    return res


def _tok_spec(ts, width, col_block=0):
    return pl.BlockSpec((None, ts, width), lambda b, s: (b, s, col_block))


def _brow_spec(width):
    return pl.BlockSpec((None, 1, width), lambda b, s: (b, 0, 0))


def _modulate(x, sc, sh, ts):
    Bl, S, D = x.shape

    def body(x_ref, sc_ref, sh_ref, o_ref):
        o_ref[...] = (x_ref[...] * (1.0 + sc_ref[...]) + sh_ref[...]).astype(BF16)

    return pl.pallas_call(
        body, name="modulate", grid=(Bl, S // ts),
        in_specs=[_tok_spec(ts, D), _brow_spec(D), _brow_spec(D)],
        out_specs=_tok_spec(ts, D), out_shape=jax.ShapeDtypeStruct((Bl, S, D), BF16),
        compiler_params=_cparams(2),
    )(x, sc, sh)


def _mix_fwd(proj, w_conv, b_conv, w_rg_a, b_rg_a, w_rg_x, b_rg_x, lam, w_sp, b_sp_t, ln_v_g, ln_v_b, *, tm, lw, sw):
    Bl, S, _ = proj.shape
    heads, hd = w_rg_a.shape[0], w_rg_a.shape[1]
    groups = w_sp.shape[0]
    cw = 2 * lw + 2 * sw
    nblk = tm // SGU_BLOCK

    G = tm // SUBLANES
    nc = lw // LANES

    def body(p_ref, wc_ref, bc_ref, wa_ref, ba_ref, wx_ref, bx_ref, lam_ref, wsp_ref, bsp_ref, lg_ref, lb_ref,
             hs_ref, ya_ref, ys_ref, xc_ref, r_ref, ig_ref, a_ref, m_ref,
             xext, hnat, hcar, h7_scr, a7_scr, hp_scr, h0_scr, cp_scr):
        s = pl.program_id(1)

        @pl.when(s == 0)
        def _():
            xext[:, 0:SUBLANES, :] = jnp.zeros((nc, SUBLANES, LANES), F32)
            hcar[...] = jnp.zeros_like(hcar)

        @pl.when(s > 0)
        def _():
            xext[:, 0:SUBLANES, :] = xext[:, tm:tm + SUBLANES, :]

        nl = -lam_ref[...]
        big_l = -LRU_C * (jnp.maximum(nl, 0.0) + _log1p_pos(jnp.exp(-jnp.abs(nl))))

        for c in range(nc):
            cs = slice(c * LANES, (c + 1) * LANES)
            xext[c, SUBLANES:SUBLANES + tm, :] = p_ref[:, cs].astype(F32)
            xs = {st: xext[c, pl.ds(st, G, stride=SUBLANES), :] for st in range(SUBLANES - 3, 2 * SUBLANES)}
            wcs = [wc_ref[k:k + 1, cs] for k in range(4)]
            xc_j = []
            for j in range(SUBLANES):
                acc = bc_ref[:, cs] + xs[SUBLANES + j] * wcs[3]
                for k in (1, 2, 3):
                    acc = acc + xs[SUBLANES + j - k] * wcs[3 - k]
                xc_j.append(acc)
                xc_ref[j * G:(j + 1) * G, cs] = acc
            xcb = jnp.concatenate(xc_j, axis=0).astype(BF16)
            pa = jnp.dot(xcb, wa_ref[c], preferred_element_type=F32)
            px = jnp.dot(xcb, wx_ref[c], preferred_element_type=F32)
            h0 = cp = None
            for j in range(SUBLANES):
                rs = slice(j * G, (j + 1) * G)
                r = _sigmoid(pa[rs] + ba_ref[:, cs])
                ig = _sigmoid(px[rs] + bx_ref[:, cs])
                la = big_l[:, cs] * r
                a = jnp.exp(la)
                th = jnp.tanh(la)
                msq = (-2.0 * th) * pl.reciprocal(1.0 - th, approx=True)
                m = msq * lax.rsqrt(jnp.maximum(msq, 1e-30))
                b = m * (ig * xc_j[j])
                r_ref[rs, cs] = r
                ig_ref[rs, cs] = ig
                a_ref[rs, cs] = a
                m_ref[rs, cs] = m
                h0 = b if j == 0 else a * h0 + b
                cp = a if j == 0 else a * cp
                h0_scr[rs, cs] = h0
                cp_scr[rs, cs] = cp
            h7_scr[:, cs] = h0
            a7_scr[:, cs] = cp
        carry = hcar[0:1, :]
        for g in range(G):
            hp_scr[g:g + 1, :] = carry
            carry = h7_scr[g:g + 1, :] + a7_scr[g:g + 1, :] * carry
        hcar[0:1, :] = carry
        for c in range(nc):
            cs = slice(c * LANES, (c + 1) * LANES)
            hprev = hp_scr[:, cs]
            for j in range(SUBLANES):
                rs = slice(j * G, (j + 1) * G)
                hnat[c, pl.ds(j, G, stride=SUBLANES), :] = h0_scr[rs, cs] + cp_scr[rs, cs] * hprev
            hs = hnat[c]
            hs_ref[:, cs] = hs
            ya_ref[:, cs] = (hs * _gelu(p_ref[:, lw + c * LANES:lw + (c + 1) * LANES].astype(F32))).astype(BF16)

        gu = _gelu(p_ref[:, 2 * lw:2 * lw + sw].astype(F32))
        gv = _gelu(p_ref[:, 2 * lw + sw:cw].astype(F32))
        xhat, _ = _ln_stats(gv)
        vn = (xhat * lg_ref[...] + lb_ref[...]).astype(BF16)
        tpos = lax.broadcasted_iota(jnp.int32, (SGU_BLOCK, SGU_BLOCK), 0) // CHUNK
        spos = lax.broadcasted_iota(jnp.int32, (SGU_BLOCK, SGU_BLOCK), 1) // CHUNK
        gw = sw // groups
        rows_out = []
        for blk in range(nblk):
            r0 = blk * SGU_BLOCK
            cols = []
            for g in range(groups):
                wm = jnp.where(spos <= tpos, wsp_ref[g], 0.0).astype(BF16)
                mixed = jnp.dot(wm, vn[r0:r0 + SGU_BLOCK, g * gw:(g + 1) * gw], preferred_element_type=F32)
                cols.append(mixed + bsp_ref[:, g:g + 1])
            rows_out.append(jnp.concatenate(cols, axis=1))
        mixed_all = jnp.concatenate(rows_out, axis=0) if nblk > 1 else rows_out[0]
        ys_ref[...] = (gu * mixed_all).astype(BF16)

    full = lambda shp: pl.BlockSpec(shp, lambda b, s: (0,) * len(shp))
    return pl.pallas_call(
        body, name="mix_fwd", grid=(Bl, S // tm),
        in_specs=[_tok_spec(tm, cw), full(w_conv.shape), full(b_conv.shape), full(w_rg_a.shape), full(b_rg_a.shape),
                  full(w_rg_x.shape), full(b_rg_x.shape), full(lam.shape), full(w_sp.shape), full(b_sp_t.shape),
                  full(ln_v_g.shape), full(ln_v_b.shape)],
        out_specs=(_tok_spec(tm, lw), _tok_spec(tm, lw), _tok_spec(tm, sw)) + (_tok_spec(tm, lw),) * 5,
        out_shape=(jax.ShapeDtypeStruct((Bl, S, lw), F32), jax.ShapeDtypeStruct((Bl, S, lw), BF16),
                   jax.ShapeDtypeStruct((Bl, S, sw), BF16)) + (jax.ShapeDtypeStruct((Bl, S, lw), F32),) * 5,
        scratch_shapes=[pltpu.VMEM((nc, tm + SUBLANES, LANES), F32), pltpu.VMEM((nc, tm, LANES), F32),
                        pltpu.VMEM((SUBLANES, lw), F32), pltpu.VMEM((G, lw), F32), pltpu.VMEM((G, lw), F32),
                        pltpu.VMEM((G, lw), F32), pltpu.VMEM((tm, lw), F32), pltpu.VMEM((tm, lw), F32)],
        compiler_params=_cparams(2, big=True),
    )(proj, w_conv, b_conv, w_rg_a, b_rg_a, w_rg_x, b_rg_x, lam, w_sp, b_sp_t, ln_v_g, ln_v_b)


def _mix_bwd(proj, hs, dya, dys, dproj, saved, w_conv, b_conv, w_rg_a, b_rg_a, w_rg_x, b_rg_x, lam, w_sp, b_sp_t,
             ln_v_g, ln_v_b, *, tm, lw, sw):
    Bl, S, din = proj.shape
    heads, hd = w_rg_a.shape[0], w_rg_a.shape[1]
    groups = w_sp.shape[0]
    gw = sw // groups
    cw = 2 * lw + 2 * sw
    nblk = tm // SGU_BLOCK
    n_s = S // tm
    per8 = tm // SUBLANES
    halo_rows = 2 * SUBLANES

    G = tm // SUBLANES
    nc = lw // LANES

    def body(p_ref, xh_ref, hs_ref, hh_ref, dya_ref, dys_ref, dpin_ref, xc_ref, r_ref, ig_ref, a_ref, m_ref,
             wc_ref, bc_ref, wa_ref, ba_ref, wx_ref, bx_ref, lam_ref, wsp_ref, bsp_ref, lg_ref, lb_ref,
             dp_ref, dbin_ref, dwc_ref, dbc_ref, dwa_ref, dba_ref, dwx_ref, dbx_ref, dlam_ref, dwsp_ref, dbsp_ref,
             dlg_ref, dlb_ref,
             xext, hext, dnat, dxext, dhcar, g00_scr, p0_scr, a0_scr, cin_scr, g0_scr, pp_scr):
        del dpin_ref
        sr = pl.program_id(1)
        first_tile = sr == n_s - 1

        @pl.when(_first_step())
        def _():
            for ref in (dbin_ref, dwc_ref, dbc_ref, dwa_ref, dba_ref, dwx_ref, dbx_ref, dlam_ref, dwsp_ref, dbsp_ref,
                        dlg_ref, dlb_ref):
                ref[...] = jnp.zeros_like(ref)

        @pl.when(sr == 0)
        def _():
            dhcar[...] = jnp.zeros_like(dhcar)
            dxext[:, tm:tm + SUBLANES, :] = jnp.zeros((nc, SUBLANES, LANES), F32)

        @pl.when(sr > 0)
        def _():
            dxext[:, tm:tm + SUBLANES, :] = dxext[:, 0:SUBLANES, :]

        keep = jnp.where(first_tile, 0.0, 1.0)
        xprev = xh_ref[...].astype(F32)[halo_rows - SUBLANES:halo_rows] * keep
        hprev8 = hh_ref[...] * keep
        nl = -lam_ref[...]
        big_l = -LRU_C * (jnp.maximum(nl, 0.0) + _log1p_pos(jnp.exp(-jnp.abs(nl))))
        dlam_scale = LRU_C * _sigmoid(nl)
        nt = (((1,), (1,)), ((), ()))
        tn = (((0,), (0,)), ((), ()))
        last = SUBLANES - 1

        for c in range(nc):
            cs = slice(c * LANES, (c + 1) * LANES)
            gcs = slice(lw + c * LANES, lw + (c + 1) * LANES)
            xext[c, 0:SUBLANES, :] = xprev[:, cs]
            xext[c, SUBLANES:SUBLANES + tm, :] = p_ref[:, cs].astype(F32)
            hext[c, 0:SUBLANES, :] = hprev8[:, cs]
            dgl_sum = None
            for i in range(SUBLANES):
                rs = slice(i * G, (i + 1) * G)
                ggl, dggl = _gelu_and_grad(p_ref[rs, gcs].astype(F32))
                dy = dya_ref[rs, cs].astype(F32)
                hsv = hs_ref[rs, cs]
                hext[c, SUBLANES + i * G:SUBLANES + (i + 1) * G, :] = hsv
                dgl = dy * hsv * dggl
                dp_ref[rs, gcs] = dgl.astype(BF16)
                dnat[c, rs, :] = dy * ggl
                dgl_sum = _fold8(dgl) if i == 0 else dgl_sum + _fold8(dgl)
            dbin_ref[:, gcs] += _colsum(dgl_sum)
            g0 = pp = None
            for j in range(last, -1, -1):
                rs = slice(j * G, (j + 1) * G)
                dhs_j = dnat[c, pl.ds(j, G, stride=SUBLANES), :]
                if j == last:
                    g0 = dhs_j
                else:
                    an = a_ref[(j + 1) * G:(j + 2) * G, cs]
                    g0 = dhs_j + an * g0
                    pp = an if j == last - 1 else an * pp
                    pp_scr[rs, cs] = pp
                g0_scr[rs, cs] = g0
            g00_scr[:, cs] = g0
            p0_scr[:, cs] = pp
            a0_scr[:, cs] = a_ref[0:G, cs]
        cin = dhcar[0:1, :]
        for g in range(G - 1, -1, -1):
            cin_scr[g:g + 1, :] = cin
            cin = a0_scr[g:g + 1, :] * (g00_scr[g:g + 1, :] + p0_scr[g:g + 1, :] * cin)
        dhcar[0:1, :] = cin
        for c in range(nc):
            cs = slice(c * LANES, (c + 1) * LANES)
            cinv = cin_scr[:, cs]
            dpa_j, dpx_j, dxc_j = [], [], []
            dlam_sum = dba_sum = dbx_sum = None
            for j in range(SUBLANES):
                rs = slice(j * G, (j + 1) * G)
                dh = g0_scr[rs, cs] + (cinv if j == last else pp_scr[rs, cs] * cinv)
                hprev = hext[c, pl.ds(last + j, G, stride=SUBLANES), :]
                xc, r, ig, a, m = xc_ref[rs, cs], r_ref[rs, cs], ig_ref[rs, cs], a_ref[rs, cs], m_ref[rs, cs]
                dixc = dh * m
                dla = (dh * hprev) * a - (dh * (ig * xc)) * ((a * a) * pl.reciprocal(m, approx=True))
                dpa = (dla * big_l[:, cs]) * r * (1.0 - r)
                dpx = (dixc * xc) * ig * (1.0 - ig)
                dpa_j.append(dpa)
                dpx_j.append(dpx)
                dxc_j.append(dixc * ig)
                sums = (_fold8(dla * r), _fold8(dpa), _fold8(dpx))
                dlam_sum, dba_sum, dbx_sum = sums if j == 0 else (dlam_sum + sums[0], dba_sum + sums[1], dbx_sum + sums[2])
            dlam_ref[:, cs] += _colsum(dlam_sum) * dlam_scale[:, cs]
            dba_ref[:, cs] += _colsum(dba_sum)
            dbx_ref[:, cs] += _colsum(dbx_sum)
            dpab = jnp.concatenate(dpa_j, axis=0).astype(BF16)
            dpxb = jnp.concatenate(dpx_j, axis=0).astype(BF16)
            xcb = xc_ref[:, cs].astype(BF16)
            dxc = (jnp.concatenate(dxc_j, axis=0)
                   + lax.dot_general(dpab, wa_ref[c], nt, preferred_element_type=F32)
                   + lax.dot_general(dpxb, wx_ref[c], nt, preferred_element_type=F32))
            dwa_ref[c] += lax.dot_general(xcb, dpab, tn, preferred_element_type=F32)
            dwx_ref[c] += lax.dot_general(xcb, dpxb, tn, preferred_element_type=F32)

            dbc_ref[:, cs] += _colsum(dxc)
            xs = {st: xext[c, pl.ds(st, G, stride=SUBLANES), :] for st in range(SUBLANES - 3, 2 * SUBLANES)}
            for k in range(4):
                tot = None
                for j in range(SUBLANES):
                    part = _fold8(dxc[j * G:(j + 1) * G] * xs[SUBLANES + j - (3 - k)])
                    tot = part if tot is None else tot + part
                dwc_ref[k:k + 1, cs] += _colsum(tot)
            for j in range(SUBLANES):
                dxext[c, pl.ds(j, G, stride=SUBLANES), :] = dxc[j * G:(j + 1) * G]
            us = {st: dxext[c, pl.ds(st, G, stride=SUBLANES), :] for st in range(SUBLANES + 3)}
            wcs = [wc_ref[k:k + 1, cs] for k in range(4)]
            for j in range(SUBLANES):
                acc = us[j] * wcs[3]
                for k in (1, 2, 3):
                    acc = acc + us[j + k] * wcs[3 - k]
                dnat[c, pl.ds(j, G, stride=SUBLANES), :] = acc
            dxl = dnat[c]
            dp_ref[:, cs] = dxl.astype(BF16)
            dbin_ref[:, cs] += _colsum(dxl)

        gu, dgu_dx = _gelu_and_grad(p_ref[:, 2 * lw:2 * lw + sw].astype(F32))
        gv, dgv_dx = _gelu_and_grad(p_ref[:, 2 * lw + sw:cw].astype(F32))
        xhat, rstd = _ln_stats(gv)
        vn = (xhat * lg_ref[...] + lb_ref[...]).astype(BF16)
        dys = dys_ref[...].astype(F32)
        dmixed = dys * gu
        dmb = dmixed.astype(BF16)
        tpos = lax.broadcasted_iota(jnp.int32, (SGU_BLOCK, SGU_BLOCK), 0) // CHUNK
        spos = lax.broadcasted_iota(jnp.int32, (SGU_BLOCK, SGU_BLOCK), 1) // CHUNK
        causal = spos <= tpos
        mixed_rows, dvn_rows = [], []
        for blk in range(nblk):
            rs = slice(blk * SGU_BLOCK, (blk + 1) * SGU_BLOCK)
            mcols, dcols = [], []
            for g in range(groups):
                cs = slice(g * gw, (g + 1) * gw)
                wm = jnp.where(causal, wsp_ref[g], 0.0).astype(BF16)
                mcols.append(jnp.dot(wm, vn[rs, cs], preferred_element_type=F32) + bsp_ref[:, g:g + 1])
                dcols.append(lax.dot_general(wm, dmb[rs, cs], tn, preferred_element_type=F32))
                dw = lax.dot_general(dmb[rs, cs], vn[rs, cs], nt, preferred_element_type=F32)
                dwsp_ref[g] += jnp.where(causal, dw, 0.0)
                dbsp_ref[:, g:g + 1] += jnp.sum(dmixed[rs, cs], axis=1, keepdims=True)
            mixed_rows.append(jnp.concatenate(mcols, axis=1))
            dvn_rows.append(jnp.concatenate(dcols, axis=1))
        mixed_all = jnp.concatenate(mixed_rows, axis=0) if nblk > 1 else mixed_rows[0]
        dvn = jnp.concatenate(dvn_rows, axis=0) if nblk > 1 else dvn_rows[0]
        du = dys * mixed_all * dgu_dx
        dlg_ref[...] += _colsum(dvn * xhat)
        dlb_ref[...] += _colsum(dvn)
        dv = _ln_bwd(dvn, xhat, rstd, lg_ref[...]) * dgv_dx
        dp_ref[:, 2 * lw:2 * lw + sw] = du.astype(BF16)
        dp_ref[:, 2 * lw + sw:cw] = dv.astype(BF16)
        dbin_ref[:, 2 * lw:2 * lw + sw] += _colsum(du)
        dbin_ref[:, 2 * lw + sw:cw] += _colsum(dv)

    rev = lambda s: n_s - 1 - s
    tile = lambda w: pl.BlockSpec((None, tm, w), lambda b, s: (b, rev(s), 0))
    halo = lambda w: pl.BlockSpec((None, SUBLANES, w), lambda b, s: (b, jnp.maximum(rev(s) * per8 - 1, 0), 0))
    xhalo = pl.BlockSpec((None, halo_rows, lw), lambda b, s: (b, jnp.maximum(rev(s) * (tm // halo_rows) - 1, 0), 0))
    full = lambda shp: pl.BlockSpec(shp, lambda b, s: (0,) * len(shp))
    small = [w_conv, b_conv, w_rg_a, b_rg_a, w_rg_x, b_rg_x, lam, w_sp, b_sp_t, ln_v_g, ln_v_b]
    acc_shapes = [(1, cw), w_conv.shape, b_conv.shape, w_rg_a.shape, b_rg_a.shape, w_rg_x.shape, b_rg_x.shape,
                  lam.shape, w_sp.shape, b_sp_t.shape, ln_v_g.shape, ln_v_b.shape]
    res = pl.pallas_call(
        body, name="mix_bwd", grid=(Bl, n_s),
        in_specs=[tile(cw), xhalo, tile(lw), halo(lw), tile(lw), tile(sw), pl.BlockSpec(memory_space=pl.ANY)]
                 + [tile(lw)] * 5 + [full(w.shape) for w in small],
        out_specs=tuple([tile(cw)] + [full(shp) for shp in acc_shapes]),
        out_shape=tuple([jax.ShapeDtypeStruct((Bl, S, din), BF16)] + [jax.ShapeDtypeStruct(shp, F32) for shp in acc_shapes]),
        input_output_aliases={6: 0},
        scratch_shapes=[pltpu.VMEM((nc, tm + SUBLANES, LANES), F32), pltpu.VMEM((nc, tm + SUBLANES, LANES), F32),
                        pltpu.VMEM((nc, tm, LANES), F32), pltpu.VMEM((nc, tm + SUBLANES, LANES), F32),
                        pltpu.VMEM((SUBLANES, lw), F32), pltpu.VMEM((G, lw), F32), pltpu.VMEM((G, lw), F32),
                        pltpu.VMEM((G, lw), F32), pltpu.VMEM((G, lw), F32), pltpu.VMEM((tm, lw), F32),
                        pltpu.VMEM((tm, lw), F32)],
        compiler_params=_cparams(2, big=True),
    )(proj, proj, hs, hs, dya, dys, dproj, *saved, *small)
    return res


def _ada_fwd(c_all, w_ada):
    R, D = c_all.shape
    nb = w_ada.shape[1]

    def body(c_ref, w_ref, act_ref, o_ref):
        cv = c_ref[...]
        act = (cv * _sigmoid(cv)).astype(BF16)
        act_ref[...] = act
        o_ref[...] = jnp.dot(act, w_ref[...].astype(BF16), preferred_element_type=F32)

    return pl.pallas_call(
        body, name="ada_fwd",
        out_shape=(jax.ShapeDtypeStruct((R, D), BF16), jax.ShapeDtypeStruct((R, nb), F32)),
        compiler_params=pltpu.CompilerParams(vmem_limit_bytes=VMEM_LIMIT),
    )(c_all, w_ada)


def _ada_bwd(c_act, dmod_cols):
    R, D = c_act.shape
    nb = dmod_cols.shape[1]

    def body(act_ref, d_ref, o_ref, b_ref):
        o_ref[...] = lax.dot_general(act_ref[...], d_ref[...].astype(BF16), (((0,), (0,)), ((), ())),
                                     preferred_element_type=F32)
        b_ref[...] = _colsum(d_ref[...])

    return pl.pallas_call(
        body, name="ada_bwd", out_shape=(jax.ShapeDtypeStruct((D, nb), F32), jax.ShapeDtypeStruct((1, nb), F32)),
        compiler_params=pltpu.CompilerParams(vmem_limit_bytes=VMEM_LIMIT),
    )(c_act, dmod_cols)


def _adamw(w, g_slots, m, v, *, tr, name, own=None):
    R, C = w.shape
    n_slot = g_slots.shape[0]
    tr = min(tr, R)
    assert R % tr == 0, (name, R, tr)
    c1 = 1.0 / (1.0 - ADAM_B1 ** ADAM_STEP)
    c2 = 1.0 / (1.0 - ADAM_B2 ** ADAM_STEP)
    n_own = 0 if own is None else 1

    def body(me_ref, w_ref, g_ref, *refs):
        m_ref, v_ref, go_ref, d_ref, mo_ref, vo_ref = refs[n_own:]
        slot = lambda d: (jnp.where(me_ref[0] == d, refs[0][...], g_ref[d]) if n_own else g_ref[d]).astype(F32)
        g = slot(0)
        for d in range(1, n_slot):
            g = g + slot(d)
        mn = ADAM_B1 * m_ref[...] + (1.0 - ADAM_B1) * g
        vn = ADAM_B2 * v_ref[...] + (1.0 - ADAM_B2) * (g * g)
        go_ref[...] = g
        mo_ref[...] = mn
        vo_ref[...] = vn
        d_ref[...] = -ADAM_LR * ((mn * c1) / (jnp.sqrt(vn * c2) + ADAM_EPS) + ADAM_WD * w_ref[...])

    me = 4 * lax.axis_index("x") + 2 * lax.axis_index("y") + lax.axis_index("c")
    blk = pl.BlockSpec((tr, C), lambda i, me_ref: (i, 0))
    own_specs = [pl.BlockSpec((None, tr, C), lambda i, me_ref: (me_ref[0], i, 0))] * n_own
    return pl.pallas_call(
        body, name=name, out_shape=tuple(jax.ShapeDtypeStruct((R, C), F32) for _ in range(4)),
        grid_spec=pltpu.PrefetchScalarGridSpec(
            num_scalar_prefetch=1, grid=(R // tr,),
            in_specs=[blk, pl.BlockSpec((n_slot, tr, C), lambda i, me_ref: (0, i, 0))] + own_specs + [blk, blk],
            out_specs=(blk, blk, blk, blk)),
        compiler_params=_cparams(1, big=True),
    )(jnp.reshape(me, (1,)).astype(jnp.int32), w, g_slots, *([own] if n_own else []), m, v)


def _adamw_many(ws, g_slots, g_owns, ms, vs, *, name):
    n = len(ws)
    c1 = 1.0 / (1.0 - ADAM_B1 ** ADAM_STEP)
    c2 = 1.0 / (1.0 - ADAM_B2 ** ADAM_STEP)

    def body(*refs):
        w_refs, g_refs, o_refs = refs[:n], refs[n:2 * n], refs[2 * n:3 * n]
        m_refs, v_refs = refs[3 * n:4 * n], refs[4 * n:5 * n]
        outs = refs[5 * n:]
        me = 4 * lax.axis_index("x") + 2 * lax.axis_index("y") + lax.axis_index("c")
        for i in range(n):
            own = o_refs[i][...]
            g = jnp.where(me == 0, own, g_refs[i][0])
            for d in range(1, N_DEV):
                g = g + jnp.where(me == d, own, g_refs[i][d])
            mn = ADAM_B1 * m_refs[i][...] + (1.0 - ADAM_B1) * g
            vn = ADAM_B2 * v_refs[i][...] + (1.0 - ADAM_B2) * (g * g)
            outs[i][...] = g
            outs[n + i][...] = -ADAM_LR * ((mn * c1) / (jnp.sqrt(vn * c2) + ADAM_EPS) + ADAM_WD * w_refs[i][...])
            outs[2 * n + i][...] = mn
            outs[3 * n + i][...] = vn

    res = pl.pallas_call(
        body, name=name, out_shape=tuple(jax.ShapeDtypeStruct(w.shape, F32) for _ in range(4) for w in ws),
        compiler_params=pltpu.CompilerParams(vmem_limit_bytes=VMEM_LIMIT),
    )(*ws, *g_slots, *g_owns, *ms, *vs)
    return res[:n], res[n:2 * n], res[2 * n:3 * n], res[3 * n:]


SMALL_NAMES = ("b_ada", "b_in", "b_conv", "w_rg_a", "b_rg_a", "w_rg_x", "b_rg_x", "lru_lambda", "w_sp", "b_sp",
               "ln_v_g", "ln_v_b", "ln1_g", "ln1_b", "ln2_g", "ln2_b")
WEIGHT_ORDER = ("w_ada", "b_ada", "w_in", "b_in", "w_conv", "b_conv", "w_rg_a", "b_rg_a", "w_rg_x", "b_rg_x",
                "lru_lambda", "w_sp", "b_sp", "ln_v_g", "ln_v_b", "w_o_lru", "w_o_sgu", "w_out", "ln1_g", "ln1_b",
                "w_up", "w_down", "ln2_g", "ln2_b")


def _blocked_cols(w2d):
    K, N = w2d.shape
    return jnp.transpose(w2d.reshape(K, N_DEV, N // N_DEV), (1, 0, 2))


def _unblock_cols(wb):
    n, K, nb = wb.shape
    return jnp.transpose(wb, (1, 0, 2)).reshape(K, n * nb)


def kernel(x, c, w_ada, b_ada, w_in, b_in, w_conv, b_conv, w_rg_a, b_rg_a, w_rg_x, b_rg_x, lru_lambda, w_sp, b_sp, ln_v_g, ln_v_b, w_o_lru, w_o_sgu, w_out, ln1_g, ln1_b, w_up, w_down, ln2_g, ln2_b, loss_target, m_w_ada, m_b_ada, m_w_in, m_b_in, m_w_conv, m_b_conv, m_w_rg_a, m_b_rg_a, m_w_rg_x, m_b_rg_x, m_lru_lambda, m_w_sp, m_b_sp, m_ln_v_g, m_ln_v_b, m_w_o_lru, m_w_o_sgu, m_w_out, m_ln1_g, m_ln1_b, m_w_up, m_w_down, m_ln2_g, m_ln2_b, v_w_ada, v_b_ada, v_w_in, v_b_in, v_w_conv, v_b_conv, v_w_rg_a, v_b_rg_a, v_w_rg_x, v_b_rg_x, v_lru_lambda, v_w_sp, v_b_sp, v_ln_v_g, v_ln_v_b, v_w_o_lru, v_w_o_sgu, v_w_out, v_ln1_g, v_ln1_b, v_w_up, v_w_down, v_ln2_g, v_ln2_b):
    W = dict(w_ada=w_ada, b_ada=b_ada, w_in=w_in, b_in=b_in, w_conv=w_conv, b_conv=b_conv, w_rg_a=w_rg_a,
             b_rg_a=b_rg_a, w_rg_x=w_rg_x, b_rg_x=b_rg_x, lru_lambda=lru_lambda, w_sp=w_sp, b_sp=b_sp,
             ln_v_g=ln_v_g, ln_v_b=ln_v_b, w_o_lru=w_o_lru, w_o_sgu=w_o_sgu, w_out=w_out, ln1_g=ln1_g, ln1_b=ln1_b,
             w_up=w_up, w_down=w_down, ln2_g=ln2_g, ln2_b=ln2_b)
    Mo = dict(w_ada=m_w_ada, b_ada=m_b_ada, w_in=m_w_in, b_in=m_b_in, w_conv=m_w_conv, b_conv=m_b_conv,
              w_rg_a=m_w_rg_a, b_rg_a=m_b_rg_a, w_rg_x=m_w_rg_x, b_rg_x=m_b_rg_x, lru_lambda=m_lru_lambda,
              w_sp=m_w_sp, b_sp=m_b_sp, ln_v_g=m_ln_v_g, ln_v_b=m_ln_v_b, w_o_lru=m_w_o_lru, w_o_sgu=m_w_o_sgu,
              w_out=m_w_out, ln1_g=m_ln1_g, ln1_b=m_ln1_b, w_up=m_w_up, w_down=m_w_down, ln2_g=m_ln2_g,
              ln2_b=m_ln2_b)
    Vo = dict(w_ada=v_w_ada, b_ada=v_b_ada, w_in=v_w_in, b_in=v_b_in, w_conv=v_w_conv, b_conv=v_b_conv,
              w_rg_a=v_w_rg_a, b_rg_a=v_b_rg_a, w_rg_x=v_w_rg_x, b_rg_x=v_b_rg_x, lru_lambda=v_lru_lambda,
              w_sp=v_w_sp, b_sp=v_b_sp, ln_v_g=v_ln_v_g, ln_v_b=v_ln_v_b, w_o_lru=v_w_o_lru, w_o_sgu=v_w_o_sgu,
              w_out=v_w_out, ln1_g=v_ln1_g, ln1_b=v_ln1_b, w_up=v_w_up, w_down=v_w_down, ln2_g=v_ln2_g,
              ln2_b=v_ln2_b)

    Bl, S, D = x.shape
    T = Bl * S
    lw = b_conv.shape[-1]
    sw = ln_v_g.shape[-1]
    din = b_in.shape[-1]
    dff = w_up.shape[-1] * N_DEV
    ts = min(2048, S)
    tmix = min(256, S)
    trow = min(512, S)

    c_pad = jnp.pad(c, ((0, SUBLANES - Bl), (0, 0)))
    c_g, wconv_g = _exchange([c_pad, w_conv[0]], True, "xchg_c")
    wconv_full = _unblock_cols(wconv_g)
    c_act, modcols = _ada_fwd(c_g.reshape(N_DEV * SUBLANES, D), w_ada[0])
    (mod_slots,) = _exchange([modcols.reshape(N_DEV, SUBLANES, -1)], False, "xchg_mod")

    nbw = din // N_DEV // WIN_PARTS
    wnames = tuple("win%d" % q for q in range(WIN_PARTS)) + ("wol", "wos", "wout", "wup", "wdown")
    shards = [w_in[0][:, q * nbw:(q + 1) * nbw].astype(BF16) for q in range(WIN_PARTS)] + [
        w_o_lru[0].astype(BF16), w_o_sgu[0].astype(BF16), w_out[0].astype(BF16), w_up[0].astype(BF16),
        w_down[0].astype(BF16)]
    col_sharded = [True] * WIN_PARTS + [False, True, False, True, False]
    g_send, g_recv, g_src, g_land, g_tok = _xstart(shards, True, mod_slots, "gather_start", cols=col_sharded)
    gidx = {n: i for i, n in enumerate(wnames)}

    def gathered(n, after):
        i = gidx[n]
        return _xwait(g_src[i], g_land[i], g_send[i], g_recv[i], after, True, "gather_wait_" + n, col=col_sharded[i])

    mod = _unblock_cols(mod_slots)[:Bl] + (b_ada + g_tok[0, 0])
    sh1, sc1, gt1, sh2, sc2, gt2 = [mod[:, i * D:(i + 1) * D].reshape(Bl, 1, D) for i in range(6)]

    wa_b, wx_b = w_rg_a[0].astype(BF16), w_rg_x[0].astype(BF16)
    b_sp_t = jnp.transpose(b_sp[0])
    small_mix = (wconv_full, b_conv, wa_b, b_rg_a, wx_b, b_rg_x, lru_lambda, w_sp[0], b_sp_t, ln_v_g, ln_v_b)

    h = _modulate(x, sc1, sh1, ts)
    proj, win_parts = None, []
    for q in range(WIN_PARTS):
        wq = gathered("win%d" % q, h if q == 0 else proj)
        win_parts.append(wq)
        proj = _mm(h.reshape(T, D), wq, mode="nn", tm=8192, tn=nbw, tk=D, outs=[BF16], extras=[(b_in, "row")],
                   epilogue=lambda acc, ex: (acc + ex[0],), scatter=(WIN_PARTS, q, din), into=proj,
                   name="mm_proj%d" % q)
    proj3 = proj.reshape(Bl, S, din)
    hs, ya_pre, ysgu, *lru_saved = _mix_fwd(proj3, *small_mix, tm=tmix, lw=lw, sw=sw)
    Wol = gathered("wol", ya_pre).reshape(lw, D)
    Wos = gathered("wos", ysgu)
    y_a = _mm(ya_pre.reshape(T, lw), Wol, mode="nn", tm=2048, tn=D, tk=lw, outs=[BF16], name="mm_ya")
    x2d, tgt2d = x.reshape(T, D), loss_target.reshape(T, D)
    gate_cb = (din - 2 * D) // D

    def ep_merge(y_b, v):
        ya, ga, gb = [t.astype(F32) for t in v]
        yb = y_b.astype(BF16).astype(F32)
        return [yb, _sigmoid(ga) * ya + _sigmoid(gb) * yb]

    y_b, merged = _mm_rows(ysgu.reshape(T, sw), Wos, mode="nn", tm=trow, seq=S,
                           ins=[("tile", y_a), ("tilecol", proj, D, gate_cb), ("tilecol", proj, D, gate_cb + 1)],
                           outs=[("tile", BF16, D), ("tile", BF16, D)], epilogue=ep_merge, name="mm_yb_merge")
    Wout = gathered("wout", merged).reshape(D, D)

    def ep_ln1(mix_acc, v):
        x_, gt, g, b, sc, sh = v
        mixr = mix_acc.astype(BF16).astype(F32)
        xhat, rstd = _ln_stats(ALPHA * x_ + (1.0 + gt) * mixr)
        x1_ = xhat * g + b
        return [mixr, x1_, x1_ * (1.0 + sc) + sh, xhat, jnp.broadcast_to(rstd, (rstd.shape[0], LANES))]

    mix, x1, h2, xhat1, rstd1 = _mm_rows(
        merged, Wout, mode="nn", tm=trow, seq=S,
        ins=[("tile", x2d), ("brow", gt1), ("row", ln1_g), ("row", ln1_b), ("brow", sc2), ("brow", sh2)],
        outs=[("tile", BF16, D), ("tile", F32, D), ("tile", BF16, D), ("tile", BF16, D), ("tile", F32, LANES)],
        epilogue=ep_ln1, name="mm_mix_ln1")
    Wup = gathered("wup", h2)
    relu_up = _mm(h2, Wup, mode="nn", tm=2048, tn=1024, tk=D, outs=[BF16],
                  epilogue=lambda acc, ex: (jnp.maximum(acc, 0.0),), name="mm_up")
    square = lambda t: t * t
    Wdown = gathered("wdown", relu_up).reshape(dff, D)

    def ep_ln2(f_acc, v):
        x1_, t_, gt, g, b = v
        xhat, rstd = _ln_stats(ALPHA * x1_ + (1.0 + gt) * f_acc)
        err = xhat * g + b - t_
        loss_t = 0.5 * jnp.sum(jnp.mean(err * err, axis=-1, keepdims=True))
        dy = err * (1.0 / D)
        dz = _ln_bwd(dy, xhat, rstd, g)
        return [dz * (2.0 * (1.0 + gt)), ALPHA * dz, _colsum(dz * f_acc), _colsum(dy * xhat), _colsum(dy), loss_t]

    df2x, dx1p, dgt2, dg2, db2, loss_part = _mm_rows(
        relu_up, Wdown, mode="nn", tm=trow, seq=S, a_fn=square,
        ins=[("tile", x1), ("tile", tgt2d), ("brow", gt2), ("row", ln2_g), ("row", ln2_b)],
        outs=[("tile", BF16, D), ("tile", F32, D), ("acc_brow", D), ("acc_row", D), ("acc_row", D), ("acc_scalar",)],
        epilogue=ep_ln2, name="mm_down_ln2")
    loss = lax.psum(loss_part[0, 0], ("x", "y", "c"))

    def send_grads(parts, name):
        snd, rcv, src, land, tok = _xstart(parts, False, None, name + "_start")
        return [(src[i], land[i], snd[i], rcv[i]) for i in range(len(parts))], tok

    dup = _mm(df2x, Wdown, mode="nt", tm=2048, tn=1024, tk=D, outs=[BF16], extras=[(relu_up, "tile")],
              epilogue=lambda acc, ex: (acc * ex[0].astype(F32),), name="mm_dup")
    g_wdown = _mm(relu_up, df2x, mode="tn", tm=1024, tn=D, tk=4096, outs=[BF16], a_fn=square,
                  epilogue=lambda acc, ex: (0.5 * acc,), name="mm_gwdown")
    (x_wdown,), tok = send_grads([g_wdown.reshape(N_DEV, dff // N_DEV, D)], "gx_wdown")
    def ep_ln1_bwd(dh2, v):
        dx1p_, x1_, xh_, rs_, mix_, sc, gt, g = v
        mixv = mix_.astype(F32)
        dx1 = dx1p_ + dh2 * (1.0 + sc)
        xhat, rstd = xh_.astype(F32), rs_[:, 0:1]
        dz = _ln_bwd(dx1, xhat, rstd, g)
        return [ALPHA * dz, dz * (1.0 + gt), _colsum(dh2 * x1_), _colsum(dh2), _colsum(dz * mixv),
                _colsum(dx1 * xhat), _colsum(dx1)]

    dxp, dmix, dsc2, dsh2, dgt1, dg1, db1 = _mm_rows(
        dup, Wup, mode="nt", tm=trow, seq=S, tok=tok,
        ins=[("tile", dx1p), ("tile", x1), ("tile", xhat1), ("tile", rstd1), ("tile", mix), ("brow", sc2), ("brow", gt1),
             ("row", ln1_g)],
        outs=[("tile", F32, D), ("tile", BF16, D), ("acc_brow", D), ("acc_brow", D), ("acc_brow", D), ("acc_row", D),
              ("acc_row", D)],
        epilogue=ep_ln1_bwd, name="mm_dh2_ln1b")
    g_wup = _mm(h2, dup, mode="tn", tm=D, tn=1024, tk=4096, outs=[BF16], nb=dff // N_DEV, name="mm_gwup")
    (x_wup,), tok = send_grads([g_wup], "gx_wup")

    def ep_merge_bwd(dm, v):
        ya, yb, ga, gb = [t.astype(F32) for t in v]
        sa, sb = _sigmoid(ga), _sigmoid(gb)
        dga, dgb = dm * ya * sa * (1.0 - sa), dm * yb * sb * (1.0 - sb)
        return [dm * sa, dm * sb, (dga, dgb), jnp.concatenate([_colsum(dga), _colsum(dgb)], axis=1)]

    dy_a, dy_b, dproj, dbin_hi = _mm_rows(
        dmix, Wout, mode="nt", tm=trow, seq=S, tok=tok,
        ins=[("tile", y_a), ("tile", y_b), ("tilecol", proj, D, gate_cb), ("tilecol", proj, D, gate_cb + 1)],
        outs=[("tile", BF16, D), ("tile", BF16, D), ("tilecol", BF16, 2 * D, gate_cb // 2, din), ("acc_row", 2 * D)],
        epilogue=ep_merge_bwd, name="mm_dmerged_mb")
    g_wout = _mm(merged, dmix, mode="tn", tm=D, tn=D, tk=4096, outs=[BF16], name="mm_gwout")
    (x_wout,), tok = send_grads([g_wout.reshape(N_DEV, D // N_DEV, D)], "gx_wout")
    dya_pre = _mm(dy_a, Wol, mode="nt", tm=2048, tn=lw, tk=D, outs=[BF16], tok=tok, name="mm_dya")
    dysgu = _mm(dy_b, Wos, mode="nt", tm=2048, tn=sw, tk=D, outs=[BF16], name="mm_dys")
    g_wol = _mm(ya_pre.reshape(T, lw), dy_a, mode="tn", tm=lw, tn=D, tk=2048, outs=[BF16], name="mm_gwol")
    g_wos = _mm(ysgu.reshape(T, sw), dy_b, mode="tn", tm=sw, tn=D, tk=2048, outs=[BF16], nb=D // N_DEV,
                name="mm_gwos")
    (x_wol, x_wos), tok = send_grads([g_wol.reshape(N_DEV, lw // N_DEV, D), g_wos], "gx_wo")
    small_mix_b = (wconv_full, b_conv + tok[0, 0]) + small_mix[2:]
    (dproj, dbin_lo, g_wconv, g_bconv, g_wa, g_ba, g_wx, g_bx, g_lam, g_wsp, g_bsp_t, g_lvg, g_lvb) = _mix_bwd(
        proj3, hs, dya_pre.reshape(Bl, S, lw), dysgu.reshape(Bl, S, sw), dproj.reshape(Bl, S, din), lru_saved,
        *small_mix_b, tm=tmix, lw=lw, sw=sw)
    dproj2 = dproj.reshape(T, din)
    small_names = [n for n in SMALL_NAMES if n != "b_ada"]
    small_g = dict(b_in=jnp.concatenate([dbin_lo, dbin_hi], axis=-1), b_conv=g_bconv, w_rg_a=g_wa[None], b_rg_a=g_ba,
                   w_rg_x=g_wx[None], b_rg_x=g_bx, lru_lambda=g_lam, w_sp=g_wsp[None],
                   b_sp=jnp.transpose(g_bsp_t)[None], ln_v_g=g_lvg, ln_v_b=g_lvb, ln1_g=dg1, ln1_b=db1, ln2_g=dg2,
                   ln2_b=db2)
    gs_snd, gs_rcv, gs_src, gs_land, tok_s = _xstart([small_g[n] for n in small_names], True, None, "gsmall_start")
    g_win = _mm(h.reshape(T, D), dproj2, mode="tn", tm=D, tn=din // 4, tk=2048, outs=[BF16], nb=din // N_DEV,
                tok=tok_s, name="mm_gwin")
    (x_win,), tok = send_grads([g_win], "gx_win")

    def ep_final(dh, v):
        dxp_, x_, sc = v
        return [dxp_ + dh * (1.0 + sc), _colsum(dh * x_), _colsum(dh)]

    grad_x, dsc1, dsh1 = _mm_rows(dproj2, win_parts, mode="nt", tm=trow, seq=S, tok=tok,
                                  ins=[("tile", dxp), ("tile", x2d), ("brow", sc1)],
                                  outs=[("tile", F32, D), ("acc_brow", D), ("acc_brow", D)], epilogue=ep_final,
                                  name="mm_dh_final")
    grad_x = grad_x.reshape(Bl, S, D)

    out_g, out_d, out_m, out_v = {}, {}, {}, {}

    def adam(name, g_slots, tr, own=None):
        shp = W[name].shape
        w2, m2, v2 = [t.reshape(g_slots.shape[1:]) for t in (W[name], Mo[name], Vo[name])]
        g, d, mn, vn = _adamw(w2, g_slots, m2, v2, tr=tr, name="adam_" + name, own=own)
        out_g[name], out_d[name], out_m[name], out_v[name] = [t.reshape(shp) for t in (g, d, mn, vn)]

    def adam_exchanged(name, handle, tr, after):
        own, slots = _xwait(*handle, after, False, "gx_%s_wait" % name, place=False)
        adam(name, slots, tr, own=own)

    adam_exchanged("w_down", x_wdown, 256, dsh1)
    adam_exchanged("w_up", x_wup, 512, dsh1)
    adam_exchanged("w_out", x_wout, 128, dsh1)
    adam_exchanged("w_o_lru", x_wol, 160, dsh1)
    adam_exchanged("w_o_sgu", x_wos, 256, dsh1)
    gs_own, gs_slots = _xwait_many(gs_src, gs_land, gs_snd, gs_rcv, dsh1, "gsmall_wait")
    res_small = _adamw_many([W[n] for n in small_names], gs_slots, gs_own, [Mo[n] for n in small_names],
                            [Vo[n] for n in small_names], name="adam_small")
    for dst, vals in zip((out_g, out_d, out_m, out_v), res_small):
        dst.update(dict(zip(small_names, vals)))

    dmod = jnp.concatenate([dsh1, dsc1, dgt1, dsh2, dsc2, dgt2], axis=-1).reshape(Bl, 6 * D)
    dmod_b = _blocked_cols(jnp.pad(dmod, ((0, SUBLANES - Bl), (0, 0))))
    dmod_s, gwconv_s = _exchange([dmod_b, _blocked_cols(g_wconv)], False, "xchg_dmod", after=out_g["ln2_b"])
    g_wada, g_bada_mine = _ada_bwd(c_act, dmod_s.reshape(N_DEV * SUBLANES, -1))
    (g_bada_all,) = _exchange([g_bada_mine], True, "xchg_bada")
    adam("w_ada", g_wada[None], 512)
    adam("b_ada", g_bada_all.reshape(1, 1, 6 * D), 1)
    adam("w_conv", gwconv_s, 8)
    adam_exchanged("w_in", x_win, 512, g_bada_all)

    return (loss, grad_x, *[out_g[n] for n in WEIGHT_ORDER], *[out_d[n] for n in WEIGHT_ORDER],
            *[out_m[n] for n in WEIGHT_ORDER], *[out_v[n] for n in WEIGHT_ORDER])
```

```python
import math

import jax
import jax.numpy as jnp
from jax import lax
from jax.experimental import pallas as pl
from jax.experimental.pallas import tpu as pltpu

N_DEV = 8
LN_EPS = 1e-5
LRU_C = 8.0
CHUNK = 64
SGU_BLOCK = 128
ALPHA = 2.0 ** 0.25
ADAM_LR = 0.001
ADAM_B1 = 0.9
ADAM_B2 = 0.999
ADAM_EPS = 1e-08
ADAM_WD = 0.01
ADAM_STEP = 10
GELU_K0 = math.sqrt(2.0 / math.pi)
GELU_K1 = 0.044715

SUBLANES = 8
LANES = 128
VMEM_LIMIT = 56 * 1024 * 1024
WIN_PARTS = 3

F32 = jnp.float32
BF16 = jnp.bfloat16
MESH = pl.DeviceIdType.MESH


def _cparams(n_axes, big=False):
    return pltpu.CompilerParams(dimension_semantics=("arbitrary",) * n_axes,
                                vmem_limit_bytes=VMEM_LIMIT if big else None)


def _sigmoid(x):
    return 0.5 * jnp.tanh(0.5 * x) + 0.5


def _gelu(x):
    t = jnp.tanh(x * (GELU_K0 + (GELU_K0 * GELU_K1) * (x * x)))
    hx = 0.5 * x
    return hx + hx * t


def _gelu_and_grad(x):
    x2 = x * x
    t = jnp.tanh(x * (GELU_K0 + (GELU_K0 * GELU_K1) * x2))
    hx = 0.5 * x
    g = hx + hx * t
    dg = (0.5 + 0.5 * t) + (hx * (1.0 - t * t)) * (GELU_K0 + (3.0 * GELU_K0 * GELU_K1) * x2)
    return g, dg


def _log1p_pos(e):
    p = e * (1.0 - e * (1.0 / 2.0) + e * e * (1.0 / 3.0) - e * e * e * (1.0 / 4.0))
    return jnp.where(e < 1e-2, p, jnp.log(1.0 + e))


def _ln_stats(z):
    mu = jnp.mean(z, axis=-1, keepdims=True)
    zc = z - mu
    var = jnp.mean(zc * zc, axis=-1, keepdims=True)
    rstd = lax.rsqrt(var + LN_EPS)
    return zc * rstd, rstd


def _ln_bwd(dy, xhat, rstd, g):
    dxh = dy * g
    m1 = jnp.mean(dxh, axis=-1, keepdims=True)
    m2 = jnp.mean(dxh * xhat, axis=-1, keepdims=True)
    return rstd * (dxh - m1 - xhat * m2)


def _colsum(v):
    return jnp.sum(v, axis=0, keepdims=True)


def _fold8(v):
    out = v[0:SUBLANES]
    for i in range(1, v.shape[0] // SUBLANES):
        out = out + v[i * SUBLANES:(i + 1) * SUBLANES]
    return out


def _first_step():
    return jnp.logical_and(pl.program_id(0) == 0, pl.program_id(1) == 0)


def _exchange(arrs, gather, name, after=None):
    n = len(arrs)
    n_peer = N_DEV - 1
    n_after = 0 if after is None else 1

    def body(*refs):
        ins, outs = refs[:n], refs[n + n_after:2 * n + n_after]
        send_sems, recv_sems, loc_sems = refs[2 * n + n_after:]
        x, y, c = lax.axis_index("x"), lax.axis_index("y"), lax.axis_index("c")
        me = 4 * x + 2 * y + c
        started = []
        for a in range(n):
            src_me = ins[a] if gather else ins[a].at[me]
            lc = pltpu.make_async_copy(src_me, outs[a].at[me], loc_sems.at[a])
            lc.start()
            started.append((lc, None))
        for p in range(1, N_DEV):
            px, py, pc = x ^ ((p >> 2) & 1), y ^ ((p >> 1) & 1), c ^ (p & 1)
            peer = 4 * px + 2 * py + pc
            for a in range(n):
                k = a * n_peer + (p - 1)
                src = ins[a] if gather else ins[a].at[peer]
                cp = pltpu.make_async_remote_copy(src_ref=src, dst_ref=outs[a].at[me],
                                                  send_sem=send_sems.at[k], recv_sem=recv_sems.at[k],
                                                  device_id=(px, py, pc), device_id_type=MESH)
                cp.start()
                rc = pltpu.make_async_remote_copy(src_ref=src, dst_ref=outs[a].at[peer],
                                                  send_sem=send_sems.at[k], recv_sem=recv_sems.at[k],
                                                  device_id=(px, py, pc), device_id_type=MESH)
                started.append((cp, rc))
        for cp, rc in started:
            if rc is None:
                cp.wait()
            else:
                cp.wait_send()
                rc.wait_recv()

    hbm = pl.BlockSpec(memory_space=pltpu.HBM)
    out_shape = tuple(
        jax.ShapeDtypeStruct(((N_DEV,) + a.shape) if gather else a.shape, a.dtype) for a in arrs)
    return pl.pallas_call(
        body, name=name, out_shape=out_shape,
        in_specs=[hbm] * n + [pl.BlockSpec(memory_space=pl.ANY)] * n_after, out_specs=tuple([hbm] * n),
        scratch_shapes=[pltpu.SemaphoreType.DMA((n * n_peer,)), pltpu.SemaphoreType.DMA((n * n_peer,)),
                        pltpu.SemaphoreType.DMA((n,))],
        compiler_params=pltpu.CompilerParams(has_side_effects=True),
    )(*arrs, *([after] if n_after else []))


_HBM = pl.BlockSpec(memory_space=pltpu.HBM)
_SEM = pl.BlockSpec(memory_space=pltpu.SEMAPHORE)
_EFFECT = pltpu.SideEffectType.DATAFLOW_SIDE_EFFECTING


def _peer_of(p):
    x, y, c = lax.axis_index("x"), lax.axis_index("y"), lax.axis_index("c")
    px, py, pc = x ^ ((p >> 2) & 1), y ^ ((p >> 1) & 1), c ^ (p & 1)
    return (px, py, pc), 4 * px + 2 * py + pc


def _slot(land_ref, idx, width):
    if width is None:
        return land_ref.at[idx]
    return land_ref.at[:, pl.ds(pl.multiple_of(idx * width, LANES), width)]


def _xstart(srcs, gather, after, name, cols=None):
    n = len(srcs)
    cols = cols or [False] * n
    widths = [t.shape[1] if cols[a] else None for a, t in enumerate(srcs)]
    lands = [lax.empty((t.shape[0], N_DEV * t.shape[1]) if cols[a] else (((N_DEV,) + t.shape) if gather else t.shape),
                       t.dtype) for a, t in enumerate(srcs)]
    n_after = 0 if after is None else 1

    def body(*refs):
        src_refs, land_refs = refs[:n], refs[n:2 * n]
        refs = refs[n_after:]
        send_sems, recv_sems = refs[2 * n:3 * n], refs[3 * n:4 * n]
        token = refs[6 * n]
        me = 4 * lax.axis_index("x") + 2 * lax.axis_index("y") + lax.axis_index("c")
        for a in range(n):
            for p in range(1, N_DEV):
                dev, peer = _peer_of(p)
                pltpu.make_async_remote_copy(
                    src_ref=src_refs[a] if gather else src_refs[a].at[peer], dst_ref=_slot(land_refs[a], me, widths[a]),
                    send_sem=send_sems[a].at[p - 1], recv_sem=recv_sems[a].at[p - 1],
                    device_id=dev, device_id_type=MESH).start()
        token[...] = jnp.zeros_like(token)

    sems = tuple(pltpu.SemaphoreType.DMA((N_DEV - 1,)) for _ in range(2 * n))
    thru = tuple(pltpu.HBM(t.shape, t.dtype) for t in list(srcs) + list(lands))
    res = pl.pallas_call(
        body, name=name,
        out_shape=sems + thru + (jax.ShapeDtypeStruct((SUBLANES, LANES), F32),),
        in_specs=[_HBM] * (2 * n) + [pl.BlockSpec(memory_space=pl.ANY)] * n_after,
        out_specs=tuple([_SEM] * (2 * n) + [_HBM] * (2 * n) + [pl.BlockSpec(memory_space=pltpu.VMEM)]),
        input_output_aliases={i: 2 * n + i for i in range(2 * n)},
        compiler_params=pltpu.CompilerParams(has_side_effects=_EFFECT),
    )(*[pltpu.with_memory_space_constraint(t, pltpu.HBM) for t in list(srcs) + list(lands)],
      *([after] if n_after else []))
    return res[:n], res[n:2 * n], res[2 * n:3 * n], res[3 * n:4 * n], res[4 * n]


def _xwait(src, land, send_sem, recv_sem, after, gather, name, col=False, place=True):
    width = src.shape[1] if col else None

    def body(src_ref, land_ref, send_ref, recv_ref, after_ref, src_dead, land_out):
        del after_ref, src_dead, land_out
        for p in range(1, N_DEV):
            dev, peer = _peer_of(p)
            cp = pltpu.make_async_remote_copy(
                src_ref=src_ref if gather else src_ref.at[peer], dst_ref=_slot(land_ref, peer, width),
                send_sem=send_ref.at[p - 1], recv_sem=recv_ref.at[p - 1], device_id=dev, device_id_type=MESH)
            cp.wait_send()
            cp.wait_recv()

    src_done, landed = pl.pallas_call(
        body, name=name, out_shape=(pltpu.HBM(src.shape, src.dtype), pltpu.HBM(land.shape, land.dtype)),
        in_specs=[_HBM, _HBM, _SEM, _SEM, pl.BlockSpec(memory_space=pl.ANY)], out_specs=(_HBM, _HBM),
        input_output_aliases={0: 0, 1: 1},
        compiler_params=pltpu.CompilerParams(has_side_effects=_EFFECT),
    )(src, land, send_sem, recv_sem, after)
    if not place:
        return src_done, landed
    me = 4 * lax.axis_index("x") + 2 * lax.axis_index("y") + lax.axis_index("c")
    return _place_own(landed, src_done, me, col, gather, name + "_own")


def _place_own(zone, src, me, col, gather, name):
    if col:
        R, C = src.shape
        src_spec = lambda tr: pl.BlockSpec((tr, C), lambda i, me_ref: (i, 0))
        out_spec = lambda tr: pl.BlockSpec((tr, C), lambda i, me_ref: (i, me_ref[0]))
    else:
        R, C = zone.shape[1:]
        src_spec = ((lambda tr: pl.BlockSpec((tr, C), lambda i, me_ref: (i, 0))) if gather else
                    (lambda tr: pl.BlockSpec((None, tr, C), lambda i, me_ref: (me_ref[0], i, 0))))
        out_spec = lambda tr: pl.BlockSpec((None, tr, C), lambda i, me_ref: (me_ref[0], i, 0))
    tr = R if R <= 512 else 256
    assert R % tr == 0, (name, R, tr)

    def body(me_ref, src_ref, zone_ref, out_ref):
        del me_ref, zone_ref
        out_ref[...] = src_ref[...]

    return pl.pallas_call(
        body, name=name, out_shape=jax.ShapeDtypeStruct(zone.shape, zone.dtype),
        grid_spec=pltpu.PrefetchScalarGridSpec(
            num_scalar_prefetch=1, grid=(R // tr,),
            in_specs=[src_spec(tr), pl.BlockSpec(memory_space=pl.ANY)], out_specs=out_spec(tr)),
        input_output_aliases={2: 0},
    )(jnp.reshape(me, (1,)).astype(jnp.int32), src, zone)


def _xwait_many(srcs, lands, send_sems, recv_sems, after, name):
    n = len(srcs)

    def body(*refs):
        src_refs, land_refs = refs[:n], refs[n:2 * n]
        snd, rcv = refs[2 * n:3 * n], refs[3 * n:4 * n]
        for a in range(n):
            for p in range(1, N_DEV):
                dev, peer = _peer_of(p)
                cp = pltpu.make_async_remote_copy(
                    src_ref=src_refs[a], dst_ref=land_refs[a].at[peer], send_sem=snd[a].at[p - 1],
                    recv_sem=rcv[a].at[p - 1], device_id=dev, device_id_type=MESH)
                cp.wait_send()
                cp.wait_recv()

    res = pl.pallas_call(
        body, name=name, out_shape=tuple(pltpu.HBM(t.shape, t.dtype) for t in list(srcs) + list(lands)),
        in_specs=[_HBM] * (2 * n) + [_SEM] * (2 * n) + [pl.BlockSpec(memory_space=pl.ANY)],
        out_specs=tuple([_HBM] * (2 * n)), input_output_aliases={i: i for i in range(2 * n)},
        compiler_params=pltpu.CompilerParams(has_side_effects=_EFFECT),
    )(*srcs, *lands, *send_sems, *recv_sems, after)
    return res[:n], res[n:]


def _mm(a, b, *, mode, tm, tn, tk, outs, epilogue=None, extras=(), nb=None, tok=None, scatter=None, into=None,
        a_fn=None, name):
    if mode == "nn":
        (M, K), (_, N) = a.shape, b.shape
    elif mode == "nt":
        (M, K), (N, _) = a.shape, b.shape
    else:
        (K, M), (_, N) = a.shape, b.shape
    tm, tn, tk = min(tm, M), min(tn, N), min(tk, K)
    assert M % tm == 0 and N % tn == 0 and K % tk == 0, (name, M, N, K, tm, tn, tk)
    if mode == "nn":
        a_spec = pl.BlockSpec((tm, tk), lambda i, j, k: (i, k))
        b_spec = pl.BlockSpec((tk, tn), lambda i, j, k: (k, j))
        dims = (((1,), (0,)), ((), ()))
    elif mode == "nt":
        a_spec = pl.BlockSpec((tm, tk), lambda i, j, k: (i, k))
        b_spec = pl.BlockSpec((tn, tk), lambda i, j, k: (j, k))
        dims = (((1,), (1,)), ((), ()))
    else:
        a_spec = pl.BlockSpec((tk, tm), lambda i, j, k: (k, i))
        b_spec = pl.BlockSpec((tk, tn), lambda i, j, k: (k, j))
        dims = (((0,), (0,)), ((), ()))
    nk = K // tk
    n_ex, n_out = len(extras), len(outs)
    n_tok = 0 if tok is None else 1
    nbytes = lambda d: jnp.dtype(d).itemsize
    vmem_est = (2 * (tm * tk * nbytes(a.dtype) + tk * tn * nbytes(b.dtype)
                     + sum(tm * tn * nbytes(e.dtype) for e, kind in extras if kind == "tile")
                     + sum(tm * tn * nbytes(d) for d in outs)) + tm * tn * 4)
    assert vmem_est <= VMEM_LIMIT, (name, vmem_est)
    if epilogue is None:
        epilogue = lambda acc, ex: tuple(acc.astype(d) for d in outs)

    n_into = 0 if into is None else 1

    def body(a_ref, b_ref, *refs):
        refs = refs[n_tok:]
        ex_refs, out_refs = refs[:n_ex], refs[n_ex + n_into:n_ex + n_into + n_out]

        def finish(acc):
            res = epilogue(acc, [r[...] for r in ex_refs])
            for o_ref, v in zip(out_refs, res):
                if nb is None:
                    o_ref[...] = v.astype(o_ref.dtype)
                else:
                    for q in range(tn // nb):
                        o_ref[q] = v[:, q * nb:(q + 1) * nb].astype(o_ref.dtype)

        a_tile = a_ref[...] if a_fn is None else a_fn(a_ref[...])
        part = lax.dot_general(a_tile, b_ref[...], dims, preferred_element_type=F32)
        if nk == 1:
            finish(part)
        else:
            acc_ref = refs[n_ex + n_into + n_out]
            k = pl.program_id(2)

            @pl.when(k == 0)
            def _():
                acc_ref[...] = part

            @pl.when(k > 0)
            def _():
                acc_ref[...] += part

            @pl.when(k == nk - 1)
            def _():
                finish(acc_ref[...])

    col = (lambda j: j) if scatter is None else (lambda j: scatter[0] * j + scatter[1])
    ex_specs = [pl.BlockSpec((tm, tn), lambda i, j, k: (i, j)) if kind == "tile"
                else pl.BlockSpec((1, tn), lambda i, j, k: (0, col(j))) for _, kind in extras]
    if nb is not None:
        assert tn % nb == 0, (name, tn, nb)
        o_spec = pl.BlockSpec((tn // nb, tm, nb), lambda i, j, k: (j, i, 0))
        o_shape = (N // nb, M, nb)
    else:
        o_spec = pl.BlockSpec((tm, tn), lambda i, j, k: (i, col(j)))
        o_shape = (M, N if scatter is None else scatter[2])
    assert n_into == 0 or n_out == 1
    res = pl.pallas_call(
        body, name=name, grid=(M // tm, N // tn, nk),
        in_specs=[a_spec, b_spec] + [pl.BlockSpec((SUBLANES, LANES), lambda i, j, k: (0, 0))] * n_tok + ex_specs
                 + [pl.BlockSpec(memory_space=pl.ANY)] * n_into,
        out_specs=tuple([o_spec] * n_out),
        out_shape=tuple(jax.ShapeDtypeStruct(o_shape, d) for d in outs),
        input_output_aliases={2 + n_tok + n_ex: 0} if n_into else {},
        scratch_shapes=[pltpu.VMEM((tm, tn), F32)] if nk > 1 else [],
        compiler_params=_cparams(3, big=True),
    )(a, b, *([tok] if n_tok else []), *[e for e, _ in extras], *([into] if n_into else []))
    return res[0] if n_out == 1 else res


def _mm_rows(a, b, *, mode, tm, seq, ins, outs, epilogue, tok=None, a_fn=None, second=None, name):
    M, K = a.shape
    b_parts = list(b) if isinstance(b, (list, tuple)) else [b]
    n_part = len(b_parts)
    assert n_part == 1 or mode == "nt"
    N = b_parts[0].shape[1] if mode == "nn" else b_parts[0].shape[0]
    tm = min(tm, M)
    assert M % tm == 0 and seq % tm == 0, (name, M, seq, tm)
    tpb = seq // tm
    n_b = M // seq
    dims = (((1,), (0,)), ((), ())) if mode == "nn" else (((1,), (1,)), ((), ()))
    n_tok = 0 if tok is None else 1
    n_in, n_out = len(ins), len(outs)
    n_second = 0 if second is None else 1
    second_specs = ([pl.BlockSpec((tm, second[0].shape[1]), lambda i: (i, 0)),
                     pl.BlockSpec(second[1].shape, lambda i: (0, 0), pipeline_mode=pl.Buffered(1))] if n_second else [])

    in_specs, in_arrs = [], []
    for spec in ins:
        kind, arr = spec[0], spec[1]
        in_arrs.append(arr)
        if kind == "tile":
            in_specs.append(pl.BlockSpec((tm, arr.shape[1]), lambda i: (i, 0)))
        elif kind == "tilecol":
            in_specs.append(pl.BlockSpec((tm, spec[2]), lambda i, cb=spec[3]: (i, cb)))
        elif kind == "row":
            in_specs.append(pl.BlockSpec(arr.shape, lambda i: (0, 0)))
        else:
            in_specs.append(pl.BlockSpec((None, 1, arr.shape[2]), lambda i: (i // tpb, 0, 0)))
    out_specs, out_shapes = [], []
    for spec in outs:
        kind = spec[0]
        if kind == "tile":
            out_specs.append(pl.BlockSpec((tm, spec[2]), lambda i: (i, 0)))
            out_shapes.append(jax.ShapeDtypeStruct((M, spec[2]), spec[1]))
        elif kind == "tilecol":
            out_specs.append(pl.BlockSpec((tm, spec[2]), lambda i, cb=spec[3]: (i, cb)))
            out_shapes.append(jax.ShapeDtypeStruct((M, spec[4]), spec[1]))
        elif kind == "acc_row":
            out_specs.append(pl.BlockSpec((1, spec[1]), lambda i: (0, 0)))
            out_shapes.append(jax.ShapeDtypeStruct((1, spec[1]), F32))
        elif kind == "acc_brow":
            out_specs.append(pl.BlockSpec((None, 1, spec[1]), lambda i: (i // tpb, 0, 0)))
            out_shapes.append(jax.ShapeDtypeStruct((n_b, 1, spec[1]), F32))
        else:
            out_specs.append(pl.BlockSpec((SUBLANES, LANES), lambda i: (0, 0)))
            out_shapes.append(jax.ShapeDtypeStruct((SUBLANES, LANES), F32))

    def body(a_ref, *refs):
        b_refs, refs = refs[:n_part], refs[n_part + n_tok:]
        second_refs, refs = refs[:2 * n_second], refs[2 * n_second:]
        in_refs, out_refs = refs[:n_in], refs[n_in:n_in + n_out]
        i = pl.program_id(0)
        if n_part == 1:
            a_tile = a_ref[...] if a_fn is None else a_fn(a_ref[...])
            prod = lax.dot_general(a_tile, b_refs[0][...], dims, preferred_element_type=F32)
        else:
            w = b_parts[0].shape[1] // N_DEV
            prod = None
            for q in range(n_part):
                a_q = jnp.concatenate([a_ref[:, (n_part * j + q) * w:(n_part * j + q + 1) * w] for j in range(N_DEV)],
                                      axis=1)
                pq = lax.dot_general(a_q, b_refs[q][...], dims, preferred_element_type=F32)
                prod = pq if prod is None else prod + pq
        extra = [jnp.dot(second_refs[0][...], second_refs[1][...], preferred_element_type=F32)] if n_second else []
        vals = epilogue(prod, extra + [r[...] for r in in_refs])
        for spec, o_ref, v in zip(outs, out_refs, vals):
            kind = spec[0]
            if kind in ("tile", "tilecol"):
                off = 0
                for part in (v if isinstance(v, tuple) else (v,)):
                    o_ref[:, off:off + part.shape[1]] = part.astype(o_ref.dtype)
                    off += part.shape[1]
            else:
                first = (i % tpb == 0) if kind == "acc_brow" else (i == 0)

                @pl.when(first)
                def _(o_ref=o_ref, v=v):
                    o_ref[...] = jnp.broadcast_to(v, o_ref.shape)

                @pl.when(jnp.logical_not(first))
                def _(o_ref=o_ref, v=v):
                    o_ref[...] += v

    res = pl.pallas_call(
        body, name=name, grid=(M // tm,),
        in_specs=[pl.BlockSpec((tm, K), lambda i: (i, 0))]
                 + [pl.BlockSpec(bp.shape, lambda i: (0, 0), pipeline_mode=pl.Buffered(1)) for bp in b_parts]
                 + [pl.BlockSpec((SUBLANES, LANES), lambda i: (0, 0))] * n_tok + second_specs + in_specs,
        out_specs=tuple(out_specs), out_shape=tuple(out_shapes),
        compiler_params=_cparams(1, big=True),
    )(a, *b_parts, *([tok] if n_tok else []), *(second or ()), *in_arrs)
    return res


def _tok_spec(ts, width, col_block=0):
    return pl.BlockSpec((None, ts, width), lambda b, s: (b, s, col_block))


def _brow_spec(width):
    return pl.BlockSpec((None, 1, width), lambda b, s: (b, 0, 0))


def _modulate(x, sc, sh, ts):
    Bl, S, D = x.shape

    def body(x_ref, sc_ref, sh_ref, o_ref):
        o_ref[...] = (x_ref[...] * (1.0 + sc_ref[...]) + sh_ref[...]).astype(BF16)

    return pl.pallas_call(
        body, name="modulate", grid=(Bl, S // ts),
        in_specs=[_tok_spec(ts, D), _brow_spec(D), _brow_spec(D)],
        out_specs=_tok_spec(ts, D), out_shape=jax.ShapeDtypeStruct((Bl, S, D), BF16),
        compiler_params=_cparams(2),
    )(x, sc, sh)


def _mix_fwd(proj, w_conv, b_conv, w_rg_a, b_rg_a, w_rg_x, b_rg_x, lam, w_sp, b_sp_t, ln_v_g, ln_v_b, *, tm, lw, sw):
    Bl, S, _ = proj.shape
    heads, hd = w_rg_a.shape[0], w_rg_a.shape[1]
    groups = w_sp.shape[0]
    cw = 2 * lw + 2 * sw
    nblk = tm // SGU_BLOCK

    G = tm // SUBLANES
    nc = lw // LANES

    def body(p_ref, wc_ref, bc_ref, wa_ref, ba_ref, wx_ref, bx_ref, lam_ref, wsp_ref, bsp_ref, lg_ref, lb_ref,
             hs_ref, ya_ref, ys_ref, xc_ref, r_ref, ig_ref, a_ref, m_ref,
             xext, hnat, hcar, h7_scr, a7_scr, hp_scr, h0_scr, cp_scr):
        s = pl.program_id(1)

        @pl.when(s == 0)
        def _():
            xext[:, 0:SUBLANES, :] = jnp.zeros((nc, SUBLANES, LANES), F32)
            hcar[...] = jnp.zeros_like(hcar)

        @pl.when(s > 0)
        def _():
            xext[:, 0:SUBLANES, :] = xext[:, tm:tm + SUBLANES, :]

        nl = -lam_ref[...]
        big_l = -LRU_C * (jnp.maximum(nl, 0.0) + _log1p_pos(jnp.exp(-jnp.abs(nl))))

        for c in range(nc):
            cs = slice(c * LANES, (c + 1) * LANES)
            xext[c, SUBLANES:SUBLANES + tm, :] = p_ref[:, cs].astype(F32)
            xs = {st: xext[c, pl.ds(st, G, stride=SUBLANES), :] for st in range(SUBLANES - 3, 2 * SUBLANES)}
            wcs = [wc_ref[k:k + 1, cs] for k in range(4)]
            xc_j = []
            for j in range(SUBLANES):
                acc = bc_ref[:, cs] + xs[SUBLANES + j] * wcs[3]
                for k in (1, 2, 3):
                    acc = acc + xs[SUBLANES + j - k] * wcs[3 - k]
                xc_j.append(acc)
                xc_ref[j * G:(j + 1) * G, cs] = acc
            xcb = jnp.concatenate(xc_j, axis=0).astype(BF16)
            pa = jnp.dot(xcb, wa_ref[c], preferred_element_type=F32)
            px = jnp.dot(xcb, wx_ref[c], preferred_element_type=F32)
            h0 = cp = None
            for j in range(SUBLANES):
                rs = slice(j * G, (j + 1) * G)
                r = _sigmoid(pa[rs] + ba_ref[:, cs])
                ig = _sigmoid(px[rs] + bx_ref[:, cs])
                la = big_l[:, cs] * r
                a = jnp.exp(la)
                th = jnp.tanh(la)
                msq = (-2.0 * th) * pl.reciprocal(1.0 - th, approx=True)
                m = msq * lax.rsqrt(jnp.maximum(msq, 1e-30))
                b = m * (ig * xc_j[j])
                r_ref[rs, cs] = r
                ig_ref[rs, cs] = ig
                a_ref[rs, cs] = a
                m_ref[rs, cs] = m
                h0 = b if j == 0 else a * h0 + b
                cp = a if j == 0 else a * cp
                h0_scr[rs, cs] = h0
                cp_scr[rs, cs] = cp
            h7_scr[:, cs] = h0
            a7_scr[:, cs] = cp
        carry = hcar[0:1, :]
        for g in range(G):
            hp_scr[g:g + 1, :] = carry
            carry = h7_scr[g:g + 1, :] + a7_scr[g:g + 1, :] * carry
        hcar[0:1, :] = carry
        for c in range(nc):
            cs = slice(c * LANES, (c + 1) * LANES)
            hprev = hp_scr[:, cs]
            for j in range(SUBLANES):
                rs = slice(j * G, (j + 1) * G)
                hnat[c, pl.ds(j, G, stride=SUBLANES), :] = h0_scr[rs, cs] + cp_scr[rs, cs] * hprev
            hs = hnat[c]
            hs_ref[:, cs] = hs
            ya_ref[:, cs] = (hs * _gelu(p_ref[:, lw + c * LANES:lw + (c + 1) * LANES].astype(F32))).astype(BF16)

        gu = _gelu(p_ref[:, 2 * lw:2 * lw + sw].astype(F32))
        gv = _gelu(p_ref[:, 2 * lw + sw:cw].astype(F32))
        xhat, _ = _ln_stats(gv)
        vn = (xhat * lg_ref[...] + lb_ref[...]).astype(BF16)
        tpos = lax.broadcasted_iota(jnp.int32, (SGU_BLOCK, SGU_BLOCK), 0) // CHUNK
        spos = lax.broadcasted_iota(jnp.int32, (SGU_BLOCK, SGU_BLOCK), 1) // CHUNK
        gw = sw // groups
        rows_out = []
        for blk in range(nblk):
            r0 = blk * SGU_BLOCK
            cols = []
            for g in range(groups):
                wm = jnp.where(spos <= tpos, wsp_ref[g], 0.0).astype(BF16)
                mixed = jnp.dot(wm, vn[r0:r0 + SGU_BLOCK, g * gw:(g + 1) * gw], preferred_element_type=F32)
                cols.append(mixed + bsp_ref[:, g:g + 1])
            rows_out.append(jnp.concatenate(cols, axis=1))
        mixed_all = jnp.concatenate(rows_out, axis=0) if nblk > 1 else rows_out[0]
        ys_ref[...] = (gu * mixed_all).astype(BF16)

    full = lambda shp: pl.BlockSpec(shp, lambda b, s: (0,) * len(shp))
    return pl.pallas_call(
        body, name="mix_fwd", grid=(Bl, S // tm),
        in_specs=[_tok_spec(tm, cw), full(w_conv.shape), full(b_conv.shape), full(w_rg_a.shape), full(b_rg_a.shape),
                  full(w_rg_x.shape), full(b_rg_x.shape), full(lam.shape), full(w_sp.shape), full(b_sp_t.shape),
                  full(ln_v_g.shape), full(ln_v_b.shape)],
        out_specs=(_tok_spec(tm, lw), _tok_spec(tm, lw), _tok_spec(tm, sw)) + (_tok_spec(tm, lw),) * 5,
        out_shape=(jax.ShapeDtypeStruct((Bl, S, lw), F32), jax.ShapeDtypeStruct((Bl, S, lw), BF16),
                   jax.ShapeDtypeStruct((Bl, S, sw), BF16)) + (jax.ShapeDtypeStruct((Bl, S, lw), F32),) * 5,
        scratch_shapes=[pltpu.VMEM((nc, tm + SUBLANES, LANES), F32), pltpu.VMEM((nc, tm, LANES), F32),
                        pltpu.VMEM((SUBLANES, lw), F32), pltpu.VMEM((G, lw), F32), pltpu.VMEM((G, lw), F32),
                        pltpu.VMEM((G, lw), F32), pltpu.VMEM((tm, lw), F32), pltpu.VMEM((tm, lw), F32)],
        compiler_params=_cparams(2, big=True),
    )(proj, w_conv, b_conv, w_rg_a, b_rg_a, w_rg_x, b_rg_x, lam, w_sp, b_sp_t, ln_v_g, ln_v_b)


def _mix_bwd(proj, hs, dya, dys, dproj, saved, w_conv, b_conv, w_rg_a, b_rg_a, w_rg_x, b_rg_x, lam, w_sp, b_sp_t,
             ln_v_g, ln_v_b, *, tm, lw, sw):
    Bl, S, din = proj.shape
    heads, hd = w_rg_a.shape[0], w_rg_a.shape[1]
    groups = w_sp.shape[0]
    gw = sw // groups
    cw = 2 * lw + 2 * sw
    nblk = tm // SGU_BLOCK
    n_s = S // tm
    per8 = tm // SUBLANES
    halo_rows = 2 * SUBLANES

    G = tm // SUBLANES
    nc = lw // LANES

    def body(p_ref, xh_ref, hs_ref, hh_ref, dya_ref, dys_ref, dpin_ref, xc_ref, r_ref, ig_ref, a_ref, m_ref,
             wc_ref, bc_ref, wa_ref, ba_ref, wx_ref, bx_ref, lam_ref, wsp_ref, bsp_ref, lg_ref, lb_ref,
             dp_ref, dbin_ref, dwc_ref, dbc_ref, dwa_ref, dba_ref, dwx_ref, dbx_ref, dlam_ref, dwsp_ref, dbsp_ref,
             dlg_ref, dlb_ref,
             xext, hext, dnat, dxext, dhcar, g00_scr, p0_scr, a0_scr, cin_scr, g0_scr, pp_scr):
        del dpin_ref
        sr = pl.program_id(1)
        first_tile = sr == n_s - 1

        @pl.when(_first_step())
        def _():
            for ref in (dbin_ref, dwc_ref, dbc_ref, dwa_ref, dba_ref, dwx_ref, dbx_ref, dlam_ref, dwsp_ref, dbsp_ref,
                        dlg_ref, dlb_ref):
                ref[...] = jnp.zeros_like(ref)

        @pl.when(sr == 0)
        def _():
            dhcar[...] = jnp.zeros_like(dhcar)
            dxext[:, tm:tm + SUBLANES, :] = jnp.zeros((nc, SUBLANES, LANES), F32)

        @pl.when(sr > 0)
        def _():
            dxext[:, tm:tm + SUBLANES, :] = dxext[:, 0:SUBLANES, :]

        keep = jnp.where(first_tile, 0.0, 1.0)
        xprev = xh_ref[...].astype(F32)[halo_rows - SUBLANES:halo_rows] * keep
        hprev8 = hh_ref[...] * keep
        nl = -lam_ref[...]
        big_l = -LRU_C * (jnp.maximum(nl, 0.0) + _log1p_pos(jnp.exp(-jnp.abs(nl))))
        dlam_scale = LRU_C * _sigmoid(nl)
        nt = (((1,), (1,)), ((), ()))
        tn = (((0,), (0,)), ((), ()))
        last = SUBLANES - 1

        for c in range(nc):
            cs = slice(c * LANES, (c + 1) * LANES)
            gcs = slice(lw + c * LANES, lw + (c + 1) * LANES)
            xext[c, 0:SUBLANES, :] = xprev[:, cs]
            xext[c, SUBLANES:SUBLANES + tm, :] = p_ref[:, cs].astype(F32)
            hext[c, 0:SUBLANES, :] = hprev8[:, cs]
            dgl_sum = None
            for i in range(SUBLANES):
                rs = slice(i * G, (i + 1) * G)
                ggl, dggl = _gelu_and_grad(p_ref[rs, gcs].astype(F32))
                dy = dya_ref[rs, cs].astype(F32)
                hsv = hs_ref[rs, cs]
                hext[c, SUBLANES + i * G:SUBLANES + (i + 1) * G, :] = hsv
                dgl = dy * hsv * dggl
                dp_ref[rs, gcs] = dgl.astype(BF16)
                dnat[c, rs, :] = dy * ggl
                dgl_sum = _fold8(dgl) if i == 0 else dgl_sum + _fold8(dgl)
            dbin_ref[:, gcs] += _colsum(dgl_sum)
            g0 = pp = None
            for j in range(last, -1, -1):
                rs = slice(j * G, (j + 1) * G)
                dhs_j = dnat[c, pl.ds(j, G, stride=SUBLANES), :]
                if j == last:
                    g0 = dhs_j
                else:
                    an = a_ref[(j + 1) * G:(j + 2) * G, cs]
                    g0 = dhs_j + an * g0
                    pp = an if j == last - 1 else an * pp
                    pp_scr[rs, cs] = pp
                g0_scr[rs, cs] = g0
            g00_scr[:, cs] = g0
            p0_scr[:, cs] = pp
            a0_scr[:, cs] = a_ref[0:G, cs]
        cin = dhcar[0:1, :]
        for g in range(G - 1, -1, -1):
            cin_scr[g:g + 1, :] = cin
            cin = a0_scr[g:g + 1, :] * (g00_scr[g:g + 1, :] + p0_scr[g:g + 1, :] * cin)
        dhcar[0:1, :] = cin
        for c in range(nc):
            cs = slice(c * LANES, (c + 1) * LANES)
            cinv = cin_scr[:, cs]
            dpa_j, dpx_j, dxc_j = [], [], []
            dlam_sum = dba_sum = dbx_sum = None
            for j in range(SUBLANES):
                rs = slice(j * G, (j + 1) * G)
                dh = g0_scr[rs, cs] + (cinv if j == last else pp_scr[rs, cs] * cinv)
                hprev = hext[c, pl.ds(last + j, G, stride=SUBLANES), :]
                xc, r, ig, a, m = xc_ref[rs, cs], r_ref[rs, cs], ig_ref[rs, cs], a_ref[rs, cs], m_ref[rs, cs]
                dixc = dh * m
                dla = (dh * hprev) * a - (dh * (ig * xc)) * ((a * a) * pl.reciprocal(m, approx=True))
                dpa = (dla * big_l[:, cs]) * r * (1.0 - r)
                dpx = (dixc * xc) * ig * (1.0 - ig)
                dpa_j.append(dpa)
                dpx_j.append(dpx)
                dxc_j.append(dixc * ig)
                sums = (_fold8(dla * r), _fold8(dpa), _fold8(dpx))
                dlam_sum, dba_sum, dbx_sum = sums if j == 0 else (dlam_sum + sums[0], dba_sum + sums[1], dbx_sum + sums[2])
            dlam_ref[:, cs] += _colsum(dlam_sum) * dlam_scale[:, cs]
            dba_ref[:, cs] += _colsum(dba_sum)
            dbx_ref[:, cs] += _colsum(dbx_sum)
            dpab = jnp.concatenate(dpa_j, axis=0).astype(BF16)
            dpxb = jnp.concatenate(dpx_j, axis=0).astype(BF16)
            xcb = xc_ref[:, cs].astype(BF16)
            dxc = (jnp.concatenate(dxc_j, axis=0)
                   + lax.dot_general(dpab, wa_ref[c], nt, preferred_element_type=F32)
                   + lax.dot_general(dpxb, wx_ref[c], nt, preferred_element_type=F32))
            dwa_ref[c] += lax.dot_general(xcb, dpab, tn, preferred_element_type=F32)
            dwx_ref[c] += lax.dot_general(xcb, dpxb, tn, preferred_element_type=F32)

            dbc_ref[:, cs] += _colsum(dxc)
            xs = {st: xext[c, pl.ds(st, G, stride=SUBLANES), :] for st in range(SUBLANES - 3, 2 * SUBLANES)}
            for k in range(4):
                tot = None
                for j in range(SUBLANES):
                    part = _fold8(dxc[j * G:(j + 1) * G] * xs[SUBLANES + j - (3 - k)])
                    tot = part if tot is None else tot + part
                dwc_ref[k:k + 1, cs] += _colsum(tot)
            for j in range(SUBLANES):
                dxext[c, pl.ds(j, G, stride=SUBLANES), :] = dxc[j * G:(j + 1) * G]
            us = {st: dxext[c, pl.ds(st, G, stride=SUBLANES), :] for st in range(SUBLANES + 3)}
            wcs = [wc_ref[k:k + 1, cs] for k in range(4)]
            for j in range(SUBLANES):
                acc = us[j] * wcs[3]
                for k in (1, 2, 3):
                    acc = acc + us[j + k] * wcs[3 - k]
                dnat[c, pl.ds(j, G, stride=SUBLANES), :] = acc
            dxl = dnat[c]
            dp_ref[:, cs] = dxl.astype(BF16)
            dbin_ref[:, cs] += _colsum(dxl)

        gu, dgu_dx = _gelu_and_grad(p_ref[:, 2 * lw:2 * lw + sw].astype(F32))
        gv, dgv_dx = _gelu_and_grad(p_ref[:, 2 * lw + sw:cw].astype(F32))
        xhat, rstd = _ln_stats(gv)
        vn = (xhat * lg_ref[...] + lb_ref[...]).astype(BF16)
        dys = dys_ref[...].astype(F32)
        dmixed = dys * gu
        dmb = dmixed.astype(BF16)
        tpos = lax.broadcasted_iota(jnp.int32, (SGU_BLOCK, SGU_BLOCK), 0) // CHUNK
        spos = lax.broadcasted_iota(jnp.int32, (SGU_BLOCK, SGU_BLOCK), 1) // CHUNK
        causal = spos <= tpos
        mixed_rows, dvn_rows = [], []
        for blk in range(nblk):
            rs = slice(blk * SGU_BLOCK, (blk + 1) * SGU_BLOCK)
            mcols, dcols = [], []
            for g in range(groups):
                cs = slice(g * gw, (g + 1) * gw)
                wm = jnp.where(causal, wsp_ref[g], 0.0).astype(BF16)
                mcols.append(jnp.dot(wm, vn[rs, cs], preferred_element_type=F32) + bsp_ref[:, g:g + 1])
                dcols.append(lax.dot_general(wm, dmb[rs, cs], tn, preferred_element_type=F32))
                dw = lax.dot_general(dmb[rs, cs], vn[rs, cs], nt, preferred_element_type=F32)
                dwsp_ref[g] += jnp.where(causal, dw, 0.0)
                dbsp_ref[:, g:g + 1] += jnp.sum(dmixed[rs, cs], axis=1, keepdims=True)
            mixed_rows.append(jnp.concatenate(mcols, axis=1))
            dvn_rows.append(jnp.concatenate(dcols, axis=1))
        mixed_all = jnp.concatenate(mixed_rows, axis=0) if nblk > 1 else mixed_rows[0]
        dvn = jnp.concatenate(dvn_rows, axis=0) if nblk > 1 else dvn_rows[0]
        du = dys * mixed_all * dgu_dx
        dlg_ref[...] += _colsum(dvn * xhat)
        dlb_ref[...] += _colsum(dvn)
        dv = _ln_bwd(dvn, xhat, rstd, lg_ref[...]) * dgv_dx
        dp_ref[:, 2 * lw:2 * lw + sw] = du.astype(BF16)
        dp_ref[:, 2 * lw + sw:cw] = dv.astype(BF16)
        dbin_ref[:, 2 * lw:2 * lw + sw] += _colsum(du)
        dbin_ref[:, 2 * lw + sw:cw] += _colsum(dv)

    rev = lambda s: n_s - 1 - s
    tile = lambda w: pl.BlockSpec((None, tm, w), lambda b, s: (b, rev(s), 0))
    halo = lambda w: pl.BlockSpec((None, SUBLANES, w), lambda b, s: (b, jnp.maximum(rev(s) * per8 - 1, 0), 0))
    xhalo = pl.BlockSpec((None, halo_rows, lw), lambda b, s: (b, jnp.maximum(rev(s) * (tm // halo_rows) - 1, 0), 0))
    full = lambda shp: pl.BlockSpec(shp, lambda b, s: (0,) * len(shp))
    small = [w_conv, b_conv, w_rg_a, b_rg_a, w_rg_x, b_rg_x, lam, w_sp, b_sp_t, ln_v_g, ln_v_b]
    acc_shapes = [(1, cw), w_conv.shape, b_conv.shape, w_rg_a.shape, b_rg_a.shape, w_rg_x.shape, b_rg_x.shape,
                  lam.shape, w_sp.shape, b_sp_t.shape, ln_v_g.shape, ln_v_b.shape]
    res = pl.pallas_call(
        body, name="mix_bwd", grid=(Bl, n_s),
        in_specs=[tile(cw), xhalo, tile(lw), halo(lw), tile(lw), tile(sw), pl.BlockSpec(memory_space=pl.ANY)]
                 + [tile(lw)] * 5 + [full(w.shape) for w in small],
        out_specs=tuple([tile(cw)] + [full(shp) for shp in acc_shapes]),
        out_shape=tuple([jax.ShapeDtypeStruct((Bl, S, din), BF16)] + [jax.ShapeDtypeStruct(shp, F32) for shp in acc_shapes]),
        input_output_aliases={6: 0},
        scratch_shapes=[pltpu.VMEM((nc, tm + SUBLANES, LANES), F32), pltpu.VMEM((nc, tm + SUBLANES, LANES), F32),
                        pltpu.VMEM((nc, tm, LANES), F32), pltpu.VMEM((nc, tm + SUBLANES, LANES), F32),
                        pltpu.VMEM((SUBLANES, lw), F32), pltpu.VMEM((G, lw), F32), pltpu.VMEM((G, lw), F32),
                        pltpu.VMEM((G, lw), F32), pltpu.VMEM((G, lw), F32), pltpu.VMEM((tm, lw), F32),
                        pltpu.VMEM((tm, lw), F32)],
        compiler_params=_cparams(2, big=True),
    )(proj, proj, hs, hs, dya, dys, dproj, *saved, *small)
    return res


def _ada_fwd(c_all, w_ada):
    R, D = c_all.shape
    nb = w_ada.shape[1]

    def body(c_ref, w_ref, act_ref, o_ref):
        cv = c_ref[...]
        act = (cv * _sigmoid(cv)).astype(BF16)
        act_ref[...] = act
        o_ref[...] = jnp.dot(act, w_ref[...].astype(BF16), preferred_element_type=F32)

    return pl.pallas_call(
        body, name="ada_fwd",
        out_shape=(jax.ShapeDtypeStruct((R, D), BF16), jax.ShapeDtypeStruct((R, nb), F32)),
        compiler_params=pltpu.CompilerParams(vmem_limit_bytes=VMEM_LIMIT),
    )(c_all, w_ada)


def _ada_bwd(c_act, dmod_cols):
    R, D = c_act.shape
    nb = dmod_cols.shape[1]

    def body(act_ref, d_ref, o_ref, b_ref):
        o_ref[...] = lax.dot_general(act_ref[...], d_ref[...].astype(BF16), (((0,), (0,)), ((), ())),
                                     preferred_element_type=F32)
        b_ref[...] = _colsum(d_ref[...])

    return pl.pallas_call(
        body, name="ada_bwd", out_shape=(jax.ShapeDtypeStruct((D, nb), F32), jax.ShapeDtypeStruct((1, nb), F32)),
        compiler_params=pltpu.CompilerParams(vmem_limit_bytes=VMEM_LIMIT),
    )(c_act, dmod_cols)


def _adamw(w, g_slots, m, v, *, tr, name, own=None):
    R, C = w.shape
    n_slot = g_slots.shape[0]
    tr = min(tr, R)
    assert R % tr == 0, (name, R, tr)
    c1 = 1.0 / (1.0 - ADAM_B1 ** ADAM_STEP)
    c2 = 1.0 / (1.0 - ADAM_B2 ** ADAM_STEP)
    n_own = 0 if own is None else 1

    def body(me_ref, w_ref, g_ref, *refs):
        m_ref, v_ref, go_ref, d_ref, mo_ref, vo_ref = refs[n_own:]
        slot = lambda d: (jnp.where(me_ref[0] == d, refs[0][...], g_ref[d]) if n_own else g_ref[d]).astype(F32)
        g = slot(0)
        for d in range(1, n_slot):
            g = g + slot(d)
        mn = ADAM_B1 * m_ref[...] + (1.0 - ADAM_B1) * g
        vn = ADAM_B2 * v_ref[...] + (1.0 - ADAM_B2) * (g * g)
        go_ref[...] = g
        mo_ref[...] = mn
        vo_ref[...] = vn
        d_ref[...] = -ADAM_LR * ((mn * c1) / (jnp.sqrt(vn * c2) + ADAM_EPS) + ADAM_WD * w_ref[...])

    me = 4 * lax.axis_index("x") + 2 * lax.axis_index("y") + lax.axis_index("c")
    blk = pl.BlockSpec((tr, C), lambda i, me_ref: (i, 0))
    own_specs = [pl.BlockSpec((None, tr, C), lambda i, me_ref: (me_ref[0], i, 0))] * n_own
    return pl.pallas_call(
        body, name=name, out_shape=tuple(jax.ShapeDtypeStruct((R, C), F32) for _ in range(4)),
        grid_spec=pltpu.PrefetchScalarGridSpec(
            num_scalar_prefetch=1, grid=(R // tr,),
            in_specs=[blk, pl.BlockSpec((n_slot, tr, C), lambda i, me_ref: (0, i, 0))] + own_specs + [blk, blk],
            out_specs=(blk, blk, blk, blk)),
        compiler_params=_cparams(1, big=True),
    )(jnp.reshape(me, (1,)).astype(jnp.int32), w, g_slots, *([own] if n_own else []), m, v)


def _adamw_many(ws, g_slots, g_owns, ms, vs, *, name):
    n = len(ws)
    c1 = 1.0 / (1.0 - ADAM_B1 ** ADAM_STEP)
    c2 = 1.0 / (1.0 - ADAM_B2 ** ADAM_STEP)

    def body(*refs):
        w_refs, g_refs, o_refs = refs[:n], refs[n:2 * n], refs[2 * n:3 * n]
        m_refs, v_refs = refs[3 * n:4 * n], refs[4 * n:5 * n]
        outs = refs[5 * n:]
        me = 4 * lax.axis_index("x") + 2 * lax.axis_index("y") + lax.axis_index("c")
        for i in range(n):
            own = o_refs[i][...]
            g = jnp.where(me == 0, own, g_refs[i][0])
            for d in range(1, N_DEV):
                g = g + jnp.where(me == d, own, g_refs[i][d])
            mn = ADAM_B1 * m_refs[i][...] + (1.0 - ADAM_B1) * g
            vn = ADAM_B2 * v_refs[i][...] + (1.0 - ADAM_B2) * (g * g)
            outs[i][...] = g
            outs[n + i][...] = -ADAM_LR * ((mn * c1) / (jnp.sqrt(vn * c2) + ADAM_EPS) + ADAM_WD * w_refs[i][...])
            outs[2 * n + i][...] = mn
            outs[3 * n + i][...] = vn

    res = pl.pallas_call(
        body, name=name, out_shape=tuple(jax.ShapeDtypeStruct(w.shape, F32) for _ in range(4) for w in ws),
        compiler_params=pltpu.CompilerParams(vmem_limit_bytes=VMEM_LIMIT),
    )(*ws, *g_slots, *g_owns, *ms, *vs)
    return res[:n], res[n:2 * n], res[2 * n:3 * n], res[3 * n:]


SMALL_NAMES = ("b_ada", "b_in", "b_conv", "w_rg_a", "b_rg_a", "w_rg_x", "b_rg_x", "lru_lambda", "w_sp", "b_sp",
               "ln_v_g", "ln_v_b", "ln1_g", "ln1_b", "ln2_g", "ln2_b")
WEIGHT_ORDER = ("w_ada", "b_ada", "w_in", "b_in", "w_conv", "b_conv", "w_rg_a", "b_rg_a", "w_rg_x", "b_rg_x",
                "lru_lambda", "w_sp", "b_sp", "ln_v_g", "ln_v_b", "w_o_lru", "w_o_sgu", "w_out", "ln1_g", "ln1_b",
                "w_up", "w_down", "ln2_g", "ln2_b")


def _blocked_cols(w2d):
    K, N = w2d.shape
    return jnp.transpose(w2d.reshape(K, N_DEV, N // N_DEV), (1, 0, 2))


def _unblock_cols(wb):
    n, K, nb = wb.shape
    return jnp.transpose(wb, (1, 0, 2)).reshape(K, n * nb)


def kernel(x, c, w_ada, b_ada, w_in, b_in, w_conv, b_conv, w_rg_a, b_rg_a, w_rg_x, b_rg_x, lru_lambda, w_sp, b_sp, ln_v_g, ln_v_b, w_o_lru, w_o_sgu, w_out, ln1_g, ln1_b, w_up, w_down, ln2_g, ln2_b, loss_target, m_w_ada, m_b_ada, m_w_in, m_b_in, m_w_conv, m_b_conv, m_w_rg_a, m_b_rg_a, m_w_rg_x, m_b_rg_x, m_lru_lambda, m_w_sp, m_b_sp, m_ln_v_g, m_ln_v_b, m_w_o_lru, m_w_o_sgu, m_w_out, m_ln1_g, m_ln1_b, m_w_up, m_w_down, m_ln2_g, m_ln2_b, v_w_ada, v_b_ada, v_w_in, v_b_in, v_w_conv, v_b_conv, v_w_rg_a, v_b_rg_a, v_w_rg_x, v_b_rg_x, v_lru_lambda, v_w_sp, v_b_sp, v_ln_v_g, v_ln_v_b, v_w_o_lru, v_w_o_sgu, v_w_out, v_ln1_g, v_ln1_b, v_w_up, v_w_down, v_ln2_g, v_ln2_b):
    W = dict(w_ada=w_ada, b_ada=b_ada, w_in=w_in, b_in=b_in, w_conv=w_conv, b_conv=b_conv, w_rg_a=w_rg_a,
             b_rg_a=b_rg_a, w_rg_x=w_rg_x, b_rg_x=b_rg_x, lru_lambda=lru_lambda, w_sp=w_sp, b_sp=b_sp,
             ln_v_g=ln_v_g, ln_v_b=ln_v_b, w_o_lru=w_o_lru, w_o_sgu=w_o_sgu, w_out=w_out, ln1_g=ln1_g, ln1_b=ln1_b,
             w_up=w_up, w_down=w_down, ln2_g=ln2_g, ln2_b=ln2_b)
    Mo = dict(w_ada=m_w_ada, b_ada=m_b_ada, w_in=m_w_in, b_in=m_b_in, w_conv=m_w_conv, b_conv=m_b_conv,
              w_rg_a=m_w_rg_a, b_rg_a=m_b_rg_a, w_rg_x=m_w_rg_x, b_rg_x=m_b_rg_x, lru_lambda=m_lru_lambda,
              w_sp=m_w_sp, b_sp=m_b_sp, ln_v_g=m_ln_v_g, ln_v_b=m_ln_v_b, w_o_lru=m_w_o_lru, w_o_sgu=m_w_o_sgu,
              w_out=m_w_out, ln1_g=m_ln1_g, ln1_b=m_ln1_b, w_up=m_w_up, w_down=m_w_down, ln2_g=m_ln2_g,
              ln2_b=m_ln2_b)
    Vo = dict(w_ada=v_w_ada, b_ada=v_b_ada, w_in=v_w_in, b_in=v_b_in, w_conv=v_w_conv, b_conv=v_b_conv,
              w_rg_a=v_w_rg_a, b_rg_a=v_b_rg_a, w_rg_x=v_w_rg_x, b_rg_x=v_b_rg_x, lru_lambda=v_lru_lambda,
              w_sp=v_w_sp, b_sp=v_b_sp, ln_v_g=v_ln_v_g, ln_v_b=v_ln_v_b, w_o_lru=v_w_o_lru, w_o_sgu=v_w_o_sgu,
              w_out=v_w_out, ln1_g=v_ln1_g, ln1_b=v_ln1_b, w_up=v_w_up, w_down=v_w_down, ln2_g=v_ln2_g,
              ln2_b=v_ln2_b)

    Bl, S, D = x.shape
    T = Bl * S
    lw = b_conv.shape[-1]
    sw = ln_v_g.shape[-1]
    din = b_in.shape[-1]
    dff = w_up.shape[-1] * N_DEV
    ts = min(2048, S)
    tmix = min(256, S)
    trow = min(512, S)

    c_pad = jnp.pad(c, ((0, SUBLANES - Bl), (0, 0)))
    c_g, wconv_g = _exchange([c_pad, w_conv[0]], True, "xchg_c")
    wconv_full = _unblock_cols(wconv_g)
    c_act, modcols = _ada_fwd(c_g.reshape(N_DEV * SUBLANES, D), w_ada[0])
    (mod_slots,) = _exchange([modcols.reshape(N_DEV, SUBLANES, -1)], False, "xchg_mod")

    nbw = din // N_DEV // WIN_PARTS
    wnames = tuple("win%d" % q for q in range(WIN_PARTS)) + ("wol", "wos", "wout", "wup", "wdown")
    shards = [w_in[0][:, q * nbw:(q + 1) * nbw].astype(BF16) for q in range(WIN_PARTS)] + [
        w_o_lru[0].astype(BF16), w_o_sgu[0].astype(BF16), w_out[0].astype(BF16), w_up[0].astype(BF16),
        w_down[0].astype(BF16)]
    col_sharded = [True] * WIN_PARTS + [False, True, False, True, False]
    g_send, g_recv, g_src, g_land, g_tok = _xstart(shards, True, mod_slots, "gather_start", cols=col_sharded)
    gidx = {n: i for i, n in enumerate(wnames)}

    def gathered(n, after):
        i = gidx[n]
        return _xwait(g_src[i], g_land[i], g_send[i], g_recv[i], after, True, "gather_wait_" + n, col=col_sharded[i])

    mod = _unblock_cols(mod_slots)[:Bl] + (b_ada + g_tok[0, 0])
    sh1, sc1, gt1, sh2, sc2, gt2 = [mod[:, i * D:(i + 1) * D].reshape(Bl, 1, D) for i in range(6)]

    wa_b, wx_b = w_rg_a[0].astype(BF16), w_rg_x[0].astype(BF16)
    b_sp_t = jnp.transpose(b_sp[0])
    small_mix = (wconv_full, b_conv, wa_b, b_rg_a, wx_b, b_rg_x, lru_lambda, w_sp[0], b_sp_t, ln_v_g, ln_v_b)

    h = _modulate(x, sc1, sh1, ts)
    proj, win_parts = None, []
    for q in range(WIN_PARTS):
        wq = gathered("win%d" % q, h if q == 0 else proj)
        win_parts.append(wq)
        proj = _mm(h.reshape(T, D), wq, mode="nn", tm=8192, tn=nbw, tk=D, outs=[BF16], extras=[(b_in, "row")],
                   epilogue=lambda acc, ex: (acc + ex[0],), scatter=(WIN_PARTS, q, din), into=proj,
                   name="mm_proj%d" % q)
    proj3 = proj.reshape(Bl, S, din)
    hs, ya_pre, ysgu, *lru_saved = _mix_fwd(proj3, *small_mix, tm=tmix, lw=lw, sw=sw)
    Wol = gathered("wol", ya_pre).reshape(lw, D)
    Wos = gathered("wos", ysgu)
    x2d, tgt2d = x.reshape(T, D), loss_target.reshape(T, D)
    gate_cb = (din - 2 * D) // D

    def ep_merge(y_b, v):
        ya = v[0].astype(BF16).astype(F32)
        yb = y_b.astype(BF16).astype(F32)
        ga, gb = v[1].astype(F32), v[2].astype(F32)
        return [ya, yb, _sigmoid(ga) * ya + _sigmoid(gb) * yb]

    y_a, y_b, merged = _mm_rows(ysgu.reshape(T, sw), Wos, mode="nn", tm=trow, seq=S, second=(ya_pre.reshape(T, lw), Wol),
                                ins=[("tilecol", proj, D, gate_cb), ("tilecol", proj, D, gate_cb + 1)],
                                outs=[("tile", BF16, D), ("tile", BF16, D), ("tile", BF16, D)], epilogue=ep_merge,
                                name="mm_yab_merge")
    Wout = gathered("wout", merged).reshape(D, D)

    def ep_ln1(mix_acc, v):
        x_, gt, g, b, sc, sh = v
        mixr = mix_acc.astype(BF16).astype(F32)
        xhat, rstd = _ln_stats(ALPHA * x_ + (1.0 + gt) * mixr)
        x1_ = xhat * g + b
        return [mixr, x1_, x1_ * (1.0 + sc) + sh, xhat, jnp.broadcast_to(rstd, (rstd.shape[0], LANES))]

    mix, x1, h2, xhat1, rstd1 = _mm_rows(
        merged, Wout, mode="nn", tm=trow, seq=S,
        ins=[("tile", x2d), ("brow", gt1), ("row", ln1_g), ("row", ln1_b), ("brow", sc2), ("brow", sh2)],
        outs=[("tile", BF16, D), ("tile", F32, D), ("tile", BF16, D), ("tile", BF16, D), ("tile", F32, LANES)],
        epilogue=ep_ln1, name="mm_mix_ln1")
    Wup = gathered("wup", h2)
    relu_up = _mm(h2, Wup, mode="nn", tm=2048, tn=1024, tk=D, outs=[BF16],
                  epilogue=lambda acc, ex: (jnp.maximum(acc, 0.0),), name="mm_up")
    square = lambda t: t * t
    Wdown = gathered("wdown", relu_up).reshape(dff, D)

    def ep_ln2(f_acc, v):
        x1_, t_, gt, g, b = v
        xhat, rstd = _ln_stats(ALPHA * x1_ + (1.0 + gt) * f_acc)
        err = xhat * g + b - t_
        loss_t = 0.5 * jnp.sum(jnp.mean(err * err, axis=-1, keepdims=True))
        dy = err * (1.0 / D)
        dz = _ln_bwd(dy, xhat, rstd, g)
        return [dz * (2.0 * (1.0 + gt)), ALPHA * dz, _colsum(dz * f_acc), _colsum(dy * xhat), _colsum(dy), loss_t]

    df2x, dx1p, dgt2, dg2, db2, loss_part = _mm_rows(
        relu_up, Wdown, mode="nn", tm=trow, seq=S, a_fn=square,
        ins=[("tile", x1), ("tile", tgt2d), ("brow", gt2), ("row", ln2_g), ("row", ln2_b)],
        outs=[("tile", BF16, D), ("tile", F32, D), ("acc_brow", D), ("acc_row", D), ("acc_row", D), ("acc_scalar",)],
        epilogue=ep_ln2, name="mm_down_ln2")
    loss = lax.psum(loss_part[0, 0], ("x", "y", "c"))

    def send_grads(parts, name):
        snd, rcv, src, land, tok = _xstart(parts, False, None, name + "_start")
        return [(src[i], land[i], snd[i], rcv[i]) for i in range(len(parts))], tok

    dup = _mm(df2x, Wdown, mode="nt", tm=2048, tn=1024, tk=D, outs=[BF16], extras=[(relu_up, "tile")],
              epilogue=lambda acc, ex: (acc * ex[0].astype(F32),), name="mm_dup")
    g_wdown = _mm(relu_up, df2x, mode="tn", tm=1024, tn=D, tk=4096, outs=[BF16], a_fn=square,
                  epilogue=lambda acc, ex: (0.5 * acc,), name="mm_gwdown")
    (x_wdown,), tok = send_grads([g_wdown.reshape(N_DEV, dff // N_DEV, D)], "gx_wdown")
    def ep_ln1_bwd(dh2, v):
        dx1p_, x1_, xh_, rs_, mix_, sc, gt, g = v
        mixv = mix_.astype(F32)
        dx1 = dx1p_ + dh2 * (1.0 + sc)
        xhat, rstd = xh_.astype(F32), rs_[:, 0:1]
        dz = _ln_bwd(dx1, xhat, rstd, g)
        return [ALPHA * dz, dz * (1.0 + gt), _colsum(dh2 * x1_), _colsum(dh2), _colsum(dz * mixv),
                _colsum(dx1 * xhat), _colsum(dx1)]

    dxp, dmix, dsc2, dsh2, dgt1, dg1, db1 = _mm_rows(
        dup, Wup, mode="nt", tm=trow, seq=S, tok=tok,
        ins=[("tile", dx1p), ("tile", x1), ("tile", xhat1), ("tile", rstd1), ("tile", mix), ("brow", sc2), ("brow", gt1),
             ("row", ln1_g)],
        outs=[("tile", F32, D), ("tile", BF16, D), ("acc_brow", D), ("acc_brow", D), ("acc_brow", D), ("acc_row", D),
              ("acc_row", D)],
        epilogue=ep_ln1_bwd, name="mm_dh2_ln1b")
    g_wup = _mm(h2, dup, mode="tn", tm=D, tn=1024, tk=4096, outs=[BF16], nb=dff // N_DEV, name="mm_gwup")
    (x_wup,), tok = send_grads([g_wup], "gx_wup")

    def ep_merge_bwd(dm, v):
        ya, yb, ga, gb = [t.astype(F32) for t in v]
        sa, sb = _sigmoid(ga), _sigmoid(gb)
        dga, dgb = dm * ya * sa * (1.0 - sa), dm * yb * sb * (1.0 - sb)
        return [dm * sa, dm * sb, (dga, dgb), jnp.concatenate([_colsum(dga), _colsum(dgb)], axis=1)]

    dy_a, dy_b, dproj, dbin_hi = _mm_rows(
        dmix, Wout, mode="nt", tm=trow, seq=S, tok=tok,
        ins=[("tile", y_a), ("tile", y_b), ("tilecol", proj, D, gate_cb), ("tilecol", proj, D, gate_cb + 1)],
        outs=[("tile", BF16, D), ("tile", BF16, D), ("tilecol", BF16, 2 * D, gate_cb // 2, din), ("acc_row", 2 * D)],
        epilogue=ep_merge_bwd, name="mm_dmerged_mb")
    g_wout = _mm(merged, dmix, mode="tn", tm=D, tn=D, tk=4096, outs=[BF16], name="mm_gwout")
    (x_wout,), tok = send_grads([g_wout.reshape(N_DEV, D // N_DEV, D)], "gx_wout")
    dya_pre = _mm(dy_a, Wol, mode="nt", tm=2048, tn=lw, tk=D, outs=[BF16], tok=tok, name="mm_dya")
    dysgu = _mm(dy_b, Wos, mode="nt", tm=2048, tn=sw, tk=D, outs=[BF16], name="mm_dys")
    g_wol = _mm(ya_pre.reshape(T, lw), dy_a, mode="tn", tm=lw, tn=D, tk=2048, outs=[BF16], name="mm_gwol")
    g_wos = _mm(ysgu.reshape(T, sw), dy_b, mode="tn", tm=sw, tn=D, tk=2048, outs=[BF16], nb=D // N_DEV,
                name="mm_gwos")
    (x_wol, x_wos), tok = send_grads([g_wol.reshape(N_DEV, lw // N_DEV, D), g_wos], "gx_wo")
    small_mix_b = (wconv_full, b_conv + tok[0, 0]) + small_mix[2:]
    (dproj, dbin_lo, g_wconv, g_bconv, g_wa, g_ba, g_wx, g_bx, g_lam, g_wsp, g_bsp_t, g_lvg, g_lvb) = _mix_bwd(
        proj3, hs, dya_pre.reshape(Bl, S, lw), dysgu.reshape(Bl, S, sw), dproj.reshape(Bl, S, din), lru_saved,
        *small_mix_b, tm=tmix, lw=lw, sw=sw)
    dproj2 = dproj.reshape(T, din)
    small_names = [n for n in SMALL_NAMES if n != "b_ada"]
    small_g = dict(b_in=jnp.concatenate([dbin_lo, dbin_hi], axis=-1), b_conv=g_bconv, w_rg_a=g_wa[None], b_rg_a=g_ba,
                   w_rg_x=g_wx[None], b_rg_x=g_bx, lru_lambda=g_lam, w_sp=g_wsp[None],
                   b_sp=jnp.transpose(g_bsp_t)[None], ln_v_g=g_lvg, ln_v_b=g_lvb, ln1_g=dg1, ln1_b=db1, ln2_g=dg2,
                   ln2_b=db2)
    gs_snd, gs_rcv, gs_src, gs_land, tok_s = _xstart([small_g[n] for n in small_names], True, None, "gsmall_start")
    g_win = _mm(h.reshape(T, D), dproj2, mode="tn", tm=D, tn=din // 4, tk=2048, outs=[BF16], nb=din // N_DEV,
                tok=tok_s, name="mm_gwin")
    (x_win,), tok = send_grads([g_win], "gx_win")

    def ep_final(dh, v):
        dxp_, x_, sc = v
        return [dxp_ + dh * (1.0 + sc), _colsum(dh * x_), _colsum(dh)]

    grad_x, dsc1, dsh1 = _mm_rows(dproj2, win_parts, mode="nt", tm=trow, seq=S, tok=tok,
                                  ins=[("tile", dxp), ("tile", x2d), ("brow", sc1)],
                                  outs=[("tile", F32, D), ("acc_brow", D), ("acc_brow", D)], epilogue=ep_final,
                                  name="mm_dh_final")
    grad_x = grad_x.reshape(Bl, S, D)

    out_g, out_d, out_m, out_v = {}, {}, {}, {}

    def adam(name, g_slots, tr, own=None):
        shp = W[name].shape
        w2, m2, v2 = [t.reshape(g_slots.shape[1:]) for t in (W[name], Mo[name], Vo[name])]
        g, d, mn, vn = _adamw(w2, g_slots, m2, v2, tr=tr, name="adam_" + name, own=own)
        out_g[name], out_d[name], out_m[name], out_v[name] = [t.reshape(shp) for t in (g, d, mn, vn)]

    def adam_exchanged(name, handle, tr, after):
        own, slots = _xwait(*handle, after, False, "gx_%s_wait" % name, place=False)
        adam(name, slots, tr, own=own)

    adam_exchanged("w_down", x_wdown, 256, dsh1)
    adam_exchanged("w_up", x_wup, 512, dsh1)
    adam_exchanged("w_out", x_wout, 128, dsh1)
    adam_exchanged("w_o_lru", x_wol, 160, dsh1)
    adam_exchanged("w_o_sgu", x_wos, 256, dsh1)
    gs_own, gs_slots = _xwait_many(gs_src, gs_land, gs_snd, gs_rcv, dsh1, "gsmall_wait")
    res_small = _adamw_many([W[n] for n in small_names], gs_slots, gs_own, [Mo[n] for n in small_names],
                            [Vo[n] for n in small_names], name="adam_small")
    for dst, vals in zip((out_g, out_d, out_m, out_v), res_small):
        dst.update(dict(zip(small_names, vals)))

    dmod = jnp.concatenate([dsh1, dsc1, dgt1, dsh2, dsc2, dgt2], axis=-1).reshape(Bl, 6 * D)
    dmod_b = _blocked_cols(jnp.pad(dmod, ((0, SUBLANES - Bl), (0, 0))))
    dmod_s, gwconv_s = _exchange([dmod_b, _blocked_cols(g_wconv)], False, "xchg_dmod", after=out_g["ln2_b"])
    g_wada, g_bada_mine = _ada_bwd(c_act, dmod_s.reshape(N_DEV * SUBLANES, -1))
    (g_bada_all,) = _exchange([g_bada_mine], True, "xchg_bada")
    adam("w_ada", g_wada[None], 512)
    adam("b_ada", g_bada_all.reshape(1, 1, 6 * D), 1)
    adam("w_conv", gwconv_s, 8)
    adam_exchanged("w_in", x_win, 512, g_bada_all)

    return (loss, grad_x, *[out_g[n] for n in WEIGHT_ORDER], *[out_d[n] for n in WEIGHT_ORDER],
            *[out_m[n] for n in WEIGHT_ORDER], *[out_v[n] for n in WEIGHT_ORDER])
```

```python
import math

import jax
import jax.numpy as jnp
from jax import lax
from jax.experimental import pallas as pl
from jax.experimental.pallas import tpu as pltpu

N_DEV = 8
LN_EPS = 1e-5
LRU_C = 8.0
CHUNK = 64
SGU_BLOCK = 128
ALPHA = 2.0 ** 0.25
ADAM_LR = 0.001
ADAM_B1 = 0.9
ADAM_B2 = 0.999
ADAM_EPS = 1e-08
ADAM_WD = 0.01
ADAM_STEP = 10
GELU_K0 = math.sqrt(2.0 / math.pi)
GELU_K1 = 0.044715

SUBLANES = 8
LANES = 128
VMEM_LIMIT = 56 * 1024 * 1024
WIN_PARTS = 3

F32 = jnp.float32
BF16 = jnp.bfloat16
MESH = pl.DeviceIdType.MESH


def _cparams(n_axes, big=False):
    return pltpu.CompilerParams(dimension_semantics=("arbitrary",) * n_axes,
                                vmem_limit_bytes=VMEM_LIMIT if big else None)


def _sigmoid(x):
    return 0.5 * jnp.tanh(0.5 * x) + 0.5


def _gelu(x):
    t = jnp.tanh(x * (GELU_K0 + (GELU_K0 * GELU_K1) * (x * x)))
    hx = 0.5 * x
    return hx + hx * t


def _gelu_and_grad(x):
    x2 = x * x
    t = jnp.tanh(x * (GELU_K0 + (GELU_K0 * GELU_K1) * x2))
    hx = 0.5 * x
    g = hx + hx * t
    dg = (0.5 + 0.5 * t) + (hx * (1.0 - t * t)) * (GELU_K0 + (3.0 * GELU_K0 * GELU_K1) * x2)
    return g, dg


def _log1p_pos(e):
    p = e * (1.0 - e * (1.0 / 2.0) + e * e * (1.0 / 3.0) - e * e * e * (1.0 / 4.0))
    return jnp.where(e < 1e-2, p, jnp.log(1.0 + e))


def _ln_stats(z):
    mu = jnp.mean(z, axis=-1, keepdims=True)
    zc = z - mu
    var = jnp.mean(zc * zc, axis=-1, keepdims=True)
    rstd = lax.rsqrt(var + LN_EPS)
    return zc * rstd, rstd


def _ln_bwd(dy, xhat, rstd, g):
    dxh = dy * g
    m1 = jnp.mean(dxh, axis=-1, keepdims=True)
    m2 = jnp.mean(dxh * xhat, axis=-1, keepdims=True)
    return rstd * (dxh - m1 - xhat * m2)


def _colsum(v):
    return jnp.sum(v, axis=0, keepdims=True)


def _fold8(v):
    out = v[0:SUBLANES]
    for i in range(1, v.shape[0] // SUBLANES):
        out = out + v[i * SUBLANES:(i + 1) * SUBLANES]
    return out


def _first_step():
    return jnp.logical_and(pl.program_id(0) == 0, pl.program_id(1) == 0)


def _exchange(arrs, gather, name, after=None):
    n = len(arrs)
    n_peer = N_DEV - 1
    n_after = 0 if after is None else 1

    def body(*refs):
        ins, outs = refs[:n], refs[n + n_after:2 * n + n_after]
        send_sems, recv_sems, loc_sems = refs[2 * n + n_after:]
        x, y, c = lax.axis_index("x"), lax.axis_index("y"), lax.axis_index("c")
        me = 4 * x + 2 * y + c
        started = []
        for a in range(n):
            src_me = ins[a] if gather else ins[a].at[me]
            lc = pltpu.make_async_copy(src_me, outs[a].at[me], loc_sems.at[a])
            lc.start()
            started.append((lc, None))
        for p in range(1, N_DEV):
            px, py, pc = x ^ ((p >> 2) & 1), y ^ ((p >> 1) & 1), c ^ (p & 1)
            peer = 4 * px + 2 * py + pc
            for a in range(n):
                k = a * n_peer + (p - 1)
                src = ins[a] if gather else ins[a].at[peer]
                cp = pltpu.make_async_remote_copy(src_ref=src, dst_ref=outs[a].at[me],
                                                  send_sem=send_sems.at[k], recv_sem=recv_sems.at[k],
                                                  device_id=(px, py, pc), device_id_type=MESH)
                cp.start()
                rc = pltpu.make_async_remote_copy(src_ref=src, dst_ref=outs[a].at[peer],
                                                  send_sem=send_sems.at[k], recv_sem=recv_sems.at[k],
                                                  device_id=(px, py, pc), device_id_type=MESH)
                started.append((cp, rc))
        for cp, rc in started:
            if rc is None:
                cp.wait()
            else:
                cp.wait_send()
                rc.wait_recv()

    hbm = pl.BlockSpec(memory_space=pltpu.HBM)
    out_shape = tuple(
        jax.ShapeDtypeStruct(((N_DEV,) + a.shape) if gather else a.shape, a.dtype) for a in arrs)
    return pl.pallas_call(
        body, name=name, out_shape=out_shape,
        in_specs=[hbm] * n + [pl.BlockSpec(memory_space=pl.ANY)] * n_after, out_specs=tuple([hbm] * n),
        scratch_shapes=[pltpu.SemaphoreType.DMA((n * n_peer,)), pltpu.SemaphoreType.DMA((n * n_peer,)),
                        pltpu.SemaphoreType.DMA((n,))],
        compiler_params=pltpu.CompilerParams(has_side_effects=True),
    )(*arrs, *([after] if n_after else []))


_HBM = pl.BlockSpec(memory_space=pltpu.HBM)
_SEM = pl.BlockSpec(memory_space=pltpu.SEMAPHORE)
_EFFECT = pltpu.SideEffectType.DATAFLOW_SIDE_EFFECTING


def _peer_of(p):
    x, y, c = lax.axis_index("x"), lax.axis_index("y"), lax.axis_index("c")
    px, py, pc = x ^ ((p >> 2) & 1), y ^ ((p >> 1) & 1), c ^ (p & 1)
    return (px, py, pc), 4 * px + 2 * py + pc


def _slot(land_ref, idx, width):
    if width is None:
        return land_ref.at[idx]
    return land_ref.at[:, pl.ds(pl.multiple_of(idx * width, LANES), width)]


def _xstart(srcs, gather, after, name, cols=None):
    n = len(srcs)
    cols = cols or [False] * n
    widths = [t.shape[1] if cols[a] else None for a, t in enumerate(srcs)]
    lands = [lax.empty((t.shape[0], N_DEV * t.shape[1]) if cols[a] else (((N_DEV,) + t.shape) if gather else t.shape),
                       t.dtype) for a, t in enumerate(srcs)]
    n_after = 0 if after is None else 1

    def body(*refs):
        src_refs, land_refs = refs[:n], refs[n:2 * n]
        refs = refs[n_after:]
        send_sems, recv_sems = refs[2 * n:3 * n], refs[3 * n:4 * n]
        token = refs[6 * n]
        me = 4 * lax.axis_index("x") + 2 * lax.axis_index("y") + lax.axis_index("c")
        for a in range(n):
            for p in range(1, N_DEV):
                dev, peer = _peer_of(p)
                pltpu.make_async_remote_copy(
                    src_ref=src_refs[a] if gather else src_refs[a].at[peer], dst_ref=_slot(land_refs[a], me, widths[a]),
                    send_sem=send_sems[a].at[p - 1], recv_sem=recv_sems[a].at[p - 1],
                    device_id=dev, device_id_type=MESH).start()
        token[...] = jnp.zeros_like(token)

    sems = tuple(pltpu.SemaphoreType.DMA((N_DEV - 1,)) for _ in range(2 * n))
    thru = tuple(pltpu.HBM(t.shape, t.dtype) for t in list(srcs) + list(lands))
    res = pl.pallas_call(
        body, name=name,
        out_shape=sems + thru + (jax.ShapeDtypeStruct((SUBLANES, LANES), F32),),
        in_specs=[_HBM] * (2 * n) + [pl.BlockSpec(memory_space=pl.ANY)] * n_after,
        out_specs=tuple([_SEM] * (2 * n) + [_HBM] * (2 * n) + [pl.BlockSpec(memory_space=pltpu.VMEM)]),
        input_output_aliases={i: 2 * n + i for i in range(2 * n)},
        compiler_params=pltpu.CompilerParams(has_side_effects=_EFFECT),
    )(*[pltpu.with_memory_space_constraint(t, pltpu.HBM) for t in list(srcs) + list(lands)],
      *([after] if n_after else []))
    return res[:n], res[n:2 * n], res[2 * n:3 * n], res[3 * n:4 * n], res[4 * n]


def _xwait(src, land, send_sem, recv_sem, after, gather, name, col=False, place=True):
    width = src.shape[1] if col else None

    def body(src_ref, land_ref, send_ref, recv_ref, after_ref, src_dead, land_out):
        del after_ref, src_dead, land_out
        for p in range(1, N_DEV):
            dev, peer = _peer_of(p)
            cp = pltpu.make_async_remote_copy(
                src_ref=src_ref if gather else src_ref.at[peer], dst_ref=_slot(land_ref, peer, width),
                send_sem=send_ref.at[p - 1], recv_sem=recv_ref.at[p - 1], device_id=dev, device_id_type=MESH)
            cp.wait_send()
            cp.wait_recv()

    src_done, landed = pl.pallas_call(
        body, name=name, out_shape=(pltpu.HBM(src.shape, src.dtype), pltpu.HBM(land.shape, land.dtype)),
        in_specs=[_HBM, _HBM, _SEM, _SEM, pl.BlockSpec(memory_space=pl.ANY)], out_specs=(_HBM, _HBM),
        input_output_aliases={0: 0, 1: 1},
        compiler_params=pltpu.CompilerParams(has_side_effects=_EFFECT),
    )(src, land, send_sem, recv_sem, after)
    if not place:
        return src_done, landed
    me = 4 * lax.axis_index("x") + 2 * lax.axis_index("y") + lax.axis_index("c")
    return _place_own(landed, src_done, me, col, gather, name + "_own")


def _place_own(zone, src, me, col, gather, name):
    if col:
        R, C = src.shape
        src_spec = lambda tr: pl.BlockSpec((tr, C), lambda i, me_ref: (i, 0))
        out_spec = lambda tr: pl.BlockSpec((tr, C), lambda i, me_ref: (i, me_ref[0]))
    else:
        R, C = zone.shape[1:]
        src_spec = ((lambda tr: pl.BlockSpec((tr, C), lambda i, me_ref: (i, 0))) if gather else
                    (lambda tr: pl.BlockSpec((None, tr, C), lambda i, me_ref: (me_ref[0], i, 0))))
        out_spec = lambda tr: pl.BlockSpec((None, tr, C), lambda i, me_ref: (me_ref[0], i, 0))
    tr = R if R <= 512 else 256
    assert R % tr == 0, (name, R, tr)

    def body(me_ref, src_ref, zone_ref, out_ref):
        del me_ref, zone_ref
        out_ref[...] = src_ref[...]

    return pl.pallas_call(
        body, name=name, out_shape=jax.ShapeDtypeStruct(zone.shape, zone.dtype),
        grid_spec=pltpu.PrefetchScalarGridSpec(
            num_scalar_prefetch=1, grid=(R // tr,),
            in_specs=[src_spec(tr), pl.BlockSpec(memory_space=pl.ANY)], out_specs=out_spec(tr)),
        input_output_aliases={2: 0},
    )(jnp.reshape(me, (1,)).astype(jnp.int32), src, zone)


def _xwait_many(srcs, lands, send_sems, recv_sems, after, name):
    n = len(srcs)

    def body(*refs):
        src_refs, land_refs = refs[:n], refs[n:2 * n]
        snd, rcv = refs[2 * n:3 * n], refs[3 * n:4 * n]
        for a in range(n):
            for p in range(1, N_DEV):
                dev, peer = _peer_of(p)
                cp = pltpu.make_async_remote_copy(
                    src_ref=src_refs[a], dst_ref=land_refs[a].at[peer], send_sem=snd[a].at[p - 1],
                    recv_sem=rcv[a].at[p - 1], device_id=dev, device_id_type=MESH)
                cp.wait_send()
                cp.wait_recv()

    res = pl.pallas_call(
        body, name=name, out_shape=tuple(pltpu.HBM(t.shape, t.dtype) for t in list(srcs) + list(lands)),
        in_specs=[_HBM] * (2 * n) + [_SEM] * (2 * n) + [pl.BlockSpec(memory_space=pl.ANY)],
        out_specs=tuple([_HBM] * (2 * n)), input_output_aliases={i: i for i in range(2 * n)},
        compiler_params=pltpu.CompilerParams(has_side_effects=_EFFECT),
    )(*srcs, *lands, *send_sems, *recv_sems, after)
    return res[:n], res[n:]


def _mm(a, b, *, mode, tm, tn, tk, outs, epilogue=None, extras=(), nb=None, tok=None, scatter=None, into=None,
        a_fn=None, name):
    if mode == "nn":
        (M, K), (_, N) = a.shape, b.shape
    elif mode == "nt":
        (M, K), (N, _) = a.shape, b.shape
    else:
        (K, M), (_, N) = a.shape, b.shape
    tm, tn, tk = min(tm, M), min(tn, N), min(tk, K)
    assert M % tm == 0 and N % tn == 0 and K % tk == 0, (name, M, N, K, tm, tn, tk)
    if mode == "nn":
        a_spec = pl.BlockSpec((tm, tk), lambda i, j, k: (i, k))
        b_spec = pl.BlockSpec((tk, tn), lambda i, j, k: (k, j))
        dims = (((1,), (0,)), ((), ()))
    elif mode == "nt":
        a_spec = pl.BlockSpec((tm, tk), lambda i, j, k: (i, k))
        b_spec = pl.BlockSpec((tn, tk), lambda i, j, k: (j, k))
        dims = (((1,), (1,)), ((), ()))
    else:
        a_spec = pl.BlockSpec((tk, tm), lambda i, j, k: (k, i))
        b_spec = pl.BlockSpec((tk, tn), lambda i, j, k: (k, j))
        dims = (((0,), (0,)), ((), ()))
    nk = K // tk
    n_ex, n_out = len(extras), len(outs)
    n_tok = 0 if tok is None else 1
    nbytes = lambda d: jnp.dtype(d).itemsize
    vmem_est = (2 * (tm * tk * nbytes(a.dtype) + tk * tn * nbytes(b.dtype)
                     + sum(tm * tn * nbytes(e.dtype) for e, kind in extras if kind == "tile")
                     + sum(tm * tn * nbytes(d) for d in outs)) + tm * tn * 4)
    assert vmem_est <= VMEM_LIMIT, (name, vmem_est)
    if epilogue is None:
        epilogue = lambda acc, ex: tuple(acc.astype(d) for d in outs)

    n_into = 0 if into is None else 1

    def body(a_ref, b_ref, *refs):
        refs = refs[n_tok:]
        ex_refs, out_refs = refs[:n_ex], refs[n_ex + n_into:n_ex + n_into + n_out]

        def finish(acc):
            res = epilogue(acc, [r[...] for r in ex_refs])
            for o_ref, v in zip(out_refs, res):
                if nb is None:
                    o_ref[...] = v.astype(o_ref.dtype)
                else:
                    for q in range(tn // nb):
                        o_ref[q] = v[:, q * nb:(q + 1) * nb].astype(o_ref.dtype)

        a_tile = a_ref[...] if a_fn is None else a_fn(a_ref[...])
        part = lax.dot_general(a_tile, b_ref[...], dims, preferred_element_type=F32)
        if nk == 1:
            finish(part)
        else:
            acc_ref = refs[n_ex + n_into + n_out]
            k = pl.program_id(2)

            @pl.when(k == 0)
            def _():
                acc_ref[...] = part

            @pl.when(k > 0)
            def _():
                acc_ref[...] += part

            @pl.when(k == nk - 1)
            def _():
                finish(acc_ref[...])

    col = (lambda j: j) if scatter is None else (lambda j: scatter[0] * j + scatter[1])
    ex_specs = [pl.BlockSpec((tm, tn), lambda i, j, k: (i, j)) if kind == "tile"
                else pl.BlockSpec((1, tn), lambda i, j, k: (0, col(j))) for _, kind in extras]
    if nb is not None:
        assert tn % nb == 0, (name, tn, nb)
        o_spec = pl.BlockSpec((tn // nb, tm, nb), lambda i, j, k: (j, i, 0))
        o_shape = (N // nb, M, nb)
    else:
        o_spec = pl.BlockSpec((tm, tn), lambda i, j, k: (i, col(j)))
        o_shape = (M, N if scatter is None else scatter[2])
    assert n_into == 0 or n_out == 1
    res = pl.pallas_call(
        body, name=name, grid=(M // tm, N // tn, nk),
        in_specs=[a_spec, b_spec] + [pl.BlockSpec((SUBLANES, LANES), lambda i, j, k: (0, 0))] * n_tok + ex_specs
                 + [pl.BlockSpec(memory_space=pl.ANY)] * n_into,
        out_specs=tuple([o_spec] * n_out),
        out_shape=tuple(jax.ShapeDtypeStruct(o_shape, d) for d in outs),
        input_output_aliases={2 + n_tok + n_ex: 0} if n_into else {},
        scratch_shapes=[pltpu.VMEM((tm, tn), F32)] if nk > 1 else [],
        compiler_params=_cparams(3, big=True),
    )(a, b, *([tok] if n_tok else []), *[e for e, _ in extras], *([into] if n_into else []))
    return res[0] if n_out == 1 else res


def _mm_rows(a, b, *, mode, tm, seq, ins, outs, epilogue, tok=None, a_fn=None, second=None, name):
    M, K = a.shape
    b_parts = list(b) if isinstance(b, (list, tuple)) else [b]
    n_part = len(b_parts)
    assert n_part == 1 or mode == "nt"
    N = b_parts[0].shape[1] if mode == "nn" else b_parts[0].shape[0]
    tm = min(tm, M)
    assert M % tm == 0 and seq % tm == 0, (name, M, seq, tm)
    tpb = seq // tm
    n_b = M // seq
    dims = (((1,), (0,)), ((), ())) if mode == "nn" else (((1,), (1,)), ((), ()))
    n_tok = 0 if tok is None else 1
    n_in, n_out = len(ins), len(outs)
    n_second = 0 if second is None else 1
    second_specs = ([pl.BlockSpec((tm, second[0].shape[1]), lambda i: (i, 0)),
                     pl.BlockSpec(second[1].shape, lambda i: (0, 0), pipeline_mode=pl.Buffered(1))] if n_second else [])

    in_specs, in_arrs = [], []
    for spec in ins:
        kind, arr = spec[0], spec[1]
        in_arrs.append(arr)
        if kind == "tile":
            in_specs.append(pl.BlockSpec((tm, arr.shape[1]), lambda i: (i, 0)))
        elif kind == "tilecol":
            in_specs.append(pl.BlockSpec((tm, spec[2]), lambda i, cb=spec[3]: (i, cb)))
        elif kind == "row":
            in_specs.append(pl.BlockSpec(arr.shape, lambda i: (0, 0)))
        else:
            in_specs.append(pl.BlockSpec((None, 1, arr.shape[2]), lambda i: (i // tpb, 0, 0)))
    out_specs, out_shapes = [], []
    for spec in outs:
        kind = spec[0]
        if kind == "tile":
            out_specs.append(pl.BlockSpec((tm, spec[2]), lambda i: (i, 0)))
            out_shapes.append(jax.ShapeDtypeStruct((M, spec[2]), spec[1]))
        elif kind == "tilecol":
            out_specs.append(pl.BlockSpec((tm, spec[2]), lambda i, cb=spec[3]: (i, cb)))
            out_shapes.append(jax.ShapeDtypeStruct((M, spec[4]), spec[1]))
        elif kind == "acc_row":
            out_specs.append(pl.BlockSpec((1, spec[1]), lambda i: (0, 0)))
            out_shapes.append(jax.ShapeDtypeStruct((1, spec[1]), F32))
        elif kind == "acc_brow":
            out_specs.append(pl.BlockSpec((None, 1, spec[1]), lambda i: (i // tpb, 0, 0)))
            out_shapes.append(jax.ShapeDtypeStruct((n_b, 1, spec[1]), F32))
        else:
            out_specs.append(pl.BlockSpec((SUBLANES, LANES), lambda i: (0, 0)))
            out_shapes.append(jax.ShapeDtypeStruct((SUBLANES, LANES), F32))

    def body(a_ref, *refs):
        b_refs, refs = refs[:n_part], refs[n_part + n_tok:]
        second_refs, refs = refs[:2 * n_second], refs[2 * n_second:]
        in_refs, out_refs = refs[:n_in], refs[n_in:n_in + n_out]
        i = pl.program_id(0)
        if n_part == 1:
            a_tile = a_ref[...] if a_fn is None else a_fn(a_ref[...])
            prod = lax.dot_general(a_tile, b_refs[0][...], dims, preferred_element_type=F32)
        else:
            w = b_parts[0].shape[1] // N_DEV
            prod = None
            for q in range(n_part):
                a_q = jnp.concatenate([a_ref[:, (n_part * j + q) * w:(n_part * j + q + 1) * w] for j in range(N_DEV)],
                                      axis=1)
                pq = lax.dot_general(a_q, b_refs[q][...], dims, preferred_element_type=F32)
                prod = pq if prod is None else prod + pq
        extra = ([lax.dot_general(second_refs[0][...], second_refs[1][...], dims, preferred_element_type=F32)]
                 if n_second else [])
        vals = epilogue(prod, extra + [r[...] for r in in_refs])
        for spec, o_ref, v in zip(outs, out_refs, vals):
            kind = spec[0]
            if kind in ("tile", "tilecol"):
                off = 0
                for part in (v if isinstance(v, tuple) else (v,)):
                    o_ref[:, off:off + part.shape[1]] = part.astype(o_ref.dtype)
                    off += part.shape[1]
            else:
                first = (i % tpb == 0) if kind == "acc_brow" else (i == 0)

                @pl.when(first)
                def _(o_ref=o_ref, v=v):
                    o_ref[...] = jnp.broadcast_to(v, o_ref.shape)

                @pl.when(jnp.logical_not(first))
                def _(o_ref=o_ref, v=v):
                    o_ref[...] += v

    res = pl.pallas_call(
        body, name=name, grid=(M // tm,),
        in_specs=[pl.BlockSpec((tm, K), lambda i: (i, 0))]
                 + [pl.BlockSpec(bp.shape, lambda i: (0, 0), pipeline_mode=pl.Buffered(1)) for bp in b_parts]
                 + [pl.BlockSpec((SUBLANES, LANES), lambda i: (0, 0))] * n_tok + second_specs + in_specs,
        out_specs=tuple(out_specs), out_shape=tuple(out_shapes),
        compiler_params=_cparams(1, big=True),
    )(a, *b_parts, *([tok] if n_tok else []), *(second or ()), *in_arrs)
    return res


def _tok_spec(ts, width, col_block=0):
    return pl.BlockSpec((None, ts, width), lambda b, s: (b, s, col_block))


def _brow_spec(width):
    return pl.BlockSpec((None, 1, width), lambda b, s: (b, 0, 0))


def _modulate(x, sc, sh, ts):
    Bl, S, D = x.shape

    def body(x_ref, sc_ref, sh_ref, o_ref):
        o_ref[...] = (x_ref[...] * (1.0 + sc_ref[...]) + sh_ref[...]).astype(BF16)

    return pl.pallas_call(
        body, name="modulate", grid=(Bl, S // ts),
        in_specs=[_tok_spec(ts, D), _brow_spec(D), _brow_spec(D)],
        out_specs=_tok_spec(ts, D), out_shape=jax.ShapeDtypeStruct((Bl, S, D), BF16),
        compiler_params=_cparams(2),
    )(x, sc, sh)


def _mix_fwd(proj, w_conv, b_conv, w_rg_a, b_rg_a, w_rg_x, b_rg_x, lam, w_sp, b_sp_t, ln_v_g, ln_v_b, *, tm, lw, sw):
    Bl, S, _ = proj.shape
    heads, hd = w_rg_a.shape[0], w_rg_a.shape[1]
    groups = w_sp.shape[0]
    cw = 2 * lw + 2 * sw
    nblk = tm // SGU_BLOCK

    G = tm // SUBLANES
    nc = lw // LANES

    def body(p_ref, wc_ref, bc_ref, wa_ref, ba_ref, wx_ref, bx_ref, lam_ref, wsp_ref, bsp_ref, lg_ref, lb_ref,
             hs_ref, ya_ref, ys_ref, xc_ref, r_ref, ig_ref, a_ref, m_ref,
             xext, hnat, hcar, h7_scr, a7_scr, hp_scr, h0_scr, cp_scr):
        s = pl.program_id(1)

        @pl.when(s == 0)
        def _():
            xext[:, 0:SUBLANES, :] = jnp.zeros((nc, SUBLANES, LANES), F32)
            hcar[...] = jnp.zeros_like(hcar)

        @pl.when(s > 0)
        def _():
            xext[:, 0:SUBLANES, :] = xext[:, tm:tm + SUBLANES, :]

        nl = -lam_ref[...]
        big_l = -LRU_C * (jnp.maximum(nl, 0.0) + _log1p_pos(jnp.exp(-jnp.abs(nl))))

        for c in range(nc):
            cs = slice(c * LANES, (c + 1) * LANES)
            xext[c, SUBLANES:SUBLANES + tm, :] = p_ref[:, cs].astype(F32)
            xs = {st: xext[c, pl.ds(st, G, stride=SUBLANES), :] for st in range(SUBLANES - 3, 2 * SUBLANES)}
            wcs = [wc_ref[k:k + 1, cs] for k in range(4)]
            xc_j = []
            for j in range(SUBLANES):
                acc = bc_ref[:, cs] + xs[SUBLANES + j] * wcs[3]
                for k in (1, 2, 3):
                    acc = acc + xs[SUBLANES + j - k] * wcs[3 - k]
                xc_j.append(acc)
                xc_ref[j * G:(j + 1) * G, cs] = acc
            xcb = jnp.concatenate(xc_j, axis=0).astype(BF16)
            pa = jnp.dot(xcb, wa_ref[c], preferred_element_type=F32)
            px = jnp.dot(xcb, wx_ref[c], preferred_element_type=F32)
            h0 = cp = None
            for j in range(SUBLANES):
                rs = slice(j * G, (j + 1) * G)
                r = _sigmoid(pa[rs] + ba_ref[:, cs])
                ig = _sigmoid(px[rs] + bx_ref[:, cs])
                la = big_l[:, cs] * r
                a = jnp.exp(la)
                th = jnp.tanh(la)
                msq = (-2.0 * th) * pl.reciprocal(1.0 - th, approx=True)
                m = msq * lax.rsqrt(jnp.maximum(msq, 1e-30))
                b = m * (ig * xc_j[j])
                r_ref[rs, cs] = r
                ig_ref[rs, cs] = ig
                a_ref[rs, cs] = a
                m_ref[rs, cs] = m
                h0 = b if j == 0 else a * h0 + b
                cp = a if j == 0 else a * cp
                h0_scr[rs, cs] = h0
                cp_scr[rs, cs] = cp
            h7_scr[:, cs] = h0
            a7_scr[:, cs] = cp
        carry = hcar[0:1, :]
        for g in range(G):
            hp_scr[g:g + 1, :] = carry
            carry = h7_scr[g:g + 1, :] + a7_scr[g:g + 1, :] * carry
        hcar[0:1, :] = carry
        for c in range(nc):
            cs = slice(c * LANES, (c + 1) * LANES)
            hprev = hp_scr[:, cs]
            for j in range(SUBLANES):
                rs = slice(j * G, (j + 1) * G)
                hnat[c, pl.ds(j, G, stride=SUBLANES), :] = h0_scr[rs, cs] + cp_scr[rs, cs] * hprev
            hs = hnat[c]
            hs_ref[:, cs] = hs
            ya_ref[:, cs] = (hs * _gelu(p_ref[:, lw + c * LANES:lw + (c + 1) * LANES].astype(F32))).astype(BF16)

        gu = _gelu(p_ref[:, 2 * lw:2 * lw + sw].astype(F32))
        gv = _gelu(p_ref[:, 2 * lw + sw:cw].astype(F32))
        xhat, _ = _ln_stats(gv)
        vn = (xhat * lg_ref[...] + lb_ref[...]).astype(BF16)
        tpos = lax.broadcasted_iota(jnp.int32, (SGU_BLOCK, SGU_BLOCK), 0) // CHUNK
        spos = lax.broadcasted_iota(jnp.int32, (SGU_BLOCK, SGU_BLOCK), 1) // CHUNK
        gw = sw // groups
        rows_out = []
        for blk in range(nblk):
            r0 = blk * SGU_BLOCK
            cols = []
            for g in range(groups):
                wm = jnp.where(spos <= tpos, wsp_ref[g], 0.0).astype(BF16)
                mixed = jnp.dot(wm, vn[r0:r0 + SGU_BLOCK, g * gw:(g + 1) * gw], preferred_element_type=F32)
                cols.append(mixed + bsp_ref[:, g:g + 1])
            rows_out.append(jnp.concatenate(cols, axis=1))
        mixed_all = jnp.concatenate(rows_out, axis=0) if nblk > 1 else rows_out[0]
        ys_ref[...] = (gu * mixed_all).astype(BF16)

    full = lambda shp: pl.BlockSpec(shp, lambda b, s: (0,) * len(shp))
    return pl.pallas_call(
        body, name="mix_fwd", grid=(Bl, S // tm),
        in_specs=[_tok_spec(tm, cw), full(w_conv.shape), full(b_conv.shape), full(w_rg_a.shape), full(b_rg_a.shape),
                  full(w_rg_x.shape), full(b_rg_x.shape), full(lam.shape), full(w_sp.shape), full(b_sp_t.shape),
                  full(ln_v_g.shape), full(ln_v_b.shape)],
        out_specs=(_tok_spec(tm, lw), _tok_spec(tm, lw), _tok_spec(tm, sw)) + (_tok_spec(tm, lw),) * 5,
        out_shape=(jax.ShapeDtypeStruct((Bl, S, lw), F32), jax.ShapeDtypeStruct((Bl, S, lw), BF16),
                   jax.ShapeDtypeStruct((Bl, S, sw), BF16)) + (jax.ShapeDtypeStruct((Bl, S, lw), F32),) * 5,
        scratch_shapes=[pltpu.VMEM((nc, tm + SUBLANES, LANES), F32), pltpu.VMEM((nc, tm, LANES), F32),
                        pltpu.VMEM((SUBLANES, lw), F32), pltpu.VMEM((G, lw), F32), pltpu.VMEM((G, lw), F32),
                        pltpu.VMEM((G, lw), F32), pltpu.VMEM((tm, lw), F32), pltpu.VMEM((tm, lw), F32)],
        compiler_params=_cparams(2, big=True),
    )(proj, w_conv, b_conv, w_rg_a, b_rg_a, w_rg_x, b_rg_x, lam, w_sp, b_sp_t, ln_v_g, ln_v_b)


def _mix_bwd(proj, hs, dya, dys, dproj, saved, w_conv, b_conv, w_rg_a, b_rg_a, w_rg_x, b_rg_x, lam, w_sp, b_sp_t,
             ln_v_g, ln_v_b, *, tm, lw, sw):
    Bl, S, din = proj.shape
    heads, hd = w_rg_a.shape[0], w_rg_a.shape[1]
    groups = w_sp.shape[0]
    gw = sw // groups
    cw = 2 * lw + 2 * sw
    nblk = tm // SGU_BLOCK
    n_s = S // tm
    per8 = tm // SUBLANES
    halo_rows = 2 * SUBLANES

    G = tm // SUBLANES
    nc = lw // LANES

    def body(p_ref, xh_ref, hs_ref, hh_ref, dya_ref, dys_ref, dpin_ref, xc_ref, r_ref, ig_ref, a_ref, m_ref,
             wc_ref, bc_ref, wa_ref, ba_ref, wx_ref, bx_ref, lam_ref, wsp_ref, bsp_ref, lg_ref, lb_ref,
             dp_ref, dbin_ref, dwc_ref, dbc_ref, dwa_ref, dba_ref, dwx_ref, dbx_ref, dlam_ref, dwsp_ref, dbsp_ref,
             dlg_ref, dlb_ref,
             xext, hext, dnat, dxext, dhcar, g00_scr, p0_scr, a0_scr, cin_scr, g0_scr, pp_scr):
        del dpin_ref
        sr = pl.program_id(1)
        first_tile = sr == n_s - 1

        @pl.when(_first_step())
        def _():
            for ref in (dbin_ref, dwc_ref, dbc_ref, dwa_ref, dba_ref, dwx_ref, dbx_ref, dlam_ref, dwsp_ref, dbsp_ref,
                        dlg_ref, dlb_ref):
                ref[...] = jnp.zeros_like(ref)

        @pl.when(sr == 0)
        def _():
            dhcar[...] = jnp.zeros_like(dhcar)
            dxext[:, tm:tm + SUBLANES, :] = jnp.zeros((nc, SUBLANES, LANES), F32)

        @pl.when(sr > 0)
        def _():
            dxext[:, tm:tm + SUBLANES, :] = dxext[:, 0:SUBLANES, :]

        keep = jnp.where(first_tile, 0.0, 1.0)
        xprev = xh_ref[...].astype(F32)[halo_rows - SUBLANES:halo_rows] * keep
        hprev8 = hh_ref[...] * keep
        nl = -lam_ref[...]
        big_l = -LRU_C * (jnp.maximum(nl, 0.0) + _log1p_pos(jnp.exp(-jnp.abs(nl))))
        dlam_scale = LRU_C * _sigmoid(nl)
        nt = (((1,), (1,)), ((), ()))
        tn = (((0,), (0,)), ((), ()))
        last = SUBLANES - 1

        for c in range(nc):
            cs = slice(c * LANES, (c + 1) * LANES)
            gcs = slice(lw + c * LANES, lw + (c + 1) * LANES)
            xext[c, 0:SUBLANES, :] = xprev[:, cs]
            xext[c, SUBLANES:SUBLANES + tm, :] = p_ref[:, cs].astype(F32)
            hext[c, 0:SUBLANES, :] = hprev8[:, cs]
            dgl_sum = None
            for i in range(SUBLANES):
                rs = slice(i * G, (i + 1) * G)
                ggl, dggl = _gelu_and_grad(p_ref[rs, gcs].astype(F32))
                dy = dya_ref[rs, cs].astype(F32)
                hsv = hs_ref[rs, cs]
                hext[c, SUBLANES + i * G:SUBLANES + (i + 1) * G, :] = hsv
                dgl = dy * hsv * dggl
                dp_ref[rs, gcs] = dgl.astype(BF16)
                dnat[c, rs, :] = dy * ggl
                dgl_sum = _fold8(dgl) if i == 0 else dgl_sum + _fold8(dgl)
            dbin_ref[:, gcs] += _colsum(dgl_sum)
            g0 = pp = None
            for j in range(last, -1, -1):
                rs = slice(j * G, (j + 1) * G)
                dhs_j = dnat[c, pl.ds(j, G, stride=SUBLANES), :]
                if j == last:
                    g0 = dhs_j
                else:
                    an = a_ref[(j + 1) * G:(j + 2) * G, cs]
                    g0 = dhs_j + an * g0
                    pp = an if j == last - 1 else an * pp
                    pp_scr[rs, cs] = pp
                g0_scr[rs, cs] = g0
            g00_scr[:, cs] = g0
            p0_scr[:, cs] = pp
            a0_scr[:, cs] = a_ref[0:G, cs]
        cin = dhcar[0:1, :]
        for g in range(G - 1, -1, -1):
            cin_scr[g:g + 1, :] = cin
            cin = a0_scr[g:g + 1, :] * (g00_scr[g:g + 1, :] + p0_scr[g:g + 1, :] * cin)
        dhcar[0:1, :] = cin
        for c in range(nc):
            cs = slice(c * LANES, (c + 1) * LANES)
            cinv = cin_scr[:, cs]
            dpa_j, dpx_j, dxc_j = [], [], []
            dlam_sum = dba_sum = dbx_sum = None
            for j in range(SUBLANES):
                rs = slice(j * G, (j + 1) * G)
                dh = g0_scr[rs, cs] + (cinv if j == last else pp_scr[rs, cs] * cinv)
                hprev = hext[c, pl.ds(last + j, G, stride=SUBLANES), :]
                xc, r, ig, a, m = xc_ref[rs, cs], r_ref[rs, cs], ig_ref[rs, cs], a_ref[rs, cs], m_ref[rs, cs]
                dixc = dh * m
                dla = (dh * hprev) * a - (dh * (ig * xc)) * ((a * a) * pl.reciprocal(m, approx=True))
                dpa = (dla * big_l[:, cs]) * r * (1.0 - r)
                dpx = (dixc * xc) * ig * (1.0 - ig)
                dpa_j.append(dpa)
                dpx_j.append(dpx)
                dxc_j.append(dixc * ig)
                sums = (_fold8(dla * r), _fold8(dpa), _fold8(dpx))
                dlam_sum, dba_sum, dbx_sum = sums if j == 0 else (dlam_sum + sums[0], dba_sum + sums[1], dbx_sum + sums[2])
            dlam_ref[:, cs] += _colsum(dlam_sum) * dlam_scale[:, cs]
            dba_ref[:, cs] += _colsum(dba_sum)
            dbx_ref[:, cs] += _colsum(dbx_sum)
            dpab = jnp.concatenate(dpa_j, axis=0).astype(BF16)
            dpxb = jnp.concatenate(dpx_j, axis=0).astype(BF16)
            xcb = xc_ref[:, cs].astype(BF16)
            dxc = (jnp.concatenate(dxc_j, axis=0)
                   + lax.dot_general(dpab, wa_ref[c], nt, preferred_element_type=F32)
                   + lax.dot_general(dpxb, wx_ref[c], nt, preferred_element_type=F32))
            dwa_ref[c] += lax.dot_general(xcb, dpab, tn, preferred_element_type=F32)
            dwx_ref[c] += lax.dot_general(xcb, dpxb, tn, preferred_element_type=F32)

            dbc_ref[:, cs] += _colsum(dxc)
            xs = {st: xext[c, pl.ds(st, G, stride=SUBLANES), :] for st in range(SUBLANES - 3, 2 * SUBLANES)}
            for k in range(4):
                tot = None
                for j in range(SUBLANES):
                    part = _fold8(dxc[j * G:(j + 1) * G] * xs[SUBLANES + j - (3 - k)])
                    tot = part if tot is None else tot + part
                dwc_ref[k:k + 1, cs] += _colsum(tot)
            for j in range(SUBLANES):
                dxext[c, pl.ds(j, G, stride=SUBLANES), :] = dxc[j * G:(j + 1) * G]
            us = {st: dxext[c, pl.ds(st, G, stride=SUBLANES), :] for st in range(SUBLANES + 3)}
            wcs = [wc_ref[k:k + 1, cs] for k in range(4)]
            for j in range(SUBLANES):
                acc = us[j] * wcs[3]
                for k in (1, 2, 3):
                    acc = acc + us[j + k] * wcs[3 - k]
                dnat[c, pl.ds(j, G, stride=SUBLANES), :] = acc
            dxl = dnat[c]
            dp_ref[:, cs] = dxl.astype(BF16)
            dbin_ref[:, cs] += _colsum(dxl)

        gu, dgu_dx = _gelu_and_grad(p_ref[:, 2 * lw:2 * lw + sw].astype(F32))
        gv, dgv_dx = _gelu_and_grad(p_ref[:, 2 * lw + sw:cw].astype(F32))
        xhat, rstd = _ln_stats(gv)
        vn = (xhat * lg_ref[...] + lb_ref[...]).astype(BF16)
        dys = dys_ref[...].astype(F32)
        dmixed = dys * gu
        dmb = dmixed.astype(BF16)
        tpos = lax.broadcasted_iota(jnp.int32, (SGU_BLOCK, SGU_BLOCK), 0) // CHUNK
        spos = lax.broadcasted_iota(jnp.int32, (SGU_BLOCK, SGU_BLOCK), 1) // CHUNK
        causal = spos <= tpos
        mixed_rows, dvn_rows = [], []
        for blk in range(nblk):
            rs = slice(blk * SGU_BLOCK, (blk + 1) * SGU_BLOCK)
            mcols, dcols = [], []
            for g in range(groups):
                cs = slice(g * gw, (g + 1) * gw)
                wm = jnp.where(causal, wsp_ref[g], 0.0).astype(BF16)
                mcols.append(jnp.dot(wm, vn[rs, cs], preferred_element_type=F32) + bsp_ref[:, g:g + 1])
                dcols.append(lax.dot_general(wm, dmb[rs, cs], tn, preferred_element_type=F32))
                dw = lax.dot_general(dmb[rs, cs], vn[rs, cs], nt, preferred_element_type=F32)
                dwsp_ref[g] += jnp.where(causal, dw, 0.0)
                dbsp_ref[:, g:g + 1] += jnp.sum(dmixed[rs, cs], axis=1, keepdims=True)
            mixed_rows.append(jnp.concatenate(mcols, axis=1))
            dvn_rows.append(jnp.concatenate(dcols, axis=1))
        mixed_all = jnp.concatenate(mixed_rows, axis=0) if nblk > 1 else mixed_rows[0]
        dvn = jnp.concatenate(dvn_rows, axis=0) if nblk > 1 else dvn_rows[0]
        du = dys * mixed_all * dgu_dx
        dlg_ref[...] += _colsum(dvn * xhat)
        dlb_ref[...] += _colsum(dvn)
        dv = _ln_bwd(dvn, xhat, rstd, lg_ref[...]) * dgv_dx
        dp_ref[:, 2 * lw:2 * lw + sw] = du.astype(BF16)
        dp_ref[:, 2 * lw + sw:cw] = dv.astype(BF16)
        dbin_ref[:, 2 * lw:2 * lw + sw] += _colsum(du)
        dbin_ref[:, 2 * lw + sw:cw] += _colsum(dv)

    rev = lambda s: n_s - 1 - s
    tile = lambda w: pl.BlockSpec((None, tm, w), lambda b, s: (b, rev(s), 0))
    halo = lambda w: pl.BlockSpec((None, SUBLANES, w), lambda b, s: (b, jnp.maximum(rev(s) * per8 - 1, 0), 0))
    xhalo = pl.BlockSpec((None, halo_rows, lw), lambda b, s: (b, jnp.maximum(rev(s) * (tm // halo_rows) - 1, 0), 0))
    full = lambda shp: pl.BlockSpec(shp, lambda b, s: (0,) * len(shp))
    small = [w_conv, b_conv, w_rg_a, b_rg_a, w_rg_x, b_rg_x, lam, w_sp, b_sp_t, ln_v_g, ln_v_b]
    acc_shapes = [(1, cw), w_conv.shape, b_conv.shape, w_rg_a.shape, b_rg_a.shape, w_rg_x.shape, b_rg_x.shape,
                  lam.shape, w_sp.shape, b_sp_t.shape, ln_v_g.shape, ln_v_b.shape]
    res = pl.pallas_call(
        body, name="mix_bwd", grid=(Bl, n_s),
        in_specs=[tile(cw), xhalo, tile(lw), halo(lw), tile(lw), tile(sw), pl.BlockSpec(memory_space=pl.ANY)]
                 + [tile(lw)] * 5 + [full(w.shape) for w in small],
        out_specs=tuple([tile(cw)] + [full(shp) for shp in acc_shapes]),
        out_shape=tuple([jax.ShapeDtypeStruct((Bl, S, din), BF16)] + [jax.ShapeDtypeStruct(shp, F32) for shp in acc_shapes]),
        input_output_aliases={6: 0},
        scratch_shapes=[pltpu.VMEM((nc, tm + SUBLANES, LANES), F32), pltpu.VMEM((nc, tm + SUBLANES, LANES), F32),
                        pltpu.VMEM((nc, tm, LANES), F32), pltpu.VMEM((nc, tm + SUBLANES, LANES), F32),
                        pltpu.VMEM((SUBLANES, lw), F32), pltpu.VMEM((G, lw), F32), pltpu.VMEM((G, lw), F32),
                        pltpu.VMEM((G, lw), F32), pltpu.VMEM((G, lw), F32), pltpu.VMEM((tm, lw), F32),
                        pltpu.VMEM((tm, lw), F32)],
        compiler_params=_cparams(2, big=True),
    )(proj, proj, hs, hs, dya, dys, dproj, *saved, *small)
    return res


def _ada_fwd(c_all, w_ada):
    R, D = c_all.shape
    nb = w_ada.shape[1]

    def body(c_ref, w_ref, act_ref, o_ref):
        cv = c_ref[...]
        act = (cv * _sigmoid(cv)).astype(BF16)
        act_ref[...] = act
        o_ref[...] = jnp.dot(act, w_ref[...].astype(BF16), preferred_element_type=F32)

    return pl.pallas_call(
        body, name="ada_fwd",
        out_shape=(jax.ShapeDtypeStruct((R, D), BF16), jax.ShapeDtypeStruct((R, nb), F32)),
        compiler_params=pltpu.CompilerParams(vmem_limit_bytes=VMEM_LIMIT),
    )(c_all, w_ada)


def _ada_bwd(c_act, dmod_cols):
    R, D = c_act.shape
    nb = dmod_cols.shape[1]

    def body(act_ref, d_ref, o_ref, b_ref):
        o_ref[...] = lax.dot_general(act_ref[...], d_ref[...].astype(BF16), (((0,), (0,)), ((), ())),
                                     preferred_element_type=F32)
        b_ref[...] = _colsum(d_ref[...])

    return pl.pallas_call(
        body, name="ada_bwd", out_shape=(jax.ShapeDtypeStruct((D, nb), F32), jax.ShapeDtypeStruct((1, nb), F32)),
        compiler_params=pltpu.CompilerParams(vmem_limit_bytes=VMEM_LIMIT),
    )(c_act, dmod_cols)


def _adamw(w, g_slots, m, v, *, tr, name, own=None):
    R, C = w.shape
    n_slot = g_slots.shape[0]
    tr = min(tr, R)
    assert R % tr == 0, (name, R, tr)
    c1 = 1.0 / (1.0 - ADAM_B1 ** ADAM_STEP)
    c2 = 1.0 / (1.0 - ADAM_B2 ** ADAM_STEP)
    n_own = 0 if own is None else 1

    def body(me_ref, w_ref, g_ref, *refs):
        m_ref, v_ref, go_ref, d_ref, mo_ref, vo_ref = refs[n_own:]
        slot = lambda d: (jnp.where(me_ref[0] == d, refs[0][...], g_ref[d]) if n_own else g_ref[d]).astype(F32)
        g = slot(0)
        for d in range(1, n_slot):
            g = g + slot(d)
        mn = ADAM_B1 * m_ref[...] + (1.0 - ADAM_B1) * g
        vn = ADAM_B2 * v_ref[...] + (1.0 - ADAM_B2) * (g * g)
        go_ref[...] = g
        mo_ref[...] = mn
        vo_ref[...] = vn
        d_ref[...] = -ADAM_LR * ((mn * c1) / (jnp.sqrt(vn * c2) + ADAM_EPS) + ADAM_WD * w_ref[...])

    me = 4 * lax.axis_index("x") + 2 * lax.axis_index("y") + lax.axis_index("c")
    blk = pl.BlockSpec((tr, C), lambda i, me_ref: (i, 0))
    own_specs = [pl.BlockSpec((None, tr, C), lambda i, me_ref: (me_ref[0], i, 0))] * n_own
    return pl.pallas_call(
        body, name=name, out_shape=tuple(jax.ShapeDtypeStruct((R, C), F32) for _ in range(4)),
        grid_spec=pltpu.PrefetchScalarGridSpec(
            num_scalar_prefetch=1, grid=(R // tr,),
            in_specs=[blk, pl.BlockSpec((n_slot, tr, C), lambda i, me_ref: (0, i, 0))] + own_specs + [blk, blk],
            out_specs=(blk, blk, blk, blk)),
        compiler_params=_cparams(1, big=True),
    )(jnp.reshape(me, (1,)).astype(jnp.int32), w, g_slots, *([own] if n_own else []), m, v)


def _adamw_many(ws, g_slots, g_owns, ms, vs, *, name):
    n = len(ws)
    c1 = 1.0 / (1.0 - ADAM_B1 ** ADAM_STEP)
    c2 = 1.0 / (1.0 - ADAM_B2 ** ADAM_STEP)

    def body(*refs):
        w_refs, g_refs, o_refs = refs[:n], refs[n:2 * n], refs[2 * n:3 * n]
        m_refs, v_refs = refs[3 * n:4 * n], refs[4 * n:5 * n]
        outs = refs[5 * n:]
        me = 4 * lax.axis_index("x") + 2 * lax.axis_index("y") + lax.axis_index("c")
        for i in range(n):
            own = o_refs[i][...]
            g = jnp.where(me == 0, own, g_refs[i][0])
            for d in range(1, N_DEV):
                g = g + jnp.where(me == d, own, g_refs[i][d])
            mn = ADAM_B1 * m_refs[i][...] + (1.0 - ADAM_B1) * g
            vn = ADAM_B2 * v_refs[i][...] + (1.0 - ADAM_B2) * (g * g)
            outs[i][...] = g
            outs[n + i][...] = -ADAM_LR * ((mn * c1) / (jnp.sqrt(vn * c2) + ADAM_EPS) + ADAM_WD * w_refs[i][...])
            outs[2 * n + i][...] = mn
            outs[3 * n + i][...] = vn

    res = pl.pallas_call(
        body, name=name, out_shape=tuple(jax.ShapeDtypeStruct(w.shape, F32) for _ in range(4) for w in ws),
        compiler_params=pltpu.CompilerParams(vmem_limit_bytes=VMEM_LIMIT),
    )(*ws, *g_slots, *g_owns, *ms, *vs)
    return res[:n], res[n:2 * n], res[2 * n:3 * n], res[3 * n:]


SMALL_NAMES = ("b_ada", "b_in", "b_conv", "w_rg_a", "b_rg_a", "w_rg_x", "b_rg_x", "lru_lambda", "w_sp", "b_sp",
               "ln_v_g", "ln_v_b", "ln1_g", "ln1_b", "ln2_g", "ln2_b")
WEIGHT_ORDER = ("w_ada", "b_ada", "w_in", "b_in", "w_conv", "b_conv", "w_rg_a", "b_rg_a", "w_rg_x", "b_rg_x",
                "lru_lambda", "w_sp", "b_sp", "ln_v_g", "ln_v_b", "w_o_lru", "w_o_sgu", "w_out", "ln1_g", "ln1_b",
                "w_up", "w_down", "ln2_g", "ln2_b")


def _blocked_cols(w2d):
    K, N = w2d.shape
    return jnp.transpose(w2d.reshape(K, N_DEV, N // N_DEV), (1, 0, 2))


def _unblock_cols(wb):
    n, K, nb = wb.shape
    return jnp.transpose(wb, (1, 0, 2)).reshape(K, n * nb)


def kernel(x, c, w_ada, b_ada, w_in, b_in, w_conv, b_conv, w_rg_a, b_rg_a, w_rg_x, b_rg_x, lru_lambda, w_sp, b_sp, ln_v_g, ln_v_b, w_o_lru, w_o_sgu, w_out, ln1_g, ln1_b, w_up, w_down, ln2_g, ln2_b, loss_target, m_w_ada, m_b_ada, m_w_in, m_b_in, m_w_conv, m_b_conv, m_w_rg_a, m_b_rg_a, m_w_rg_x, m_b_rg_x, m_lru_lambda, m_w_sp, m_b_sp, m_ln_v_g, m_ln_v_b, m_w_o_lru, m_w_o_sgu, m_w_out, m_ln1_g, m_ln1_b, m_w_up, m_w_down, m_ln2_g, m_ln2_b, v_w_ada, v_b_ada, v_w_in, v_b_in, v_w_conv, v_b_conv, v_w_rg_a, v_b_rg_a, v_w_rg_x, v_b_rg_x, v_lru_lambda, v_w_sp, v_b_sp, v_ln_v_g, v_ln_v_b, v_w_o_lru, v_w_o_sgu, v_w_out, v_ln1_g, v_ln1_b, v_w_up, v_w_down, v_ln2_g, v_ln2_b):
    W = dict(w_ada=w_ada, b_ada=b_ada, w_in=w_in, b_in=b_in, w_conv=w_conv, b_conv=b_conv, w_rg_a=w_rg_a,
             b_rg_a=b_rg_a, w_rg_x=w_rg_x, b_rg_x=b_rg_x, lru_lambda=lru_lambda, w_sp=w_sp, b_sp=b_sp,
             ln_v_g=ln_v_g, ln_v_b=ln_v_b, w_o_lru=w_o_lru, w_o_sgu=w_o_sgu, w_out=w_out, ln1_g=ln1_g, ln1_b=ln1_b,
             w_up=w_up, w_down=w_down, ln2_g=ln2_g, ln2_b=ln2_b)
    Mo = dict(w_ada=m_w_ada, b_ada=m_b_ada, w_in=m_w_in, b_in=m_b_in, w_conv=m_w_conv, b_conv=m_b_conv,
              w_rg_a=m_w_rg_a, b_rg_a=m_b_rg_a, w_rg_x=m_w_rg_x, b_rg_x=m_b_rg_x, lru_lambda=m_lru_lambda,
              w_sp=m_w_sp, b_sp=m_b_sp, ln_v_g=m_ln_v_g, ln_v_b=m_ln_v_b, w_o_lru=m_w_o_lru, w_o_sgu=m_w_o_sgu,
              w_out=m_w_out, ln1_g=m_ln1_g, ln1_b=m_ln1_b, w_up=m_w_up, w_down=m_w_down, ln2_g=m_ln2_g,
              ln2_b=m_ln2_b)
    Vo = dict(w_ada=v_w_ada, b_ada=v_b_ada, w_in=v_w_in, b_in=v_b_in, w_conv=v_w_conv, b_conv=v_b_conv,
              w_rg_a=v_w_rg_a, b_rg_a=v_b_rg_a, w_rg_x=v_w_rg_x, b_rg_x=v_b_rg_x, lru_lambda=v_lru_lambda,
              w_sp=v_w_sp, b_sp=v_b_sp, ln_v_g=v_ln_v_g, ln_v_b=v_ln_v_b, w_o_lru=v_w_o_lru, w_o_sgu=v_w_o_sgu,
              w_out=v_w_out, ln1_g=v_ln1_g, ln1_b=v_ln1_b, w_up=v_w_up, w_down=v_w_down, ln2_g=v_ln2_g,
              ln2_b=v_ln2_b)

    Bl, S, D = x.shape
    T = Bl * S
    lw = b_conv.shape[-1]
    sw = ln_v_g.shape[-1]
    din = b_in.shape[-1]
    dff = w_up.shape[-1] * N_DEV
    ts = min(2048, S)
    tmix = min(256, S)
    trow = min(512, S)

    c_pad = jnp.pad(c, ((0, SUBLANES - Bl), (0, 0)))
    c_g, wconv_g = _exchange([c_pad, w_conv[0]], True, "xchg_c")
    wconv_full = _unblock_cols(wconv_g)
    c_act, modcols = _ada_fwd(c_g.reshape(N_DEV * SUBLANES, D), w_ada[0])
    (mod_slots,) = _exchange([modcols.reshape(N_DEV, SUBLANES, -1)], False, "xchg_mod")

    nbw = din // N_DEV // WIN_PARTS
    wnames = tuple("win%d" % q for q in range(WIN_PARTS)) + ("wol", "wos", "wout", "wup", "wdown")
    shards = [w_in[0][:, q * nbw:(q + 1) * nbw].astype(BF16) for q in range(WIN_PARTS)] + [
        w_o_lru[0].astype(BF16), w_o_sgu[0].astype(BF16), w_out[0].astype(BF16), w_up[0].astype(BF16),
        w_down[0].astype(BF16)]
    col_sharded = [True] * WIN_PARTS + [False, True, False, True, False]
    g_send, g_recv, g_src, g_land, g_tok = _xstart(shards, True, mod_slots, "gather_start", cols=col_sharded)
    gidx = {n: i for i, n in enumerate(wnames)}

    def gathered(n, after):
        i = gidx[n]
        return _xwait(g_src[i], g_land[i], g_send[i], g_recv[i], after, True, "gather_wait_" + n, col=col_sharded[i])

    mod = _unblock_cols(mod_slots)[:Bl] + (b_ada + g_tok[0, 0])
    sh1, sc1, gt1, sh2, sc2, gt2 = [mod[:, i * D:(i + 1) * D].reshape(Bl, 1, D) for i in range(6)]

    wa_b, wx_b = w_rg_a[0].astype(BF16), w_rg_x[0].astype(BF16)
    b_sp_t = jnp.transpose(b_sp[0])
    small_mix = (wconv_full, b_conv, wa_b, b_rg_a, wx_b, b_rg_x, lru_lambda, w_sp[0], b_sp_t, ln_v_g, ln_v_b)

    h = _modulate(x, sc1, sh1, ts)
    proj, win_parts = None, []
    for q in range(WIN_PARTS):
        wq = gathered("win%d" % q, h if q == 0 else proj)
        win_parts.append(wq)
        proj = _mm(h.reshape(T, D), wq, mode="nn", tm=8192, tn=nbw, tk=D, outs=[BF16], extras=[(b_in, "row")],
                   epilogue=lambda acc, ex: (acc + ex[0],), scatter=(WIN_PARTS, q, din), into=proj,
                   name="mm_proj%d" % q)
    proj3 = proj.reshape(Bl, S, din)
    hs, ya_pre, ysgu, *lru_saved = _mix_fwd(proj3, *small_mix, tm=tmix, lw=lw, sw=sw)
    Wol = gathered("wol", ya_pre).reshape(lw, D)
    Wos = gathered("wos", ysgu)
    x2d, tgt2d = x.reshape(T, D), loss_target.reshape(T, D)
    gate_cb = (din - 2 * D) // D

    def ep_merge(y_b, v):
        ya = v[0].astype(BF16).astype(F32)
        yb = y_b.astype(BF16).astype(F32)
        ga, gb = v[1].astype(F32), v[2].astype(F32)
        return [ya, yb, _sigmoid(ga) * ya + _sigmoid(gb) * yb]

    y_a, y_b, merged = _mm_rows(ysgu.reshape(T, sw), Wos, mode="nn", tm=trow, seq=S, second=(ya_pre.reshape(T, lw), Wol),
                                ins=[("tilecol", proj, D, gate_cb), ("tilecol", proj, D, gate_cb + 1)],
                                outs=[("tile", BF16, D), ("tile", BF16, D), ("tile", BF16, D)], epilogue=ep_merge,
                                name="mm_yab_merge")
    Wout = gathered("wout", merged).reshape(D, D)

    def ep_ln1(mix_acc, v):
        x_, gt, g, b, sc, sh = v
        mixr = mix_acc.astype(BF16).astype(F32)
        xhat, rstd = _ln_stats(ALPHA * x_ + (1.0 + gt) * mixr)
        x1_ = xhat * g + b
        return [mixr, x1_, x1_ * (1.0 + sc) + sh, xhat, jnp.broadcast_to(rstd, (rstd.shape[0], LANES))]

    mix, x1, h2, xhat1, rstd1 = _mm_rows(
        merged, Wout, mode="nn", tm=trow, seq=S,
        ins=[("tile", x2d), ("brow", gt1), ("row", ln1_g), ("row", ln1_b), ("brow", sc2), ("brow", sh2)],
        outs=[("tile", BF16, D), ("tile", F32, D), ("tile", BF16, D), ("tile", BF16, D), ("tile", F32, LANES)],
        epilogue=ep_ln1, name="mm_mix_ln1")
    Wup = gathered("wup", h2)
    relu_up = _mm(h2, Wup, mode="nn", tm=2048, tn=1024, tk=D, outs=[BF16],
                  epilogue=lambda acc, ex: (jnp.maximum(acc, 0.0),), name="mm_up")
    square = lambda t: t * t
    Wdown = gathered("wdown", relu_up).reshape(dff, D)

    def ep_ln2(f_acc, v):
        x1_, t_, gt, g, b = v
        xhat, rstd = _ln_stats(ALPHA * x1_ + (1.0 + gt) * f_acc)
        err = xhat * g + b - t_
        loss_t = 0.5 * jnp.sum(jnp.mean(err * err, axis=-1, keepdims=True))
        dy = err * (1.0 / D)
        dz = _ln_bwd(dy, xhat, rstd, g)
        return [dz * (2.0 * (1.0 + gt)), ALPHA * dz, _colsum(dz * f_acc), _colsum(dy * xhat), _colsum(dy), loss_t]

    df2x, dx1p, dgt2, dg2, db2, loss_part = _mm_rows(
        relu_up, Wdown, mode="nn", tm=trow, seq=S, a_fn=square,
        ins=[("tile", x1), ("tile", tgt2d), ("brow", gt2), ("row", ln2_g), ("row", ln2_b)],
        outs=[("tile", BF16, D), ("tile", F32, D), ("acc_brow", D), ("acc_row", D), ("acc_row", D), ("acc_scalar",)],
        epilogue=ep_ln2, name="mm_down_ln2")
    loss = lax.psum(loss_part[0, 0], ("x", "y", "c"))

    def send_grads(parts, name):
        snd, rcv, src, land, tok = _xstart(parts, False, None, name + "_start")
        return [(src[i], land[i], snd[i], rcv[i]) for i in range(len(parts))], tok

    dup = _mm(df2x, Wdown, mode="nt", tm=2048, tn=1024, tk=D, outs=[BF16], extras=[(relu_up, "tile")],
              epilogue=lambda acc, ex: (acc * ex[0].astype(F32),), name="mm_dup")
    g_wdown = _mm(relu_up, df2x, mode="tn", tm=1024, tn=D, tk=4096, outs=[BF16], a_fn=square,
                  epilogue=lambda acc, ex: (0.5 * acc,), name="mm_gwdown")
    (x_wdown,), tok = send_grads([g_wdown.reshape(N_DEV, dff // N_DEV, D)], "gx_wdown")
    def ep_ln1_bwd(dh2, v):
        dx1p_, x1_, xh_, rs_, mix_, sc, gt, g = v
        mixv = mix_.astype(F32)
        dx1 = dx1p_ + dh2 * (1.0 + sc)
        xhat, rstd = xh_.astype(F32), rs_[:, 0:1]
        dz = _ln_bwd(dx1, xhat, rstd, g)
        return [ALPHA * dz, dz * (1.0 + gt), _colsum(dh2 * x1_), _colsum(dh2), _colsum(dz * mixv),
                _colsum(dx1 * xhat), _colsum(dx1)]

    dxp, dmix, dsc2, dsh2, dgt1, dg1, db1 = _mm_rows(
        dup, Wup, mode="nt", tm=trow, seq=S, tok=tok,
        ins=[("tile", dx1p), ("tile", x1), ("tile", xhat1), ("tile", rstd1), ("tile", mix), ("brow", sc2), ("brow", gt1),
             ("row", ln1_g)],
        outs=[("tile", F32, D), ("tile", BF16, D), ("acc_brow", D), ("acc_brow", D), ("acc_brow", D), ("acc_row", D),
              ("acc_row", D)],
        epilogue=ep_ln1_bwd, name="mm_dh2_ln1b")
    g_wup = _mm(h2, dup, mode="tn", tm=D, tn=1024, tk=4096, outs=[BF16], nb=dff // N_DEV, name="mm_gwup")
    (x_wup,), tok = send_grads([g_wup], "gx_wup")

    def ep_merge_bwd(dm, v):
        ya, yb, ga, gb = [t.astype(F32) for t in v]
        sa, sb = _sigmoid(ga), _sigmoid(gb)
        dga, dgb = dm * ya * sa * (1.0 - sa), dm * yb * sb * (1.0 - sb)
        return [dm * sa, dm * sb, (dga, dgb), jnp.concatenate([_colsum(dga), _colsum(dgb)], axis=1)]

    dy_a, dy_b, dproj, dbin_hi = _mm_rows(
        dmix, Wout, mode="nt", tm=trow, seq=S, tok=tok,
        ins=[("tile", y_a), ("tile", y_b), ("tilecol", proj, D, gate_cb), ("tilecol", proj, D, gate_cb + 1)],
        outs=[("tile", BF16, D), ("tile", BF16, D), ("tilecol", BF16, 2 * D, gate_cb // 2, din), ("acc_row", 2 * D)],
        epilogue=ep_merge_bwd, name="mm_dmerged_mb")
    g_wout = _mm(merged, dmix, mode="tn", tm=D, tn=D, tk=4096, outs=[BF16], name="mm_gwout")
    (x_wout,), tok = send_grads([g_wout.reshape(N_DEV, D // N_DEV, D)], "gx_wout")
    dya_pre, dysgu = _mm_rows(dy_a, Wol, mode="nt", tm=trow, seq=S, tok=tok, second=(dy_b, Wos), ins=[],
                              outs=[("tile", BF16, lw), ("tile", BF16, sw)], epilogue=lambda pa_, v: [pa_, v[0]],
                              name="mm_dya_dys")
    g_wol = _mm(ya_pre.reshape(T, lw), dy_a, mode="tn", tm=lw, tn=D, tk=2048, outs=[BF16], name="mm_gwol")
    g_wos = _mm(ysgu.reshape(T, sw), dy_b, mode="tn", tm=sw, tn=D, tk=2048, outs=[BF16], nb=D // N_DEV,
                name="mm_gwos")
    (x_wol, x_wos), tok = send_grads([g_wol.reshape(N_DEV, lw // N_DEV, D), g_wos], "gx_wo")
    small_mix_b = (wconv_full, b_conv + tok[0, 0]) + small_mix[2:]
    (dproj, dbin_lo, g_wconv, g_bconv, g_wa, g_ba, g_wx, g_bx, g_lam, g_wsp, g_bsp_t, g_lvg, g_lvb) = _mix_bwd(
        proj3, hs, dya_pre.reshape(Bl, S, lw), dysgu.reshape(Bl, S, sw), dproj.reshape(Bl, S, din), lru_saved,
        *small_mix_b, tm=tmix, lw=lw, sw=sw)
    dproj2 = dproj.reshape(T, din)
    small_names = [n for n in SMALL_NAMES if n != "b_ada"]
    small_g = dict(b_in=jnp.concatenate([dbin_lo, dbin_hi], axis=-1), b_conv=g_bconv, w_rg_a=g_wa[None], b_rg_a=g_ba,
                   w_rg_x=g_wx[None], b_rg_x=g_bx, lru_lambda=g_lam, w_sp=g_wsp[None],
                   b_sp=jnp.transpose(g_bsp_t)[None], ln_v_g=g_lvg, ln_v_b=g_lvb, ln1_g=dg1, ln1_b=db1, ln2_g=dg2,
                   ln2_b=db2)
    gs_snd, gs_rcv, gs_src, gs_land, tok_s = _xstart([small_g[n] for n in small_names], True, None, "gsmall_start")
    g_win = _mm(h.reshape(T, D), dproj2, mode="tn", tm=D, tn=din // 4, tk=2048, outs=[BF16], nb=din // N_DEV,
                tok=tok_s, name="mm_gwin")
    (x_win,), tok = send_grads([g_win], "gx_win")

    def ep_final(dh, v):
        dxp_, x_, sc = v
        return [dxp_ + dh * (1.0 + sc), _colsum(dh * x_), _colsum(dh)]

    grad_x, dsc1, dsh1 = _mm_rows(dproj2, win_parts, mode="nt", tm=trow, seq=S, tok=tok,
                                  ins=[("tile", dxp), ("tile", x2d), ("brow", sc1)],
                                  outs=[("tile", F32, D), ("acc_brow", D), ("acc_brow", D)], epilogue=ep_final,
                                  name="mm_dh_final")
    grad_x = grad_x.reshape(Bl, S, D)

    out_g, out_d, out_m, out_v = {}, {}, {}, {}

    def adam(name, g_slots, tr, own=None):
        shp = W[name].shape
        w2, m2, v2 = [t.reshape(g_slots.shape[1:]) for t in (W[name], Mo[name], Vo[name])]
        g, d, mn, vn = _adamw(w2, g_slots, m2, v2, tr=tr, name="adam_" + name, own=own)
        out_g[name], out_d[name], out_m[name], out_v[name] = [t.reshape(shp) for t in (g, d, mn, vn)]

    def adam_exchanged(name, handle, tr, after):
        own, slots = _xwait(*handle, after, False, "gx_%s_wait" % name, place=False)
        adam(name, slots, tr, own=own)

    adam_exchanged("w_down", x_wdown, 256, dsh1)
    adam_exchanged("w_up", x_wup, 512, dsh1)
    adam_exchanged("w_out", x_wout, 128, dsh1)
    adam_exchanged("w_o_lru", x_wol, 160, dsh1)
    adam_exchanged("w_o_sgu", x_wos, 256, dsh1)
    gs_own, gs_slots = _xwait_many(gs_src, gs_land, gs_snd, gs_rcv, dsh1, "gsmall_wait")
    res_small = _adamw_many([W[n] for n in small_names], gs_slots, gs_own, [Mo[n] for n in small_names],
                            [Vo[n] for n in small_names], name="adam_small")
    for dst, vals in zip((out_g, out_d, out_m, out_v), res_small):
        dst.update(dict(zip(small_names, vals)))

    dmod = jnp.concatenate([dsh1, dsc1, dgt1, dsh2, dsc2, dgt2], axis=-1).reshape(Bl, 6 * D)
    dmod_b = _blocked_cols(jnp.pad(dmod, ((0, SUBLANES - Bl), (0, 0))))
    dmod_s, gwconv_s = _exchange([dmod_b, _blocked_cols(g_wconv)], False, "xchg_dmod", after=out_g["ln2_b"])
    g_wada, g_bada_mine = _ada_bwd(c_act, dmod_s.reshape(N_DEV * SUBLANES, -1))
    (g_bada_all,) = _exchange([g_bada_mine], True, "xchg_bada")
    adam("w_ada", g_wada[None], 512)
    adam("b_ada", g_bada_all.reshape(1, 1, 6 * D), 1)
    adam("w_conv", gwconv_s, 8)
    adam_exchanged("w_in", x_win, 512, g_bada_all)

    return (loss, grad_x, *[out_g[n] for n in WEIGHT_ORDER], *[out_d[n] for n in WEIGHT_ORDER],
            *[out_m[n] for n in WEIGHT_ORDER], *[out_v[n] for n in WEIGHT_ORDER])
```
